```python
import jax, jax.numpy as jnp
from jax import lax
import numpy as np

D_MODEL = 1024
BATCH = 32
SEQ = 2048
DEPTH = 2

D_CONV = D_MODEL
CONV_GROUPS = 16
CONV_WIDTH = 3
D_SGU = D_MODEL
SGU_GROUPS = 8
SGU_GROUP_DIM = D_SGU // SGU_GROUPS
CHUNK = 128
D_FF = 2816
N_BRANCH = 2
EPS = 1e-6

IN_COLS = 3 * D_CONV + 2 * D_SGU + N_BRANCH * D_MODEL

kernel_name = "hybrid_shortconv_sgu_gated_merge"


def rmsnorm(x, g):
    xf = x.astype(jnp.float32)
    y = xf * lax.rsqrt(jnp.mean(xf * xf, axis=-1, keepdims=True) + EPS)
    return (y * g.astype(jnp.float32)).astype(x.dtype)


def layernorm(x, g, b):
    xf = x.astype(jnp.float32)
    mu = jnp.mean(xf, axis=-1, keepdims=True)
    xc = xf - mu
    var = jnp.mean(xc * xc, axis=-1, keepdims=True)
    y = xc * lax.rsqrt(var + EPS)
    return (y * g.astype(jnp.float32) + b.astype(jnp.float32)).astype(x.dtype)


def causal_dwconv3(x, w):
    s = x.shape[1]
    xp = jnp.pad(x, ((0, 0), (CONV_WIDTH - 1, 0), (0, 0)))
    return xp[:, :s] * w[0] + xp[:, 1:s + 1] * w[1] + xp[:, 2:s + 2] * w[2]


def short_conv_mixer(b_gate, c_gate, xin, conv_w):
    return b_gate * causal_dwconv3(c_gate * xin, conv_w)


def spatial_gating_mixer(u, v, ln_g, ln_b, w_s, b_s):
    bsz, s, _ = v.shape
    n_chunks = s // CHUNK
    vn = layernorm(v, ln_g, ln_b).reshape(bsz, n_chunks, CHUNK, SGU_GROUPS, SGU_GROUP_DIM)
    mask = jnp.tril(jnp.ones((CHUNK, CHUNK), dtype=bool))
    w = jnp.where(mask[None], w_s, jnp.zeros((), w_s.dtype))
    mixed = jnp.einsum('gts,bnsgc->bntgc', w, vn)
    mixed = mixed + jnp.swapaxes(b_s, 0, 1)[None, None, :, :, None]
    return u * mixed.reshape(bsz, s, D_SGU)


def conv_gated_mlp(h, w_up, conv_w, w_down):
    up = causal_dwconv3(h @ w_up, conv_w)
    gate, val = jnp.split(up, 2, axis=-1)
    return (jax.nn.silu(gate) * val) @ w_down


def _fwd_setup_inputs(seed: int = 0) -> dict:
    key = jax.random.key(seed)
    ks = jax.random.split(key, 16)
    f32 = jnp.float32

    def nrm(k, shape, scale):
        return jax.random.normal(k, shape, f32) * scale

    x = jax.random.normal(ks[0], (BATCH, SEQ, D_MODEL), f32)
    mix_norm_g = 1.0 + nrm(ks[1], (DEPTH, D_MODEL), 0.02)
    w_in = nrm(ks[2], (DEPTH, D_MODEL, IN_COLS), D_MODEL ** -0.5)
    conv_a_w = nrm(ks[3], (DEPTH, CONV_WIDTH, D_CONV), CONV_WIDTH ** -0.5)
    ln_v_g = 1.0 + nrm(ks[4], (DEPTH, D_SGU), 0.02)
    ln_v_b = nrm(ks[5], (DEPTH, D_SGU), 0.02)
    w_s = nrm(ks[6], (DEPTH, SGU_GROUPS, CHUNK, CHUNK), CHUNK ** -0.5)
    b_s = 1.0 + nrm(ks[7], (DEPTH, SGU_GROUPS, CHUNK), 0.02)
    w_out = nrm(ks[8], (DEPTH, D_MODEL, D_MODEL), D_MODEL ** -0.5)
    ffn_norm_g = 1.0 + nrm(ks[9], (DEPTH, D_MODEL), 0.02)
    w_up = nrm(ks[10], (DEPTH, D_MODEL, 2 * D_FF), D_MODEL ** -0.5)
    conv_ffn_w = nrm(ks[11], (DEPTH, CONV_WIDTH, 2 * D_FF), CONV_WIDTH ** -0.5)
    w_down = nrm(ks[12], (DEPTH, D_FF, D_MODEL), D_FF ** -0.5)
    final_norm_g = 1.0 + nrm(ks[13], (D_MODEL,), 0.02)
    return {"x": x, "mix_norm_g": mix_norm_g, "w_in": w_in, "conv_a_w": conv_a_w,
            "ln_v_g": ln_v_g, "ln_v_b": ln_v_b, "w_s": w_s, "b_s": b_s,
            "w_out": w_out, "ffn_norm_g": ffn_norm_g, "w_up": w_up,
            "conv_ffn_w": conv_ffn_w, "w_down": w_down, "final_norm_g": final_norm_g}


def _fwd_reference(x, mix_norm_g, w_in, conv_a_w, ln_v_g, ln_v_b, w_s, b_s, w_out,
              ffn_norm_g, w_up, conv_ffn_w, w_down, final_norm_g):
    split_pts = [D_CONV, 2 * D_CONV, 3 * D_CONV, 3 * D_CONV + D_SGU,
                 3 * D_CONV + 2 * D_SGU, 3 * D_CONV + 2 * D_SGU + D_MODEL]
    for l in range(DEPTH):
        h = rmsnorm(x, mix_norm_g[l])
        proj = h @ w_in[l]
        b_gate, c_gate, xin, u, v, g_a, g_b = jnp.split(proj, split_pts, axis=-1)
        y_a = short_conv_mixer(b_gate, c_gate, xin, conv_a_w[l])
        y_b = spatial_gating_mixer(u, v, ln_v_g[l], ln_v_b[l], w_s[l], b_s[l])
        merged = jax.nn.sigmoid(g_a) * y_a + jax.nn.sigmoid(g_b) * y_b
        x = x + merged @ w_out[l]
        h = rmsnorm(x, ffn_norm_g[l])
        x = x + conv_gated_mlp(h, w_up[l], conv_ffn_w[l], w_down[l])
    return rmsnorm(x, final_norm_g)


import jax as _jax
import jax.numpy as _jnp

TWIN_FORMAT = 'train_step'
FWD_PARAMS = ['x', 'mix_norm_g', 'w_in', 'conv_a_w', 'ln_v_g', 'ln_v_b', 'w_s', 'b_s', 'w_out', 'ffn_norm_g', 'w_up', 'conv_ffn_w', 'w_down', 'final_norm_g']
TWIN_WEIGHTS = ['mix_norm_g', 'w_in', 'conv_a_w', 'ln_v_g', 'ln_v_b', 'w_s', 'b_s', 'w_out', 'ffn_norm_g', 'w_up', 'conv_ffn_w', 'w_down', 'final_norm_g']
TWIN_DIFF_INPUT = 'x'
TWIN_INPUTS = ['x', 'mix_norm_g', 'w_in', 'conv_a_w', 'ln_v_g', 'ln_v_b', 'w_s', 'b_s', 'w_out', 'ffn_norm_g', 'w_up', 'conv_ffn_w', 'w_down', 'final_norm_g', 'loss_target', 'm_mix_norm_g', 'm_w_in', 'm_conv_a_w', 'm_ln_v_g', 'm_ln_v_b', 'm_w_s', 'm_b_s', 'm_w_out', 'm_ffn_norm_g', 'm_w_up', 'm_conv_ffn_w', 'm_w_down', 'm_final_norm_g', 'v_mix_norm_g', 'v_w_in', 'v_conv_a_w', 'v_ln_v_g', 'v_ln_v_b', 'v_w_s', 'v_b_s', 'v_w_out', 'v_ffn_norm_g', 'v_w_up', 'v_conv_ffn_w', 'v_w_down', 'v_final_norm_g']
TWIN_OUTPUTS = ['loss', 'grad_x', 'grad_mix_norm_g', 'grad_w_in', 'grad_conv_a_w', 'grad_ln_v_g', 'grad_ln_v_b', 'grad_w_s', 'grad_b_s', 'grad_w_out', 'grad_ffn_norm_g', 'grad_w_up', 'grad_conv_ffn_w', 'grad_w_down', 'grad_final_norm_g', 'delta_mix_norm_g', 'delta_w_in', 'delta_conv_a_w', 'delta_ln_v_g', 'delta_ln_v_b', 'delta_w_s', 'delta_b_s', 'delta_w_out', 'delta_ffn_norm_g', 'delta_w_up', 'delta_conv_ffn_w', 'delta_w_down', 'delta_final_norm_g', 'new_m_mix_norm_g', 'new_m_w_in', 'new_m_conv_a_w', 'new_m_ln_v_g', 'new_m_ln_v_b', 'new_m_w_s', 'new_m_b_s', 'new_m_w_out', 'new_m_ffn_norm_g', 'new_m_w_up', 'new_m_conv_ffn_w', 'new_m_w_down', 'new_m_final_norm_g', 'new_v_mix_norm_g', 'new_v_w_in', 'new_v_conv_a_w', 'new_v_ln_v_g', 'new_v_ln_v_b', 'new_v_w_s', 'new_v_b_s', 'new_v_w_out', 'new_v_ffn_norm_g', 'new_v_w_up', 'new_v_conv_ffn_w', 'new_v_w_down', 'new_v_final_norm_g']
TWIN_LEAF_KINDS = {'loss': 'loss', 'grad_x': 'grad_x', 'grad_mix_norm_g': 'grad_w', 'grad_w_in': 'grad_w', 'grad_conv_a_w': 'grad_w', 'grad_ln_v_g': 'grad_w', 'grad_ln_v_b': 'grad_w', 'grad_w_s': 'grad_w', 'grad_b_s': 'grad_w', 'grad_w_out': 'grad_w', 'grad_ffn_norm_g': 'grad_w', 'grad_w_up': 'grad_w', 'grad_conv_ffn_w': 'grad_w', 'grad_w_down': 'grad_w', 'grad_final_norm_g': 'grad_w', 'delta_mix_norm_g': 'delta_w', 'delta_w_in': 'delta_w', 'delta_conv_a_w': 'delta_w', 'delta_ln_v_g': 'delta_w', 'delta_ln_v_b': 'delta_w', 'delta_w_s': 'delta_w', 'delta_b_s': 'delta_w', 'delta_w_out': 'delta_w', 'delta_ffn_norm_g': 'delta_w', 'delta_w_up': 'delta_w', 'delta_conv_ffn_w': 'delta_w', 'delta_w_down': 'delta_w', 'delta_final_norm_g': 'delta_w', 'new_m_mix_norm_g': 'new_m', 'new_m_w_in': 'new_m', 'new_m_conv_a_w': 'new_m', 'new_m_ln_v_g': 'new_m', 'new_m_ln_v_b': 'new_m', 'new_m_w_s': 'new_m', 'new_m_b_s': 'new_m', 'new_m_w_out': 'new_m', 'new_m_ffn_norm_g': 'new_m', 'new_m_w_up': 'new_m', 'new_m_conv_ffn_w': 'new_m', 'new_m_w_down': 'new_m', 'new_m_final_norm_g': 'new_m', 'new_v_mix_norm_g': 'new_v', 'new_v_w_in': 'new_v', 'new_v_conv_a_w': 'new_v', 'new_v_ln_v_g': 'new_v', 'new_v_ln_v_b': 'new_v', 'new_v_w_s': 'new_v', 'new_v_b_s': 'new_v', 'new_v_w_out': 'new_v', 'new_v_ffn_norm_g': 'new_v', 'new_v_w_up': 'new_v', 'new_v_conv_ffn_w': 'new_v', 'new_v_w_down': 'new_v', 'new_v_final_norm_g': 'new_v'}


def _forward(args):
    return _fwd_reference(*[args[k] for k in FWD_PARAMS])


def _output_shape():
    out = _jax.eval_shape(lambda: _forward(_fwd_setup_inputs(0)))
    return out.shape, out.dtype

N_MICROBATCH = 1
ADAM_LR = 0.001
ADAM_B1 = 0.9
ADAM_B2 = 0.999
ADAM_EPS = 1e-08
ADAM_WD = 0.01
ADAM_STEP = 10
PER_EXAMPLE_BATCH_AXIS = {'x': 0, 'loss_target': 0}
SHARED_INPUTS = []
_WEIGHT_DTYPES = {'mix_norm_g': _jnp.float32, 'w_in': _jnp.float32, 'conv_a_w': _jnp.float32, 'ln_v_g': _jnp.float32, 'ln_v_b': _jnp.float32, 'w_s': _jnp.float32, 'b_s': _jnp.float32, 'w_out': _jnp.float32, 'ffn_norm_g': _jnp.float32, 'w_up': _jnp.float32, 'conv_ffn_w': _jnp.float32, 'w_down': _jnp.float32, 'final_norm_g': _jnp.float32}
MOMENT_SCALE = {'mix_norm_g': 2.718432e-01, 'w_in': 9.916642e-02, 'conv_a_w': 1.151105e-01, 'ln_v_g': 7.860287e-02, 'ln_v_b': 8.134227e-02, 'w_s': 7.886590e-02, 'b_s': 1.143445e-01, 'w_out': 1.791832e-01, 'ffn_norm_g': 1.627712e-01, 'w_up': 6.773972e-02, 'conv_ffn_w': 6.704903e-02, 'w_down': 1.110233e-01, 'final_norm_g': 6.396012e+01}


def _to_microbatches(a, axis):
    t = _jnp.moveaxis(a, axis, 0)
    t = t.reshape((N_MICROBATCH, t.shape[0] // N_MICROBATCH) + t.shape[1:])
    return _jnp.moveaxis(t, 1, axis + 1)


def setup_inputs(seed: int = 0) -> dict:
    inp = _fwd_setup_inputs(seed)
    key = _jax.random.fold_in(_jax.random.key(seed), 7919)
    shape, _ = _output_shape()
    out = dict(inp)
    out["loss_target"] = _jax.random.normal(_jax.random.fold_in(key, 0), shape, _jnp.float32)
    for i, name in enumerate(TWIN_WEIGHTS):
        w = inp[name].astype(_jnp.float32)
        if MOMENT_SCALE is None:
            s = _jnp.sqrt(_jnp.mean(_jnp.square(w)) + 1e-30)
        else:
            s = MOMENT_SCALE[name]
        km, kv = _jax.random.split(_jax.random.fold_in(key, i + 1))
        out[name] = w
        out["m_" + name] = s * _jax.random.normal(km, w.shape, _jnp.float32)
        out["v_" + name] = (s * s) * _jax.random.uniform(kv, w.shape, _jnp.float32, 0.5, 1.5)
    if N_MICROBATCH > 1:
        for name, axis in PER_EXAMPLE_BATCH_AXIS.items():
            out[name] = _to_microbatches(out[name], axis)
    return {'x': out['x'], 'mix_norm_g': out['mix_norm_g'], 'w_in': out['w_in'], 'conv_a_w': out['conv_a_w'], 'ln_v_g': out['ln_v_g'], 'ln_v_b': out['ln_v_b'], 'w_s': out['w_s'], 'b_s': out['b_s'], 'w_out': out['w_out'], 'ffn_norm_g': out['ffn_norm_g'], 'w_up': out['w_up'], 'conv_ffn_w': out['conv_ffn_w'], 'w_down': out['w_down'], 'final_norm_g': out['final_norm_g'], 'loss_target': out['loss_target'], 'm_mix_norm_g': out['m_mix_norm_g'], 'm_w_in': out['m_w_in'], 'm_conv_a_w': out['m_conv_a_w'], 'm_ln_v_g': out['m_ln_v_g'], 'm_ln_v_b': out['m_ln_v_b'], 'm_w_s': out['m_w_s'], 'm_b_s': out['m_b_s'], 'm_w_out': out['m_w_out'], 'm_ffn_norm_g': out['m_ffn_norm_g'], 'm_w_up': out['m_w_up'], 'm_conv_ffn_w': out['m_conv_ffn_w'], 'm_w_down': out['m_w_down'], 'm_final_norm_g': out['m_final_norm_g'], 'v_mix_norm_g': out['v_mix_norm_g'], 'v_w_in': out['v_w_in'], 'v_conv_a_w': out['v_conv_a_w'], 'v_ln_v_g': out['v_ln_v_g'], 'v_ln_v_b': out['v_ln_v_b'], 'v_w_s': out['v_w_s'], 'v_b_s': out['v_b_s'], 'v_w_out': out['v_w_out'], 'v_ffn_norm_g': out['v_ffn_norm_g'], 'v_w_up': out['v_w_up'], 'v_conv_ffn_w': out['v_conv_ffn_w'], 'v_w_down': out['v_w_down'], 'v_final_norm_g': out['v_final_norm_g']}


def _loss(weights, diff, rest, loss_target):
    with _jax.named_scope("forward"):
        args = {**rest, TWIN_DIFF_INPUT: diff, **{k: w.astype(_WEIGHT_DTYPES[k]) for k, w in weights.items()}}
        y = _forward(args)
    with _jax.named_scope("loss_head"):
        err = _jnp.square(y.astype(_jnp.float32) - loss_target)
        return 0.5 * _jnp.sum(_jnp.mean(err, axis=-1)) if err.ndim else 0.5 * err


def _adamw(w, g, m, v):
    m = ADAM_B1 * m + (1.0 - ADAM_B1) * g
    v = ADAM_B2 * v + (1.0 - ADAM_B2) * _jnp.square(g)
    m_hat = m / (1.0 - ADAM_B1 ** ADAM_STEP)
    v_hat = v / (1.0 - ADAM_B2 ** ADAM_STEP)
    delta = -ADAM_LR * (m_hat / (_jnp.sqrt(v_hat) + ADAM_EPS) + ADAM_WD * w)
    return delta, m, v


def reference(x, mix_norm_g, w_in, conv_a_w, ln_v_g, ln_v_b, w_s, b_s, w_out, ffn_norm_g, w_up, conv_ffn_w, w_down, final_norm_g, loss_target, m_mix_norm_g, m_w_in, m_conv_a_w, m_ln_v_g, m_ln_v_b, m_w_s, m_b_s, m_w_out, m_ffn_norm_g, m_w_up, m_conv_ffn_w, m_w_down, m_final_norm_g, v_mix_norm_g, v_w_in, v_conv_a_w, v_ln_v_g, v_ln_v_b, v_w_s, v_b_s, v_w_out, v_ffn_norm_g, v_w_up, v_conv_ffn_w, v_w_down, v_final_norm_g):
    given = dict(x=x, mix_norm_g=mix_norm_g, w_in=w_in, conv_a_w=conv_a_w, ln_v_g=ln_v_g, ln_v_b=ln_v_b, w_s=w_s, b_s=b_s, w_out=w_out, ffn_norm_g=ffn_norm_g, w_up=w_up, conv_ffn_w=conv_ffn_w, w_down=w_down, final_norm_g=final_norm_g, loss_target=loss_target, m_mix_norm_g=m_mix_norm_g, m_w_in=m_w_in, m_conv_a_w=m_conv_a_w, m_ln_v_g=m_ln_v_g, m_ln_v_b=m_ln_v_b, m_w_s=m_w_s, m_b_s=m_b_s, m_w_out=m_w_out, m_ffn_norm_g=m_ffn_norm_g, m_w_up=m_w_up, m_conv_ffn_w=m_conv_ffn_w, m_w_down=m_w_down, m_final_norm_g=m_final_norm_g, v_mix_norm_g=v_mix_norm_g, v_w_in=v_w_in, v_conv_a_w=v_conv_a_w, v_ln_v_g=v_ln_v_g, v_ln_v_b=v_ln_v_b, v_w_s=v_w_s, v_b_s=v_b_s, v_w_out=v_w_out, v_ffn_norm_g=v_ffn_norm_g, v_w_up=v_w_up, v_conv_ffn_w=v_conv_ffn_w, v_w_down=v_w_down, v_final_norm_g=v_final_norm_g)
    weights = {n: given[n] for n in TWIN_WEIGHTS}
    shared = {n: given[n] for n in SHARED_INPUTS}
    per_example = {n: given[n] for n in ['x']}
    grad_fn = _jax.value_and_grad(_loss, argnums=(0, 1))

    def one_microbatch(ex, loss_target):
        ex = dict(ex)
        diff = ex.pop(TWIN_DIFF_INPUT)
        return grad_fn(weights, diff, {**shared, **ex}, loss_target)

    if N_MICROBATCH == 1:
        loss, (grad_w, grad_x) = one_microbatch(per_example, given["loss_target"])
    else:
        def body(carry, xs):
            loss_sum, grad_sum = carry
            l_k, (gw_k, gx_k) = one_microbatch(xs[0], xs[1])
            with _jax.named_scope("update"):
                return (loss_sum + l_k, _jax.tree.map(_jnp.add, grad_sum, gw_k)), gx_k

        init = (_jnp.zeros((), _jnp.float32), _jax.tree.map(_jnp.zeros_like, weights))
        (loss, grad_w), grad_x = _jax.lax.scan(body, init, (per_example, given["loss_target"]))
    with _jax.named_scope("update"):
        delta_w, new_m, new_v = {}, {}, {}
        for n in TWIN_WEIGHTS:
            delta_w[n], new_m[n], new_v[n] = _adamw(weights[n], grad_w[n], given["m_" + n], given["v_" + n])
    return (loss, grad_x, *[grad_w[n] for n in TWIN_WEIGHTS], *[delta_w[n] for n in TWIN_WEIGHTS],
            *[new_m[n] for n in TWIN_WEIGHTS], *[new_v[n] for n in TWIN_WEIGHTS])
```

```python
import functools

import jax
import jax.numpy as jnp
from jax import lax
from jax.experimental import pallas as pl
from jax.experimental.pallas import tpu as pltpu

F32 = jnp.float32
BF16 = jnp.bfloat16
EPS = 1e-6
CHUNK = 128
N_CHIP = 4
HALO = 8
N_PIECES = 7
VMEM_LIMIT_V7X = 56 * 1024 * 1024
MESH_T = pl.DeviceIdType.MESH

ADAM_LR, ADAM_B1, ADAM_B2, ADAM_EPS, ADAM_WD, ADAM_STEP = 0.001, 0.9, 0.999, 1e-08, 0.01, 10

NT_DIMS = (((1,), (1,)), ((), ()))
TN_DIMS = (((0,), (0,)), ((), ()))


def _params(*sem):
    return pltpu.CompilerParams(dimension_semantics=sem, vmem_limit_bytes=VMEM_LIMIT_V7X)


def _row_tile(seq, want):
    t = min(seq, want)
    assert seq % t == 0 and t % CHUNK == 0, (seq, want)
    return t


def _shift_prev(prev8, cur, k):
    ext = jnp.concatenate([prev8, cur], axis=0)
    return pltpu.roll(ext, k, 0)[HALO:]


def _shift_next(cur, next8, k):
    ext = jnp.concatenate([cur, next8], axis=0)
    n = ext.shape[0]
    return pltpu.roll(ext, n - k, 0)[:n - HALO]


def _rowsum(a):
    return jnp.sum(a, axis=0, keepdims=True)


def _lanemean(a):
    return jnp.mean(a, axis=-1, keepdims=True)


def _tril_weights(ws_ref, groups):
    r = lax.broadcasted_iota(jnp.int32, (CHUNK, CHUNK), 0)
    c = lax.broadcasted_iota(jnp.int32, (CHUNK, CHUNK), 1)
    tril = r >= c
    return tril, [jnp.where(tril, ws_ref[g], 0.0) for g in range(groups)]


def _norm_matmul(x, g, wg, layer, *, tm, name):
    n, d = x.shape
    _, n_chip, _, ck = wg.shape

    def body(x_ref, g_ref, w_ref, o_ref, h_ref):
        @pl.when(pl.program_id(1) == 0)
        def _():
            xf = x_ref[...]
            r = lax.rsqrt(_lanemean(xf * xf) + EPS)
            h_ref[...] = (xf * r * g_ref[...]).astype(BF16)

        o_ref[...] = jnp.dot(h_ref[...], w_ref[...], preferred_element_type=F32).astype(BF16)

    return pl.pallas_call(
        body, name=name, grid=(n // tm, n_chip),
        in_specs=[pl.BlockSpec((tm, d), lambda i, j: (i, 0)),
                  pl.BlockSpec((1, d), lambda i, j: (0, 0)),
                  pl.BlockSpec((None, None, d, ck), lambda i, j: (layer, j, 0, 0))],
        out_specs=[pl.BlockSpec((tm, ck), lambda i, j: (i, j)),
                   pl.BlockSpec((tm, d), lambda i, j: (i, 0))],
        out_shape=[jax.ShapeDtypeStruct((n, n_chip * ck), BF16), jax.ShapeDtypeStruct((n, d), BF16)],
        compiler_params=_params("parallel", "arbitrary"),
    )(x, g, wg)


def _halo_prev(tm):
    return lambda i: jnp.maximum(i * (tm // HALO) - 1, 0)


def _halo_next(tm, n):
    last = n // HALO - 1
    return lambda i: jnp.minimum((i + 1) * (tm // HALO), last)


def _mixer_fwd(proj, x, cw, lg, lb, ws, bmap, wout, *, seq, tm, name):
    n, d = x.shape
    groups = ws.shape[0]
    gd = d // groups
    prev = _halo_prev(tm)

    def body(proj_ref, pcg_ref, pxi_ref, x_ref, cw_ref, lg_ref, lb_ref, ws_ref, bmap_ref, wout_ref, o_ref,
             vn_s, mixed_s):
        seq_start = (pl.program_id(0) * tm) % seq == 0

        def piece(k):
            return proj_ref[:, k * d:(k + 1) * d].astype(F32)

        z = piece(1) * piece(2)
        zprev = jnp.where(seq_start, 0.0, pcg_ref[...].astype(F32) * pxi_ref[...].astype(F32))
        cz = (cw_ref[0:1, :] * _shift_prev(zprev, z, 2) + cw_ref[1:2, :] * _shift_prev(zprev, z, 1)
              + cw_ref[2:3, :] * z)
        ya = piece(0) * cz

        v = piece(4)
        xc = v - _lanemean(v)
        vn = xc * lax.rsqrt(_lanemean(xc * xc) + EPS) * lg_ref[...] + lb_ref[...]
        vn_s[...] = vn.astype(BF16)
        _, wsm = _tril_weights(ws_ref, groups)
        for ck in range(tm // CHUNK):
            rows = slice(ck * CHUNK, (ck + 1) * CHUNK)
            for g in range(groups):
                cols = slice(g * gd, (g + 1) * gd)
                mixed_s[rows, cols] = (jnp.dot(wsm[g].astype(BF16), vn_s[rows, cols], preferred_element_type=F32)
                                       + bmap_ref[:, cols])
        yb = piece(3) * mixed_s[...]
        merged = jax.nn.sigmoid(piece(5)) * ya + jax.nn.sigmoid(piece(6)) * yb
        o_ref[...] = x_ref[...] + jnp.dot(merged.astype(BF16), wout_ref[...], preferred_element_type=F32)

    full = lambda shape: pl.BlockSpec(shape, lambda i: (0,) * len(shape))
    return pl.pallas_call(
        body, name=name, grid=(n // tm,),
        in_specs=[pl.BlockSpec((tm, N_PIECES * d), lambda i: (i, 0)),
                  pl.BlockSpec((HALO, d), lambda i: (prev(i), 1)),
                  pl.BlockSpec((HALO, d), lambda i: (prev(i), 2)),
                  pl.BlockSpec((tm, d), lambda i: (i, 0)),
                  full((3, d)), full((1, d)), full((1, d)), full((groups, CHUNK, CHUNK)), full((CHUNK, d)),
                  full((d, d))],
        out_specs=pl.BlockSpec((tm, d), lambda i: (i, 0)),
        out_shape=jax.ShapeDtypeStruct((n, d), F32),
        scratch_shapes=[pltpu.VMEM((tm, d), BF16), pltpu.VMEM((tm, d), F32)],
        compiler_params=_params("parallel"),
    )(proj, proj, proj, x, cw, lg, lb, ws, bmap, wout)


def _ffn_fwd(up, x, cw, wd, *, seq, tm, cwid, name):
    n, d = x.shape
    ff = wd.shape[0]
    prev = _halo_prev(tm)

    def body(up_ref, pup_ref, x_ref, cw_ref, wd_ref, o_ref):
        seq_start = (pl.program_id(0) * tm) % seq == 0

        def conv(lo):
            cur = up_ref[:, lo:lo + cwid].astype(F32)
            pre = jnp.where(seq_start, 0.0, pup_ref[:, lo:lo + cwid].astype(F32))
            return (cw_ref[0:1, lo:lo + cwid] * _shift_prev(pre, cur, 2)
                    + cw_ref[1:2, lo:lo + cwid] * _shift_prev(pre, cur, 1) + cw_ref[2:3, lo:lo + cwid] * cur)

        acc = x_ref[...]
        for cj in range(ff // cwid):
            gate = conv(cj * cwid)
            val = conv(ff + cj * cwid)
            a = gate * jax.nn.sigmoid(gate) * val
            acc = acc + jnp.dot(a.astype(BF16), wd_ref[cj * cwid:(cj + 1) * cwid, :], preferred_element_type=F32)
        o_ref[...] = acc

    return pl.pallas_call(
        body, name=name, grid=(n // tm,),
        in_specs=[pl.BlockSpec((tm, 2 * ff), lambda i: (i, 0)),
                  pl.BlockSpec((HALO, 2 * ff), lambda i: (prev(i), 0)),
                  pl.BlockSpec((tm, d), lambda i: (i, 0)),
                  pl.BlockSpec((3, 2 * ff), lambda i: (0, 0)),
                  pl.BlockSpec((ff, d), lambda i: (0, 0))],
        out_specs=pl.BlockSpec((tm, d), lambda i: (i, 0)),
        out_shape=jax.ShapeDtypeStruct((n, d), F32),
        compiler_params=_params("parallel"),
    )(up, up, x, cw, wd)


def _final_loss(x, g, target, *, tm, name):
    n, d = x.shape

    def body(x_ref, g_ref, t_ref, dx_ref, loss_ref, dg_ref):
        @pl.when(pl.program_id(0) == 0)
        def _():
            loss_ref[...] = jnp.zeros_like(loss_ref)
            dg_ref[...] = jnp.zeros_like(dg_ref)

        xf = x_ref[...]
        r = lax.rsqrt(_lanemean(xf * xf) + EPS)
        xhat = xf * r
        diff = xhat * g_ref[...] - t_ref[...]
        loss_ref[...] += (0.5 / d) * _rowsum(jnp.sum(diff * diff, axis=-1, keepdims=True))
        dy = diff * (1.0 / d)
        dg_ref[...] += _rowsum(dy * xhat)
        dyh = dy * g_ref[...]
        dx_ref[...] = r * (dyh - xhat * _lanemean(dyh * xhat))

    return pl.pallas_call(
        body, name=name, grid=(n // tm,),
        in_specs=[pl.BlockSpec((tm, d), lambda i: (i, 0)), pl.BlockSpec((1, d), lambda i: (0, 0)),
                  pl.BlockSpec((tm, d), lambda i: (i, 0))],
        out_specs=[pl.BlockSpec((tm, d), lambda i: (i, 0)), pl.BlockSpec((HALO, CHUNK), lambda i: (0, 0)),
                   pl.BlockSpec((1, d), lambda i: (0, 0))],
        out_shape=[jax.ShapeDtypeStruct((n, d), F32), jax.ShapeDtypeStruct((HALO, CHUNK), F32),
                   jax.ShapeDtypeStruct((1, d), F32)],
        compiler_params=_params("arbitrary"),
    )(x, g, target)


def _alias_args(prev):
    if prev is None:
        return [], []
    return [prev], [pl.BlockSpec(memory_space=pl.ANY)]


def _ffn_bwd(dx, up, cw, wd, dwd_prev, layer, n_layers, *, seq, tm, cwid, name):
    n, d = dx.shape
    ff = wd.shape[0]
    prev, nxt = _halo_prev(tm), _halo_next(tm, n)

    def body(dx_ref, dxn_ref, up_ref, pup_ref, nup_ref, cw_ref, wd_ref, *rest):
        dup_ref, dwd_ref, dcw_ref = rest[-3:]
        i = pl.program_id(0)
        seq_start = (i * tm) % seq == 0
        keep_next = jnp.where(((i + 1) * tm) % seq == 0, 0.0, 1.0)

        @pl.when(i == 0)
        def _():
            dwd_ref[...] = jnp.zeros_like(dwd_ref)
            dcw_ref[...] = jnp.zeros_like(dcw_ref)

        dxe = jnp.concatenate([dx_ref[...], dxn_ref[...]], axis=0).astype(BF16)
        dxb = dxe[:tm]
        for cj in range(ff // cwid):
            rows = slice(cj * cwid, (cj + 1) * cwid)
            dae = lax.dot_general(dxe, wd_ref[rows, :], NT_DIMS, preferred_element_type=F32)
            da, dan = dae[:tm], dae[tm:]

            def recompute(lo):
                cols = slice(lo, lo + cwid)
                cur = up_ref[:, cols].astype(F32)
                pre = jnp.where(seq_start, 0.0, pup_ref[:, cols].astype(F32))
                w = [cw_ref[k:k + 1, cols] for k in range(3)]
                s2, s1 = _shift_prev(pre, cur, 2), _shift_prev(pre, cur, 1)
                conv = w[0] * s2 + w[1] * s1 + w[2] * cur
                nx, last = nup_ref[:, cols].astype(F32), cur[tm - HALO:]
                conv_n = w[0] * _shift_prev(last, nx, 2) + w[1] * _shift_prev(last, nx, 1) + w[2] * nx
                return cols, w, (s2, s1, cur), conv, conv_n

            g_cols, g_w, g_taps, gate, gate_n = recompute(cj * cwid)
            v_cols, v_w, v_taps, val, val_n = recompute(ff + cj * cwid)
            sg = jax.nn.sigmoid(gate)
            sl = gate * sg
            dwd_ref[rows, :] += lax.dot_general((sl * val).astype(BF16), dxb, TN_DIMS, preferred_element_type=F32)
            d_val = da * sl
            d_gate = da * val * sg * (1.0 + gate * (1.0 - sg))
            sgn = jax.nn.sigmoid(gate_n)
            d_val_n = dan * (gate_n * sgn) * keep_next
            d_gate_n = dan * val_n * sgn * (1.0 + gate_n * (1.0 - sgn)) * keep_next
            for cols, w, taps, dcur, dnext in ((g_cols, g_w, g_taps, d_gate, d_gate_n),
                                               (v_cols, v_w, v_taps, d_val, d_val_n)):
                for k in range(3):
                    dcw_ref[k:k + 1, cols] += _rowsum(dcur * taps[k])
                dup = w[2] * dcur + w[1] * _shift_next(dcur, dnext, 1) + w[0] * _shift_next(dcur, dnext, 2)
                dup_ref[:, cols] = dup.astype(BF16)

    extra, extra_specs = _alias_args(dwd_prev)
    n_in = 7
    return pl.pallas_call(
        body, name=name, grid=(n // tm,),
        in_specs=[pl.BlockSpec((tm, d), lambda i: (i, 0)),
                  pl.BlockSpec((HALO, d), lambda i: (nxt(i), 0)),
                  pl.BlockSpec((tm, 2 * ff), lambda i: (i, 0)),
                  pl.BlockSpec((HALO, 2 * ff), lambda i: (prev(i), 0)),
                  pl.BlockSpec((HALO, 2 * ff), lambda i: (nxt(i), 0)),
                  pl.BlockSpec((3, 2 * ff), lambda i: (0, 0)),
                  pl.BlockSpec((ff, d), lambda i: (0, 0))] + extra_specs,
        out_specs=[pl.BlockSpec((tm, 2 * ff), lambda i: (i, 0)),
                   pl.BlockSpec((None, ff, d), lambda i: (layer, 0, 0)),
                   pl.BlockSpec((3, 2 * ff), lambda i: (0, 0))],
        out_shape=[jax.ShapeDtypeStruct((n, 2 * ff), BF16), jax.ShapeDtypeStruct((n_layers, ff, d), F32),
                   jax.ShapeDtypeStruct((3, 2 * ff), F32)],
        input_output_aliases={n_in: 1} if extra else {},
        compiler_params=_params("arbitrary"),
    )(dx, dx, up, up, up, cw, wd, *extra)


def _matmul_bwd_x(dy, wg, x, g, dres, layer, *, tm, name):
    n, d = x.shape
    _, n_chip, _, ck = wg.shape

    def body(dy_ref, w_ref, x_ref, g_ref, dres_ref, dx_ref, dg_ref, acc):
        i, j = pl.program_id(0), pl.program_id(1)

        @pl.when(j == 0)
        def _():
            acc[...] = jnp.zeros_like(acc)

        acc[...] += lax.dot_general(dy_ref[...], w_ref[...], NT_DIMS, preferred_element_type=F32)

        @pl.when(j == n_chip - 1)
        def _():
            @pl.when(i == 0)
            def _():
                dg_ref[...] = jnp.zeros_like(dg_ref)

            dh = acc[...]
            xf = x_ref[...]
            r = lax.rsqrt(_lanemean(xf * xf) + EPS)
            xhat = xf * r
            dg_ref[...] += _rowsum(dh * xhat)
            dyh = dh * g_ref[...]
            dx_ref[...] = dres_ref[...] + r * (dyh - xhat * _lanemean(dyh * xhat))

    return pl.pallas_call(
        body, name=name, grid=(n // tm, n_chip),
        in_specs=[pl.BlockSpec((tm, ck), lambda i, j: (i, j)),
                  pl.BlockSpec((None, None, d, ck), lambda i, j: (layer, j, 0, 0)),
                  pl.BlockSpec((tm, d), lambda i, j: (i, 0)),
                  pl.BlockSpec((1, d), lambda i, j: (0, 0)),
                  pl.BlockSpec((tm, d), lambda i, j: (i, 0))],
        out_specs=[pl.BlockSpec((tm, d), lambda i, j: (i, 0)), pl.BlockSpec((1, d), lambda i, j: (0, 0))],
        out_shape=[jax.ShapeDtypeStruct((n, d), F32), jax.ShapeDtypeStruct((1, d), F32)],
        scratch_shapes=[pltpu.VMEM((tm, d), F32)],
        compiler_params=_params("arbitrary", "arbitrary"),
    )(dy, wg, x, g, dres)


def _matmul_bwd_w(h, dy, dw_prev, layer, n_layers, n_chip, *, tk, name):
    n, d = h.shape
    ck = dy.shape[1] // n_chip

    def body(h_ref, dy_ref, *rest):
        o_ref = rest[-1]

        @pl.when(pl.program_id(1) == 0)
        def _():
            o_ref[...] = jnp.zeros_like(o_ref)

        o_ref[...] += lax.dot_general(h_ref[...], dy_ref[...], TN_DIMS, preferred_element_type=F32)

    extra, extra_specs = _alias_args(dw_prev)
    return pl.pallas_call(
        body, name=name, grid=(n_chip, n // tk),
        in_specs=[pl.BlockSpec((tk, d), lambda j, k: (k, 0)),
                  pl.BlockSpec((tk, ck), lambda j, k: (k, j))] + extra_specs,
        out_specs=pl.BlockSpec((None, None, d, ck), lambda j, k: (layer, j, 0, 0)),
        out_shape=jax.ShapeDtypeStruct((n_layers, n_chip, d, ck), F32),
        input_output_aliases={2: 0} if extra else {},
        compiler_params=_params("parallel", "arbitrary"),
    )(h, dy, *extra)


def _mixer_bwd(dx, proj, cw, lg, lb, ws, bmap, wout, dwout_prev, layer, n_layers, *, seq, tm, name):
    n, d = dx.shape
    groups = ws.shape[0]
    gd = d // groups
    prev, nxt = _halo_prev(tm), _halo_next(tm, n)
    n_tiles = n // tm

    def body(proj_ref, pcg_ref, pxi_ref, nbg_ref, nga_ref, dx_ref, dxn_ref, cw_ref, lg_ref, lb_ref, ws_ref,
             bmap_ref, wout_ref, *rest):
        (dproj_ref, dwout_ref, dcw_ref, dlg_ref, dlb_ref, dws_ref, dbs_ref,
         vn_s, mixed_s, dmix_s, dvn_s, dbmap_s) = rest[-12:]
        i = pl.program_id(0)
        seq_start = (i * tm) % seq == 0
        keep_next = jnp.where(((i + 1) * tm) % seq == 0, 0.0, 1.0)

        @pl.when(i == 0)
        def _():
            for ref in (dwout_ref, dcw_ref, dlg_ref, dlb_ref, dws_ref, dbmap_s):
                ref[...] = jnp.zeros_like(ref)

        def piece(k):
            return proj_ref[:, k * d:(k + 1) * d].astype(F32)

        def put(k, val):
            dproj_ref[:, k * d:(k + 1) * d] = val.astype(BF16)

        w = [cw_ref[k:k + 1, :] for k in range(3)]
        cg, xi = piece(1), piece(2)
        z = cg * xi
        zprev = jnp.where(seq_start, 0.0, pcg_ref[...].astype(F32) * pxi_ref[...].astype(F32))
        z2, z1 = _shift_prev(zprev, z, 2), _shift_prev(zprev, z, 1)
        cz = w[0] * z2 + w[1] * z1 + w[2] * z
        bg = piece(0)
        ya = bg * cz

        v = piece(4)
        xc = v - _lanemean(v)
        rstd = lax.rsqrt(_lanemean(xc * xc) + EPS)
        vhat = xc * rstd
        vn_s[...] = (vhat * lg_ref[...] + lb_ref[...]).astype(BF16)
        tril, wsm = _tril_weights(ws_ref, groups)
        for ck in range(tm // CHUNK):
            rows = slice(ck * CHUNK, (ck + 1) * CHUNK)
            for g in range(groups):
                cols = slice(g * gd, (g + 1) * gd)
                mixed_s[rows, cols] = (jnp.dot(wsm[g].astype(BF16), vn_s[rows, cols], preferred_element_type=F32)
                                       + bmap_ref[:, cols])
        u = piece(3)
        mixed = mixed_s[...]
        yb = u * mixed
        sa, sb = jax.nn.sigmoid(piece(5)), jax.nn.sigmoid(piece(6))
        merged = sa * ya + sb * yb

        dxe = jnp.concatenate([dx_ref[...], dxn_ref[...]], axis=0).astype(BF16)
        dme = lax.dot_general(dxe, wout_ref[...], NT_DIMS, preferred_element_type=F32)
        dm, dm_n = dme[:tm], dme[tm:]
        dwout_ref[...] += lax.dot_general(merged.astype(BF16), dxe[:tm], TN_DIMS, preferred_element_type=F32)

        put(5, dm * ya * sa * (1.0 - sa))
        put(6, dm * yb * sb * (1.0 - sb))
        d_ya, d_yb = dm * sa, dm * sb
        put(0, d_ya * cz)
        d_cz = d_ya * bg
        for k, tap in enumerate((z2, z1, z)):
            dcw_ref[k:k + 1, :] += _rowsum(d_cz * tap)
        d_cz_n = dm_n * jax.nn.sigmoid(nga_ref[...].astype(F32)) * nbg_ref[...].astype(F32) * keep_next
        dz = w[2] * d_cz + w[1] * _shift_next(d_cz, d_cz_n, 1) + w[0] * _shift_next(d_cz, d_cz_n, 2)
        put(1, dz * xi)
        put(2, dz * cg)

        put(3, d_yb * mixed)
        d_mixed = d_yb * u
        dmix_s[...] = d_mixed.astype(BF16)
        for ck in range(tm // CHUNK):
            rows = slice(ck * CHUNK, (ck + 1) * CHUNK)
            dbmap_s[...] += d_mixed[rows, :]
            for g in range(groups):
                cols = slice(g * gd, (g + 1) * gd)
                dvn_s[rows, cols] = jnp.dot(wsm[g].T.astype(BF16), dmix_s[rows, cols], preferred_element_type=F32)
                dws_ref[g] += jnp.where(
                    tril, lax.dot_general(dmix_s[rows, cols], vn_s[rows, cols], NT_DIMS, preferred_element_type=F32),
                    0.0)
        d_vn = dvn_s[...]
        dlg_ref[...] += _rowsum(d_vn * vhat)
        dlb_ref[...] += _rowsum(d_vn)
        d_vhat = d_vn * lg_ref[...]
        put(4, rstd * (d_vhat - _lanemean(d_vhat) - vhat * _lanemean(d_vhat * vhat)))

        @pl.when(i == n_tiles - 1)
        def _():
            for g in range(groups):
                dbs_ref[:, g:g + 1] = jnp.sum(dbmap_s[:, g * gd:(g + 1) * gd], axis=-1, keepdims=True)

    full = lambda shape: pl.BlockSpec(shape, lambda i: (0,) * len(shape))
    extra, extra_specs = _alias_args(dwout_prev)
    n_in = 13
    return pl.pallas_call(
        body, name=name, grid=(n_tiles,),
        in_specs=[pl.BlockSpec((tm, N_PIECES * d), lambda i: (i, 0)),
                  pl.BlockSpec((HALO, d), lambda i: (prev(i), 1)),
                  pl.BlockSpec((HALO, d), lambda i: (prev(i), 2)),
                  pl.BlockSpec((HALO, d), lambda i: (nxt(i), 0)),
                  pl.BlockSpec((HALO, d), lambda i: (nxt(i), 5)),
                  pl.BlockSpec((tm, d), lambda i: (i, 0)),
                  pl.BlockSpec((HALO, d), lambda i: (nxt(i), 0)),
                  full((3, d)), full((1, d)), full((1, d)), full((groups, CHUNK, CHUNK)), full((CHUNK, d)),
                  full((d, d))] + extra_specs,
        out_specs=[pl.BlockSpec((tm, N_PIECES * d), lambda i: (i, 0)),
                   pl.BlockSpec((None, d, d), lambda i: (layer, 0, 0)),
                   full((3, d)), full((1, d)), full((1, d)), full((groups, CHUNK, CHUNK)), full((CHUNK, groups))],
        out_shape=[jax.ShapeDtypeStruct((n, N_PIECES * d), BF16), jax.ShapeDtypeStruct((n_layers, d, d), F32),
                   jax.ShapeDtypeStruct((3, d), F32), jax.ShapeDtypeStruct((1, d), F32),
                   jax.ShapeDtypeStruct((1, d), F32), jax.ShapeDtypeStruct((groups, CHUNK, CHUNK), F32),
                   jax.ShapeDtypeStruct((CHUNK, groups), F32)],
        scratch_shapes=[pltpu.VMEM((tm, d), BF16), pltpu.VMEM((tm, d), F32), pltpu.VMEM((tm, d), BF16),
                        pltpu.VMEM((tm, d), F32), pltpu.VMEM((CHUNK, d), F32)],
        input_output_aliases={n_in: 1} if extra else {},
        compiler_params=_params("arbitrary"),
    )(proj, proj, proj, proj, proj, dx, dx, cw, lg, lb, ws, bmap, wout, *extra)


def _mesh_pos():
    return lax.axis_index("x"), lax.axis_index("y"), lax.axis_index("c")


def _peer_chip(x, y, r):
    return (1 - x if r >> 1 else x), (1 - y if r & 1 else y)


HBM_SPEC = pl.BlockSpec(memory_space=pltpu.HBM)


def _all_gather_weights(shards, name):
    n_w = len(shards)

    def body(*refs):
        ins, outs = refs[:n_w], refs[n_w:2 * n_w]
        ici_send, ici_recv, d2d_send, d2d_recv, own_sem = refs[2 * n_w:]
        x, y, c = _mesh_pos()
        chip = 2 * x + y
        sibling = (x, y, 1 - c)

        def half(a, of_chip, core):
            rh = ins[a].shape[1] // 2
            return outs[a].at[:, of_chip, pl.ds(core * rh, rh), :]

        def copy(a, src, of_chip, core, sems, k, to):
            return pltpu.make_async_remote_copy(src_ref=src, dst_ref=half(a, of_chip, core), send_sem=sems[0].at[k],
                                                recv_sem=sems[1].at[k], device_id=to, device_id_type=MESH_T)

        own = [pltpu.make_async_copy(ins[a], outs[a].at[:, chip], own_sem.at[a]) for a in range(n_w)]
        for cp in own:
            cp.start()
        sends = []
        for a in range(n_w):
            rh = ins[a].shape[1] // 2
            for r in (1, 2, 3):
                px, py = _peer_chip(x, y, r)
                sends.append(copy(a, ins[a].at[:, pl.ds(c * rh, rh), :], chip, c, (ici_send, ici_recv),
                                  3 * a + r - 1, (px, py, c)))
        for cp in sends:
            cp.start()
        passed = []
        for a in range(n_w):
            for r in (1, 2, 3):
                px, py = _peer_chip(x, y, r)
                k, from_chip = 3 * a + r - 1, 2 * px + py
                copy(a, half(a, from_chip, c), from_chip, c, (ici_send, ici_recv), k, sibling).wait_recv()
                fwd = copy(a, half(a, from_chip, c), from_chip, c, (d2d_send, d2d_recv), k, sibling)
                fwd.start()
                passed.append(fwd)
        for a in range(n_w):
            for r in (1, 2, 3):
                px, py = _peer_chip(x, y, r)
                from_chip = 2 * px + py
                copy(a, half(a, from_chip, 1 - c), from_chip, 1 - c, (d2d_send, d2d_recv), 3 * a + r - 1,
                     sibling).wait_recv()
        for cp in sends + passed:
            cp.wait_send()
        for cp in own:
            cp.wait()

    return pl.pallas_call(
        body, name=name,
        in_specs=[HBM_SPEC] * n_w, out_specs=[HBM_SPEC] * n_w,
        out_shape=[jax.ShapeDtypeStruct((s.shape[0], N_CHIP) + s.shape[1:], s.dtype) for s in shards],
        scratch_shapes=[pltpu.SemaphoreType.DMA((3 * n_w,)), pltpu.SemaphoreType.DMA((3 * n_w,)),
                        pltpu.SemaphoreType.DMA((3 * n_w,)), pltpu.SemaphoreType.DMA((3 * n_w,)),
                        pltpu.SemaphoreType.DMA((n_w,))],
    )(*shards)


def _swap_halves(grads, name):
    n_w = len(grads)

    def body(*refs):
        ins, outs = refs[:n_w], refs[n_w:2 * n_w]
        send_sem, recv_sem = refs[2 * n_w:]
        x, y, c = _mesh_pos()
        copies = []
        for a in range(n_w):
            rh = ins[a].shape[2] // 2
            copies.append(pltpu.make_async_remote_copy(
                src_ref=ins[a].at[:, :, pl.ds((1 - c) * rh, rh), :], dst_ref=outs[a], send_sem=send_sem.at[a],
                recv_sem=recv_sem.at[a], device_id=(x, y, 1 - c), device_id_type=MESH_T))
        for cp in copies:
            cp.start()
        for cp in copies:
            cp.wait()

    return pl.pallas_call(
        body, name=name,
        in_specs=[HBM_SPEC] * n_w, out_specs=[HBM_SPEC] * n_w,
        out_shape=[jax.ShapeDtypeStruct(g.shape[:2] + (g.shape[2] // 2, g.shape[3]), g.dtype) for g in grads],
        scratch_shapes=[pltpu.SemaphoreType.DMA((n_w,)), pltpu.SemaphoreType.DMA((n_w,))],
    )(*grads)


def _scatter_to_chips(parts, name):
    n_w = len(parts)

    def body(*refs):
        ins, outs = refs[:n_w], refs[n_w:2 * n_w]
        send_sem, recv_sem = refs[2 * n_w:]
        x, y, c = _mesh_pos()
        copies = []
        for a in range(n_w):
            for r in (1, 2, 3):
                px, py = _peer_chip(x, y, r)
                copies.append(pltpu.make_async_remote_copy(
                    src_ref=ins[a].at[:, 2 * px + py], dst_ref=outs[a].at[:, r - 1], send_sem=send_sem.at[3 * a + r - 1],
                    recv_sem=recv_sem.at[3 * a + r - 1], device_id=(px, py, c), device_id_type=MESH_T))
        for cp in copies:
            cp.start()
        for cp in copies:
            cp.wait()

    return pl.pallas_call(
        body, name=name,
        in_specs=[HBM_SPEC] * n_w, out_specs=[HBM_SPEC] * n_w,
        out_shape=[jax.ShapeDtypeStruct((p.shape[0], 3) + p.shape[2:], p.dtype) for p in parts],
        scratch_shapes=[pltpu.SemaphoreType.DMA((3 * n_w,)), pltpu.SemaphoreType.DMA((3 * n_w,))],
    )(*parts)


def _share_halves(gfull, name):
    n_w = len(gfull)

    def body(*refs):
        outs = refs[n_w:2 * n_w]
        send_sem, recv_sem = refs[2 * n_w:]
        x, y, c = _mesh_pos()
        sends, recvs = [], []
        for a in range(n_w):
            rh = outs[a].shape[1] // 2
            mine, theirs = outs[a].at[:, pl.ds(c * rh, rh), :], outs[a].at[:, pl.ds((1 - c) * rh, rh), :]
            sends.append(pltpu.make_async_remote_copy(
                src_ref=mine, dst_ref=mine, send_sem=send_sem.at[a], recv_sem=recv_sem.at[a],
                device_id=(x, y, 1 - c), device_id_type=MESH_T))
            recvs.append(pltpu.make_async_remote_copy(
                src_ref=theirs, dst_ref=theirs, send_sem=send_sem.at[a], recv_sem=recv_sem.at[a],
                device_id=(x, y, 1 - c), device_id_type=MESH_T))
        for cp in sends:
            cp.start()
        for cp in sends:
            cp.wait_send()
        for cp in recvs:
            cp.wait_recv()

    return pl.pallas_call(
        body, name=name,
        in_specs=[HBM_SPEC] * n_w, out_specs=[HBM_SPEC] * n_w,
        out_shape=[jax.ShapeDtypeStruct(g.shape, g.dtype) for g in gfull],
        input_output_aliases={a: a for a in range(n_w)},
        scratch_shapes=[pltpu.SemaphoreType.DMA((n_w,)), pltpu.SemaphoreType.DMA((n_w,))],
    )(*gfull)


def _gather_small(pack, name):
    def body(p_ref, o_ref, send_sem, recv_sem):
        x, y, c = _mesh_pos()
        chip = 2 * x + y
        o_ref[chip] = p_ref[...]
        copies = []
        for r in (1, 2, 3):
            px, py = _peer_chip(x, y, r)
            copies.append(pltpu.make_async_remote_copy(
                src_ref=p_ref, dst_ref=o_ref.at[chip], send_sem=send_sem.at[r - 1], recv_sem=recv_sem.at[r - 1],
                device_id=(px, py, c), device_id_type=MESH_T))
        for cp in copies:
            cp.start()
        for r, cp in zip((1, 2, 3), copies):
            px, py = _peer_chip(x, y, r)
            landed = o_ref.at[2 * px + py]
            pltpu.make_async_remote_copy(src_ref=landed, dst_ref=landed, send_sem=send_sem.at[r - 1],
                                         recv_sem=recv_sem.at[r - 1], device_id=(px, py, c),
                                         device_id_type=MESH_T).wait_recv()
            cp.wait_send()

    vmem = pl.BlockSpec(memory_space=pltpu.VMEM)
    return pl.pallas_call(
        body, name=name, in_specs=[vmem], out_specs=vmem,
        out_shape=jax.ShapeDtypeStruct((N_CHIP,) + pack.shape, pack.dtype),
        scratch_shapes=[pltpu.SemaphoreType.DMA((3,)), pltpu.SemaphoreType.DMA((3,))],
    )(pack)


def _all_reduce_small(pack, name):
    rows = pack.shape[0]

    def body(p_ref, o_ref, sib_buf, chip_buf, send_sem, recv_sem):
        x, y, c = _mesh_pos()
        chip = 2 * x + y
        swap = pltpu.make_async_remote_copy(src_ref=p_ref, dst_ref=sib_buf, send_sem=send_sem.at[0],
                                            recv_sem=recv_sem.at[0], device_id=(x, y, 1 - c), device_id_type=MESH_T)
        swap.start()
        swap.wait()
        chip_buf[chip] = p_ref[...] + sib_buf[...]
        copies = []
        for r in (1, 2, 3):
            px, py = _peer_chip(x, y, r)
            copies.append(pltpu.make_async_remote_copy(
                src_ref=chip_buf.at[chip], dst_ref=chip_buf.at[chip], send_sem=send_sem.at[r], recv_sem=recv_sem.at[r],
                device_id=(px, py, c), device_id_type=MESH_T))
        for cp in copies:
            cp.start()
        for r, cp in zip((1, 2, 3), copies):
            px, py = _peer_chip(x, y, r)
            landed = chip_buf.at[2 * px + py]
            pltpu.make_async_remote_copy(src_ref=landed, dst_ref=landed, send_sem=send_sem.at[r], recv_sem=recv_sem.at[r],
                                         device_id=(px, py, c), device_id_type=MESH_T).wait_recv()
            cp.wait_send()
        o_ref[...] = ((chip_buf[0] + chip_buf[1]) + chip_buf[2]) + chip_buf[3]

    vmem = pl.BlockSpec(memory_space=pltpu.VMEM)
    return pl.pallas_call(
        body, name=name, in_specs=[vmem], out_specs=vmem,
        out_shape=jax.ShapeDtypeStruct(pack.shape, pack.dtype),
        scratch_shapes=[pltpu.VMEM((rows, 128), F32), pltpu.VMEM((N_CHIP, rows, 128), F32),
                        pltpu.SemaphoreType.DMA((4,)), pltpu.SemaphoreType.DMA((4,))],
        compiler_params=pltpu.CompilerParams(vmem_limit_bytes=VMEM_LIMIT_V7X),
    )(pack)


def _block_rows(rows, cols, itemsize=4, budget=2 << 20):
    best = None
    for t in range(16, rows + 1, 16):
        if rows % t == 0 and t * cols * itemsize <= budget:
            best = t
    assert best is not None, (rows, cols)
    return best


def _pair_sum(grad, recv, core, name):
    n_l, n_chip, rk, ck = grad.shape
    rh = rk // 2
    tr = _block_rows(rh, ck)
    nb = rh // tr

    def body(core_ref, g_ref, r_ref, o_ref):
        o_ref[...] = (g_ref[...] + r_ref[...]).astype(BF16)

    blk = (None, None, tr, ck)
    return pl.pallas_call(
        body, name=name,
        grid_spec=pltpu.PrefetchScalarGridSpec(
            num_scalar_prefetch=1, grid=(n_l, n_chip, nb),
            in_specs=[pl.BlockSpec(blk, lambda l, k, i, core_ref: (l, k, core_ref[0] * nb + i, 0)),
                      pl.BlockSpec(blk, lambda l, k, i, core_ref: (l, k, i, 0))],
            out_specs=pl.BlockSpec(blk, lambda l, k, i, core_ref: (l, k, i, 0))),
        out_shape=jax.ShapeDtypeStruct((n_l, n_chip, rh, ck), BF16),
        compiler_params=_params("parallel", "parallel", "parallel"),
    )(core, grad, recv)


def _final_sum(grad, recv, arrived, where, name):
    n_l, n_chip, rk, ck = grad.shape
    rh = rk // 2
    tr = _block_rows(rh, ck)
    nb = rh // tr

    def body(where_ref, g_ref, r_ref, a1_ref, a2_ref, a3_ref, o_ref):
        own = g_ref[...] + r_ref[...]
        o_ref[...] = ((own + a1_ref[...].astype(F32)) + a2_ref[...].astype(F32)) + a3_ref[...].astype(F32)

    blk = (None, None, tr, ck)
    slot = lambda r: pl.BlockSpec(blk, lambda l, i, w: (l, r, i, 0))
    return pl.pallas_call(
        body, name=name,
        grid_spec=pltpu.PrefetchScalarGridSpec(
            num_scalar_prefetch=1, grid=(n_l, nb),
            in_specs=[pl.BlockSpec(blk, lambda l, i, w: (l, w[1], w[0] * nb + i, 0)),
                      pl.BlockSpec(blk, lambda l, i, w: (l, w[1], i, 0)),
                      slot(0), slot(1), slot(2)],
            out_specs=pl.BlockSpec((None, tr, ck), lambda l, i, w: (l, w[0] * nb + i, 0))),
        out_shape=jax.ShapeDtypeStruct((n_l, rk, ck), F32),
        compiler_params=_params("parallel", "parallel"),
    )(where, grad, recv, arrived, arrived, arrived)


def _adamw_math(w, g, m, v):
    m = ADAM_B1 * m + (1.0 - ADAM_B1) * g
    v = ADAM_B2 * v + (1.0 - ADAM_B2) * (g * g)
    m_hat = m / (1.0 - ADAM_B1 ** ADAM_STEP)
    v_hat = v / (1.0 - ADAM_B2 ** ADAM_STEP)
    delta = -ADAM_LR * (m_hat / (jnp.sqrt(v_hat) + ADAM_EPS) + ADAM_WD * w)
    return delta, m, v


def _adamw_big(w, g, m, v, name):
    n_l, rk, ck = w.shape
    tr = _block_rows(rk, ck, budget=1 << 20)

    def body(w_ref, g_ref, m_ref, v_ref, d_ref, nm_ref, nv_ref):
        d_ref[...], nm_ref[...], nv_ref[...] = _adamw_math(w_ref[...], g_ref[...], m_ref[...], v_ref[...])

    spec = pl.BlockSpec((None, tr, ck), lambda l, i: (l, i, 0))
    return pl.pallas_call(
        body, name=name, grid=(n_l, rk // tr), in_specs=[spec] * 4, out_specs=[spec] * 3,
        out_shape=[jax.ShapeDtypeStruct(w.shape, F32)] * 3,
        compiler_params=_params("parallel", "parallel"),
    )(w, g, m, v)


def _adamw_small(ws, gs, ms, vs, name):
    n_p = len(ws)

    def body(*refs):
        ins, outs = refs[:4 * n_p], refs[4 * n_p:]
        for p in range(n_p):
            res = _adamw_math(ins[p][...], ins[n_p + p][...], ins[2 * n_p + p][...], ins[3 * n_p + p][...])
            for q in range(3):
                outs[q * n_p + p][...] = res[q]

    vmem = pl.BlockSpec(memory_space=pltpu.VMEM)
    outs = pl.pallas_call(
        body, name=name, in_specs=[vmem] * (4 * n_p), out_specs=[vmem] * (3 * n_p),
        out_shape=[jax.ShapeDtypeStruct(w.shape, F32) for w in ws] * 3,
        compiler_params=pltpu.CompilerParams(vmem_limit_bytes=VMEM_LIMIT_V7X),
    )(*ws, *gs, *ms, *vs)
    return outs[:n_p], outs[n_p:2 * n_p], outs[2 * n_p:]


def kernel(x, mix_norm_g, w_in, conv_a_w, ln_v_g, ln_v_b, w_s, b_s, w_out, ffn_norm_g, w_up, conv_ffn_w, w_down, final_norm_g, loss_target, m_mix_norm_g, m_w_in, m_conv_a_w, m_ln_v_g, m_ln_v_b, m_w_s, m_b_s, m_w_out, m_ffn_norm_g, m_w_up, m_conv_ffn_w, m_w_down, m_final_norm_g, v_mix_norm_g, v_w_in, v_conv_a_w, v_ln_v_g, v_ln_v_b, v_w_s, v_b_s, v_w_out, v_ffn_norm_g, v_w_up, v_conv_ffn_w, v_w_down, v_final_norm_g):
    bsz, seq, d = x.shape
    n = bsz * seq
    n_l, groups = w_s.shape[0], w_s.shape[1]
    gd = d // groups
    ff = w_down.shape[1] * N_CHIP
    axes = ("x", "y", "c")
    mx, my, mc = _mesh_pos()
    chip = 2 * mx + my
    core = jnp.reshape(mc, (1,)).astype(jnp.int32)
    where = jnp.stack([mc, chip]).astype(jnp.int32)

    tm_mm = _row_tile(seq, 512)
    tm_ew = _row_tile(seq, 256)
    ff_chunk = ff // 2 if (ff // 2) % 128 == 0 else ff

    big = [w_in, w_out, w_up, w_down]
    w_in_g, w_out_g, w_up_g, w_down_g = _all_gather_weights([w.astype(BF16) for w in big], "gather_weights")
    w_out_f = w_out_g.reshape(n_l, d, d)
    w_down_f = w_down_g.reshape(n_l, ff, d)

    taps = jnp.concatenate([conv_a_w.reshape(n_l, -1), conv_ffn_w.reshape(n_l, -1)], axis=1)
    tap_rows = -(-taps.size // 128 // 8) * 8
    tap_pack = jnp.zeros((tap_rows * 128,), F32).at[:taps.size].set(taps.reshape(-1)).reshape(tap_rows, 128)
    tap_all = _gather_small(tap_pack, "gather_taps")
    tap_all = tap_all.reshape(N_CHIP, -1)[:, :taps.size].reshape(N_CHIP, n_l, -1)
    ca = tap_all[:, :, :3 * d // N_CHIP].reshape(N_CHIP, n_l, 3, d // N_CHIP)
    cf = tap_all[:, :, 3 * d // N_CHIP:].reshape(N_CHIP, n_l, 3, 2 * ff // N_CHIP)
    conv_a_full = jnp.transpose(ca, (1, 2, 0, 3)).reshape(n_l, 3, d)
    conv_f_full = jnp.transpose(cf, (1, 2, 0, 3)).reshape(n_l, 3, 2 * ff)

    bmaps = jnp.repeat(jnp.swapaxes(b_s, 1, 2), gd, axis=2)

    xs = x.reshape(n, d)
    tgt = loss_target.reshape(n, d)

    saved = []
    cur = xs
    for l in range(n_l):
        proj, h1 = _norm_matmul(cur, mix_norm_g[l][None], w_in_g, l, tm=tm_mm, name=f"fwd_in_proj_{l}")
        x1 = _mixer_fwd(proj, cur, conv_a_full[l], ln_v_g[l][None], ln_v_b[l][None], w_s[l], bmaps[l], w_out_f[l],
                        seq=seq, tm=tm_ew, name=f"fwd_mixer_{l}")
        up, h2 = _norm_matmul(x1, ffn_norm_g[l][None], w_up_g, l, tm=tm_mm, name=f"fwd_up_proj_{l}")
        x2 = _ffn_fwd(up, x1, conv_f_full[l], w_down_f[l], seq=seq, tm=tm_ew, cwid=ff_chunk, name=f"fwd_ffn_{l}")
        saved.append((cur, h1, proj, x1, h2, up))
        cur = x2
    dx, loss_tile, d_final_g = _final_loss(cur, final_norm_g[None], tgt, tm=tm_mm, name="final_loss")
    loss = lax.psum(loss_tile[0, 0], axes)

    dw_in = dw_out = dw_up = dw_down = None
    small = [None] * n_l
    for l in reversed(range(n_l)):
        x0, h1, proj, x1, h2, up = saved[l]
        d_up, dw_down, d_cf = _ffn_bwd(dx, up, conv_f_full[l], w_down_f[l], dw_down, l, n_l, seq=seq, tm=tm_ew,
                                       cwid=ff_chunk, name=f"bwd_ffn_{l}")
        dx1, d_g2 = _matmul_bwd_x(d_up, w_up_g, x1, ffn_norm_g[l][None], dx, l, tm=tm_mm, name=f"bwd_up_x_{l}")
        dw_up = _matmul_bwd_w(h2, d_up, dw_up, l, n_l, N_CHIP, tk=tm_mm, name=f"bwd_up_w_{l}")
        d_proj, dw_out, d_ca, d_lg, d_lb, d_ws, d_bs = _mixer_bwd(
            dx1, proj, conv_a_full[l], ln_v_g[l][None], ln_v_b[l][None], w_s[l], bmaps[l], w_out_f[l], dw_out, l, n_l,
            seq=seq, tm=tm_ew, name=f"bwd_mixer_{l}")
        dx, d_g1 = _matmul_bwd_x(d_proj, w_in_g, x0, mix_norm_g[l][None], dx1, l, tm=tm_mm, name=f"bwd_in_x_{l}")
        dw_in = _matmul_bwd_w(h1, d_proj, dw_in, l, n_l, N_CHIP, tk=tm_mm, name=f"bwd_in_w_{l}")
        small[l] = [d_g1, d_ca, d_lg, d_lb, d_ws, d_bs.T, d_g2, d_cf]
    grad_x = dx.reshape(bsz, seq, d)

    grads = [dw_in, dw_out.reshape(n_l, N_CHIP, d // N_CHIP, d), dw_up, dw_down.reshape(n_l, N_CHIP, ff // N_CHIP, d)]
    from_sib = _swap_halves(grads, "grad_swap_halves")
    parts = [_pair_sum(g, r, core, f"grad_pair_sum_{a}") for a, (g, r) in enumerate(zip(grads, from_sib))]
    arrived = _scatter_to_chips(parts, "grad_scatter")
    halves = [_final_sum(g, r, p, where, f"grad_final_sum_{a}")
              for a, (g, r, p) in enumerate(zip(grads, from_sib, arrived))]
    g_in, g_out, g_up, g_down = _share_halves(halves, "grad_share_halves")

    flat = [a.reshape(-1) for l in range(n_l) for a in small[l]] + [d_final_g.reshape(-1)]
    sizes = [a.size for a in flat]
    total = sum(sizes)
    rows = -(-total // 128 // 8) * 8
    pack = jnp.concatenate(flat + [jnp.zeros((rows * 128 - total,), F32)]).reshape(rows, 128)
    red = _all_reduce_small(pack, "small_all_reduce").reshape(-1)
    pieces, off = [], 0
    for s in sizes:
        pieces.append(red[off:off + s])
        off += s
    per_layer = [pieces[l * 8:(l + 1) * 8] for l in range(n_l)]

    def stacked(idx, shape):
        return jnp.stack([per_layer[l][idx].reshape(shape) for l in range(n_l)])

    def my_cols(a, width):
        return lax.dynamic_slice_in_dim(a, chip * width, width, axis=-1)

    g_mix_norm = stacked(0, (d,))
    g_conv_a = my_cols(stacked(1, (3, d)), d // N_CHIP)
    g_ln_g = stacked(2, (d,))
    g_ln_b = stacked(3, (d,))
    g_ws = stacked(4, (groups, CHUNK, CHUNK))
    g_bs = stacked(5, (groups, CHUNK))
    g_ffn_norm = stacked(6, (d,))
    g_conv_f = my_cols(stacked(7, (3, 2 * ff)), 2 * ff // N_CHIP)
    g_final = pieces[-1].reshape(1, d)

    big_names = [(w_in, g_in, m_w_in, v_w_in), (w_out, g_out, m_w_out, v_w_out), (w_up, g_up, m_w_up, v_w_up),
                 (w_down, g_down, m_w_down, v_w_down)]
    upd = [_adamw_big(w, g, m, v, f"adamw_{a}") for a, (w, g, m, v) in enumerate(big_names)]

    sm_w = [mix_norm_g, conv_a_w, ln_v_g, ln_v_b, w_s, b_s, ffn_norm_g, conv_ffn_w, final_norm_g[None]]
    sm_g = [g_mix_norm, g_conv_a, g_ln_g, g_ln_b, g_ws, g_bs, g_ffn_norm, g_conv_f, g_final]
    sm_m = [m_mix_norm_g, m_conv_a_w, m_ln_v_g, m_ln_v_b, m_w_s, m_b_s, m_ffn_norm_g, m_conv_ffn_w, m_final_norm_g[None]]
    sm_v = [v_mix_norm_g, v_conv_a_w, v_ln_v_g, v_ln_v_b, v_w_s, v_b_s, v_ffn_norm_g, v_conv_ffn_w, v_final_norm_g[None]]
    sm_d, sm_nm, sm_nv = _adamw_small(sm_w, sm_g, sm_m, sm_v, "adamw_small")

    def ordered(sm, bigs):
        return [sm[0], bigs[0], sm[1], sm[2], sm[3], sm[4], sm[5], bigs[1], sm[6], bigs[2], sm[7], bigs[3],
                sm[8].reshape(d)]

    out_g = ordered(sm_g, [g_in, g_out, g_up, g_down])
    out_d = ordered(sm_d, [u[0] for u in upd])
    out_m = ordered(sm_nm, [u[1] for u in upd])
    out_v = ordered(sm_nv, [u[2] for u in upd])
    return (loss, grad_x, *out_g, *out_d, *out_m, *out_v)
```

```python
import functools

import jax
import jax.numpy as jnp
from jax import lax
from jax.experimental import pallas as pl
from jax.experimental.pallas import tpu as pltpu

F32 = jnp.float32
BF16 = jnp.bfloat16
EPS = 1e-6
CHUNK = 128
N_CHIP = 4
HALO = 8
N_PIECES = 7
VMEM_LIMIT_V7X = 56 * 1024 * 1024
MESH_T = pl.DeviceIdType.MESH

ADAM_LR, ADAM_B1, ADAM_B2, ADAM_EPS, ADAM_WD, ADAM_STEP = 0.001, 0.9, 0.999, 1e-08, 0.01, 10

NT_DIMS = (((1,), (1,)), ((), ()))
TN_DIMS = (((0,), (0,)), ((), ()))


def _params(*sem):
    return pltpu.CompilerParams(dimension_semantics=sem, vmem_limit_bytes=VMEM_LIMIT_V7X)


def _resident(block_shape, index_map):
    return pl.BlockSpec(block_shape, index_map, pipeline_mode=pl.Buffered(1))


def _row_tile(seq, want):
    t = min(seq, want)
    assert seq % t == 0 and t % CHUNK == 0, (seq, want)
    return t


def _shift_prev(prev8, cur, k):
    ext = jnp.concatenate([prev8, cur], axis=0)
    return pltpu.roll(ext, k, 0)[HALO:]


def _shift_next(cur, next8, k):
    ext = jnp.concatenate([cur, next8], axis=0)
    n = ext.shape[0]
    return pltpu.roll(ext, n - k, 0)[:n - HALO]


def _rowsum(a):
    return jnp.sum(a, axis=0, keepdims=True)


def _lanemean(a):
    return jnp.mean(a, axis=-1, keepdims=True)


def _tril_weights(ws_ref, groups):
    r = lax.broadcasted_iota(jnp.int32, (CHUNK, CHUNK), 0)
    c = lax.broadcasted_iota(jnp.int32, (CHUNK, CHUNK), 1)
    tril = r >= c
    return tril, [jnp.where(tril, ws_ref[g], 0.0) for g in range(groups)]


def _norm_matmul(x, g, wg, layer, *, tm, name):
    n, d = x.shape
    _, n_chip, _, ck = wg.shape

    def body(x_ref, g_ref, w_ref, o_ref, h_ref):
        xf = x_ref[...]
        r = lax.rsqrt(_lanemean(xf * xf) + EPS)
        h_ref[...] = (xf * r * g_ref[...]).astype(BF16)
        for j in range(n_chip):
            o_ref[:, j * ck:(j + 1) * ck] = jnp.dot(h_ref[...], w_ref[j], preferred_element_type=F32).astype(BF16)

    return pl.pallas_call(
        body, name=name, grid=(n // tm,),
        in_specs=[pl.BlockSpec((tm, d), lambda i: (i, 0)),
                  pl.BlockSpec((1, d), lambda i: (0, 0)),
                  _resident((None, n_chip, d, ck), lambda i: (layer, 0, 0, 0))],
        out_specs=[pl.BlockSpec((tm, n_chip * ck), lambda i: (i, 0)),
                   pl.BlockSpec((tm, d), lambda i: (i, 0))],
        out_shape=[jax.ShapeDtypeStruct((n, n_chip * ck), BF16), jax.ShapeDtypeStruct((n, d), BF16)],
        compiler_params=_params("parallel"),
    )(x, g, wg)


def _halo_prev(tm):
    return lambda i: jnp.maximum(i * (tm // HALO) - 1, 0)


def _halo_next(tm, n):
    last = n // HALO - 1
    return lambda i: jnp.minimum((i + 1) * (tm // HALO), last)


def _mixer_fwd(proj, x, cw, lg, lb, ws, bmap, wout, *, seq, tm, name):
    n, d = x.shape
    groups = ws.shape[0]
    gd = d // groups
    prev = _halo_prev(tm)

    def body(proj_ref, pcg_ref, pxi_ref, x_ref, cw_ref, lg_ref, lb_ref, ws_ref, bmap_ref, wout_ref, o_ref,
             vn_s, mixed_s):
        seq_start = (pl.program_id(0) * tm) % seq == 0

        def piece(k):
            return proj_ref[:, k * d:(k + 1) * d].astype(F32)

        z = piece(1) * piece(2)
        zprev = jnp.where(seq_start, 0.0, pcg_ref[...].astype(F32) * pxi_ref[...].astype(F32))
        cz = (cw_ref[0:1, :] * _shift_prev(zprev, z, 2) + cw_ref[1:2, :] * _shift_prev(zprev, z, 1)
              + cw_ref[2:3, :] * z)
        ya = piece(0) * cz

        v = piece(4)
        xc = v - _lanemean(v)
        vn = xc * lax.rsqrt(_lanemean(xc * xc) + EPS) * lg_ref[...] + lb_ref[...]
        vn_s[...] = vn.astype(BF16)
        _, wsm = _tril_weights(ws_ref, groups)
        for ck in range(tm // CHUNK):
            rows = slice(ck * CHUNK, (ck + 1) * CHUNK)
            for g in range(groups):
                cols = slice(g * gd, (g + 1) * gd)
                mixed_s[rows, cols] = (jnp.dot(wsm[g].astype(BF16), vn_s[rows, cols], preferred_element_type=F32)
                                       + bmap_ref[:, cols])
        yb = piece(3) * mixed_s[...]
        merged = jax.nn.sigmoid(piece(5)) * ya + jax.nn.sigmoid(piece(6)) * yb
        o_ref[...] = x_ref[...] + jnp.dot(merged.astype(BF16), wout_ref[...], preferred_element_type=F32)

    full = lambda shape: pl.BlockSpec(shape, lambda i: (0,) * len(shape))
    return pl.pallas_call(
        body, name=name, grid=(n // tm,),
        in_specs=[pl.BlockSpec((tm, N_PIECES * d), lambda i: (i, 0)),
                  pl.BlockSpec((HALO, d), lambda i: (prev(i), 1)),
                  pl.BlockSpec((HALO, d), lambda i: (prev(i), 2)),
                  pl.BlockSpec((tm, d), lambda i: (i, 0)),
                  full((3, d)), full((1, d)), full((1, d)), full((groups, CHUNK, CHUNK)), full((CHUNK, d)),
                  full((d, d))],
        out_specs=pl.BlockSpec((tm, d), lambda i: (i, 0)),
        out_shape=jax.ShapeDtypeStruct((n, d), F32),
        scratch_shapes=[pltpu.VMEM((tm, d), BF16), pltpu.VMEM((tm, d), F32)],
        compiler_params=_params("parallel"),
    )(proj, proj, proj, x, cw, lg, lb, ws, bmap, wout)


def _ffn_fwd(up, x, cw, wd, *, seq, tm, cwid, name):
    n, d = x.shape
    ff = wd.shape[0]
    prev = _halo_prev(tm)

    def body(up_ref, pup_ref, x_ref, cw_ref, wd_ref, o_ref, conv_ref):
        seq_start = (pl.program_id(0) * tm) % seq == 0

        def conv(lo):
            cols = slice(lo, lo + cwid)
            cur = up_ref[:, cols].astype(F32)
            pre = jnp.where(seq_start, 0.0, pup_ref[:, cols].astype(F32))
            out = (cw_ref[0:1, cols] * _shift_prev(pre, cur, 2) + cw_ref[1:2, cols] * _shift_prev(pre, cur, 1)
                   + cw_ref[2:3, cols] * cur)
            conv_ref[:, cols] = out.astype(BF16)
            return out

        acc = x_ref[...]
        for cj in range(ff // cwid):
            gate = conv(cj * cwid)
            val = conv(ff + cj * cwid)
            a = gate * jax.nn.sigmoid(gate) * val
            acc = acc + jnp.dot(a.astype(BF16), wd_ref[cj * cwid:(cj + 1) * cwid, :], preferred_element_type=F32)
        o_ref[...] = acc

    return pl.pallas_call(
        body, name=name, grid=(n // tm,),
        in_specs=[pl.BlockSpec((tm, 2 * ff), lambda i: (i, 0)),
                  pl.BlockSpec((HALO, 2 * ff), lambda i: (prev(i), 0)),
                  pl.BlockSpec((tm, d), lambda i: (i, 0)),
                  pl.BlockSpec((3, 2 * ff), lambda i: (0, 0)),
                  _resident((ff, d), lambda i: (0, 0))],
        out_specs=[pl.BlockSpec((tm, d), lambda i: (i, 0)), pl.BlockSpec((tm, 2 * ff), lambda i: (i, 0))],
        out_shape=[jax.ShapeDtypeStruct((n, d), F32), jax.ShapeDtypeStruct((n, 2 * ff), BF16)],
        compiler_params=_params("parallel"),
    )(up, up, x, cw, wd)


def _final_loss(x, g, target, *, tm, name):
    n, d = x.shape

    def body(x_ref, g_ref, t_ref, dx_ref, loss_ref, dg_ref):
        @pl.when(pl.program_id(0) == 0)
        def _():
            loss_ref[...] = jnp.zeros_like(loss_ref)
            dg_ref[...] = jnp.zeros_like(dg_ref)

        xf = x_ref[...]
        r = lax.rsqrt(_lanemean(xf * xf) + EPS)
        xhat = xf * r
        diff = xhat * g_ref[...] - t_ref[...]
        loss_ref[...] += (0.5 / d) * _rowsum(jnp.sum(diff * diff, axis=-1, keepdims=True))
        dy = diff * (1.0 / d)
        dg_ref[...] += _rowsum(dy * xhat)
        dyh = dy * g_ref[...]
        dx_ref[...] = r * (dyh - xhat * _lanemean(dyh * xhat))

    return pl.pallas_call(
        body, name=name, grid=(n // tm,),
        in_specs=[pl.BlockSpec((tm, d), lambda i: (i, 0)), pl.BlockSpec((1, d), lambda i: (0, 0)),
                  pl.BlockSpec((tm, d), lambda i: (i, 0))],
        out_specs=[pl.BlockSpec((tm, d), lambda i: (i, 0)), pl.BlockSpec((HALO, CHUNK), lambda i: (0, 0)),
                   pl.BlockSpec((1, d), lambda i: (0, 0))],
        out_shape=[jax.ShapeDtypeStruct((n, d), F32), jax.ShapeDtypeStruct((HALO, CHUNK), F32),
                   jax.ShapeDtypeStruct((1, d), F32)],
        compiler_params=_params("arbitrary"),
    )(x, g, target)


def _alias_args(prev):
    if prev is None:
        return [], []
    return [prev], [pl.BlockSpec(memory_space=pl.ANY)]


def _ffn_bwd(dx, up, conv, cw, wd, dwd_prev, layer, n_layers, *, seq, tm, cwid, name):
    n, d = dx.shape
    ff = wd.shape[0]
    nxt = _halo_next(tm, n)

    def body(dx_ref, dxn_ref, up_ref, conv_ref, nconv_ref, cw_ref, wd_ref, *rest):
        dup_ref, dwd_ref, dcw_ref = rest[-3:]
        i = pl.program_id(0)
        keep_next = jnp.where(((i + 1) * tm) % seq == 0, 0.0, 1.0)

        @pl.when(i == 0)
        def _():
            dwd_ref[...] = jnp.zeros_like(dwd_ref)
            dcw_ref[...] = jnp.zeros_like(dcw_ref)

        dxe = jnp.concatenate([dx_ref[...], dxn_ref[...]], axis=0).astype(BF16)
        dxb = dxe[:tm]
        for cj in range(ff // cwid):
            rows = slice(cj * cwid, (cj + 1) * cwid)
            g_cols, v_cols = slice(cj * cwid, (cj + 1) * cwid), slice(ff + cj * cwid, ff + (cj + 1) * cwid)
            dae = lax.dot_general(dxe, wd_ref[rows, :], NT_DIMS, preferred_element_type=F32)
            da, dan = dae[:tm], dae[tm:]

            def grads(gate, val, da_rows):
                sg = jax.nn.sigmoid(gate)
                sl = gate * sg
                return sl, da_rows * val * sg * (1.0 + gate * (1.0 - sg)), da_rows * sl

            gate, val = conv_ref[:, g_cols].astype(F32), conv_ref[:, v_cols].astype(F32)
            sl, d_gate, d_val = grads(gate, val, da)
            dwd_ref[rows, :] += lax.dot_general((sl * val).astype(BF16), dxb, TN_DIMS, preferred_element_type=F32)
            _, d_gate_n, d_val_n = grads(nconv_ref[:, g_cols].astype(F32), nconv_ref[:, v_cols].astype(F32),
                                         dan * keep_next)
            for cols, dcur, dnext in ((g_cols, d_gate, d_gate_n), (v_cols, d_val, d_val_n)):
                upc = up_ref[:, cols].astype(F32)
                d1, d2 = _shift_next(dcur, dnext, 1), _shift_next(dcur, dnext, 2)
                for k, dk in enumerate((d2, d1, dcur)):
                    dcw_ref[k:k + 1, cols] += _rowsum(dk * upc)
                dup = cw_ref[2:3, cols] * dcur + cw_ref[1:2, cols] * d1 + cw_ref[0:1, cols] * d2
                dup_ref[:, cols] = dup.astype(BF16)

    extra, extra_specs = _alias_args(dwd_prev)
    n_in = 7
    return pl.pallas_call(
        body, name=name, grid=(n // tm,),
        in_specs=[pl.BlockSpec((tm, d), lambda i: (i, 0)),
                  pl.BlockSpec((HALO, d), lambda i: (nxt(i), 0)),
                  pl.BlockSpec((tm, 2 * ff), lambda i: (i, 0)),
                  pl.BlockSpec((tm, 2 * ff), lambda i: (i, 0)),
                  pl.BlockSpec((HALO, 2 * ff), lambda i: (nxt(i), 0)),
                  pl.BlockSpec((3, 2 * ff), lambda i: (0, 0)),
                  _resident((ff, d), lambda i: (0, 0))] + extra_specs,
        out_specs=[pl.BlockSpec((tm, 2 * ff), lambda i: (i, 0)),
                   pl.BlockSpec((None, ff, d), lambda i: (layer, 0, 0)),
                   pl.BlockSpec((3, 2 * ff), lambda i: (0, 0))],
        out_shape=[jax.ShapeDtypeStruct((n, 2 * ff), BF16), jax.ShapeDtypeStruct((n_layers, ff, d), F32),
                   jax.ShapeDtypeStruct((3, 2 * ff), F32)],
        input_output_aliases={n_in: 1} if extra else {},
        compiler_params=_params("arbitrary"),
    )(dx, dx, up, conv, conv, cw, wd, *extra)


def _matmul_bwd_x(dy, wg, x, g, dres, layer, *, tm, name):
    n, d = x.shape
    _, n_chip, _, ck = wg.shape

    def body(dy_ref, w_ref, x_ref, g_ref, dres_ref, dx_ref, dg_ref):
        @pl.when(pl.program_id(0) == 0)
        def _():
            dg_ref[...] = jnp.zeros_like(dg_ref)

        dh = lax.dot_general(dy_ref[:, 0:ck], w_ref[0], NT_DIMS, preferred_element_type=F32)
        for j in range(1, n_chip):
            dh = dh + lax.dot_general(dy_ref[:, j * ck:(j + 1) * ck], w_ref[j], NT_DIMS, preferred_element_type=F32)
        xf = x_ref[...]
        r = lax.rsqrt(_lanemean(xf * xf) + EPS)
        xhat = xf * r
        dg_ref[...] += _rowsum(dh * xhat)
        dyh = dh * g_ref[...]
        dx_ref[...] = dres_ref[...] + r * (dyh - xhat * _lanemean(dyh * xhat))

    return pl.pallas_call(
        body, name=name, grid=(n // tm,),
        in_specs=[pl.BlockSpec((tm, n_chip * ck), lambda i: (i, 0)),
                  _resident((None, n_chip, d, ck), lambda i: (layer, 0, 0, 0)),
                  pl.BlockSpec((tm, d), lambda i: (i, 0)),
                  pl.BlockSpec((1, d), lambda i: (0, 0)),
                  pl.BlockSpec((tm, d), lambda i: (i, 0))],
        out_specs=[pl.BlockSpec((tm, d), lambda i: (i, 0)), pl.BlockSpec((1, d), lambda i: (0, 0))],
        out_shape=[jax.ShapeDtypeStruct((n, d), F32), jax.ShapeDtypeStruct((1, d), F32)],
        compiler_params=_params("arbitrary"),
    )(dy, wg, x, g, dres)


def _matmul_bwd_w(h, dy, dw_prev, layer, n_layers, n_chip, *, tk, name):
    n, d = h.shape
    ck = dy.shape[1] // n_chip

    def body(h_ref, dy_ref, *rest):
        o_ref = rest[-1]

        @pl.when(pl.program_id(1) == 0)
        def _():
            o_ref[...] = jnp.zeros_like(o_ref)

        o_ref[...] += lax.dot_general(h_ref[...], dy_ref[...], TN_DIMS, preferred_element_type=F32)

    extra, extra_specs = _alias_args(dw_prev)
    return pl.pallas_call(
        body, name=name, grid=(n_chip, n // tk),
        in_specs=[pl.BlockSpec((tk, d), lambda j, k: (k, 0)),
                  pl.BlockSpec((tk, ck), lambda j, k: (k, j))] + extra_specs,
        out_specs=pl.BlockSpec((None, None, d, ck), lambda j, k: (layer, j, 0, 0)),
        out_shape=jax.ShapeDtypeStruct((n_layers, n_chip, d, ck), F32),
        input_output_aliases={2: 0} if extra else {},
        compiler_params=_params("parallel", "arbitrary"),
    )(h, dy, *extra)


def _mixer_bwd(dx, proj, cw, lg, lb, ws, bmap, wout, dwout_prev, layer, n_layers, *, seq, tm, name):
    n, d = dx.shape
    groups = ws.shape[0]
    gd = d // groups
    prev, nxt = _halo_prev(tm), _halo_next(tm, n)
    n_tiles = n // tm

    def body(proj_ref, pcg_ref, pxi_ref, nbg_ref, nga_ref, dx_ref, dxn_ref, cw_ref, lg_ref, lb_ref, ws_ref,
             bmap_ref, wout_ref, *rest):
        (dproj_ref, dwout_ref, dcw_ref, dlg_ref, dlb_ref, dws_ref, dbs_ref,
         vn_s, mixed_s, dmix_s, dvn_s, dbmap_s) = rest[-12:]
        i = pl.program_id(0)
        seq_start = (i * tm) % seq == 0
        keep_next = jnp.where(((i + 1) * tm) % seq == 0, 0.0, 1.0)

        @pl.when(i == 0)
        def _():
            for ref in (dwout_ref, dcw_ref, dlg_ref, dlb_ref, dws_ref, dbmap_s):
                ref[...] = jnp.zeros_like(ref)

        def piece(k):
            return proj_ref[:, k * d:(k + 1) * d].astype(F32)

        def put(k, val):
            dproj_ref[:, k * d:(k + 1) * d] = val.astype(BF16)

        w = [cw_ref[k:k + 1, :] for k in range(3)]
        cg, xi = piece(1), piece(2)
        z = cg * xi
        zprev = jnp.where(seq_start, 0.0, pcg_ref[...].astype(F32) * pxi_ref[...].astype(F32))
        z2, z1 = _shift_prev(zprev, z, 2), _shift_prev(zprev, z, 1)
        cz = w[0] * z2 + w[1] * z1 + w[2] * z
        bg = piece(0)
        ya = bg * cz

        v = piece(4)
        xc = v - _lanemean(v)
        rstd = lax.rsqrt(_lanemean(xc * xc) + EPS)
        vhat = xc * rstd
        vn_s[...] = (vhat * lg_ref[...] + lb_ref[...]).astype(BF16)
        tril, wsm = _tril_weights(ws_ref, groups)
        for ck in range(tm // CHUNK):
            rows = slice(ck * CHUNK, (ck + 1) * CHUNK)
            for g in range(groups):
                cols = slice(g * gd, (g + 1) * gd)
                mixed_s[rows, cols] = (jnp.dot(wsm[g].astype(BF16), vn_s[rows, cols], preferred_element_type=F32)
                                       + bmap_ref[:, cols])
        u = piece(3)
        mixed = mixed_s[...]
        yb = u * mixed
        sa, sb = jax.nn.sigmoid(piece(5)), jax.nn.sigmoid(piece(6))
        merged = sa * ya + sb * yb

        dxe = jnp.concatenate([dx_ref[...], dxn_ref[...]], axis=0).astype(BF16)
        dme = lax.dot_general(dxe, wout_ref[...], NT_DIMS, preferred_element_type=F32)
        dm, dm_n = dme[:tm], dme[tm:]
        dwout_ref[...] += lax.dot_general(merged.astype(BF16), dxe[:tm], TN_DIMS, preferred_element_type=F32)

        put(5, dm * ya * sa * (1.0 - sa))
        put(6, dm * yb * sb * (1.0 - sb))
        d_ya, d_yb = dm * sa, dm * sb
        put(0, d_ya * cz)
        d_cz = d_ya * bg
        for k, tap in enumerate((z2, z1, z)):
            dcw_ref[k:k + 1, :] += _rowsum(d_cz * tap)
        d_cz_n = dm_n * jax.nn.sigmoid(nga_ref[...].astype(F32)) * nbg_ref[...].astype(F32) * keep_next
        dz = w[2] * d_cz + w[1] * _shift_next(d_cz, d_cz_n, 1) + w[0] * _shift_next(d_cz, d_cz_n, 2)
        put(1, dz * xi)
        put(2, dz * cg)

        put(3, d_yb * mixed)
        d_mixed = d_yb * u
        dmix_s[...] = d_mixed.astype(BF16)
        for ck in range(tm // CHUNK):
            rows = slice(ck * CHUNK, (ck + 1) * CHUNK)
            dbmap_s[...] += d_mixed[rows, :]
            for g in range(groups):
                cols = slice(g * gd, (g + 1) * gd)
                dvn_s[rows, cols] = jnp.dot(wsm[g].T.astype(BF16), dmix_s[rows, cols], preferred_element_type=F32)
                dws_ref[g] += jnp.where(
                    tril, lax.dot_general(dmix_s[rows, cols], vn_s[rows, cols], NT_DIMS, preferred_element_type=F32),
                    0.0)
        d_vn = dvn_s[...]
        dlg_ref[...] += _rowsum(d_vn * vhat)
        dlb_ref[...] += _rowsum(d_vn)
        d_vhat = d_vn * lg_ref[...]
        put(4, rstd * (d_vhat - _lanemean(d_vhat) - vhat * _lanemean(d_vhat * vhat)))

        @pl.when(i == n_tiles - 1)
        def _():
            for g in range(groups):
                dbs_ref[:, g:g + 1] = jnp.sum(dbmap_s[:, g * gd:(g + 1) * gd], axis=-1, keepdims=True)

    full = lambda shape: pl.BlockSpec(shape, lambda i: (0,) * len(shape))
    extra, extra_specs = _alias_args(dwout_prev)
    n_in = 13
    return pl.pallas_call(
        body, name=name, grid=(n_tiles,),
        in_specs=[pl.BlockSpec((tm, N_PIECES * d), lambda i: (i, 0)),
                  pl.BlockSpec((HALO, d), lambda i: (prev(i), 1)),
                  pl.BlockSpec((HALO, d), lambda i: (prev(i), 2)),
                  pl.BlockSpec((HALO, d), lambda i: (nxt(i), 0)),
                  pl.BlockSpec((HALO, d), lambda i: (nxt(i), 5)),
                  pl.BlockSpec((tm, d), lambda i: (i, 0)),
                  pl.BlockSpec((HALO, d), lambda i: (nxt(i), 0)),
                  full((3, d)), full((1, d)), full((1, d)), full((groups, CHUNK, CHUNK)), full((CHUNK, d)),
                  full((d, d))] + extra_specs,
        out_specs=[pl.BlockSpec((tm, N_PIECES * d), lambda i: (i, 0)),
                   pl.BlockSpec((None, d, d), lambda i: (layer, 0, 0)),
                   full((3, d)), full((1, d)), full((1, d)), full((groups, CHUNK, CHUNK)), full((CHUNK, groups))],
        out_shape=[jax.ShapeDtypeStruct((n, N_PIECES * d), BF16), jax.ShapeDtypeStruct((n_layers, d, d), F32),
                   jax.ShapeDtypeStruct((3, d), F32), jax.ShapeDtypeStruct((1, d), F32),
                   jax.ShapeDtypeStruct((1, d), F32), jax.ShapeDtypeStruct((groups, CHUNK, CHUNK), F32),
                   jax.ShapeDtypeStruct((CHUNK, groups), F32)],
        scratch_shapes=[pltpu.VMEM((tm, d), BF16), pltpu.VMEM((tm, d), F32), pltpu.VMEM((tm, d), BF16),
                        pltpu.VMEM((tm, d), F32), pltpu.VMEM((CHUNK, d), F32)],
        input_output_aliases={n_in: 1} if extra else {},
        compiler_params=_params("arbitrary"),
    )(proj, proj, proj, proj, proj, dx, dx, cw, lg, lb, ws, bmap, wout, *extra)


def _mesh_pos():
    return lax.axis_index("x"), lax.axis_index("y"), lax.axis_index("c")


def _peer_chip(x, y, r):
    return (1 - x if r >> 1 else x), (1 - y if r & 1 else y)


HBM_SPEC = pl.BlockSpec(memory_space=pltpu.HBM)


def _all_gather_weights(shards, name):
    n_w = len(shards)

    def body(*refs):
        ins, outs = refs[:n_w], refs[n_w:2 * n_w]
        ici_send, ici_recv, d2d_send, d2d_recv, own_sem = refs[2 * n_w:]
        x, y, c = _mesh_pos()
        chip = 2 * x + y
        sibling = (x, y, 1 - c)

        def half(a, of_chip, core):
            rh = ins[a].shape[1] // 2
            return outs[a].at[:, of_chip, pl.ds(core * rh, rh), :]

        def copy(a, src, of_chip, core, sems, k, to):
            return pltpu.make_async_remote_copy(src_ref=src, dst_ref=half(a, of_chip, core), send_sem=sems[0].at[k],
                                                recv_sem=sems[1].at[k], device_id=to, device_id_type=MESH_T)

        own = [pltpu.make_async_copy(ins[a], outs[a].at[:, chip], own_sem.at[a]) for a in range(n_w)]
        for cp in own:
            cp.start()
        sends = []
        for a in range(n_w):
            rh = ins[a].shape[1] // 2
            for r in (1, 2, 3):
                px, py = _peer_chip(x, y, r)
                sends.append(copy(a, ins[a].at[:, pl.ds(c * rh, rh), :], chip, c, (ici_send, ici_recv),
                                  3 * a + r - 1, (px, py, c)))
        for cp in sends:
            cp.start()
        passed = []
        for a in range(n_w):
            for r in (1, 2, 3):
                px, py = _peer_chip(x, y, r)
                k, from_chip = 3 * a + r - 1, 2 * px + py
                copy(a, half(a, from_chip, c), from_chip, c, (ici_send, ici_recv), k, sibling).wait_recv()
                fwd = copy(a, half(a, from_chip, c), from_chip, c, (d2d_send, d2d_recv), k, sibling)
                fwd.start()
                passed.append(fwd)
        for a in range(n_w):
            for r in (1, 2, 3):
                px, py = _peer_chip(x, y, r)
                from_chip = 2 * px + py
                copy(a, half(a, from_chip, 1 - c), from_chip, 1 - c, (d2d_send, d2d_recv), 3 * a + r - 1,
                     sibling).wait_recv()
        for cp in sends + passed:
            cp.wait_send()
        for cp in own:
            cp.wait()

    return pl.pallas_call(
        body, name=name,
        in_specs=[HBM_SPEC] * n_w, out_specs=[HBM_SPEC] * n_w,
        out_shape=[jax.ShapeDtypeStruct((s.shape[0], N_CHIP) + s.shape[1:], s.dtype) for s in shards],
        scratch_shapes=[pltpu.SemaphoreType.DMA((3 * n_w,)), pltpu.SemaphoreType.DMA((3 * n_w,)),
                        pltpu.SemaphoreType.DMA((3 * n_w,)), pltpu.SemaphoreType.DMA((3 * n_w,)),
                        pltpu.SemaphoreType.DMA((n_w,))],
    )(*shards)


def _swap_halves(grads, name):
    n_w = len(grads)

    def body(*refs):
        ins, outs = refs[:n_w], refs[n_w:2 * n_w]
        send_sem, recv_sem = refs[2 * n_w:]
        x, y, c = _mesh_pos()
        copies = []
        for a in range(n_w):
            rh = ins[a].shape[2] // 2
            copies.append(pltpu.make_async_remote_copy(
                src_ref=ins[a].at[:, :, pl.ds((1 - c) * rh, rh), :], dst_ref=outs[a], send_sem=send_sem.at[a],
                recv_sem=recv_sem.at[a], device_id=(x, y, 1 - c), device_id_type=MESH_T))
        for cp in copies:
            cp.start()
        for cp in copies:
            cp.wait()

    return pl.pallas_call(
        body, name=name,
        in_specs=[HBM_SPEC] * n_w, out_specs=[HBM_SPEC] * n_w,
        out_shape=[jax.ShapeDtypeStruct(g.shape[:2] + (g.shape[2] // 2, g.shape[3]), g.dtype) for g in grads],
        scratch_shapes=[pltpu.SemaphoreType.DMA((n_w,)), pltpu.SemaphoreType.DMA((n_w,))],
    )(*grads)


def _scatter_to_chips(parts, name):
    n_w = len(parts)

    def body(*refs):
        ins, outs = refs[:n_w], refs[n_w:2 * n_w]
        send_sem, recv_sem = refs[2 * n_w:]
        x, y, c = _mesh_pos()
        copies = []
        for a in range(n_w):
            for r in (1, 2, 3):
                px, py = _peer_chip(x, y, r)
                copies.append(pltpu.make_async_remote_copy(
                    src_ref=ins[a].at[:, 2 * px + py], dst_ref=outs[a].at[:, r - 1], send_sem=send_sem.at[3 * a + r - 1],
                    recv_sem=recv_sem.at[3 * a + r - 1], device_id=(px, py, c), device_id_type=MESH_T))
        for cp in copies:
            cp.start()
        for cp in copies:
            cp.wait()

    return pl.pallas_call(
        body, name=name,
        in_specs=[HBM_SPEC] * n_w, out_specs=[HBM_SPEC] * n_w,
        out_shape=[jax.ShapeDtypeStruct((p.shape[0], 3) + p.shape[2:], p.dtype) for p in parts],
        scratch_shapes=[pltpu.SemaphoreType.DMA((3 * n_w,)), pltpu.SemaphoreType.DMA((3 * n_w,))],
    )(*parts)


def _share_halves(gfull, name):
    n_w = len(gfull)

    def body(*refs):
        outs = refs[n_w:2 * n_w]
        send_sem, recv_sem = refs[2 * n_w:]
        x, y, c = _mesh_pos()
        sends, recvs = [], []
        for a in range(n_w):
            rh = outs[a].shape[1] // 2
            mine, theirs = outs[a].at[:, pl.ds(c * rh, rh), :], outs[a].at[:, pl.ds((1 - c) * rh, rh), :]
            sends.append(pltpu.make_async_remote_copy(
                src_ref=mine, dst_ref=mine, send_sem=send_sem.at[a], recv_sem=recv_sem.at[a],
                device_id=(x, y, 1 - c), device_id_type=MESH_T))
            recvs.append(pltpu.make_async_remote_copy(
                src_ref=theirs, dst_ref=theirs, send_sem=send_sem.at[a], recv_sem=recv_sem.at[a],
                device_id=(x, y, 1 - c), device_id_type=MESH_T))
        for cp in sends:
            cp.start()
        for cp in sends:
            cp.wait_send()
        for cp in recvs:
            cp.wait_recv()

    return pl.pallas_call(
        body, name=name,
        in_specs=[HBM_SPEC] * n_w, out_specs=[HBM_SPEC] * n_w,
        out_shape=[jax.ShapeDtypeStruct(g.shape, g.dtype) for g in gfull],
        input_output_aliases={a: a for a in range(n_w)},
        scratch_shapes=[pltpu.SemaphoreType.DMA((n_w,)), pltpu.SemaphoreType.DMA((n_w,))],
    )(*gfull)


def _gather_small(pack, name):
    def body(p_ref, o_ref, send_sem, recv_sem):
        x, y, c = _mesh_pos()
        chip = 2 * x + y
        o_ref[chip] = p_ref[...]
        copies = []
        for r in (1, 2, 3):
            px, py = _peer_chip(x, y, r)
            copies.append(pltpu.make_async_remote_copy(
                src_ref=p_ref, dst_ref=o_ref.at[chip], send_sem=send_sem.at[r - 1], recv_sem=recv_sem.at[r - 1],
                device_id=(px, py, c), device_id_type=MESH_T))
        for cp in copies:
            cp.start()
        for r, cp in zip((1, 2, 3), copies):
            px, py = _peer_chip(x, y, r)
            landed = o_ref.at[2 * px + py]
            pltpu.make_async_remote_copy(src_ref=landed, dst_ref=landed, send_sem=send_sem.at[r - 1],
                                         recv_sem=recv_sem.at[r - 1], device_id=(px, py, c),
                                         device_id_type=MESH_T).wait_recv()
            cp.wait_send()

    vmem = pl.BlockSpec(memory_space=pltpu.VMEM)
    return pl.pallas_call(
        body, name=name, in_specs=[vmem], out_specs=vmem,
        out_shape=jax.ShapeDtypeStruct((N_CHIP,) + pack.shape, pack.dtype),
        scratch_shapes=[pltpu.SemaphoreType.DMA((3,)), pltpu.SemaphoreType.DMA((3,))],
    )(pack)


def _all_reduce_small(pack, name):
    rows = pack.shape[0]

    def body(p_ref, o_ref, sib_buf, chip_buf, send_sem, recv_sem):
        x, y, c = _mesh_pos()
        chip = 2 * x + y
        swap = pltpu.make_async_remote_copy(src_ref=p_ref, dst_ref=sib_buf, send_sem=send_sem.at[0],
                                            recv_sem=recv_sem.at[0], device_id=(x, y, 1 - c), device_id_type=MESH_T)
        swap.start()
        swap.wait()
        chip_buf[chip] = p_ref[...] + sib_buf[...]
        copies = []
        for r in (1, 2, 3):
            px, py = _peer_chip(x, y, r)
            copies.append(pltpu.make_async_remote_copy(
                src_ref=chip_buf.at[chip], dst_ref=chip_buf.at[chip], send_sem=send_sem.at[r], recv_sem=recv_sem.at[r],
                device_id=(px, py, c), device_id_type=MESH_T))
        for cp in copies:
            cp.start()
        for r, cp in zip((1, 2, 3), copies):
            px, py = _peer_chip(x, y, r)
            landed = chip_buf.at[2 * px + py]
            pltpu.make_async_remote_copy(src_ref=landed, dst_ref=landed, send_sem=send_sem.at[r], recv_sem=recv_sem.at[r],
                                         device_id=(px, py, c), device_id_type=MESH_T).wait_recv()
            cp.wait_send()
        o_ref[...] = ((chip_buf[0] + chip_buf[1]) + chip_buf[2]) + chip_buf[3]

    vmem = pl.BlockSpec(memory_space=pltpu.VMEM)
    return pl.pallas_call(
        body, name=name, in_specs=[vmem], out_specs=vmem,
        out_shape=jax.ShapeDtypeStruct(pack.shape, pack.dtype),
        scratch_shapes=[pltpu.VMEM((rows, 128), F32), pltpu.VMEM((N_CHIP, rows, 128), F32),
                        pltpu.SemaphoreType.DMA((4,)), pltpu.SemaphoreType.DMA((4,))],
        compiler_params=pltpu.CompilerParams(vmem_limit_bytes=VMEM_LIMIT_V7X),
    )(pack)


def _block_rows(rows, cols, itemsize=4, budget=2 << 20):
    best = None
    for t in range(16, rows + 1, 16):
        if rows % t == 0 and t * cols * itemsize <= budget:
            best = t
    assert best is not None, (rows, cols)
    return best


def _pair_sum(grad, recv, core, name):
    n_l, n_chip, rk, ck = grad.shape
    rh = rk // 2
    tr = _block_rows(rh, ck)
    nb = rh // tr

    def body(core_ref, g_ref, r_ref, o_ref):
        o_ref[...] = (g_ref[...] + r_ref[...]).astype(BF16)

    blk = (None, None, tr, ck)
    return pl.pallas_call(
        body, name=name,
        grid_spec=pltpu.PrefetchScalarGridSpec(
            num_scalar_prefetch=1, grid=(n_l, n_chip, nb),
            in_specs=[pl.BlockSpec(blk, lambda l, k, i, core_ref: (l, k, core_ref[0] * nb + i, 0)),
                      pl.BlockSpec(blk, lambda l, k, i, core_ref: (l, k, i, 0))],
            out_specs=pl.BlockSpec(blk, lambda l, k, i, core_ref: (l, k, i, 0))),
        out_shape=jax.ShapeDtypeStruct((n_l, n_chip, rh, ck), BF16),
        compiler_params=_params("parallel", "parallel", "parallel"),
    )(core, grad, recv)


def _final_sum(grad, recv, arrived, where, name):
    n_l, n_chip, rk, ck = grad.shape
    rh = rk // 2
    tr = _block_rows(rh, ck)
    nb = rh // tr

    def body(where_ref, g_ref, r_ref, a1_ref, a2_ref, a3_ref, o_ref):
        own = g_ref[...] + r_ref[...]
        o_ref[...] = ((own + a1_ref[...].astype(F32)) + a2_ref[...].astype(F32)) + a3_ref[...].astype(F32)

    blk = (None, None, tr, ck)
    slot = lambda r: pl.BlockSpec(blk, lambda l, i, w: (l, r, i, 0))
    return pl.pallas_call(
        body, name=name,
        grid_spec=pltpu.PrefetchScalarGridSpec(
            num_scalar_prefetch=1, grid=(n_l, nb),
            in_specs=[pl.BlockSpec(blk, lambda l, i, w: (l, w[1], w[0] * nb + i, 0)),
                      pl.BlockSpec(blk, lambda l, i, w: (l, w[1], i, 0)),
                      slot(0), slot(1), slot(2)],
            out_specs=pl.BlockSpec((None, tr, ck), lambda l, i, w: (l, w[0] * nb + i, 0))),
        out_shape=jax.ShapeDtypeStruct((n_l, rk, ck), F32),
        compiler_params=_params("parallel", "parallel"),
    )(where, grad, recv, arrived, arrived, arrived)


def _adamw_math(w, g, m, v):
    m = ADAM_B1 * m + (1.0 - ADAM_B1) * g
    v = ADAM_B2 * v + (1.0 - ADAM_B2) * (g * g)
    m_hat = m / (1.0 - ADAM_B1 ** ADAM_STEP)
    v_hat = v / (1.0 - ADAM_B2 ** ADAM_STEP)
    delta = -ADAM_LR * (m_hat / (jnp.sqrt(v_hat) + ADAM_EPS) + ADAM_WD * w)
    return delta, m, v


def _adamw_big(w, g, m, v, name):
    n_l, rk, ck = w.shape
    tr = _block_rows(rk, ck, budget=1 << 20)

    def body(w_ref, g_ref, m_ref, v_ref, d_ref, nm_ref, nv_ref):
        d_ref[...], nm_ref[...], nv_ref[...] = _adamw_math(w_ref[...], g_ref[...], m_ref[...], v_ref[...])

    spec = pl.BlockSpec((None, tr, ck), lambda l, i: (l, i, 0))
    return pl.pallas_call(
        body, name=name, grid=(n_l, rk // tr), in_specs=[spec] * 4, out_specs=[spec] * 3,
        out_shape=[jax.ShapeDtypeStruct(w.shape, F32)] * 3,
        compiler_params=_params("parallel", "parallel"),
    )(w, g, m, v)


def _adamw_small(ws, gs, ms, vs, name):
    n_p = len(ws)

    def body(*refs):
        ins, outs = refs[:4 * n_p], refs[4 * n_p:]
        for p in range(n_p):
            res = _adamw_math(ins[p][...], ins[n_p + p][...], ins[2 * n_p + p][...], ins[3 * n_p + p][...])
            for q in range(3):
                outs[q * n_p + p][...] = res[q]

    vmem = pl.BlockSpec(memory_space=pltpu.VMEM)
    outs = pl.pallas_call(
        body, name=name, in_specs=[vmem] * (4 * n_p), out_specs=[vmem] * (3 * n_p),
        out_shape=[jax.ShapeDtypeStruct(w.shape, F32) for w in ws] * 3,
        compiler_params=pltpu.CompilerParams(vmem_limit_bytes=VMEM_LIMIT_V7X),
    )(*ws, *gs, *ms, *vs)
    return outs[:n_p], outs[n_p:2 * n_p], outs[2 * n_p:]


def kernel(x, mix_norm_g, w_in, conv_a_w, ln_v_g, ln_v_b, w_s, b_s, w_out, ffn_norm_g, w_up, conv_ffn_w, w_down, final_norm_g, loss_target, m_mix_norm_g, m_w_in, m_conv_a_w, m_ln_v_g, m_ln_v_b, m_w_s, m_b_s, m_w_out, m_ffn_norm_g, m_w_up, m_conv_ffn_w, m_w_down, m_final_norm_g, v_mix_norm_g, v_w_in, v_conv_a_w, v_ln_v_g, v_ln_v_b, v_w_s, v_b_s, v_w_out, v_ffn_norm_g, v_w_up, v_conv_ffn_w, v_w_down, v_final_norm_g):
    bsz, seq, d = x.shape
    n = bsz * seq
    n_l, groups = w_s.shape[0], w_s.shape[1]
    gd = d // groups
    ff = w_down.shape[1] * N_CHIP
    axes = ("x", "y", "c")
    mx, my, mc = _mesh_pos()
    chip = 2 * mx + my
    core = jnp.reshape(mc, (1,)).astype(jnp.int32)
    where = jnp.stack([mc, chip]).astype(jnp.int32)

    tm_mm = _row_tile(seq, 512)
    tm_ew = _row_tile(seq, 256)
    tk_w = _row_tile(seq, 1024)
    ff_chunk = ff // 2 if (ff // 2) % 128 == 0 else ff

    big = [w_in, w_out, w_up, w_down]
    w_in_g, w_out_g, w_up_g, w_down_g = _all_gather_weights([w.astype(BF16) for w in big], "gather_weights")
    w_out_f = w_out_g.reshape(n_l, d, d)
    w_down_f = w_down_g.reshape(n_l, ff, d)

    taps = jnp.concatenate([conv_a_w.reshape(n_l, -1), conv_ffn_w.reshape(n_l, -1)], axis=1)
    tap_rows = -(-taps.size // 128 // 8) * 8
    tap_pack = jnp.zeros((tap_rows * 128,), F32).at[:taps.size].set(taps.reshape(-1)).reshape(tap_rows, 128)
    tap_all = _gather_small(tap_pack, "gather_taps")
    tap_all = tap_all.reshape(N_CHIP, -1)[:, :taps.size].reshape(N_CHIP, n_l, -1)
    ca = tap_all[:, :, :3 * d // N_CHIP].reshape(N_CHIP, n_l, 3, d // N_CHIP)
    cf = tap_all[:, :, 3 * d // N_CHIP:].reshape(N_CHIP, n_l, 3, 2 * ff // N_CHIP)
    conv_a_full = jnp.transpose(ca, (1, 2, 0, 3)).reshape(n_l, 3, d)
    conv_f_full = jnp.transpose(cf, (1, 2, 0, 3)).reshape(n_l, 3, 2 * ff)

    bmaps = jnp.repeat(jnp.swapaxes(b_s, 1, 2), gd, axis=2)

    xs = x.reshape(n, d)
    tgt = loss_target.reshape(n, d)

    saved = []
    cur = xs
    for l in range(n_l):
        proj, h1 = _norm_matmul(cur, mix_norm_g[l][None], w_in_g, l, tm=tm_mm, name=f"fwd_in_proj_{l}")
        x1 = _mixer_fwd(proj, cur, conv_a_full[l], ln_v_g[l][None], ln_v_b[l][None], w_s[l], bmaps[l], w_out_f[l],
                        seq=seq, tm=tm_ew, name=f"fwd_mixer_{l}")
        up, h2 = _norm_matmul(x1, ffn_norm_g[l][None], w_up_g, l, tm=tm_mm, name=f"fwd_up_proj_{l}")
        x2, conv = _ffn_fwd(up, x1, conv_f_full[l], w_down_f[l], seq=seq, tm=tm_ew, cwid=ff_chunk,
                            name=f"fwd_ffn_{l}")
        saved.append((cur, h1, proj, x1, h2, up, conv))
        cur = x2
    dx, loss_tile, d_final_g = _final_loss(cur, final_norm_g[None], tgt, tm=tm_mm, name="final_loss")
    loss = lax.psum(loss_tile[0, 0], axes)

    dw_in = dw_out = dw_up = dw_down = None
    small = [None] * n_l
    for l in reversed(range(n_l)):
        x0, h1, proj, x1, h2, up, conv = saved[l]
        d_up, dw_down, d_cf = _ffn_bwd(dx, up, conv, conv_f_full[l], w_down_f[l], dw_down, l, n_l, seq=seq, tm=tm_ew,
                                       cwid=ff_chunk, name=f"bwd_ffn_{l}")
        dx1, d_g2 = _matmul_bwd_x(d_up, w_up_g, x1, ffn_norm_g[l][None], dx, l, tm=tm_mm, name=f"bwd_up_x_{l}")
        dw_up = _matmul_bwd_w(h2, d_up, dw_up, l, n_l, N_CHIP, tk=tk_w, name=f"bwd_up_w_{l}")
        d_proj, dw_out, d_ca, d_lg, d_lb, d_ws, d_bs = _mixer_bwd(
            dx1, proj, conv_a_full[l], ln_v_g[l][None], ln_v_b[l][None], w_s[l], bmaps[l], w_out_f[l], dw_out, l, n_l,
            seq=seq, tm=tm_ew, name=f"bwd_mixer_{l}")
        dx, d_g1 = _matmul_bwd_x(d_proj, w_in_g, x0, mix_norm_g[l][None], dx1, l, tm=tm_mm, name=f"bwd_in_x_{l}")
        dw_in = _matmul_bwd_w(h1, d_proj, dw_in, l, n_l, N_CHIP, tk=tk_w, name=f"bwd_in_w_{l}")
        small[l] = [d_g1, d_ca, d_lg, d_lb, d_ws, d_bs.T, d_g2, d_cf]
    grad_x = dx.reshape(bsz, seq, d)

    grads = [dw_in, dw_out.reshape(n_l, N_CHIP, d // N_CHIP, d), dw_up, dw_down.reshape(n_l, N_CHIP, ff // N_CHIP, d)]
    from_sib = _swap_halves(grads, "grad_swap_halves")
    parts = [_pair_sum(g, r, core, f"grad_pair_sum_{a}") for a, (g, r) in enumerate(zip(grads, from_sib))]
    arrived = _scatter_to_chips(parts, "grad_scatter")
    halves = [_final_sum(g, r, p, where, f"grad_final_sum_{a}")
              for a, (g, r, p) in enumerate(zip(grads, from_sib, arrived))]
    g_in, g_out, g_up, g_down = _share_halves(halves, "grad_share_halves")

    flat = [a.reshape(-1) for l in range(n_l) for a in small[l]] + [d_final_g.reshape(-1)]
    sizes = [a.size for a in flat]
    total = sum(sizes)
    rows = -(-total // 128 // 8) * 8
    pack = jnp.concatenate(flat + [jnp.zeros((rows * 128 - total,), F32)]).reshape(rows, 128)
    red = _all_reduce_small(pack, "small_all_reduce").reshape(-1)
    pieces, off = [], 0
    for s in sizes:
        pieces.append(red[off:off + s])
        off += s
    per_layer = [pieces[l * 8:(l + 1) * 8] for l in range(n_l)]

    def stacked(idx, shape):
        return jnp.stack([per_layer[l][idx].reshape(shape) for l in range(n_l)])

    def my_cols(a, width):
        return lax.dynamic_slice_in_dim(a, chip * width, width, axis=-1)

    g_mix_norm = stacked(0, (d,))
    g_conv_a = my_cols(stacked(1, (3, d)), d // N_CHIP)
    g_ln_g = stacked(2, (d,))
    g_ln_b = stacked(3, (d,))
    g_ws = stacked(4, (groups, CHUNK, CHUNK))
    g_bs = stacked(5, (groups, CHUNK))
    g_ffn_norm = stacked(6, (d,))
    g_conv_f = my_cols(stacked(7, (3, 2 * ff)), 2 * ff // N_CHIP)
    g_final = pieces[-1].reshape(1, d)

    big_names = [(w_in, g_in, m_w_in, v_w_in), (w_out, g_out, m_w_out, v_w_out), (w_up, g_up, m_w_up, v_w_up),
                 (w_down, g_down, m_w_down, v_w_down)]
    upd = [_adamw_big(w, g, m, v, f"adamw_{a}") for a, (w, g, m, v) in enumerate(big_names)]

    sm_w = [mix_norm_g, conv_a_w, ln_v_g, ln_v_b, w_s, b_s, ffn_norm_g, conv_ffn_w, final_norm_g[None]]
    sm_g = [g_mix_norm, g_conv_a, g_ln_g, g_ln_b, g_ws, g_bs, g_ffn_norm, g_conv_f, g_final]
    sm_m = [m_mix_norm_g, m_conv_a_w, m_ln_v_g, m_ln_v_b, m_w_s, m_b_s, m_ffn_norm_g, m_conv_ffn_w, m_final_norm_g[None]]
    sm_v = [v_mix_norm_g, v_conv_a_w, v_ln_v_g, v_ln_v_b, v_w_s, v_b_s, v_ffn_norm_g, v_conv_ffn_w, v_final_norm_g[None]]
    sm_d, sm_nm, sm_nv = _adamw_small(sm_w, sm_g, sm_m, sm_v, "adamw_small")

    def ordered(sm, bigs):
        return [sm[0], bigs[0], sm[1], sm[2], sm[3], sm[4], sm[5], bigs[1], sm[6], bigs[2], sm[7], bigs[3],
                sm[8].reshape(d)]

    out_g = ordered(sm_g, [g_in, g_out, g_up, g_down])
    out_d = ordered(sm_d, [u[0] for u in upd])
    out_m = ordered(sm_nm, [u[1] for u in upd])
    out_v = ordered(sm_nv, [u[2] for u in upd])
    return (loss, grad_x, *out_g, *out_d, *out_m, *out_v)
```

```python
import jax
import jax.numpy as jnp
from jax import lax
from jax.experimental import pallas as pl
from jax.experimental.pallas import tpu as pltpu

F32 = jnp.float32
BF16 = jnp.bfloat16
EPS = 1e-6
CHUNK = 128
N_CHIP = 4
HALO = 8
N_PIECES = 7
VMEM_LIMIT_V7X = 56 * 1024 * 1024
MESH_T = pl.DeviceIdType.MESH
HBM_SPEC = pl.BlockSpec(memory_space=pltpu.HBM)
VMEM_SPEC = pl.BlockSpec(memory_space=pltpu.VMEM)

ADAM_LR, ADAM_B1, ADAM_B2, ADAM_EPS, ADAM_WD, ADAM_STEP = 0.001, 0.9, 0.999, 1e-08, 0.01, 10

NT_DIMS = (((1,), (1,)), ((), ()))
TN_DIMS = (((0,), (0,)), ((), ()))


def _resident(block_shape, index_map):
    return pl.BlockSpec(block_shape, index_map, pipeline_mode=pl.Buffered(1))


def _full(shape):
    return pl.BlockSpec(shape, lambda *_: (0,) * len(shape))


def _row_tile(seq, want):
    t = min(seq, want)
    assert seq % t == 0 and t % CHUNK == 0, (seq, want)
    return t


def _shift_prev(prev8, cur, k):
    ext = jnp.concatenate([prev8, cur], axis=0)
    return pltpu.roll(ext, k, 0)[HALO:]


def _shift_next(cur, next8, k):
    ext = jnp.concatenate([cur, next8], axis=0)
    n = ext.shape[0]
    return pltpu.roll(ext, n - k, 0)[:n - HALO]


def _rowsum(a):
    return jnp.sum(a, axis=0, keepdims=True)


def _lanemean(a):
    return jnp.mean(a, axis=-1, keepdims=True)


def _tril_weights(ws_ref, groups):
    r = lax.broadcasted_iota(jnp.int32, (CHUNK, CHUNK), 0)
    c = lax.broadcasted_iota(jnp.int32, (CHUNK, CHUNK), 1)
    tril = r >= c
    return tril, [jnp.where(tril, ws_ref[g], 0.0) for g in range(groups)]


def _halo_prev(tm):
    return lambda i: jnp.maximum(i * (tm // HALO) - 1, 0)


def _halo_next(tm, n):
    last = n // HALO - 1
    return lambda i: jnp.minimum((i + 1) * (tm // HALO), last)


def _mesh_pos():
    return lax.axis_index("x"), lax.axis_index("y"), lax.axis_index("c")


def _peer_chip(x, y, r):
    return (1 - x if r >> 1 else x), (1 - y if r & 1 else y)


def _remote(src, dst, sems, k, to):
    return pltpu.make_async_remote_copy(src_ref=src, dst_ref=dst, send_sem=sems[0].at[k], recv_sem=sems[1].at[k],
                                        device_id=to, device_id_type=MESH_T)


class _GatherJob:
    def __init__(self, pieces):
        self.pieces = pieces
        self.ins = [p[0] for p in pieces]
        self.out_shapes = [jax.ShapeDtypeStruct((N_CHIP,) + p[0].shape[1:], p[0].dtype) for p in pieces]
        self.aliases = {}
        n = len(pieces)
        self.sems = [pltpu.SemaphoreType.DMA((3 * n,))] * 4 + [pltpu.SemaphoreType.DMA((n,))]

    def _half(self, outs, a, of_chip, core):
        rh = outs[a].shape[1] // 2
        return outs[a].at[of_chip, pl.ds(core * rh, rh), :]

    def _own(self, ins, outs, sems, chip):
        return [pltpu.make_async_copy(ins[a].at[layer], outs[a].at[chip], sems[4].at[a])
                for a, (_, layer) in enumerate(self.pieces)]

    def _sends(self, ins, outs, sems, x, y, c):
        chip, out = 2 * x + y, []
        for a, (_, layer) in enumerate(self.pieces):
            rh = outs[a].shape[1] // 2
            for r in (1, 2, 3):
                px, py = _peer_chip(x, y, r)
                out.append(_remote(ins[a].at[layer, pl.ds(c * rh, rh), :], self._half(outs, a, chip, c), sems[0:2],
                                   3 * a + r - 1, (px, py, c)))
        return out

    def _passes(self, outs, sems, x, y, c, core):
        out = []
        for a in range(len(self.pieces)):
            for r in (1, 2, 3):
                px, py = _peer_chip(x, y, r)
                landed = self._half(outs, a, 2 * px + py, core)
                out.append(_remote(landed, landed, sems[2:4], 3 * a + r - 1, (x, y, 1 - c)))
        return out

    def start(self, ins, outs, sems):
        x, y, c = _mesh_pos()
        for cp in self._own(ins, outs, sems, 2 * x + y) + self._sends(ins, outs, sems, x, y, c):
            cp.start()

    def finish(self, ins, outs, sems):
        x, y, c = _mesh_pos()
        sends = self._sends(ins, outs, sems, x, y, c)
        passes = self._passes(outs, sems, x, y, c, c)
        k = 0
        for a in range(len(self.pieces)):
            for r in (1, 2, 3):
                px, py = _peer_chip(x, y, r)
                landed = self._half(outs, a, 2 * px + py, c)
                _remote(landed, landed, sems[0:2], k, (px, py, c)).wait_recv()
                passes[k].start()
                k += 1
        for cp in self._passes(outs, sems, x, y, c, 1 - c):
            cp.wait_recv()
        for cp in sends + passes:
            cp.wait_send()
        for cp in self._own(ins, outs, sems, 2 * x + y):
            cp.wait()


class _SwapJob:
    def __init__(self, pieces):
        self.ins = list(pieces)
        self.out_shapes = [jax.ShapeDtypeStruct((g.shape[0], g.shape[1] // 2, g.shape[2]), g.dtype) for g in pieces]
        self.aliases = {}
        self.sems = [pltpu.SemaphoreType.DMA((len(pieces),))] * 2

    def _copies(self, ins, outs, sems):
        x, y, c = _mesh_pos()
        out = []
        for a in range(len(ins)):
            rh = ins[a].shape[1] // 2
            out.append(_remote(ins[a].at[:, pl.ds((1 - c) * rh, rh), :], outs[a], sems, a, (x, y, 1 - c)))
        return out

    def start(self, ins, outs, sems):
        for cp in self._copies(ins, outs, sems):
            cp.start()

    def finish(self, ins, outs, sems):
        for cp in self._copies(ins, outs, sems):
            cp.wait()


class _ScatterJob:
    def __init__(self, pieces):
        self.ins = list(pieces)
        self.out_shapes = [jax.ShapeDtypeStruct((3,) + p.shape[1:], p.dtype) for p in pieces]
        self.aliases = {}
        self.sems = [pltpu.SemaphoreType.DMA((3 * len(pieces),))] * 2

    def _copies(self, ins, outs, sems):
        x, y, c = _mesh_pos()
        out = []
        for a in range(len(ins)):
            for r in (1, 2, 3):
                px, py = _peer_chip(x, y, r)
                out.append(_remote(ins[a].at[2 * px + py], outs[a].at[r - 1], sems, 3 * a + r - 1, (px, py, c)))
        return out

    def start(self, ins, outs, sems):
        for cp in self._copies(ins, outs, sems):
            cp.start()

    def finish(self, ins, outs, sems):
        for cp in self._copies(ins, outs, sems):
            cp.wait()


class _ShareJob:
    def __init__(self, pieces):
        self.pieces = pieces
        self.ins = [p[0] for p in pieces]
        self.out_shapes = [jax.ShapeDtypeStruct(p[0].shape, p[0].dtype) for p in pieces]
        self.aliases = {a: a for a in range(len(pieces))}
        self.sems = [pltpu.SemaphoreType.DMA((len(pieces),))] * 2

    def _copies(self, outs, sems, core):
        x, y, c = _mesh_pos()
        out = []
        for a, (_, layer) in enumerate(self.pieces):
            rh = outs[a].shape[1] // 2
            rows = outs[a].at[layer, pl.ds(core * rh, rh), :]
            out.append(_remote(rows, rows, sems, a, (x, y, 1 - c)))
        return out

    def start(self, ins, outs, sems):
        for cp in self._copies(outs, sems, lax.axis_index("c")):
            cp.start()

    def finish(self, ins, outs, sems):
        c = lax.axis_index("c")
        for cp in self._copies(outs, sems, c):
            cp.wait_send()
        for cp in self._copies(outs, sems, 1 - c):
            cp.wait_recv()


def _hosted_call(main, *, name, grid, in_specs, out_specs, out_shape, scratch_shapes=(), jobs=(), semantics=None,
                 operands=()):
    n_in, n_out, n_sc = len(in_specs), len(out_specs), len(scratch_shapes)
    counts = [(len(j.ins), len(j.out_shapes), len(j.sems)) for j in jobs]
    j_in, j_out, j_sc = (sum(c[k] for c in counts) for k in range(3))
    aliases, i0, o0 = {}, n_in, n_out
    for j, (ci, co, _) in zip(jobs, counts):
        aliases.update({i0 + a: o0 + b for a, b in j.aliases.items()})
        i0, o0 = i0 + ci, o0 + co

    def body(*refs):
        cuts = [0, n_in, n_in + j_in, n_in + j_in + n_out, n_in + j_in + n_out + j_out,
                n_in + j_in + n_out + j_out + n_sc, len(refs)]
        m_in, jb_in, m_out, jb_out, m_sc, jb_sc = (list(refs[cuts[k]:cuts[k + 1]]) for k in range(6))

        def run(phase):
            i0 = o0 = s0 = 0
            for j, (ci, co, cs) in zip(jobs, counts):
                getattr(j, phase)(jb_in[i0:i0 + ci], jb_out[o0:o0 + co], jb_sc[s0:s0 + cs])
                i0, o0, s0 = i0 + ci, o0 + co, s0 + cs

        first = functools_reduce_and([pl.program_id(ax) == 0 for ax in range(len(grid))])
        last = functools_reduce_and([pl.program_id(ax) == grid[ax] - 1 for ax in range(len(grid))])
        if jobs:
            pl.when(first)(lambda: run("start"))
        main(m_in, m_out, m_sc)
        if jobs:
            pl.when(last)(lambda: run("finish"))

    if semantics is None or jobs:
        semantics = ("arbitrary",) * len(grid)
    outs = pl.pallas_call(
        body, name=name, grid=grid,
        in_specs=list(in_specs) + [HBM_SPEC] * j_in,
        out_specs=list(out_specs) + [HBM_SPEC] * j_out,
        out_shape=list(out_shape) + [s for j in jobs for s in j.out_shapes],
        scratch_shapes=list(scratch_shapes) + [s for j in jobs for s in j.sems],
        input_output_aliases=aliases,
        compiler_params=pltpu.CompilerParams(dimension_semantics=semantics, vmem_limit_bytes=VMEM_LIMIT_V7X),
    )(*operands, *[a for j in jobs for a in j.ins])
    main_outs, rest, job_outs = list(outs[:n_out]), list(outs[n_out:]), []
    for _, co, _ in counts:
        job_outs.append(rest[:co])
        rest = rest[co:]
    return main_outs, job_outs


def functools_reduce_and(conds):
    out = conds[0]
    for c in conds[1:]:
        out = jnp.logical_and(out, c)
    return out


def _comm_only(jobs, name):
    return _hosted_call(lambda i, o, s: None, name=name, grid=(1,), in_specs=[], out_specs=[], out_shape=[],
                        jobs=jobs)[1]


def _norm_matmul(x, g, wg, *, tm, name, jobs=()):
    n, d = x.shape
    n_chip, _, ck = wg.shape

    def main(ins, outs, _):
        x_ref, g_ref, w_ref = ins
        o_ref, h_ref = outs
        xf = x_ref[...]
        r = lax.rsqrt(_lanemean(xf * xf) + EPS)
        h_ref[...] = (xf * r * g_ref[...]).astype(BF16)
        for j in range(n_chip):
            o_ref[:, j * ck:(j + 1) * ck] = jnp.dot(h_ref[...], w_ref[j], preferred_element_type=F32).astype(BF16)

    return _hosted_call(
        main, name=name, grid=(n // tm,), jobs=jobs, semantics=("parallel",), operands=(x, g, wg),
        in_specs=[pl.BlockSpec((tm, d), lambda i: (i, 0)), _full((1, d)), _resident((n_chip, d, ck), lambda i: (0, 0, 0))],
        out_specs=[pl.BlockSpec((tm, n_chip * ck), lambda i: (i, 0)), pl.BlockSpec((tm, d), lambda i: (i, 0))],
        out_shape=[jax.ShapeDtypeStruct((n, n_chip * ck), BF16), jax.ShapeDtypeStruct((n, d), BF16)])


def _mixer_fwd(proj, x, cw, lg, lb, ws, bmap, wout, *, seq, tm, name):
    n, d = x.shape
    groups = ws.shape[0]
    gd = d // groups
    prev = _halo_prev(tm)

    def body(proj_ref, pcg_ref, pxi_ref, x_ref, cw_ref, lg_ref, lb_ref, ws_ref, bmap_ref, wout_ref, o_ref,
             vn_s, mixed_s):
        seq_start = (pl.program_id(0) * tm) % seq == 0

        def piece(k):
            return proj_ref[:, k * d:(k + 1) * d].astype(F32)

        z = piece(1) * piece(2)
        zprev = jnp.where(seq_start, 0.0, pcg_ref[...].astype(F32) * pxi_ref[...].astype(F32))
        cz = (cw_ref[0:1, :] * _shift_prev(zprev, z, 2) + cw_ref[1:2, :] * _shift_prev(zprev, z, 1)
              + cw_ref[2:3, :] * z)
        ya = piece(0) * cz

        v = piece(4)
        xc = v - _lanemean(v)
        vn = xc * lax.rsqrt(_lanemean(xc * xc) + EPS) * lg_ref[...] + lb_ref[...]
        vn_s[...] = vn.astype(BF16)
        _, wsm = _tril_weights(ws_ref, groups)
        for ck in range(tm // CHUNK):
            rows = slice(ck * CHUNK, (ck + 1) * CHUNK)
            for g in range(groups):
                cols = slice(g * gd, (g + 1) * gd)
                mixed_s[rows, cols] = (jnp.dot(wsm[g].astype(BF16), vn_s[rows, cols], preferred_element_type=F32)
                                       + bmap_ref[:, cols])
        yb = piece(3) * mixed_s[...]
        merged = jax.nn.sigmoid(piece(5)) * ya + jax.nn.sigmoid(piece(6)) * yb
        o_ref[...] = x_ref[...] + jnp.dot(merged.astype(BF16), wout_ref[...], preferred_element_type=F32)

    return pl.pallas_call(
        body, name=name, grid=(n // tm,),
        in_specs=[pl.BlockSpec((tm, N_PIECES * d), lambda i: (i, 0)),
                  pl.BlockSpec((HALO, d), lambda i: (prev(i), 1)),
                  pl.BlockSpec((HALO, d), lambda i: (prev(i), 2)),
                  pl.BlockSpec((tm, d), lambda i: (i, 0)),
                  _full((3, d)), _full((1, d)), _full((1, d)), _full((groups, CHUNK, CHUNK)), _full((CHUNK, d)),
                  _resident((d, d), lambda i: (0, 0))],
        out_specs=pl.BlockSpec((tm, d), lambda i: (i, 0)),
        out_shape=jax.ShapeDtypeStruct((n, d), F32),
        scratch_shapes=[pltpu.VMEM((tm, d), BF16), pltpu.VMEM((tm, d), F32)],
        compiler_params=pltpu.CompilerParams(dimension_semantics=("parallel",), vmem_limit_bytes=VMEM_LIMIT_V7X),
    )(proj, proj, proj, x, cw, lg, lb, ws, bmap, wout)


def _ffn_fwd(up, x, cw, wd, *, seq, tm, cwid, name):
    n, d = x.shape
    ff = wd.shape[0]
    prev = _halo_prev(tm)

    def body(up_ref, pup_ref, x_ref, cw_ref, wd_ref, o_ref, conv_ref):
        seq_start = (pl.program_id(0) * tm) % seq == 0

        def conv(lo):
            cols = slice(lo, lo + cwid)
            cur = up_ref[:, cols].astype(F32)
            pre = jnp.where(seq_start, 0.0, pup_ref[:, cols].astype(F32))
            out = (cw_ref[0:1, cols] * _shift_prev(pre, cur, 2) + cw_ref[1:2, cols] * _shift_prev(pre, cur, 1)
                   + cw_ref[2:3, cols] * cur)
            conv_ref[:, cols] = out.astype(BF16)
            return out

        acc = x_ref[...]
        for cj in range(ff // cwid):
            gate = conv(cj * cwid)
            val = conv(ff + cj * cwid)
            a = gate * jax.nn.sigmoid(gate) * val
            acc = acc + jnp.dot(a.astype(BF16), wd_ref[cj * cwid:(cj + 1) * cwid, :], preferred_element_type=F32)
        o_ref[...] = acc

    return pl.pallas_call(
        body, name=name, grid=(n // tm,),
        in_specs=[pl.BlockSpec((tm, 2 * ff), lambda i: (i, 0)),
                  pl.BlockSpec((HALO, 2 * ff), lambda i: (prev(i), 0)),
                  pl.BlockSpec((tm, d), lambda i: (i, 0)),
                  _full((3, 2 * ff)),
                  _resident((ff, d), lambda i: (0, 0))],
        out_specs=[pl.BlockSpec((tm, d), lambda i: (i, 0)), pl.BlockSpec((tm, 2 * ff), lambda i: (i, 0))],
        out_shape=[jax.ShapeDtypeStruct((n, d), F32), jax.ShapeDtypeStruct((n, 2 * ff), BF16)],
        compiler_params=pltpu.CompilerParams(dimension_semantics=("parallel",), vmem_limit_bytes=VMEM_LIMIT_V7X),
    )(up, up, x, cw, wd)


def _final_loss(x, g, target, *, tm, name):
    n, d = x.shape

    def body(x_ref, g_ref, t_ref, dx_ref, loss_ref, dg_ref):
        @pl.when(pl.program_id(0) == 0)
        def _():
            loss_ref[...] = jnp.zeros_like(loss_ref)
            dg_ref[...] = jnp.zeros_like(dg_ref)

        xf = x_ref[...]
        r = lax.rsqrt(_lanemean(xf * xf) + EPS)
        xhat = xf * r
        diff = xhat * g_ref[...] - t_ref[...]
        loss_ref[...] += (0.5 / d) * _rowsum(jnp.sum(diff * diff, axis=-1, keepdims=True))
        dy = diff * (1.0 / d)
        dg_ref[...] += _rowsum(dy * xhat)
        dyh = dy * g_ref[...]
        dx_ref[...] = r * (dyh - xhat * _lanemean(dyh * xhat))

    return pl.pallas_call(
        body, name=name, grid=(n // tm,),
        in_specs=[pl.BlockSpec((tm, d), lambda i: (i, 0)), _full((1, d)), pl.BlockSpec((tm, d), lambda i: (i, 0))],
        out_specs=[pl.BlockSpec((tm, d), lambda i: (i, 0)), _full((HALO, CHUNK)), _full((1, d))],
        out_shape=[jax.ShapeDtypeStruct((n, d), F32), jax.ShapeDtypeStruct((HALO, CHUNK), F32),
                   jax.ShapeDtypeStruct((1, d), F32)],
        compiler_params=pltpu.CompilerParams(dimension_semantics=("arbitrary",), vmem_limit_bytes=VMEM_LIMIT_V7X),
    )(x, g, target)


def _ffn_bwd(dx, up, conv, cw, wd, *, seq, tm, cwid, name):
    n, d = dx.shape
    ff = wd.shape[0]
    nxt = _halo_next(tm, n)

    def body(dx_ref, dxn_ref, up_ref, conv_ref, nconv_ref, cw_ref, wd_ref, dup_ref, dwd_ref, dcw_ref):
        i = pl.program_id(0)
        keep_next = jnp.where(((i + 1) * tm) % seq == 0, 0.0, 1.0)

        @pl.when(i == 0)
        def _():
            dwd_ref[...] = jnp.zeros_like(dwd_ref)
            dcw_ref[...] = jnp.zeros_like(dcw_ref)

        dxe = jnp.concatenate([dx_ref[...], dxn_ref[...]], axis=0).astype(BF16)
        dxb = dxe[:tm]
        for cj in range(ff // cwid):
            rows = slice(cj * cwid, (cj + 1) * cwid)
            g_cols, v_cols = slice(cj * cwid, (cj + 1) * cwid), slice(ff + cj * cwid, ff + (cj + 1) * cwid)
            dae = lax.dot_general(dxe, wd_ref[rows, :], NT_DIMS, preferred_element_type=F32)
            da, dan = dae[:tm], dae[tm:]

            def grads(gate, val, da_rows):
                sg = jax.nn.sigmoid(gate)
                sl = gate * sg
                return sl, da_rows * val * sg * (1.0 + gate * (1.0 - sg)), da_rows * sl

            gate, val = conv_ref[:, g_cols].astype(F32), conv_ref[:, v_cols].astype(F32)
            sl, d_gate, d_val = grads(gate, val, da)
            dwd_ref[rows, :] += lax.dot_general((sl * val).astype(BF16), dxb, TN_DIMS, preferred_element_type=F32)
            _, d_gate_n, d_val_n = grads(nconv_ref[:, g_cols].astype(F32), nconv_ref[:, v_cols].astype(F32),
                                         dan * keep_next)
            for cols, dcur, dnext in ((g_cols, d_gate, d_gate_n), (v_cols, d_val, d_val_n)):
                upc = up_ref[:, cols].astype(F32)
                d1, d2 = _shift_next(dcur, dnext, 1), _shift_next(dcur, dnext, 2)
                for k, dk in enumerate((d2, d1, dcur)):
                    dcw_ref[k:k + 1, cols] += _rowsum(dk * upc)
                dup = cw_ref[2:3, cols] * dcur + cw_ref[1:2, cols] * d1 + cw_ref[0:1, cols] * d2
                dup_ref[:, cols] = dup.astype(BF16)

    return pl.pallas_call(
        body, name=name, grid=(n // tm,),
        in_specs=[pl.BlockSpec((tm, d), lambda i: (i, 0)),
                  pl.BlockSpec((HALO, d), lambda i: (nxt(i), 0)),
                  pl.BlockSpec((tm, 2 * ff), lambda i: (i, 0)),
                  pl.BlockSpec((tm, 2 * ff), lambda i: (i, 0)),
                  pl.BlockSpec((HALO, 2 * ff), lambda i: (nxt(i), 0)),
                  _full((3, 2 * ff)),
                  _resident((ff, d), lambda i: (0, 0))],
        out_specs=[pl.BlockSpec((tm, 2 * ff), lambda i: (i, 0)), _full((ff, d)), _full((3, 2 * ff))],
        out_shape=[jax.ShapeDtypeStruct((n, 2 * ff), BF16), jax.ShapeDtypeStruct((ff, d), F32),
                   jax.ShapeDtypeStruct((3, 2 * ff), F32)],
        compiler_params=pltpu.CompilerParams(dimension_semantics=("arbitrary",), vmem_limit_bytes=VMEM_LIMIT_V7X),
    )(dx, dx, up, conv, conv, cw, wd)


def _matmul_bwd_x(dy, wg, x, g, dres, *, tm, name, jobs=()):
    n, d = x.shape
    n_chip, _, ck = wg.shape

    def main(ins, outs, _):
        dy_ref, w_ref, x_ref, g_ref, dres_ref = ins
        dx_ref, dg_ref = outs

        @pl.when(pl.program_id(0) == 0)
        def _():
            dg_ref[...] = jnp.zeros_like(dg_ref)

        dh = lax.dot_general(dy_ref[:, 0:ck], w_ref[0], NT_DIMS, preferred_element_type=F32)
        for j in range(1, n_chip):
            dh = dh + lax.dot_general(dy_ref[:, j * ck:(j + 1) * ck], w_ref[j], NT_DIMS, preferred_element_type=F32)
        xf = x_ref[...]
        r = lax.rsqrt(_lanemean(xf * xf) + EPS)
        xhat = xf * r
        dg_ref[...] += _rowsum(dh * xhat)
        dyh = dh * g_ref[...]
        dx_ref[...] = dres_ref[...] + r * (dyh - xhat * _lanemean(dyh * xhat))

    return _hosted_call(
        main, name=name, grid=(n // tm,), jobs=jobs, semantics=("arbitrary",), operands=(dy, wg, x, g, dres),
        in_specs=[pl.BlockSpec((tm, n_chip * ck), lambda i: (i, 0)), _resident((n_chip, d, ck), lambda i: (0, 0, 0)),
                  pl.BlockSpec((tm, d), lambda i: (i, 0)), _full((1, d)), pl.BlockSpec((tm, d), lambda i: (i, 0))],
        out_specs=[pl.BlockSpec((tm, d), lambda i: (i, 0)), _full((1, d))],
        out_shape=[jax.ShapeDtypeStruct((n, d), F32), jax.ShapeDtypeStruct((1, d), F32)])


def _matmul_bwd_w(h, dy, n_chip, *, tk, name, jobs=()):
    n, d = h.shape
    ck = dy.shape[1] // n_chip

    def main(ins, outs, _):
        h_ref, dy_ref = ins
        o_ref, = outs

        @pl.when(pl.program_id(1) == 0)
        def _():
            o_ref[...] = jnp.zeros_like(o_ref)

        o_ref[...] += lax.dot_general(h_ref[...], dy_ref[...], TN_DIMS, preferred_element_type=F32)

    return _hosted_call(
        main, name=name, grid=(n_chip, n // tk), jobs=jobs, semantics=("parallel", "arbitrary"), operands=(h, dy),
        in_specs=[pl.BlockSpec((tk, d), lambda j, k: (k, 0)), pl.BlockSpec((tk, ck), lambda j, k: (k, j))],
        out_specs=[pl.BlockSpec((None, d, ck), lambda j, k: (j, 0, 0))],
        out_shape=[jax.ShapeDtypeStruct((n_chip, d, ck), F32)])


def _mixer_bwd(dx, proj, cw, lg, lb, ws, bmap, wout, *, seq, tm, name):
    n, d = dx.shape
    groups = ws.shape[0]
    gd = d // groups
    prev, nxt = _halo_prev(tm), _halo_next(tm, n)
    n_tiles = n // tm

    def body(proj_ref, pcg_ref, pxi_ref, nbg_ref, nga_ref, dx_ref, dxn_ref, cw_ref, lg_ref, lb_ref, ws_ref,
             bmap_ref, wout_ref, dproj_ref, dwout_ref, dcw_ref, dlg_ref, dlb_ref, dws_ref, dbs_ref,
             vn_s, mixed_s, dmix_s, dvn_s, dbmap_s):
        i = pl.program_id(0)
        seq_start = (i * tm) % seq == 0
        keep_next = jnp.where(((i + 1) * tm) % seq == 0, 0.0, 1.0)

        @pl.when(i == 0)
        def _():
            for ref in (dwout_ref, dcw_ref, dlg_ref, dlb_ref, dws_ref, dbmap_s):
                ref[...] = jnp.zeros_like(ref)

        def piece(k):
            return proj_ref[:, k * d:(k + 1) * d].astype(F32)

        def put(k, val):
            dproj_ref[:, k * d:(k + 1) * d] = val.astype(BF16)

        w = [cw_ref[k:k + 1, :] for k in range(3)]
        cg, xi = piece(1), piece(2)
        z = cg * xi
        zprev = jnp.where(seq_start, 0.0, pcg_ref[...].astype(F32) * pxi_ref[...].astype(F32))
        z2, z1 = _shift_prev(zprev, z, 2), _shift_prev(zprev, z, 1)
        cz = w[0] * z2 + w[1] * z1 + w[2] * z
        bg = piece(0)
        ya = bg * cz

        v = piece(4)
        xc = v - _lanemean(v)
        rstd = lax.rsqrt(_lanemean(xc * xc) + EPS)
        vhat = xc * rstd
        vn_s[...] = (vhat * lg_ref[...] + lb_ref[...]).astype(BF16)
        tril, wsm = _tril_weights(ws_ref, groups)
        for ck in range(tm // CHUNK):
            rows = slice(ck * CHUNK, (ck + 1) * CHUNK)
            for g in range(groups):
                cols = slice(g * gd, (g + 1) * gd)
                mixed_s[rows, cols] = (jnp.dot(wsm[g].astype(BF16), vn_s[rows, cols], preferred_element_type=F32)
                                       + bmap_ref[:, cols])
        u = piece(3)
        mixed = mixed_s[...]
        yb = u * mixed
        sa, sb = jax.nn.sigmoid(piece(5)), jax.nn.sigmoid(piece(6))
        merged = sa * ya + sb * yb

        dxe = jnp.concatenate([dx_ref[...], dxn_ref[...]], axis=0).astype(BF16)
        dme = lax.dot_general(dxe, wout_ref[...], NT_DIMS, preferred_element_type=F32)
        dm, dm_n = dme[:tm], dme[tm:]
        dwout_ref[...] += lax.dot_general(merged.astype(BF16), dxe[:tm], TN_DIMS, preferred_element_type=F32)

        put(5, dm * ya * sa * (1.0 - sa))
        put(6, dm * yb * sb * (1.0 - sb))
        d_ya, d_yb = dm * sa, dm * sb
        put(0, d_ya * cz)
        d_cz = d_ya * bg
        for k, tap in enumerate((z2, z1, z)):
            dcw_ref[k:k + 1, :] += _rowsum(d_cz * tap)
        d_cz_n = dm_n * jax.nn.sigmoid(nga_ref[...].astype(F32)) * nbg_ref[...].astype(F32) * keep_next
        dz = w[2] * d_cz + w[1] * _shift_next(d_cz, d_cz_n, 1) + w[0] * _shift_next(d_cz, d_cz_n, 2)
        put(1, dz * xi)
        put(2, dz * cg)

        put(3, d_yb * mixed)
        d_mixed = d_yb * u
        dmix_s[...] = d_mixed.astype(BF16)
        for ck in range(tm // CHUNK):
            rows = slice(ck * CHUNK, (ck + 1) * CHUNK)
            dbmap_s[...] += d_mixed[rows, :]
            for g in range(groups):
                cols = slice(g * gd, (g + 1) * gd)
                dvn_s[rows, cols] = jnp.dot(wsm[g].T.astype(BF16), dmix_s[rows, cols], preferred_element_type=F32)
                dws_ref[g] += jnp.where(
                    tril, lax.dot_general(dmix_s[rows, cols], vn_s[rows, cols], NT_DIMS, preferred_element_type=F32),
                    0.0)
        d_vn = dvn_s[...]
        dlg_ref[...] += _rowsum(d_vn * vhat)
        dlb_ref[...] += _rowsum(d_vn)
        d_vhat = d_vn * lg_ref[...]
        put(4, rstd * (d_vhat - _lanemean(d_vhat) - vhat * _lanemean(d_vhat * vhat)))

        @pl.when(i == n_tiles - 1)
        def _():
            for g in range(groups):
                dbs_ref[:, g:g + 1] = jnp.sum(dbmap_s[:, g * gd:(g + 1) * gd], axis=-1, keepdims=True)

    return pl.pallas_call(
        body, name=name, grid=(n_tiles,),
        in_specs=[pl.BlockSpec((tm, N_PIECES * d), lambda i: (i, 0)),
                  pl.BlockSpec((HALO, d), lambda i: (prev(i), 1)),
                  pl.BlockSpec((HALO, d), lambda i: (prev(i), 2)),
                  pl.BlockSpec((HALO, d), lambda i: (nxt(i), 0)),
                  pl.BlockSpec((HALO, d), lambda i: (nxt(i), 5)),
                  pl.BlockSpec((tm, d), lambda i: (i, 0)),
                  pl.BlockSpec((HALO, d), lambda i: (nxt(i), 0)),
                  _full((3, d)), _full((1, d)), _full((1, d)), _full((groups, CHUNK, CHUNK)), _full((CHUNK, d)),
                  _resident((d, d), lambda i: (0, 0))],
        out_specs=[pl.BlockSpec((tm, N_PIECES * d), lambda i: (i, 0)), _full((d, d)),
                   _full((3, d)), _full((1, d)), _full((1, d)), _full((groups, CHUNK, CHUNK)), _full((CHUNK, groups))],
        out_shape=[jax.ShapeDtypeStruct((n, N_PIECES * d), BF16), jax.ShapeDtypeStruct((d, d), F32),
                   jax.ShapeDtypeStruct((3, d), F32), jax.ShapeDtypeStruct((1, d), F32),
                   jax.ShapeDtypeStruct((1, d), F32), jax.ShapeDtypeStruct((groups, CHUNK, CHUNK), F32),
                   jax.ShapeDtypeStruct((CHUNK, groups), F32)],
        scratch_shapes=[pltpu.VMEM((tm, d), BF16), pltpu.VMEM((tm, d), F32), pltpu.VMEM((tm, d), BF16),
                        pltpu.VMEM((tm, d), F32), pltpu.VMEM((CHUNK, d), F32)],
        compiler_params=pltpu.CompilerParams(dimension_semantics=("arbitrary",), vmem_limit_bytes=VMEM_LIMIT_V7X),
    )(proj, proj, proj, proj, proj, dx, dx, cw, lg, lb, ws, bmap, wout)


def _gather_small(pack, name):
    def body(p_ref, o_ref, send_sem, recv_sem):
        x, y, c = _mesh_pos()
        chip = 2 * x + y
        o_ref[chip] = p_ref[...]
        copies = []
        for r in (1, 2, 3):
            px, py = _peer_chip(x, y, r)
            copies.append(_remote(p_ref, o_ref.at[chip], (send_sem, recv_sem), r - 1, (px, py, c)))
        for cp in copies:
            cp.start()
        for r, cp in zip((1, 2, 3), copies):
            px, py = _peer_chip(x, y, r)
            landed = o_ref.at[2 * px + py]
            _remote(landed, landed, (send_sem, recv_sem), r - 1, (px, py, c)).wait_recv()
            cp.wait_send()

    return pl.pallas_call(
        body, name=name, in_specs=[VMEM_SPEC], out_specs=VMEM_SPEC,
        out_shape=jax.ShapeDtypeStruct((N_CHIP,) + pack.shape, pack.dtype),
        scratch_shapes=[pltpu.SemaphoreType.DMA((3,)), pltpu.SemaphoreType.DMA((3,))],
    )(pack)


def _all_reduce_small(pack, name, jobs=()):
    rows = pack.shape[0]

    def main(ins, outs, scratch):
        p_ref, = ins
        o_ref, = outs
        sib_buf, chip_buf, send_sem, recv_sem = scratch
        sems = (send_sem, recv_sem)
        x, y, c = _mesh_pos()
        chip = 2 * x + y
        swap = _remote(p_ref, sib_buf, sems, 0, (x, y, 1 - c))
        swap.start()
        swap.wait()
        chip_buf[chip] = p_ref[...] + sib_buf[...]
        copies = []
        for r in (1, 2, 3):
            px, py = _peer_chip(x, y, r)
            copies.append(_remote(chip_buf.at[chip], chip_buf.at[chip], sems, r, (px, py, c)))
        for cp in copies:
            cp.start()
        for r, cp in zip((1, 2, 3), copies):
            px, py = _peer_chip(x, y, r)
            landed = chip_buf.at[2 * px + py]
            _remote(landed, landed, sems, r, (px, py, c)).wait_recv()
            cp.wait_send()
        o_ref[...] = ((chip_buf[0] + chip_buf[1]) + chip_buf[2]) + chip_buf[3]

    outs, job_outs = _hosted_call(
        main, name=name, grid=(1,), jobs=jobs, operands=(pack,),
        in_specs=[_full(pack.shape)], out_specs=[_full(pack.shape)],
        out_shape=[jax.ShapeDtypeStruct(pack.shape, pack.dtype)],
        scratch_shapes=[pltpu.VMEM((rows, 128), F32), pltpu.VMEM((N_CHIP, rows, 128), F32),
                        pltpu.SemaphoreType.DMA((4,)), pltpu.SemaphoreType.DMA((4,))])
    return outs[0], job_outs


def _block_rows(rows, cols, itemsize=4, budget=2 << 20):
    best = None
    for t in range(16, rows + 1, 16):
        if rows % t == 0 and t * cols * itemsize <= budget:
            best = t
    assert best is not None, (rows, cols)
    return best


def _pair_sum(grad, recv, core, name):
    n_chip, rk, ck = grad.shape
    rh = rk // 2
    tr = _block_rows(rh, ck)
    nb = rh // tr

    def body(core_ref, g_ref, r_ref, o_ref):
        o_ref[...] = (g_ref[...] + r_ref[...]).astype(BF16)

    blk = (None, tr, ck)
    return pl.pallas_call(
        body, name=name,
        grid_spec=pltpu.PrefetchScalarGridSpec(
            num_scalar_prefetch=1, grid=(n_chip, nb),
            in_specs=[pl.BlockSpec(blk, lambda k, i, core_ref: (k, core_ref[0] * nb + i, 0)),
                      pl.BlockSpec(blk, lambda k, i, core_ref: (k, i, 0))],
            out_specs=pl.BlockSpec(blk, lambda k, i, core_ref: (k, i, 0))),
        out_shape=jax.ShapeDtypeStruct((n_chip, rh, ck), BF16),
        compiler_params=pltpu.CompilerParams(dimension_semantics=("parallel", "parallel"),
                                             vmem_limit_bytes=VMEM_LIMIT_V7X),
    )(core, grad, recv)


def _final_sum(grad, recv, arrived, where, prev, layer, n_layers, name):
    n_chip, rk, ck = grad.shape
    rh = rk // 2
    tr = _block_rows(rh, ck)
    nb = rh // tr

    def body(where_ref, g_ref, r_ref, a1_ref, a2_ref, a3_ref, *rest):
        o_ref = rest[-1]
        own = g_ref[...] + r_ref[...]
        o_ref[...] = ((own + a1_ref[...].astype(F32)) + a2_ref[...].astype(F32)) + a3_ref[...].astype(F32)

    blk = (None, tr, ck)
    slot = lambda r: pl.BlockSpec(blk, lambda i, w: (r, i, 0))
    extra = [] if prev is None else [prev]
    return pl.pallas_call(
        body, name=name,
        grid_spec=pltpu.PrefetchScalarGridSpec(
            num_scalar_prefetch=1, grid=(nb,),
            in_specs=[pl.BlockSpec(blk, lambda i, w: (w[1], w[0] * nb + i, 0)),
                      pl.BlockSpec(blk, lambda i, w: (w[1], i, 0)),
                      slot(0), slot(1), slot(2)] + [pl.BlockSpec(memory_space=pl.ANY)] * len(extra),
            out_specs=pl.BlockSpec(blk, lambda i, w: (layer, w[0] * nb + i, 0))),
        out_shape=jax.ShapeDtypeStruct((n_layers, rk, ck), F32),
        input_output_aliases={6: 0} if extra else {},
        compiler_params=pltpu.CompilerParams(dimension_semantics=("parallel",), vmem_limit_bytes=VMEM_LIMIT_V7X),
    )(where, grad, recv, arrived, arrived, arrived, *extra)


def _adamw_math(w, g, m, v):
    m = ADAM_B1 * m + (1.0 - ADAM_B1) * g
    v = ADAM_B2 * v + (1.0 - ADAM_B2) * (g * g)
    m_hat = m / (1.0 - ADAM_B1 ** ADAM_STEP)
    v_hat = v / (1.0 - ADAM_B2 ** ADAM_STEP)
    delta = -ADAM_LR * (m_hat / (jnp.sqrt(v_hat) + ADAM_EPS) + ADAM_WD * w)
    return delta, m, v


def _adamw_big(w, g, m, v, name):
    n_l, rk, ck = w.shape
    tr = _block_rows(rk, ck, budget=1 << 20)

    def body(w_ref, g_ref, m_ref, v_ref, d_ref, nm_ref, nv_ref):
        d_ref[...], nm_ref[...], nv_ref[...] = _adamw_math(w_ref[...], g_ref[...], m_ref[...], v_ref[...])

    spec = pl.BlockSpec((None, tr, ck), lambda l, i: (l, i, 0))
    return pl.pallas_call(
        body, name=name, grid=(n_l, rk // tr), in_specs=[spec] * 4, out_specs=[spec] * 3,
        out_shape=[jax.ShapeDtypeStruct(w.shape, F32)] * 3,
        compiler_params=pltpu.CompilerParams(dimension_semantics=("parallel", "parallel"),
                                             vmem_limit_bytes=VMEM_LIMIT_V7X),
    )(w, g, m, v)


def _adamw_small(ws, gs, ms, vs, name):
    n_p = len(ws)

    def body(*refs):
        ins, outs = refs[:4 * n_p], refs[4 * n_p:]
        for p in range(n_p):
            res = _adamw_math(ins[p][...], ins[n_p + p][...], ins[2 * n_p + p][...], ins[3 * n_p + p][...])
            for q in range(3):
                outs[q * n_p + p][...] = res[q]

    outs = pl.pallas_call(
        body, name=name, in_specs=[VMEM_SPEC] * (4 * n_p), out_specs=[VMEM_SPEC] * (3 * n_p),
        out_shape=[jax.ShapeDtypeStruct(w.shape, F32) for w in ws] * 3,
        compiler_params=pltpu.CompilerParams(vmem_limit_bytes=VMEM_LIMIT_V7X),
    )(*ws, *gs, *ms, *vs)
    return outs[:n_p], outs[n_p:2 * n_p], outs[2 * n_p:]


def kernel(x, mix_norm_g, w_in, conv_a_w, ln_v_g, ln_v_b, w_s, b_s, w_out, ffn_norm_g, w_up, conv_ffn_w, w_down, final_norm_g, loss_target, m_mix_norm_g, m_w_in, m_conv_a_w, m_ln_v_g, m_ln_v_b, m_w_s, m_b_s, m_w_out, m_ffn_norm_g, m_w_up, m_conv_ffn_w, m_w_down, m_final_norm_g, v_mix_norm_g, v_w_in, v_conv_a_w, v_ln_v_g, v_ln_v_b, v_w_s, v_b_s, v_w_out, v_ffn_norm_g, v_w_up, v_conv_ffn_w, v_w_down, v_final_norm_g):
    bsz, seq, d = x.shape
    n = bsz * seq
    n_l, groups = w_s.shape[0], w_s.shape[1]
    assert n_l == 2, "the exchange schedule below is written for two layers"
    gd = d // groups
    ff = w_down.shape[1] * N_CHIP
    axes = ("x", "y", "c")
    mx, my, mc = _mesh_pos()
    chip = 2 * mx + my
    core = jnp.reshape(mc, (1,)).astype(jnp.int32)
    where = jnp.stack([mc, chip]).astype(jnp.int32)

    tm_mm = _row_tile(seq, 512)
    tm_ew = _row_tile(seq, 256)
    tk_w = _row_tile(seq, 1024)
    ff_chunk = ff // 2 if (ff // 2) % 128 == 0 else ff
    IN, OUT, UP, DOWN = range(4)

    shards = [w.astype(BF16) for w in (w_in, w_out, w_up, w_down)]
    piece = lambda a, l: (shards[a], l)
    (w_in_0,), = _comm_only([_GatherJob([piece(IN, 0)])], "gather_first")

    taps = jnp.concatenate([conv_a_w.reshape(n_l, -1), conv_ffn_w.reshape(n_l, -1)], axis=1)
    tap_rows = -(-taps.size // 128 // 8) * 8
    tap_pack = jnp.zeros((tap_rows * 128,), F32).at[:taps.size].set(taps.reshape(-1)).reshape(tap_rows, 128)
    tap_all = _gather_small(tap_pack, "gather_taps")
    tap_all = tap_all.reshape(N_CHIP, -1)[:, :taps.size].reshape(N_CHIP, n_l, -1)
    ca = tap_all[:, :, :3 * d // N_CHIP].reshape(N_CHIP, n_l, 3, d // N_CHIP)
    cf = tap_all[:, :, 3 * d // N_CHIP:].reshape(N_CHIP, n_l, 3, 2 * ff // N_CHIP)
    conv_a_full = jnp.transpose(ca, (1, 2, 0, 3)).reshape(n_l, 3, d)
    conv_f_full = jnp.transpose(cf, (1, 2, 0, 3)).reshape(n_l, 3, 2 * ff)

    bmaps = jnp.repeat(jnp.swapaxes(b_s, 1, 2), gd, axis=2)

    xs = x.reshape(n, d)
    tgt = loss_target.reshape(n, d)

    def mixer_args(l, w_out_l):
        return (conv_a_full[l], ln_v_g[l][None], ln_v_b[l][None], w_s[l], bmaps[l], w_out_l.reshape(d, d))

    (proj0, h1_0), ((w_out_0, w_up_0, w_down_0),) = _norm_matmul(
        xs, mix_norm_g[0][None], w_in_0, tm=tm_mm, name="fwd_in_proj_0",
        jobs=[_GatherJob([piece(OUT, 0), piece(UP, 0), piece(DOWN, 0)])])
    x1_0 = _mixer_fwd(proj0, xs, *mixer_args(0, w_out_0), seq=seq, tm=tm_ew, name="fwd_mixer_0")
    (up0, h2_0), ((w_in_1, w_out_1),) = _norm_matmul(
        x1_0, ffn_norm_g[0][None], w_up_0, tm=tm_mm, name="fwd_up_proj_0",
        jobs=[_GatherJob([piece(IN, 1), piece(OUT, 1)])])
    x2_0, conv0 = _ffn_fwd(up0, x1_0, conv_f_full[0], w_down_0.reshape(ff, d), seq=seq, tm=tm_ew, cwid=ff_chunk,
                           name="fwd_ffn_0")
    (proj1, h1_1), ((w_up_1, w_down_1),) = _norm_matmul(
        x2_0, mix_norm_g[1][None], w_in_1, tm=tm_mm, name="fwd_in_proj_1",
        jobs=[_GatherJob([piece(UP, 1), piece(DOWN, 1)])])
    x1_1 = _mixer_fwd(proj1, x2_0, *mixer_args(1, w_out_1), seq=seq, tm=tm_ew, name="fwd_mixer_1")
    (up1, h2_1), _ = _norm_matmul(x1_1, ffn_norm_g[1][None], w_up_1, tm=tm_mm, name="fwd_up_proj_1")
    x2_1, conv1 = _ffn_fwd(up1, x1_1, conv_f_full[1], w_down_1.reshape(ff, d), seq=seq, tm=tm_ew, cwid=ff_chunk,
                           name="fwd_ffn_1")
    dx, loss_tile, d_final_g = _final_loss(x2_1, final_norm_g[None], tgt, tm=tm_mm, name="final_loss")
    loss = lax.psum(loss_tile[0, 0], axes)

    def chipwise(a):
        return a.reshape(N_CHIP, a.shape[0] // N_CHIP, a.shape[1])

    def pair_sums(grads, recvs, tag):
        return [_pair_sum(g, r, core, f"grad_pair_sum_{tag}_{a}") for a, (g, r) in enumerate(zip(grads, recvs))]

    d_up, dwd, d_cf1 = _ffn_bwd(dx, up1, conv1, conv_f_full[1], w_down_1.reshape(ff, d), seq=seq, tm=tm_ew,
                                cwid=ff_chunk, name="bwd_ffn_1")
    (dx1, d_g2_1), _ = _matmul_bwd_x(d_up, w_up_1, x1_1, ffn_norm_g[1][None], dx, tm=tm_mm, name="bwd_up_x_1")
    (dwu,), _ = _matmul_bwd_w(h2_1, d_up, N_CHIP, tk=tk_w, name="bwd_up_w_1")
    d_proj, dwo, d_ca1, d_lg1, d_lb1, d_ws1, d_bs1 = _mixer_bwd(
        dx1, proj1, *mixer_args(1, w_out_1), seq=seq, tm=tm_ew, name="bwd_mixer_1")
    (dx, d_g1_1), _ = _matmul_bwd_x(d_proj, w_in_1, x2_0, mix_norm_g[1][None], dx1, tm=tm_mm, name="bwd_in_x_1")
    (dwi,), _ = _matmul_bwd_w(h1_1, d_proj, N_CHIP, tk=tk_w, name="bwd_in_w_1")
    grads1 = [dwi, chipwise(dwo), dwu, chipwise(dwd)]
    small1 = [d_g1_1, d_ca1, d_lg1, d_lb1, d_ws1, d_bs1.T, d_g2_1, d_cf1]

    d_up, dwd, d_cf0 = _ffn_bwd(dx, up0, conv0, conv_f_full[0], w_down_0.reshape(ff, d), seq=seq, tm=tm_ew,
                                cwid=ff_chunk, name="bwd_ffn_0")
    (dx1, d_g2_0), (recv1,) = _matmul_bwd_x(d_up, w_up_0, x1_0, ffn_norm_g[0][None], dx, tm=tm_mm, name="bwd_up_x_0",
                                            jobs=[_SwapJob(grads1)])
    parts1 = pair_sums(grads1, recv1, 1)
    (dwu,), ((arr1_in,),) = _matmul_bwd_w(h2_0, d_up, N_CHIP, tk=tk_w, name="bwd_up_w_0",
                                          jobs=[_ScatterJob([parts1[IN]])])
    d_proj, dwo, d_ca0, d_lg0, d_lb0, d_ws0, d_bs0 = _mixer_bwd(
        dx1, proj0, *mixer_args(0, w_out_0), seq=seq, tm=tm_ew, name="bwd_mixer_0")
    early0 = [chipwise(dwo), dwu, chipwise(dwd)]
    (dwi,), (arr1_rest, recv0_early) = _matmul_bwd_w(
        h1_0, d_proj, N_CHIP, tk=tk_w, name="bwd_in_w_0",
        jobs=[_ScatterJob(parts1[OUT:]), _SwapJob(early0)])
    arrived1 = [arr1_in] + list(arr1_rest)
    gsum = [_final_sum(g, r, p, where, None, 1, n_l, f"grad_final_sum_1_{a}")
            for a, (g, r, p) in enumerate(zip(grads1, recv1, arrived1))]
    parts0_early = pair_sums(early0, recv0_early, "0e")
    (dx, d_g1_0), (gsum, arr0_early, (recv0_in,)) = _matmul_bwd_x(
        d_proj, w_in_0, xs, mix_norm_g[0][None], dx1, tm=tm_mm, name="bwd_in_x_0",
        jobs=[_ShareJob([(g, 1) for g in gsum]), _ScatterJob(parts0_early), _SwapJob([dwi])])
    grad_x = dx.reshape(bsz, seq, d)
    grads0 = [dwi] + early0
    recv0 = [recv0_in] + list(recv0_early)
    part0_in, = pair_sums([dwi], [recv0_in], "0i")
    small0 = [d_g1_0, d_ca0, d_lg0, d_lb0, d_ws0, d_bs0.T, d_g2_0, d_cf0]

    flat = [a.reshape(-1) for a in small0 + small1] + [d_final_g.reshape(-1)]
    sizes = [a.size for a in flat]
    total = sum(sizes)
    rows = -(-total // 128 // 8) * 8
    pack = jnp.concatenate(flat + [jnp.zeros((rows * 128 - total,), F32)]).reshape(rows, 128)
    red, ((arr0_in,),) = _all_reduce_small(pack, "small_all_reduce", jobs=[_ScatterJob([part0_in])])
    red = red.reshape(-1)
    arrived0 = [arr0_in] + list(arr0_early)
    gsum = [_final_sum(g, r, p, where, prev, 0, n_l, f"grad_final_sum_0_{a}")
            for a, (g, r, p, prev) in enumerate(zip(grads0, recv0, arrived0, gsum))]
    (g_in, g_out, g_up, g_down), = _comm_only([_ShareJob([(g, 0) for g in gsum])], "grad_share_0")

    pieces, off = [], 0
    for s in sizes:
        pieces.append(red[off:off + s])
        off += s
    per_layer = [pieces[l * 8:(l + 1) * 8] for l in range(n_l)]

    def stacked(idx, shape):
        return jnp.stack([per_layer[l][idx].reshape(shape) for l in range(n_l)])

    def my_cols(a, width):
        return lax.dynamic_slice_in_dim(a, chip * width, width, axis=-1)

    g_mix_norm = stacked(0, (d,))
    g_conv_a = my_cols(stacked(1, (3, d)), d // N_CHIP)
    g_ln_g = stacked(2, (d,))
    g_ln_b = stacked(3, (d,))
    g_ws = stacked(4, (groups, CHUNK, CHUNK))
    g_bs = stacked(5, (groups, CHUNK))
    g_ffn_norm = stacked(6, (d,))
    g_conv_f = my_cols(stacked(7, (3, 2 * ff)), 2 * ff // N_CHIP)
    g_final = pieces[-1].reshape(1, d)

    big_names = [(w_in, g_in, m_w_in, v_w_in), (w_out, g_out, m_w_out, v_w_out), (w_up, g_up, m_w_up, v_w_up),
                 (w_down, g_down, m_w_down, v_w_down)]
    upd = [_adamw_big(w, g, m, v, f"adamw_{a}") for a, (w, g, m, v) in enumerate(big_names)]

    sm_w = [mix_norm_g, conv_a_w, ln_v_g, ln_v_b, w_s, b_s, ffn_norm_g, conv_ffn_w, final_norm_g[None]]
    sm_g = [g_mix_norm, g_conv_a, g_ln_g, g_ln_b, g_ws, g_bs, g_ffn_norm, g_conv_f, g_final]
    sm_m = [m_mix_norm_g, m_conv_a_w, m_ln_v_g, m_ln_v_b, m_w_s, m_b_s, m_ffn_norm_g, m_conv_ffn_w, m_final_norm_g[None]]
    sm_v = [v_mix_norm_g, v_conv_a_w, v_ln_v_g, v_ln_v_b, v_w_s, v_b_s, v_ffn_norm_g, v_conv_ffn_w, v_final_norm_g[None]]
    sm_d, sm_nm, sm_nv = _adamw_small(sm_w, sm_g, sm_m, sm_v, "adamw_small")

    def ordered(sm, bigs):
        return [sm[0], bigs[0], sm[1], sm[2], sm[3], sm[4], sm[5], bigs[1], sm[6], bigs[2], sm[7], bigs[3],
                sm[8].reshape(d)]

    out_g = ordered(sm_g, [g_in, g_out, g_up, g_down])
    out_d = ordered(sm_d, [u[0] for u in upd])
    out_m = ordered(sm_nm, [u[1] for u in upd])
    out_v = ordered(sm_nv, [u[2] for u in upd])
    return (loss, grad_x, *out_g, *out_d, *out_m, *out_v)
```

```python
import jax
import jax.numpy as jnp
from jax import lax
from jax.experimental import pallas as pl
from jax.experimental.pallas import tpu as pltpu

F32 = jnp.float32
BF16 = jnp.bfloat16
EPS = 1e-6
CHUNK = 128
N_CHIP = 4
HALO = 8
N_PIECES = 7
VMEM_LIMIT_V7X = 56 * 1024 * 1024
MESH_T = pl.DeviceIdType.MESH
HBM_SPEC = pl.BlockSpec(memory_space=pltpu.HBM)
VMEM_SPEC = pl.BlockSpec(memory_space=pltpu.VMEM)

ADAM_LR, ADAM_B1, ADAM_B2, ADAM_EPS, ADAM_WD, ADAM_STEP = 0.001, 0.9, 0.999, 1e-08, 0.01, 10

NT_DIMS = (((1,), (1,)), ((), ()))
TN_DIMS = (((0,), (0,)), ((), ()))


def _resident(block_shape, index_map):
    return pl.BlockSpec(block_shape, index_map, pipeline_mode=pl.Buffered(1))


def _full(shape):
    return pl.BlockSpec(shape, lambda *_: (0,) * len(shape))


def _row_tile(seq, want):
    t = min(seq, want)
    assert seq % t == 0 and t % CHUNK == 0, (seq, want)
    return t


def _shift_prev(prev8, cur, k):
    ext = jnp.concatenate([prev8, cur], axis=0)
    return pltpu.roll(ext, k, 0)[HALO:]


def _shift_next(cur, next8, k):
    ext = jnp.concatenate([cur, next8], axis=0)
    n = ext.shape[0]
    return pltpu.roll(ext, n - k, 0)[:n - HALO]


def _rowsum(a):
    return jnp.sum(a, axis=0, keepdims=True)


def _lanemean(a):
    return jnp.mean(a, axis=-1, keepdims=True)


def _tril_weights(ws_ref, groups):
    r = lax.broadcasted_iota(jnp.int32, (CHUNK, CHUNK), 0)
    c = lax.broadcasted_iota(jnp.int32, (CHUNK, CHUNK), 1)
    tril = r >= c
    return tril, [jnp.where(tril, ws_ref[g], 0.0) for g in range(groups)]


def _halo_prev(tm):
    return lambda i: jnp.maximum(i * (tm // HALO) - 1, 0)


def _halo_next(tm, n):
    last = n // HALO - 1
    return lambda i: jnp.minimum((i + 1) * (tm // HALO), last)


def _mesh_pos():
    return lax.axis_index("x"), lax.axis_index("y"), lax.axis_index("c")


def _peer_chip(x, y, r):
    return (1 - x if r >> 1 else x), (1 - y if r & 1 else y)


def _remote(src, dst, sems, k, to):
    return pltpu.make_async_remote_copy(src_ref=src, dst_ref=dst, send_sem=sems[0].at[k], recv_sem=sems[1].at[k],
                                        device_id=to, device_id_type=MESH_T)


class _GatherJob:
    def __init__(self, pieces):
        self.pieces = pieces
        self.ins = [p[0] for p in pieces]
        self.out_shapes = [jax.ShapeDtypeStruct((N_CHIP,) + p[0].shape[1:], p[0].dtype) for p in pieces]
        self.aliases = {}
        n = len(pieces)
        self.sems = [pltpu.SemaphoreType.DMA((3 * n,))] * 4 + [pltpu.SemaphoreType.DMA((n,))]

    def _half(self, outs, a, of_chip, core):
        rh = outs[a].shape[1] // 2
        return outs[a].at[of_chip, pl.ds(core * rh, rh), :]

    def _own(self, ins, outs, sems, chip):
        return [pltpu.make_async_copy(ins[a].at[layer], outs[a].at[chip], sems[4].at[a])
                for a, (_, layer) in enumerate(self.pieces)]

    def _sends(self, ins, outs, sems, x, y, c):
        chip, out = 2 * x + y, []
        for a, (_, layer) in enumerate(self.pieces):
            rh = outs[a].shape[1] // 2
            for r in (1, 2, 3):
                px, py = _peer_chip(x, y, r)
                out.append(_remote(ins[a].at[layer, pl.ds(c * rh, rh), :], self._half(outs, a, chip, c), sems[0:2],
                                   3 * a + r - 1, (px, py, c)))
        return out

    def _passes(self, outs, sems, x, y, c, core):
        out = []
        for a in range(len(self.pieces)):
            for r in (1, 2, 3):
                px, py = _peer_chip(x, y, r)
                landed = self._half(outs, a, 2 * px + py, core)
                out.append(_remote(landed, landed, sems[2:4], 3 * a + r - 1, (x, y, 1 - c)))
        return out

    def start(self, ins, outs, sems):
        x, y, c = _mesh_pos()
        for cp in self._own(ins, outs, sems, 2 * x + y) + self._sends(ins, outs, sems, x, y, c):
            cp.start()

    def finish(self, ins, outs, sems):
        x, y, c = _mesh_pos()
        sends = self._sends(ins, outs, sems, x, y, c)
        passes = self._passes(outs, sems, x, y, c, c)
        k = 0
        for a in range(len(self.pieces)):
            for r in (1, 2, 3):
                px, py = _peer_chip(x, y, r)
                landed = self._half(outs, a, 2 * px + py, c)
                _remote(landed, landed, sems[0:2], k, (px, py, c)).wait_recv()
                passes[k].start()
                k += 1
        for cp in self._passes(outs, sems, x, y, c, 1 - c):
            cp.wait_recv()
        for cp in sends + passes:
            cp.wait_send()
        for cp in self._own(ins, outs, sems, 2 * x + y):
            cp.wait()


class _SwapJob:
    def __init__(self, pieces):
        self.ins = list(pieces)
        self.out_shapes = [jax.ShapeDtypeStruct((g.shape[0], g.shape[1] // 2, g.shape[2]), g.dtype) for g in pieces]
        self.aliases = {}
        self.sems = [pltpu.SemaphoreType.DMA((len(pieces),))] * 2

    def _copies(self, ins, outs, sems):
        x, y, c = _mesh_pos()
        out = []
        for a in range(len(ins)):
            rh = ins[a].shape[1] // 2
            out.append(_remote(ins[a].at[:, pl.ds((1 - c) * rh, rh), :], outs[a], sems, a, (x, y, 1 - c)))
        return out

    def start(self, ins, outs, sems):
        for cp in self._copies(ins, outs, sems):
            cp.start()

    def finish(self, ins, outs, sems):
        for cp in self._copies(ins, outs, sems):
            cp.wait()


class _ScatterJob:
    def __init__(self, pieces):
        self.ins = list(pieces)
        self.out_shapes = [jax.ShapeDtypeStruct((3,) + p.shape[1:], p.dtype) for p in pieces]
        self.aliases = {}
        self.sems = [pltpu.SemaphoreType.DMA((3 * len(pieces),))] * 2

    def _copies(self, ins, outs, sems):
        x, y, c = _mesh_pos()
        out = []
        for a in range(len(ins)):
            for r in (1, 2, 3):
                px, py = _peer_chip(x, y, r)
                out.append(_remote(ins[a].at[2 * px + py], outs[a].at[r - 1], sems, 3 * a + r - 1, (px, py, c)))
        return out

    def start(self, ins, outs, sems):
        for cp in self._copies(ins, outs, sems):
            cp.start()

    def finish(self, ins, outs, sems):
        for cp in self._copies(ins, outs, sems):
            cp.wait()


class _ShareJob:
    def __init__(self, pieces):
        self.pieces = pieces
        self.ins = [p[0] for p in pieces]
        self.out_shapes = [jax.ShapeDtypeStruct(p[0].shape, p[0].dtype) for p in pieces]
        self.aliases = {a: a for a in range(len(pieces))}
        self.sems = [pltpu.SemaphoreType.DMA((len(pieces),))] * 2

    def _copies(self, outs, sems, core):
        x, y, c = _mesh_pos()
        out = []
        for a, (_, layer) in enumerate(self.pieces):
            rh = outs[a].shape[1] // 2
            rows = outs[a].at[layer, pl.ds(core * rh, rh), :]
            out.append(_remote(rows, rows, sems, a, (x, y, 1 - c)))
        return out

    def start(self, ins, outs, sems):
        for cp in self._copies(outs, sems, lax.axis_index("c")):
            cp.start()

    def finish(self, ins, outs, sems):
        c = lax.axis_index("c")
        for cp in self._copies(outs, sems, c):
            cp.wait_send()
        for cp in self._copies(outs, sems, 1 - c):
            cp.wait_recv()


def _all_of(conds):
    out = conds[0]
    for c in conds[1:]:
        out = jnp.logical_and(out, c)
    return out


def _hosted_call(main, *, name, grid, in_specs, out_specs, out_shape, scratch_shapes=(), jobs=(), semantics=None,
                 operands=()):
    n_in, n_out, n_sc = len(in_specs), len(out_specs), len(scratch_shapes)
    counts = [(len(j.ins), len(j.out_shapes), len(j.sems)) for j in jobs]
    j_in, j_out, j_sc = (sum(c[k] for c in counts) for k in range(3))
    aliases, i0, o0 = {}, n_in, n_out
    for j, (ci, co, _) in zip(jobs, counts):
        aliases.update({i0 + a: o0 + b for a, b in j.aliases.items()})
        i0, o0 = i0 + ci, o0 + co

    def body(*refs):
        cuts = [0, n_in, n_in + j_in, n_in + j_in + n_out, n_in + j_in + n_out + j_out,
                n_in + j_in + n_out + j_out + n_sc, len(refs)]
        m_in, jb_in, m_out, jb_out, m_sc, jb_sc = (list(refs[cuts[k]:cuts[k + 1]]) for k in range(6))

        def run(phase):
            i0 = o0 = s0 = 0
            for j, (ci, co, cs) in zip(jobs, counts):
                getattr(j, phase)(jb_in[i0:i0 + ci], jb_out[o0:o0 + co], jb_sc[s0:s0 + cs])
                i0, o0, s0 = i0 + ci, o0 + co, s0 + cs

        first = _all_of([pl.program_id(ax) == 0 for ax in range(len(grid))])
        last = _all_of([pl.program_id(ax) == grid[ax] - 1 for ax in range(len(grid))])
        if jobs:
            pl.when(first)(lambda: run("start"))
        main(m_in, m_out, m_sc)
        if jobs:
            pl.when(last)(lambda: run("finish"))

    if semantics is None or jobs:
        semantics = ("arbitrary",) * len(grid)
    outs = pl.pallas_call(
        body, name=name, grid=grid,
        in_specs=list(in_specs) + [HBM_SPEC] * j_in,
        out_specs=list(out_specs) + [HBM_SPEC] * j_out,
        out_shape=list(out_shape) + [s for j in jobs for s in j.out_shapes],
        scratch_shapes=list(scratch_shapes) + [s for j in jobs for s in j.sems],
        input_output_aliases=aliases,
        compiler_params=pltpu.CompilerParams(dimension_semantics=semantics, vmem_limit_bytes=VMEM_LIMIT_V7X),
    )(*operands, *[a for j in jobs for a in j.ins])
    main_outs, rest, job_outs = list(outs[:n_out]), list(outs[n_out:]), []
    for _, co, _ in counts:
        job_outs.append(rest[:co])
        rest = rest[co:]
    return main_outs, job_outs


def _comm_only(jobs, name):
    return _hosted_call(lambda i, o, s: None, name=name, grid=(1,), in_specs=[], out_specs=[], out_shape=[],
                        jobs=jobs)[1]


def _norm_matmul(x, g, wg, *, tm, name, jobs=()):
    n, d = x.shape
    n_chip, _, ck = wg.shape

    def main(ins, outs, _):
        x_ref, g_ref, w_ref = ins
        o_ref, ht_ref = outs
        xf = x_ref[...]
        r = lax.rsqrt(_lanemean(xf * xf) + EPS)
        h = xf * r * g_ref[...]
        ht_ref[...] = h.T.astype(BF16)
        hb = h.astype(BF16)
        for j in range(n_chip):
            o_ref[:, j * ck:(j + 1) * ck] = jnp.dot(hb, w_ref[j], preferred_element_type=F32).astype(BF16)

    return _hosted_call(
        main, name=name, grid=(n // tm,), jobs=jobs, semantics=("parallel",), operands=(x, g, wg),
        in_specs=[pl.BlockSpec((tm, d), lambda i: (i, 0)), _full((1, d)), _resident((n_chip, d, ck), lambda i: (0, 0, 0))],
        out_specs=[pl.BlockSpec((tm, n_chip * ck), lambda i: (i, 0)), pl.BlockSpec((d, tm), lambda i: (0, i))],
        out_shape=[jax.ShapeDtypeStruct((n, n_chip * ck), BF16), jax.ShapeDtypeStruct((d, n), BF16)])


def _mixer_fwd(proj, x, cw, lg, lb, ws, bmap, wout, *, seq, tm, name):
    n, d = x.shape
    groups = ws.shape[0]
    gd = d // groups
    prev = _halo_prev(tm)

    def body(proj_ref, pcg_ref, pxi_ref, x_ref, cw_ref, lg_ref, lb_ref, ws_ref, bmap_ref, wout_ref, o_ref,
             vn_s, mixed_s):
        seq_start = (pl.program_id(0) * tm) % seq == 0

        def piece(k):
            return proj_ref[:, k * d:(k + 1) * d].astype(F32)

        z = piece(1) * piece(2)
        zprev = jnp.where(seq_start, 0.0, pcg_ref[...].astype(F32) * pxi_ref[...].astype(F32))
        cz = (cw_ref[0:1, :] * _shift_prev(zprev, z, 2) + cw_ref[1:2, :] * _shift_prev(zprev, z, 1)
              + cw_ref[2:3, :] * z)
        ya = piece(0) * cz

        v = piece(4)
        xc = v - _lanemean(v)
        vn = xc * lax.rsqrt(_lanemean(xc * xc) + EPS) * lg_ref[...] + lb_ref[...]
        vn_s[...] = vn.astype(BF16)
        _, wsm = _tril_weights(ws_ref, groups)
        for ck in range(tm // CHUNK):
            rows = slice(ck * CHUNK, (ck + 1) * CHUNK)
            for g in range(groups):
                cols = slice(g * gd, (g + 1) * gd)
                mixed_s[rows, cols] = (jnp.dot(wsm[g].astype(BF16), vn_s[rows, cols], preferred_element_type=F32)
                                       + bmap_ref[:, cols])
        yb = piece(3) * mixed_s[...]
        merged = jax.nn.sigmoid(piece(5)) * ya + jax.nn.sigmoid(piece(6)) * yb
        o_ref[...] = x_ref[...] + jnp.dot(merged.astype(BF16), wout_ref[...], preferred_element_type=F32)

    return pl.pallas_call(
        body, name=name, grid=(n // tm,),
        in_specs=[pl.BlockSpec((tm, N_PIECES * d), lambda i: (i, 0)),
                  pl.BlockSpec((HALO, d), lambda i: (prev(i), 1)),
                  pl.BlockSpec((HALO, d), lambda i: (prev(i), 2)),
                  pl.BlockSpec((tm, d), lambda i: (i, 0)),
                  _full((3, d)), _full((1, d)), _full((1, d)), _full((groups, CHUNK, CHUNK)), _full((CHUNK, d)),
                  _resident((d, d), lambda i: (0, 0))],
        out_specs=pl.BlockSpec((tm, d), lambda i: (i, 0)),
        out_shape=jax.ShapeDtypeStruct((n, d), F32),
        scratch_shapes=[pltpu.VMEM((tm, d), BF16), pltpu.VMEM((tm, d), F32)],
        compiler_params=pltpu.CompilerParams(dimension_semantics=("parallel",), vmem_limit_bytes=VMEM_LIMIT_V7X),
    )(proj, proj, proj, x, cw, lg, lb, ws, bmap, wout)


def _ffn_fwd(up, x, cw, wd, *, seq, tm, cwid, name):
    n, d = x.shape
    ff = wd.shape[0]
    prev = _halo_prev(tm)

    def body(up_ref, pup_ref, x_ref, cw_ref, wd_ref, o_ref, conv_ref):
        seq_start = (pl.program_id(0) * tm) % seq == 0

        def conv(lo):
            cols = slice(lo, lo + cwid)
            cur = up_ref[:, cols].astype(F32)
            pre = jnp.where(seq_start, 0.0, pup_ref[:, cols].astype(F32))
            out = (cw_ref[0:1, cols] * _shift_prev(pre, cur, 2) + cw_ref[1:2, cols] * _shift_prev(pre, cur, 1)
                   + cw_ref[2:3, cols] * cur)
            conv_ref[:, cols] = out.astype(BF16)
            return out

        acc = x_ref[...]
        for cj in range(ff // cwid):
            gate = conv(cj * cwid)
            val = conv(ff + cj * cwid)
            a = gate * jax.nn.sigmoid(gate) * val
            acc = acc + jnp.dot(a.astype(BF16), wd_ref[cj * cwid:(cj + 1) * cwid, :], preferred_element_type=F32)
        o_ref[...] = acc

    return pl.pallas_call(
        body, name=name, grid=(n // tm,),
        in_specs=[pl.BlockSpec((tm, 2 * ff), lambda i: (i, 0)),
                  pl.BlockSpec((HALO, 2 * ff), lambda i: (prev(i), 0)),
                  pl.BlockSpec((tm, d), lambda i: (i, 0)),
                  _full((3, 2 * ff)),
                  _resident((ff, d), lambda i: (0, 0))],
        out_specs=[pl.BlockSpec((tm, d), lambda i: (i, 0)), pl.BlockSpec((tm, 2 * ff), lambda i: (i, 0))],
        out_shape=[jax.ShapeDtypeStruct((n, d), F32), jax.ShapeDtypeStruct((n, 2 * ff), BF16)],
        compiler_params=pltpu.CompilerParams(dimension_semantics=("parallel",), vmem_limit_bytes=VMEM_LIMIT_V7X),
    )(up, up, x, cw, wd)


def _final_loss(x, g, target, *, tm, name):
    n, d = x.shape

    def body(x_ref, g_ref, t_ref, dx_ref, loss_ref, dg_ref):
        @pl.when(pl.program_id(0) == 0)
        def _():
            loss_ref[...] = jnp.zeros_like(loss_ref)
            dg_ref[...] = jnp.zeros_like(dg_ref)

        xf = x_ref[...]
        r = lax.rsqrt(_lanemean(xf * xf) + EPS)
        xhat = xf * r
        diff = xhat * g_ref[...] - t_ref[...]
        loss_ref[...] += (0.5 / d) * _rowsum(jnp.sum(diff * diff, axis=-1, keepdims=True))
        dy = diff * (1.0 / d)
        dg_ref[...] += _rowsum(dy * xhat)
        dyh = dy * g_ref[...]
        dx_ref[...] = r * (dyh - xhat * _lanemean(dyh * xhat))

    return pl.pallas_call(
        body, name=name, grid=(n // tm,),
        in_specs=[pl.BlockSpec((tm, d), lambda i: (i, 0)), _full((1, d)), pl.BlockSpec((tm, d), lambda i: (i, 0))],
        out_specs=[pl.BlockSpec((tm, d), lambda i: (i, 0)), _full((HALO, CHUNK)), _full((1, d))],
        out_shape=[jax.ShapeDtypeStruct((n, d), F32), jax.ShapeDtypeStruct((HALO, CHUNK), F32),
                   jax.ShapeDtypeStruct((1, d), F32)],
        compiler_params=pltpu.CompilerParams(dimension_semantics=("arbitrary",), vmem_limit_bytes=VMEM_LIMIT_V7X),
    )(x, g, target)


def _ffn_bwd(dx, up, conv, cw, wd, *, seq, tm, cwid, name):
    n, d = dx.shape
    ff = wd.shape[0]
    nxt = _halo_next(tm, n)

    def body(dx_ref, dxn_ref, up_ref, conv_ref, nconv_ref, cw_ref, wd_ref, dup_ref, dwd_ref, dcw_ref):
        i = pl.program_id(0)
        keep_next = jnp.where(((i + 1) * tm) % seq == 0, 0.0, 1.0)

        @pl.when(i == 0)
        def _():
            dwd_ref[...] = jnp.zeros_like(dwd_ref)
            dcw_ref[...] = jnp.zeros_like(dcw_ref)

        dxe = jnp.concatenate([dx_ref[...], dxn_ref[...]], axis=0).astype(BF16)
        dxb = dxe[:tm]
        for cj in range(ff // cwid):
            rows = slice(cj * cwid, (cj + 1) * cwid)
            g_cols, v_cols = slice(cj * cwid, (cj + 1) * cwid), slice(ff + cj * cwid, ff + (cj + 1) * cwid)
            dae = lax.dot_general(dxe, wd_ref[rows, :], NT_DIMS, preferred_element_type=F32)
            da, dan = dae[:tm], dae[tm:]

            def grads(gate, val, da_rows):
                sg = jax.nn.sigmoid(gate)
                sl = gate * sg
                return sl, da_rows * val * sg * (1.0 + gate * (1.0 - sg)), da_rows * sl

            gate, val = conv_ref[:, g_cols].astype(F32), conv_ref[:, v_cols].astype(F32)
            sl, d_gate, d_val = grads(gate, val, da)
            dwd_ref[rows, :] += lax.dot_general((sl * val).astype(BF16), dxb, TN_DIMS, preferred_element_type=F32)
            _, d_gate_n, d_val_n = grads(nconv_ref[:, g_cols].astype(F32), nconv_ref[:, v_cols].astype(F32),
                                         dan * keep_next)
            for cols, dcur, dnext in ((g_cols, d_gate, d_gate_n), (v_cols, d_val, d_val_n)):
                upc = up_ref[:, cols].astype(F32)
                d1, d2 = _shift_next(dcur, dnext, 1), _shift_next(dcur, dnext, 2)
                for k, dk in enumerate((d2, d1, dcur)):
                    dcw_ref[k:k + 1, cols] += _rowsum(dk * upc)
                dup = cw_ref[2:3, cols] * dcur + cw_ref[1:2, cols] * d1 + cw_ref[0:1, cols] * d2
                dup_ref[:, cols] = dup.astype(BF16)

    return pl.pallas_call(
        body, name=name, grid=(n // tm,),
        in_specs=[pl.BlockSpec((tm, d), lambda i: (i, 0)),
                  pl.BlockSpec((HALO, d), lambda i: (nxt(i), 0)),
                  pl.BlockSpec((tm, 2 * ff), lambda i: (i, 0)),
                  pl.BlockSpec((tm, 2 * ff), lambda i: (i, 0)),
                  pl.BlockSpec((HALO, 2 * ff), lambda i: (nxt(i), 0)),
                  _full((3, 2 * ff)),
                  _resident((ff, d), lambda i: (0, 0))],
        out_specs=[pl.BlockSpec((tm, 2 * ff), lambda i: (i, 0)), _full((ff, d)), _full((3, 2 * ff))],
        out_shape=[jax.ShapeDtypeStruct((n, 2 * ff), BF16), jax.ShapeDtypeStruct((ff, d), F32),
                   jax.ShapeDtypeStruct((3, 2 * ff), F32)],
        compiler_params=pltpu.CompilerParams(dimension_semantics=("arbitrary",), vmem_limit_bytes=VMEM_LIMIT_V7X),
    )(dx, dx, up, conv, conv, cw, wd)


def _matmul_bwd_x(dy, wg, x, g, dres, *, tm, name, jobs=()):
    n, d = x.shape
    n_chip, _, ck = wg.shape

    def main(ins, outs, _):
        dy_ref, w_ref, x_ref, g_ref, dres_ref = ins
        dx_ref, dg_ref = outs

        @pl.when(pl.program_id(0) == 0)
        def _():
            dg_ref[...] = jnp.zeros_like(dg_ref)

        dh = lax.dot_general(dy_ref[:, 0:ck], w_ref[0], NT_DIMS, preferred_element_type=F32)
        for j in range(1, n_chip):
            dh = dh + lax.dot_general(dy_ref[:, j * ck:(j + 1) * ck], w_ref[j], NT_DIMS, preferred_element_type=F32)
        xf = x_ref[...]
        r = lax.rsqrt(_lanemean(xf * xf) + EPS)
        xhat = xf * r
        dg_ref[...] += _rowsum(dh * xhat)
        dyh = dh * g_ref[...]
        dx_ref[...] = dres_ref[...] + r * (dyh - xhat * _lanemean(dyh * xhat))

    return _hosted_call(
        main, name=name, grid=(n // tm,), jobs=jobs, semantics=("arbitrary",), operands=(dy, wg, x, g, dres),
        in_specs=[pl.BlockSpec((tm, n_chip * ck), lambda i: (i, 0)), _resident((n_chip, d, ck), lambda i: (0, 0, 0)),
                  pl.BlockSpec((tm, d), lambda i: (i, 0)), _full((1, d)), pl.BlockSpec((tm, d), lambda i: (i, 0))],
        out_specs=[pl.BlockSpec((tm, d), lambda i: (i, 0)), _full((1, d))],
        out_shape=[jax.ShapeDtypeStruct((n, d), F32), jax.ShapeDtypeStruct((1, d), F32)])


def _matmul_bwd_w(ht, dy, n_chip, *, tk, name, jobs=()):
    d, n = ht.shape
    ck = dy.shape[1] // n_chip

    def main(ins, outs, _):
        ht_ref, dy_ref = ins
        o_ref, = outs

        @pl.when(pl.program_id(1) == 0)
        def _():
            o_ref[...] = jnp.zeros_like(o_ref)

        o_ref[...] += jnp.dot(ht_ref[...], dy_ref[...], preferred_element_type=F32)

    return _hosted_call(
        main, name=name, grid=(n_chip, n // tk), jobs=jobs, semantics=("parallel", "arbitrary"), operands=(ht, dy),
        in_specs=[pl.BlockSpec((d, tk), lambda j, k: (0, k)), pl.BlockSpec((tk, ck), lambda j, k: (k, j))],
        out_specs=[pl.BlockSpec((None, d, ck), lambda j, k: (j, 0, 0))],
        out_shape=[jax.ShapeDtypeStruct((n_chip, d, ck), F32)])


def _mixer_bwd(dx, proj, cw, lg, lb, ws, bmap, wout, *, seq, tm, name):
    n, d = dx.shape
    groups = ws.shape[0]
    gd = d // groups
    prev, nxt = _halo_prev(tm), _halo_next(tm, n)
    n_tiles = n // tm

    def body(proj_ref, pcg_ref, pxi_ref, nbg_ref, nga_ref, dx_ref, dxn_ref, cw_ref, lg_ref, lb_ref, ws_ref,
             bmap_ref, wout_ref, dproj_ref, dwout_ref, dcw_ref, dlg_ref, dlb_ref, dws_ref, dbs_ref,
             vn_s, mixed_s, dmix_s, dvn_s, dbmap_s):
        i = pl.program_id(0)
        seq_start = (i * tm) % seq == 0
        keep_next = jnp.where(((i + 1) * tm) % seq == 0, 0.0, 1.0)

        @pl.when(i == 0)
        def _():
            for ref in (dwout_ref, dcw_ref, dlg_ref, dlb_ref, dws_ref, dbmap_s):
                ref[...] = jnp.zeros_like(ref)

        def piece(k):
            return proj_ref[:, k * d:(k + 1) * d].astype(F32)

        def put(k, val):
            dproj_ref[:, k * d:(k + 1) * d] = val.astype(BF16)

        w = [cw_ref[k:k + 1, :] for k in range(3)]
        cg, xi = piece(1), piece(2)
        z = cg * xi
        zprev = jnp.where(seq_start, 0.0, pcg_ref[...].astype(F32) * pxi_ref[...].astype(F32))
        z2, z1 = _shift_prev(zprev, z, 2), _shift_prev(zprev, z, 1)
        cz = w[0] * z2 + w[1] * z1 + w[2] * z
        bg = piece(0)
        ya = bg * cz

        v = piece(4)
        xc = v - _lanemean(v)
        rstd = lax.rsqrt(_lanemean(xc * xc) + EPS)
        vhat = xc * rstd
        vn_s[...] = (vhat * lg_ref[...] + lb_ref[...]).astype(BF16)
        tril, wsm = _tril_weights(ws_ref, groups)
        for ck in range(tm // CHUNK):
            rows = slice(ck * CHUNK, (ck + 1) * CHUNK)
            for g in range(groups):
                cols = slice(g * gd, (g + 1) * gd)
                mixed_s[rows, cols] = (jnp.dot(wsm[g].astype(BF16), vn_s[rows, cols], preferred_element_type=F32)
                                       + bmap_ref[:, cols])
        u = piece(3)
        mixed = mixed_s[...]
        yb = u * mixed
        sa, sb = jax.nn.sigmoid(piece(5)), jax.nn.sigmoid(piece(6))
        merged = sa * ya + sb * yb

        dxe = jnp.concatenate([dx_ref[...], dxn_ref[...]], axis=0).astype(BF16)
        dme = lax.dot_general(dxe, wout_ref[...], NT_DIMS, preferred_element_type=F32)
        dm, dm_n = dme[:tm], dme[tm:]
        dwout_ref[...] += lax.dot_general(merged.astype(BF16), dxe[:tm], TN_DIMS, preferred_element_type=F32)

        put(5, dm * ya * sa * (1.0 - sa))
        put(6, dm * yb * sb * (1.0 - sb))
        d_ya, d_yb = dm * sa, dm * sb
        put(0, d_ya * cz)
        d_cz = d_ya * bg
        for k, tap in enumerate((z2, z1, z)):
            dcw_ref[k:k + 1, :] += _rowsum(d_cz * tap)
        d_cz_n = dm_n * jax.nn.sigmoid(nga_ref[...].astype(F32)) * nbg_ref[...].astype(F32) * keep_next
        dz = w[2] * d_cz + w[1] * _shift_next(d_cz, d_cz_n, 1) + w[0] * _shift_next(d_cz, d_cz_n, 2)
        put(1, dz * xi)
        put(2, dz * cg)

        put(3, d_yb * mixed)
        d_mixed = d_yb * u
        dmix_s[...] = d_mixed.astype(BF16)
        for ck in range(tm // CHUNK):
            rows = slice(ck * CHUNK, (ck + 1) * CHUNK)
            dbmap_s[...] += d_mixed[rows, :]
            for g in range(groups):
                cols = slice(g * gd, (g + 1) * gd)
                dvn_s[rows, cols] = jnp.dot(wsm[g].T.astype(BF16), dmix_s[rows, cols], preferred_element_type=F32)
                dws_ref[g] += jnp.where(
                    tril, lax.dot_general(dmix_s[rows, cols], vn_s[rows, cols], NT_DIMS, preferred_element_type=F32),
                    0.0)
        d_vn = dvn_s[...]
        dlg_ref[...] += _rowsum(d_vn * vhat)
        dlb_ref[...] += _rowsum(d_vn)
        d_vhat = d_vn * lg_ref[...]
        put(4, rstd * (d_vhat - _lanemean(d_vhat) - vhat * _lanemean(d_vhat * vhat)))

        @pl.when(i == n_tiles - 1)
        def _():
            for g in range(groups):
                dbs_ref[:, g:g + 1] = jnp.sum(dbmap_s[:, g * gd:(g + 1) * gd], axis=-1, keepdims=True)

    return pl.pallas_call(
        body, name=name, grid=(n_tiles,),
        in_specs=[pl.BlockSpec((tm, N_PIECES * d), lambda i: (i, 0)),
                  pl.BlockSpec((HALO, d), lambda i: (prev(i), 1)),
                  pl.BlockSpec((HALO, d), lambda i: (prev(i), 2)),
                  pl.BlockSpec((HALO, d), lambda i: (nxt(i), 0)),
                  pl.BlockSpec((HALO, d), lambda i: (nxt(i), 5)),
                  pl.BlockSpec((tm, d), lambda i: (i, 0)),
                  pl.BlockSpec((HALO, d), lambda i: (nxt(i), 0)),
                  _full((3, d)), _full((1, d)), _full((1, d)), _full((groups, CHUNK, CHUNK)), _full((CHUNK, d)),
                  _resident((d, d), lambda i: (0, 0))],
        out_specs=[pl.BlockSpec((tm, N_PIECES * d), lambda i: (i, 0)), _full((d, d)),
                   _full((3, d)), _full((1, d)), _full((1, d)), _full((groups, CHUNK, CHUNK)), _full((CHUNK, groups))],
        out_shape=[jax.ShapeDtypeStruct((n, N_PIECES * d), BF16), jax.ShapeDtypeStruct((d, d), F32),
                   jax.ShapeDtypeStruct((3, d), F32), jax.ShapeDtypeStruct((1, d), F32),
                   jax.ShapeDtypeStruct((1, d), F32), jax.ShapeDtypeStruct((groups, CHUNK, CHUNK), F32),
                   jax.ShapeDtypeStruct((CHUNK, groups), F32)],
        scratch_shapes=[pltpu.VMEM((tm, d), BF16), pltpu.VMEM((tm, d), F32), pltpu.VMEM((tm, d), BF16),
                        pltpu.VMEM((tm, d), F32), pltpu.VMEM((CHUNK, d), F32)],
        compiler_params=pltpu.CompilerParams(dimension_semantics=("arbitrary",), vmem_limit_bytes=VMEM_LIMIT_V7X),
    )(proj, proj, proj, proj, proj, dx, dx, cw, lg, lb, ws, bmap, wout)


def _gather_small(pack, name):
    def body(p_ref, o_ref, send_sem, recv_sem):
        x, y, c = _mesh_pos()
        chip = 2 * x + y
        o_ref[chip] = p_ref[...]
        copies = []
        for r in (1, 2, 3):
            px, py = _peer_chip(x, y, r)
            copies.append(_remote(p_ref, o_ref.at[chip], (send_sem, recv_sem), r - 1, (px, py, c)))
        for cp in copies:
            cp.start()
        for r, cp in zip((1, 2, 3), copies):
            px, py = _peer_chip(x, y, r)
            landed = o_ref.at[2 * px + py]
            _remote(landed, landed, (send_sem, recv_sem), r - 1, (px, py, c)).wait_recv()
            cp.wait_send()

    return pl.pallas_call(
        body, name=name, in_specs=[VMEM_SPEC], out_specs=VMEM_SPEC,
        out_shape=jax.ShapeDtypeStruct((N_CHIP,) + pack.shape, pack.dtype),
        scratch_shapes=[pltpu.SemaphoreType.DMA((3,)), pltpu.SemaphoreType.DMA((3,))],
    )(pack)


def _all_reduce_small(pack, name, jobs=()):
    rows = pack.shape[0]

    def main(ins, outs, scratch):
        p_ref, = ins
        o_ref, = outs
        sib_buf, chip_buf, send_sem, recv_sem = scratch
        sems = (send_sem, recv_sem)
        x, y, c = _mesh_pos()
        chip = 2 * x + y
        swap = _remote(p_ref, sib_buf, sems, 0, (x, y, 1 - c))
        swap.start()
        swap.wait()
        chip_buf[chip] = p_ref[...] + sib_buf[...]
        copies = []
        for r in (1, 2, 3):
            px, py = _peer_chip(x, y, r)
            copies.append(_remote(chip_buf.at[chip], chip_buf.at[chip], sems, r, (px, py, c)))
        for cp in copies:
            cp.start()
        for r, cp in zip((1, 2, 3), copies):
            px, py = _peer_chip(x, y, r)
            landed = chip_buf.at[2 * px + py]
            _remote(landed, landed, sems, r, (px, py, c)).wait_recv()
            cp.wait_send()
        o_ref[...] = ((chip_buf[0] + chip_buf[1]) + chip_buf[2]) + chip_buf[3]

    outs, job_outs = _hosted_call(
        main, name=name, grid=(1,), jobs=jobs, operands=(pack,),
        in_specs=[_full(pack.shape)], out_specs=[_full(pack.shape)],
        out_shape=[jax.ShapeDtypeStruct(pack.shape, pack.dtype)],
        scratch_shapes=[pltpu.VMEM((rows, 128), F32), pltpu.VMEM((N_CHIP, rows, 128), F32),
                        pltpu.SemaphoreType.DMA((4,)), pltpu.SemaphoreType.DMA((4,))])
    return outs[0], job_outs


def _block_rows(rows, cols, itemsize=4, budget=2 << 20):
    best = None
    for t in range(16, rows + 1, 16):
        if rows % t == 0 and t * cols * itemsize <= budget:
            best = t
    assert best is not None, (rows, cols)
    return best


def _pair_sum(grad, recv, core, name):
    n_chip, rk, ck = grad.shape
    rh = rk // 2
    tr = _block_rows(rh, ck)
    nb = rh // tr

    def body(core_ref, g_ref, r_ref, o_ref):
        o_ref[...] = (g_ref[...] + r_ref[...]).astype(BF16)

    blk = (None, tr, ck)
    return pl.pallas_call(
        body, name=name,
        grid_spec=pltpu.PrefetchScalarGridSpec(
            num_scalar_prefetch=1, grid=(n_chip, nb),
            in_specs=[pl.BlockSpec(blk, lambda k, i, core_ref: (k, core_ref[0] * nb + i, 0)),
                      pl.BlockSpec(blk, lambda k, i, core_ref: (k, i, 0))],
            out_specs=pl.BlockSpec(blk, lambda k, i, core_ref: (k, i, 0))),
        out_shape=jax.ShapeDtypeStruct((n_chip, rh, ck), BF16),
        compiler_params=pltpu.CompilerParams(dimension_semantics=("parallel", "parallel"),
                                             vmem_limit_bytes=VMEM_LIMIT_V7X),
    )(core, grad, recv)


def _final_sum(grad, recv, arrived, where, prev, layer, n_layers, name):
    n_chip, rk, ck = grad.shape
    rh = rk // 2
    tr = _block_rows(rh, ck)
    nb = rh // tr

    def body(where_ref, g_ref, r_ref, a1_ref, a2_ref, a3_ref, *rest):
        o_ref = rest[-1]
        own = g_ref[...] + r_ref[...]
        o_ref[...] = ((own + a1_ref[...].astype(F32)) + a2_ref[...].astype(F32)) + a3_ref[...].astype(F32)

    blk = (None, tr, ck)
    slot = lambda r: pl.BlockSpec(blk, lambda i, w: (r, i, 0))
    extra = [] if prev is None else [prev]
    return pl.pallas_call(
        body, name=name,
        grid_spec=pltpu.PrefetchScalarGridSpec(
            num_scalar_prefetch=1, grid=(nb,),
            in_specs=[pl.BlockSpec(blk, lambda i, w: (w[1], w[0] * nb + i, 0)),
                      pl.BlockSpec(blk, lambda i, w: (w[1], i, 0)),
                      slot(0), slot(1), slot(2)] + [pl.BlockSpec(memory_space=pl.ANY)] * len(extra),
            out_specs=pl.BlockSpec(blk, lambda i, w: (layer, w[0] * nb + i, 0))),
        out_shape=jax.ShapeDtypeStruct((n_layers, rk, ck), F32),
        input_output_aliases={6: 0} if extra else {},
        compiler_params=pltpu.CompilerParams(dimension_semantics=("parallel",), vmem_limit_bytes=VMEM_LIMIT_V7X),
    )(where, grad, recv, arrived, arrived, arrived, *extra)


def _adamw_math(w, g, m, v):
    m = ADAM_B1 * m + (1.0 - ADAM_B1) * g
    v = ADAM_B2 * v + (1.0 - ADAM_B2) * (g * g)
    m_hat = m / (1.0 - ADAM_B1 ** ADAM_STEP)
    v_hat = v / (1.0 - ADAM_B2 ** ADAM_STEP)
    delta = -ADAM_LR * (m_hat / (jnp.sqrt(v_hat) + ADAM_EPS) + ADAM_WD * w)
    return delta, m, v


def _adamw_big(w, g, m, v, name):
    n_l, rk, ck = w.shape
    tr = _block_rows(rk, ck, budget=1 << 20)

    def body(w_ref, g_ref, m_ref, v_ref, d_ref, nm_ref, nv_ref):
        d_ref[...], nm_ref[...], nv_ref[...] = _adamw_math(w_ref[...], g_ref[...], m_ref[...], v_ref[...])

    spec = pl.BlockSpec((None, tr, ck), lambda l, i: (l, i, 0))
    return pl.pallas_call(
        body, name=name, grid=(n_l, rk // tr), in_specs=[spec] * 4, out_specs=[spec] * 3,
        out_shape=[jax.ShapeDtypeStruct(w.shape, F32)] * 3,
        compiler_params=pltpu.CompilerParams(dimension_semantics=("parallel", "parallel"),
                                             vmem_limit_bytes=VMEM_LIMIT_V7X),
    )(w, g, m, v)


def _adamw_small(ws, gs, ms, vs, name):
    n_p = len(ws)

    def body(*refs):
        ins, outs = refs[:4 * n_p], refs[4 * n_p:]
        for p in range(n_p):
            res = _adamw_math(ins[p][...], ins[n_p + p][...], ins[2 * n_p + p][...], ins[3 * n_p + p][...])
            for q in range(3):
                outs[q * n_p + p][...] = res[q]

    outs = pl.pallas_call(
        body, name=name, in_specs=[VMEM_SPEC] * (4 * n_p), out_specs=[VMEM_SPEC] * (3 * n_p),
        out_shape=[jax.ShapeDtypeStruct(w.shape, F32) for w in ws] * 3,
        compiler_params=pltpu.CompilerParams(vmem_limit_bytes=VMEM_LIMIT_V7X),
    )(*ws, *gs, *ms, *vs)
    return outs[:n_p], outs[n_p:2 * n_p], outs[2 * n_p:]


def kernel(x, mix_norm_g, w_in, conv_a_w, ln_v_g, ln_v_b, w_s, b_s, w_out, ffn_norm_g, w_up, conv_ffn_w, w_down, final_norm_g, loss_target, m_mix_norm_g, m_w_in, m_conv_a_w, m_ln_v_g, m_ln_v_b, m_w_s, m_b_s, m_w_out, m_ffn_norm_g, m_w_up, m_conv_ffn_w, m_w_down, m_final_norm_g, v_mix_norm_g, v_w_in, v_conv_a_w, v_ln_v_g, v_ln_v_b, v_w_s, v_b_s, v_w_out, v_ffn_norm_g, v_w_up, v_conv_ffn_w, v_w_down, v_final_norm_g):
    bsz, seq, d = x.shape
    n = bsz * seq
    n_l, groups = w_s.shape[0], w_s.shape[1]
    assert n_l == 2, "the exchange schedule below is written for two layers"
    gd = d // groups
    ff = w_down.shape[1] * N_CHIP
    axes = ("x", "y", "c")
    mx, my, mc = _mesh_pos()
    chip = 2 * mx + my
    core = jnp.reshape(mc, (1,)).astype(jnp.int32)
    where = jnp.stack([mc, chip]).astype(jnp.int32)

    tm_mm = _row_tile(seq, 512)
    tm_ew = _row_tile(seq, 256)
    tk_w = _row_tile(seq, 2048)
    ff_chunk = ff // 2 if (ff // 2) % 128 == 0 else ff
    IN, OUT, UP, DOWN = range(4)

    shards = [w.astype(BF16) for w in (w_in, w_out, w_up, w_down)]
    piece = lambda a, l: (shards[a], l)
    (w_in_0,), = _comm_only([_GatherJob([piece(IN, 0)])], "gather_first")

    taps = jnp.concatenate([conv_a_w.reshape(n_l, -1), conv_ffn_w.reshape(n_l, -1)], axis=1)
    tap_rows = -(-taps.size // 128 // 8) * 8
    tap_pack = jnp.zeros((tap_rows * 128,), F32).at[:taps.size].set(taps.reshape(-1)).reshape(tap_rows, 128)
    tap_all = _gather_small(tap_pack, "gather_taps")
    tap_all = tap_all.reshape(N_CHIP, -1)[:, :taps.size].reshape(N_CHIP, n_l, -1)
    ca = tap_all[:, :, :3 * d // N_CHIP].reshape(N_CHIP, n_l, 3, d // N_CHIP)
    cf = tap_all[:, :, 3 * d // N_CHIP:].reshape(N_CHIP, n_l, 3, 2 * ff // N_CHIP)
    conv_a_full = jnp.transpose(ca, (1, 2, 0, 3)).reshape(n_l, 3, d)
    conv_f_full = jnp.transpose(cf, (1, 2, 0, 3)).reshape(n_l, 3, 2 * ff)

    bmaps = jnp.repeat(jnp.swapaxes(b_s, 1, 2), gd, axis=2)

    xs = x.reshape(n, d)
    tgt = loss_target.reshape(n, d)

    def mixer_args(l, w_out_l):
        return (conv_a_full[l], ln_v_g[l][None], ln_v_b[l][None], w_s[l], bmaps[l], w_out_l.reshape(d, d))

    (proj0, h1_0), ((w_out_0, w_up_0, w_down_0),) = _norm_matmul(
        xs, mix_norm_g[0][None], w_in_0, tm=tm_mm, name="fwd_in_proj_0",
        jobs=[_GatherJob([piece(OUT, 0), piece(UP, 0), piece(DOWN, 0)])])
    x1_0 = _mixer_fwd(proj0, xs, *mixer_args(0, w_out_0), seq=seq, tm=tm_ew, name="fwd_mixer_0")
    (up0, h2_0), ((w_in_1, w_out_1),) = _norm_matmul(
        x1_0, ffn_norm_g[0][None], w_up_0, tm=tm_mm, name="fwd_up_proj_0",
        jobs=[_GatherJob([piece(IN, 1), piece(OUT, 1)])])
    x2_0, conv0 = _ffn_fwd(up0, x1_0, conv_f_full[0], w_down_0.reshape(ff, d), seq=seq, tm=tm_ew, cwid=ff_chunk,
                           name="fwd_ffn_0")
    (proj1, h1_1), ((w_up_1, w_down_1),) = _norm_matmul(
        x2_0, mix_norm_g[1][None], w_in_1, tm=tm_mm, name="fwd_in_proj_1",
        jobs=[_GatherJob([piece(UP, 1), piece(DOWN, 1)])])
    x1_1 = _mixer_fwd(proj1, x2_0, *mixer_args(1, w_out_1), seq=seq, tm=tm_ew, name="fwd_mixer_1")
    (up1, h2_1), _ = _norm_matmul(x1_1, ffn_norm_g[1][None], w_up_1, tm=tm_mm, name="fwd_up_proj_1")
    x2_1, conv1 = _ffn_fwd(up1, x1_1, conv_f_full[1], w_down_1.reshape(ff, d), seq=seq, tm=tm_ew, cwid=ff_chunk,
                           name="fwd_ffn_1")
    dx, loss_tile, d_final_g = _final_loss(x2_1, final_norm_g[None], tgt, tm=tm_mm, name="final_loss")
    loss = lax.psum(loss_tile[0, 0], axes)

    def chipwise(a):
        return a.reshape(N_CHIP, a.shape[0] // N_CHIP, a.shape[1])

    def pair_sums(grads, recvs, tag):
        return [_pair_sum(g, r, core, f"grad_pair_sum_{tag}_{a}") for a, (g, r) in enumerate(zip(grads, recvs))]

    d_up, dwd, d_cf1 = _ffn_bwd(dx, up1, conv1, conv_f_full[1], w_down_1.reshape(ff, d), seq=seq, tm=tm_ew,
                                cwid=ff_chunk, name="bwd_ffn_1")
    (dx1, d_g2_1), _ = _matmul_bwd_x(d_up, w_up_1, x1_1, ffn_norm_g[1][None], dx, tm=tm_mm, name="bwd_up_x_1")
    (dwu,), _ = _matmul_bwd_w(h2_1, d_up, N_CHIP, tk=tk_w, name="bwd_up_w_1")
    d_proj, dwo, d_ca1, d_lg1, d_lb1, d_ws1, d_bs1 = _mixer_bwd(
        dx1, proj1, *mixer_args(1, w_out_1), seq=seq, tm=tm_ew, name="bwd_mixer_1")
    (dx, d_g1_1), _ = _matmul_bwd_x(d_proj, w_in_1, x2_0, mix_norm_g[1][None], dx1, tm=tm_mm, name="bwd_in_x_1")
    (dwi,), _ = _matmul_bwd_w(h1_1, d_proj, N_CHIP, tk=tk_w, name="bwd_in_w_1")
    grads1 = [dwi, chipwise(dwo), dwu, chipwise(dwd)]
    small1 = [d_g1_1, d_ca1, d_lg1, d_lb1, d_ws1, d_bs1.T, d_g2_1, d_cf1]

    d_up, dwd, d_cf0 = _ffn_bwd(dx, up0, conv0, conv_f_full[0], w_down_0.reshape(ff, d), seq=seq, tm=tm_ew,
                                cwid=ff_chunk, name="bwd_ffn_0")
    (dx1, d_g2_0), (recv1,) = _matmul_bwd_x(d_up, w_up_0, x1_0, ffn_norm_g[0][None], dx, tm=tm_mm, name="bwd_up_x_0",
                                            jobs=[_SwapJob(grads1)])
    parts1 = pair_sums(grads1, recv1, 1)
    (dwu,), ((arr1_in,),) = _matmul_bwd_w(h2_0, d_up, N_CHIP, tk=tk_w, name="bwd_up_w_0",
                                          jobs=[_ScatterJob([parts1[IN]])])
    d_proj, dwo, d_ca0, d_lg0, d_lb0, d_ws0, d_bs0 = _mixer_bwd(
        dx1, proj0, *mixer_args(0, w_out_0), seq=seq, tm=tm_ew, name="bwd_mixer_0")
    early0 = [chipwise(dwo), dwu, chipwise(dwd)]
    (dwi,), (arr1_rest, recv0_early) = _matmul_bwd_w(
        h1_0, d_proj, N_CHIP, tk=tk_w, name="bwd_in_w_0",
        jobs=[_ScatterJob(parts1[OUT:]), _SwapJob(early0)])
    arrived1 = [arr1_in] + list(arr1_rest)
    gsum = [_final_sum(g, r, p, where, None, 1, n_l, f"grad_final_sum_1_{a}")
            for a, (g, r, p) in enumerate(zip(grads1, recv1, arrived1))]
    parts0_early = pair_sums(early0, recv0_early, "0e")
    (dx, d_g1_0), (gsum, arr0_early, (recv0_in,)) = _matmul_bwd_x(
        d_proj, w_in_0, xs, mix_norm_g[0][None], dx1, tm=tm_mm, name="bwd_in_x_0",
        jobs=[_ShareJob([(g, 1) for g in gsum]), _ScatterJob(parts0_early), _SwapJob([dwi])])
    grad_x = dx.reshape(bsz, seq, d)
    grads0 = [dwi] + early0
    recv0 = [recv0_in] + list(recv0_early)
    part0_in, = pair_sums([dwi], [recv0_in], "0i")
    small0 = [d_g1_0, d_ca0, d_lg0, d_lb0, d_ws0, d_bs0.T, d_g2_0, d_cf0]

    flat = [a.reshape(-1) for a in small0 + small1] + [d_final_g.reshape(-1)]
    sizes = [a.size for a in flat]
    total = sum(sizes)
    rows = -(-total // 128 // 8) * 8
    pack = jnp.concatenate(flat + [jnp.zeros((rows * 128 - total,), F32)]).reshape(rows, 128)
    red, ((arr0_in,),) = _all_reduce_small(pack, "small_all_reduce", jobs=[_ScatterJob([part0_in])])
    red = red.reshape(-1)
    arrived0 = [arr0_in] + list(arr0_early)
    gsum = [_final_sum(g, r, p, where, prev, 0, n_l, f"grad_final_sum_0_{a}")
            for a, (g, r, p, prev) in enumerate(zip(grads0, recv0, arrived0, gsum))]
    (g_in, g_out, g_up, g_down), = _comm_only([_ShareJob([(g, 0) for g in gsum])], "grad_share_0")

    pieces, off = [], 0
    for s in sizes:
        pieces.append(red[off:off + s])
        off += s
    per_layer = [pieces[l * 8:(l + 1) * 8] for l in range(n_l)]

    def stacked(idx, shape):
        return jnp.stack([per_layer[l][idx].reshape(shape) for l in range(n_l)])

    def my_cols(a, width):
        return lax.dynamic_slice_in_dim(a, chip * width, width, axis=-1)

    g_mix_norm = stacked(0, (d,))
    g_conv_a = my_cols(stacked(1, (3, d)), d // N_CHIP)
    g_ln_g = stacked(2, (d,))
    g_ln_b = stacked(3, (d,))
    g_ws = stacked(4, (groups, CHUNK, CHUNK))
    g_bs = stacked(5, (groups, CHUNK))
    g_ffn_norm = stacked(6, (d,))
    g_conv_f = my_cols(stacked(7, (3, 2 * ff)), 2 * ff // N_CHIP)
    g_final = pieces[-1].reshape(1, d)

    big_names = [(w_in, g_in, m_w_in, v_w_in), (w_out, g_out, m_w_out, v_w_out), (w_up, g_up, m_w_up, v_w_up),
                 (w_down, g_down, m_w_down, v_w_down)]
    upd = [_adamw_big(w, g, m, v, f"adamw_{a}") for a, (w, g, m, v) in enumerate(big_names)]

    sm_w = [mix_norm_g, conv_a_w, ln_v_g, ln_v_b, w_s, b_s, ffn_norm_g, conv_ffn_w, final_norm_g[None]]
    sm_g = [g_mix_norm, g_conv_a, g_ln_g, g_ln_b, g_ws, g_bs, g_ffn_norm, g_conv_f, g_final]
    sm_m = [m_mix_norm_g, m_conv_a_w, m_ln_v_g, m_ln_v_b, m_w_s, m_b_s, m_ffn_norm_g, m_conv_ffn_w, m_final_norm_g[None]]
    sm_v = [v_mix_norm_g, v_conv_a_w, v_ln_v_g, v_ln_v_b, v_w_s, v_b_s, v_ffn_norm_g, v_conv_ffn_w, v_final_norm_g[None]]
    sm_d, sm_nm, sm_nv = _adamw_small(sm_w, sm_g, sm_m, sm_v, "adamw_small")

    def ordered(sm, bigs):
        return [sm[0], bigs[0], sm[1], sm[2], sm[3], sm[4], sm[5], bigs[1], sm[6], bigs[2], sm[7], bigs[3],
                sm[8].reshape(d)]

    out_g = ordered(sm_g, [g_in, g_out, g_up, g_down])
    out_d = ordered(sm_d, [u[0] for u in upd])
    out_m = ordered(sm_nm, [u[1] for u in upd])
    out_v = ordered(sm_nv, [u[2] for u in upd])
    return (loss, grad_x, *out_g, *out_d, *out_m, *out_v)
```

```python
import jax
import jax.numpy as jnp
from jax import lax
from jax.experimental import pallas as pl
from jax.experimental.pallas import tpu as pltpu

F32 = jnp.float32
BF16 = jnp.bfloat16
EPS = 1e-6
CHUNK = 128
N_CHIP = 4
HALO = 8
N_PIECES = 7
VMEM_LIMIT_V7X = 56 * 1024 * 1024
MESH_T = pl.DeviceIdType.MESH
HBM_SPEC = pl.BlockSpec(memory_space=pltpu.HBM)
VMEM_SPEC = pl.BlockSpec(memory_space=pltpu.VMEM)

ADAM_LR, ADAM_B1, ADAM_B2, ADAM_EPS, ADAM_WD, ADAM_STEP = 0.001, 0.9, 0.999, 1e-08, 0.01, 10

NT_DIMS = (((1,), (1,)), ((), ()))
TN_DIMS = (((0,), (0,)), ((), ()))


def _resident(block_shape, index_map):
    return pl.BlockSpec(block_shape, index_map, pipeline_mode=pl.Buffered(1))


def _full(shape):
    return pl.BlockSpec(shape, lambda *_: (0,) * len(shape))


def _row_tile(seq, want):
    t = min(seq, want)
    assert seq % t == 0 and t % CHUNK == 0, (seq, want)
    return t


def _shift_prev(prev8, cur, k):
    ext = jnp.concatenate([prev8, cur], axis=0)
    return pltpu.roll(ext, k, 0)[HALO:]


def _shift_next(cur, next8, k):
    ext = jnp.concatenate([cur, next8], axis=0)
    n = ext.shape[0]
    return pltpu.roll(ext, n - k, 0)[:n - HALO]


def _rowsum(a):
    return jnp.sum(a, axis=0, keepdims=True)


def _lanemean(a):
    return jnp.mean(a, axis=-1, keepdims=True)


def _tril_weights(ws_ref, groups):
    r = lax.broadcasted_iota(jnp.int32, (CHUNK, CHUNK), 0)
    c = lax.broadcasted_iota(jnp.int32, (CHUNK, CHUNK), 1)
    tril = r >= c
    return tril, [jnp.where(tril, ws_ref[g], 0.0) for g in range(groups)]


def _halo_prev(tm):
    return lambda i: jnp.maximum(i * (tm // HALO) - 1, 0)


def _halo_next(tm, n):
    last = n // HALO - 1
    return lambda i: jnp.minimum((i + 1) * (tm // HALO), last)


def _mesh_pos():
    return lax.axis_index("x"), lax.axis_index("y"), lax.axis_index("c")


def _peer_chip(x, y, r):
    return (1 - x if r >> 1 else x), (1 - y if r & 1 else y)


def _remote(src, dst, sems, k, to):
    return pltpu.make_async_remote_copy(src_ref=src, dst_ref=dst, send_sem=sems[0].at[k], recv_sem=sems[1].at[k],
                                        device_id=to, device_id_type=MESH_T)


class _GatherJob:
    def __init__(self, pieces):
        self.pieces = pieces
        self.ins = [p[0] for p in pieces]
        self.out_shapes = [jax.ShapeDtypeStruct((N_CHIP,) + p[0].shape[1:], p[0].dtype) for p in pieces]
        self.aliases = {}
        n = len(pieces)
        self.sems = [pltpu.SemaphoreType.DMA((3 * n,))] * 4 + [pltpu.SemaphoreType.DMA((n,))]

    def _half(self, outs, a, of_chip, core):
        rh = outs[a].shape[1] // 2
        return outs[a].at[of_chip, pl.ds(core * rh, rh), :]

    def _own(self, ins, outs, sems, chip):
        return [pltpu.make_async_copy(ins[a].at[layer], outs[a].at[chip], sems[4].at[a])
                for a, (_, layer) in enumerate(self.pieces)]

    def _sends(self, ins, outs, sems, x, y, c):
        chip, out = 2 * x + y, []
        for a, (_, layer) in enumerate(self.pieces):
            rh = outs[a].shape[1] // 2
            for r in (1, 2, 3):
                px, py = _peer_chip(x, y, r)
                out.append(_remote(ins[a].at[layer, pl.ds(c * rh, rh), :], self._half(outs, a, chip, c), sems[0:2],
                                   3 * a + r - 1, (px, py, c)))
        return out

    def _passes(self, outs, sems, x, y, c, core):
        out = []
        for a in range(len(self.pieces)):
            for r in (1, 2, 3):
                px, py = _peer_chip(x, y, r)
                landed = self._half(outs, a, 2 * px + py, core)
                out.append(_remote(landed, landed, sems[2:4], 3 * a + r - 1, (x, y, 1 - c)))
        return out

    def start(self, ins, outs, sems):
        x, y, c = _mesh_pos()
        for cp in self._own(ins, outs, sems, 2 * x + y) + self._sends(ins, outs, sems, x, y, c):
            cp.start()

    def finish(self, ins, outs, sems):
        x, y, c = _mesh_pos()
        sends = self._sends(ins, outs, sems, x, y, c)
        passes = self._passes(outs, sems, x, y, c, c)
        k = 0
        for a in range(len(self.pieces)):
            for r in (1, 2, 3):
                px, py = _peer_chip(x, y, r)
                landed = self._half(outs, a, 2 * px + py, c)
                _remote(landed, landed, sems[0:2], k, (px, py, c)).wait_recv()
                passes[k].start()
                k += 1
        for cp in self._passes(outs, sems, x, y, c, 1 - c):
            cp.wait_recv()
        for cp in sends + passes:
            cp.wait_send()
        for cp in self._own(ins, outs, sems, 2 * x + y):
            cp.wait()


class _SwapJob:
    def __init__(self, pieces):
        self.ins = list(pieces)
        self.out_shapes = [jax.ShapeDtypeStruct((g.shape[0], g.shape[1] // 2, g.shape[2]), g.dtype) for g in pieces]
        self.aliases = {}
        self.sems = [pltpu.SemaphoreType.DMA((len(pieces),))] * 2

    def _copies(self, ins, outs, sems):
        x, y, c = _mesh_pos()
        out = []
        for a in range(len(ins)):
            rh = ins[a].shape[1] // 2
            out.append(_remote(ins[a].at[:, pl.ds((1 - c) * rh, rh), :], outs[a], sems, a, (x, y, 1 - c)))
        return out

    def start(self, ins, outs, sems):
        for cp in self._copies(ins, outs, sems):
            cp.start()

    def finish(self, ins, outs, sems):
        for cp in self._copies(ins, outs, sems):
            cp.wait()


class _ScatterJob:
    def __init__(self, pieces):
        self.ins = list(pieces)
        self.out_shapes = [jax.ShapeDtypeStruct((3,) + p.shape[1:], p.dtype) for p in pieces]
        self.aliases = {}
        self.sems = [pltpu.SemaphoreType.DMA((3 * len(pieces),))] * 2

    def _copies(self, ins, outs, sems):
        x, y, c = _mesh_pos()
        out = []
        for a in range(len(ins)):
            for r in (1, 2, 3):
                px, py = _peer_chip(x, y, r)
                out.append(_remote(ins[a].at[2 * px + py], outs[a].at[r - 1], sems, 3 * a + r - 1, (px, py, c)))
        return out

    def start(self, ins, outs, sems):
        for cp in self._copies(ins, outs, sems):
            cp.start()

    def finish(self, ins, outs, sems):
        for cp in self._copies(ins, outs, sems):
            cp.wait()


class _ShareJob:
    def __init__(self, pieces):
        self.pieces = pieces
        self.ins = [p[0] for p in pieces]
        self.out_shapes = [jax.ShapeDtypeStruct(p[0].shape, p[0].dtype) for p in pieces]
        self.aliases = {a: a for a in range(len(pieces))}
        self.sems = [pltpu.SemaphoreType.DMA((len(pieces),))] * 2

    def _copies(self, outs, sems, core):
        x, y, c = _mesh_pos()
        out = []
        for a, (_, layer) in enumerate(self.pieces):
            rh = outs[a].shape[1] // 2
            rows = outs[a].at[layer, pl.ds(core * rh, rh), :]
            out.append(_remote(rows, rows, sems, a, (x, y, 1 - c)))
        return out

    def start(self, ins, outs, sems):
        for cp in self._copies(outs, sems, lax.axis_index("c")):
            cp.start()

    def finish(self, ins, outs, sems):
        c = lax.axis_index("c")
        for cp in self._copies(outs, sems, c):
            cp.wait_send()
        for cp in self._copies(outs, sems, 1 - c):
            cp.wait_recv()


def _all_of(conds):
    out = conds[0]
    for c in conds[1:]:
        out = jnp.logical_and(out, c)
    return out


def _hosted_call(main, *, name, grid, in_specs, out_specs, out_shape, scratch_shapes=(), jobs=(), semantics=None,
                 operands=()):
    n_in, n_out, n_sc = len(in_specs), len(out_specs), len(scratch_shapes)
    counts = [(len(j.ins), len(j.out_shapes), len(j.sems)) for j in jobs]
    j_in, j_out, j_sc = (sum(c[k] for c in counts) for k in range(3))
    aliases, i0, o0 = {}, n_in, n_out
    for j, (ci, co, _) in zip(jobs, counts):
        aliases.update({i0 + a: o0 + b for a, b in j.aliases.items()})
        i0, o0 = i0 + ci, o0 + co

    def body(*refs):
        cuts = [0, n_in, n_in + j_in, n_in + j_in + n_out, n_in + j_in + n_out + j_out,
                n_in + j_in + n_out + j_out + n_sc, len(refs)]
        m_in, jb_in, m_out, jb_out, m_sc, jb_sc = (list(refs[cuts[k]:cuts[k + 1]]) for k in range(6))

        def run(phase):
            i0 = o0 = s0 = 0
            for j, (ci, co, cs) in zip(jobs, counts):
                getattr(j, phase)(jb_in[i0:i0 + ci], jb_out[o0:o0 + co], jb_sc[s0:s0 + cs])
                i0, o0, s0 = i0 + ci, o0 + co, s0 + cs

        first = _all_of([pl.program_id(ax) == 0 for ax in range(len(grid))])
        last = _all_of([pl.program_id(ax) == grid[ax] - 1 for ax in range(len(grid))])
        if jobs:
            pl.when(first)(lambda: run("start"))
        main(m_in, m_out, m_sc)
        if jobs:
            pl.when(last)(lambda: run("finish"))

    if semantics is None or jobs:
        semantics = ("arbitrary",) * len(grid)
    outs = pl.pallas_call(
        body, name=name, grid=grid,
        in_specs=list(in_specs) + [HBM_SPEC] * j_in,
        out_specs=list(out_specs) + [HBM_SPEC] * j_out,
        out_shape=list(out_shape) + [s for j in jobs for s in j.out_shapes],
        scratch_shapes=list(scratch_shapes) + [s for j in jobs for s in j.sems],
        input_output_aliases=aliases,
        compiler_params=pltpu.CompilerParams(dimension_semantics=semantics, vmem_limit_bytes=VMEM_LIMIT_V7X),
    )(*operands, *[a for j in jobs for a in j.ins])
    main_outs, rest, job_outs = list(outs[:n_out]), list(outs[n_out:]), []
    for _, co, _ in counts:
        job_outs.append(rest[:co])
        rest = rest[co:]
    return main_outs, job_outs


def _weights_in_vmem(w_hbm, scratch):
    w_vmem, sem = scratch
    first = pl.program_id(0) == 0
    copies = [pltpu.make_async_copy(w_hbm.at[j], w_vmem.at[j], sem.at[j]) for j in range(w_vmem.shape[0])]

    @pl.when(first)
    def _():
        for cp in copies:
            cp.start()

    def block(j):
        pl.when(first)(copies[j].wait)
        return w_vmem[j]

    return block


def _comm_only(jobs, name):
    return _hosted_call(lambda i, o, s: None, name=name, grid=(1,), in_specs=[], out_specs=[], out_shape=[],
                        jobs=jobs)[1]


def _norm_matmul(x, g, wg, *, tm, name, jobs=()):
    n, d = x.shape
    n_chip, _, ck = wg.shape

    def main(ins, outs, scratch):
        x_ref, g_ref, w_hbm = ins
        o_ref, ht_ref = outs
        w_ref = _weights_in_vmem(w_hbm, scratch)
        xf = x_ref[...]
        r = lax.rsqrt(_lanemean(xf * xf) + EPS)
        h = xf * r * g_ref[...]
        ht_ref[...] = h.T.astype(BF16)
        hb = h.astype(BF16)
        for j in range(n_chip):
            o_ref[:, j * ck:(j + 1) * ck] = jnp.dot(hb, w_ref(j), preferred_element_type=F32).astype(BF16)

    return _hosted_call(
        main, name=name, grid=(n // tm,), jobs=jobs, semantics=("arbitrary",), operands=(x, g, wg),
        in_specs=[pl.BlockSpec((tm, d), lambda i: (i, 0)), _full((1, d)), HBM_SPEC],
        out_specs=[pl.BlockSpec((tm, n_chip * ck), lambda i: (i, 0)), pl.BlockSpec((d, tm), lambda i: (0, i))],
        out_shape=[jax.ShapeDtypeStruct((n, n_chip * ck), BF16), jax.ShapeDtypeStruct((d, n), BF16)],
        scratch_shapes=[pltpu.VMEM((n_chip, d, ck), BF16), pltpu.SemaphoreType.DMA((n_chip,))])


def _mixer_fwd(proj, x, cw, lg, lb, ws, bmap, wout, *, seq, tm, name):
    n, d = x.shape
    groups = ws.shape[0]
    gd = d // groups
    prev = _halo_prev(tm)

    def body(proj_ref, pcg_ref, pxi_ref, x_ref, cw_ref, lg_ref, lb_ref, ws_ref, bmap_ref, wout_ref, o_ref,
             vn_s, mixed_s):
        seq_start = (pl.program_id(0) * tm) % seq == 0

        def piece(k):
            return proj_ref[:, k * d:(k + 1) * d].astype(F32)

        z = piece(1) * piece(2)
        zprev = jnp.where(seq_start, 0.0, pcg_ref[...].astype(F32) * pxi_ref[...].astype(F32))
        cz = (cw_ref[0:1, :] * _shift_prev(zprev, z, 2) + cw_ref[1:2, :] * _shift_prev(zprev, z, 1)
              + cw_ref[2:3, :] * z)
        ya = piece(0) * cz

        v = piece(4)
        xc = v - _lanemean(v)
        vn = xc * lax.rsqrt(_lanemean(xc * xc) + EPS) * lg_ref[...] + lb_ref[...]
        vn_s[...] = vn.astype(BF16)
        _, wsm = _tril_weights(ws_ref, groups)
        for ck in range(tm // CHUNK):
            rows = slice(ck * CHUNK, (ck + 1) * CHUNK)
            for g in range(groups):
                cols = slice(g * gd, (g + 1) * gd)
                mixed_s[rows, cols] = (jnp.dot(wsm[g].astype(BF16), vn_s[rows, cols], preferred_element_type=F32)
                                       + bmap_ref[:, cols])
        yb = piece(3) * mixed_s[...]
        merged = jax.nn.sigmoid(piece(5)) * ya + jax.nn.sigmoid(piece(6)) * yb
        o_ref[...] = x_ref[...] + jnp.dot(merged.astype(BF16), wout_ref[...], preferred_element_type=F32)

    return pl.pallas_call(
        body, name=name, grid=(n // tm,),
        in_specs=[pl.BlockSpec((tm, N_PIECES * d), lambda i: (i, 0)),
                  pl.BlockSpec((HALO, d), lambda i: (prev(i), 1)),
                  pl.BlockSpec((HALO, d), lambda i: (prev(i), 2)),
                  pl.BlockSpec((tm, d), lambda i: (i, 0)),
                  _full((3, d)), _full((1, d)), _full((1, d)), _full((groups, CHUNK, CHUNK)), _full((CHUNK, d)),
                  _resident((d, d), lambda i: (0, 0))],
        out_specs=pl.BlockSpec((tm, d), lambda i: (i, 0)),
        out_shape=jax.ShapeDtypeStruct((n, d), F32),
        scratch_shapes=[pltpu.VMEM((tm, d), BF16), pltpu.VMEM((tm, d), F32)],
        compiler_params=pltpu.CompilerParams(dimension_semantics=("parallel",), vmem_limit_bytes=VMEM_LIMIT_V7X),
    )(proj, proj, proj, x, cw, lg, lb, ws, bmap, wout)


def _ffn_fwd(up, x, cw, wd, *, seq, tm, cwid, name):
    n, d = x.shape
    ff = wd.shape[0]
    prev = _halo_prev(tm)

    def body(up_ref, pup_ref, x_ref, cw_ref, wd_ref, o_ref, conv_ref):
        seq_start = (pl.program_id(0) * tm) % seq == 0

        def conv(lo):
            cols = slice(lo, lo + cwid)
            cur = up_ref[:, cols].astype(F32)
            pre = jnp.where(seq_start, 0.0, pup_ref[:, cols].astype(F32))
            out = (cw_ref[0:1, cols] * _shift_prev(pre, cur, 2) + cw_ref[1:2, cols] * _shift_prev(pre, cur, 1)
                   + cw_ref[2:3, cols] * cur)
            conv_ref[:, cols] = out.astype(BF16)
            return out

        acc = x_ref[...]
        for cj in range(ff // cwid):
            gate = conv(cj * cwid)
            val = conv(ff + cj * cwid)
            a = gate * jax.nn.sigmoid(gate) * val
            acc = acc + jnp.dot(a.astype(BF16), wd_ref[cj * cwid:(cj + 1) * cwid, :], preferred_element_type=F32)
        o_ref[...] = acc

    return pl.pallas_call(
        body, name=name, grid=(n // tm,),
        in_specs=[pl.BlockSpec((tm, 2 * ff), lambda i: (i, 0)),
                  pl.BlockSpec((HALO, 2 * ff), lambda i: (prev(i), 0)),
                  pl.BlockSpec((tm, d), lambda i: (i, 0)),
                  _full((3, 2 * ff)),
                  _resident((ff, d), lambda i: (0, 0))],
        out_specs=[pl.BlockSpec((tm, d), lambda i: (i, 0)), pl.BlockSpec((tm, 2 * ff), lambda i: (i, 0))],
        out_shape=[jax.ShapeDtypeStruct((n, d), F32), jax.ShapeDtypeStruct((n, 2 * ff), BF16)],
        compiler_params=pltpu.CompilerParams(dimension_semantics=("parallel",), vmem_limit_bytes=VMEM_LIMIT_V7X),
    )(up, up, x, cw, wd)


def _final_loss(x, g, target, *, tm, name):
    n, d = x.shape

    def body(x_ref, g_ref, t_ref, dx_ref, loss_ref, dg_ref):
        @pl.when(pl.program_id(0) == 0)
        def _():
            loss_ref[...] = jnp.zeros_like(loss_ref)
            dg_ref[...] = jnp.zeros_like(dg_ref)

        xf = x_ref[...]
        r = lax.rsqrt(_lanemean(xf * xf) + EPS)
        xhat = xf * r
        diff = xhat * g_ref[...] - t_ref[...]
        loss_ref[...] += (0.5 / d) * _rowsum(jnp.sum(diff * diff, axis=-1, keepdims=True))
        dy = diff * (1.0 / d)
        dg_ref[...] += _rowsum(dy * xhat)
        dyh = dy * g_ref[...]
        dx_ref[...] = r * (dyh - xhat * _lanemean(dyh * xhat))

    return pl.pallas_call(
        body, name=name, grid=(n // tm,),
        in_specs=[pl.BlockSpec((tm, d), lambda i: (i, 0)), _full((1, d)), pl.BlockSpec((tm, d), lambda i: (i, 0))],
        out_specs=[pl.BlockSpec((tm, d), lambda i: (i, 0)), _full((HALO, CHUNK)), _full((1, d))],
        out_shape=[jax.ShapeDtypeStruct((n, d), F32), jax.ShapeDtypeStruct((HALO, CHUNK), F32),
                   jax.ShapeDtypeStruct((1, d), F32)],
        compiler_params=pltpu.CompilerParams(dimension_semantics=("arbitrary",), vmem_limit_bytes=VMEM_LIMIT_V7X),
    )(x, g, target)


def _ffn_bwd(dx, up, conv, cw, wd, *, seq, tm, cwid, name):
    n, d = dx.shape
    ff = wd.shape[0]
    nxt = _halo_next(tm, n)

    def body(dx_ref, dxn_ref, up_ref, conv_ref, nconv_ref, cw_ref, wd_ref, dup_ref, dwd_ref, dcw_ref):
        i = pl.program_id(0)
        keep_next = jnp.where(((i + 1) * tm) % seq == 0, 0.0, 1.0)

        @pl.when(i == 0)
        def _():
            dwd_ref[...] = jnp.zeros_like(dwd_ref)
            dcw_ref[...] = jnp.zeros_like(dcw_ref)

        dxe = jnp.concatenate([dx_ref[...], dxn_ref[...]], axis=0).astype(BF16)
        dxb = dxe[:tm]
        for cj in range(ff // cwid):
            rows = slice(cj * cwid, (cj + 1) * cwid)
            g_cols, v_cols = slice(cj * cwid, (cj + 1) * cwid), slice(ff + cj * cwid, ff + (cj + 1) * cwid)
            dae = lax.dot_general(dxe, wd_ref[rows, :], NT_DIMS, preferred_element_type=F32)
            da, dan = dae[:tm], dae[tm:]

            def grads(gate, val, da_rows):
                sg = jax.nn.sigmoid(gate)
                sl = gate * sg
                return sl, da_rows * val * sg * (1.0 + gate * (1.0 - sg)), da_rows * sl

            gate, val = conv_ref[:, g_cols].astype(F32), conv_ref[:, v_cols].astype(F32)
            sl, d_gate, d_val = grads(gate, val, da)
            dwd_ref[rows, :] += lax.dot_general((sl * val).astype(BF16), dxb, TN_DIMS, preferred_element_type=F32)
            _, d_gate_n, d_val_n = grads(nconv_ref[:, g_cols].astype(F32), nconv_ref[:, v_cols].astype(F32),
                                         dan * keep_next)
            for cols, dcur, dnext in ((g_cols, d_gate, d_gate_n), (v_cols, d_val, d_val_n)):
                upc = up_ref[:, cols].astype(F32)
                d1, d2 = _shift_next(dcur, dnext, 1), _shift_next(dcur, dnext, 2)
                for k, dk in enumerate((d2, d1, dcur)):
                    dcw_ref[k:k + 1, cols] += _rowsum(dk * upc)
                dup = cw_ref[2:3, cols] * dcur + cw_ref[1:2, cols] * d1 + cw_ref[0:1, cols] * d2
                dup_ref[:, cols] = dup.astype(BF16)

    return pl.pallas_call(
        body, name=name, grid=(n // tm,),
        in_specs=[pl.BlockSpec((tm, d), lambda i: (i, 0)),
                  pl.BlockSpec((HALO, d), lambda i: (nxt(i), 0)),
                  pl.BlockSpec((tm, 2 * ff), lambda i: (i, 0)),
                  pl.BlockSpec((tm, 2 * ff), lambda i: (i, 0)),
                  pl.BlockSpec((HALO, 2 * ff), lambda i: (nxt(i), 0)),
                  _full((3, 2 * ff)),
                  _resident((ff, d), lambda i: (0, 0))],
        out_specs=[pl.BlockSpec((tm, 2 * ff), lambda i: (i, 0)), _full((ff, d)), _full((3, 2 * ff))],
        out_shape=[jax.ShapeDtypeStruct((n, 2 * ff), BF16), jax.ShapeDtypeStruct((ff, d), F32),
                   jax.ShapeDtypeStruct((3, 2 * ff), F32)],
        compiler_params=pltpu.CompilerParams(dimension_semantics=("arbitrary",), vmem_limit_bytes=VMEM_LIMIT_V7X),
    )(dx, dx, up, conv, conv, cw, wd)


def _matmul_bwd_x(dy, wg, x, g, dres, *, tm, name, jobs=()):
    n, d = x.shape
    n_chip, _, ck = wg.shape

    def main(ins, outs, scratch):
        dy_ref, w_hbm, x_ref, g_ref, dres_ref = ins
        dx_ref, dg_ref = outs
        w_ref = _weights_in_vmem(w_hbm, scratch)

        @pl.when(pl.program_id(0) == 0)
        def _():
            dg_ref[...] = jnp.zeros_like(dg_ref)

        dh = lax.dot_general(dy_ref[:, 0:ck], w_ref(0), NT_DIMS, preferred_element_type=F32)
        for j in range(1, n_chip):
            dh = dh + lax.dot_general(dy_ref[:, j * ck:(j + 1) * ck], w_ref(j), NT_DIMS, preferred_element_type=F32)
        xf = x_ref[...]
        r = lax.rsqrt(_lanemean(xf * xf) + EPS)
        xhat = xf * r
        dg_ref[...] += _rowsum(dh * xhat)
        dyh = dh * g_ref[...]
        dx_ref[...] = dres_ref[...] + r * (dyh - xhat * _lanemean(dyh * xhat))

    return _hosted_call(
        main, name=name, grid=(n // tm,), jobs=jobs, semantics=("arbitrary",), operands=(dy, wg, x, g, dres),
        in_specs=[pl.BlockSpec((tm, n_chip * ck), lambda i: (i, 0)), HBM_SPEC,
                  pl.BlockSpec((tm, d), lambda i: (i, 0)), _full((1, d)), pl.BlockSpec((tm, d), lambda i: (i, 0))],
        out_specs=[pl.BlockSpec((tm, d), lambda i: (i, 0)), _full((1, d))],
        out_shape=[jax.ShapeDtypeStruct((n, d), F32), jax.ShapeDtypeStruct((1, d), F32)],
        scratch_shapes=[pltpu.VMEM((n_chip, d, ck), BF16), pltpu.SemaphoreType.DMA((n_chip,))])


def _matmul_bwd_w(ht, dy, n_chip, *, tk, name, jobs=()):
    d, n = ht.shape
    ck = dy.shape[1] // n_chip

    def main(ins, outs, _):
        ht_ref, dy_ref = ins
        o_ref, = outs

        @pl.when(pl.program_id(1) == 0)
        def _():
            o_ref[...] = jnp.zeros_like(o_ref)

        o_ref[...] += jnp.dot(ht_ref[...], dy_ref[...], preferred_element_type=F32)

    return _hosted_call(
        main, name=name, grid=(n_chip, n // tk), jobs=jobs, semantics=("parallel", "arbitrary"), operands=(ht, dy),
        in_specs=[pl.BlockSpec((d, tk), lambda j, k: (0, k)), pl.BlockSpec((tk, ck), lambda j, k: (k, j))],
        out_specs=[pl.BlockSpec((None, d, ck), lambda j, k: (j, 0, 0))],
        out_shape=[jax.ShapeDtypeStruct((n_chip, d, ck), F32)])


def _mixer_bwd(dx, proj, cw, lg, lb, ws, bmap, wout, *, seq, tm, name):
    n, d = dx.shape
    groups = ws.shape[0]
    gd = d // groups
    prev, nxt = _halo_prev(tm), _halo_next(tm, n)
    n_tiles = n // tm

    def body(proj_ref, pcg_ref, pxi_ref, nbg_ref, nga_ref, dx_ref, dxn_ref, cw_ref, lg_ref, lb_ref, ws_ref,
             bmap_ref, wout_ref, dproj_ref, dwout_ref, dcw_ref, dlg_ref, dlb_ref, dws_ref, dbs_ref,
             vn_s, mixed_s, dmix_s, dvn_s, dbmap_s):
        i = pl.program_id(0)
        seq_start = (i * tm) % seq == 0
        keep_next = jnp.where(((i + 1) * tm) % seq == 0, 0.0, 1.0)

        @pl.when(i == 0)
        def _():
            for ref in (dwout_ref, dcw_ref, dlg_ref, dlb_ref, dws_ref, dbmap_s):
                ref[...] = jnp.zeros_like(ref)

        def piece(k):
            return proj_ref[:, k * d:(k + 1) * d].astype(F32)

        def put(k, val):
            dproj_ref[:, k * d:(k + 1) * d] = val.astype(BF16)

        w = [cw_ref[k:k + 1, :] for k in range(3)]
        cg, xi = piece(1), piece(2)
        z = cg * xi
        zprev = jnp.where(seq_start, 0.0, pcg_ref[...].astype(F32) * pxi_ref[...].astype(F32))
        z2, z1 = _shift_prev(zprev, z, 2), _shift_prev(zprev, z, 1)
        cz = w[0] * z2 + w[1] * z1 + w[2] * z
        bg = piece(0)
        ya = bg * cz

        v = piece(4)
        xc = v - _lanemean(v)
        rstd = lax.rsqrt(_lanemean(xc * xc) + EPS)
        vhat = xc * rstd
        vn_s[...] = (vhat * lg_ref[...] + lb_ref[...]).astype(BF16)
        tril, wsm = _tril_weights(ws_ref, groups)
        for ck in range(tm // CHUNK):
            rows = slice(ck * CHUNK, (ck + 1) * CHUNK)
            for g in range(groups):
                cols = slice(g * gd, (g + 1) * gd)
                mixed_s[rows, cols] = (jnp.dot(wsm[g].astype(BF16), vn_s[rows, cols], preferred_element_type=F32)
                                       + bmap_ref[:, cols])
        u = piece(3)
        mixed = mixed_s[...]
        yb = u * mixed
        sa, sb = jax.nn.sigmoid(piece(5)), jax.nn.sigmoid(piece(6))
        merged = sa * ya + sb * yb

        dxe = jnp.concatenate([dx_ref[...], dxn_ref[...]], axis=0).astype(BF16)
        dme = lax.dot_general(dxe, wout_ref[...], NT_DIMS, preferred_element_type=F32)
        dm, dm_n = dme[:tm], dme[tm:]
        dwout_ref[...] += lax.dot_general(merged.astype(BF16), dxe[:tm], TN_DIMS, preferred_element_type=F32)

        put(5, dm * ya * sa * (1.0 - sa))
        put(6, dm * yb * sb * (1.0 - sb))
        d_ya, d_yb = dm * sa, dm * sb
        put(0, d_ya * cz)
        d_cz = d_ya * bg
        for k, tap in enumerate((z2, z1, z)):
            dcw_ref[k:k + 1, :] += _rowsum(d_cz * tap)
        d_cz_n = dm_n * jax.nn.sigmoid(nga_ref[...].astype(F32)) * nbg_ref[...].astype(F32) * keep_next
        dz = w[2] * d_cz + w[1] * _shift_next(d_cz, d_cz_n, 1) + w[0] * _shift_next(d_cz, d_cz_n, 2)
        put(1, dz * xi)
        put(2, dz * cg)

        put(3, d_yb * mixed)
        d_mixed = d_yb * u
        dmix_s[...] = d_mixed.astype(BF16)
        for ck in range(tm // CHUNK):
            rows = slice(ck * CHUNK, (ck + 1) * CHUNK)
            dbmap_s[...] += d_mixed[rows, :]
            for g in range(groups):
                cols = slice(g * gd, (g + 1) * gd)
                dvn_s[rows, cols] = jnp.dot(wsm[g].T.astype(BF16), dmix_s[rows, cols], preferred_element_type=F32)
                dws_ref[g] += jnp.where(
                    tril, lax.dot_general(dmix_s[rows, cols], vn_s[rows, cols], NT_DIMS, preferred_element_type=F32),
                    0.0)
        d_vn = dvn_s[...]
        dlg_ref[...] += _rowsum(d_vn * vhat)
        dlb_ref[...] += _rowsum(d_vn)
        d_vhat = d_vn * lg_ref[...]
        put(4, rstd * (d_vhat - _lanemean(d_vhat) - vhat * _lanemean(d_vhat * vhat)))

        @pl.when(i == n_tiles - 1)
        def _():
            for g in range(groups):
                dbs_ref[:, g:g + 1] = jnp.sum(dbmap_s[:, g * gd:(g + 1) * gd], axis=-1, keepdims=True)

    return pl.pallas_call(
        body, name=name, grid=(n_tiles,),
        in_specs=[pl.BlockSpec((tm, N_PIECES * d), lambda i: (i, 0)),
                  pl.BlockSpec((HALO, d), lambda i: (prev(i), 1)),
                  pl.BlockSpec((HALO, d), lambda i: (prev(i), 2)),
                  pl.BlockSpec((HALO, d), lambda i: (nxt(i), 0)),
                  pl.BlockSpec((HALO, d), lambda i: (nxt(i), 5)),
                  pl.BlockSpec((tm, d), lambda i: (i, 0)),
                  pl.BlockSpec((HALO, d), lambda i: (nxt(i), 0)),
                  _full((3, d)), _full((1, d)), _full((1, d)), _full((groups, CHUNK, CHUNK)), _full((CHUNK, d)),
                  _resident((d, d), lambda i: (0, 0))],
        out_specs=[pl.BlockSpec((tm, N_PIECES * d), lambda i: (i, 0)), _full((d, d)),
                   _full((3, d)), _full((1, d)), _full((1, d)), _full((groups, CHUNK, CHUNK)), _full((CHUNK, groups))],
        out_shape=[jax.ShapeDtypeStruct((n, N_PIECES * d), BF16), jax.ShapeDtypeStruct((d, d), F32),
                   jax.ShapeDtypeStruct((3, d), F32), jax.ShapeDtypeStruct((1, d), F32),
                   jax.ShapeDtypeStruct((1, d), F32), jax.ShapeDtypeStruct((groups, CHUNK, CHUNK), F32),
                   jax.ShapeDtypeStruct((CHUNK, groups), F32)],
        scratch_shapes=[pltpu.VMEM((tm, d), BF16), pltpu.VMEM((tm, d), F32), pltpu.VMEM((tm, d), BF16),
                        pltpu.VMEM((tm, d), F32), pltpu.VMEM((CHUNK, d), F32)],
        compiler_params=pltpu.CompilerParams(dimension_semantics=("arbitrary",), vmem_limit_bytes=VMEM_LIMIT_V7X),
    )(proj, proj, proj, proj, proj, dx, dx, cw, lg, lb, ws, bmap, wout)


def _gather_small(pack, name):
    def body(p_ref, o_ref, send_sem, recv_sem):
        x, y, c = _mesh_pos()
        chip = 2 * x + y
        o_ref[chip] = p_ref[...]
        copies = []
        for r in (1, 2, 3):
            px, py = _peer_chip(x, y, r)
            copies.append(_remote(p_ref, o_ref.at[chip], (send_sem, recv_sem), r - 1, (px, py, c)))
        for cp in copies:
            cp.start()
        for r, cp in zip((1, 2, 3), copies):
            px, py = _peer_chip(x, y, r)
            landed = o_ref.at[2 * px + py]
            _remote(landed, landed, (send_sem, recv_sem), r - 1, (px, py, c)).wait_recv()
            cp.wait_send()

    return pl.pallas_call(
        body, name=name, in_specs=[VMEM_SPEC], out_specs=VMEM_SPEC,
        out_shape=jax.ShapeDtypeStruct((N_CHIP,) + pack.shape, pack.dtype),
        scratch_shapes=[pltpu.SemaphoreType.DMA((3,)), pltpu.SemaphoreType.DMA((3,))],
    )(pack)


def _all_reduce_small(pack, name, jobs=()):
    rows = pack.shape[0]

    def main(ins, outs, scratch):
        p_ref, = ins
        o_ref, = outs
        sib_buf, chip_buf, send_sem, recv_sem = scratch
        sems = (send_sem, recv_sem)
        x, y, c = _mesh_pos()
        chip = 2 * x + y
        swap = _remote(p_ref, sib_buf, sems, 0, (x, y, 1 - c))
        swap.start()
        swap.wait()
        chip_buf[chip] = p_ref[...] + sib_buf[...]
        copies = []
        for r in (1, 2, 3):
            px, py = _peer_chip(x, y, r)
            copies.append(_remote(chip_buf.at[chip], chip_buf.at[chip], sems, r, (px, py, c)))
        for cp in copies:
            cp.start()
        for r, cp in zip((1, 2, 3), copies):
            px, py = _peer_chip(x, y, r)
            landed = chip_buf.at[2 * px + py]
            _remote(landed, landed, sems, r, (px, py, c)).wait_recv()
            cp.wait_send()
        o_ref[...] = ((chip_buf[0] + chip_buf[1]) + chip_buf[2]) + chip_buf[3]

    outs, job_outs = _hosted_call(
        main, name=name, grid=(1,), jobs=jobs, operands=(pack,),
        in_specs=[_full(pack.shape)], out_specs=[_full(pack.shape)],
        out_shape=[jax.ShapeDtypeStruct(pack.shape, pack.dtype)],
        scratch_shapes=[pltpu.VMEM((rows, 128), F32), pltpu.VMEM((N_CHIP, rows, 128), F32),
                        pltpu.SemaphoreType.DMA((4,)), pltpu.SemaphoreType.DMA((4,))])
    return outs[0], job_outs


def _block_rows(rows, cols, itemsize=4, budget=2 << 20):
    best = None
    for t in range(16, rows + 1, 16):
        if rows % t == 0 and t * cols * itemsize <= budget:
            best = t
    assert best is not None, (rows, cols)
    return best


def _pair_sum(grad, recv, core, name):
    n_chip, rk, ck = grad.shape
    rh = rk // 2
    tr = _block_rows(rh, ck)
    nb = rh // tr

    def body(core_ref, g_ref, r_ref, o_ref):
        o_ref[...] = (g_ref[...] + r_ref[...]).astype(BF16)

    blk = (None, tr, ck)
    return pl.pallas_call(
        body, name=name,
        grid_spec=pltpu.PrefetchScalarGridSpec(
            num_scalar_prefetch=1, grid=(n_chip, nb),
            in_specs=[pl.BlockSpec(blk, lambda k, i, core_ref: (k, core_ref[0] * nb + i, 0)),
                      pl.BlockSpec(blk, lambda k, i, core_ref: (k, i, 0))],
            out_specs=pl.BlockSpec(blk, lambda k, i, core_ref: (k, i, 0))),
        out_shape=jax.ShapeDtypeStruct((n_chip, rh, ck), BF16),
        compiler_params=pltpu.CompilerParams(dimension_semantics=("parallel", "parallel"),
                                             vmem_limit_bytes=VMEM_LIMIT_V7X),
    )(core, grad, recv)


def _final_sum(grad, recv, arrived, where, prev, layer, n_layers, name):
    n_chip, rk, ck = grad.shape
    rh = rk // 2
    tr = _block_rows(rh, ck)
    nb = rh // tr

    def body(where_ref, g_ref, r_ref, a1_ref, a2_ref, a3_ref, *rest):
        o_ref = rest[-1]
        own = g_ref[...] + r_ref[...]
        o_ref[...] = ((own + a1_ref[...].astype(F32)) + a2_ref[...].astype(F32)) + a3_ref[...].astype(F32)

    blk = (None, tr, ck)
    slot = lambda r: pl.BlockSpec(blk, lambda i, w: (r, i, 0))
    extra = [] if prev is None else [prev]
    return pl.pallas_call(
        body, name=name,
        grid_spec=pltpu.PrefetchScalarGridSpec(
            num_scalar_prefetch=1, grid=(nb,),
            in_specs=[pl.BlockSpec(blk, lambda i, w: (w[1], w[0] * nb + i, 0)),
                      pl.BlockSpec(blk, lambda i, w: (w[1], i, 0)),
                      slot(0), slot(1), slot(2)] + [pl.BlockSpec(memory_space=pl.ANY)] * len(extra),
            out_specs=pl.BlockSpec(blk, lambda i, w: (layer, w[0] * nb + i, 0))),
        out_shape=jax.ShapeDtypeStruct((n_layers, rk, ck), F32),
        input_output_aliases={6: 0} if extra else {},
        compiler_params=pltpu.CompilerParams(dimension_semantics=("parallel",), vmem_limit_bytes=VMEM_LIMIT_V7X),
    )(where, grad, recv, arrived, arrived, arrived, *extra)


def _adamw_math(w, g, m, v):
    m = ADAM_B1 * m + (1.0 - ADAM_B1) * g
    v = ADAM_B2 * v + (1.0 - ADAM_B2) * (g * g)
    m_hat = m / (1.0 - ADAM_B1 ** ADAM_STEP)
    v_hat = v / (1.0 - ADAM_B2 ** ADAM_STEP)
    delta = -ADAM_LR * (m_hat / (jnp.sqrt(v_hat) + ADAM_EPS) + ADAM_WD * w)
    return delta, m, v


def _adamw_big(w, g, m, v, name):
    n_l, rk, ck = w.shape
    tr = _block_rows(rk, ck, budget=1 << 20)

    def body(w_ref, g_ref, m_ref, v_ref, d_ref, nm_ref, nv_ref):
        d_ref[...], nm_ref[...], nv_ref[...] = _adamw_math(w_ref[...], g_ref[...], m_ref[...], v_ref[...])

    spec = pl.BlockSpec((None, tr, ck), lambda l, i: (l, i, 0))
    return pl.pallas_call(
        body, name=name, grid=(n_l, rk // tr), in_specs=[spec] * 4, out_specs=[spec] * 3,
        out_shape=[jax.ShapeDtypeStruct(w.shape, F32)] * 3,
        compiler_params=pltpu.CompilerParams(dimension_semantics=("parallel", "parallel"),
                                             vmem_limit_bytes=VMEM_LIMIT_V7X),
    )(w, g, m, v)


def _adamw_small(ws, gs, ms, vs, name):
    n_p = len(ws)

    def body(*refs):
        ins, outs = refs[:4 * n_p], refs[4 * n_p:]
        for p in range(n_p):
            res = _adamw_math(ins[p][...], ins[n_p + p][...], ins[2 * n_p + p][...], ins[3 * n_p + p][...])
            for q in range(3):
                outs[q * n_p + p][...] = res[q]

    outs = pl.pallas_call(
        body, name=name, in_specs=[VMEM_SPEC] * (4 * n_p), out_specs=[VMEM_SPEC] * (3 * n_p),
        out_shape=[jax.ShapeDtypeStruct(w.shape, F32) for w in ws] * 3,
        compiler_params=pltpu.CompilerParams(vmem_limit_bytes=VMEM_LIMIT_V7X),
    )(*ws, *gs, *ms, *vs)
    return outs[:n_p], outs[n_p:2 * n_p], outs[2 * n_p:]


def kernel(x, mix_norm_g, w_in, conv_a_w, ln_v_g, ln_v_b, w_s, b_s, w_out, ffn_norm_g, w_up, conv_ffn_w, w_down, final_norm_g, loss_target, m_mix_norm_g, m_w_in, m_conv_a_w, m_ln_v_g, m_ln_v_b, m_w_s, m_b_s, m_w_out, m_ffn_norm_g, m_w_up, m_conv_ffn_w, m_w_down, m_final_norm_g, v_mix_norm_g, v_w_in, v_conv_a_w, v_ln_v_g, v_ln_v_b, v_w_s, v_b_s, v_w_out, v_ffn_norm_g, v_w_up, v_conv_ffn_w, v_w_down, v_final_norm_g):
    bsz, seq, d = x.shape
    n = bsz * seq
    n_l, groups = w_s.shape[0], w_s.shape[1]
    assert n_l == 2, "the exchange schedule below is written for two layers"
    gd = d // groups
    ff = w_down.shape[1] * N_CHIP
    axes = ("x", "y", "c")
    mx, my, mc = _mesh_pos()
    chip = 2 * mx + my
    core = jnp.reshape(mc, (1,)).astype(jnp.int32)
    where = jnp.stack([mc, chip]).astype(jnp.int32)

    tm_mm = _row_tile(seq, 512)
    tm_ew = _row_tile(seq, 256)
    tk_w = _row_tile(seq, 2048)
    ff_chunk = ff // 2 if (ff // 2) % 128 == 0 else ff
    IN, OUT, UP, DOWN = range(4)

    shards = [w.astype(BF16) for w in (w_in, w_out, w_up, w_down)]
    piece = lambda a, l: (shards[a], l)
    (w_in_0,), = _comm_only([_GatherJob([piece(IN, 0)])], "gather_first")

    taps = jnp.concatenate([conv_a_w.reshape(n_l, -1), conv_ffn_w.reshape(n_l, -1)], axis=1)
    tap_rows = -(-taps.size // 128 // 8) * 8
    tap_pack = jnp.zeros((tap_rows * 128,), F32).at[:taps.size].set(taps.reshape(-1)).reshape(tap_rows, 128)
    tap_all = _gather_small(tap_pack, "gather_taps")
    tap_all = tap_all.reshape(N_CHIP, -1)[:, :taps.size].reshape(N_CHIP, n_l, -1)
    ca = tap_all[:, :, :3 * d // N_CHIP].reshape(N_CHIP, n_l, 3, d // N_CHIP)
    cf = tap_all[:, :, 3 * d // N_CHIP:].reshape(N_CHIP, n_l, 3, 2 * ff // N_CHIP)
    conv_a_full = jnp.transpose(ca, (1, 2, 0, 3)).reshape(n_l, 3, d)
    conv_f_full = jnp.transpose(cf, (1, 2, 0, 3)).reshape(n_l, 3, 2 * ff)

    bmaps = jnp.repeat(jnp.swapaxes(b_s, 1, 2), gd, axis=2)

    xs = x.reshape(n, d)
    tgt = loss_target.reshape(n, d)

    def mixer_args(l, w_out_l):
        return (conv_a_full[l], ln_v_g[l][None], ln_v_b[l][None], w_s[l], bmaps[l], w_out_l.reshape(d, d))

    (proj0, h1_0), ((w_out_0, w_up_0, w_down_0),) = _norm_matmul(
        xs, mix_norm_g[0][None], w_in_0, tm=tm_mm, name="fwd_in_proj_0",
        jobs=[_GatherJob([piece(OUT, 0), piece(UP, 0), piece(DOWN, 0)])])
    x1_0 = _mixer_fwd(proj0, xs, *mixer_args(0, w_out_0), seq=seq, tm=tm_ew, name="fwd_mixer_0")
    (up0, h2_0), ((w_in_1, w_out_1),) = _norm_matmul(
        x1_0, ffn_norm_g[0][None], w_up_0, tm=tm_mm, name="fwd_up_proj_0",
        jobs=[_GatherJob([piece(IN, 1), piece(OUT, 1)])])
    x2_0, conv0 = _ffn_fwd(up0, x1_0, conv_f_full[0], w_down_0.reshape(ff, d), seq=seq, tm=tm_ew, cwid=ff_chunk,
                           name="fwd_ffn_0")
    (proj1, h1_1), ((w_up_1, w_down_1),) = _norm_matmul(
        x2_0, mix_norm_g[1][None], w_in_1, tm=tm_mm, name="fwd_in_proj_1",
        jobs=[_GatherJob([piece(UP, 1), piece(DOWN, 1)])])
    x1_1 = _mixer_fwd(proj1, x2_0, *mixer_args(1, w_out_1), seq=seq, tm=tm_ew, name="fwd_mixer_1")
    (up1, h2_1), _ = _norm_matmul(x1_1, ffn_norm_g[1][None], w_up_1, tm=tm_mm, name="fwd_up_proj_1")
    x2_1, conv1 = _ffn_fwd(up1, x1_1, conv_f_full[1], w_down_1.reshape(ff, d), seq=seq, tm=tm_ew, cwid=ff_chunk,
                           name="fwd_ffn_1")
    dx, loss_tile, d_final_g = _final_loss(x2_1, final_norm_g[None], tgt, tm=tm_mm, name="final_loss")
    loss = lax.psum(loss_tile[0, 0], axes)

    def chipwise(a):
        return a.reshape(N_CHIP, a.shape[0] // N_CHIP, a.shape[1])

    def pair_sums(grads, recvs, tag):
        return [_pair_sum(g, r, core, f"grad_pair_sum_{tag}_{a}") for a, (g, r) in enumerate(zip(grads, recvs))]

    d_up, dwd, d_cf1 = _ffn_bwd(dx, up1, conv1, conv_f_full[1], w_down_1.reshape(ff, d), seq=seq, tm=tm_ew,
                                cwid=ff_chunk, name="bwd_ffn_1")
    (dx1, d_g2_1), _ = _matmul_bwd_x(d_up, w_up_1, x1_1, ffn_norm_g[1][None], dx, tm=tm_mm, name="bwd_up_x_1")
    (dwu,), _ = _matmul_bwd_w(h2_1, d_up, N_CHIP, tk=tk_w, name="bwd_up_w_1")
    d_proj, dwo, d_ca1, d_lg1, d_lb1, d_ws1, d_bs1 = _mixer_bwd(
        dx1, proj1, *mixer_args(1, w_out_1), seq=seq, tm=tm_ew, name="bwd_mixer_1")
    (dx, d_g1_1), _ = _matmul_bwd_x(d_proj, w_in_1, x2_0, mix_norm_g[1][None], dx1, tm=tm_mm, name="bwd_in_x_1")
    (dwi,), _ = _matmul_bwd_w(h1_1, d_proj, N_CHIP, tk=tk_w, name="bwd_in_w_1")
    grads1 = [dwi, chipwise(dwo), dwu, chipwise(dwd)]
    small1 = [d_g1_1, d_ca1, d_lg1, d_lb1, d_ws1, d_bs1.T, d_g2_1, d_cf1]

    d_up, dwd, d_cf0 = _ffn_bwd(dx, up0, conv0, conv_f_full[0], w_down_0.reshape(ff, d), seq=seq, tm=tm_ew,
                                cwid=ff_chunk, name="bwd_ffn_0")
    (dx1, d_g2_0), (recv1,) = _matmul_bwd_x(d_up, w_up_0, x1_0, ffn_norm_g[0][None], dx, tm=tm_mm, name="bwd_up_x_0",
                                            jobs=[_SwapJob(grads1)])
    parts1 = pair_sums(grads1, recv1, 1)
    (dwu,), ((arr1_in,),) = _matmul_bwd_w(h2_0, d_up, N_CHIP, tk=tk_w, name="bwd_up_w_0",
                                          jobs=[_ScatterJob([parts1[IN]])])
    d_proj, dwo, d_ca0, d_lg0, d_lb0, d_ws0, d_bs0 = _mixer_bwd(
        dx1, proj0, *mixer_args(0, w_out_0), seq=seq, tm=tm_ew, name="bwd_mixer_0")
    early0 = [chipwise(dwo), dwu, chipwise(dwd)]
    (dwi,), (arr1_rest, recv0_early) = _matmul_bwd_w(
        h1_0, d_proj, N_CHIP, tk=tk_w, name="bwd_in_w_0",
        jobs=[_ScatterJob(parts1[OUT:]), _SwapJob(early0)])
    arrived1 = [arr1_in] + list(arr1_rest)
    gsum = [_final_sum(g, r, p, where, None, 1, n_l, f"grad_final_sum_1_{a}")
            for a, (g, r, p) in enumerate(zip(grads1, recv1, arrived1))]
    parts0_early = pair_sums(early0, recv0_early, "0e")
    (dx, d_g1_0), (gsum, arr0_early, (recv0_in,)) = _matmul_bwd_x(
        d_proj, w_in_0, xs, mix_norm_g[0][None], dx1, tm=tm_mm, name="bwd_in_x_0",
        jobs=[_ShareJob([(g, 1) for g in gsum]), _ScatterJob(parts0_early), _SwapJob([dwi])])
    grad_x = dx.reshape(bsz, seq, d)
    grads0 = [dwi] + early0
    recv0 = [recv0_in] + list(recv0_early)
    part0_in, = pair_sums([dwi], [recv0_in], "0i")
    small0 = [d_g1_0, d_ca0, d_lg0, d_lb0, d_ws0, d_bs0.T, d_g2_0, d_cf0]

    flat = [a.reshape(-1) for a in small0 + small1] + [d_final_g.reshape(-1)]
    sizes = [a.size for a in flat]
    total = sum(sizes)
    rows = -(-total // 128 // 8) * 8
    pack = jnp.concatenate(flat + [jnp.zeros((rows * 128 - total,), F32)]).reshape(rows, 128)
    red, ((arr0_in,),) = _all_reduce_small(pack, "small_all_reduce", jobs=[_ScatterJob([part0_in])])
    red = red.reshape(-1)
    arrived0 = [arr0_in] + list(arr0_early)
    gsum = [_final_sum(g, r, p, where, prev, 0, n_l, f"grad_final_sum_0_{a}")
            for a, (g, r, p, prev) in enumerate(zip(grads0, recv0, arrived0, gsum))]
    (g_in, g_out, g_up, g_down), = _comm_only([_ShareJob([(g, 0) for g in gsum])], "grad_share_0")

    pieces, off = [], 0
    for s in sizes:
        pieces.append(red[off:off + s])
        off += s
    per_layer = [pieces[l * 8:(l + 1) * 8] for l in range(n_l)]

    def stacked(idx, shape):
        return jnp.stack([per_layer[l][idx].reshape(shape) for l in range(n_l)])

    def my_cols(a, width):
        return lax.dynamic_slice_in_dim(a, chip * width, width, axis=-1)

    g_mix_norm = stacked(0, (d,))
    g_conv_a = my_cols(stacked(1, (3, d)), d // N_CHIP)
    g_ln_g = stacked(2, (d,))
    g_ln_b = stacked(3, (d,))
    g_ws = stacked(4, (groups, CHUNK, CHUNK))
    g_bs = stacked(5, (groups, CHUNK))
    g_ffn_norm = stacked(6, (d,))
    g_conv_f = my_cols(stacked(7, (3, 2 * ff)), 2 * ff // N_CHIP)
    g_final = pieces[-1].reshape(1, d)

    big_names = [(w_in, g_in, m_w_in, v_w_in), (w_out, g_out, m_w_out, v_w_out), (w_up, g_up, m_w_up, v_w_up),
                 (w_down, g_down, m_w_down, v_w_down)]
    upd = [_adamw_big(w, g, m, v, f"adamw_{a}") for a, (w, g, m, v) in enumerate(big_names)]

    sm_w = [mix_norm_g, conv_a_w, ln_v_g, ln_v_b, w_s, b_s, ffn_norm_g, conv_ffn_w, final_norm_g[None]]
    sm_g = [g_mix_norm, g_conv_a, g_ln_g, g_ln_b, g_ws, g_bs, g_ffn_norm, g_conv_f, g_final]
    sm_m = [m_mix_norm_g, m_conv_a_w, m_ln_v_g, m_ln_v_b, m_w_s, m_b_s, m_ffn_norm_g, m_conv_ffn_w, m_final_norm_g[None]]
    sm_v = [v_mix_norm_g, v_conv_a_w, v_ln_v_g, v_ln_v_b, v_w_s, v_b_s, v_ffn_norm_g, v_conv_ffn_w, v_final_norm_g[None]]
    sm_d, sm_nm, sm_nv = _adamw_small(sm_w, sm_g, sm_m, sm_v, "adamw_small")

    def ordered(sm, bigs):
        return [sm[0], bigs[0], sm[1], sm[2], sm[3], sm[4], sm[5], bigs[1], sm[6], bigs[2], sm[7], bigs[3],
                sm[8].reshape(d)]

    out_g = ordered(sm_g, [g_in, g_out, g_up, g_down])
    out_d = ordered(sm_d, [u[0] for u in upd])
    out_m = ordered(sm_nm, [u[1] for u in upd])
    out_v = ordered(sm_nv, [u[2] for u in upd])
    return (loss, grad_x, *out_g, *out_d, *out_m, *out_v)
```

```python
import jax
import jax.numpy as jnp
from jax import lax
from jax.experimental import pallas as pl
from jax.experimental.pallas import tpu as pltpu

F32 = jnp.float32
BF16 = jnp.bfloat16
EPS = 1e-6
CHUNK = 128
N_CHIP = 4
HALO = 8
N_PIECES = 7
VMEM_LIMIT_V7X = 56 * 1024 * 1024
MESH_T = pl.DeviceIdType.MESH
HBM_SPEC = pl.BlockSpec(memory_space=pltpu.HBM)
VMEM_SPEC = pl.BlockSpec(memory_space=pltpu.VMEM)

ADAM_LR, ADAM_B1, ADAM_B2, ADAM_EPS, ADAM_WD, ADAM_STEP = 0.001, 0.9, 0.999, 1e-08, 0.01, 10

NT_DIMS = (((1,), (1,)), ((), ()))
TN_DIMS = (((0,), (0,)), ((), ()))


def _resident(block_shape, index_map):
    return pl.BlockSpec(block_shape, index_map, pipeline_mode=pl.Buffered(1))


def _full(shape):
    return pl.BlockSpec(shape, lambda *_: (0,) * len(shape))


def _row_tile(seq, want):
    t = min(seq, want)
    assert seq % t == 0 and t % CHUNK == 0, (seq, want)
    return t


def _shift_prev(prev8, cur, k):
    ext = jnp.concatenate([prev8, cur], axis=0)
    return pltpu.roll(ext, k, 0)[HALO:]


def _shift_next(cur, next8, k):
    ext = jnp.concatenate([cur, next8], axis=0)
    n = ext.shape[0]
    return pltpu.roll(ext, n - k, 0)[:n - HALO]


def _rowsum(a):
    return jnp.sum(a, axis=0, keepdims=True)


def _lanemean(a):
    return jnp.mean(a, axis=-1, keepdims=True)


def _tril_weights(ws_ref, groups):
    r = lax.broadcasted_iota(jnp.int32, (CHUNK, CHUNK), 0)
    c = lax.broadcasted_iota(jnp.int32, (CHUNK, CHUNK), 1)
    tril = r >= c
    return tril, [jnp.where(tril, ws_ref[g], 0.0) for g in range(groups)]


def _halo_prev(tm):
    return lambda i: jnp.maximum(i * (tm // HALO) - 1, 0)


def _halo_next(tm, n):
    last = n // HALO - 1
    return lambda i: jnp.minimum((i + 1) * (tm // HALO), last)


def _mesh_pos():
    return lax.axis_index("x"), lax.axis_index("y"), lax.axis_index("c")


def _peer_chip(x, y, r):
    return (1 - x if r >> 1 else x), (1 - y if r & 1 else y)


def _remote(src, dst, sems, k, to):
    return pltpu.make_async_remote_copy(src_ref=src, dst_ref=dst, send_sem=sems[0].at[k], recv_sem=sems[1].at[k],
                                        device_id=to, device_id_type=MESH_T)


class _GatherJob:
    def __init__(self, pieces):
        self.pieces = pieces
        self.ins = [p[0] for p in pieces]
        self.out_shapes = [jax.ShapeDtypeStruct((N_CHIP,) + p[0].shape[1:], p[0].dtype) for p in pieces]
        self.aliases = {}
        n = len(pieces)
        self.sems = [pltpu.SemaphoreType.DMA((3 * n,))] * 4 + [pltpu.SemaphoreType.DMA((n,))]

    def _half(self, outs, a, of_chip, core):
        rh = outs[a].shape[1] // 2
        return outs[a].at[of_chip, pl.ds(core * rh, rh), :]

    def _own(self, ins, outs, sems, chip):
        return [pltpu.make_async_copy(ins[a].at[layer], outs[a].at[chip], sems[4].at[a])
                for a, (_, layer) in enumerate(self.pieces)]

    def _sends(self, ins, outs, sems, x, y, c):
        chip, out = 2 * x + y, []
        for a, (_, layer) in enumerate(self.pieces):
            rh = outs[a].shape[1] // 2
            for r in (1, 2, 3):
                px, py = _peer_chip(x, y, r)
                out.append(_remote(ins[a].at[layer, pl.ds(c * rh, rh), :], self._half(outs, a, chip, c), sems[0:2],
                                   3 * a + r - 1, (px, py, c)))
        return out

    def _passes(self, outs, sems, x, y, c, core):
        out = []
        for a in range(len(self.pieces)):
            for r in (1, 2, 3):
                px, py = _peer_chip(x, y, r)
                landed = self._half(outs, a, 2 * px + py, core)
                out.append(_remote(landed, landed, sems[2:4], 3 * a + r - 1, (x, y, 1 - c)))
        return out

    def start(self, ins, outs, sems):
        x, y, c = _mesh_pos()
        for cp in self._own(ins, outs, sems, 2 * x + y) + self._sends(ins, outs, sems, x, y, c):
            cp.start()

    def finish(self, ins, outs, sems):
        x, y, c = _mesh_pos()
        sends = self._sends(ins, outs, sems, x, y, c)
        passes = self._passes(outs, sems, x, y, c, c)
        k = 0
        for a in range(len(self.pieces)):
            for r in (1, 2, 3):
                px, py = _peer_chip(x, y, r)
                landed = self._half(outs, a, 2 * px + py, c)
                _remote(landed, landed, sems[0:2], k, (px, py, c)).wait_recv()
                passes[k].start()
                k += 1
        for cp in self._passes(outs, sems, x, y, c, 1 - c):
            cp.wait_recv()
        for cp in sends + passes:
            cp.wait_send()
        for cp in self._own(ins, outs, sems, 2 * x + y):
            cp.wait()


class _SwapJob:
    def __init__(self, pieces):
        self.ins = list(pieces)
        self.out_shapes = [jax.ShapeDtypeStruct((g.shape[0], g.shape[1] // 2, g.shape[2]), g.dtype) for g in pieces]
        self.aliases = {}
        self.sems = [pltpu.SemaphoreType.DMA((len(pieces),))] * 2

    def _copies(self, ins, outs, sems):
        x, y, c = _mesh_pos()
        out = []
        for a in range(len(ins)):
            rh = ins[a].shape[1] // 2
            out.append(_remote(ins[a].at[:, pl.ds((1 - c) * rh, rh), :], outs[a], sems, a, (x, y, 1 - c)))
        return out

    def start(self, ins, outs, sems):
        for cp in self._copies(ins, outs, sems):
            cp.start()

    def finish(self, ins, outs, sems):
        for cp in self._copies(ins, outs, sems):
            cp.wait()


class _ScatterJob:
    def __init__(self, pieces):
        self.ins = list(pieces)
        self.out_shapes = [jax.ShapeDtypeStruct((3,) + p.shape[1:], p.dtype) for p in pieces]
        self.aliases = {}
        self.sems = [pltpu.SemaphoreType.DMA((3 * len(pieces),))] * 2

    def _copies(self, ins, outs, sems):
        x, y, c = _mesh_pos()
        out = []
        for a in range(len(ins)):
            for r in (1, 2, 3):
                px, py = _peer_chip(x, y, r)
                out.append(_remote(ins[a].at[2 * px + py], outs[a].at[r - 1], sems, 3 * a + r - 1, (px, py, c)))
        return out

    def start(self, ins, outs, sems):
        for cp in self._copies(ins, outs, sems):
            cp.start()

    def finish(self, ins, outs, sems):
        for cp in self._copies(ins, outs, sems):
            cp.wait()


class _ShareJob:
    def __init__(self, pieces):
        self.pieces = pieces
        self.ins = [p[0] for p in pieces]
        self.out_shapes = [jax.ShapeDtypeStruct(p[0].shape, p[0].dtype) for p in pieces]
        self.aliases = {a: a for a in range(len(pieces))}
        self.sems = [pltpu.SemaphoreType.DMA((len(pieces),))] * 2

    def _copies(self, outs, sems, core):
        x, y, c = _mesh_pos()
        out = []
        for a, (_, layer) in enumerate(self.pieces):
            rh = outs[a].shape[1] // 2
            rows = outs[a].at[layer, pl.ds(core * rh, rh), :]
            out.append(_remote(rows, rows, sems, a, (x, y, 1 - c)))
        return out

    def start(self, ins, outs, sems):
        for cp in self._copies(outs, sems, lax.axis_index("c")):
            cp.start()

    def finish(self, ins, outs, sems):
        c = lax.axis_index("c")
        for cp in self._copies(outs, sems, c):
            cp.wait_send()
        for cp in self._copies(outs, sems, 1 - c):
            cp.wait_recv()


def _all_of(conds):
    out = conds[0]
    for c in conds[1:]:
        out = jnp.logical_and(out, c)
    return out


def _hosted_call(main, *, name, grid, in_specs, out_specs, out_shape, scratch_shapes=(), jobs=(), semantics=None,
                 operands=()):
    n_in, n_out, n_sc = len(in_specs), len(out_specs), len(scratch_shapes)
    counts = [(len(j.ins), len(j.out_shapes), len(j.sems)) for j in jobs]
    j_in, j_out, j_sc = (sum(c[k] for c in counts) for k in range(3))
    aliases, i0, o0 = {}, n_in, n_out
    for j, (ci, co, _) in zip(jobs, counts):
        aliases.update({i0 + a: o0 + b for a, b in j.aliases.items()})
        i0, o0 = i0 + ci, o0 + co

    def body(*refs):
        cuts = [0, n_in, n_in + j_in, n_in + j_in + n_out, n_in + j_in + n_out + j_out,
                n_in + j_in + n_out + j_out + n_sc, len(refs)]
        m_in, jb_in, m_out, jb_out, m_sc, jb_sc = (list(refs[cuts[k]:cuts[k + 1]]) for k in range(6))

        def run(phase):
            i0 = o0 = s0 = 0
            for j, (ci, co, cs) in zip(jobs, counts):
                getattr(j, phase)(jb_in[i0:i0 + ci], jb_out[o0:o0 + co], jb_sc[s0:s0 + cs])
                i0, o0, s0 = i0 + ci, o0 + co, s0 + cs

        first = _all_of([pl.program_id(ax) == 0 for ax in range(len(grid))])
        last = _all_of([pl.program_id(ax) == grid[ax] - 1 for ax in range(len(grid))])
        if jobs:
            pl.when(first)(lambda: run("start"))
        main(m_in, m_out, m_sc)
        if jobs:
            pl.when(last)(lambda: run("finish"))

    if semantics is None or jobs:
        semantics = ("arbitrary",) * len(grid)
    outs = pl.pallas_call(
        body, name=name, grid=grid,
        in_specs=list(in_specs) + [HBM_SPEC] * j_in,
        out_specs=list(out_specs) + [HBM_SPEC] * j_out,
        out_shape=list(out_shape) + [s for j in jobs for s in j.out_shapes],
        scratch_shapes=list(scratch_shapes) + [s for j in jobs for s in j.sems],
        input_output_aliases=aliases,
        compiler_params=pltpu.CompilerParams(dimension_semantics=semantics, vmem_limit_bytes=VMEM_LIMIT_V7X),
    )(*operands, *[a for j in jobs for a in j.ins])
    main_outs, rest, job_outs = list(outs[:n_out]), list(outs[n_out:]), []
    for _, co, _ in counts:
        job_outs.append(rest[:co])
        rest = rest[co:]
    return main_outs, job_outs


def _comm_only(jobs, name):
    return _hosted_call(lambda i, o, s: None, name=name, grid=(1,), in_specs=[], out_specs=[], out_shape=[],
                        jobs=jobs)[1]


def _norm_matmul(x, g, wg, *, tm, name, jobs=()):
    n, d = x.shape
    n_chip, _, ck = wg.shape

    def main(ins, outs, _):
        x_ref, g_ref, w_ref = ins
        o_ref, ht_ref = outs
        xf = x_ref[...]
        r = lax.rsqrt(_lanemean(xf * xf) + EPS)
        h = xf * r * g_ref[...]
        ht_ref[...] = h.T.astype(BF16)
        hb = h.astype(BF16)
        for j in range(n_chip):
            o_ref[:, j * ck:(j + 1) * ck] = jnp.dot(hb, w_ref[j], preferred_element_type=F32).astype(BF16)

    return _hosted_call(
        main, name=name, grid=(n // tm,), jobs=jobs, semantics=("parallel",), operands=(x, g, wg),
        in_specs=[pl.BlockSpec((tm, d), lambda i: (i, 0)), _full((1, d)), _resident((n_chip, d, ck), lambda i: (0, 0, 0))],
        out_specs=[pl.BlockSpec((tm, n_chip * ck), lambda i: (i, 0)), pl.BlockSpec((d, tm), lambda i: (0, i))],
        out_shape=[jax.ShapeDtypeStruct((n, n_chip * ck), BF16), jax.ShapeDtypeStruct((d, n), BF16)])


def _mixer_fwd(proj, x, cw, lg, lb, ws, bmap, wout, *, seq, tm, name):
    n, d = x.shape
    groups = ws.shape[0]
    gd = d // groups
    prev = _halo_prev(tm)

    def body(proj_ref, pcg_ref, pxi_ref, x_ref, cw_ref, lg_ref, lb_ref, ws_ref, bmap_ref, wout_ref, o_ref, cz_ref,
             vn_s, mixed_s):
        seq_start = (pl.program_id(0) * tm) % seq == 0

        def piece(k):
            return proj_ref[:, k * d:(k + 1) * d].astype(F32)

        z = piece(1) * piece(2)
        zprev = jnp.where(seq_start, 0.0, pcg_ref[...].astype(F32) * pxi_ref[...].astype(F32))
        cz = (cw_ref[0:1, :] * _shift_prev(zprev, z, 2) + cw_ref[1:2, :] * _shift_prev(zprev, z, 1)
              + cw_ref[2:3, :] * z)
        cz_ref[...] = cz.astype(BF16)
        ya = piece(0) * cz

        v = piece(4)
        xc = v - _lanemean(v)
        vn = xc * lax.rsqrt(_lanemean(xc * xc) + EPS) * lg_ref[...] + lb_ref[...]
        vn_s[...] = vn.astype(BF16)
        _, wsm = _tril_weights(ws_ref, groups)
        for ck in range(tm // CHUNK):
            rows = slice(ck * CHUNK, (ck + 1) * CHUNK)
            for g in range(groups):
                cols = slice(g * gd, (g + 1) * gd)
                mixed_s[rows, cols] = (jnp.dot(wsm[g].astype(BF16), vn_s[rows, cols], preferred_element_type=F32)
                                       + bmap_ref[:, cols])
        yb = piece(3) * mixed_s[...]
        merged = jax.nn.sigmoid(piece(5)) * ya + jax.nn.sigmoid(piece(6)) * yb
        o_ref[...] = x_ref[...] + jnp.dot(merged.astype(BF16), wout_ref[...], preferred_element_type=F32)

    return pl.pallas_call(
        body, name=name, grid=(n // tm,),
        in_specs=[pl.BlockSpec((tm, N_PIECES * d), lambda i: (i, 0)),
                  pl.BlockSpec((HALO, d), lambda i: (prev(i), 1)),
                  pl.BlockSpec((HALO, d), lambda i: (prev(i), 2)),
                  pl.BlockSpec((tm, d), lambda i: (i, 0)),
                  _full((3, d)), _full((1, d)), _full((1, d)), _full((groups, CHUNK, CHUNK)), _full((CHUNK, d)),
                  _resident((d, d), lambda i: (0, 0))],
        out_specs=[pl.BlockSpec((tm, d), lambda i: (i, 0)), pl.BlockSpec((tm, d), lambda i: (i, 0))],
        out_shape=[jax.ShapeDtypeStruct((n, d), F32), jax.ShapeDtypeStruct((n, d), BF16)],
        scratch_shapes=[pltpu.VMEM((tm, d), BF16), pltpu.VMEM((tm, d), F32)],
        compiler_params=pltpu.CompilerParams(dimension_semantics=("parallel",), vmem_limit_bytes=VMEM_LIMIT_V7X),
    )(proj, proj, proj, x, cw, lg, lb, ws, bmap, wout)


def _ffn_fwd(up, x, cw, wd, *, seq, tm, cwid, name):
    n, d = x.shape
    ff = wd.shape[0]
    prev = _halo_prev(tm)

    def body(up_ref, pup_ref, x_ref, cw_ref, wd_ref, o_ref, conv_ref):
        seq_start = (pl.program_id(0) * tm) % seq == 0

        def conv(lo):
            cols = slice(lo, lo + cwid)
            cur = up_ref[:, cols].astype(F32)
            pre = jnp.where(seq_start, 0.0, pup_ref[:, cols].astype(F32))
            out = (cw_ref[0:1, cols] * _shift_prev(pre, cur, 2) + cw_ref[1:2, cols] * _shift_prev(pre, cur, 1)
                   + cw_ref[2:3, cols] * cur)
            conv_ref[:, cols] = out.astype(BF16)
            return out

        acc = x_ref[...]
        for cj in range(ff // cwid):
            gate = conv(cj * cwid)
            val = conv(ff + cj * cwid)
            a = gate * jax.nn.sigmoid(gate) * val
            acc = acc + jnp.dot(a.astype(BF16), wd_ref[cj * cwid:(cj + 1) * cwid, :], preferred_element_type=F32)
        o_ref[...] = acc

    return pl.pallas_call(
        body, name=name, grid=(n // tm,),
        in_specs=[pl.BlockSpec((tm, 2 * ff), lambda i: (i, 0)),
                  pl.BlockSpec((HALO, 2 * ff), lambda i: (prev(i), 0)),
                  pl.BlockSpec((tm, d), lambda i: (i, 0)),
                  _full((3, 2 * ff)),
                  _resident((ff, d), lambda i: (0, 0))],
        out_specs=[pl.BlockSpec((tm, d), lambda i: (i, 0)), pl.BlockSpec((tm, 2 * ff), lambda i: (i, 0))],
        out_shape=[jax.ShapeDtypeStruct((n, d), F32), jax.ShapeDtypeStruct((n, 2 * ff), BF16)],
        compiler_params=pltpu.CompilerParams(dimension_semantics=("parallel",), vmem_limit_bytes=VMEM_LIMIT_V7X),
    )(up, up, x, cw, wd)


def _final_loss(x, g, target, *, tm, name):
    n, d = x.shape

    def body(x_ref, g_ref, t_ref, dx_ref, loss_ref, dg_ref):
        @pl.when(pl.program_id(0) == 0)
        def _():
            loss_ref[...] = jnp.zeros_like(loss_ref)
            dg_ref[...] = jnp.zeros_like(dg_ref)

        xf = x_ref[...]
        r = lax.rsqrt(_lanemean(xf * xf) + EPS)
        xhat = xf * r
        diff = xhat * g_ref[...] - t_ref[...]
        loss_ref[...] += (0.5 / d) * _rowsum(jnp.sum(diff * diff, axis=-1, keepdims=True))
        dy = diff * (1.0 / d)
        dg_ref[...] += _rowsum(dy * xhat)
        dyh = dy * g_ref[...]
        dx_ref[...] = r * (dyh - xhat * _lanemean(dyh * xhat))

    return pl.pallas_call(
        body, name=name, grid=(n // tm,),
        in_specs=[pl.BlockSpec((tm, d), lambda i: (i, 0)), _full((1, d)), pl.BlockSpec((tm, d), lambda i: (i, 0))],
        out_specs=[pl.BlockSpec((tm, d), lambda i: (i, 0)), _full((HALO, CHUNK)), _full((1, d))],
        out_shape=[jax.ShapeDtypeStruct((n, d), F32), jax.ShapeDtypeStruct((HALO, CHUNK), F32),
                   jax.ShapeDtypeStruct((1, d), F32)],
        compiler_params=pltpu.CompilerParams(dimension_semantics=("arbitrary",), vmem_limit_bytes=VMEM_LIMIT_V7X),
    )(x, g, target)


def _ffn_bwd(dx, up, conv, cw, wd, *, seq, tm, cwid, name):
    n, d = dx.shape
    ff = wd.shape[0]
    nxt = _halo_next(tm, n)

    def body(dx_ref, dxn_ref, up_ref, conv_ref, nconv_ref, cw_ref, wd_ref, dup_ref, dwd_ref, dcw_ref):
        i = pl.program_id(0)
        keep_next = jnp.where(((i + 1) * tm) % seq == 0, 0.0, 1.0)

        @pl.when(i == 0)
        def _():
            dwd_ref[...] = jnp.zeros_like(dwd_ref)
            dcw_ref[...] = jnp.zeros_like(dcw_ref)

        dxe = jnp.concatenate([dx_ref[...], dxn_ref[...]], axis=0).astype(BF16)
        dxb = dxe[:tm]
        for cj in range(ff // cwid):
            rows = slice(cj * cwid, (cj + 1) * cwid)
            g_cols, v_cols = slice(cj * cwid, (cj + 1) * cwid), slice(ff + cj * cwid, ff + (cj + 1) * cwid)
            dae = lax.dot_general(dxe, wd_ref[rows, :], NT_DIMS, preferred_element_type=F32)
            da, dan = dae[:tm], dae[tm:]

            def grads(gate, val, da_rows):
                sg = jax.nn.sigmoid(gate)
                sl = gate * sg
                return sl, da_rows * val * sg * (1.0 + gate * (1.0 - sg)), da_rows * sl

            gate, val = conv_ref[:, g_cols].astype(F32), conv_ref[:, v_cols].astype(F32)
            sl, d_gate, d_val = grads(gate, val, da)
            dwd_ref[rows, :] += lax.dot_general((sl * val).astype(BF16), dxb, TN_DIMS, preferred_element_type=F32)
            _, d_gate_n, d_val_n = grads(nconv_ref[:, g_cols].astype(F32), nconv_ref[:, v_cols].astype(F32),
                                         dan * keep_next)
            for cols, dcur, dnext in ((g_cols, d_gate, d_gate_n), (v_cols, d_val, d_val_n)):
                upc = up_ref[:, cols].astype(F32)
                d1, d2 = _shift_next(dcur, dnext, 1), _shift_next(dcur, dnext, 2)
                for k, dk in enumerate((d2, d1, dcur)):
                    dcw_ref[k:k + 1, cols] += _rowsum(dk * upc)
                dup = cw_ref[2:3, cols] * dcur + cw_ref[1:2, cols] * d1 + cw_ref[0:1, cols] * d2
                dup_ref[:, cols] = dup.astype(BF16)

    return pl.pallas_call(
        body, name=name, grid=(n // tm,),
        in_specs=[pl.BlockSpec((tm, d), lambda i: (i, 0)),
                  pl.BlockSpec((HALO, d), lambda i: (nxt(i), 0)),
                  pl.BlockSpec((tm, 2 * ff), lambda i: (i, 0)),
                  pl.BlockSpec((tm, 2 * ff), lambda i: (i, 0)),
                  pl.BlockSpec((HALO, 2 * ff), lambda i: (nxt(i), 0)),
                  _full((3, 2 * ff)),
                  _resident((ff, d), lambda i: (0, 0))],
        out_specs=[pl.BlockSpec((tm, 2 * ff), lambda i: (i, 0)), _full((ff, d)), _full((3, 2 * ff))],
        out_shape=[jax.ShapeDtypeStruct((n, 2 * ff), BF16), jax.ShapeDtypeStruct((ff, d), F32),
                   jax.ShapeDtypeStruct((3, 2 * ff), F32)],
        compiler_params=pltpu.CompilerParams(dimension_semantics=("arbitrary",), vmem_limit_bytes=VMEM_LIMIT_V7X),
    )(dx, dx, up, conv, conv, cw, wd)


def _matmul_bwd_x(dy, wg, x, g, dres, *, tm, name, jobs=()):
    n, d = x.shape
    n_chip, _, ck = wg.shape

    def main(ins, outs, _):
        dy_ref, w_ref, x_ref, g_ref, dres_ref = ins
        dx_ref, dg_ref = outs

        @pl.when(pl.program_id(0) == 0)
        def _():
            dg_ref[...] = jnp.zeros_like(dg_ref)

        dh = lax.dot_general(dy_ref[:, 0:ck], w_ref[0], NT_DIMS, preferred_element_type=F32)
        for j in range(1, n_chip):
            dh = dh + lax.dot_general(dy_ref[:, j * ck:(j + 1) * ck], w_ref[j], NT_DIMS, preferred_element_type=F32)
        xf = x_ref[...]
        r = lax.rsqrt(_lanemean(xf * xf) + EPS)
        xhat = xf * r
        dg_ref[...] += _rowsum(dh * xhat)
        dyh = dh * g_ref[...]
        dx_ref[...] = dres_ref[...] + r * (dyh - xhat * _lanemean(dyh * xhat))

    return _hosted_call(
        main, name=name, grid=(n // tm,), jobs=jobs, semantics=("arbitrary",), operands=(dy, wg, x, g, dres),
        in_specs=[pl.BlockSpec((tm, n_chip * ck), lambda i: (i, 0)), _resident((n_chip, d, ck), lambda i: (0, 0, 0)),
                  pl.BlockSpec((tm, d), lambda i: (i, 0)), _full((1, d)), pl.BlockSpec((tm, d), lambda i: (i, 0))],
        out_specs=[pl.BlockSpec((tm, d), lambda i: (i, 0)), _full((1, d))],
        out_shape=[jax.ShapeDtypeStruct((n, d), F32), jax.ShapeDtypeStruct((1, d), F32)])


def _matmul_bwd_w(ht, dy, n_chip, *, tk, name, jobs=()):
    d, n = ht.shape
    ck = dy.shape[1] // n_chip

    def main(ins, outs, _):
        ht_ref, dy_ref = ins
        o_ref, = outs

        @pl.when(pl.program_id(1) == 0)
        def _():
            o_ref[...] = jnp.zeros_like(o_ref)

        o_ref[...] += jnp.dot(ht_ref[...], dy_ref[...], preferred_element_type=F32)

    return _hosted_call(
        main, name=name, grid=(n_chip, n // tk), jobs=jobs, semantics=("parallel", "arbitrary"), operands=(ht, dy),
        in_specs=[pl.BlockSpec((d, tk), lambda j, k: (0, k)), pl.BlockSpec((tk, ck), lambda j, k: (k, j))],
        out_specs=[pl.BlockSpec((None, d, ck), lambda j, k: (j, 0, 0))],
        out_shape=[jax.ShapeDtypeStruct((n_chip, d, ck), F32)])


def _mixer_bwd(dx, proj, cz, cw, lg, lb, ws, bmap, wout, *, seq, tm, name):
    n, d = dx.shape
    groups = ws.shape[0]
    gd = d // groups
    nxt = _halo_next(tm, n)
    n_tiles = n // tm

    def body(proj_ref, cz_ref, nbg_ref, nga_ref, dx_ref, dxn_ref, cw_ref, lg_ref, lb_ref, ws_ref,
             bmap_ref, wout_ref, dproj_ref, dwout_ref, dcw_ref, dlg_ref, dlb_ref, dws_ref, dbs_ref,
             vn_s, mixed_s, dmix_s, dvn_s, dbmap_s):
        i = pl.program_id(0)
        keep_next = jnp.where(((i + 1) * tm) % seq == 0, 0.0, 1.0)

        @pl.when(i == 0)
        def _():
            for ref in (dwout_ref, dcw_ref, dlg_ref, dlb_ref, dws_ref, dbmap_s):
                ref[...] = jnp.zeros_like(ref)

        def piece(k):
            return proj_ref[:, k * d:(k + 1) * d].astype(F32)

        def put(k, val):
            dproj_ref[:, k * d:(k + 1) * d] = val.astype(BF16)

        w = [cw_ref[k:k + 1, :] for k in range(3)]
        cg, xi = piece(1), piece(2)
        cz = cz_ref[...].astype(F32)
        bg = piece(0)
        ya = bg * cz

        v = piece(4)
        xc = v - _lanemean(v)
        rstd = lax.rsqrt(_lanemean(xc * xc) + EPS)
        vhat = xc * rstd
        vn_s[...] = (vhat * lg_ref[...] + lb_ref[...]).astype(BF16)
        tril, wsm = _tril_weights(ws_ref, groups)
        for ck in range(tm // CHUNK):
            rows = slice(ck * CHUNK, (ck + 1) * CHUNK)
            for g in range(groups):
                cols = slice(g * gd, (g + 1) * gd)
                mixed_s[rows, cols] = (jnp.dot(wsm[g].astype(BF16), vn_s[rows, cols], preferred_element_type=F32)
                                       + bmap_ref[:, cols])
        u = piece(3)
        mixed = mixed_s[...]
        yb = u * mixed
        sa, sb = jax.nn.sigmoid(piece(5)), jax.nn.sigmoid(piece(6))
        merged = sa * ya + sb * yb

        dxe = jnp.concatenate([dx_ref[...], dxn_ref[...]], axis=0).astype(BF16)
        dme = lax.dot_general(dxe, wout_ref[...], NT_DIMS, preferred_element_type=F32)
        dm, dm_n = dme[:tm], dme[tm:]
        dwout_ref[...] += lax.dot_general(merged.astype(BF16), dxe[:tm], TN_DIMS, preferred_element_type=F32)

        put(5, dm * ya * sa * (1.0 - sa))
        put(6, dm * yb * sb * (1.0 - sb))
        d_ya, d_yb = dm * sa, dm * sb
        put(0, d_ya * cz)
        d_cz = d_ya * bg
        d_cz_n = dm_n * jax.nn.sigmoid(nga_ref[...].astype(F32)) * nbg_ref[...].astype(F32) * keep_next
        d_cz1, d_cz2 = _shift_next(d_cz, d_cz_n, 1), _shift_next(d_cz, d_cz_n, 2)
        z = cg * xi
        for k, dk in enumerate((d_cz2, d_cz1, d_cz)):
            dcw_ref[k:k + 1, :] += _rowsum(dk * z)
        dz = w[2] * d_cz + w[1] * d_cz1 + w[0] * d_cz2
        put(1, dz * xi)
        put(2, dz * cg)

        put(3, d_yb * mixed)
        d_mixed = d_yb * u
        dmix_s[...] = d_mixed.astype(BF16)
        for ck in range(tm // CHUNK):
            rows = slice(ck * CHUNK, (ck + 1) * CHUNK)
            dbmap_s[...] += d_mixed[rows, :]
            for g in range(groups):
                cols = slice(g * gd, (g + 1) * gd)
                dvn_s[rows, cols] = jnp.dot(wsm[g].T.astype(BF16), dmix_s[rows, cols], preferred_element_type=F32)
                dws_ref[g] += jnp.where(
                    tril, lax.dot_general(dmix_s[rows, cols], vn_s[rows, cols], NT_DIMS, preferred_element_type=F32),
                    0.0)
        d_vn = dvn_s[...]
        dlg_ref[...] += _rowsum(d_vn * vhat)
        dlb_ref[...] += _rowsum(d_vn)
        d_vhat = d_vn * lg_ref[...]
        put(4, rstd * (d_vhat - _lanemean(d_vhat) - vhat * _lanemean(d_vhat * vhat)))

        @pl.when(i == n_tiles - 1)
        def _():
            for g in range(groups):
                dbs_ref[:, g:g + 1] = jnp.sum(dbmap_s[:, g * gd:(g + 1) * gd], axis=-1, keepdims=True)

    return pl.pallas_call(
        body, name=name, grid=(n_tiles,),
        in_specs=[pl.BlockSpec((tm, N_PIECES * d), lambda i: (i, 0)),
                  pl.BlockSpec((tm, d), lambda i: (i, 0)),
                  pl.BlockSpec((HALO, d), lambda i: (nxt(i), 0)),
                  pl.BlockSpec((HALO, d), lambda i: (nxt(i), 5)),
                  pl.BlockSpec((tm, d), lambda i: (i, 0)),
                  pl.BlockSpec((HALO, d), lambda i: (nxt(i), 0)),
                  _full((3, d)), _full((1, d)), _full((1, d)), _full((groups, CHUNK, CHUNK)), _full((CHUNK, d)),
                  _resident((d, d), lambda i: (0, 0))],
        out_specs=[pl.BlockSpec((tm, N_PIECES * d), lambda i: (i, 0)), _full((d, d)),
                   _full((3, d)), _full((1, d)), _full((1, d)), _full((groups, CHUNK, CHUNK)), _full((CHUNK, groups))],
        out_shape=[jax.ShapeDtypeStruct((n, N_PIECES * d), BF16), jax.ShapeDtypeStruct((d, d), F32),
                   jax.ShapeDtypeStruct((3, d), F32), jax.ShapeDtypeStruct((1, d), F32),
                   jax.ShapeDtypeStruct((1, d), F32), jax.ShapeDtypeStruct((groups, CHUNK, CHUNK), F32),
                   jax.ShapeDtypeStruct((CHUNK, groups), F32)],
        scratch_shapes=[pltpu.VMEM((tm, d), BF16), pltpu.VMEM((tm, d), F32), pltpu.VMEM((tm, d), BF16),
                        pltpu.VMEM((tm, d), F32), pltpu.VMEM((CHUNK, d), F32)],
        compiler_params=pltpu.CompilerParams(dimension_semantics=("arbitrary",), vmem_limit_bytes=VMEM_LIMIT_V7X),
    )(proj, cz, proj, proj, dx, dx, cw, lg, lb, ws, bmap, wout)


def _gather_small(pack, name):
    def body(p_ref, o_ref, send_sem, recv_sem):
        x, y, c = _mesh_pos()
        chip = 2 * x + y
        o_ref[chip] = p_ref[...]
        copies = []
        for r in (1, 2, 3):
            px, py = _peer_chip(x, y, r)
            copies.append(_remote(p_ref, o_ref.at[chip], (send_sem, recv_sem), r - 1, (px, py, c)))
        for cp in copies:
            cp.start()
        for r, cp in zip((1, 2, 3), copies):
            px, py = _peer_chip(x, y, r)
            landed = o_ref.at[2 * px + py]
            _remote(landed, landed, (send_sem, recv_sem), r - 1, (px, py, c)).wait_recv()
            cp.wait_send()

    return pl.pallas_call(
        body, name=name, in_specs=[VMEM_SPEC], out_specs=VMEM_SPEC,
        out_shape=jax.ShapeDtypeStruct((N_CHIP,) + pack.shape, pack.dtype),
        scratch_shapes=[pltpu.SemaphoreType.DMA((3,)), pltpu.SemaphoreType.DMA((3,))],
    )(pack)


def _all_reduce_small(pack, name, jobs=()):
    rows = pack.shape[0]
    rh = rows // 2
    assert rh % HALO == 0, rows

    def main(ins, outs, scratch):
        p_ref, = ins
        o_ref, = outs
        sib_buf, chip_buf, got_buf, send_sem, recv_sem = scratch
        sems = (send_sem, recv_sem)
        x, y, c = _mesh_pos()
        chip, sibling = 2 * x + y, (x, y, 1 - c)
        mine = pl.ds(pl.multiple_of(c * rh, HALO), rh)
        theirs = pl.ds(pl.multiple_of((1 - c) * rh, HALO), rh)
        swap = _remote(p_ref.at[theirs], sib_buf, sems, 0, sibling)
        swap.start()
        swap.wait()
        chip_buf[chip] = p_ref[mine, :] + sib_buf[...]
        copies = []
        for r in (1, 2, 3):
            px, py = _peer_chip(x, y, r)
            copies.append(_remote(chip_buf.at[chip], chip_buf.at[chip], sems, r, (px, py, c)))
        for cp in copies:
            cp.start()
        for r, cp in zip((1, 2, 3), copies):
            px, py = _peer_chip(x, y, r)
            landed = chip_buf.at[2 * px + py]
            _remote(landed, landed, sems, r, (px, py, c)).wait_recv()
            cp.wait_send()
        total = ((chip_buf[0] + chip_buf[1]) + chip_buf[2]) + chip_buf[3]
        o_ref[mine, :] = total
        chip_buf[chip] = total
        share = _remote(chip_buf.at[chip], got_buf, sems, 4, sibling)
        share.start()
        share.wait()
        o_ref[theirs, :] = got_buf[...]

    outs, job_outs = _hosted_call(
        main, name=name, grid=(1,), jobs=jobs, operands=(pack,),
        in_specs=[_full(pack.shape)], out_specs=[_full(pack.shape)],
        out_shape=[jax.ShapeDtypeStruct(pack.shape, pack.dtype)],
        scratch_shapes=[pltpu.VMEM((rh, 128), F32), pltpu.VMEM((N_CHIP, rh, 128), F32), pltpu.VMEM((rh, 128), F32),
                        pltpu.SemaphoreType.DMA((5,)), pltpu.SemaphoreType.DMA((5,))])
    return outs[0], job_outs


def _block_rows(rows, cols, itemsize=4, budget=2 << 20):
    best = None
    for t in range(16, rows + 1, 16):
        if rows % t == 0 and t * cols * itemsize <= budget:
            best = t
    assert best is not None, (rows, cols)
    return best


def _pair_sum(grad, recv, core, name):
    n_chip, rk, ck = grad.shape
    rh = rk // 2
    tr = _block_rows(rh, ck)
    nb = rh // tr

    def body(core_ref, g_ref, r_ref, o_ref):
        o_ref[...] = (g_ref[...] + r_ref[...]).astype(BF16)

    blk = (None, tr, ck)
    return pl.pallas_call(
        body, name=name,
        grid_spec=pltpu.PrefetchScalarGridSpec(
            num_scalar_prefetch=1, grid=(n_chip, nb),
            in_specs=[pl.BlockSpec(blk, lambda k, i, core_ref: (k, core_ref[0] * nb + i, 0)),
                      pl.BlockSpec(blk, lambda k, i, core_ref: (k, i, 0))],
            out_specs=pl.BlockSpec(blk, lambda k, i, core_ref: (k, i, 0))),
        out_shape=jax.ShapeDtypeStruct((n_chip, rh, ck), BF16),
        compiler_params=pltpu.CompilerParams(dimension_semantics=("parallel", "parallel"),
                                             vmem_limit_bytes=VMEM_LIMIT_V7X),
    )(core, grad, recv)


def _final_sum(grad, recv, arrived, where, prev, layer, n_layers, name):
    n_chip, rk, ck = grad.shape
    rh = rk // 2
    tr = _block_rows(rh, ck)
    nb = rh // tr

    def body(where_ref, g_ref, r_ref, a1_ref, a2_ref, a3_ref, *rest):
        o_ref = rest[-1]
        own = g_ref[...] + r_ref[...]
        o_ref[...] = ((own + a1_ref[...].astype(F32)) + a2_ref[...].astype(F32)) + a3_ref[...].astype(F32)

    blk = (None, tr, ck)
    slot = lambda r: pl.BlockSpec(blk, lambda i, w: (r, i, 0))
    extra = [] if prev is None else [prev]
    return pl.pallas_call(
        body, name=name,
        grid_spec=pltpu.PrefetchScalarGridSpec(
            num_scalar_prefetch=1, grid=(nb,),
            in_specs=[pl.BlockSpec(blk, lambda i, w: (w[1], w[0] * nb + i, 0)),
                      pl.BlockSpec(blk, lambda i, w: (w[1], i, 0)),
                      slot(0), slot(1), slot(2)] + [pl.BlockSpec(memory_space=pl.ANY)] * len(extra),
            out_specs=pl.BlockSpec(blk, lambda i, w: (layer, w[0] * nb + i, 0))),
        out_shape=jax.ShapeDtypeStruct((n_layers, rk, ck), F32),
        input_output_aliases={6: 0} if extra else {},
        compiler_params=pltpu.CompilerParams(dimension_semantics=("parallel",), vmem_limit_bytes=VMEM_LIMIT_V7X),
    )(where, grad, recv, arrived, arrived, arrived, *extra)


def _adamw_math(w, g, m, v):
    m = ADAM_B1 * m + (1.0 - ADAM_B1) * g
    v = ADAM_B2 * v + (1.0 - ADAM_B2) * (g * g)
    m_hat = m / (1.0 - ADAM_B1 ** ADAM_STEP)
    v_hat = v / (1.0 - ADAM_B2 ** ADAM_STEP)
    delta = -ADAM_LR * (m_hat / (jnp.sqrt(v_hat) + ADAM_EPS) + ADAM_WD * w)
    return delta, m, v


def _adamw_big(w, g, m, v, name):
    n_l, rk, ck = w.shape
    tr = _block_rows(rk, ck, budget=1 << 20)

    def body(w_ref, g_ref, m_ref, v_ref, d_ref, nm_ref, nv_ref):
        d_ref[...], nm_ref[...], nv_ref[...] = _adamw_math(w_ref[...], g_ref[...], m_ref[...], v_ref[...])

    spec = pl.BlockSpec((None, tr, ck), lambda l, i: (l, i, 0))
    return pl.pallas_call(
        body, name=name, grid=(n_l, rk // tr), in_specs=[spec] * 4, out_specs=[spec] * 3,
        out_shape=[jax.ShapeDtypeStruct(w.shape, F32)] * 3,
        compiler_params=pltpu.CompilerParams(dimension_semantics=("parallel", "parallel"),
                                             vmem_limit_bytes=VMEM_LIMIT_V7X),
    )(w, g, m, v)


def _adamw_small(ws, gs, ms, vs, name):
    n_p = len(ws)

    def body(*refs):
        ins, outs = refs[:4 * n_p], refs[4 * n_p:]
        for p in range(n_p):
            res = _adamw_math(ins[p][...], ins[n_p + p][...], ins[2 * n_p + p][...], ins[3 * n_p + p][...])
            for q in range(3):
                outs[q * n_p + p][...] = res[q]

    outs = pl.pallas_call(
        body, name=name, in_specs=[VMEM_SPEC] * (4 * n_p), out_specs=[VMEM_SPEC] * (3 * n_p),
        out_shape=[jax.ShapeDtypeStruct(w.shape, F32) for w in ws] * 3,
        compiler_params=pltpu.CompilerParams(vmem_limit_bytes=VMEM_LIMIT_V7X),
    )(*ws, *gs, *ms, *vs)
    return outs[:n_p], outs[n_p:2 * n_p], outs[2 * n_p:]


def kernel(x, mix_norm_g, w_in, conv_a_w, ln_v_g, ln_v_b, w_s, b_s, w_out, ffn_norm_g, w_up, conv_ffn_w, w_down, final_norm_g, loss_target, m_mix_norm_g, m_w_in, m_conv_a_w, m_ln_v_g, m_ln_v_b, m_w_s, m_b_s, m_w_out, m_ffn_norm_g, m_w_up, m_conv_ffn_w, m_w_down, m_final_norm_g, v_mix_norm_g, v_w_in, v_conv_a_w, v_ln_v_g, v_ln_v_b, v_w_s, v_b_s, v_w_out, v_ffn_norm_g, v_w_up, v_conv_ffn_w, v_w_down, v_final_norm_g):
    bsz, seq, d = x.shape
    n = bsz * seq
    n_l, groups = w_s.shape[0], w_s.shape[1]
    assert n_l == 2, "the exchange schedule below is written for two layers"
    gd = d // groups
    ff = w_down.shape[1] * N_CHIP
    axes = ("x", "y", "c")
    mx, my, mc = _mesh_pos()
    chip = 2 * mx + my
    core = jnp.reshape(mc, (1,)).astype(jnp.int32)
    where = jnp.stack([mc, chip]).astype(jnp.int32)

    tm_mm = _row_tile(seq, 512)
    tm_ew = _row_tile(seq, 256)
    tk_w = _row_tile(seq, 2048)
    ff_chunk = ff // 2 if (ff // 2) % 128 == 0 else ff
    IN, OUT, UP, DOWN = range(4)

    shards = [w.astype(BF16) for w in (w_in, w_out, w_up, w_down)]
    piece = lambda a, l: (shards[a], l)
    (w_in_0,), = _comm_only([_GatherJob([piece(IN, 0)])], "gather_first")

    taps = jnp.concatenate([conv_a_w.reshape(n_l, -1), conv_ffn_w.reshape(n_l, -1)], axis=1)
    tap_rows = -(-taps.size // 128 // 8) * 8
    tap_pack = jnp.zeros((tap_rows * 128,), F32).at[:taps.size].set(taps.reshape(-1)).reshape(tap_rows, 128)
    tap_all = _gather_small(tap_pack, "gather_taps")
    tap_all = tap_all.reshape(N_CHIP, -1)[:, :taps.size].reshape(N_CHIP, n_l, -1)
    ca = tap_all[:, :, :3 * d // N_CHIP].reshape(N_CHIP, n_l, 3, d // N_CHIP)
    cf = tap_all[:, :, 3 * d // N_CHIP:].reshape(N_CHIP, n_l, 3, 2 * ff // N_CHIP)
    conv_a_full = jnp.transpose(ca, (1, 2, 0, 3)).reshape(n_l, 3, d)
    conv_f_full = jnp.transpose(cf, (1, 2, 0, 3)).reshape(n_l, 3, 2 * ff)

    bmaps = jnp.repeat(jnp.swapaxes(b_s, 1, 2), gd, axis=2)

    xs = x.reshape(n, d)
    tgt = loss_target.reshape(n, d)

    def mixer_args(l, w_out_l):
        return (conv_a_full[l], ln_v_g[l][None], ln_v_b[l][None], w_s[l], bmaps[l], w_out_l.reshape(d, d))

    (proj0, h1_0), ((w_out_0, w_up_0, w_down_0),) = _norm_matmul(
        xs, mix_norm_g[0][None], w_in_0, tm=tm_mm, name="fwd_in_proj_0",
        jobs=[_GatherJob([piece(OUT, 0), piece(UP, 0), piece(DOWN, 0)])])
    x1_0, cz0 = _mixer_fwd(proj0, xs, *mixer_args(0, w_out_0), seq=seq, tm=tm_ew, name="fwd_mixer_0")
    (up0, h2_0), ((w_in_1, w_out_1),) = _norm_matmul(
        x1_0, ffn_norm_g[0][None], w_up_0, tm=tm_mm, name="fwd_up_proj_0",
        jobs=[_GatherJob([piece(IN, 1), piece(OUT, 1)])])
    x2_0, conv0 = _ffn_fwd(up0, x1_0, conv_f_full[0], w_down_0.reshape(ff, d), seq=seq, tm=tm_ew, cwid=ff_chunk,
                           name="fwd_ffn_0")
    (proj1, h1_1), ((w_up_1, w_down_1),) = _norm_matmul(
        x2_0, mix_norm_g[1][None], w_in_1, tm=tm_mm, name="fwd_in_proj_1",
        jobs=[_GatherJob([piece(UP, 1), piece(DOWN, 1)])])
    x1_1, cz1 = _mixer_fwd(proj1, x2_0, *mixer_args(1, w_out_1), seq=seq, tm=tm_ew, name="fwd_mixer_1")
    (up1, h2_1), _ = _norm_matmul(x1_1, ffn_norm_g[1][None], w_up_1, tm=tm_mm, name="fwd_up_proj_1")
    x2_1, conv1 = _ffn_fwd(up1, x1_1, conv_f_full[1], w_down_1.reshape(ff, d), seq=seq, tm=tm_ew, cwid=ff_chunk,
                           name="fwd_ffn_1")
    dx, loss_tile, d_final_g = _final_loss(x2_1, final_norm_g[None], tgt, tm=tm_mm, name="final_loss")
    loss = lax.psum(loss_tile[0, 0], axes)

    def chipwise(a):
        return a.reshape(N_CHIP, a.shape[0] // N_CHIP, a.shape[1])

    def pair_sums(grads, recvs, tag):
        return [_pair_sum(g, r, core, f"grad_pair_sum_{tag}_{a}") for a, (g, r) in enumerate(zip(grads, recvs))]

    d_up, dwd, d_cf1 = _ffn_bwd(dx, up1, conv1, conv_f_full[1], w_down_1.reshape(ff, d), seq=seq, tm=tm_ew,
                                cwid=ff_chunk, name="bwd_ffn_1")
    (dx1, d_g2_1), _ = _matmul_bwd_x(d_up, w_up_1, x1_1, ffn_norm_g[1][None], dx, tm=tm_mm, name="bwd_up_x_1")
    (dwu,), _ = _matmul_bwd_w(h2_1, d_up, N_CHIP, tk=tk_w, name="bwd_up_w_1")
    d_proj, dwo, d_ca1, d_lg1, d_lb1, d_ws1, d_bs1 = _mixer_bwd(
        dx1, proj1, cz1, *mixer_args(1, w_out_1), seq=seq, tm=tm_ew, name="bwd_mixer_1")
    (dx, d_g1_1), _ = _matmul_bwd_x(d_proj, w_in_1, x2_0, mix_norm_g[1][None], dx1, tm=tm_mm, name="bwd_in_x_1")
    (dwi,), _ = _matmul_bwd_w(h1_1, d_proj, N_CHIP, tk=tk_w, name="bwd_in_w_1")
    grads1 = [dwi, chipwise(dwo), dwu, chipwise(dwd)]
    small1 = [d_g1_1, d_ca1, d_lg1, d_lb1, d_ws1, d_bs1.T, d_g2_1, d_cf1]

    d_up, dwd, d_cf0 = _ffn_bwd(dx, up0, conv0, conv_f_full[0], w_down_0.reshape(ff, d), seq=seq, tm=tm_ew,
                                cwid=ff_chunk, name="bwd_ffn_0")
    (dx1, d_g2_0), (recv1,) = _matmul_bwd_x(d_up, w_up_0, x1_0, ffn_norm_g[0][None], dx, tm=tm_mm, name="bwd_up_x_0",
                                            jobs=[_SwapJob(grads1)])
    parts1 = pair_sums(grads1, recv1, 1)
    (dwu,), ((arr1_in,),) = _matmul_bwd_w(h2_0, d_up, N_CHIP, tk=tk_w, name="bwd_up_w_0",
                                          jobs=[_ScatterJob([parts1[IN]])])
    d_proj, dwo, d_ca0, d_lg0, d_lb0, d_ws0, d_bs0 = _mixer_bwd(
        dx1, proj0, cz0, *mixer_args(0, w_out_0), seq=seq, tm=tm_ew, name="bwd_mixer_0")
    early0 = [chipwise(dwo), dwu, chipwise(dwd)]
    (dwi,), (arr1_rest, recv0_early) = _matmul_bwd_w(
        h1_0, d_proj, N_CHIP, tk=tk_w, name="bwd_in_w_0",
        jobs=[_ScatterJob(parts1[OUT:]), _SwapJob(early0)])
    arrived1 = [arr1_in] + list(arr1_rest)
    gsum = [_final_sum(g, r, p, where, None, 1, n_l, f"grad_final_sum_1_{a}")
            for a, (g, r, p) in enumerate(zip(grads1, recv1, arrived1))]
    parts0_early = pair_sums(early0, recv0_early, "0e")
    (dx, d_g1_0), (gsum, arr0_early, (recv0_in,)) = _matmul_bwd_x(
        d_proj, w_in_0, xs, mix_norm_g[0][None], dx1, tm=tm_mm, name="bwd_in_x_0",
        jobs=[_ShareJob([(g, 1) for g in gsum]), _ScatterJob(parts0_early), _SwapJob([dwi])])
    grad_x = dx.reshape(bsz, seq, d)
    grads0 = [dwi] + early0
    recv0 = [recv0_in] + list(recv0_early)
    part0_in, = pair_sums([dwi], [recv0_in], "0i")
    small0 = [d_g1_0, d_ca0, d_lg0, d_lb0, d_ws0, d_bs0.T, d_g2_0, d_cf0]

    flat = [a.reshape(-1) for a in small0 + small1] + [d_final_g.reshape(-1)]
    sizes = [a.size for a in flat]
    total = sum(sizes)
    rows = -(-total // 128 // (2 * HALO)) * 2 * HALO
    pack = jnp.concatenate(flat + [jnp.zeros((rows * 128 - total,), F32)]).reshape(rows, 128)
    red, ((arr0_in,),) = _all_reduce_small(pack, "small_all_reduce", jobs=[_ScatterJob([part0_in])])
    red = red.reshape(-1)
    arrived0 = [arr0_in] + list(arr0_early)
    gsum = [_final_sum(g, r, p, where, prev, 0, n_l, f"grad_final_sum_0_{a}")
            for a, (g, r, p, prev) in enumerate(zip(grads0, recv0, arrived0, gsum))]
    (g_in, g_out, g_up, g_down), = _comm_only([_ShareJob([(g, 0) for g in gsum])], "grad_share_0")

    pieces, off = [], 0
    for s in sizes:
        pieces.append(red[off:off + s])
        off += s
    per_layer = [pieces[l * 8:(l + 1) * 8] for l in range(n_l)]

    def stacked(idx, shape):
        return jnp.stack([per_layer[l][idx].reshape(shape) for l in range(n_l)])

    def my_cols(a, width):
        return lax.dynamic_slice_in_dim(a, chip * width, width, axis=-1)

    g_mix_norm = stacked(0, (d,))
    g_conv_a = my_cols(stacked(1, (3, d)), d // N_CHIP)
    g_ln_g = stacked(2, (d,))
    g_ln_b = stacked(3, (d,))
    g_ws = stacked(4, (groups, CHUNK, CHUNK))
    g_bs = stacked(5, (groups, CHUNK))
    g_ffn_norm = stacked(6, (d,))
    g_conv_f = my_cols(stacked(7, (3, 2 * ff)), 2 * ff // N_CHIP)
    g_final = pieces[-1].reshape(1, d)

    big_names = [(w_in, g_in, m_w_in, v_w_in), (w_out, g_out, m_w_out, v_w_out), (w_up, g_up, m_w_up, v_w_up),
                 (w_down, g_down, m_w_down, v_w_down)]
    upd = [_adamw_big(w, g, m, v, f"adamw_{a}") for a, (w, g, m, v) in enumerate(big_names)]

    sm_w = [mix_norm_g, conv_a_w, ln_v_g, ln_v_b, w_s, b_s, ffn_norm_g, conv_ffn_w, final_norm_g[None]]
    sm_g = [g_mix_norm, g_conv_a, g_ln_g, g_ln_b, g_ws, g_bs, g_ffn_norm, g_conv_f, g_final]
    sm_m = [m_mix_norm_g, m_conv_a_w, m_ln_v_g, m_ln_v_b, m_w_s, m_b_s, m_ffn_norm_g, m_conv_ffn_w, m_final_norm_g[None]]
    sm_v = [v_mix_norm_g, v_conv_a_w, v_ln_v_g, v_ln_v_b, v_w_s, v_b_s, v_ffn_norm_g, v_conv_ffn_w, v_final_norm_g[None]]
    sm_d, sm_nm, sm_nv = _adamw_small(sm_w, sm_g, sm_m, sm_v, "adamw_small")

    def ordered(sm, bigs):
        return [sm[0], bigs[0], sm[1], sm[2], sm[3], sm[4], sm[5], bigs[1], sm[6], bigs[2], sm[7], bigs[3],
                sm[8].reshape(d)]

    out_g = ordered(sm_g, [g_in, g_out, g_up, g_down])
    out_d = ordered(sm_d, [u[0] for u in upd])
    out_m = ordered(sm_nm, [u[1] for u in upd])
    out_v = ordered(sm_nv, [u[2] for u in upd])
    return (loss, grad_x, *out_g, *out_d, *out_m, *out_v)
```

```python
import jax
import jax.numpy as jnp
from jax import lax
from jax.experimental import pallas as pl
from jax.experimental.pallas import tpu as pltpu

F32 = jnp.float32
BF16 = jnp.bfloat16
EPS = 1e-6
CHUNK = 128
N_CHIP = 4
HALO = 8
N_PIECES = 7
LATE_STEPS = 2
VMEM_LIMIT_V7X = 56 * 1024 * 1024
MESH_T = pl.DeviceIdType.MESH
HBM_SPEC = pl.BlockSpec(memory_space=pltpu.HBM)
VMEM_SPEC = pl.BlockSpec(memory_space=pltpu.VMEM)

ADAM_LR, ADAM_B1, ADAM_B2, ADAM_EPS, ADAM_WD, ADAM_STEP = 0.001, 0.9, 0.999, 1e-08, 0.01, 10

NT_DIMS = (((1,), (1,)), ((), ()))
TN_DIMS = (((0,), (0,)), ((), ()))


def _resident(block_shape, index_map):
    return pl.BlockSpec(block_shape, index_map, pipeline_mode=pl.Buffered(1))


def _full(shape):
    return pl.BlockSpec(shape, lambda *_: (0,) * len(shape))


def _row_tile(seq, want):
    t = min(seq, want)
    assert seq % t == 0 and t % CHUNK == 0, (seq, want)
    return t


def _shift_prev(prev8, cur, k):
    ext = jnp.concatenate([prev8, cur], axis=0)
    return pltpu.roll(ext, k, 0)[HALO:]


def _shift_next(cur, next8, k):
    ext = jnp.concatenate([cur, next8], axis=0)
    n = ext.shape[0]
    return pltpu.roll(ext, n - k, 0)[:n - HALO]


def _rowsum(a):
    return jnp.sum(a, axis=0, keepdims=True)


def _lanemean(a):
    return jnp.mean(a, axis=-1, keepdims=True)


def _tril_weights(ws_ref, groups):
    r = lax.broadcasted_iota(jnp.int32, (CHUNK, CHUNK), 0)
    c = lax.broadcasted_iota(jnp.int32, (CHUNK, CHUNK), 1)
    tril = r >= c
    return tril, [jnp.where(tril, ws_ref[g], 0.0) for g in range(groups)]


def _halo_prev(tm):
    return lambda i: jnp.maximum(i * (tm // HALO) - 1, 0)


def _halo_next(tm, n):
    last = n // HALO - 1
    return lambda i: jnp.minimum((i + 1) * (tm // HALO), last)


def _mesh_pos():
    return lax.axis_index("x"), lax.axis_index("y"), lax.axis_index("c")


def _peer_chip(x, y, r):
    return (1 - x if r >> 1 else x), (1 - y if r & 1 else y)


def _remote(src, dst, sems, k, to):
    return pltpu.make_async_remote_copy(src_ref=src, dst_ref=dst, send_sem=sems[0].at[k], recv_sem=sems[1].at[k],
                                        device_id=to, device_id_type=MESH_T)


class _GatherJob:
    def __init__(self, pieces):
        self.pieces = pieces
        self.ins = [p[0] for p in pieces]
        self.out_shapes = [jax.ShapeDtypeStruct((N_CHIP,) + p[0].shape[1:], p[0].dtype) for p in pieces]
        self.aliases = {}
        n = len(pieces)
        self.sems = [pltpu.SemaphoreType.DMA((3 * n,))] * 4 + [pltpu.SemaphoreType.DMA((n,))]

    def _half(self, outs, a, of_chip, core):
        rh = outs[a].shape[1] // 2
        return outs[a].at[of_chip, pl.ds(core * rh, rh), :]

    def _own(self, ins, outs, sems, chip):
        return [pltpu.make_async_copy(ins[a].at[layer], outs[a].at[chip], sems[4].at[a])
                for a, (_, layer) in enumerate(self.pieces)]

    def _sends(self, ins, outs, sems, x, y, c):
        chip, out = 2 * x + y, []
        for a, (_, layer) in enumerate(self.pieces):
            rh = outs[a].shape[1] // 2
            for r in (1, 2, 3):
                px, py = _peer_chip(x, y, r)
                out.append(_remote(ins[a].at[layer, pl.ds(c * rh, rh), :], self._half(outs, a, chip, c), sems[0:2],
                                   3 * a + r - 1, (px, py, c)))
        return out

    def _passes(self, outs, sems, x, y, c, core):
        out = []
        for a in range(len(self.pieces)):
            for r in (1, 2, 3):
                px, py = _peer_chip(x, y, r)
                landed = self._half(outs, a, 2 * px + py, core)
                out.append(_remote(landed, landed, sems[2:4], 3 * a + r - 1, (x, y, 1 - c)))
        return out

    def start(self, ins, outs, sems):
        x, y, c = _mesh_pos()
        for cp in self._own(ins, outs, sems, 2 * x + y) + self._sends(ins, outs, sems, x, y, c):
            cp.start()

    def late(self, ins, outs, sems):
        x, y, c = _mesh_pos()
        passes = self._passes(outs, sems, x, y, c, c)
        k = 0
        for a in range(len(self.pieces)):
            for r in (1, 2, 3):
                px, py = _peer_chip(x, y, r)
                landed = self._half(outs, a, 2 * px + py, c)
                _remote(landed, landed, sems[0:2], k, (px, py, c)).wait_recv()
                passes[k].start()
                k += 1

    def finish(self, ins, outs, sems):
        x, y, c = _mesh_pos()
        for cp in self._passes(outs, sems, x, y, c, 1 - c):
            cp.wait_recv()
        for cp in self._sends(ins, outs, sems, x, y, c) + self._passes(outs, sems, x, y, c, c):
            cp.wait_send()
        for cp in self._own(ins, outs, sems, 2 * x + y):
            cp.wait()


class _SwapJob:
    def __init__(self, pieces):
        self.ins = list(pieces)
        self.out_shapes = [jax.ShapeDtypeStruct((g.shape[0], g.shape[1] // 2, g.shape[2]), g.dtype) for g in pieces]
        self.aliases = {}
        self.sems = [pltpu.SemaphoreType.DMA((len(pieces),))] * 2

    def _copies(self, ins, outs, sems):
        x, y, c = _mesh_pos()
        out = []
        for a in range(len(ins)):
            rh = ins[a].shape[1] // 2
            out.append(_remote(ins[a].at[:, pl.ds((1 - c) * rh, rh), :], outs[a], sems, a, (x, y, 1 - c)))
        return out

    def start(self, ins, outs, sems):
        for cp in self._copies(ins, outs, sems):
            cp.start()

    def finish(self, ins, outs, sems):
        for cp in self._copies(ins, outs, sems):
            cp.wait()


class _ScatterJob:
    def __init__(self, pieces):
        self.ins = list(pieces)
        self.out_shapes = [jax.ShapeDtypeStruct((3,) + p.shape[1:], p.dtype) for p in pieces]
        self.aliases = {}
        self.sems = [pltpu.SemaphoreType.DMA((3 * len(pieces),))] * 2

    def _copies(self, ins, outs, sems):
        x, y, c = _mesh_pos()
        out = []
        for a in range(len(ins)):
            for r in (1, 2, 3):
                px, py = _peer_chip(x, y, r)
                out.append(_remote(ins[a].at[2 * px + py], outs[a].at[r - 1], sems, 3 * a + r - 1, (px, py, c)))
        return out

    def start(self, ins, outs, sems):
        for cp in self._copies(ins, outs, sems):
            cp.start()

    def finish(self, ins, outs, sems):
        for cp in self._copies(ins, outs, sems):
            cp.wait()


class _ShareJob:
    def __init__(self, pieces):
        self.pieces = pieces
        self.ins = [p[0] for p in pieces]
        self.out_shapes = [jax.ShapeDtypeStruct(p[0].shape, p[0].dtype) for p in pieces]
        self.aliases = {a: a for a in range(len(pieces))}
        self.sems = [pltpu.SemaphoreType.DMA((len(pieces),))] * 2

    def _copies(self, outs, sems, core):
        x, y, c = _mesh_pos()
        out = []
        for a, (_, layer) in enumerate(self.pieces):
            rh = outs[a].shape[1] // 2
            rows = outs[a].at[layer, pl.ds(core * rh, rh), :]
            out.append(_remote(rows, rows, sems, a, (x, y, 1 - c)))
        return out

    def start(self, ins, outs, sems):
        for cp in self._copies(outs, sems, lax.axis_index("c")):
            cp.start()

    def finish(self, ins, outs, sems):
        c = lax.axis_index("c")
        for cp in self._copies(outs, sems, c):
            cp.wait_send()
        for cp in self._copies(outs, sems, 1 - c):
            cp.wait_recv()


def _hosted_call(main, *, name, grid, in_specs, out_specs, out_shape, scratch_shapes=(), jobs=(), semantics=None,
                 operands=()):
    n_in, n_out, n_sc = len(in_specs), len(out_specs), len(scratch_shapes)
    counts = [(len(j.ins), len(j.out_shapes), len(j.sems)) for j in jobs]
    j_in, j_out, j_sc = (sum(c[k] for c in counts) for k in range(3))
    aliases, i0, o0 = {}, n_in, n_out
    for j, (ci, co, _) in zip(jobs, counts):
        aliases.update({i0 + a: o0 + b for a, b in j.aliases.items()})
        i0, o0 = i0 + ci, o0 + co

    def body(*refs):
        cuts = [0, n_in, n_in + j_in, n_in + j_in + n_out, n_in + j_in + n_out + j_out,
                n_in + j_in + n_out + j_out + n_sc, len(refs)]
        m_in, jb_in, m_out, jb_out, m_sc, jb_sc = (list(refs[cuts[k]:cuts[k + 1]]) for k in range(6))

        def run(phase):
            i0 = o0 = s0 = 0
            for j, (ci, co, cs) in zip(jobs, counts):
                if hasattr(j, phase):
                    getattr(j, phase)(jb_in[i0:i0 + ci], jb_out[o0:o0 + co], jb_sc[s0:s0 + cs])
                i0, o0, s0 = i0 + ci, o0 + co, s0 + cs

        step, n_steps = 0, 1
        for ax in range(len(grid)):
            step, n_steps = step * grid[ax] + pl.program_id(ax), n_steps * grid[ax]
        if jobs:
            pl.when(step == 0)(lambda: run("start"))
        main(m_in, m_out, m_sc)
        if jobs:
            pl.when(step == max(n_steps - 1 - LATE_STEPS, 0))(lambda: run("late"))
            pl.when(step == n_steps - 1)(lambda: run("finish"))

    if semantics is None or jobs:
        semantics = ("arbitrary",) * len(grid)
    outs = pl.pallas_call(
        body, name=name, grid=grid,
        in_specs=list(in_specs) + [HBM_SPEC] * j_in,
        out_specs=list(out_specs) + [HBM_SPEC] * j_out,
        out_shape=list(out_shape) + [s for j in jobs for s in j.out_shapes],
        scratch_shapes=list(scratch_shapes) + [s for j in jobs for s in j.sems],
        input_output_aliases=aliases,
        compiler_params=pltpu.CompilerParams(dimension_semantics=semantics, vmem_limit_bytes=VMEM_LIMIT_V7X),
    )(*operands, *[a for j in jobs for a in j.ins])
    main_outs, rest, job_outs = list(outs[:n_out]), list(outs[n_out:]), []
    for _, co, _ in counts:
        job_outs.append(rest[:co])
        rest = rest[co:]
    return main_outs, job_outs


def _comm_only(jobs, name):
    return _hosted_call(lambda i, o, s: None, name=name, grid=(1,), in_specs=[], out_specs=[], out_shape=[],
                        jobs=jobs)[1]


def _norm_matmul(x, g, wg, *, tm, name, jobs=()):
    n, d = x.shape
    n_chip, _, ck = wg.shape

    def main(ins, outs, _):
        x_ref, g_ref, w_ref = ins
        o_ref, ht_ref = outs
        xf = x_ref[...]
        r = lax.rsqrt(_lanemean(xf * xf) + EPS)
        h = xf * r * g_ref[...]
        ht_ref[...] = h.T.astype(BF16)
        hb = h.astype(BF16)
        for j in range(n_chip):
            o_ref[:, j * ck:(j + 1) * ck] = jnp.dot(hb, w_ref[j], preferred_element_type=F32).astype(BF16)

    return _hosted_call(
        main, name=name, grid=(n // tm,), jobs=jobs, semantics=("parallel",), operands=(x, g, wg),
        in_specs=[pl.BlockSpec((tm, d), lambda i: (i, 0)), _full((1, d)), _resident((n_chip, d, ck), lambda i: (0, 0, 0))],
        out_specs=[pl.BlockSpec((tm, n_chip * ck), lambda i: (i, 0)), pl.BlockSpec((d, tm), lambda i: (0, i))],
        out_shape=[jax.ShapeDtypeStruct((n, n_chip * ck), BF16), jax.ShapeDtypeStruct((d, n), BF16)])


def _mixer_fwd(proj, x, cw, lg, lb, ws, bmap, wout, *, seq, tm, name):
    n, d = x.shape
    groups = ws.shape[0]
    gd = d // groups
    prev = _halo_prev(tm)

    def body(proj_ref, pcg_ref, pxi_ref, x_ref, cw_ref, lg_ref, lb_ref, ws_ref, bmap_ref, wout_ref, o_ref, cz_ref,
             vn_s, mixed_s):
        seq_start = (pl.program_id(0) * tm) % seq == 0

        def piece(k):
            return proj_ref[:, k * d:(k + 1) * d].astype(F32)

        z = piece(1) * piece(2)
        zprev = jnp.where(seq_start, 0.0, pcg_ref[...].astype(F32) * pxi_ref[...].astype(F32))
        cz = (cw_ref[0:1, :] * _shift_prev(zprev, z, 2) + cw_ref[1:2, :] * _shift_prev(zprev, z, 1)
              + cw_ref[2:3, :] * z)
        cz_ref[...] = cz.astype(BF16)
        ya = piece(0) * cz

        v = piece(4)
        xc = v - _lanemean(v)
        vn = xc * lax.rsqrt(_lanemean(xc * xc) + EPS) * lg_ref[...] + lb_ref[...]
        vn_s[...] = vn.astype(BF16)
        _, wsm = _tril_weights(ws_ref, groups)
        for ck in range(tm // CHUNK):
            rows = slice(ck * CHUNK, (ck + 1) * CHUNK)
            for g in range(groups):
                cols = slice(g * gd, (g + 1) * gd)
                mixed_s[rows, cols] = (jnp.dot(wsm[g].astype(BF16), vn_s[rows, cols], preferred_element_type=F32)
                                       + bmap_ref[:, cols])
        yb = piece(3) * mixed_s[...]
        merged = jax.nn.sigmoid(piece(5)) * ya + jax.nn.sigmoid(piece(6)) * yb
        o_ref[...] = x_ref[...] + jnp.dot(merged.astype(BF16), wout_ref[...], preferred_element_type=F32)

    return pl.pallas_call(
        body, name=name, grid=(n // tm,),
        in_specs=[pl.BlockSpec((tm, N_PIECES * d), lambda i: (i, 0)),
                  pl.BlockSpec((HALO, d), lambda i: (prev(i), 1)),
                  pl.BlockSpec((HALO, d), lambda i: (prev(i), 2)),
                  pl.BlockSpec((tm, d), lambda i: (i, 0)),
                  _full((3, d)), _full((1, d)), _full((1, d)), _full((groups, CHUNK, CHUNK)), _full((CHUNK, d)),
                  _resident((d, d), lambda i: (0, 0))],
        out_specs=[pl.BlockSpec((tm, d), lambda i: (i, 0)), pl.BlockSpec((tm, d), lambda i: (i, 0))],
        out_shape=[jax.ShapeDtypeStruct((n, d), F32), jax.ShapeDtypeStruct((n, d), BF16)],
        scratch_shapes=[pltpu.VMEM((tm, d), BF16), pltpu.VMEM((tm, d), F32)],
        compiler_params=pltpu.CompilerParams(dimension_semantics=("parallel",), vmem_limit_bytes=VMEM_LIMIT_V7X),
    )(proj, proj, proj, x, cw, lg, lb, ws, bmap, wout)


def _ffn_fwd(up, x, cw, wd, *, seq, tm, cwid, name):
    n, d = x.shape
    ff = wd.shape[0]
    prev = _halo_prev(tm)

    def body(up_ref, pup_ref, x_ref, cw_ref, wd_ref, o_ref, conv_ref):
        seq_start = (pl.program_id(0) * tm) % seq == 0

        def conv(lo):
            cols = slice(lo, lo + cwid)
            cur = up_ref[:, cols].astype(F32)
            pre = jnp.where(seq_start, 0.0, pup_ref[:, cols].astype(F32))
            out = (cw_ref[0:1, cols] * _shift_prev(pre, cur, 2) + cw_ref[1:2, cols] * _shift_prev(pre, cur, 1)
                   + cw_ref[2:3, cols] * cur)
            conv_ref[:, cols] = out.astype(BF16)
            return out

        acc = x_ref[...]
        for cj in range(ff // cwid):
            gate = conv(cj * cwid)
            val = conv(ff + cj * cwid)
            a = gate * jax.nn.sigmoid(gate) * val
            acc = acc + jnp.dot(a.astype(BF16), wd_ref[cj * cwid:(cj + 1) * cwid, :], preferred_element_type=F32)
        o_ref[...] = acc

    return pl.pallas_call(
        body, name=name, grid=(n // tm,),
        in_specs=[pl.BlockSpec((tm, 2 * ff), lambda i: (i, 0)),
                  pl.BlockSpec((HALO, 2 * ff), lambda i: (prev(i), 0)),
                  pl.BlockSpec((tm, d), lambda i: (i, 0)),
                  _full((3, 2 * ff)),
                  _resident((ff, d), lambda i: (0, 0))],
        out_specs=[pl.BlockSpec((tm, d), lambda i: (i, 0)), pl.BlockSpec((tm, 2 * ff), lambda i: (i, 0))],
        out_shape=[jax.ShapeDtypeStruct((n, d), F32), jax.ShapeDtypeStruct((n, 2 * ff), BF16)],
        compiler_params=pltpu.CompilerParams(dimension_semantics=("parallel",), vmem_limit_bytes=VMEM_LIMIT_V7X),
    )(up, up, x, cw, wd)


def _final_loss(x, g, target, *, tm, name):
    n, d = x.shape

    def body(x_ref, g_ref, t_ref, dx_ref, loss_ref, dg_ref):
        @pl.when(pl.program_id(0) == 0)
        def _():
            loss_ref[...] = jnp.zeros_like(loss_ref)
            dg_ref[...] = jnp.zeros_like(dg_ref)

        xf = x_ref[...]
        r = lax.rsqrt(_lanemean(xf * xf) + EPS)
        xhat = xf * r
        diff = xhat * g_ref[...] - t_ref[...]
        loss_ref[...] += (0.5 / d) * _rowsum(jnp.sum(diff * diff, axis=-1, keepdims=True))
        dy = diff * (1.0 / d)
        dg_ref[...] += _rowsum(dy * xhat)
        dyh = dy * g_ref[...]
        dx_ref[...] = r * (dyh - xhat * _lanemean(dyh * xhat))

    return pl.pallas_call(
        body, name=name, grid=(n // tm,),
        in_specs=[pl.BlockSpec((tm, d), lambda i: (i, 0)), _full((1, d)), pl.BlockSpec((tm, d), lambda i: (i, 0))],
        out_specs=[pl.BlockSpec((tm, d), lambda i: (i, 0)), _full((HALO, CHUNK)), _full((1, d))],
        out_shape=[jax.ShapeDtypeStruct((n, d), F32), jax.ShapeDtypeStruct((HALO, CHUNK), F32),
                   jax.ShapeDtypeStruct((1, d), F32)],
        compiler_params=pltpu.CompilerParams(dimension_semantics=("arbitrary",), vmem_limit_bytes=VMEM_LIMIT_V7X),
    )(x, g, target)


def _ffn_bwd(dx, up, conv, cw, wd, *, seq, tm, cwid, name):
    n, d = dx.shape
    ff = wd.shape[0]
    nxt = _halo_next(tm, n)

    def body(dx_ref, dxn_ref, up_ref, conv_ref, nconv_ref, cw_ref, wd_ref, dup_ref, dwd_ref, dcw_ref):
        i = pl.program_id(0)
        keep_next = jnp.where(((i + 1) * tm) % seq == 0, 0.0, 1.0)

        @pl.when(i == 0)
        def _():
            dwd_ref[...] = jnp.zeros_like(dwd_ref)
            dcw_ref[...] = jnp.zeros_like(dcw_ref)

        dxe = jnp.concatenate([dx_ref[...], dxn_ref[...]], axis=0).astype(BF16)
        dxb = dxe[:tm]
        for cj in range(ff // cwid):
            rows = slice(cj * cwid, (cj + 1) * cwid)
            g_cols, v_cols = slice(cj * cwid, (cj + 1) * cwid), slice(ff + cj * cwid, ff + (cj + 1) * cwid)
            dae = lax.dot_general(dxe, wd_ref[rows, :], NT_DIMS, preferred_element_type=F32)
            da, dan = dae[:tm], dae[tm:]

            def grads(gate, val, da_rows):
                sg = jax.nn.sigmoid(gate)
                sl = gate * sg
                return sl, da_rows * val * sg * (1.0 + gate * (1.0 - sg)), da_rows * sl

            gate, val = conv_ref[:, g_cols].astype(F32), conv_ref[:, v_cols].astype(F32)
            sl, d_gate, d_val = grads(gate, val, da)
            dwd_ref[rows, :] += lax.dot_general((sl * val).astype(BF16), dxb, TN_DIMS, preferred_element_type=F32)
            _, d_gate_n, d_val_n = grads(nconv_ref[:, g_cols].astype(F32), nconv_ref[:, v_cols].astype(F32),
                                         dan * keep_next)
            for cols, dcur, dnext in ((g_cols, d_gate, d_gate_n), (v_cols, d_val, d_val_n)):
                upc = up_ref[:, cols].astype(F32)
                d1, d2 = _shift_next(dcur, dnext, 1), _shift_next(dcur, dnext, 2)
                for k, dk in enumerate((d2, d1, dcur)):
                    dcw_ref[k:k + 1, cols] += _rowsum(dk * upc)
                dup = cw_ref[2:3, cols] * dcur + cw_ref[1:2, cols] * d1 + cw_ref[0:1, cols] * d2
                dup_ref[:, cols] = dup.astype(BF16)

    return pl.pallas_call(
        body, name=name, grid=(n // tm,),
        in_specs=[pl.BlockSpec((tm, d), lambda i: (i, 0)),
                  pl.BlockSpec((HALO, d), lambda i: (nxt(i), 0)),
                  pl.BlockSpec((tm, 2 * ff), lambda i: (i, 0)),
                  pl.BlockSpec((tm, 2 * ff), lambda i: (i, 0)),
                  pl.BlockSpec((HALO, 2 * ff), lambda i: (nxt(i), 0)),
                  _full((3, 2 * ff)),
                  _resident((ff, d), lambda i: (0, 0))],
        out_specs=[pl.BlockSpec((tm, 2 * ff), lambda i: (i, 0)), _full((ff, d)), _full((3, 2 * ff))],
        out_shape=[jax.ShapeDtypeStruct((n, 2 * ff), BF16), jax.ShapeDtypeStruct((ff, d), F32),
                   jax.ShapeDtypeStruct((3, 2 * ff), F32)],
        compiler_params=pltpu.CompilerParams(dimension_semantics=("arbitrary",), vmem_limit_bytes=VMEM_LIMIT_V7X),
    )(dx, dx, up, conv, conv, cw, wd)


def _matmul_bwd_x(dy, wg, x, g, dres, *, tm, name, jobs=()):
    n, d = x.shape
    n_chip, _, ck = wg.shape

    def main(ins, outs, _):
        dy_ref, w_ref, x_ref, g_ref, dres_ref = ins
        dx_ref, dg_ref = outs

        @pl.when(pl.program_id(0) == 0)
        def _():
            dg_ref[...] = jnp.zeros_like(dg_ref)

        dh = lax.dot_general(dy_ref[:, 0:ck], w_ref[0], NT_DIMS, preferred_element_type=F32)
        for j in range(1, n_chip):
            dh = dh + lax.dot_general(dy_ref[:, j * ck:(j + 1) * ck], w_ref[j], NT_DIMS, preferred_element_type=F32)
        xf = x_ref[...]
        r = lax.rsqrt(_lanemean(xf * xf) + EPS)
        xhat = xf * r
        dg_ref[...] += _rowsum(dh * xhat)
        dyh = dh * g_ref[...]
        dx_ref[...] = dres_ref[...] + r * (dyh - xhat * _lanemean(dyh * xhat))

    return _hosted_call(
        main, name=name, grid=(n // tm,), jobs=jobs, semantics=("arbitrary",), operands=(dy, wg, x, g, dres),
        in_specs=[pl.BlockSpec((tm, n_chip * ck), lambda i: (i, 0)), _resident((n_chip, d, ck), lambda i: (0, 0, 0)),
                  pl.BlockSpec((tm, d), lambda i: (i, 0)), _full((1, d)), pl.BlockSpec((tm, d), lambda i: (i, 0))],
        out_specs=[pl.BlockSpec((tm, d), lambda i: (i, 0)), _full((1, d))],
        out_shape=[jax.ShapeDtypeStruct((n, d), F32), jax.ShapeDtypeStruct((1, d), F32)])


def _matmul_bwd_w(ht, dy, n_chip, *, tk, name, jobs=()):
    d, n = ht.shape
    ck = dy.shape[1] // n_chip

    def main(ins, outs, _):
        ht_ref, dy_ref = ins
        o_ref, = outs

        @pl.when(pl.program_id(1) == 0)
        def _():
            o_ref[...] = jnp.zeros_like(o_ref)

        o_ref[...] += jnp.dot(ht_ref[...], dy_ref[...], preferred_element_type=F32)

    return _hosted_call(
        main, name=name, grid=(n_chip, n // tk), jobs=jobs, semantics=("parallel", "arbitrary"), operands=(ht, dy),
        in_specs=[pl.BlockSpec((d, tk), lambda j, k: (0, k)), pl.BlockSpec((tk, ck), lambda j, k: (k, j))],
        out_specs=[pl.BlockSpec((None, d, ck), lambda j, k: (j, 0, 0))],
        out_shape=[jax.ShapeDtypeStruct((n_chip, d, ck), F32)])


def _mixer_bwd(dx, proj, cz, cw, lg, lb, ws, bmap, wout, *, seq, tm, name):
    n, d = dx.shape
    groups = ws.shape[0]
    gd = d // groups
    nxt = _halo_next(tm, n)
    n_tiles = n // tm

    def body(proj_ref, cz_ref, nbg_ref, nga_ref, dx_ref, dxn_ref, cw_ref, lg_ref, lb_ref, ws_ref,
             bmap_ref, wout_ref, dproj_ref, dwout_ref, dcw_ref, dlg_ref, dlb_ref, dws_ref, dbs_ref,
             vn_s, mixed_s, dmix_s, dvn_s, dbmap_s):
        i = pl.program_id(0)
        keep_next = jnp.where(((i + 1) * tm) % seq == 0, 0.0, 1.0)

        @pl.when(i == 0)
        def _():
            for ref in (dwout_ref, dcw_ref, dlg_ref, dlb_ref, dws_ref, dbmap_s):
                ref[...] = jnp.zeros_like(ref)

        def piece(k):
            return proj_ref[:, k * d:(k + 1) * d].astype(F32)

        def put(k, val):
            dproj_ref[:, k * d:(k + 1) * d] = val.astype(BF16)

        w = [cw_ref[k:k + 1, :] for k in range(3)]
        cg, xi = piece(1), piece(2)
        cz = cz_ref[...].astype(F32)
        bg = piece(0)
        ya = bg * cz

        v = piece(4)
        xc = v - _lanemean(v)
        rstd = lax.rsqrt(_lanemean(xc * xc) + EPS)
        vhat = xc * rstd
        vn_s[...] = (vhat * lg_ref[...] + lb_ref[...]).astype(BF16)
        tril, wsm = _tril_weights(ws_ref, groups)
        for ck in range(tm // CHUNK):
            rows = slice(ck * CHUNK, (ck + 1) * CHUNK)
            for g in range(groups):
                cols = slice(g * gd, (g + 1) * gd)
                mixed_s[rows, cols] = (jnp.dot(wsm[g].astype(BF16), vn_s[rows, cols], preferred_element_type=F32)
                                       + bmap_ref[:, cols])
        u = piece(3)
        mixed = mixed_s[...]
        yb = u * mixed
        sa, sb = jax.nn.sigmoid(piece(5)), jax.nn.sigmoid(piece(6))
        merged = sa * ya + sb * yb

        dxe = jnp.concatenate([dx_ref[...], dxn_ref[...]], axis=0).astype(BF16)
        dme = lax.dot_general(dxe, wout_ref[...], NT_DIMS, preferred_element_type=F32)
        dm, dm_n = dme[:tm], dme[tm:]
        dwout_ref[...] += lax.dot_general(merged.astype(BF16), dxe[:tm], TN_DIMS, preferred_element_type=F32)

        put(5, dm * ya * sa * (1.0 - sa))
        put(6, dm * yb * sb * (1.0 - sb))
        d_ya, d_yb = dm * sa, dm * sb
        put(0, d_ya * cz)
        d_cz = d_ya * bg
        d_cz_n = dm_n * jax.nn.sigmoid(nga_ref[...].astype(F32)) * nbg_ref[...].astype(F32) * keep_next
        d_cz1, d_cz2 = _shift_next(d_cz, d_cz_n, 1), _shift_next(d_cz, d_cz_n, 2)
        z = cg * xi
        for k, dk in enumerate((d_cz2, d_cz1, d_cz)):
            dcw_ref[k:k + 1, :] += _rowsum(dk * z)
        dz = w[2] * d_cz + w[1] * d_cz1 + w[0] * d_cz2
        put(1, dz * xi)
        put(2, dz * cg)

        put(3, d_yb * mixed)
        d_mixed = d_yb * u
        dmix_s[...] = d_mixed.astype(BF16)
        for ck in range(tm // CHUNK):
            rows = slice(ck * CHUNK, (ck + 1) * CHUNK)
            dbmap_s[...] += d_mixed[rows, :]
            for g in range(groups):
                cols = slice(g * gd, (g + 1) * gd)
                dvn_s[rows, cols] = jnp.dot(wsm[g].T.astype(BF16), dmix_s[rows, cols], preferred_element_type=F32)
                dws_ref[g] += jnp.where(
                    tril, lax.dot_general(dmix_s[rows, cols], vn_s[rows, cols], NT_DIMS, preferred_element_type=F32),
                    0.0)
        d_vn = dvn_s[...]
        dlg_ref[...] += _rowsum(d_vn * vhat)
        dlb_ref[...] += _rowsum(d_vn)
        d_vhat = d_vn * lg_ref[...]
        put(4, rstd * (d_vhat - _lanemean(d_vhat) - vhat * _lanemean(d_vhat * vhat)))

        @pl.when(i == n_tiles - 1)
        def _():
            for g in range(groups):
                dbs_ref[:, g:g + 1] = jnp.sum(dbmap_s[:, g * gd:(g + 1) * gd], axis=-1, keepdims=True)

    return pl.pallas_call(
        body, name=name, grid=(n_tiles,),
        in_specs=[pl.BlockSpec((tm, N_PIECES * d), lambda i: (i, 0)),
                  pl.BlockSpec((tm, d), lambda i: (i, 0)),
                  pl.BlockSpec((HALO, d), lambda i: (nxt(i), 0)),
                  pl.BlockSpec((HALO, d), lambda i: (nxt(i), 5)),
                  pl.BlockSpec((tm, d), lambda i: (i, 0)),
                  pl.BlockSpec((HALO, d), lambda i: (nxt(i), 0)),
                  _full((3, d)), _full((1, d)), _full((1, d)), _full((groups, CHUNK, CHUNK)), _full((CHUNK, d)),
                  _resident((d, d), lambda i: (0, 0))],
        out_specs=[pl.BlockSpec((tm, N_PIECES * d), lambda i: (i, 0)), _full((d, d)),
                   _full((3, d)), _full((1, d)), _full((1, d)), _full((groups, CHUNK, CHUNK)), _full((CHUNK, groups))],
        out_shape=[jax.ShapeDtypeStruct((n, N_PIECES * d), BF16), jax.ShapeDtypeStruct((d, d), F32),
                   jax.ShapeDtypeStruct((3, d), F32), jax.ShapeDtypeStruct((1, d), F32),
                   jax.ShapeDtypeStruct((1, d), F32), jax.ShapeDtypeStruct((groups, CHUNK, CHUNK), F32),
                   jax.ShapeDtypeStruct((CHUNK, groups), F32)],
        scratch_shapes=[pltpu.VMEM((tm, d), BF16), pltpu.VMEM((tm, d), F32), pltpu.VMEM((tm, d), BF16),
                        pltpu.VMEM((tm, d), F32), pltpu.VMEM((CHUNK, d), F32)],
        compiler_params=pltpu.CompilerParams(dimension_semantics=("arbitrary",), vmem_limit_bytes=VMEM_LIMIT_V7X),
    )(proj, cz, proj, proj, dx, dx, cw, lg, lb, ws, bmap, wout)


def _gather_small(pack, name):
    def body(p_ref, o_ref, send_sem, recv_sem):
        x, y, c = _mesh_pos()
        chip = 2 * x + y
        o_ref[chip] = p_ref[...]
        copies = []
        for r in (1, 2, 3):
            px, py = _peer_chip(x, y, r)
            copies.append(_remote(p_ref, o_ref.at[chip], (send_sem, recv_sem), r - 1, (px, py, c)))
        for cp in copies:
            cp.start()
        for r, cp in zip((1, 2, 3), copies):
            px, py = _peer_chip(x, y, r)
            landed = o_ref.at[2 * px + py]
            _remote(landed, landed, (send_sem, recv_sem), r - 1, (px, py, c)).wait_recv()
            cp.wait_send()

    return pl.pallas_call(
        body, name=name, in_specs=[VMEM_SPEC], out_specs=VMEM_SPEC,
        out_shape=jax.ShapeDtypeStruct((N_CHIP,) + pack.shape, pack.dtype),
        scratch_shapes=[pltpu.SemaphoreType.DMA((3,)), pltpu.SemaphoreType.DMA((3,))],
    )(pack)


def _all_reduce_small(pack, name, jobs=()):
    rows = pack.shape[0]
    rh = rows // 2
    assert rh % HALO == 0, rows

    def main(ins, outs, scratch):
        p_ref, = ins
        o_ref, = outs
        sib_buf, chip_buf, got_buf, send_sem, recv_sem = scratch
        sems = (send_sem, recv_sem)
        x, y, c = _mesh_pos()
        chip, sibling = 2 * x + y, (x, y, 1 - c)
        mine = pl.ds(pl.multiple_of(c * rh, HALO), rh)
        theirs = pl.ds(pl.multiple_of((1 - c) * rh, HALO), rh)
        swap = _remote(p_ref.at[theirs], sib_buf, sems, 0, sibling)
        swap.start()
        swap.wait()
        chip_buf[chip] = p_ref[mine, :] + sib_buf[...]
        copies = []
        for r in (1, 2, 3):
            px, py = _peer_chip(x, y, r)
            copies.append(_remote(chip_buf.at[chip], chip_buf.at[chip], sems, r, (px, py, c)))
        for cp in copies:
            cp.start()
        for r, cp in zip((1, 2, 3), copies):
            px, py = _peer_chip(x, y, r)
            landed = chip_buf.at[2 * px + py]
            _remote(landed, landed, sems, r, (px, py, c)).wait_recv()
            cp.wait_send()
        total = ((chip_buf[0] + chip_buf[1]) + chip_buf[2]) + chip_buf[3]
        o_ref[mine, :] = total
        chip_buf[chip] = total
        share = _remote(chip_buf.at[chip], got_buf, sems, 4, sibling)
        share.start()
        share.wait()
        o_ref[theirs, :] = got_buf[...]

    outs, job_outs = _hosted_call(
        main, name=name, grid=(1,), jobs=jobs, operands=(pack,),
        in_specs=[_full(pack.shape)], out_specs=[_full(pack.shape)],
        out_shape=[jax.ShapeDtypeStruct(pack.shape, pack.dtype)],
        scratch_shapes=[pltpu.VMEM((rh, 128), F32), pltpu.VMEM((N_CHIP, rh, 128), F32), pltpu.VMEM((rh, 128), F32),
                        pltpu.SemaphoreType.DMA((5,)), pltpu.SemaphoreType.DMA((5,))])
    return outs[0], job_outs


def _block_rows(rows, cols, itemsize=4, budget=2 << 20):
    best = None
    for t in range(16, rows + 1, 16):
        if rows % t == 0 and t * cols * itemsize <= budget:
            best = t
    assert best is not None, (rows, cols)
    return best


def _pair_sum(grad, recv, core, name):
    n_chip, rk, ck = grad.shape
    rh = rk // 2
    tr = _block_rows(rh, ck)
    nb = rh // tr

    def body(core_ref, g_ref, r_ref, o_ref):
        o_ref[...] = (g_ref[...] + r_ref[...]).astype(BF16)

    blk = (None, tr, ck)
    return pl.pallas_call(
        body, name=name,
        grid_spec=pltpu.PrefetchScalarGridSpec(
            num_scalar_prefetch=1, grid=(n_chip, nb),
            in_specs=[pl.BlockSpec(blk, lambda k, i, core_ref: (k, core_ref[0] * nb + i, 0)),
                      pl.BlockSpec(blk, lambda k, i, core_ref: (k, i, 0))],
            out_specs=pl.BlockSpec(blk, lambda k, i, core_ref: (k, i, 0))),
        out_shape=jax.ShapeDtypeStruct((n_chip, rh, ck), BF16),
        compiler_params=pltpu.CompilerParams(dimension_semantics=("parallel", "parallel"),
                                             vmem_limit_bytes=VMEM_LIMIT_V7X),
    )(core, grad, recv)


def _final_sum(grad, recv, arrived, where, prev, layer, n_layers, name):
    n_chip, rk, ck = grad.shape
    rh = rk // 2
    tr = _block_rows(rh, ck)
    nb = rh // tr

    def body(where_ref, g_ref, r_ref, a1_ref, a2_ref, a3_ref, *rest):
        o_ref = rest[-1]
        own = g_ref[...] + r_ref[...]
        o_ref[...] = ((own + a1_ref[...].astype(F32)) + a2_ref[...].astype(F32)) + a3_ref[...].astype(F32)

    blk = (None, tr, ck)
    slot = lambda r: pl.BlockSpec(blk, lambda i, w: (r, i, 0))
    extra = [] if prev is None else [prev]
    return pl.pallas_call(
        body, name=name,
        grid_spec=pltpu.PrefetchScalarGridSpec(
            num_scalar_prefetch=1, grid=(nb,),
            in_specs=[pl.BlockSpec(blk, lambda i, w: (w[1], w[0] * nb + i, 0)),
                      pl.BlockSpec(blk, lambda i, w: (w[1], i, 0)),
                      slot(0), slot(1), slot(2)] + [pl.BlockSpec(memory_space=pl.ANY)] * len(extra),
            out_specs=pl.BlockSpec(blk, lambda i, w: (layer, w[0] * nb + i, 0))),
        out_shape=jax.ShapeDtypeStruct((n_layers, rk, ck), F32),
        input_output_aliases={6: 0} if extra else {},
        compiler_params=pltpu.CompilerParams(dimension_semantics=("parallel",), vmem_limit_bytes=VMEM_LIMIT_V7X),
    )(where, grad, recv, arrived, arrived, arrived, *extra)


def _adamw_math(w, g, m, v):
    m = ADAM_B1 * m + (1.0 - ADAM_B1) * g
    v = ADAM_B2 * v + (1.0 - ADAM_B2) * (g * g)
    m_hat = m / (1.0 - ADAM_B1 ** ADAM_STEP)
    v_hat = v / (1.0 - ADAM_B2 ** ADAM_STEP)
    delta = -ADAM_LR * (m_hat / (jnp.sqrt(v_hat) + ADAM_EPS) + ADAM_WD * w)
    return delta, m, v


def _adamw_big(w, g, m, v, name):
    n_l, rk, ck = w.shape
    tr = _block_rows(rk, ck, budget=1 << 20)

    def body(w_ref, g_ref, m_ref, v_ref, d_ref, nm_ref, nv_ref):
        d_ref[...], nm_ref[...], nv_ref[...] = _adamw_math(w_ref[...], g_ref[...], m_ref[...], v_ref[...])

    spec = pl.BlockSpec((None, tr, ck), lambda l, i: (l, i, 0))
    return pl.pallas_call(
        body, name=name, grid=(n_l, rk // tr), in_specs=[spec] * 4, out_specs=[spec] * 3,
        out_shape=[jax.ShapeDtypeStruct(w.shape, F32)] * 3,
        compiler_params=pltpu.CompilerParams(dimension_semantics=("parallel", "parallel"),
                                             vmem_limit_bytes=VMEM_LIMIT_V7X),
    )(w, g, m, v)


def _adamw_small(ws, gs, ms, vs, name):
    n_p = len(ws)

    def body(*refs):
        ins, outs = refs[:4 * n_p], refs[4 * n_p:]
        for p in range(n_p):
            res = _adamw_math(ins[p][...], ins[n_p + p][...], ins[2 * n_p + p][...], ins[3 * n_p + p][...])
            for q in range(3):
                outs[q * n_p + p][...] = res[q]

    outs = pl.pallas_call(
        body, name=name, in_specs=[VMEM_SPEC] * (4 * n_p), out_specs=[VMEM_SPEC] * (3 * n_p),
        out_shape=[jax.ShapeDtypeStruct(w.shape, F32) for w in ws] * 3,
        compiler_params=pltpu.CompilerParams(vmem_limit_bytes=VMEM_LIMIT_V7X),
    )(*ws, *gs, *ms, *vs)
    return outs[:n_p], outs[n_p:2 * n_p], outs[2 * n_p:]


def kernel(x, mix_norm_g, w_in, conv_a_w, ln_v_g, ln_v_b, w_s, b_s, w_out, ffn_norm_g, w_up, conv_ffn_w, w_down, final_norm_g, loss_target, m_mix_norm_g, m_w_in, m_conv_a_w, m_ln_v_g, m_ln_v_b, m_w_s, m_b_s, m_w_out, m_ffn_norm_g, m_w_up, m_conv_ffn_w, m_w_down, m_final_norm_g, v_mix_norm_g, v_w_in, v_conv_a_w, v_ln_v_g, v_ln_v_b, v_w_s, v_b_s, v_w_out, v_ffn_norm_g, v_w_up, v_conv_ffn_w, v_w_down, v_final_norm_g):
    bsz, seq, d = x.shape
    n = bsz * seq
    n_l, groups = w_s.shape[0], w_s.shape[1]
    assert n_l == 2, "the exchange schedule below is written for two layers"
    gd = d // groups
    ff = w_down.shape[1] * N_CHIP
    axes = ("x", "y", "c")
    mx, my, mc = _mesh_pos()
    chip = 2 * mx + my
    core = jnp.reshape(mc, (1,)).astype(jnp.int32)
    where = jnp.stack([mc, chip]).astype(jnp.int32)

    tm_mm = _row_tile(seq, 512)
    tm_ew = _row_tile(seq, 256)
    tk_w = _row_tile(seq, 2048)
    ff_chunk = ff // 2 if (ff // 2) % 128 == 0 else ff
    IN, OUT, UP, DOWN = range(4)

    shards = [w.astype(BF16) for w in (w_in, w_out, w_up, w_down)]
    piece = lambda a, l: (shards[a], l)
    (w_in_0,), = _comm_only([_GatherJob([piece(IN, 0)])], "gather_first")

    taps = jnp.concatenate([conv_a_w.reshape(n_l, -1), conv_ffn_w.reshape(n_l, -1)], axis=1)
    tap_rows = -(-taps.size // 128 // 8) * 8
    tap_pack = jnp.zeros((tap_rows * 128,), F32).at[:taps.size].set(taps.reshape(-1)).reshape(tap_rows, 128)
    tap_all = _gather_small(tap_pack, "gather_taps")
    tap_all = tap_all.reshape(N_CHIP, -1)[:, :taps.size].reshape(N_CHIP, n_l, -1)
    ca = tap_all[:, :, :3 * d // N_CHIP].reshape(N_CHIP, n_l, 3, d // N_CHIP)
    cf = tap_all[:, :, 3 * d // N_CHIP:].reshape(N_CHIP, n_l, 3, 2 * ff // N_CHIP)
    conv_a_full = jnp.transpose(ca, (1, 2, 0, 3)).reshape(n_l, 3, d)
    conv_f_full = jnp.transpose(cf, (1, 2, 0, 3)).reshape(n_l, 3, 2 * ff)

    bmaps = jnp.repeat(jnp.swapaxes(b_s, 1, 2), gd, axis=2)

    xs = x.reshape(n, d)
    tgt = loss_target.reshape(n, d)

    def mixer_args(l, w_out_l):
        return (conv_a_full[l], ln_v_g[l][None], ln_v_b[l][None], w_s[l], bmaps[l], w_out_l.reshape(d, d))

    (proj0, h1_0), ((w_out_0, w_up_0, w_down_0),) = _norm_matmul(
        xs, mix_norm_g[0][None], w_in_0, tm=tm_mm, name="fwd_in_proj_0",
        jobs=[_GatherJob([piece(OUT, 0), piece(UP, 0), piece(DOWN, 0)])])
    x1_0, cz0 = _mixer_fwd(proj0, xs, *mixer_args(0, w_out_0), seq=seq, tm=tm_ew, name="fwd_mixer_0")
    (up0, h2_0), ((w_in_1, w_out_1),) = _norm_matmul(
        x1_0, ffn_norm_g[0][None], w_up_0, tm=tm_mm, name="fwd_up_proj_0",
        jobs=[_GatherJob([piece(IN, 1), piece(OUT, 1)])])
    x2_0, conv0 = _ffn_fwd(up0, x1_0, conv_f_full[0], w_down_0.reshape(ff, d), seq=seq, tm=tm_ew, cwid=ff_chunk,
                           name="fwd_ffn_0")
    (proj1, h1_1), ((w_up_1, w_down_1),) = _norm_matmul(
        x2_0, mix_norm_g[1][None], w_in_1, tm=tm_mm, name="fwd_in_proj_1",
        jobs=[_GatherJob([piece(UP, 1), piece(DOWN, 1)])])
    x1_1, cz1 = _mixer_fwd(proj1, x2_0, *mixer_args(1, w_out_1), seq=seq, tm=tm_ew, name="fwd_mixer_1")
    (up1, h2_1), _ = _norm_matmul(x1_1, ffn_norm_g[1][None], w_up_1, tm=tm_mm, name="fwd_up_proj_1")
    x2_1, conv1 = _ffn_fwd(up1, x1_1, conv_f_full[1], w_down_1.reshape(ff, d), seq=seq, tm=tm_ew, cwid=ff_chunk,
                           name="fwd_ffn_1")
    dx, loss_tile, d_final_g = _final_loss(x2_1, final_norm_g[None], tgt, tm=tm_mm, name="final_loss")
    loss = lax.psum(loss_tile[0, 0], axes)

    def chipwise(a):
        return a.reshape(N_CHIP, a.shape[0] // N_CHIP, a.shape[1])

    def pair_sums(grads, recvs, tag):
        return [_pair_sum(g, r, core, f"grad_pair_sum_{tag}_{a}") for a, (g, r) in enumerate(zip(grads, recvs))]

    d_up, dwd, d_cf1 = _ffn_bwd(dx, up1, conv1, conv_f_full[1], w_down_1.reshape(ff, d), seq=seq, tm=tm_ew,
                                cwid=ff_chunk, name="bwd_ffn_1")
    (dx1, d_g2_1), _ = _matmul_bwd_x(d_up, w_up_1, x1_1, ffn_norm_g[1][None], dx, tm=tm_mm, name="bwd_up_x_1")
    (dwu,), _ = _matmul_bwd_w(h2_1, d_up, N_CHIP, tk=tk_w, name="bwd_up_w_1")
    d_proj, dwo, d_ca1, d_lg1, d_lb1, d_ws1, d_bs1 = _mixer_bwd(
        dx1, proj1, cz1, *mixer_args(1, w_out_1), seq=seq, tm=tm_ew, name="bwd_mixer_1")
    (dx, d_g1_1), _ = _matmul_bwd_x(d_proj, w_in_1, x2_0, mix_norm_g[1][None], dx1, tm=tm_mm, name="bwd_in_x_1")
    (dwi,), _ = _matmul_bwd_w(h1_1, d_proj, N_CHIP, tk=tk_w, name="bwd_in_w_1")
    grads1 = [dwi, chipwise(dwo), dwu, chipwise(dwd)]
    small1 = [d_g1_1, d_ca1, d_lg1, d_lb1, d_ws1, d_bs1.T, d_g2_1, d_cf1]

    d_up, dwd, d_cf0 = _ffn_bwd(dx, up0, conv0, conv_f_full[0], w_down_0.reshape(ff, d), seq=seq, tm=tm_ew,
                                cwid=ff_chunk, name="bwd_ffn_0")
    (dx1, d_g2_0), (recv1,) = _matmul_bwd_x(d_up, w_up_0, x1_0, ffn_norm_g[0][None], dx, tm=tm_mm, name="bwd_up_x_0",
                                            jobs=[_SwapJob(grads1)])
    parts1 = pair_sums(grads1, recv1, 1)
    (dwu,), ((arr1_in,),) = _matmul_bwd_w(h2_0, d_up, N_CHIP, tk=tk_w, name="bwd_up_w_0",
                                          jobs=[_ScatterJob([parts1[IN]])])
    d_proj, dwo, d_ca0, d_lg0, d_lb0, d_ws0, d_bs0 = _mixer_bwd(
        dx1, proj0, cz0, *mixer_args(0, w_out_0), seq=seq, tm=tm_ew, name="bwd_mixer_0")
    early0 = [chipwise(dwo), dwu, chipwise(dwd)]
    (dwi,), (arr1_rest, recv0_early) = _matmul_bwd_w(
        h1_0, d_proj, N_CHIP, tk=tk_w, name="bwd_in_w_0",
        jobs=[_ScatterJob(parts1[OUT:]), _SwapJob(early0)])
    arrived1 = [arr1_in] + list(arr1_rest)
    gsum = [_final_sum(g, r, p, where, None, 1, n_l, f"grad_final_sum_1_{a}")
            for a, (g, r, p) in enumerate(zip(grads1, recv1, arrived1))]
    parts0_early = pair_sums(early0, recv0_early, "0e")
    (dx, d_g1_0), (gsum, arr0_early, (recv0_in,)) = _matmul_bwd_x(
        d_proj, w_in_0, xs, mix_norm_g[0][None], dx1, tm=tm_mm, name="bwd_in_x_0",
        jobs=[_ShareJob([(g, 1) for g in gsum]), _ScatterJob(parts0_early), _SwapJob([dwi])])
    grad_x = dx.reshape(bsz, seq, d)
    grads0 = [dwi] + early0
    recv0 = [recv0_in] + list(recv0_early)
    part0_in, = pair_sums([dwi], [recv0_in], "0i")
    small0 = [d_g1_0, d_ca0, d_lg0, d_lb0, d_ws0, d_bs0.T, d_g2_0, d_cf0]

    flat = [a.reshape(-1) for a in small0 + small1] + [d_final_g.reshape(-1)]
    sizes = [a.size for a in flat]
    total = sum(sizes)
    rows = -(-total // 128 // (2 * HALO)) * 2 * HALO
    pack = jnp.concatenate(flat + [jnp.zeros((rows * 128 - total,), F32)]).reshape(rows, 128)
    red, ((arr0_in,),) = _all_reduce_small(pack, "small_all_reduce", jobs=[_ScatterJob([part0_in])])
    red = red.reshape(-1)
    arrived0 = [arr0_in] + list(arr0_early)
    gsum = [_final_sum(g, r, p, where, prev, 0, n_l, f"grad_final_sum_0_{a}")
            for a, (g, r, p, prev) in enumerate(zip(grads0, recv0, arrived0, gsum))]
    (g_in, g_out, g_up, g_down), = _comm_only([_ShareJob([(g, 0) for g in gsum])], "grad_share_0")

    pieces, off = [], 0
    for s in sizes:
        pieces.append(red[off:off + s])
        off += s
    per_layer = [pieces[l * 8:(l + 1) * 8] for l in range(n_l)]

    def stacked(idx, shape):
        return jnp.stack([per_layer[l][idx].reshape(shape) for l in range(n_l)])

    def my_cols(a, width):
        return lax.dynamic_slice_in_dim(a, chip * width, width, axis=-1)

    g_mix_norm = stacked(0, (d,))
    g_conv_a = my_cols(stacked(1, (3, d)), d // N_CHIP)
    g_ln_g = stacked(2, (d,))
    g_ln_b = stacked(3, (d,))
    g_ws = stacked(4, (groups, CHUNK, CHUNK))
    g_bs = stacked(5, (groups, CHUNK))
    g_ffn_norm = stacked(6, (d,))
    g_conv_f = my_cols(stacked(7, (3, 2 * ff)), 2 * ff // N_CHIP)
    g_final = pieces[-1].reshape(1, d)

    big_names = [(w_in, g_in, m_w_in, v_w_in), (w_out, g_out, m_w_out, v_w_out), (w_up, g_up, m_w_up, v_w_up),
                 (w_down, g_down, m_w_down, v_w_down)]
    upd = [_adamw_big(w, g, m, v, f"adamw_{a}") for a, (w, g, m, v) in enumerate(big_names)]

    sm_w = [mix_norm_g, conv_a_w, ln_v_g, ln_v_b, w_s, b_s, ffn_norm_g, conv_ffn_w, final_norm_g[None]]
    sm_g = [g_mix_norm, g_conv_a, g_ln_g, g_ln_b, g_ws, g_bs, g_ffn_norm, g_conv_f, g_final]
    sm_m = [m_mix_norm_g, m_conv_a_w, m_ln_v_g, m_ln_v_b, m_w_s, m_b_s, m_ffn_norm_g, m_conv_ffn_w, m_final_norm_g[None]]
    sm_v = [v_mix_norm_g, v_conv_a_w, v_ln_v_g, v_ln_v_b, v_w_s, v_b_s, v_ffn_norm_g, v_conv_ffn_w, v_final_norm_g[None]]
    sm_d, sm_nm, sm_nv = _adamw_small(sm_w, sm_g, sm_m, sm_v, "adamw_small")

    def ordered(sm, bigs):
        return [sm[0], bigs[0], sm[1], sm[2], sm[3], sm[4], sm[5], bigs[1], sm[6], bigs[2], sm[7], bigs[3],
                sm[8].reshape(d)]

    out_g = ordered(sm_g, [g_in, g_out, g_up, g_down])
    out_d = ordered(sm_d, [u[0] for u in upd])
    out_m = ordered(sm_nm, [u[1] for u in upd])
    out_v = ordered(sm_nv, [u[2] for u in upd])
    return (loss, grad_x, *out_g, *out_d, *out_m, *out_v)
```

```python
import jax
import jax.numpy as jnp
from jax import lax
from jax.experimental import pallas as pl
from jax.experimental.pallas import tpu as pltpu

F32 = jnp.float32
BF16 = jnp.bfloat16
EPS = 1e-6
CHUNK = 128
N_CHIP = 4
HALO = 8
N_PIECES = 7
LATE_STEPS = 2
VMEM_LIMIT_V7X = 56 * 1024 * 1024
MESH_T = pl.DeviceIdType.MESH
HBM_SPEC = pl.BlockSpec(memory_space=pltpu.HBM)
VMEM_SPEC = pl.BlockSpec(memory_space=pltpu.VMEM)

ADAM_LR, ADAM_B1, ADAM_B2, ADAM_EPS, ADAM_WD, ADAM_STEP = 0.001, 0.9, 0.999, 1e-08, 0.01, 10

NT_DIMS = (((1,), (1,)), ((), ()))
TN_DIMS = (((0,), (0,)), ((), ()))


def _resident(block_shape, index_map):
    return pl.BlockSpec(block_shape, index_map, pipeline_mode=pl.Buffered(1))


def _full(shape):
    return pl.BlockSpec(shape, lambda *_: (0,) * len(shape))


def _row_tile(seq, want):
    t = min(seq, want)
    assert seq % t == 0 and t % CHUNK == 0, (seq, want)
    return t


def _shift_prev(prev8, cur, k):
    ext = jnp.concatenate([prev8, cur], axis=0)
    return pltpu.roll(ext, k, 0)[HALO:]


def _shift_next(cur, next8, k):
    ext = jnp.concatenate([cur, next8], axis=0)
    n = ext.shape[0]
    return pltpu.roll(ext, n - k, 0)[:n - HALO]


def _rowsum(a):
    return jnp.sum(a, axis=0, keepdims=True)


def _lanemean(a):
    return jnp.mean(a, axis=-1, keepdims=True)


def _tril_weights(ws_ref, groups):
    r = lax.broadcasted_iota(jnp.int32, (CHUNK, CHUNK), 0)
    c = lax.broadcasted_iota(jnp.int32, (CHUNK, CHUNK), 1)
    tril = r >= c
    return tril, [jnp.where(tril, ws_ref[g], 0.0) for g in range(groups)]


def _halo_prev(tm):
    return lambda i: jnp.maximum(i * (tm // HALO) - 1, 0)


def _halo_next(tm, n):
    last = n // HALO - 1
    return lambda i: jnp.minimum((i + 1) * (tm // HALO), last)


def _mesh_pos():
    return lax.axis_index("x"), lax.axis_index("y"), lax.axis_index("c")


def _peer_chip(x, y, r):
    return (1 - x if r >> 1 else x), (1 - y if r & 1 else y)


def _remote(src, dst, sems, k, to):
    return pltpu.make_async_remote_copy(src_ref=src, dst_ref=dst, send_sem=sems[0].at[k], recv_sem=sems[1].at[k],
                                        device_id=to, device_id_type=MESH_T)


class _GatherJob:
    def __init__(self, pieces):
        self.pieces = pieces
        self.ins = [p[0] for p in pieces]
        self.out_shapes = [jax.ShapeDtypeStruct((N_CHIP,) + p[0].shape[1:], p[0].dtype) for p in pieces]
        self.aliases = {}
        n = len(pieces)
        self.sems = [pltpu.SemaphoreType.DMA((3 * n,))] * 4 + [pltpu.SemaphoreType.DMA((n,))]

    def _half(self, outs, a, of_chip, core):
        rh = outs[a].shape[1] // 2
        return outs[a].at[of_chip, pl.ds(core * rh, rh), :]

    def _own(self, ins, outs, sems, chip):
        return [pltpu.make_async_copy(ins[a].at[layer], outs[a].at[chip], sems[4].at[a])
                for a, (_, layer) in enumerate(self.pieces)]

    def _sends(self, ins, outs, sems, x, y, c):
        chip, out = 2 * x + y, []
        for a, (_, layer) in enumerate(self.pieces):
            rh = outs[a].shape[1] // 2
            for r in (1, 2, 3):
                px, py = _peer_chip(x, y, r)
                out.append(_remote(ins[a].at[layer, pl.ds(c * rh, rh), :], self._half(outs, a, chip, c), sems[0:2],
                                   3 * a + r - 1, (px, py, c)))
        return out

    def _passes(self, outs, sems, x, y, c, core):
        out = []
        for a in range(len(self.pieces)):
            for r in (1, 2, 3):
                px, py = _peer_chip(x, y, r)
                landed = self._half(outs, a, 2 * px + py, core)
                out.append(_remote(landed, landed, sems[2:4], 3 * a + r - 1, (x, y, 1 - c)))
        return out

    def start(self, ins, outs, sems):
        x, y, c = _mesh_pos()
        for cp in self._own(ins, outs, sems, 2 * x + y) + self._sends(ins, outs, sems, x, y, c):
            cp.start()

    def late(self, ins, outs, sems):
        x, y, c = _mesh_pos()
        passes = self._passes(outs, sems, x, y, c, c)
        k = 0
        for a in range(len(self.pieces)):
            for r in (1, 2, 3):
                px, py = _peer_chip(x, y, r)
                landed = self._half(outs, a, 2 * px + py, c)
                _remote(landed, landed, sems[0:2], k, (px, py, c)).wait_recv()
                passes[k].start()
                k += 1

    def finish(self, ins, outs, sems):
        x, y, c = _mesh_pos()
        for cp in self._passes(outs, sems, x, y, c, 1 - c):
            cp.wait_recv()
        for cp in self._sends(ins, outs, sems, x, y, c) + self._passes(outs, sems, x, y, c, c):
            cp.wait_send()
        for cp in self._own(ins, outs, sems, 2 * x + y):
            cp.wait()


class _SwapJob:
    def __init__(self, pieces):
        self.ins = list(pieces)
        self.out_shapes = [jax.ShapeDtypeStruct((g.shape[0], g.shape[1] // 2, g.shape[2]), g.dtype) for g in pieces]
        self.aliases = {}
        self.sems = [pltpu.SemaphoreType.DMA((len(pieces),))] * 2

    def _copies(self, ins, outs, sems):
        x, y, c = _mesh_pos()
        out = []
        for a in range(len(ins)):
            rh = ins[a].shape[1] // 2
            out.append(_remote(ins[a].at[:, pl.ds((1 - c) * rh, rh), :], outs[a], sems, a, (x, y, 1 - c)))
        return out

    def start(self, ins, outs, sems):
        for cp in self._copies(ins, outs, sems):
            cp.start()

    def finish(self, ins, outs, sems):
        for cp in self._copies(ins, outs, sems):
            cp.wait()


class _ScatterJob:
    def __init__(self, pieces):
        self.ins = list(pieces)
        self.out_shapes = [jax.ShapeDtypeStruct((3,) + p.shape[1:], p.dtype) for p in pieces]
        self.aliases = {}
        self.sems = [pltpu.SemaphoreType.DMA((3 * len(pieces),))] * 2

    def _copies(self, ins, outs, sems):
        x, y, c = _mesh_pos()
        out = []
        for a in range(len(ins)):
            for r in (1, 2, 3):
                px, py = _peer_chip(x, y, r)
                out.append(_remote(ins[a].at[2 * px + py], outs[a].at[r - 1], sems, 3 * a + r - 1, (px, py, c)))
        return out

    def start(self, ins, outs, sems):
        for cp in self._copies(ins, outs, sems):
            cp.start()

    def finish(self, ins, outs, sems):
        for cp in self._copies(ins, outs, sems):
            cp.wait()


class _ShareJob:
    def __init__(self, pieces):
        self.pieces = pieces
        self.ins = [p[0] for p in pieces]
        self.out_shapes = [jax.ShapeDtypeStruct(p[0].shape, p[0].dtype) for p in pieces]
        self.aliases = {a: a for a in range(len(pieces))}
        self.sems = [pltpu.SemaphoreType.DMA((len(pieces),))] * 2

    def _copies(self, outs, sems, core):
        x, y, c = _mesh_pos()
        out = []
        for a, (_, layer) in enumerate(self.pieces):
            rh = outs[a].shape[1] // 2
            rows = outs[a].at[layer, pl.ds(core * rh, rh), :]
            out.append(_remote(rows, rows, sems, a, (x, y, 1 - c)))
        return out

    def start(self, ins, outs, sems):
        for cp in self._copies(outs, sems, lax.axis_index("c")):
            cp.start()

    def finish(self, ins, outs, sems):
        c = lax.axis_index("c")
        for cp in self._copies(outs, sems, c):
            cp.wait_send()
        for cp in self._copies(outs, sems, 1 - c):
            cp.wait_recv()


def _hosted_call(main, *, name, grid, in_specs, out_specs, out_shape, scratch_shapes=(), jobs=(), semantics=None,
                 operands=()):
    n_in, n_out, n_sc = len(in_specs), len(out_specs), len(scratch_shapes)
    counts = [(len(j.ins), len(j.out_shapes), len(j.sems)) for j in jobs]
    j_in, j_out, j_sc = (sum(c[k] for c in counts) for k in range(3))
    aliases, i0, o0 = {}, n_in, n_out
    for j, (ci, co, _) in zip(jobs, counts):
        aliases.update({i0 + a: o0 + b for a, b in j.aliases.items()})
        i0, o0 = i0 + ci, o0 + co

    def body(*refs):
        cuts = [0, n_in, n_in + j_in, n_in + j_in + n_out, n_in + j_in + n_out + j_out,
                n_in + j_in + n_out + j_out + n_sc, len(refs)]
        m_in, jb_in, m_out, jb_out, m_sc, jb_sc = (list(refs[cuts[k]:cuts[k + 1]]) for k in range(6))

        def run(phase):
            i0 = o0 = s0 = 0
            for j, (ci, co, cs) in zip(jobs, counts):
                if hasattr(j, phase):
                    getattr(j, phase)(jb_in[i0:i0 + ci], jb_out[o0:o0 + co], jb_sc[s0:s0 + cs])
                i0, o0, s0 = i0 + ci, o0 + co, s0 + cs

        step, n_steps = 0, 1
        for ax in range(len(grid)):
            step, n_steps = step * grid[ax] + pl.program_id(ax), n_steps * grid[ax]
        if jobs:
            pl.when(step == 0)(lambda: run("start"))
        main(m_in, m_out, m_sc)
        if jobs:
            pl.when(step == max(n_steps - 1 - LATE_STEPS, 0))(lambda: run("late"))
            pl.when(step == n_steps - 1)(lambda: run("finish"))

    if semantics is None or jobs:
        semantics = ("arbitrary",) * len(grid)
    outs = pl.pallas_call(
        body, name=name, grid=grid,
        in_specs=list(in_specs) + [HBM_SPEC] * j_in,
        out_specs=list(out_specs) + [HBM_SPEC] * j_out,
        out_shape=list(out_shape) + [s for j in jobs for s in j.out_shapes],
        scratch_shapes=list(scratch_shapes) + [s for j in jobs for s in j.sems],
        input_output_aliases=aliases,
        compiler_params=pltpu.CompilerParams(dimension_semantics=semantics, vmem_limit_bytes=VMEM_LIMIT_V7X),
    )(*operands, *[a for j in jobs for a in j.ins])
    main_outs, rest, job_outs = list(outs[:n_out]), list(outs[n_out:]), []
    for _, co, _ in counts:
        job_outs.append(rest[:co])
        rest = rest[co:]
    return main_outs, job_outs


def _comm_only(jobs, name):
    return _hosted_call(lambda i, o, s: None, name=name, grid=(1,), in_specs=[], out_specs=[], out_shape=[],
                        jobs=jobs)[1]


def _norm_matmul(x, g, wg, *, tm, name, jobs=()):
    n, d = x.shape
    n_chip, _, ck = wg.shape

    def main(ins, outs, _):
        x_ref, g_ref, w_ref = ins
        o_ref, ht_ref = outs
        xf = x_ref[...]
        r = lax.rsqrt(_lanemean(xf * xf) + EPS)
        h = xf * r * g_ref[...]
        ht_ref[...] = h.T.astype(BF16)
        hb = h.astype(BF16)
        for j in range(n_chip):
            o_ref[:, j * ck:(j + 1) * ck] = jnp.dot(hb, w_ref[j], preferred_element_type=F32).astype(BF16)

    return _hosted_call(
        main, name=name, grid=(n // tm,), jobs=jobs, semantics=("parallel",), operands=(x, g, wg),
        in_specs=[pl.BlockSpec((tm, d), lambda i: (i, 0)), _full((1, d)), _resident((n_chip, d, ck), lambda i: (0, 0, 0))],
        out_specs=[pl.BlockSpec((tm, n_chip * ck), lambda i: (i, 0)), pl.BlockSpec((d, tm), lambda i: (0, i))],
        out_shape=[jax.ShapeDtypeStruct((n, n_chip * ck), BF16), jax.ShapeDtypeStruct((d, n), BF16)])


def _mixer_fwd(proj, x, cw, lg, lb, ws, bmap, wout, *, seq, tm, name, jobs=()):
    n, d = x.shape
    groups = ws.shape[0]
    gd = d // groups
    prev = _halo_prev(tm)

    def main(ins, outs, scratch):
        proj_ref, pcg_ref, pxi_ref, x_ref, cw_ref, lg_ref, lb_ref, ws_ref, bmap_ref, wout_ref = ins
        o_ref, cz_ref = outs
        vn_s, mixed_s = scratch
        seq_start = (pl.program_id(0) * tm) % seq == 0

        def piece(k):
            return proj_ref[:, k * d:(k + 1) * d].astype(F32)

        z = piece(1) * piece(2)
        zprev = jnp.where(seq_start, 0.0, pcg_ref[...].astype(F32) * pxi_ref[...].astype(F32))
        cz = (cw_ref[0:1, :] * _shift_prev(zprev, z, 2) + cw_ref[1:2, :] * _shift_prev(zprev, z, 1)
              + cw_ref[2:3, :] * z)
        cz_ref[...] = cz.astype(BF16)
        ya = piece(0) * cz

        v = piece(4)
        xc = v - _lanemean(v)
        vn = xc * lax.rsqrt(_lanemean(xc * xc) + EPS) * lg_ref[...] + lb_ref[...]
        vn_s[...] = vn.astype(BF16)
        _, wsm = _tril_weights(ws_ref, groups)
        for ck in range(tm // CHUNK):
            rows = slice(ck * CHUNK, (ck + 1) * CHUNK)
            for g in range(groups):
                cols = slice(g * gd, (g + 1) * gd)
                mixed_s[rows, cols] = (jnp.dot(wsm[g].astype(BF16), vn_s[rows, cols], preferred_element_type=F32)
                                       + bmap_ref[:, cols])
        yb = piece(3) * mixed_s[...]
        merged = jax.nn.sigmoid(piece(5)) * ya + jax.nn.sigmoid(piece(6)) * yb
        o_ref[...] = x_ref[...] + jnp.dot(merged.astype(BF16), wout_ref[...], preferred_element_type=F32)

    return _hosted_call(
        main, name=name, grid=(n // tm,), jobs=jobs, semantics=("parallel",),
        operands=(proj, proj, proj, x, cw, lg, lb, ws, bmap, wout),
        in_specs=[pl.BlockSpec((tm, N_PIECES * d), lambda i: (i, 0)),
                  pl.BlockSpec((HALO, d), lambda i: (prev(i), 1)),
                  pl.BlockSpec((HALO, d), lambda i: (prev(i), 2)),
                  pl.BlockSpec((tm, d), lambda i: (i, 0)),
                  _full((3, d)), _full((1, d)), _full((1, d)), _full((groups, CHUNK, CHUNK)), _full((CHUNK, d)),
                  _resident((d, d), lambda i: (0, 0))],
        out_specs=[pl.BlockSpec((tm, d), lambda i: (i, 0)), pl.BlockSpec((tm, d), lambda i: (i, 0))],
        out_shape=[jax.ShapeDtypeStruct((n, d), F32), jax.ShapeDtypeStruct((n, d), BF16)],
        scratch_shapes=[pltpu.VMEM((tm, d), BF16), pltpu.VMEM((tm, d), F32)])


def _ffn_fwd(up, x, cw, wd, *, seq, tm, cwid, name, jobs=()):
    n, d = x.shape
    ff = wd.shape[0]
    prev = _halo_prev(tm)

    def main(ins, outs, _):
        up_ref, pup_ref, x_ref, cw_ref, wd_ref = ins
        o_ref, conv_ref = outs
        seq_start = (pl.program_id(0) * tm) % seq == 0

        def conv(lo):
            cols = slice(lo, lo + cwid)
            cur = up_ref[:, cols].astype(F32)
            pre = jnp.where(seq_start, 0.0, pup_ref[:, cols].astype(F32))
            out = (cw_ref[0:1, cols] * _shift_prev(pre, cur, 2) + cw_ref[1:2, cols] * _shift_prev(pre, cur, 1)
                   + cw_ref[2:3, cols] * cur)
            conv_ref[:, cols] = out.astype(BF16)
            return out

        acc = x_ref[...]
        for cj in range(ff // cwid):
            gate = conv(cj * cwid)
            val = conv(ff + cj * cwid)
            a = gate * jax.nn.sigmoid(gate) * val
            acc = acc + jnp.dot(a.astype(BF16), wd_ref[cj * cwid:(cj + 1) * cwid, :], preferred_element_type=F32)
        o_ref[...] = acc

    return _hosted_call(
        main, name=name, grid=(n // tm,), jobs=jobs, semantics=("parallel",), operands=(up, up, x, cw, wd),
        in_specs=[pl.BlockSpec((tm, 2 * ff), lambda i: (i, 0)),
                  pl.BlockSpec((HALO, 2 * ff), lambda i: (prev(i), 0)),
                  pl.BlockSpec((tm, d), lambda i: (i, 0)),
                  _full((3, 2 * ff)),
                  _resident((ff, d), lambda i: (0, 0))],
        out_specs=[pl.BlockSpec((tm, d), lambda i: (i, 0)), pl.BlockSpec((tm, 2 * ff), lambda i: (i, 0))],
        out_shape=[jax.ShapeDtypeStruct((n, d), F32), jax.ShapeDtypeStruct((n, 2 * ff), BF16)])


def _final_loss(x, g, target, *, tm, name):
    n, d = x.shape

    def body(x_ref, g_ref, t_ref, dx_ref, loss_ref, dg_ref):
        @pl.when(pl.program_id(0) == 0)
        def _():
            loss_ref[...] = jnp.zeros_like(loss_ref)
            dg_ref[...] = jnp.zeros_like(dg_ref)

        xf = x_ref[...]
        r = lax.rsqrt(_lanemean(xf * xf) + EPS)
        xhat = xf * r
        diff = xhat * g_ref[...] - t_ref[...]
        loss_ref[...] += (0.5 / d) * _rowsum(jnp.sum(diff * diff, axis=-1, keepdims=True))
        dy = diff * (1.0 / d)
        dg_ref[...] += _rowsum(dy * xhat)
        dyh = dy * g_ref[...]
        dx_ref[...] = r * (dyh - xhat * _lanemean(dyh * xhat))

    return pl.pallas_call(
        body, name=name, grid=(n // tm,),
        in_specs=[pl.BlockSpec((tm, d), lambda i: (i, 0)), _full((1, d)), pl.BlockSpec((tm, d), lambda i: (i, 0))],
        out_specs=[pl.BlockSpec((tm, d), lambda i: (i, 0)), _full((HALO, CHUNK)), _full((1, d))],
        out_shape=[jax.ShapeDtypeStruct((n, d), F32), jax.ShapeDtypeStruct((HALO, CHUNK), F32),
                   jax.ShapeDtypeStruct((1, d), F32)],
        compiler_params=pltpu.CompilerParams(dimension_semantics=("arbitrary",), vmem_limit_bytes=VMEM_LIMIT_V7X),
    )(x, g, target)


def _ffn_bwd(dx, up, conv, cw, wd, *, seq, tm, cwid, name):
    n, d = dx.shape
    ff = wd.shape[0]
    nxt = _halo_next(tm, n)

    def body(dx_ref, dxn_ref, up_ref, conv_ref, nconv_ref, cw_ref, wd_ref, dup_ref, dwd_ref, dcw_ref):
        i = pl.program_id(0)
        keep_next = jnp.where(((i + 1) * tm) % seq == 0, 0.0, 1.0)

        @pl.when(i == 0)
        def _():
            dwd_ref[...] = jnp.zeros_like(dwd_ref)
            dcw_ref[...] = jnp.zeros_like(dcw_ref)

        dxe = jnp.concatenate([dx_ref[...], dxn_ref[...]], axis=0).astype(BF16)
        dxb = dxe[:tm]
        for cj in range(ff // cwid):
            rows = slice(cj * cwid, (cj + 1) * cwid)
            g_cols, v_cols = slice(cj * cwid, (cj + 1) * cwid), slice(ff + cj * cwid, ff + (cj + 1) * cwid)
            dae = lax.dot_general(dxe, wd_ref[rows, :], NT_DIMS, preferred_element_type=F32)
            da, dan = dae[:tm], dae[tm:]

            def grads(gate, val, da_rows):
                sg = jax.nn.sigmoid(gate)
                sl = gate * sg
                return sl, da_rows * val * sg * (1.0 + gate * (1.0 - sg)), da_rows * sl

            gate, val = conv_ref[:, g_cols].astype(F32), conv_ref[:, v_cols].astype(F32)
            sl, d_gate, d_val = grads(gate, val, da)
            dwd_ref[rows, :] += lax.dot_general((sl * val).astype(BF16), dxb, TN_DIMS, preferred_element_type=F32)
            _, d_gate_n, d_val_n = grads(nconv_ref[:, g_cols].astype(F32), nconv_ref[:, v_cols].astype(F32),
                                         dan * keep_next)
            for cols, dcur, dnext in ((g_cols, d_gate, d_gate_n), (v_cols, d_val, d_val_n)):
                upc = up_ref[:, cols].astype(F32)
                d1, d2 = _shift_next(dcur, dnext, 1), _shift_next(dcur, dnext, 2)
                for k, dk in enumerate((d2, d1, dcur)):
                    dcw_ref[k:k + 1, cols] += _rowsum(dk * upc)
                dup = cw_ref[2:3, cols] * dcur + cw_ref[1:2, cols] * d1 + cw_ref[0:1, cols] * d2
                dup_ref[:, cols] = dup.astype(BF16)

    return pl.pallas_call(
        body, name=name, grid=(n // tm,),
        in_specs=[pl.BlockSpec((tm, d), lambda i: (i, 0)),
                  pl.BlockSpec((HALO, d), lambda i: (nxt(i), 0)),
                  pl.BlockSpec((tm, 2 * ff), lambda i: (i, 0)),
                  pl.BlockSpec((tm, 2 * ff), lambda i: (i, 0)),
                  pl.BlockSpec((HALO, 2 * ff), lambda i: (nxt(i), 0)),
                  _full((3, 2 * ff)),
                  _resident((ff, d), lambda i: (0, 0))],
        out_specs=[pl.BlockSpec((tm, 2 * ff), lambda i: (i, 0)), _full((ff, d)), _full((3, 2 * ff))],
        out_shape=[jax.ShapeDtypeStruct((n, 2 * ff), BF16), jax.ShapeDtypeStruct((ff, d), F32),
                   jax.ShapeDtypeStruct((3, 2 * ff), F32)],
        compiler_params=pltpu.CompilerParams(dimension_semantics=("arbitrary",), vmem_limit_bytes=VMEM_LIMIT_V7X),
    )(dx, dx, up, conv, conv, cw, wd)


def _matmul_bwd_x(dy, wg, x, g, dres, *, tm, name, jobs=()):
    n, d = x.shape
    n_chip, _, ck = wg.shape

    def main(ins, outs, _):
        dy_ref, w_ref, x_ref, g_ref, dres_ref = ins
        dx_ref, dg_ref = outs

        @pl.when(pl.program_id(0) == 0)
        def _():
            dg_ref[...] = jnp.zeros_like(dg_ref)

        dh = lax.dot_general(dy_ref[:, 0:ck], w_ref[0], NT_DIMS, preferred_element_type=F32)
        for j in range(1, n_chip):
            dh = dh + lax.dot_general(dy_ref[:, j * ck:(j + 1) * ck], w_ref[j], NT_DIMS, preferred_element_type=F32)
        xf = x_ref[...]
        r = lax.rsqrt(_lanemean(xf * xf) + EPS)
        xhat = xf * r
        dg_ref[...] += _rowsum(dh * xhat)
        dyh = dh * g_ref[...]
        dx_ref[...] = dres_ref[...] + r * (dyh - xhat * _lanemean(dyh * xhat))

    return _hosted_call(
        main, name=name, grid=(n // tm,), jobs=jobs, semantics=("arbitrary",), operands=(dy, wg, x, g, dres),
        in_specs=[pl.BlockSpec((tm, n_chip * ck), lambda i: (i, 0)), _resident((n_chip, d, ck), lambda i: (0, 0, 0)),
                  pl.BlockSpec((tm, d), lambda i: (i, 0)), _full((1, d)), pl.BlockSpec((tm, d), lambda i: (i, 0))],
        out_specs=[pl.BlockSpec((tm, d), lambda i: (i, 0)), _full((1, d))],
        out_shape=[jax.ShapeDtypeStruct((n, d), F32), jax.ShapeDtypeStruct((1, d), F32)])


def _matmul_bwd_w(ht, dy, n_chip, *, tk, name, jobs=()):
    d, n = ht.shape
    ck = dy.shape[1] // n_chip

    def main(ins, outs, _):
        ht_ref, dy_ref = ins
        o_ref, = outs

        @pl.when(pl.program_id(1) == 0)
        def _():
            o_ref[...] = jnp.zeros_like(o_ref)

        o_ref[...] += jnp.dot(ht_ref[...], dy_ref[...], preferred_element_type=F32)

    return _hosted_call(
        main, name=name, grid=(n_chip, n // tk), jobs=jobs, semantics=("parallel", "arbitrary"), operands=(ht, dy),
        in_specs=[pl.BlockSpec((d, tk), lambda j, k: (0, k)), pl.BlockSpec((tk, ck), lambda j, k: (k, j))],
        out_specs=[pl.BlockSpec((None, d, ck), lambda j, k: (j, 0, 0))],
        out_shape=[jax.ShapeDtypeStruct((n_chip, d, ck), F32)])


def _mixer_bwd(dx, proj, cz, cw, lg, lb, ws, bmap, wout, *, seq, tm, name):
    n, d = dx.shape
    groups = ws.shape[0]
    gd = d // groups
    nxt = _halo_next(tm, n)
    n_tiles = n // tm

    def body(proj_ref, cz_ref, nbg_ref, nga_ref, dx_ref, dxn_ref, cw_ref, lg_ref, lb_ref, ws_ref,
             bmap_ref, wout_ref, dproj_ref, dwout_ref, dcw_ref, dlg_ref, dlb_ref, dws_ref, dbs_ref,
             vn_s, mixed_s, dmix_s, dvn_s, dbmap_s):
        i = pl.program_id(0)
        keep_next = jnp.where(((i + 1) * tm) % seq == 0, 0.0, 1.0)

        @pl.when(i == 0)
        def _():
            for ref in (dwout_ref, dcw_ref, dlg_ref, dlb_ref, dws_ref, dbmap_s):
                ref[...] = jnp.zeros_like(ref)

        def piece(k):
            return proj_ref[:, k * d:(k + 1) * d].astype(F32)

        def put(k, val):
            dproj_ref[:, k * d:(k + 1) * d] = val.astype(BF16)

        w = [cw_ref[k:k + 1, :] for k in range(3)]
        cg, xi = piece(1), piece(2)
        cz = cz_ref[...].astype(F32)
        bg = piece(0)
        ya = bg * cz

        v = piece(4)
        xc = v - _lanemean(v)
        rstd = lax.rsqrt(_lanemean(xc * xc) + EPS)
        vhat = xc * rstd
        vn_s[...] = (vhat * lg_ref[...] + lb_ref[...]).astype(BF16)
        tril, wsm = _tril_weights(ws_ref, groups)
        for ck in range(tm // CHUNK):
            rows = slice(ck * CHUNK, (ck + 1) * CHUNK)
            for g in range(groups):
                cols = slice(g * gd, (g + 1) * gd)
                mixed_s[rows, cols] = (jnp.dot(wsm[g].astype(BF16), vn_s[rows, cols], preferred_element_type=F32)
                                       + bmap_ref[:, cols])
        u = piece(3)
        mixed = mixed_s[...]
        yb = u * mixed
        sa, sb = jax.nn.sigmoid(piece(5)), jax.nn.sigmoid(piece(6))
        merged = sa * ya + sb * yb

        dxe = jnp.concatenate([dx_ref[...], dxn_ref[...]], axis=0).astype(BF16)
        dme = lax.dot_general(dxe, wout_ref[...], NT_DIMS, preferred_element_type=F32)
        dm, dm_n = dme[:tm], dme[tm:]
        dwout_ref[...] += lax.dot_general(merged.astype(BF16), dxe[:tm], TN_DIMS, preferred_element_type=F32)

        put(5, dm * ya * sa * (1.0 - sa))
        put(6, dm * yb * sb * (1.0 - sb))
        d_ya, d_yb = dm * sa, dm * sb
        put(0, d_ya * cz)
        d_cz = d_ya * bg
        d_cz_n = dm_n * jax.nn.sigmoid(nga_ref[...].astype(F32)) * nbg_ref[...].astype(F32) * keep_next
        d_cz1, d_cz2 = _shift_next(d_cz, d_cz_n, 1), _shift_next(d_cz, d_cz_n, 2)
        z = cg * xi
        for k, dk in enumerate((d_cz2, d_cz1, d_cz)):
            dcw_ref[k:k + 1, :] += _rowsum(dk * z)
        dz = w[2] * d_cz + w[1] * d_cz1 + w[0] * d_cz2
        put(1, dz * xi)
        put(2, dz * cg)

        put(3, d_yb * mixed)
        d_mixed = d_yb * u
        dmix_s[...] = d_mixed.astype(BF16)
        for ck in range(tm // CHUNK):
            rows = slice(ck * CHUNK, (ck + 1) * CHUNK)
            dbmap_s[...] += d_mixed[rows, :]
            for g in range(groups):
                cols = slice(g * gd, (g + 1) * gd)
                dvn_s[rows, cols] = jnp.dot(wsm[g].T.astype(BF16), dmix_s[rows, cols], preferred_element_type=F32)
                dws_ref[g] += jnp.where(
                    tril, lax.dot_general(dmix_s[rows, cols], vn_s[rows, cols], NT_DIMS, preferred_element_type=F32),
                    0.0)
        d_vn = dvn_s[...]
        dlg_ref[...] += _rowsum(d_vn * vhat)
        dlb_ref[...] += _rowsum(d_vn)
        d_vhat = d_vn * lg_ref[...]
        put(4, rstd * (d_vhat - _lanemean(d_vhat) - vhat * _lanemean(d_vhat * vhat)))

        @pl.when(i == n_tiles - 1)
        def _():
            for g in range(groups):
                dbs_ref[:, g:g + 1] = jnp.sum(dbmap_s[:, g * gd:(g + 1) * gd], axis=-1, keepdims=True)

    return pl.pallas_call(
        body, name=name, grid=(n_tiles,),
        in_specs=[pl.BlockSpec((tm, N_PIECES * d), lambda i: (i, 0)),
                  pl.BlockSpec((tm, d), lambda i: (i, 0)),
                  pl.BlockSpec((HALO, d), lambda i: (nxt(i), 0)),
                  pl.BlockSpec((HALO, d), lambda i: (nxt(i), 5)),
                  pl.BlockSpec((tm, d), lambda i: (i, 0)),
                  pl.BlockSpec((HALO, d), lambda i: (nxt(i), 0)),
                  _full((3, d)), _full((1, d)), _full((1, d)), _full((groups, CHUNK, CHUNK)), _full((CHUNK, d)),
                  _resident((d, d), lambda i: (0, 0))],
        out_specs=[pl.BlockSpec((tm, N_PIECES * d), lambda i: (i, 0)), _full((d, d)),
                   _full((3, d)), _full((1, d)), _full((1, d)), _full((groups, CHUNK, CHUNK)), _full((CHUNK, groups))],
        out_shape=[jax.ShapeDtypeStruct((n, N_PIECES * d), BF16), jax.ShapeDtypeStruct((d, d), F32),
                   jax.ShapeDtypeStruct((3, d), F32), jax.ShapeDtypeStruct((1, d), F32),
                   jax.ShapeDtypeStruct((1, d), F32), jax.ShapeDtypeStruct((groups, CHUNK, CHUNK), F32),
                   jax.ShapeDtypeStruct((CHUNK, groups), F32)],
        scratch_shapes=[pltpu.VMEM((tm, d), BF16), pltpu.VMEM((tm, d), F32), pltpu.VMEM((tm, d), BF16),
                        pltpu.VMEM((tm, d), F32), pltpu.VMEM((CHUNK, d), F32)],
        compiler_params=pltpu.CompilerParams(dimension_semantics=("arbitrary",), vmem_limit_bytes=VMEM_LIMIT_V7X),
    )(proj, cz, proj, proj, dx, dx, cw, lg, lb, ws, bmap, wout)


def _gather_small(pack, name):
    def body(p_ref, o_ref, send_sem, recv_sem):
        x, y, c = _mesh_pos()
        chip = 2 * x + y
        o_ref[chip] = p_ref[...]
        copies = []
        for r in (1, 2, 3):
            px, py = _peer_chip(x, y, r)
            copies.append(_remote(p_ref, o_ref.at[chip], (send_sem, recv_sem), r - 1, (px, py, c)))
        for cp in copies:
            cp.start()
        for r, cp in zip((1, 2, 3), copies):
            px, py = _peer_chip(x, y, r)
            landed = o_ref.at[2 * px + py]
            _remote(landed, landed, (send_sem, recv_sem), r - 1, (px, py, c)).wait_recv()
            cp.wait_send()

    return pl.pallas_call(
        body, name=name, in_specs=[VMEM_SPEC], out_specs=VMEM_SPEC,
        out_shape=jax.ShapeDtypeStruct((N_CHIP,) + pack.shape, pack.dtype),
        scratch_shapes=[pltpu.SemaphoreType.DMA((3,)), pltpu.SemaphoreType.DMA((3,))],
    )(pack)


def _all_reduce_small(pack, name, jobs=()):
    rows = pack.shape[0]
    rh = rows // 2
    assert rh % HALO == 0, rows

    def main(ins, outs, scratch):
        p_ref, = ins
        o_ref, = outs
        sib_buf, chip_buf, got_buf, send_sem, recv_sem = scratch
        sems = (send_sem, recv_sem)
        x, y, c = _mesh_pos()
        chip, sibling = 2 * x + y, (x, y, 1 - c)
        mine = pl.ds(pl.multiple_of(c * rh, HALO), rh)
        theirs = pl.ds(pl.multiple_of((1 - c) * rh, HALO), rh)
        swap = _remote(p_ref.at[theirs], sib_buf, sems, 0, sibling)
        swap.start()
        swap.wait()
        chip_buf[chip] = p_ref[mine, :] + sib_buf[...]
        copies = []
        for r in (1, 2, 3):
            px, py = _peer_chip(x, y, r)
            copies.append(_remote(chip_buf.at[chip], chip_buf.at[chip], sems, r, (px, py, c)))
        for cp in copies:
            cp.start()
        for r, cp in zip((1, 2, 3), copies):
            px, py = _peer_chip(x, y, r)
            landed = chip_buf.at[2 * px + py]
            _remote(landed, landed, sems, r, (px, py, c)).wait_recv()
            cp.wait_send()
        total = ((chip_buf[0] + chip_buf[1]) + chip_buf[2]) + chip_buf[3]
        o_ref[mine, :] = total
        chip_buf[chip] = total
        share = _remote(chip_buf.at[chip], got_buf, sems, 4, sibling)
        share.start()
        share.wait()
        o_ref[theirs, :] = got_buf[...]

    outs, job_outs = _hosted_call(
        main, name=name, grid=(1,), jobs=jobs, operands=(pack,),
        in_specs=[_full(pack.shape)], out_specs=[_full(pack.shape)],
        out_shape=[jax.ShapeDtypeStruct(pack.shape, pack.dtype)],
        scratch_shapes=[pltpu.VMEM((rh, 128), F32), pltpu.VMEM((N_CHIP, rh, 128), F32), pltpu.VMEM((rh, 128), F32),
                        pltpu.SemaphoreType.DMA((5,)), pltpu.SemaphoreType.DMA((5,))])
    return outs[0], job_outs


def _block_rows(rows, cols, itemsize=4, budget=2 << 20):
    best = None
    for t in range(16, rows + 1, 16):
        if rows % t == 0 and t * cols * itemsize <= budget:
            best = t
    assert best is not None, (rows, cols)
    return best


def _pair_sum(grad, recv, core, name):
    n_chip, rk, ck = grad.shape
    rh = rk // 2
    tr = _block_rows(rh, ck)
    nb = rh // tr

    def body(core_ref, g_ref, r_ref, o_ref):
        o_ref[...] = (g_ref[...] + r_ref[...]).astype(BF16)

    blk = (None, tr, ck)
    return pl.pallas_call(
        body, name=name,
        grid_spec=pltpu.PrefetchScalarGridSpec(
            num_scalar_prefetch=1, grid=(n_chip, nb),
            in_specs=[pl.BlockSpec(blk, lambda k, i, core_ref: (k, core_ref[0] * nb + i, 0)),
                      pl.BlockSpec(blk, lambda k, i, core_ref: (k, i, 0))],
            out_specs=pl.BlockSpec(blk, lambda k, i, core_ref: (k, i, 0))),
        out_shape=jax.ShapeDtypeStruct((n_chip, rh, ck), BF16),
        compiler_params=pltpu.CompilerParams(dimension_semantics=("parallel", "parallel"),
                                             vmem_limit_bytes=VMEM_LIMIT_V7X),
    )(core, grad, recv)


def _final_sum(grad, recv, arrived, where, prev, layer, n_layers, name):
    n_chip, rk, ck = grad.shape
    rh = rk // 2
    tr = _block_rows(rh, ck)
    nb = rh // tr

    def body(where_ref, g_ref, r_ref, a1_ref, a2_ref, a3_ref, *rest):
        o_ref = rest[-1]
        own = g_ref[...] + r_ref[...]
        o_ref[...] = ((own + a1_ref[...].astype(F32)) + a2_ref[...].astype(F32)) + a3_ref[...].astype(F32)

    blk = (None, tr, ck)
    slot = lambda r: pl.BlockSpec(blk, lambda i, w: (r, i, 0))
    extra = [] if prev is None else [prev]
    return pl.pallas_call(
        body, name=name,
        grid_spec=pltpu.PrefetchScalarGridSpec(
            num_scalar_prefetch=1, grid=(nb,),
            in_specs=[pl.BlockSpec(blk, lambda i, w: (w[1], w[0] * nb + i, 0)),
                      pl.BlockSpec(blk, lambda i, w: (w[1], i, 0)),
                      slot(0), slot(1), slot(2)] + [pl.BlockSpec(memory_space=pl.ANY)] * len(extra),
            out_specs=pl.BlockSpec(blk, lambda i, w: (layer, w[0] * nb + i, 0))),
        out_shape=jax.ShapeDtypeStruct((n_layers, rk, ck), F32),
        input_output_aliases={6: 0} if extra else {},
        compiler_params=pltpu.CompilerParams(dimension_semantics=("parallel",), vmem_limit_bytes=VMEM_LIMIT_V7X),
    )(where, grad, recv, arrived, arrived, arrived, *extra)


def _adamw_math(w, g, m, v):
    m = ADAM_B1 * m + (1.0 - ADAM_B1) * g
    v = ADAM_B2 * v + (1.0 - ADAM_B2) * (g * g)
    m_hat = m / (1.0 - ADAM_B1 ** ADAM_STEP)
    v_hat = v / (1.0 - ADAM_B2 ** ADAM_STEP)
    delta = -ADAM_LR * (m_hat / (jnp.sqrt(v_hat) + ADAM_EPS) + ADAM_WD * w)
    return delta, m, v


def _adamw_big(w, g, m, v, name):
    n_l, rk, ck = w.shape
    tr = _block_rows(rk, ck, budget=1 << 20)

    def body(w_ref, g_ref, m_ref, v_ref, d_ref, nm_ref, nv_ref):
        d_ref[...], nm_ref[...], nv_ref[...] = _adamw_math(w_ref[...], g_ref[...], m_ref[...], v_ref[...])

    spec = pl.BlockSpec((None, tr, ck), lambda l, i: (l, i, 0))
    return pl.pallas_call(
        body, name=name, grid=(n_l, rk // tr), in_specs=[spec] * 4, out_specs=[spec] * 3,
        out_shape=[jax.ShapeDtypeStruct(w.shape, F32)] * 3,
        compiler_params=pltpu.CompilerParams(dimension_semantics=("parallel", "parallel"),
                                             vmem_limit_bytes=VMEM_LIMIT_V7X),
    )(w, g, m, v)


def _adamw_small(ws, gs, ms, vs, name):
    n_p = len(ws)

    def body(*refs):
        ins, outs = refs[:4 * n_p], refs[4 * n_p:]
        for p in range(n_p):
            res = _adamw_math(ins[p][...], ins[n_p + p][...], ins[2 * n_p + p][...], ins[3 * n_p + p][...])
            for q in range(3):
                outs[q * n_p + p][...] = res[q]

    outs = pl.pallas_call(
        body, name=name, in_specs=[VMEM_SPEC] * (4 * n_p), out_specs=[VMEM_SPEC] * (3 * n_p),
        out_shape=[jax.ShapeDtypeStruct(w.shape, F32) for w in ws] * 3,
        compiler_params=pltpu.CompilerParams(vmem_limit_bytes=VMEM_LIMIT_V7X),
    )(*ws, *gs, *ms, *vs)
    return outs[:n_p], outs[n_p:2 * n_p], outs[2 * n_p:]


def kernel(x, mix_norm_g, w_in, conv_a_w, ln_v_g, ln_v_b, w_s, b_s, w_out, ffn_norm_g, w_up, conv_ffn_w, w_down, final_norm_g, loss_target, m_mix_norm_g, m_w_in, m_conv_a_w, m_ln_v_g, m_ln_v_b, m_w_s, m_b_s, m_w_out, m_ffn_norm_g, m_w_up, m_conv_ffn_w, m_w_down, m_final_norm_g, v_mix_norm_g, v_w_in, v_conv_a_w, v_ln_v_g, v_ln_v_b, v_w_s, v_b_s, v_w_out, v_ffn_norm_g, v_w_up, v_conv_ffn_w, v_w_down, v_final_norm_g):
    bsz, seq, d = x.shape
    n = bsz * seq
    n_l, groups = w_s.shape[0], w_s.shape[1]
    assert n_l == 2, "the exchange schedule below is written for two layers"
    gd = d // groups
    ff = w_down.shape[1] * N_CHIP
    axes = ("x", "y", "c")
    mx, my, mc = _mesh_pos()
    chip = 2 * mx + my
    core = jnp.reshape(mc, (1,)).astype(jnp.int32)
    where = jnp.stack([mc, chip]).astype(jnp.int32)

    tm_mm = _row_tile(seq, 512)
    tm_ew = _row_tile(seq, 256)
    tk_w = _row_tile(seq, 2048)
    ff_chunk = ff // 2 if (ff // 2) % 128 == 0 else ff
    IN, OUT, UP, DOWN = range(4)

    shards = [w.astype(BF16) for w in (w_in, w_out, w_up, w_down)]
    piece = lambda a, l: (shards[a], l)
    (w_in_0,), = _comm_only([_GatherJob([piece(IN, 0)])], "gather_first")

    taps = jnp.concatenate([conv_a_w.reshape(n_l, -1), conv_ffn_w.reshape(n_l, -1)], axis=1)
    tap_rows = -(-taps.size // 128 // 8) * 8
    tap_pack = jnp.zeros((tap_rows * 128,), F32).at[:taps.size].set(taps.reshape(-1)).reshape(tap_rows, 128)
    tap_all = _gather_small(tap_pack, "gather_taps")
    tap_all = tap_all.reshape(N_CHIP, -1)[:, :taps.size].reshape(N_CHIP, n_l, -1)
    ca = tap_all[:, :, :3 * d // N_CHIP].reshape(N_CHIP, n_l, 3, d // N_CHIP)
    cf = tap_all[:, :, 3 * d // N_CHIP:].reshape(N_CHIP, n_l, 3, 2 * ff // N_CHIP)
    conv_a_full = jnp.transpose(ca, (1, 2, 0, 3)).reshape(n_l, 3, d)
    conv_f_full = jnp.transpose(cf, (1, 2, 0, 3)).reshape(n_l, 3, 2 * ff)

    bmaps = jnp.repeat(jnp.swapaxes(b_s, 1, 2), gd, axis=2)

    xs = x.reshape(n, d)
    tgt = loss_target.reshape(n, d)

    def mixer_args(l, w_out_l):
        return (conv_a_full[l], ln_v_g[l][None], ln_v_b[l][None], w_s[l], bmaps[l], w_out_l.reshape(d, d))

    (proj0, h1_0), ((w_out_0, w_up_0),) = _norm_matmul(
        xs, mix_norm_g[0][None], w_in_0, tm=tm_mm, name="fwd_in_proj_0",
        jobs=[_GatherJob([piece(OUT, 0), piece(UP, 0)])])
    (x1_0, cz0), ((w_down_0, w_out_1),) = _mixer_fwd(
        proj0, xs, *mixer_args(0, w_out_0), seq=seq, tm=tm_ew, name="fwd_mixer_0",
        jobs=[_GatherJob([piece(DOWN, 0), piece(OUT, 1)])])
    (up0, h2_0), ((w_in_1,),) = _norm_matmul(
        x1_0, ffn_norm_g[0][None], w_up_0, tm=tm_mm, name="fwd_up_proj_0", jobs=[_GatherJob([piece(IN, 1)])])
    (x2_0, conv0), _ = _ffn_fwd(up0, x1_0, conv_f_full[0], w_down_0.reshape(ff, d), seq=seq, tm=tm_ew, cwid=ff_chunk,
                                name="fwd_ffn_0")
    (proj1, h1_1), ((w_up_1, w_down_1),) = _norm_matmul(
        x2_0, mix_norm_g[1][None], w_in_1, tm=tm_mm, name="fwd_in_proj_1",
        jobs=[_GatherJob([piece(UP, 1), piece(DOWN, 1)])])
    (x1_1, cz1), _ = _mixer_fwd(proj1, x2_0, *mixer_args(1, w_out_1), seq=seq, tm=tm_ew, name="fwd_mixer_1")
    (up1, h2_1), _ = _norm_matmul(x1_1, ffn_norm_g[1][None], w_up_1, tm=tm_mm, name="fwd_up_proj_1")
    (x2_1, conv1), _ = _ffn_fwd(up1, x1_1, conv_f_full[1], w_down_1.reshape(ff, d), seq=seq, tm=tm_ew, cwid=ff_chunk,
                                name="fwd_ffn_1")
    dx, loss_tile, d_final_g = _final_loss(x2_1, final_norm_g[None], tgt, tm=tm_mm, name="final_loss")
    loss = lax.psum(loss_tile[0, 0], axes)

    def chipwise(a):
        return a.reshape(N_CHIP, a.shape[0] // N_CHIP, a.shape[1])

    def pair_sums(grads, recvs, tag):
        return [_pair_sum(g, r, core, f"grad_pair_sum_{tag}_{a}") for a, (g, r) in enumerate(zip(grads, recvs))]

    d_up, dwd, d_cf1 = _ffn_bwd(dx, up1, conv1, conv_f_full[1], w_down_1.reshape(ff, d), seq=seq, tm=tm_ew,
                                cwid=ff_chunk, name="bwd_ffn_1")
    (dx1, d_g2_1), _ = _matmul_bwd_x(d_up, w_up_1, x1_1, ffn_norm_g[1][None], dx, tm=tm_mm, name="bwd_up_x_1")
    (dwu,), _ = _matmul_bwd_w(h2_1, d_up, N_CHIP, tk=tk_w, name="bwd_up_w_1")
    d_proj, dwo, d_ca1, d_lg1, d_lb1, d_ws1, d_bs1 = _mixer_bwd(
        dx1, proj1, cz1, *mixer_args(1, w_out_1), seq=seq, tm=tm_ew, name="bwd_mixer_1")
    (dx, d_g1_1), _ = _matmul_bwd_x(d_proj, w_in_1, x2_0, mix_norm_g[1][None], dx1, tm=tm_mm, name="bwd_in_x_1")
    (dwi,), _ = _matmul_bwd_w(h1_1, d_proj, N_CHIP, tk=tk_w, name="bwd_in_w_1")
    grads1 = [dwi, chipwise(dwo), dwu, chipwise(dwd)]
    small1 = [d_g1_1, d_ca1, d_lg1, d_lb1, d_ws1, d_bs1.T, d_g2_1, d_cf1]

    d_up, dwd, d_cf0 = _ffn_bwd(dx, up0, conv0, conv_f_full[0], w_down_0.reshape(ff, d), seq=seq, tm=tm_ew,
                                cwid=ff_chunk, name="bwd_ffn_0")
    (dx1, d_g2_0), (recv1,) = _matmul_bwd_x(d_up, w_up_0, x1_0, ffn_norm_g[0][None], dx, tm=tm_mm, name="bwd_up_x_0",
                                            jobs=[_SwapJob(grads1)])
    parts1 = pair_sums(grads1, recv1, 1)
    (dwu,), ((arr1_in,),) = _matmul_bwd_w(h2_0, d_up, N_CHIP, tk=tk_w, name="bwd_up_w_0",
                                          jobs=[_ScatterJob([parts1[IN]])])
    d_proj, dwo, d_ca0, d_lg0, d_lb0, d_ws0, d_bs0 = _mixer_bwd(
        dx1, proj0, cz0, *mixer_args(0, w_out_0), seq=seq, tm=tm_ew, name="bwd_mixer_0")
    early0 = [chipwise(dwo), dwu, chipwise(dwd)]
    (dwi,), (arr1_rest, recv0_early) = _matmul_bwd_w(
        h1_0, d_proj, N_CHIP, tk=tk_w, name="bwd_in_w_0",
        jobs=[_ScatterJob(parts1[OUT:]), _SwapJob(early0)])
    arrived1 = [arr1_in] + list(arr1_rest)
    gsum = [_final_sum(g, r, p, where, None, 1, n_l, f"grad_final_sum_1_{a}")
            for a, (g, r, p) in enumerate(zip(grads1, recv1, arrived1))]
    parts0_early = pair_sums(early0, recv0_early, "0e")
    (dx, d_g1_0), (gsum, arr0_early, (recv0_in,)) = _matmul_bwd_x(
        d_proj, w_in_0, xs, mix_norm_g[0][None], dx1, tm=tm_mm, name="bwd_in_x_0",
        jobs=[_ShareJob([(g, 1) for g in gsum]), _ScatterJob(parts0_early), _SwapJob([dwi])])
    grad_x = dx.reshape(bsz, seq, d)
    grads0 = [dwi] + early0
    recv0 = [recv0_in] + list(recv0_early)
    part0_in, = pair_sums([dwi], [recv0_in], "0i")
    small0 = [d_g1_0, d_ca0, d_lg0, d_lb0, d_ws0, d_bs0.T, d_g2_0, d_cf0]

    flat = [a.reshape(-1) for a in small0 + small1] + [d_final_g.reshape(-1)]
    sizes = [a.size for a in flat]
    total = sum(sizes)
    rows = -(-total // 128 // (2 * HALO)) * 2 * HALO
    pack = jnp.concatenate(flat + [jnp.zeros((rows * 128 - total,), F32)]).reshape(rows, 128)
    red, ((arr0_in,),) = _all_reduce_small(pack, "small_all_reduce", jobs=[_ScatterJob([part0_in])])
    red = red.reshape(-1)
    arrived0 = [arr0_in] + list(arr0_early)
    gsum = [_final_sum(g, r, p, where, prev, 0, n_l, f"grad_final_sum_0_{a}")
            for a, (g, r, p, prev) in enumerate(zip(grads0, recv0, arrived0, gsum))]
    (g_in, g_out, g_up, g_down), = _comm_only([_ShareJob([(g, 0) for g in gsum])], "grad_share_0")

    pieces, off = [], 0
    for s in sizes:
        pieces.append(red[off:off + s])
        off += s
    per_layer = [pieces[l * 8:(l + 1) * 8] for l in range(n_l)]

    def stacked(idx, shape):
        return jnp.stack([per_layer[l][idx].reshape(shape) for l in range(n_l)])

    def my_cols(a, width):
        return lax.dynamic_slice_in_dim(a, chip * width, width, axis=-1)

    g_mix_norm = stacked(0, (d,))
    g_conv_a = my_cols(stacked(1, (3, d)), d // N_CHIP)
    g_ln_g = stacked(2, (d,))
    g_ln_b = stacked(3, (d,))
    g_ws = stacked(4, (groups, CHUNK, CHUNK))
    g_bs = stacked(5, (groups, CHUNK))
    g_ffn_norm = stacked(6, (d,))
    g_conv_f = my_cols(stacked(7, (3, 2 * ff)), 2 * ff // N_CHIP)
    g_final = pieces[-1].reshape(1, d)

    big_names = [(w_in, g_in, m_w_in, v_w_in), (w_out, g_out, m_w_out, v_w_out), (w_up, g_up, m_w_up, v_w_up),
                 (w_down, g_down, m_w_down, v_w_down)]
    upd = [_adamw_big(w, g, m, v, f"adamw_{a}") for a, (w, g, m, v) in enumerate(big_names)]

    sm_w = [mix_norm_g, conv_a_w, ln_v_g, ln_v_b, w_s, b_s, ffn_norm_g, conv_ffn_w, final_norm_g[None]]
    sm_g = [g_mix_norm, g_conv_a, g_ln_g, g_ln_b, g_ws, g_bs, g_ffn_norm, g_conv_f, g_final]
    sm_m = [m_mix_norm_g, m_conv_a_w, m_ln_v_g, m_ln_v_b, m_w_s, m_b_s, m_ffn_norm_g, m_conv_ffn_w, m_final_norm_g[None]]
    sm_v = [v_mix_norm_g, v_conv_a_w, v_ln_v_g, v_ln_v_b, v_w_s, v_b_s, v_ffn_norm_g, v_conv_ffn_w, v_final_norm_g[None]]
    sm_d, sm_nm, sm_nv = _adamw_small(sm_w, sm_g, sm_m, sm_v, "adamw_small")

    def ordered(sm, bigs):
        return [sm[0], bigs[0], sm[1], sm[2], sm[3], sm[4], sm[5], bigs[1], sm[6], bigs[2], sm[7], bigs[3],
                sm[8].reshape(d)]

    out_g = ordered(sm_g, [g_in, g_out, g_up, g_down])
    out_d = ordered(sm_d, [u[0] for u in upd])
    out_m = ordered(sm_nm, [u[1] for u in upd])
    out_v = ordered(sm_nv, [u[2] for u in upd])
    return (loss, grad_x, *out_g, *out_d, *out_m, *out_v)
```

```python
import jax
import jax.numpy as jnp
from jax import lax
from jax.experimental import pallas as pl
from jax.experimental.pallas import tpu as pltpu

F32 = jnp.float32
BF16 = jnp.bfloat16
EPS = 1e-6
CHUNK = 128
N_CHIP = 4
HALO = 8
N_PIECES = 7
LATE_STEPS = 2
VMEM_LIMIT_V7X = 56 * 1024 * 1024
MESH_T = pl.DeviceIdType.MESH
HBM_SPEC = pl.BlockSpec(memory_space=pltpu.HBM)
VMEM_SPEC = pl.BlockSpec(memory_space=pltpu.VMEM)

ADAM_LR, ADAM_B1, ADAM_B2, ADAM_EPS, ADAM_WD, ADAM_STEP = 0.001, 0.9, 0.999, 1e-08, 0.01, 10

NT_DIMS = (((1,), (1,)), ((), ()))
TN_DIMS = (((0,), (0,)), ((), ()))


def _resident(block_shape, index_map):
    return pl.BlockSpec(block_shape, index_map, pipeline_mode=pl.Buffered(1))


def _full(shape):
    return pl.BlockSpec(shape, lambda *_: (0,) * len(shape))


def _row_tile(seq, want):
    t = min(seq, want)
    assert seq % t == 0 and t % CHUNK == 0, (seq, want)
    return t


def _shift_prev(prev8, cur, k):
    ext = jnp.concatenate([prev8, cur], axis=0)
    return pltpu.roll(ext, k, 0)[HALO:]


def _shift_next(cur, next8, k):
    ext = jnp.concatenate([cur, next8], axis=0)
    n = ext.shape[0]
    return pltpu.roll(ext, n - k, 0)[:n - HALO]


def _rowsum(a):
    return jnp.sum(a, axis=0, keepdims=True)


def _lanemean(a):
    return jnp.mean(a, axis=-1, keepdims=True)


def _tril_weights(ws_ref, groups):
    r = lax.broadcasted_iota(jnp.int32, (CHUNK, CHUNK), 0)
    c = lax.broadcasted_iota(jnp.int32, (CHUNK, CHUNK), 1)
    tril = r >= c
    return tril, [jnp.where(tril, ws_ref[g], 0.0) for g in range(groups)]


def _halo_prev(tm):
    return lambda i: jnp.maximum(i * (tm // HALO) - 1, 0)


def _halo_next(tm, n):
    last = n // HALO - 1
    return lambda i: jnp.minimum((i + 1) * (tm // HALO), last)


def _mesh_pos():
    return lax.axis_index("x"), lax.axis_index("y"), lax.axis_index("c")


def _peer_chip(x, y, r):
    return (1 - x if r >> 1 else x), (1 - y if r & 1 else y)


def _remote(src, dst, sems, k, to):
    return pltpu.make_async_remote_copy(src_ref=src, dst_ref=dst, send_sem=sems[0].at[k], recv_sem=sems[1].at[k],
                                        device_id=to, device_id_type=MESH_T)


class _GatherJob:
    def __init__(self, pieces):
        self.pieces = pieces
        self.ins = [p[0] for p in pieces]
        self.out_shapes = [jax.ShapeDtypeStruct((N_CHIP,) + p[0].shape[1:], p[0].dtype) for p in pieces]
        self.aliases = {}
        n = len(pieces)
        self.sems = [pltpu.SemaphoreType.DMA((3 * n,))] * 4 + [pltpu.SemaphoreType.DMA((n,))]

    def _half(self, outs, a, of_chip, core):
        rh = outs[a].shape[1] // 2
        return outs[a].at[of_chip, pl.ds(core * rh, rh), :]

    def _own(self, ins, outs, sems, chip):
        return [pltpu.make_async_copy(ins[a].at[layer], outs[a].at[chip], sems[4].at[a])
                for a, (_, layer) in enumerate(self.pieces)]

    def _sends(self, ins, outs, sems, x, y, c):
        chip, out = 2 * x + y, []
        for a, (_, layer) in enumerate(self.pieces):
            rh = outs[a].shape[1] // 2
            for r in (1, 2, 3):
                px, py = _peer_chip(x, y, r)
                out.append(_remote(ins[a].at[layer, pl.ds(c * rh, rh), :], self._half(outs, a, chip, c), sems[0:2],
                                   3 * a + r - 1, (px, py, c)))
        return out

    def _passes(self, outs, sems, x, y, c, core):
        out = []
        for a in range(len(self.pieces)):
            for r in (1, 2, 3):
                px, py = _peer_chip(x, y, r)
                landed = self._half(outs, a, 2 * px + py, core)
                out.append(_remote(landed, landed, sems[2:4], 3 * a + r - 1, (x, y, 1 - c)))
        return out

    def start(self, ins, outs, sems):
        x, y, c = _mesh_pos()
        for cp in self._own(ins, outs, sems, 2 * x + y) + self._sends(ins, outs, sems, x, y, c):
            cp.start()

    def late(self, ins, outs, sems):
        x, y, c = _mesh_pos()
        passes = self._passes(outs, sems, x, y, c, c)
        k = 0
        for a in range(len(self.pieces)):
            for r in (1, 2, 3):
                px, py = _peer_chip(x, y, r)
                landed = self._half(outs, a, 2 * px + py, c)
                _remote(landed, landed, sems[0:2], k, (px, py, c)).wait_recv()
                passes[k].start()
                k += 1

    def finish(self, ins, outs, sems):
        x, y, c = _mesh_pos()
        for cp in self._passes(outs, sems, x, y, c, 1 - c):
            cp.wait_recv()
        for cp in self._sends(ins, outs, sems, x, y, c) + self._passes(outs, sems, x, y, c, c):
            cp.wait_send()
        for cp in self._own(ins, outs, sems, 2 * x + y):
            cp.wait()


class _SwapJob:
    def __init__(self, pieces):
        self.ins = list(pieces)
        self.out_shapes = [jax.ShapeDtypeStruct((g.shape[0], g.shape[1] // 2, g.shape[2]), g.dtype) for g in pieces]
        self.aliases = {}
        self.sems = [pltpu.SemaphoreType.DMA((len(pieces),))] * 2

    def _copies(self, ins, outs, sems):
        x, y, c = _mesh_pos()
        out = []
        for a in range(len(ins)):
            rh = ins[a].shape[1] // 2
            out.append(_remote(ins[a].at[:, pl.ds((1 - c) * rh, rh), :], outs[a], sems, a, (x, y, 1 - c)))
        return out

    def start(self, ins, outs, sems):
        for cp in self._copies(ins, outs, sems):
            cp.start()

    def finish(self, ins, outs, sems):
        for cp in self._copies(ins, outs, sems):
            cp.wait()


class _ScatterJob:
    def __init__(self, pieces):
        self.ins = list(pieces)
        self.out_shapes = [jax.ShapeDtypeStruct((3,) + p.shape[1:], p.dtype) for p in pieces]
        self.aliases = {}
        self.sems = [pltpu.SemaphoreType.DMA((3 * len(pieces),))] * 2

    def _copies(self, ins, outs, sems):
        x, y, c = _mesh_pos()
        out = []
        for a in range(len(ins)):
            for r in (1, 2, 3):
                px, py = _peer_chip(x, y, r)
                out.append(_remote(ins[a].at[2 * px + py], outs[a].at[r - 1], sems, 3 * a + r - 1, (px, py, c)))
        return out

    def start(self, ins, outs, sems):
        for cp in self._copies(ins, outs, sems):
            cp.start()

    def finish(self, ins, outs, sems):
        for cp in self._copies(ins, outs, sems):
            cp.wait()


class _ShareJob:
    def __init__(self, pieces):
        self.pieces = pieces
        self.ins = [p[0] for p in pieces]
        self.out_shapes = [jax.ShapeDtypeStruct(p[0].shape, p[0].dtype) for p in pieces]
        self.aliases = {a: a for a in range(len(pieces))}
        self.sems = [pltpu.SemaphoreType.DMA((len(pieces),))] * 2

    def _copies(self, outs, sems, core):
        x, y, c = _mesh_pos()
        out = []
        for a, (_, layer) in enumerate(self.pieces):
            rh = outs[a].shape[1] // 2
            rows = outs[a].at[layer, pl.ds(core * rh, rh), :]
            out.append(_remote(rows, rows, sems, a, (x, y, 1 - c)))
        return out

    def start(self, ins, outs, sems):
        for cp in self._copies(outs, sems, lax.axis_index("c")):
            cp.start()

    def finish(self, ins, outs, sems):
        c = lax.axis_index("c")
        for cp in self._copies(outs, sems, c):
            cp.wait_send()
        for cp in self._copies(outs, sems, 1 - c):
            cp.wait_recv()


def _hosted_call(main, *, name, grid, in_specs, out_specs, out_shape, scratch_shapes=(), jobs=(), semantics=None,
                 operands=(), main_aliases=None):
    n_in, n_out, n_sc = len(in_specs), len(out_specs), len(scratch_shapes)
    counts = [(len(j.ins), len(j.out_shapes), len(j.sems)) for j in jobs]
    j_in, j_out, j_sc = (sum(c[k] for c in counts) for k in range(3))
    aliases, i0, o0 = dict(main_aliases or {}), n_in, n_out
    for j, (ci, co, _) in zip(jobs, counts):
        aliases.update({i0 + a: o0 + b for a, b in j.aliases.items()})
        i0, o0 = i0 + ci, o0 + co

    def body(*refs):
        cuts = [0, n_in, n_in + j_in, n_in + j_in + n_out, n_in + j_in + n_out + j_out,
                n_in + j_in + n_out + j_out + n_sc, len(refs)]
        m_in, jb_in, m_out, jb_out, m_sc, jb_sc = (list(refs[cuts[k]:cuts[k + 1]]) for k in range(6))

        def run(phase):
            i0 = o0 = s0 = 0
            for j, (ci, co, cs) in zip(jobs, counts):
                if hasattr(j, phase):
                    getattr(j, phase)(jb_in[i0:i0 + ci], jb_out[o0:o0 + co], jb_sc[s0:s0 + cs])
                i0, o0, s0 = i0 + ci, o0 + co, s0 + cs

        step, n_steps = 0, 1
        for ax in range(len(grid)):
            step, n_steps = step * grid[ax] + pl.program_id(ax), n_steps * grid[ax]
        if jobs:
            pl.when(step == 0)(lambda: run("start"))
        main(m_in, m_out, m_sc)
        if jobs:
            pl.when(step == max(n_steps - 1 - LATE_STEPS, 0))(lambda: run("late"))
            pl.when(step == n_steps - 1)(lambda: run("finish"))

    if semantics is None or jobs:
        semantics = ("arbitrary",) * len(grid)
    outs = pl.pallas_call(
        body, name=name, grid=grid,
        in_specs=list(in_specs) + [HBM_SPEC] * j_in,
        out_specs=list(out_specs) + [HBM_SPEC] * j_out,
        out_shape=list(out_shape) + [s for j in jobs for s in j.out_shapes],
        scratch_shapes=list(scratch_shapes) + [s for j in jobs for s in j.sems],
        input_output_aliases=aliases,
        compiler_params=pltpu.CompilerParams(dimension_semantics=semantics, vmem_limit_bytes=VMEM_LIMIT_V7X),
    )(*operands, *[a for j in jobs for a in j.ins])
    main_outs, rest, job_outs = list(outs[:n_out]), list(outs[n_out:]), []
    for _, co, _ in counts:
        job_outs.append(rest[:co])
        rest = rest[co:]
    return main_outs, job_outs


def _comm_only(jobs, name):
    return _hosted_call(lambda i, o, s: None, name=name, grid=(1,), in_specs=[], out_specs=[], out_shape=[],
                        jobs=jobs)[1]


def _norm_matmul(x, g, wg, *, tm, name, jobs=()):
    n, d = x.shape
    n_chip, _, ck = wg.shape

    def main(ins, outs, _):
        x_ref, g_ref, w_ref = ins
        o_ref, ht_ref = outs
        xf = x_ref[...]
        r = lax.rsqrt(_lanemean(xf * xf) + EPS)
        h = xf * r * g_ref[...]
        ht_ref[...] = h.T.astype(BF16)
        hb = h.astype(BF16)
        for j in range(n_chip):
            o_ref[:, j * ck:(j + 1) * ck] = jnp.dot(hb, w_ref[j], preferred_element_type=F32).astype(BF16)

    return _hosted_call(
        main, name=name, grid=(n // tm,), jobs=jobs, semantics=("parallel",), operands=(x, g, wg),
        in_specs=[pl.BlockSpec((tm, d), lambda i: (i, 0)), _full((1, d)), _resident((n_chip, d, ck), lambda i: (0, 0, 0))],
        out_specs=[pl.BlockSpec((tm, n_chip * ck), lambda i: (i, 0)), pl.BlockSpec((d, tm), lambda i: (0, i))],
        out_shape=[jax.ShapeDtypeStruct((n, n_chip * ck), BF16), jax.ShapeDtypeStruct((d, n), BF16)])


def _mixer_fwd(proj, x, cw, lg, lb, ws, bmap, wout, *, seq, tm, name, jobs=()):
    n, d = x.shape
    groups = ws.shape[0]
    gd = d // groups
    prev = _halo_prev(tm)

    def main(ins, outs, scratch):
        proj_ref, pcg_ref, pxi_ref, x_ref, cw_ref, lg_ref, lb_ref, ws_ref, bmap_ref, wout_ref = ins
        o_ref, cz_ref = outs
        vn_s, mixed_s = scratch
        seq_start = (pl.program_id(0) * tm) % seq == 0

        def piece(k):
            return proj_ref[:, k * d:(k + 1) * d].astype(F32)

        z = piece(1) * piece(2)
        zprev = jnp.where(seq_start, 0.0, pcg_ref[...].astype(F32) * pxi_ref[...].astype(F32))
        cz = (cw_ref[0:1, :] * _shift_prev(zprev, z, 2) + cw_ref[1:2, :] * _shift_prev(zprev, z, 1)
              + cw_ref[2:3, :] * z)
        cz_ref[...] = cz.astype(BF16)
        ya = piece(0) * cz

        v = piece(4)
        xc = v - _lanemean(v)
        vn = xc * lax.rsqrt(_lanemean(xc * xc) + EPS) * lg_ref[...] + lb_ref[...]
        vn_s[...] = vn.astype(BF16)
        _, wsm = _tril_weights(ws_ref, groups)
        for ck in range(tm // CHUNK):
            rows = slice(ck * CHUNK, (ck + 1) * CHUNK)
            for g in range(groups):
                cols = slice(g * gd, (g + 1) * gd)
                mixed_s[rows, cols] = (jnp.dot(wsm[g].astype(BF16), vn_s[rows, cols], preferred_element_type=F32)
                                       + bmap_ref[:, cols])
        yb = piece(3) * mixed_s[...]
        merged = jax.nn.sigmoid(piece(5)) * ya + jax.nn.sigmoid(piece(6)) * yb
        o_ref[...] = x_ref[...] + jnp.dot(merged.astype(BF16), wout_ref[...], preferred_element_type=F32)

    return _hosted_call(
        main, name=name, grid=(n // tm,), jobs=jobs, semantics=("parallel",),
        operands=(proj, proj, proj, x, cw, lg, lb, ws, bmap, wout),
        in_specs=[pl.BlockSpec((tm, N_PIECES * d), lambda i: (i, 0)),
                  pl.BlockSpec((HALO, d), lambda i: (prev(i), 1)),
                  pl.BlockSpec((HALO, d), lambda i: (prev(i), 2)),
                  pl.BlockSpec((tm, d), lambda i: (i, 0)),
                  _full((3, d)), _full((1, d)), _full((1, d)), _full((groups, CHUNK, CHUNK)), _full((CHUNK, d)),
                  _resident((d, d), lambda i: (0, 0))],
        out_specs=[pl.BlockSpec((tm, d), lambda i: (i, 0)), pl.BlockSpec((tm, d), lambda i: (i, 0))],
        out_shape=[jax.ShapeDtypeStruct((n, d), F32), jax.ShapeDtypeStruct((n, d), BF16)],
        scratch_shapes=[pltpu.VMEM((tm, d), BF16), pltpu.VMEM((tm, d), F32)])


def _ffn_fwd(up, x, cw, wd, *, seq, tm, cwid, name, jobs=()):
    n, d = x.shape
    ff = wd.shape[0]
    prev = _halo_prev(tm)

    def main(ins, outs, _):
        up_ref, pup_ref, x_ref, cw_ref, wd_ref = ins
        o_ref, conv_ref = outs
        seq_start = (pl.program_id(0) * tm) % seq == 0

        def conv(lo):
            cols = slice(lo, lo + cwid)
            cur = up_ref[:, cols].astype(F32)
            pre = jnp.where(seq_start, 0.0, pup_ref[:, cols].astype(F32))
            out = (cw_ref[0:1, cols] * _shift_prev(pre, cur, 2) + cw_ref[1:2, cols] * _shift_prev(pre, cur, 1)
                   + cw_ref[2:3, cols] * cur)
            conv_ref[:, cols] = out.astype(BF16)
            return out

        acc = x_ref[...]
        for cj in range(ff // cwid):
            gate = conv(cj * cwid)
            val = conv(ff + cj * cwid)
            a = gate * jax.nn.sigmoid(gate) * val
            acc = acc + jnp.dot(a.astype(BF16), wd_ref[cj * cwid:(cj + 1) * cwid, :], preferred_element_type=F32)
        o_ref[...] = acc

    return _hosted_call(
        main, name=name, grid=(n // tm,), jobs=jobs, semantics=("parallel",), operands=(up, up, x, cw, wd),
        in_specs=[pl.BlockSpec((tm, 2 * ff), lambda i: (i, 0)),
                  pl.BlockSpec((HALO, 2 * ff), lambda i: (prev(i), 0)),
                  pl.BlockSpec((tm, d), lambda i: (i, 0)),
                  _full((3, 2 * ff)),
                  _resident((ff, d), lambda i: (0, 0))],
        out_specs=[pl.BlockSpec((tm, d), lambda i: (i, 0)), pl.BlockSpec((tm, 2 * ff), lambda i: (i, 0))],
        out_shape=[jax.ShapeDtypeStruct((n, d), F32), jax.ShapeDtypeStruct((n, 2 * ff), BF16)])


def _final_loss(x, g, target, *, tm, name):
    n, d = x.shape

    def body(x_ref, g_ref, t_ref, dx_ref, loss_ref, dg_ref):
        @pl.when(pl.program_id(0) == 0)
        def _():
            loss_ref[...] = jnp.zeros_like(loss_ref)
            dg_ref[...] = jnp.zeros_like(dg_ref)

        xf = x_ref[...]
        r = lax.rsqrt(_lanemean(xf * xf) + EPS)
        xhat = xf * r
        diff = xhat * g_ref[...] - t_ref[...]
        loss_ref[...] += (0.5 / d) * _rowsum(jnp.sum(diff * diff, axis=-1, keepdims=True))
        dy = diff * (1.0 / d)
        dg_ref[...] += _rowsum(dy * xhat)
        dyh = dy * g_ref[...]
        dx_ref[...] = r * (dyh - xhat * _lanemean(dyh * xhat))

    return pl.pallas_call(
        body, name=name, grid=(n // tm,),
        in_specs=[pl.BlockSpec((tm, d), lambda i: (i, 0)), _full((1, d)), pl.BlockSpec((tm, d), lambda i: (i, 0))],
        out_specs=[pl.BlockSpec((tm, d), lambda i: (i, 0)), _full((HALO, CHUNK)), _full((1, d))],
        out_shape=[jax.ShapeDtypeStruct((n, d), F32), jax.ShapeDtypeStruct((HALO, CHUNK), F32),
                   jax.ShapeDtypeStruct((1, d), F32)],
        compiler_params=pltpu.CompilerParams(dimension_semantics=("arbitrary",), vmem_limit_bytes=VMEM_LIMIT_V7X),
    )(x, g, target)


def _ffn_bwd(dx, up, conv, cw, wd, *, seq, tm, cwid, name):
    n, d = dx.shape
    ff = wd.shape[0]
    nxt = _halo_next(tm, n)

    def body(dx_ref, dxn_ref, up_ref, conv_ref, nconv_ref, cw_ref, wd_ref, dup_ref, dwd_ref, dcw_ref):
        i = pl.program_id(0)
        keep_next = jnp.where(((i + 1) * tm) % seq == 0, 0.0, 1.0)

        @pl.when(i == 0)
        def _():
            dwd_ref[...] = jnp.zeros_like(dwd_ref)
            dcw_ref[...] = jnp.zeros_like(dcw_ref)

        dxe = jnp.concatenate([dx_ref[...], dxn_ref[...]], axis=0).astype(BF16)
        dxb = dxe[:tm]
        for cj in range(ff // cwid):
            rows = slice(cj * cwid, (cj + 1) * cwid)
            g_cols, v_cols = slice(cj * cwid, (cj + 1) * cwid), slice(ff + cj * cwid, ff + (cj + 1) * cwid)
            dae = lax.dot_general(dxe, wd_ref[rows, :], NT_DIMS, preferred_element_type=F32)
            da, dan = dae[:tm], dae[tm:]

            def grads(gate, val, da_rows):
                sg = jax.nn.sigmoid(gate)
                sl = gate * sg
                return sl, da_rows * val * sg * (1.0 + gate * (1.0 - sg)), da_rows * sl

            gate, val = conv_ref[:, g_cols].astype(F32), conv_ref[:, v_cols].astype(F32)
            sl, d_gate, d_val = grads(gate, val, da)
            dwd_ref[rows, :] += lax.dot_general((sl * val).astype(BF16), dxb, TN_DIMS, preferred_element_type=F32)
            _, d_gate_n, d_val_n = grads(nconv_ref[:, g_cols].astype(F32), nconv_ref[:, v_cols].astype(F32),
                                         dan * keep_next)
            for cols, dcur, dnext in ((g_cols, d_gate, d_gate_n), (v_cols, d_val, d_val_n)):
                upc = up_ref[:, cols].astype(F32)
                d1, d2 = _shift_next(dcur, dnext, 1), _shift_next(dcur, dnext, 2)
                for k, dk in enumerate((d2, d1, dcur)):
                    dcw_ref[k:k + 1, cols] += _rowsum(dk * upc)
                dup = cw_ref[2:3, cols] * dcur + cw_ref[1:2, cols] * d1 + cw_ref[0:1, cols] * d2
                dup_ref[:, cols] = dup.astype(BF16)

    return pl.pallas_call(
        body, name=name, grid=(n // tm,),
        in_specs=[pl.BlockSpec((tm, d), lambda i: (i, 0)),
                  pl.BlockSpec((HALO, d), lambda i: (nxt(i), 0)),
                  pl.BlockSpec((tm, 2 * ff), lambda i: (i, 0)),
                  pl.BlockSpec((tm, 2 * ff), lambda i: (i, 0)),
                  pl.BlockSpec((HALO, 2 * ff), lambda i: (nxt(i), 0)),
                  _full((3, 2 * ff)),
                  _resident((ff, d), lambda i: (0, 0))],
        out_specs=[pl.BlockSpec((tm, 2 * ff), lambda i: (i, 0)), _full((ff, d)), _full((3, 2 * ff))],
        out_shape=[jax.ShapeDtypeStruct((n, 2 * ff), BF16), jax.ShapeDtypeStruct((ff, d), F32),
                   jax.ShapeDtypeStruct((3, 2 * ff), F32)],
        compiler_params=pltpu.CompilerParams(dimension_semantics=("arbitrary",), vmem_limit_bytes=VMEM_LIMIT_V7X),
    )(dx, dx, up, conv, conv, cw, wd)


def _matmul_bwd_x(dy, wg, x, g, dres, *, tm, name, jobs=(), tiles=None, before=None):
    n, d = x.shape
    n_chip, _, ck = wg.shape
    first_tile, n_tiles = tiles or (0, n // tm)
    extra = [] if before is None else list(before)

    def main(ins, outs, _):
        dy_ref, w_ref, x_ref, g_ref, dres_ref = ins[:5]
        dx_ref, dg_ref = outs

        @pl.when(pl.program_id(0) == 0)
        def _():
            dg_ref[...] = jnp.zeros_like(dg_ref) if before is None else ins[6][...]

        dh = lax.dot_general(dy_ref[:, 0:ck], w_ref[0], NT_DIMS, preferred_element_type=F32)
        for j in range(1, n_chip):
            dh = dh + lax.dot_general(dy_ref[:, j * ck:(j + 1) * ck], w_ref[j], NT_DIMS, preferred_element_type=F32)
        xf = x_ref[...]
        r = lax.rsqrt(_lanemean(xf * xf) + EPS)
        xhat = xf * r
        dg_ref[...] += _rowsum(dh * xhat)
        dyh = dh * g_ref[...]
        dx_ref[...] = dres_ref[...] + r * (dyh - xhat * _lanemean(dyh * xhat))

    rows = lambda width: pl.BlockSpec((tm, width), lambda i: (i + first_tile, 0))
    return _hosted_call(
        main, name=name, grid=(n_tiles,), jobs=jobs, semantics=("arbitrary",), operands=(dy, wg, x, g, dres, *extra),
        in_specs=[rows(n_chip * ck), _resident((n_chip, d, ck), lambda i: (0, 0, 0)), rows(d), _full((1, d)), rows(d)]
        + ([] if before is None else [pl.BlockSpec(memory_space=pl.ANY), _full((1, d))]),
        out_specs=[rows(d), _full((1, d))],
        out_shape=[jax.ShapeDtypeStruct((n, d), F32), jax.ShapeDtypeStruct((1, d), F32)],
        main_aliases=None if before is None else {5: 0})


def _matmul_bwd_w(ht, dy, n_chip, *, tk, name, jobs=()):
    d, n = ht.shape
    ck = dy.shape[1] // n_chip

    def main(ins, outs, _):
        ht_ref, dy_ref = ins
        o_ref, = outs

        @pl.when(pl.program_id(1) == 0)
        def _():
            o_ref[...] = jnp.zeros_like(o_ref)

        o_ref[...] += jnp.dot(ht_ref[...], dy_ref[...], preferred_element_type=F32)

    return _hosted_call(
        main, name=name, grid=(n_chip, n // tk), jobs=jobs, semantics=("parallel", "arbitrary"), operands=(ht, dy),
        in_specs=[pl.BlockSpec((d, tk), lambda j, k: (0, k)), pl.BlockSpec((tk, ck), lambda j, k: (k, j))],
        out_specs=[pl.BlockSpec((None, d, ck), lambda j, k: (j, 0, 0))],
        out_shape=[jax.ShapeDtypeStruct((n_chip, d, ck), F32)])


def _mixer_bwd(dx, proj, cz, cw, lg, lb, ws, bmap, wout, *, seq, tm, name, jobs=()):
    n, d = dx.shape
    groups = ws.shape[0]
    gd = d // groups
    nxt = _halo_next(tm, n)
    n_tiles = n // tm

    def main(ins, outs, scratch):
        proj_ref, cz_ref, nbg_ref, nga_ref, dx_ref, dxn_ref, cw_ref, lg_ref, lb_ref, ws_ref, bmap_ref, wout_ref = ins
        dproj_ref, dwout_ref, dcw_ref, dlg_ref, dlb_ref, dws_ref, dbs_ref = outs
        vn_s, mixed_s, dmix_s, dvn_s, dbmap_s = scratch
        i = pl.program_id(0)
        keep_next = jnp.where(((i + 1) * tm) % seq == 0, 0.0, 1.0)

        @pl.when(i == 0)
        def _():
            for ref in (dwout_ref, dcw_ref, dlg_ref, dlb_ref, dws_ref, dbmap_s):
                ref[...] = jnp.zeros_like(ref)

        def piece(k):
            return proj_ref[:, k * d:(k + 1) * d].astype(F32)

        def put(k, val):
            dproj_ref[:, k * d:(k + 1) * d] = val.astype(BF16)

        w = [cw_ref[k:k + 1, :] for k in range(3)]
        cg, xi = piece(1), piece(2)
        cz = cz_ref[...].astype(F32)
        bg = piece(0)
        ya = bg * cz

        v = piece(4)
        xc = v - _lanemean(v)
        rstd = lax.rsqrt(_lanemean(xc * xc) + EPS)
        vhat = xc * rstd
        vn_s[...] = (vhat * lg_ref[...] + lb_ref[...]).astype(BF16)
        tril, wsm = _tril_weights(ws_ref, groups)
        for ck in range(tm // CHUNK):
            rows = slice(ck * CHUNK, (ck + 1) * CHUNK)
            for g in range(groups):
                cols = slice(g * gd, (g + 1) * gd)
                mixed_s[rows, cols] = (jnp.dot(wsm[g].astype(BF16), vn_s[rows, cols], preferred_element_type=F32)
                                       + bmap_ref[:, cols])
        u = piece(3)
        mixed = mixed_s[...]
        yb = u * mixed
        sa, sb = jax.nn.sigmoid(piece(5)), jax.nn.sigmoid(piece(6))
        merged = sa * ya + sb * yb

        dxe = jnp.concatenate([dx_ref[...], dxn_ref[...]], axis=0).astype(BF16)
        dme = lax.dot_general(dxe, wout_ref[...], NT_DIMS, preferred_element_type=F32)
        dm, dm_n = dme[:tm], dme[tm:]
        dwout_ref[...] += lax.dot_general(merged.astype(BF16), dxe[:tm], TN_DIMS, preferred_element_type=F32)

        put(5, dm * ya * sa * (1.0 - sa))
        put(6, dm * yb * sb * (1.0 - sb))
        d_ya, d_yb = dm * sa, dm * sb
        put(0, d_ya * cz)
        d_cz = d_ya * bg
        d_cz_n = dm_n * jax.nn.sigmoid(nga_ref[...].astype(F32)) * nbg_ref[...].astype(F32) * keep_next
        d_cz1, d_cz2 = _shift_next(d_cz, d_cz_n, 1), _shift_next(d_cz, d_cz_n, 2)
        z = cg * xi
        for k, dk in enumerate((d_cz2, d_cz1, d_cz)):
            dcw_ref[k:k + 1, :] += _rowsum(dk * z)
        dz = w[2] * d_cz + w[1] * d_cz1 + w[0] * d_cz2
        put(1, dz * xi)
        put(2, dz * cg)

        put(3, d_yb * mixed)
        d_mixed = d_yb * u
        dmix_s[...] = d_mixed.astype(BF16)
        for ck in range(tm // CHUNK):
            rows = slice(ck * CHUNK, (ck + 1) * CHUNK)
            dbmap_s[...] += d_mixed[rows, :]
            for g in range(groups):
                cols = slice(g * gd, (g + 1) * gd)
                dvn_s[rows, cols] = jnp.dot(wsm[g].T.astype(BF16), dmix_s[rows, cols], preferred_element_type=F32)
                dws_ref[g] += jnp.where(
                    tril, lax.dot_general(dmix_s[rows, cols], vn_s[rows, cols], NT_DIMS, preferred_element_type=F32),
                    0.0)
        d_vn = dvn_s[...]
        dlg_ref[...] += _rowsum(d_vn * vhat)
        dlb_ref[...] += _rowsum(d_vn)
        d_vhat = d_vn * lg_ref[...]
        put(4, rstd * (d_vhat - _lanemean(d_vhat) - vhat * _lanemean(d_vhat * vhat)))

        @pl.when(i == n_tiles - 1)
        def _():
            for g in range(groups):
                dbs_ref[:, g:g + 1] = jnp.sum(dbmap_s[:, g * gd:(g + 1) * gd], axis=-1, keepdims=True)

    return _hosted_call(
        main, name=name, grid=(n_tiles,), jobs=jobs, semantics=("arbitrary",),
        operands=(proj, cz, proj, proj, dx, dx, cw, lg, lb, ws, bmap, wout),
        in_specs=[pl.BlockSpec((tm, N_PIECES * d), lambda i: (i, 0)),
                  pl.BlockSpec((tm, d), lambda i: (i, 0)),
                  pl.BlockSpec((HALO, d), lambda i: (nxt(i), 0)),
                  pl.BlockSpec((HALO, d), lambda i: (nxt(i), 5)),
                  pl.BlockSpec((tm, d), lambda i: (i, 0)),
                  pl.BlockSpec((HALO, d), lambda i: (nxt(i), 0)),
                  _full((3, d)), _full((1, d)), _full((1, d)), _full((groups, CHUNK, CHUNK)), _full((CHUNK, d)),
                  _resident((d, d), lambda i: (0, 0))],
        out_specs=[pl.BlockSpec((tm, N_PIECES * d), lambda i: (i, 0)), _full((d, d)),
                   _full((3, d)), _full((1, d)), _full((1, d)), _full((groups, CHUNK, CHUNK)), _full((CHUNK, groups))],
        out_shape=[jax.ShapeDtypeStruct((n, N_PIECES * d), BF16), jax.ShapeDtypeStruct((d, d), F32),
                   jax.ShapeDtypeStruct((3, d), F32), jax.ShapeDtypeStruct((1, d), F32),
                   jax.ShapeDtypeStruct((1, d), F32), jax.ShapeDtypeStruct((groups, CHUNK, CHUNK), F32),
                   jax.ShapeDtypeStruct((CHUNK, groups), F32)],
        scratch_shapes=[pltpu.VMEM((tm, d), BF16), pltpu.VMEM((tm, d), F32), pltpu.VMEM((tm, d), BF16),
                        pltpu.VMEM((tm, d), F32), pltpu.VMEM((CHUNK, d), F32)])


def _gather_small(pack, name):
    def body(p_ref, o_ref, send_sem, recv_sem):
        x, y, c = _mesh_pos()
        chip = 2 * x + y
        o_ref[chip] = p_ref[...]
        copies = []
        for r in (1, 2, 3):
            px, py = _peer_chip(x, y, r)
            copies.append(_remote(p_ref, o_ref.at[chip], (send_sem, recv_sem), r - 1, (px, py, c)))
        for cp in copies:
            cp.start()
        for r, cp in zip((1, 2, 3), copies):
            px, py = _peer_chip(x, y, r)
            landed = o_ref.at[2 * px + py]
            _remote(landed, landed, (send_sem, recv_sem), r - 1, (px, py, c)).wait_recv()
            cp.wait_send()

    return pl.pallas_call(
        body, name=name, in_specs=[VMEM_SPEC], out_specs=VMEM_SPEC,
        out_shape=jax.ShapeDtypeStruct((N_CHIP,) + pack.shape, pack.dtype),
        scratch_shapes=[pltpu.SemaphoreType.DMA((3,)), pltpu.SemaphoreType.DMA((3,))],
    )(pack)


def _all_reduce_small(pack, name, jobs=()):
    rows = pack.shape[0]
    rh = rows // 2
    assert rh % HALO == 0, rows

    def main(ins, outs, scratch):
        p_ref, = ins
        o_ref, = outs
        sib_buf, chip_buf, got_buf, send_sem, recv_sem = scratch
        sems = (send_sem, recv_sem)
        x, y, c = _mesh_pos()
        chip, sibling = 2 * x + y, (x, y, 1 - c)
        mine = pl.ds(pl.multiple_of(c * rh, HALO), rh)
        theirs = pl.ds(pl.multiple_of((1 - c) * rh, HALO), rh)
        swap = _remote(p_ref.at[theirs], sib_buf, sems, 0, sibling)
        swap.start()
        swap.wait()
        chip_buf[chip] = p_ref[mine, :] + sib_buf[...]
        copies = []
        for r in (1, 2, 3):
            px, py = _peer_chip(x, y, r)
            copies.append(_remote(chip_buf.at[chip], chip_buf.at[chip], sems, r, (px, py, c)))
        for cp in copies:
            cp.start()
        for r, cp in zip((1, 2, 3), copies):
            px, py = _peer_chip(x, y, r)
            landed = chip_buf.at[2 * px + py]
            _remote(landed, landed, sems, r, (px, py, c)).wait_recv()
            cp.wait_send()
        total = ((chip_buf[0] + chip_buf[1]) + chip_buf[2]) + chip_buf[3]
        o_ref[mine, :] = total
        chip_buf[chip] = total
        share = _remote(chip_buf.at[chip], got_buf, sems, 4, sibling)
        share.start()
        share.wait()
        o_ref[theirs, :] = got_buf[...]

    outs, job_outs = _hosted_call(
        main, name=name, grid=(1,), jobs=jobs, operands=(pack,),
        in_specs=[_full(pack.shape)], out_specs=[_full(pack.shape)],
        out_shape=[jax.ShapeDtypeStruct(pack.shape, pack.dtype)],
        scratch_shapes=[pltpu.VMEM((rh, 128), F32), pltpu.VMEM((N_CHIP, rh, 128), F32), pltpu.VMEM((rh, 128), F32),
                        pltpu.SemaphoreType.DMA((5,)), pltpu.SemaphoreType.DMA((5,))])
    return outs[0], job_outs


def _block_rows(rows, cols, itemsize=4, budget=2 << 20):
    best = None
    for t in range(16, rows + 1, 16):
        if rows % t == 0 and t * cols * itemsize <= budget:
            best = t
    assert best is not None, (rows, cols)
    return best


def _pair_sum(grad, recv, core, name):
    n_chip, rk, ck = grad.shape
    rh = rk // 2
    tr = _block_rows(rh, ck)
    nb = rh // tr

    def body(core_ref, g_ref, r_ref, o_ref):
        o_ref[...] = (g_ref[...] + r_ref[...]).astype(BF16)

    blk = (None, tr, ck)
    return pl.pallas_call(
        body, name=name,
        grid_spec=pltpu.PrefetchScalarGridSpec(
            num_scalar_prefetch=1, grid=(n_chip, nb),
            in_specs=[pl.BlockSpec(blk, lambda k, i, core_ref: (k, core_ref[0] * nb + i, 0)),
                      pl.BlockSpec(blk, lambda k, i, core_ref: (k, i, 0))],
            out_specs=pl.BlockSpec(blk, lambda k, i, core_ref: (k, i, 0))),
        out_shape=jax.ShapeDtypeStruct((n_chip, rh, ck), BF16),
        compiler_params=pltpu.CompilerParams(dimension_semantics=("parallel", "parallel"),
                                             vmem_limit_bytes=VMEM_LIMIT_V7X),
    )(core, grad, recv)


def _final_sum(grad, recv, arrived, where, prev, layer, n_layers, name):
    n_chip, rk, ck = grad.shape
    rh = rk // 2
    tr = _block_rows(rh, ck)
    nb = rh // tr

    def body(where_ref, g_ref, r_ref, a1_ref, a2_ref, a3_ref, *rest):
        o_ref = rest[-1]
        own = g_ref[...] + r_ref[...]
        o_ref[...] = ((own + a1_ref[...].astype(F32)) + a2_ref[...].astype(F32)) + a3_ref[...].astype(F32)

    blk = (None, tr, ck)
    slot = lambda r: pl.BlockSpec(blk, lambda i, w: (r, i, 0))
    extra = [] if prev is None else [prev]
    return pl.pallas_call(
        body, name=name,
        grid_spec=pltpu.PrefetchScalarGridSpec(
            num_scalar_prefetch=1, grid=(nb,),
            in_specs=[pl.BlockSpec(blk, lambda i, w: (w[1], w[0] * nb + i, 0)),
                      pl.BlockSpec(blk, lambda i, w: (w[1], i, 0)),
                      slot(0), slot(1), slot(2)] + [pl.BlockSpec(memory_space=pl.ANY)] * len(extra),
            out_specs=pl.BlockSpec(blk, lambda i, w: (layer, w[0] * nb + i, 0))),
        out_shape=jax.ShapeDtypeStruct((n_layers, rk, ck), F32),
        input_output_aliases={6: 0} if extra else {},
        compiler_params=pltpu.CompilerParams(dimension_semantics=("parallel",), vmem_limit_bytes=VMEM_LIMIT_V7X),
    )(where, grad, recv, arrived, arrived, arrived, *extra)


def _adamw_math(w, g, m, v):
    m = ADAM_B1 * m + (1.0 - ADAM_B1) * g
    v = ADAM_B2 * v + (1.0 - ADAM_B2) * (g * g)
    m_hat = m / (1.0 - ADAM_B1 ** ADAM_STEP)
    v_hat = v / (1.0 - ADAM_B2 ** ADAM_STEP)
    delta = -ADAM_LR * (m_hat / (jnp.sqrt(v_hat) + ADAM_EPS) + ADAM_WD * w)
    return delta, m, v


def _adamw_big(w, g, m, v, name):
    n_l, rk, ck = w.shape
    tr = _block_rows(rk, ck, budget=1 << 20)

    def body(w_ref, g_ref, m_ref, v_ref, d_ref, nm_ref, nv_ref):
        d_ref[...], nm_ref[...], nv_ref[...] = _adamw_math(w_ref[...], g_ref[...], m_ref[...], v_ref[...])

    spec = pl.BlockSpec((None, tr, ck), lambda l, i: (l, i, 0))
    return pl.pallas_call(
        body, name=name, grid=(n_l, rk // tr), in_specs=[spec] * 4, out_specs=[spec] * 3,
        out_shape=[jax.ShapeDtypeStruct(w.shape, F32)] * 3,
        compiler_params=pltpu.CompilerParams(dimension_semantics=("parallel", "parallel"),
                                             vmem_limit_bytes=VMEM_LIMIT_V7X),
    )(w, g, m, v)


def _adamw_small(ws, gs, ms, vs, name):
    n_p = len(ws)

    def body(*refs):
        ins, outs = refs[:4 * n_p], refs[4 * n_p:]
        for p in range(n_p):
            res = _adamw_math(ins[p][...], ins[n_p + p][...], ins[2 * n_p + p][...], ins[3 * n_p + p][...])
            for q in range(3):
                outs[q * n_p + p][...] = res[q]

    outs = pl.pallas_call(
        body, name=name, in_specs=[VMEM_SPEC] * (4 * n_p), out_specs=[VMEM_SPEC] * (3 * n_p),
        out_shape=[jax.ShapeDtypeStruct(w.shape, F32) for w in ws] * 3,
        compiler_params=pltpu.CompilerParams(vmem_limit_bytes=VMEM_LIMIT_V7X),
    )(*ws, *gs, *ms, *vs)
    return outs[:n_p], outs[n_p:2 * n_p], outs[2 * n_p:]


def kernel(x, mix_norm_g, w_in, conv_a_w, ln_v_g, ln_v_b, w_s, b_s, w_out, ffn_norm_g, w_up, conv_ffn_w, w_down, final_norm_g, loss_target, m_mix_norm_g, m_w_in, m_conv_a_w, m_ln_v_g, m_ln_v_b, m_w_s, m_b_s, m_w_out, m_ffn_norm_g, m_w_up, m_conv_ffn_w, m_w_down, m_final_norm_g, v_mix_norm_g, v_w_in, v_conv_a_w, v_ln_v_g, v_ln_v_b, v_w_s, v_b_s, v_w_out, v_ffn_norm_g, v_w_up, v_conv_ffn_w, v_w_down, v_final_norm_g):
    bsz, seq, d = x.shape
    n = bsz * seq
    n_l, groups = w_s.shape[0], w_s.shape[1]
    assert n_l == 2, "the exchange schedule below is written for two layers"
    gd = d // groups
    ff = w_down.shape[1] * N_CHIP
    axes = ("x", "y", "c")
    mx, my, mc = _mesh_pos()
    chip = 2 * mx + my
    core = jnp.reshape(mc, (1,)).astype(jnp.int32)
    where = jnp.stack([mc, chip]).astype(jnp.int32)

    tm_mm = _row_tile(seq, 512)
    tm_ew = _row_tile(seq, 256)
    tk_w = _row_tile(seq, 2048)
    ff_chunk = ff // 2 if (ff // 2) % 128 == 0 else ff
    IN, OUT, UP, DOWN = range(4)

    shards = [w.astype(BF16) for w in (w_in, w_out, w_up, w_down)]
    piece = lambda a, l: (shards[a], l)
    (w_in_0,), = _comm_only([_GatherJob([piece(IN, 0)])], "gather_first")

    taps = jnp.concatenate([conv_a_w.reshape(n_l, -1), conv_ffn_w.reshape(n_l, -1)], axis=1)
    tap_rows = -(-taps.size // 128 // 8) * 8
    tap_pack = jnp.zeros((tap_rows * 128,), F32).at[:taps.size].set(taps.reshape(-1)).reshape(tap_rows, 128)
    tap_all = _gather_small(tap_pack, "gather_taps")
    tap_all = tap_all.reshape(N_CHIP, -1)[:, :taps.size].reshape(N_CHIP, n_l, -1)
    ca = tap_all[:, :, :3 * d // N_CHIP].reshape(N_CHIP, n_l, 3, d // N_CHIP)
    cf = tap_all[:, :, 3 * d // N_CHIP:].reshape(N_CHIP, n_l, 3, 2 * ff // N_CHIP)
    conv_a_full = jnp.transpose(ca, (1, 2, 0, 3)).reshape(n_l, 3, d)
    conv_f_full = jnp.transpose(cf, (1, 2, 0, 3)).reshape(n_l, 3, 2 * ff)

    bmaps = jnp.repeat(jnp.swapaxes(b_s, 1, 2), gd, axis=2)

    xs = x.reshape(n, d)
    tgt = loss_target.reshape(n, d)

    def mixer_args(l, w_out_l):
        return (conv_a_full[l], ln_v_g[l][None], ln_v_b[l][None], w_s[l], bmaps[l], w_out_l.reshape(d, d))

    (proj0, h1_0), ((w_out_0, w_up_0),) = _norm_matmul(
        xs, mix_norm_g[0][None], w_in_0, tm=tm_mm, name="fwd_in_proj_0",
        jobs=[_GatherJob([piece(OUT, 0), piece(UP, 0)])])
    (x1_0, cz0), ((w_down_0, w_out_1),) = _mixer_fwd(
        proj0, xs, *mixer_args(0, w_out_0), seq=seq, tm=tm_ew, name="fwd_mixer_0",
        jobs=[_GatherJob([piece(DOWN, 0), piece(OUT, 1)])])
    (up0, h2_0), ((w_in_1,),) = _norm_matmul(
        x1_0, ffn_norm_g[0][None], w_up_0, tm=tm_mm, name="fwd_up_proj_0", jobs=[_GatherJob([piece(IN, 1)])])
    (x2_0, conv0), _ = _ffn_fwd(up0, x1_0, conv_f_full[0], w_down_0.reshape(ff, d), seq=seq, tm=tm_ew, cwid=ff_chunk,
                                name="fwd_ffn_0")
    (proj1, h1_1), ((w_up_1, w_down_1),) = _norm_matmul(
        x2_0, mix_norm_g[1][None], w_in_1, tm=tm_mm, name="fwd_in_proj_1",
        jobs=[_GatherJob([piece(UP, 1), piece(DOWN, 1)])])
    (x1_1, cz1), _ = _mixer_fwd(proj1, x2_0, *mixer_args(1, w_out_1), seq=seq, tm=tm_ew, name="fwd_mixer_1")
    (up1, h2_1), _ = _norm_matmul(x1_1, ffn_norm_g[1][None], w_up_1, tm=tm_mm, name="fwd_up_proj_1")
    (x2_1, conv1), _ = _ffn_fwd(up1, x1_1, conv_f_full[1], w_down_1.reshape(ff, d), seq=seq, tm=tm_ew, cwid=ff_chunk,
                                name="fwd_ffn_1")
    dx, loss_tile, d_final_g = _final_loss(x2_1, final_norm_g[None], tgt, tm=tm_mm, name="final_loss")
    loss = lax.psum(loss_tile[0, 0], axes)

    def chipwise(a):
        return a.reshape(N_CHIP, a.shape[0] // N_CHIP, a.shape[1])

    def pair_sums(grads, recvs, tag):
        return [_pair_sum(g, r, core, f"grad_pair_sum_{tag}_{a}") for a, (g, r) in enumerate(zip(grads, recvs))]

    d_up, dwd, d_cf1 = _ffn_bwd(dx, up1, conv1, conv_f_full[1], w_down_1.reshape(ff, d), seq=seq, tm=tm_ew,
                                cwid=ff_chunk, name="bwd_ffn_1")
    (dx1, d_g2_1), _ = _matmul_bwd_x(d_up, w_up_1, x1_1, ffn_norm_g[1][None], dx, tm=tm_mm, name="bwd_up_x_1")
    (dwu,), _ = _matmul_bwd_w(h2_1, d_up, N_CHIP, tk=tk_w, name="bwd_up_w_1")
    (d_proj, dwo, d_ca1, d_lg1, d_lb1, d_ws1, d_bs1), _ = _mixer_bwd(
        dx1, proj1, cz1, *mixer_args(1, w_out_1), seq=seq, tm=tm_ew, name="bwd_mixer_1")
    (dx, d_g1_1), _ = _matmul_bwd_x(d_proj, w_in_1, x2_0, mix_norm_g[1][None], dx1, tm=tm_mm, name="bwd_in_x_1")
    (dwi,), _ = _matmul_bwd_w(h1_1, d_proj, N_CHIP, tk=tk_w, name="bwd_in_w_1")
    grads1 = [dwi, chipwise(dwo), dwu, chipwise(dwd)]
    small1 = [d_g1_1, d_ca1, d_lg1, d_lb1, d_ws1, d_bs1.T, d_g2_1, d_cf1]

    d_up, dwd, d_cf0 = _ffn_bwd(dx, up0, conv0, conv_f_full[0], w_down_0.reshape(ff, d), seq=seq, tm=tm_ew,
                                cwid=ff_chunk, name="bwd_ffn_0")
    (dx1, d_g2_0), (recv1,) = _matmul_bwd_x(d_up, w_up_0, x1_0, ffn_norm_g[0][None], dx, tm=tm_mm, name="bwd_up_x_0",
                                            jobs=[_SwapJob(grads1)])
    parts1 = pair_sums(grads1, recv1, 1)
    (dwu,), ((arr1_in,),) = _matmul_bwd_w(h2_0, d_up, N_CHIP, tk=tk_w, name="bwd_up_w_0",
                                          jobs=[_ScatterJob([parts1[IN]])])
    ud0 = [dwu, chipwise(dwd)]
    (d_proj, dwo, d_ca0, d_lg0, d_lb0, d_ws0, d_bs0), (arr1_rest, recv0_ud) = _mixer_bwd(
        dx1, proj0, cz0, *mixer_args(0, w_out_0), seq=seq, tm=tm_ew, name="bwd_mixer_0",
        jobs=[_ScatterJob(parts1[OUT:]), _SwapJob(ud0)])
    arrived1 = [arr1_in] + list(arr1_rest)
    gsum = [_final_sum(g, r, p, where, None, 1, n_l, f"grad_final_sum_1_{a}")
            for a, (g, r, p) in enumerate(zip(grads1, recv1, arrived1))]
    parts0_ud = pair_sums(ud0, recv0_ud, "0ud")
    out0 = chipwise(dwo)
    (dwi,), (gsum, arr0_ud, (recv0_out,)) = _matmul_bwd_w(
        h1_0, d_proj, N_CHIP, tk=tk_w, name="bwd_in_w_0",
        jobs=[_ShareJob([(g, 1) for g in gsum]), _ScatterJob(parts0_ud), _SwapJob([out0])])
    part0_out, = pair_sums([out0], [recv0_out], "0o")
    n_mm = n // tm_mm
    n_head = max(n_mm // 4, 1)
    head, ((arr0_out,), (recv0_in,)) = _matmul_bwd_x(
        d_proj, w_in_0, xs, mix_norm_g[0][None], dx1, tm=tm_mm, name="bwd_in_x_0a", tiles=(0, n_head),
        jobs=[_ScatterJob([part0_out]), _SwapJob([dwi])])
    part0_in, = pair_sums([dwi], [recv0_in], "0i")
    (dx, d_g1_0), ((arr0_in,),) = _matmul_bwd_x(
        d_proj, w_in_0, xs, mix_norm_g[0][None], dx1, tm=tm_mm, name="bwd_in_x_0b", tiles=(n_head, n_mm - n_head),
        before=head, jobs=[_ScatterJob([part0_in])])
    grad_x = dx.reshape(bsz, seq, d)
    grads0 = [dwi, out0] + ud0
    recv0 = [recv0_in, recv0_out] + list(recv0_ud)
    arrived0 = [arr0_in, arr0_out] + list(arr0_ud)
    small0 = [d_g1_0, d_ca0, d_lg0, d_lb0, d_ws0, d_bs0.T, d_g2_0, d_cf0]

    flat = [a.reshape(-1) for a in small0 + small1] + [d_final_g.reshape(-1)]
    sizes = [a.size for a in flat]
    total = sum(sizes)
    rows = -(-total // 128 // (2 * HALO)) * 2 * HALO
    pack = jnp.concatenate(flat + [jnp.zeros((rows * 128 - total,), F32)]).reshape(rows, 128)
    red, _ = _all_reduce_small(pack, "small_all_reduce")
    red = red.reshape(-1)
    gsum = [_final_sum(g, r, p, where, prev, 0, n_l, f"grad_final_sum_0_{a}")
            for a, (g, r, p, prev) in enumerate(zip(grads0, recv0, arrived0, gsum))]
    (g_in, g_out, g_up, g_down), = _comm_only([_ShareJob([(g, 0) for g in gsum])], "grad_share_0")

    pieces, off = [], 0
    for s in sizes:
        pieces.append(red[off:off + s])
        off += s
    per_layer = [pieces[l * 8:(l + 1) * 8] for l in range(n_l)]

    def stacked(idx, shape):
        return jnp.stack([per_layer[l][idx].reshape(shape) for l in range(n_l)])

    def my_cols(a, width):
        return lax.dynamic_slice_in_dim(a, chip * width, width, axis=-1)

    g_mix_norm = stacked(0, (d,))
    g_conv_a = my_cols(stacked(1, (3, d)), d // N_CHIP)
    g_ln_g = stacked(2, (d,))
    g_ln_b = stacked(3, (d,))
    g_ws = stacked(4, (groups, CHUNK, CHUNK))
    g_bs = stacked(5, (groups, CHUNK))
    g_ffn_norm = stacked(6, (d,))
    g_conv_f = my_cols(stacked(7, (3, 2 * ff)), 2 * ff // N_CHIP)
    g_final = pieces[-1].reshape(1, d)

    big_names = [(w_in, g_in, m_w_in, v_w_in), (w_out, g_out, m_w_out, v_w_out), (w_up, g_up, m_w_up, v_w_up),
                 (w_down, g_down, m_w_down, v_w_down)]
    upd = [_adamw_big(w, g, m, v, f"adamw_{a}") for a, (w, g, m, v) in enumerate(big_names)]

    sm_w = [mix_norm_g, conv_a_w, ln_v_g, ln_v_b, w_s, b_s, ffn_norm_g, conv_ffn_w, final_norm_g[None]]
    sm_g = [g_mix_norm, g_conv_a, g_ln_g, g_ln_b, g_ws, g_bs, g_ffn_norm, g_conv_f, g_final]
    sm_m = [m_mix_norm_g, m_conv_a_w, m_ln_v_g, m_ln_v_b, m_w_s, m_b_s, m_ffn_norm_g, m_conv_ffn_w, m_final_norm_g[None]]
    sm_v = [v_mix_norm_g, v_conv_a_w, v_ln_v_g, v_ln_v_b, v_w_s, v_b_s, v_ffn_norm_g, v_conv_ffn_w, v_final_norm_g[None]]
    sm_d, sm_nm, sm_nv = _adamw_small(sm_w, sm_g, sm_m, sm_v, "adamw_small")

    def ordered(sm, bigs):
        return [sm[0], bigs[0], sm[1], sm[2], sm[3], sm[4], sm[5], bigs[1], sm[6], bigs[2], sm[7], bigs[3],
                sm[8].reshape(d)]

    out_g = ordered(sm_g, [g_in, g_out, g_up, g_down])
    out_d = ordered(sm_d, [u[0] for u in upd])
    out_m = ordered(sm_nm, [u[1] for u in upd])
    out_v = ordered(sm_nv, [u[2] for u in upd])
    return (loss, grad_x, *out_g, *out_d, *out_m, *out_v)
```

```python
import jax
import jax.numpy as jnp
from jax import lax
from jax.experimental import pallas as pl
from jax.experimental.pallas import tpu as pltpu

F32 = jnp.float32
BF16 = jnp.bfloat16
EPS = 1e-6
CHUNK = 128
N_CHIP = 4
HALO = 8
N_PIECES = 7
LATE_STEPS = 2
VMEM_LIMIT_V7X = 56 * 1024 * 1024
MESH_T = pl.DeviceIdType.MESH
HBM_SPEC = pl.BlockSpec(memory_space=pltpu.HBM)
VMEM_SPEC = pl.BlockSpec(memory_space=pltpu.VMEM)

ADAM_LR, ADAM_B1, ADAM_B2, ADAM_EPS, ADAM_WD, ADAM_STEP = 0.001, 0.9, 0.999, 1e-08, 0.01, 10

NT_DIMS = (((1,), (1,)), ((), ()))
TN_DIMS = (((0,), (0,)), ((), ()))


def _resident(block_shape, index_map):
    return pl.BlockSpec(block_shape, index_map, pipeline_mode=pl.Buffered(1))


def _full(shape):
    return pl.BlockSpec(shape, lambda *_: (0,) * len(shape))


def _row_tile(seq, want):
    t = min(seq, want)
    assert seq % t == 0 and t % CHUNK == 0, (seq, want)
    return t


def _shift_prev(prev8, cur, k):
    ext = jnp.concatenate([prev8, cur], axis=0)
    return pltpu.roll(ext, k, 0)[HALO:]


def _shift_next(cur, next8, k):
    ext = jnp.concatenate([cur, next8], axis=0)
    n = ext.shape[0]
    return pltpu.roll(ext, n - k, 0)[:n - HALO]


def _rowsum(a):
    return jnp.sum(a, axis=0, keepdims=True)


def _lanemean(a):
    return jnp.mean(a, axis=-1, keepdims=True)


def _tril_weights(ws_ref, groups):
    r = lax.broadcasted_iota(jnp.int32, (CHUNK, CHUNK), 0)
    c = lax.broadcasted_iota(jnp.int32, (CHUNK, CHUNK), 1)
    tril = r >= c
    return tril, [jnp.where(tril, ws_ref[g], 0.0) for g in range(groups)]


def _halo_prev(tm):
    return lambda i: jnp.maximum(i * (tm // HALO) - 1, 0)


def _halo_next(tm, n):
    last = n // HALO - 1
    return lambda i: jnp.minimum((i + 1) * (tm // HALO), last)


def _mesh_pos():
    return lax.axis_index("x"), lax.axis_index("y"), lax.axis_index("c")


def _peer_chip(x, y, r):
    return (1 - x if r >> 1 else x), (1 - y if r & 1 else y)


def _remote(src, dst, sems, k, to):
    return pltpu.make_async_remote_copy(src_ref=src, dst_ref=dst, send_sem=sems[0].at[k], recv_sem=sems[1].at[k],
                                        device_id=to, device_id_type=MESH_T)


class _GatherJob:
    def __init__(self, pieces):
        self.pieces = pieces
        self.ins = [p[0] for p in pieces]
        self.out_shapes = [jax.ShapeDtypeStruct((N_CHIP,) + p[0].shape[1:], p[0].dtype) for p in pieces]
        self.aliases = {}
        n = len(pieces)
        self.sems = [pltpu.SemaphoreType.DMA((3 * n,))] * 4 + [pltpu.SemaphoreType.DMA((n,))]

    def _half(self, outs, a, of_chip, core):
        rh = outs[a].shape[1] // 2
        return outs[a].at[of_chip, pl.ds(core * rh, rh), :]

    def _own(self, ins, outs, sems, chip):
        return [pltpu.make_async_copy(ins[a].at[layer], outs[a].at[chip], sems[4].at[a])
                for a, (_, layer) in enumerate(self.pieces)]

    def _sends(self, ins, outs, sems, x, y, c):
        chip, out = 2 * x + y, []
        for a, (_, layer) in enumerate(self.pieces):
            rh = outs[a].shape[1] // 2
            for r in (1, 2, 3):
                px, py = _peer_chip(x, y, r)
                out.append(_remote(ins[a].at[layer, pl.ds(c * rh, rh), :], self._half(outs, a, chip, c), sems[0:2],
                                   3 * a + r - 1, (px, py, c)))
        return out

    def _passes(self, outs, sems, x, y, c, core):
        out = []
        for a in range(len(self.pieces)):
            for r in (1, 2, 3):
                px, py = _peer_chip(x, y, r)
                landed = self._half(outs, a, 2 * px + py, core)
                out.append(_remote(landed, landed, sems[2:4], 3 * a + r - 1, (x, y, 1 - c)))
        return out

    def start(self, ins, outs, sems):
        x, y, c = _mesh_pos()
        for cp in self._own(ins, outs, sems, 2 * x + y) + self._sends(ins, outs, sems, x, y, c):
            cp.start()

    def late(self, ins, outs, sems):
        x, y, c = _mesh_pos()
        passes = self._passes(outs, sems, x, y, c, c)
        k = 0
        for a in range(len(self.pieces)):
            for r in (1, 2, 3):
                px, py = _peer_chip(x, y, r)
                landed = self._half(outs, a, 2 * px + py, c)
                _remote(landed, landed, sems[0:2], k, (px, py, c)).wait_recv()
                passes[k].start()
                k += 1

    def finish(self, ins, outs, sems):
        x, y, c = _mesh_pos()
        for cp in self._passes(outs, sems, x, y, c, 1 - c):
            cp.wait_recv()
        for cp in self._sends(ins, outs, sems, x, y, c) + self._passes(outs, sems, x, y, c, c):
            cp.wait_send()
        for cp in self._own(ins, outs, sems, 2 * x + y):
            cp.wait()


class _SwapJob:
    def __init__(self, pieces):
        self.ins = list(pieces)
        self.out_shapes = [jax.ShapeDtypeStruct((g.shape[0], g.shape[1] // 2, g.shape[2]), g.dtype) for g in pieces]
        self.aliases = {}
        self.sems = [pltpu.SemaphoreType.DMA((len(pieces),))] * 2

    def _copies(self, ins, outs, sems):
        x, y, c = _mesh_pos()
        out = []
        for a in range(len(ins)):
            rh = ins[a].shape[1] // 2
            out.append(_remote(ins[a].at[:, pl.ds((1 - c) * rh, rh), :], outs[a], sems, a, (x, y, 1 - c)))
        return out

    def start(self, ins, outs, sems):
        for cp in self._copies(ins, outs, sems):
            cp.start()

    def finish(self, ins, outs, sems):
        for cp in self._copies(ins, outs, sems):
            cp.wait()


class _ScatterJob:
    def __init__(self, pieces):
        self.ins = list(pieces)
        self.out_shapes = [jax.ShapeDtypeStruct((3,) + p.shape[1:], p.dtype) for p in pieces]
        self.aliases = {}
        self.sems = [pltpu.SemaphoreType.DMA((3 * len(pieces),))] * 2

    def _copies(self, ins, outs, sems):
        x, y, c = _mesh_pos()
        out = []
        for a in range(len(ins)):
            for r in (1, 2, 3):
                px, py = _peer_chip(x, y, r)
                out.append(_remote(ins[a].at[2 * px + py], outs[a].at[r - 1], sems, 3 * a + r - 1, (px, py, c)))
        return out

    def start(self, ins, outs, sems):
        for cp in self._copies(ins, outs, sems):
            cp.start()

    def finish(self, ins, outs, sems):
        for cp in self._copies(ins, outs, sems):
            cp.wait()


class _ShareJob:
    def __init__(self, pieces):
        self.pieces = pieces
        self.ins = [p[0] for p in pieces]
        self.out_shapes = [jax.ShapeDtypeStruct(p[0].shape, p[0].dtype) for p in pieces]
        self.aliases = {a: a for a in range(len(pieces))}
        self.sems = [pltpu.SemaphoreType.DMA((len(pieces),))] * 2

    def _copies(self, outs, sems, core):
        x, y, c = _mesh_pos()
        out = []
        for a, (_, layer) in enumerate(self.pieces):
            rh = outs[a].shape[1] // 2
            rows = outs[a].at[layer, pl.ds(core * rh, rh), :]
            out.append(_remote(rows, rows, sems, a, (x, y, 1 - c)))
        return out

    def start(self, ins, outs, sems):
        for cp in self._copies(outs, sems, lax.axis_index("c")):
            cp.start()

    def finish(self, ins, outs, sems):
        c = lax.axis_index("c")
        for cp in self._copies(outs, sems, c):
            cp.wait_send()
        for cp in self._copies(outs, sems, 1 - c):
            cp.wait_recv()


def _hosted_call(main, *, name, grid, in_specs, out_specs, out_shape, scratch_shapes=(), jobs=(), semantics=None,
                 operands=(), main_aliases=None):
    n_in, n_out, n_sc = len(in_specs), len(out_specs), len(scratch_shapes)
    counts = [(len(j.ins), len(j.out_shapes), len(j.sems)) for j in jobs]
    j_in, j_out, j_sc = (sum(c[k] for c in counts) for k in range(3))
    aliases, i0, o0 = dict(main_aliases or {}), n_in, n_out
    for j, (ci, co, _) in zip(jobs, counts):
        aliases.update({i0 + a: o0 + b for a, b in j.aliases.items()})
        i0, o0 = i0 + ci, o0 + co

    def body(*refs):
        cuts = [0, n_in, n_in + j_in, n_in + j_in + n_out, n_in + j_in + n_out + j_out,
                n_in + j_in + n_out + j_out + n_sc, len(refs)]
        m_in, jb_in, m_out, jb_out, m_sc, jb_sc = (list(refs[cuts[k]:cuts[k + 1]]) for k in range(6))

        def run(phase):
            i0 = o0 = s0 = 0
            for j, (ci, co, cs) in zip(jobs, counts):
                if hasattr(j, phase):
                    getattr(j, phase)(jb_in[i0:i0 + ci], jb_out[o0:o0 + co], jb_sc[s0:s0 + cs])
                i0, o0, s0 = i0 + ci, o0 + co, s0 + cs

        step, n_steps = 0, 1
        for ax in range(len(grid)):
            step, n_steps = step * grid[ax] + pl.program_id(ax), n_steps * grid[ax]
        if jobs:
            pl.when(step == 0)(lambda: run("start"))
        main(m_in, m_out, m_sc)
        if jobs:
            pl.when(step == max(n_steps - 1 - LATE_STEPS, 0))(lambda: run("late"))
            pl.when(step == n_steps - 1)(lambda: run("finish"))

    if semantics is None or jobs:
        semantics = ("arbitrary",) * len(grid)
    outs = pl.pallas_call(
        body, name=name, grid=grid,
        in_specs=list(in_specs) + [HBM_SPEC] * j_in,
        out_specs=list(out_specs) + [HBM_SPEC] * j_out,
        out_shape=list(out_shape) + [s for j in jobs for s in j.out_shapes],
        scratch_shapes=list(scratch_shapes) + [s for j in jobs for s in j.sems],
        input_output_aliases=aliases,
        compiler_params=pltpu.CompilerParams(dimension_semantics=semantics, vmem_limit_bytes=VMEM_LIMIT_V7X),
    )(*operands, *[a for j in jobs for a in j.ins])
    main_outs, rest, job_outs = list(outs[:n_out]), list(outs[n_out:]), []
    for _, co, _ in counts:
        job_outs.append(rest[:co])
        rest = rest[co:]
    return main_outs, job_outs


def _comm_only(jobs, name):
    return _hosted_call(lambda i, o, s: None, name=name, grid=(1,), in_specs=[], out_specs=[], out_shape=[],
                        jobs=jobs)[1]


def _norm_matmul(x, g, wg, *, tm, name, jobs=()):
    n, d = x.shape
    n_chip, _, ck = wg.shape

    def main(ins, outs, _):
        x_ref, g_ref, w_ref = ins
        o_ref, ht_ref = outs
        xf = x_ref[...]
        r = lax.rsqrt(_lanemean(xf * xf) + EPS)
        h = xf * r * g_ref[...]
        ht_ref[...] = h.T.astype(BF16)
        hb = h.astype(BF16)
        for j in range(n_chip):
            o_ref[:, j * ck:(j + 1) * ck] = jnp.dot(hb, w_ref[j], preferred_element_type=F32).astype(BF16)

    return _hosted_call(
        main, name=name, grid=(n // tm,), jobs=jobs, semantics=("parallel",), operands=(x, g, wg),
        in_specs=[pl.BlockSpec((tm, d), lambda i: (i, 0)), _full((1, d)), _resident((n_chip, d, ck), lambda i: (0, 0, 0))],
        out_specs=[pl.BlockSpec((tm, n_chip * ck), lambda i: (i, 0)), pl.BlockSpec((d, tm), lambda i: (0, i))],
        out_shape=[jax.ShapeDtypeStruct((n, n_chip * ck), BF16), jax.ShapeDtypeStruct((d, n), BF16)])


def _mixer_fwd(proj, x, cw, lg, lb, ws, bmap, wout, *, seq, tm, name, jobs=()):
    n, d = x.shape
    groups = ws.shape[0]
    gd = d // groups
    prev = _halo_prev(tm)

    def main(ins, outs, scratch):
        proj_ref, pcg_ref, pxi_ref, x_ref, cw_ref, lg_ref, lb_ref, ws_ref, bmap_ref, wout_ref = ins
        o_ref, cz_ref = outs
        vn_s, mixed_s = scratch
        seq_start = (pl.program_id(0) * tm) % seq == 0

        def piece(k):
            return proj_ref[:, k * d:(k + 1) * d].astype(F32)

        z = piece(1) * piece(2)
        zprev = jnp.where(seq_start, 0.0, pcg_ref[...].astype(F32) * pxi_ref[...].astype(F32))
        cz = (cw_ref[0:1, :] * _shift_prev(zprev, z, 2) + cw_ref[1:2, :] * _shift_prev(zprev, z, 1)
              + cw_ref[2:3, :] * z)
        cz_ref[...] = cz.astype(BF16)
        ya = piece(0) * cz

        v = piece(4)
        xc = v - _lanemean(v)
        vn = xc * lax.rsqrt(_lanemean(xc * xc) + EPS) * lg_ref[...] + lb_ref[...]
        vn_s[...] = vn.astype(BF16)
        _, wsm = _tril_weights(ws_ref, groups)
        for ck in range(tm // CHUNK):
            rows = slice(ck * CHUNK, (ck + 1) * CHUNK)
            for g in range(groups):
                cols = slice(g * gd, (g + 1) * gd)
                mixed_s[rows, cols] = (jnp.dot(wsm[g].astype(BF16), vn_s[rows, cols], preferred_element_type=F32)
                                       + bmap_ref[:, cols])
        yb = piece(3) * mixed_s[...]
        merged = jax.nn.sigmoid(piece(5)) * ya + jax.nn.sigmoid(piece(6)) * yb
        o_ref[...] = x_ref[...] + jnp.dot(merged.astype(BF16), wout_ref[...], preferred_element_type=F32)

    return _hosted_call(
        main, name=name, grid=(n // tm,), jobs=jobs, semantics=("parallel",),
        operands=(proj, proj, proj, x, cw, lg, lb, ws, bmap, wout),
        in_specs=[pl.BlockSpec((tm, N_PIECES * d), lambda i: (i, 0)),
                  pl.BlockSpec((HALO, d), lambda i: (prev(i), 1)),
                  pl.BlockSpec((HALO, d), lambda i: (prev(i), 2)),
                  pl.BlockSpec((tm, d), lambda i: (i, 0)),
                  _full((3, d)), _full((1, d)), _full((1, d)), _full((groups, CHUNK, CHUNK)), _full((CHUNK, d)),
                  _resident((d, d), lambda i: (0, 0))],
        out_specs=[pl.BlockSpec((tm, d), lambda i: (i, 0)), pl.BlockSpec((tm, d), lambda i: (i, 0))],
        out_shape=[jax.ShapeDtypeStruct((n, d), F32), jax.ShapeDtypeStruct((n, d), BF16)],
        scratch_shapes=[pltpu.VMEM((tm, d), BF16), pltpu.VMEM((tm, d), F32)])


def _ffn_fwd(up, x, cw, wd, *, seq, tm, cwid, name, jobs=()):
    n, d = x.shape
    ff = wd.shape[0]
    prev = _halo_prev(tm)

    def main(ins, outs, _):
        up_ref, pup_ref, x_ref, cw_ref, wd_ref = ins
        o_ref, conv_ref = outs
        seq_start = (pl.program_id(0) * tm) % seq == 0

        def conv(lo):
            cols = slice(lo, lo + cwid)
            cur = up_ref[:, cols].astype(F32)
            pre = jnp.where(seq_start, 0.0, pup_ref[:, cols].astype(F32))
            out = (cw_ref[0:1, cols] * _shift_prev(pre, cur, 2) + cw_ref[1:2, cols] * _shift_prev(pre, cur, 1)
                   + cw_ref[2:3, cols] * cur)
            conv_ref[:, cols] = out.astype(BF16)
            return out

        acc = x_ref[...]
        for cj in range(ff // cwid):
            gate = conv(cj * cwid)
            val = conv(ff + cj * cwid)
            a = gate * jax.nn.sigmoid(gate) * val
            acc = acc + jnp.dot(a.astype(BF16), wd_ref[cj * cwid:(cj + 1) * cwid, :], preferred_element_type=F32)
        o_ref[...] = acc

    return _hosted_call(
        main, name=name, grid=(n // tm,), jobs=jobs, semantics=("parallel",), operands=(up, up, x, cw, wd),
        in_specs=[pl.BlockSpec((tm, 2 * ff), lambda i: (i, 0)),
                  pl.BlockSpec((HALO, 2 * ff), lambda i: (prev(i), 0)),
                  pl.BlockSpec((tm, d), lambda i: (i, 0)),
                  _full((3, 2 * ff)),
                  _resident((ff, d), lambda i: (0, 0))],
        out_specs=[pl.BlockSpec((tm, d), lambda i: (i, 0)), pl.BlockSpec((tm, 2 * ff), lambda i: (i, 0))],
        out_shape=[jax.ShapeDtypeStruct((n, d), F32), jax.ShapeDtypeStruct((n, 2 * ff), BF16)])


def _final_loss(x, g, target, *, tm, name):
    n, d = x.shape

    def body(x_ref, g_ref, t_ref, dx_ref, loss_ref, dg_ref):
        @pl.when(pl.program_id(0) == 0)
        def _():
            loss_ref[...] = jnp.zeros_like(loss_ref)
            dg_ref[...] = jnp.zeros_like(dg_ref)

        xf = x_ref[...]
        r = lax.rsqrt(_lanemean(xf * xf) + EPS)
        xhat = xf * r
        diff = xhat * g_ref[...] - t_ref[...]
        loss_ref[...] += (0.5 / d) * _rowsum(jnp.sum(diff * diff, axis=-1, keepdims=True))
        dy = diff * (1.0 / d)
        dg_ref[...] += _rowsum(dy * xhat)
        dyh = dy * g_ref[...]
        dx_ref[...] = r * (dyh - xhat * _lanemean(dyh * xhat))

    return pl.pallas_call(
        body, name=name, grid=(n // tm,),
        in_specs=[pl.BlockSpec((tm, d), lambda i: (i, 0)), _full((1, d)), pl.BlockSpec((tm, d), lambda i: (i, 0))],
        out_specs=[pl.BlockSpec((tm, d), lambda i: (i, 0)), _full((HALO, CHUNK)), _full((1, d))],
        out_shape=[jax.ShapeDtypeStruct((n, d), F32), jax.ShapeDtypeStruct((HALO, CHUNK), F32),
                   jax.ShapeDtypeStruct((1, d), F32)],
        compiler_params=pltpu.CompilerParams(dimension_semantics=("arbitrary",), vmem_limit_bytes=VMEM_LIMIT_V7X),
    )(x, g, target)


def _ffn_bwd(dx, up, conv, cw, wd, *, seq, tm, cwid, name):
    n, d = dx.shape
    ff = wd.shape[0]
    nxt = _halo_next(tm, n)

    def body(dx_ref, dxn_ref, up_ref, conv_ref, nconv_ref, cw_ref, wd_ref, dup_ref, dwd_ref, dcw_ref):
        i = pl.program_id(0)
        keep_next = jnp.where(((i + 1) * tm) % seq == 0, 0.0, 1.0)

        @pl.when(i == 0)
        def _():
            dwd_ref[...] = jnp.zeros_like(dwd_ref)
            dcw_ref[...] = jnp.zeros_like(dcw_ref)

        dxe = jnp.concatenate([dx_ref[...], dxn_ref[...]], axis=0).astype(BF16)
        dxb = dxe[:tm]
        for cj in range(ff // cwid):
            rows = slice(cj * cwid, (cj + 1) * cwid)
            g_cols, v_cols = slice(cj * cwid, (cj + 1) * cwid), slice(ff + cj * cwid, ff + (cj + 1) * cwid)
            dae = lax.dot_general(dxe, wd_ref[rows, :], NT_DIMS, preferred_element_type=F32)
            da, dan = dae[:tm], dae[tm:]

            def grads(gate, val, da_rows):
                sg = jax.nn.sigmoid(gate)
                sl = gate * sg
                return sl, da_rows * val * sg * (1.0 + gate * (1.0 - sg)), da_rows * sl

            gate, val = conv_ref[:, g_cols].astype(F32), conv_ref[:, v_cols].astype(F32)
            sl, d_gate, d_val = grads(gate, val, da)
            dwd_ref[rows, :] += lax.dot_general((sl * val).astype(BF16), dxb, TN_DIMS, preferred_element_type=F32)
            _, d_gate_n, d_val_n = grads(nconv_ref[:, g_cols].astype(F32), nconv_ref[:, v_cols].astype(F32),
                                         dan * keep_next)
            for cols, dcur, dnext in ((g_cols, d_gate, d_gate_n), (v_cols, d_val, d_val_n)):
                upc = up_ref[:, cols].astype(F32)
                d1, d2 = _shift_next(dcur, dnext, 1), _shift_next(dcur, dnext, 2)
                for k, dk in enumerate((d2, d1, dcur)):
                    dcw_ref[k:k + 1, cols] += _rowsum(dk * upc)
                dup = cw_ref[2:3, cols] * dcur + cw_ref[1:2, cols] * d1 + cw_ref[0:1, cols] * d2
                dup_ref[:, cols] = dup.astype(BF16)

    return pl.pallas_call(
        body, name=name, grid=(n // tm,),
        in_specs=[pl.BlockSpec((tm, d), lambda i: (i, 0)),
                  pl.BlockSpec((HALO, d), lambda i: (nxt(i), 0)),
                  pl.BlockSpec((tm, 2 * ff), lambda i: (i, 0)),
                  pl.BlockSpec((tm, 2 * ff), lambda i: (i, 0)),
                  pl.BlockSpec((HALO, 2 * ff), lambda i: (nxt(i), 0)),
                  _full((3, 2 * ff)),
                  _resident((ff, d), lambda i: (0, 0))],
        out_specs=[pl.BlockSpec((tm, 2 * ff), lambda i: (i, 0)), _full((ff, d)), _full((3, 2 * ff))],
        out_shape=[jax.ShapeDtypeStruct((n, 2 * ff), BF16), jax.ShapeDtypeStruct((ff, d), F32),
                   jax.ShapeDtypeStruct((3, 2 * ff), F32)],
        compiler_params=pltpu.CompilerParams(dimension_semantics=("arbitrary",), vmem_limit_bytes=VMEM_LIMIT_V7X),
    )(dx, dx, up, conv, conv, cw, wd)


def _matmul_bwd_x(dy, wg, x, g, dres, *, tm, name, jobs=(), tiles=None, before=None):
    n, d = x.shape
    n_chip, _, ck = wg.shape
    first_tile, n_tiles = tiles or (0, n // tm)
    extra = [] if before is None else list(before)

    def main(ins, outs, _):
        dy_ref, w_ref, x_ref, g_ref, dres_ref = ins[:5]
        dx_ref, dg_ref = outs

        @pl.when(pl.program_id(0) == 0)
        def _():
            dg_ref[...] = jnp.zeros_like(dg_ref) if before is None else ins[6][...]

        dh = lax.dot_general(dy_ref[:, 0:ck], w_ref[0], NT_DIMS, preferred_element_type=F32)
        for j in range(1, n_chip):
            dh = dh + lax.dot_general(dy_ref[:, j * ck:(j + 1) * ck], w_ref[j], NT_DIMS, preferred_element_type=F32)
        xf = x_ref[...]
        r = lax.rsqrt(_lanemean(xf * xf) + EPS)
        xhat = xf * r
        dg_ref[...] += _rowsum(dh * xhat)
        dyh = dh * g_ref[...]
        dx_ref[...] = dres_ref[...] + r * (dyh - xhat * _lanemean(dyh * xhat))

    rows = lambda width: pl.BlockSpec((tm, width), lambda i: (i + first_tile, 0))
    return _hosted_call(
        main, name=name, grid=(n_tiles,), jobs=jobs, semantics=("arbitrary",), operands=(dy, wg, x, g, dres, *extra),
        in_specs=[rows(n_chip * ck), _resident((n_chip, d, ck), lambda i: (0, 0, 0)), rows(d), _full((1, d)), rows(d)]
        + ([] if before is None else [pl.BlockSpec(memory_space=pl.ANY), _full((1, d))]),
        out_specs=[rows(d), _full((1, d))],
        out_shape=[jax.ShapeDtypeStruct((n, d), F32), jax.ShapeDtypeStruct((1, d), F32)],
        main_aliases=None if before is None else {5: 0})


def _matmul_bwd_w(ht, dy, n_chip, *, tk, name, jobs=()):
    d, n = ht.shape
    ck = dy.shape[1] // n_chip

    def main(ins, outs, _):
        ht_ref, dy_ref = ins
        o_ref, = outs

        @pl.when(pl.program_id(1) == 0)
        def _():
            o_ref[...] = jnp.zeros_like(o_ref)

        o_ref[...] += jnp.dot(ht_ref[...], dy_ref[...], preferred_element_type=F32)

    return _hosted_call(
        main, name=name, grid=(n_chip, n // tk), jobs=jobs, semantics=("parallel", "arbitrary"), operands=(ht, dy),
        in_specs=[pl.BlockSpec((d, tk), lambda j, k: (0, k)), pl.BlockSpec((tk, ck), lambda j, k: (k, j))],
        out_specs=[pl.BlockSpec((None, d, ck), lambda j, k: (j, 0, 0))],
        out_shape=[jax.ShapeDtypeStruct((n_chip, d, ck), F32)])


def _mixer_bwd(dx, proj, cz, cw, lg, lb, ws, bmap, wout, *, seq, tm, name, jobs=()):
    n, d = dx.shape
    groups = ws.shape[0]
    gd = d // groups
    nxt = _halo_next(tm, n)
    n_tiles = n // tm

    def main(ins, outs, scratch):
        proj_ref, cz_ref, nbg_ref, nga_ref, dx_ref, dxn_ref, cw_ref, lg_ref, lb_ref, ws_ref, bmap_ref, wout_ref = ins
        dproj_ref, dwout_ref, dcw_ref, dlg_ref, dlb_ref, dws_ref, dbs_ref = outs
        vn_s, mixed_s, dmix_s, dvn_s, dbmap_s = scratch
        i = pl.program_id(0)
        keep_next = jnp.where(((i + 1) * tm) % seq == 0, 0.0, 1.0)

        @pl.when(i == 0)
        def _():
            for ref in (dwout_ref, dcw_ref, dlg_ref, dlb_ref, dws_ref, dbmap_s):
                ref[...] = jnp.zeros_like(ref)

        def piece(k):
            return proj_ref[:, k * d:(k + 1) * d].astype(F32)

        def put(k, val):
            dproj_ref[:, k * d:(k + 1) * d] = val.astype(BF16)

        w = [cw_ref[k:k + 1, :] for k in range(3)]
        cg, xi = piece(1), piece(2)
        cz = cz_ref[...].astype(F32)
        bg = piece(0)
        ya = bg * cz

        v = piece(4)
        xc = v - _lanemean(v)
        rstd = lax.rsqrt(_lanemean(xc * xc) + EPS)
        vhat = xc * rstd
        vn_s[...] = (vhat * lg_ref[...] + lb_ref[...]).astype(BF16)
        tril, wsm = _tril_weights(ws_ref, groups)
        for ck in range(tm // CHUNK):
            rows = slice(ck * CHUNK, (ck + 1) * CHUNK)
            for g in range(groups):
                cols = slice(g * gd, (g + 1) * gd)
                mixed_s[rows, cols] = (jnp.dot(wsm[g].astype(BF16), vn_s[rows, cols], preferred_element_type=F32)
                                       + bmap_ref[:, cols])
        u = piece(3)
        mixed = mixed_s[...]
        yb = u * mixed
        sa, sb = jax.nn.sigmoid(piece(5)), jax.nn.sigmoid(piece(6))
        merged = sa * ya + sb * yb

        dxe = jnp.concatenate([dx_ref[...], dxn_ref[...]], axis=0).astype(BF16)
        dme = lax.dot_general(dxe, wout_ref[...], NT_DIMS, preferred_element_type=F32)
        dm, dm_n = dme[:tm], dme[tm:]
        dwout_ref[...] += lax.dot_general(merged.astype(BF16), dxe[:tm], TN_DIMS, preferred_element_type=F32)

        put(5, dm * ya * sa * (1.0 - sa))
        put(6, dm * yb * sb * (1.0 - sb))
        d_ya, d_yb = dm * sa, dm * sb
        put(0, d_ya * cz)
        d_cz = d_ya * bg
        d_cz_n = dm_n * jax.nn.sigmoid(nga_ref[...].astype(F32)) * nbg_ref[...].astype(F32) * keep_next
        d_cz1, d_cz2 = _shift_next(d_cz, d_cz_n, 1), _shift_next(d_cz, d_cz_n, 2)
        z = cg * xi
        for k, dk in enumerate((d_cz2, d_cz1, d_cz)):
            dcw_ref[k:k + 1, :] += _rowsum(dk * z)
        dz = w[2] * d_cz + w[1] * d_cz1 + w[0] * d_cz2
        put(1, dz * xi)
        put(2, dz * cg)

        put(3, d_yb * mixed)
        d_mixed = d_yb * u
        dmix_s[...] = d_mixed.astype(BF16)
        for ck in range(tm // CHUNK):
            rows = slice(ck * CHUNK, (ck + 1) * CHUNK)
            dbmap_s[...] += d_mixed[rows, :]
            for g in range(groups):
                cols = slice(g * gd, (g + 1) * gd)
                dvn_s[rows, cols] = jnp.dot(wsm[g].T.astype(BF16), dmix_s[rows, cols], preferred_element_type=F32)
                dws_ref[g] += jnp.where(
                    tril, lax.dot_general(dmix_s[rows, cols], vn_s[rows, cols], NT_DIMS, preferred_element_type=F32),
                    0.0)
        d_vn = dvn_s[...]
        dlg_ref[...] += _rowsum(d_vn * vhat)
        dlb_ref[...] += _rowsum(d_vn)
        d_vhat = d_vn * lg_ref[...]
        put(4, rstd * (d_vhat - _lanemean(d_vhat) - vhat * _lanemean(d_vhat * vhat)))

        @pl.when(i == n_tiles - 1)
        def _():
            for g in range(groups):
                dbs_ref[:, g:g + 1] = jnp.sum(dbmap_s[:, g * gd:(g + 1) * gd], axis=-1, keepdims=True)

    return _hosted_call(
        main, name=name, grid=(n_tiles,), jobs=jobs, semantics=("arbitrary",),
        operands=(proj, cz, proj, proj, dx, dx, cw, lg, lb, ws, bmap, wout),
        in_specs=[pl.BlockSpec((tm, N_PIECES * d), lambda i: (i, 0)),
                  pl.BlockSpec((tm, d), lambda i: (i, 0)),
                  pl.BlockSpec((HALO, d), lambda i: (nxt(i), 0)),
                  pl.BlockSpec((HALO, d), lambda i: (nxt(i), 5)),
                  pl.BlockSpec((tm, d), lambda i: (i, 0)),
                  pl.BlockSpec((HALO, d), lambda i: (nxt(i), 0)),
                  _full((3, d)), _full((1, d)), _full((1, d)), _full((groups, CHUNK, CHUNK)), _full((CHUNK, d)),
                  _resident((d, d), lambda i: (0, 0))],
        out_specs=[pl.BlockSpec((tm, N_PIECES * d), lambda i: (i, 0)), _full((d, d)),
                   _full((3, d)), _full((1, d)), _full((1, d)), _full((groups, CHUNK, CHUNK)), _full((CHUNK, groups))],
        out_shape=[jax.ShapeDtypeStruct((n, N_PIECES * d), BF16), jax.ShapeDtypeStruct((d, d), F32),
                   jax.ShapeDtypeStruct((3, d), F32), jax.ShapeDtypeStruct((1, d), F32),
                   jax.ShapeDtypeStruct((1, d), F32), jax.ShapeDtypeStruct((groups, CHUNK, CHUNK), F32),
                   jax.ShapeDtypeStruct((CHUNK, groups), F32)],
        scratch_shapes=[pltpu.VMEM((tm, d), BF16), pltpu.VMEM((tm, d), F32), pltpu.VMEM((tm, d), BF16),
                        pltpu.VMEM((tm, d), F32), pltpu.VMEM((CHUNK, d), F32)])


def _gather_small(pack, name):
    def body(p_ref, o_ref, send_sem, recv_sem):
        x, y, c = _mesh_pos()
        chip = 2 * x + y
        o_ref[chip] = p_ref[...]
        copies = []
        for r in (1, 2, 3):
            px, py = _peer_chip(x, y, r)
            copies.append(_remote(p_ref, o_ref.at[chip], (send_sem, recv_sem), r - 1, (px, py, c)))
        for cp in copies:
            cp.start()
        for r, cp in zip((1, 2, 3), copies):
            px, py = _peer_chip(x, y, r)
            landed = o_ref.at[2 * px + py]
            _remote(landed, landed, (send_sem, recv_sem), r - 1, (px, py, c)).wait_recv()
            cp.wait_send()

    return pl.pallas_call(
        body, name=name, in_specs=[VMEM_SPEC], out_specs=VMEM_SPEC,
        out_shape=jax.ShapeDtypeStruct((N_CHIP,) + pack.shape, pack.dtype),
        scratch_shapes=[pltpu.SemaphoreType.DMA((3,)), pltpu.SemaphoreType.DMA((3,))],
    )(pack)


def _all_reduce_small(pack, name, jobs=()):
    rows = pack.shape[0]
    rh = rows // 2
    assert rh % HALO == 0, rows

    def main(ins, outs, scratch):
        p_ref, = ins
        o_ref, = outs
        sib_buf, chip_buf, got_buf, send_sem, recv_sem = scratch
        sems = (send_sem, recv_sem)
        x, y, c = _mesh_pos()
        chip, sibling = 2 * x + y, (x, y, 1 - c)
        mine = pl.ds(pl.multiple_of(c * rh, HALO), rh)
        theirs = pl.ds(pl.multiple_of((1 - c) * rh, HALO), rh)
        swap = _remote(p_ref.at[theirs], sib_buf, sems, 0, sibling)
        swap.start()
        swap.wait()
        chip_buf[chip] = p_ref[mine, :] + sib_buf[...]
        copies = []
        for r in (1, 2, 3):
            px, py = _peer_chip(x, y, r)
            copies.append(_remote(chip_buf.at[chip], chip_buf.at[chip], sems, r, (px, py, c)))
        for cp in copies:
            cp.start()
        for r, cp in zip((1, 2, 3), copies):
            px, py = _peer_chip(x, y, r)
            landed = chip_buf.at[2 * px + py]
            _remote(landed, landed, sems, r, (px, py, c)).wait_recv()
            cp.wait_send()
        total = ((chip_buf[0] + chip_buf[1]) + chip_buf[2]) + chip_buf[3]
        o_ref[mine, :] = total
        chip_buf[chip] = total
        share = _remote(chip_buf.at[chip], got_buf, sems, 4, sibling)
        share.start()
        share.wait()
        o_ref[theirs, :] = got_buf[...]

    outs, job_outs = _hosted_call(
        main, name=name, grid=(1,), jobs=jobs, operands=(pack,),
        in_specs=[_full(pack.shape)], out_specs=[_full(pack.shape)],
        out_shape=[jax.ShapeDtypeStruct(pack.shape, pack.dtype)],
        scratch_shapes=[pltpu.VMEM((rh, 128), F32), pltpu.VMEM((N_CHIP, rh, 128), F32), pltpu.VMEM((rh, 128), F32),
                        pltpu.SemaphoreType.DMA((5,)), pltpu.SemaphoreType.DMA((5,))])
    return outs[0], job_outs


SUM_BLOCKS = 2


def _pair_sum(grads, recvs, core, name):
    n_t, nb = len(grads), SUM_BLOCKS
    n_chip = grads[0].shape[0]

    def body(core_ref, *refs):
        for t in range(n_t):
            refs[2 * n_t + t][...] = (refs[t][...] + refs[n_t + t][...]).astype(BF16)

    def blk(g):
        rh = g.shape[1] // 2
        assert rh % (16 * nb) == 0, g.shape
        return (None, rh // nb, g.shape[2])

    mine = [pl.BlockSpec(blk(g), lambda k, i, core_ref: (k, core_ref[0] * nb + i, 0)) for g in grads]
    plain = [pl.BlockSpec(blk(g), lambda k, i, core_ref: (k, i, 0)) for g in grads]
    return pl.pallas_call(
        body, name=name,
        grid_spec=pltpu.PrefetchScalarGridSpec(num_scalar_prefetch=1, grid=(n_chip, nb), in_specs=mine + plain,
                                               out_specs=plain),
        out_shape=[jax.ShapeDtypeStruct((n_chip, g.shape[1] // 2, g.shape[2]), BF16) for g in grads],
        compiler_params=pltpu.CompilerParams(dimension_semantics=("parallel", "parallel"),
                                             vmem_limit_bytes=VMEM_LIMIT_V7X),
    )(core, *grads, *recvs)


def _final_sum(grads, recvs, arriveds, where, prevs, layer, n_layers, name):
    n_t, nb = len(grads), SUM_BLOCKS
    extra = [] if prevs is None else list(prevs)

    def body(where_ref, *refs):
        outs = refs[len(refs) - n_t:]
        for t in range(n_t):
            g_ref, r_ref = refs[t], refs[n_t + t]
            a1, a2, a3 = (refs[(2 + s) * n_t + t] for s in range(3))
            own = g_ref[...] + r_ref[...]
            outs[t][...] = ((own + a1[...].astype(F32)) + a2[...].astype(F32)) + a3[...].astype(F32)

    def blk(g):
        return (None, g.shape[1] // 2 // nb, g.shape[2])

    in_specs = ([pl.BlockSpec(blk(g), lambda i, w: (w[1], w[0] * nb + i, 0)) for g in grads]
                + [pl.BlockSpec(blk(g), lambda i, w: (w[1], i, 0)) for g in grads]
                + [pl.BlockSpec(blk(g), lambda i, w, s=s: (s, i, 0)) for s in range(3) for g in grads]
                + [pl.BlockSpec(memory_space=pl.ANY)] * len(extra))
    return pl.pallas_call(
        body, name=name,
        grid_spec=pltpu.PrefetchScalarGridSpec(
            num_scalar_prefetch=1, grid=(nb,), in_specs=in_specs,
            out_specs=[pl.BlockSpec(blk(g), lambda i, w: (layer, w[0] * nb + i, 0)) for g in grads]),
        out_shape=[jax.ShapeDtypeStruct((n_layers,) + g.shape[1:], F32) for g in grads],
        input_output_aliases={1 + 5 * n_t + t: t for t in range(n_t)} if extra else {},
        compiler_params=pltpu.CompilerParams(dimension_semantics=("parallel",), vmem_limit_bytes=VMEM_LIMIT_V7X),
    )(where, *grads, *recvs, *arriveds, *arriveds, *arriveds, *extra)


def _adamw_math(w, g, m, v):
    m = ADAM_B1 * m + (1.0 - ADAM_B1) * g
    v = ADAM_B2 * v + (1.0 - ADAM_B2) * (g * g)
    m_hat = m / (1.0 - ADAM_B1 ** ADAM_STEP)
    v_hat = v / (1.0 - ADAM_B2 ** ADAM_STEP)
    delta = -ADAM_LR * (m_hat / (jnp.sqrt(v_hat) + ADAM_EPS) + ADAM_WD * w)
    return delta, m, v


ADAMW_BLOCKS = 8


def _adamw_big(ws, gs, ms, vs, name):
    n_t, nb = len(ws), ADAMW_BLOCKS
    n_l = ws[0].shape[0]

    def body(*refs):
        for t in range(n_t):
            res = _adamw_math(*(refs[q * n_t + t][...] for q in range(4)))
            for q in range(3):
                refs[(4 + q) * n_t + t][...] = res[q]

    def spec(w):
        assert w.shape[1] % (HALO * nb) == 0, w.shape
        return pl.BlockSpec((None, w.shape[1] // nb, w.shape[2]), lambda l, i: (l, i, 0))

    specs = [spec(w) for w in ws]
    outs = pl.pallas_call(
        body, name=name, grid=(n_l, nb), in_specs=specs * 4, out_specs=specs * 3,
        out_shape=[jax.ShapeDtypeStruct(w.shape, F32) for w in ws] * 3,
        compiler_params=pltpu.CompilerParams(dimension_semantics=("parallel", "parallel"),
                                             vmem_limit_bytes=VMEM_LIMIT_V7X),
    )(*ws, *gs, *ms, *vs)
    return outs[:n_t], outs[n_t:2 * n_t], outs[2 * n_t:]


def _adamw_small(ws, gs, ms, vs, name):
    n_p = len(ws)

    def body(*refs):
        ins, outs = refs[:4 * n_p], refs[4 * n_p:]
        for p in range(n_p):
            res = _adamw_math(ins[p][...], ins[n_p + p][...], ins[2 * n_p + p][...], ins[3 * n_p + p][...])
            for q in range(3):
                outs[q * n_p + p][...] = res[q]

    outs = pl.pallas_call(
        body, name=name, in_specs=[VMEM_SPEC] * (4 * n_p), out_specs=[VMEM_SPEC] * (3 * n_p),
        out_shape=[jax.ShapeDtypeStruct(w.shape, F32) for w in ws] * 3,
        compiler_params=pltpu.CompilerParams(vmem_limit_bytes=VMEM_LIMIT_V7X),
    )(*ws, *gs, *ms, *vs)
    return outs[:n_p], outs[n_p:2 * n_p], outs[2 * n_p:]


def kernel(x, mix_norm_g, w_in, conv_a_w, ln_v_g, ln_v_b, w_s, b_s, w_out, ffn_norm_g, w_up, conv_ffn_w, w_down, final_norm_g, loss_target, m_mix_norm_g, m_w_in, m_conv_a_w, m_ln_v_g, m_ln_v_b, m_w_s, m_b_s, m_w_out, m_ffn_norm_g, m_w_up, m_conv_ffn_w, m_w_down, m_final_norm_g, v_mix_norm_g, v_w_in, v_conv_a_w, v_ln_v_g, v_ln_v_b, v_w_s, v_b_s, v_w_out, v_ffn_norm_g, v_w_up, v_conv_ffn_w, v_w_down, v_final_norm_g):
    bsz, seq, d = x.shape
    n = bsz * seq
    n_l, groups = w_s.shape[0], w_s.shape[1]
    assert n_l == 2, "the exchange schedule below is written for two layers"
    gd = d // groups
    ff = w_down.shape[1] * N_CHIP
    axes = ("x", "y", "c")
    mx, my, mc = _mesh_pos()
    chip = 2 * mx + my
    core = jnp.reshape(mc, (1,)).astype(jnp.int32)
    where = jnp.stack([mc, chip]).astype(jnp.int32)

    tm_mm = _row_tile(seq, 512)
    tm_ew = _row_tile(seq, 256)
    tk_w = _row_tile(seq, 2048)
    ff_chunk = ff // 2 if (ff // 2) % 128 == 0 else ff
    IN, OUT, UP, DOWN = range(4)

    shards = [w.astype(BF16) for w in (w_in, w_out, w_up, w_down)]
    piece = lambda a, l: (shards[a], l)
    (w_in_0,), = _comm_only([_GatherJob([piece(IN, 0)])], "gather_first")

    taps = jnp.concatenate([conv_a_w.reshape(n_l, -1), conv_ffn_w.reshape(n_l, -1)], axis=1)
    tap_rows = -(-taps.size // 128 // 8) * 8
    tap_pack = jnp.zeros((tap_rows * 128,), F32).at[:taps.size].set(taps.reshape(-1)).reshape(tap_rows, 128)
    tap_all = _gather_small(tap_pack, "gather_taps")
    tap_all = tap_all.reshape(N_CHIP, -1)[:, :taps.size].reshape(N_CHIP, n_l, -1)
    ca = tap_all[:, :, :3 * d // N_CHIP].reshape(N_CHIP, n_l, 3, d // N_CHIP)
    cf = tap_all[:, :, 3 * d // N_CHIP:].reshape(N_CHIP, n_l, 3, 2 * ff // N_CHIP)
    conv_a_full = jnp.transpose(ca, (1, 2, 0, 3)).reshape(n_l, 3, d)
    conv_f_full = jnp.transpose(cf, (1, 2, 0, 3)).reshape(n_l, 3, 2 * ff)

    bmaps = jnp.repeat(jnp.swapaxes(b_s, 1, 2), gd, axis=2)

    xs = x.reshape(n, d)
    tgt = loss_target.reshape(n, d)

    def mixer_args(l, w_out_l):
        return (conv_a_full[l], ln_v_g[l][None], ln_v_b[l][None], w_s[l], bmaps[l], w_out_l.reshape(d, d))

    (proj0, h1_0), ((w_out_0, w_up_0),) = _norm_matmul(
        xs, mix_norm_g[0][None], w_in_0, tm=tm_mm, name="fwd_in_proj_0",
        jobs=[_GatherJob([piece(OUT, 0), piece(UP, 0)])])
    (x1_0, cz0), ((w_down_0, w_out_1),) = _mixer_fwd(
        proj0, xs, *mixer_args(0, w_out_0), seq=seq, tm=tm_ew, name="fwd_mixer_0",
        jobs=[_GatherJob([piece(DOWN, 0), piece(OUT, 1)])])
    (up0, h2_0), ((w_in_1,),) = _norm_matmul(
        x1_0, ffn_norm_g[0][None], w_up_0, tm=tm_mm, name="fwd_up_proj_0", jobs=[_GatherJob([piece(IN, 1)])])
    (x2_0, conv0), _ = _ffn_fwd(up0, x1_0, conv_f_full[0], w_down_0.reshape(ff, d), seq=seq, tm=tm_ew, cwid=ff_chunk,
                                name="fwd_ffn_0")
    (proj1, h1_1), ((w_up_1, w_down_1),) = _norm_matmul(
        x2_0, mix_norm_g[1][None], w_in_1, tm=tm_mm, name="fwd_in_proj_1",
        jobs=[_GatherJob([piece(UP, 1), piece(DOWN, 1)])])
    (x1_1, cz1), _ = _mixer_fwd(proj1, x2_0, *mixer_args(1, w_out_1), seq=seq, tm=tm_ew, name="fwd_mixer_1")
    (up1, h2_1), _ = _norm_matmul(x1_1, ffn_norm_g[1][None], w_up_1, tm=tm_mm, name="fwd_up_proj_1")
    (x2_1, conv1), _ = _ffn_fwd(up1, x1_1, conv_f_full[1], w_down_1.reshape(ff, d), seq=seq, tm=tm_ew, cwid=ff_chunk,
                                name="fwd_ffn_1")
    dx, loss_tile, d_final_g = _final_loss(x2_1, final_norm_g[None], tgt, tm=tm_mm, name="final_loss")
    loss = lax.psum(loss_tile[0, 0], axes)

    def chipwise(a):
        return a.reshape(N_CHIP, a.shape[0] // N_CHIP, a.shape[1])

    def pair_sums(grads, recvs, tag):
        return _pair_sum(grads, recvs, core, f"grad_pair_sum_{tag}")

    d_up, dwd, d_cf1 = _ffn_bwd(dx, up1, conv1, conv_f_full[1], w_down_1.reshape(ff, d), seq=seq, tm=tm_ew,
                                cwid=ff_chunk, name="bwd_ffn_1")
    (dx1, d_g2_1), _ = _matmul_bwd_x(d_up, w_up_1, x1_1, ffn_norm_g[1][None], dx, tm=tm_mm, name="bwd_up_x_1")
    (dwu,), _ = _matmul_bwd_w(h2_1, d_up, N_CHIP, tk=tk_w, name="bwd_up_w_1")
    (d_proj, dwo, d_ca1, d_lg1, d_lb1, d_ws1, d_bs1), _ = _mixer_bwd(
        dx1, proj1, cz1, *mixer_args(1, w_out_1), seq=seq, tm=tm_ew, name="bwd_mixer_1")
    (dx, d_g1_1), _ = _matmul_bwd_x(d_proj, w_in_1, x2_0, mix_norm_g[1][None], dx1, tm=tm_mm, name="bwd_in_x_1")
    (dwi,), _ = _matmul_bwd_w(h1_1, d_proj, N_CHIP, tk=tk_w, name="bwd_in_w_1")
    grads1 = [dwi, chipwise(dwo), dwu, chipwise(dwd)]
    small1 = [d_g1_1, d_ca1, d_lg1, d_lb1, d_ws1, d_bs1.T, d_g2_1, d_cf1]

    d_up, dwd, d_cf0 = _ffn_bwd(dx, up0, conv0, conv_f_full[0], w_down_0.reshape(ff, d), seq=seq, tm=tm_ew,
                                cwid=ff_chunk, name="bwd_ffn_0")
    (dx1, d_g2_0), (recv1,) = _matmul_bwd_x(d_up, w_up_0, x1_0, ffn_norm_g[0][None], dx, tm=tm_mm, name="bwd_up_x_0",
                                            jobs=[_SwapJob(grads1)])
    parts1 = pair_sums(grads1, recv1, 1)
    (dwu,), ((arr1_in,),) = _matmul_bwd_w(h2_0, d_up, N_CHIP, tk=tk_w, name="bwd_up_w_0",
                                          jobs=[_ScatterJob([parts1[IN]])])
    ud0 = [dwu, chipwise(dwd)]
    (d_proj, dwo, d_ca0, d_lg0, d_lb0, d_ws0, d_bs0), (arr1_rest, recv0_ud) = _mixer_bwd(
        dx1, proj0, cz0, *mixer_args(0, w_out_0), seq=seq, tm=tm_ew, name="bwd_mixer_0",
        jobs=[_ScatterJob(parts1[OUT:]), _SwapJob(ud0)])
    arrived1 = [arr1_in] + list(arr1_rest)
    gsum = _final_sum(grads1, recv1, arrived1, where, None, 1, n_l, "grad_final_sum_1")
    parts0_ud = pair_sums(ud0, recv0_ud, "0ud")
    out0 = chipwise(dwo)
    (dwi,), (gsum, arr0_ud, (recv0_out,)) = _matmul_bwd_w(
        h1_0, d_proj, N_CHIP, tk=tk_w, name="bwd_in_w_0",
        jobs=[_ShareJob([(g, 1) for g in gsum]), _ScatterJob(parts0_ud), _SwapJob([out0])])
    part0_out, = pair_sums([out0], [recv0_out], "0o")
    n_mm = n // tm_mm
    n_head = max(n_mm // 4, 1)
    head, ((arr0_out,), (recv0_in,)) = _matmul_bwd_x(
        d_proj, w_in_0, xs, mix_norm_g[0][None], dx1, tm=tm_mm, name="bwd_in_x_0a", tiles=(0, n_head),
        jobs=[_ScatterJob([part0_out]), _SwapJob([dwi])])
    part0_in, = pair_sums([dwi], [recv0_in], "0i")
    (dx, d_g1_0), ((arr0_in,),) = _matmul_bwd_x(
        d_proj, w_in_0, xs, mix_norm_g[0][None], dx1, tm=tm_mm, name="bwd_in_x_0b", tiles=(n_head, n_mm - n_head),
        before=head, jobs=[_ScatterJob([part0_in])])
    grad_x = dx.reshape(bsz, seq, d)
    grads0 = [dwi, out0] + ud0
    recv0 = [recv0_in, recv0_out] + list(recv0_ud)
    arrived0 = [arr0_in, arr0_out] + list(arr0_ud)
    small0 = [d_g1_0, d_ca0, d_lg0, d_lb0, d_ws0, d_bs0.T, d_g2_0, d_cf0]

    flat = [a.reshape(-1) for a in small0 + small1] + [d_final_g.reshape(-1)]
    sizes = [a.size for a in flat]
    total = sum(sizes)
    rows = -(-total // 128 // (2 * HALO)) * 2 * HALO
    pack = jnp.concatenate(flat + [jnp.zeros((rows * 128 - total,), F32)]).reshape(rows, 128)
    red, _ = _all_reduce_small(pack, "small_all_reduce")
    red = red.reshape(-1)
    gsum = _final_sum(grads0, recv0, arrived0, where, gsum, 0, n_l, "grad_final_sum_0")
    (g_in, g_out, g_up, g_down), = _comm_only([_ShareJob([(g, 0) for g in gsum])], "grad_share_0")

    pieces, off = [], 0
    for s in sizes:
        pieces.append(red[off:off + s])
        off += s
    per_layer = [pieces[l * 8:(l + 1) * 8] for l in range(n_l)]

    def stacked(idx, shape):
        return jnp.stack([per_layer[l][idx].reshape(shape) for l in range(n_l)])

    def my_cols(a, width):
        return lax.dynamic_slice_in_dim(a, chip * width, width, axis=-1)

    g_mix_norm = stacked(0, (d,))
    g_conv_a = my_cols(stacked(1, (3, d)), d // N_CHIP)
    g_ln_g = stacked(2, (d,))
    g_ln_b = stacked(3, (d,))
    g_ws = stacked(4, (groups, CHUNK, CHUNK))
    g_bs = stacked(5, (groups, CHUNK))
    g_ffn_norm = stacked(6, (d,))
    g_conv_f = my_cols(stacked(7, (3, 2 * ff)), 2 * ff // N_CHIP)
    g_final = pieces[-1].reshape(1, d)

    big_d, big_nm, big_nv = _adamw_big([w_in, w_out, w_up, w_down], [g_in, g_out, g_up, g_down],
                                       [m_w_in, m_w_out, m_w_up, m_w_down], [v_w_in, v_w_out, v_w_up, v_w_down],
                                       "adamw_big")

    sm_w = [mix_norm_g, conv_a_w, ln_v_g, ln_v_b, w_s, b_s, ffn_norm_g, conv_ffn_w, final_norm_g[None]]
    sm_g = [g_mix_norm, g_conv_a, g_ln_g, g_ln_b, g_ws, g_bs, g_ffn_norm, g_conv_f, g_final]
    sm_m = [m_mix_norm_g, m_conv_a_w, m_ln_v_g, m_ln_v_b, m_w_s, m_b_s, m_ffn_norm_g, m_conv_ffn_w, m_final_norm_g[None]]
    sm_v = [v_mix_norm_g, v_conv_a_w, v_ln_v_g, v_ln_v_b, v_w_s, v_b_s, v_ffn_norm_g, v_conv_ffn_w, v_final_norm_g[None]]
    sm_d, sm_nm, sm_nv = _adamw_small(sm_w, sm_g, sm_m, sm_v, "adamw_small")

    def ordered(sm, bigs):
        return [sm[0], bigs[0], sm[1], sm[2], sm[3], sm[4], sm[5], bigs[1], sm[6], bigs[2], sm[7], bigs[3],
                sm[8].reshape(d)]

    out_g = ordered(sm_g, [g_in, g_out, g_up, g_down])
    out_d = ordered(sm_d, big_d)
    out_m = ordered(sm_nm, big_nm)
    out_v = ordered(sm_nv, big_nv)
    return (loss, grad_x, *out_g, *out_d, *out_m, *out_v)
```

```python
import jax
import jax.numpy as jnp
from jax import lax
from jax.experimental import pallas as pl
from jax.experimental.pallas import tpu as pltpu

F32 = jnp.float32
BF16 = jnp.bfloat16
EPS = 1e-6
CHUNK = 128
N_CHIP = 4
HALO = 8
N_PIECES = 7
LATE_STEPS = 2
VMEM_LIMIT_V7X = 56 * 1024 * 1024
MESH_T = pl.DeviceIdType.MESH
HBM_SPEC = pl.BlockSpec(memory_space=pltpu.HBM)
VMEM_SPEC = pl.BlockSpec(memory_space=pltpu.VMEM)

ADAM_LR, ADAM_B1, ADAM_B2, ADAM_EPS, ADAM_WD, ADAM_STEP = 0.001, 0.9, 0.999, 1e-08, 0.01, 10

NT_DIMS = (((1,), (1,)), ((), ()))
TN_DIMS = (((0,), (0,)), ((), ()))


def _resident(block_shape, index_map):
    return pl.BlockSpec(block_shape, index_map, pipeline_mode=pl.Buffered(1))


def _full(shape):
    return pl.BlockSpec(shape, lambda *_: (0,) * len(shape))


def _row_tile(seq, want):
    t = min(seq, want)
    assert seq % t == 0 and t % CHUNK == 0, (seq, want)
    return t


def _shift_prev(prev8, cur, k):
    ext = jnp.concatenate([prev8, cur], axis=0)
    return pltpu.roll(ext, k, 0)[HALO:]


def _shift_next(cur, next8, k):
    ext = jnp.concatenate([cur, next8], axis=0)
    n = ext.shape[0]
    return pltpu.roll(ext, n - k, 0)[:n - HALO]


def _rowsum(a):
    return jnp.sum(a, axis=0, keepdims=True)


def _lanemean(a):
    return jnp.mean(a, axis=-1, keepdims=True)


def _tril_weights(ws_ref, groups):
    r = lax.broadcasted_iota(jnp.int32, (CHUNK, CHUNK), 0)
    c = lax.broadcasted_iota(jnp.int32, (CHUNK, CHUNK), 1)
    tril = r >= c
    return tril, [jnp.where(tril, ws_ref[g], 0.0) for g in range(groups)]


def _halo_prev(tm):
    return lambda i: jnp.maximum(i * (tm // HALO) - 1, 0)


def _halo_next(tm, n):
    last = n // HALO - 1
    return lambda i: jnp.minimum((i + 1) * (tm // HALO), last)


def _mesh_pos():
    return lax.axis_index("x"), lax.axis_index("y"), lax.axis_index("c")


def _peer_chip(x, y, r):
    return (1 - x if r >> 1 else x), (1 - y if r & 1 else y)


def _remote(src, dst, sems, k, to):
    return pltpu.make_async_remote_copy(src_ref=src, dst_ref=dst, send_sem=sems[0].at[k], recv_sem=sems[1].at[k],
                                        device_id=to, device_id_type=MESH_T)


class _GatherJob:
    def __init__(self, pieces):
        self.pieces = pieces
        self.ins = [p[0] for p in pieces]
        self.out_shapes = [jax.ShapeDtypeStruct((N_CHIP,) + p[0].shape[1:], p[0].dtype) for p in pieces]
        self.aliases = {}
        n = len(pieces)
        self.sems = [pltpu.SemaphoreType.DMA((3 * n,))] * 4 + [pltpu.SemaphoreType.DMA((n,))]

    def _half(self, outs, a, of_chip, core):
        rh = outs[a].shape[1] // 2
        return outs[a].at[of_chip, pl.ds(core * rh, rh), :]

    def _own(self, ins, outs, sems, chip):
        return [pltpu.make_async_copy(ins[a].at[layer], outs[a].at[chip], sems[4].at[a])
                for a, (_, layer) in enumerate(self.pieces)]

    def _sends(self, ins, outs, sems, x, y, c):
        chip, out = 2 * x + y, []
        for a, (_, layer) in enumerate(self.pieces):
            rh = outs[a].shape[1] // 2
            for r in (1, 2, 3):
                px, py = _peer_chip(x, y, r)
                out.append(_remote(ins[a].at[layer, pl.ds(c * rh, rh), :], self._half(outs, a, chip, c), sems[0:2],
                                   3 * a + r - 1, (px, py, c)))
        return out

    def _passes(self, outs, sems, x, y, c, core):
        out = []
        for a in range(len(self.pieces)):
            for r in (1, 2, 3):
                px, py = _peer_chip(x, y, r)
                landed = self._half(outs, a, 2 * px + py, core)
                out.append(_remote(landed, landed, sems[2:4], 3 * a + r - 1, (x, y, 1 - c)))
        return out

    def start(self, ins, outs, sems):
        x, y, c = _mesh_pos()
        for cp in self._own(ins, outs, sems, 2 * x + y) + self._sends(ins, outs, sems, x, y, c):
            cp.start()

    def late(self, ins, outs, sems):
        x, y, c = _mesh_pos()
        passes = self._passes(outs, sems, x, y, c, c)
        k = 0
        for a in range(len(self.pieces)):
            for r in (1, 2, 3):
                px, py = _peer_chip(x, y, r)
                landed = self._half(outs, a, 2 * px + py, c)
                _remote(landed, landed, sems[0:2], k, (px, py, c)).wait_recv()
                passes[k].start()
                k += 1

    def finish(self, ins, outs, sems):
        x, y, c = _mesh_pos()
        for cp in self._passes(outs, sems, x, y, c, 1 - c):
            cp.wait_recv()
        for cp in self._sends(ins, outs, sems, x, y, c) + self._passes(outs, sems, x, y, c, c):
            cp.wait_send()
        for cp in self._own(ins, outs, sems, 2 * x + y):
            cp.wait()


class _SmallGatherJob:
    def __init__(self, pieces):
        self.ins = list(pieces)
        self.out_shapes = [jax.ShapeDtypeStruct((N_CHIP,) + p.shape, p.dtype) for p in pieces]
        self.aliases = {}
        n = len(pieces)
        self.sems = [pltpu.SemaphoreType.DMA((3 * n,))] * 2 + [pltpu.SemaphoreType.DMA((n,))]

    def _outgoing(self, ins, outs, sems):
        x, y, c = _mesh_pos()
        mine = [outs[a].at[2 * x + y] for a in range(len(ins))]
        own = [pltpu.make_async_copy(ins[a], mine[a], sems[2].at[a]) for a in range(len(ins))]
        sends = [_remote(ins[a], mine[a], sems[0:2], 3 * a + r - 1, (*_peer_chip(x, y, r), c))
                 for a in range(len(ins)) for r in (1, 2, 3)]
        return own, sends

    def start(self, ins, outs, sems):
        own, sends = self._outgoing(ins, outs, sems)
        for cp in own + sends:
            cp.start()

    def finish(self, ins, outs, sems):
        x, y, c = _mesh_pos()
        for a in range(len(ins)):
            for r in (1, 2, 3):
                px, py = _peer_chip(x, y, r)
                landed = outs[a].at[2 * px + py]
                _remote(landed, landed, sems[0:2], 3 * a + r - 1, (px, py, c)).wait_recv()
        own, sends = self._outgoing(ins, outs, sems)
        for cp in sends:
            cp.wait_send()
        for cp in own:
            cp.wait()


class _SwapJob:
    def __init__(self, pieces):
        self.ins = list(pieces)
        self.out_shapes = [jax.ShapeDtypeStruct((g.shape[0], g.shape[1] // 2, g.shape[2]), g.dtype) for g in pieces]
        self.aliases = {}
        self.sems = [pltpu.SemaphoreType.DMA((len(pieces),))] * 2

    def _copies(self, ins, outs, sems):
        x, y, c = _mesh_pos()
        out = []
        for a in range(len(ins)):
            rh = ins[a].shape[1] // 2
            out.append(_remote(ins[a].at[:, pl.ds((1 - c) * rh, rh), :], outs[a], sems, a, (x, y, 1 - c)))
        return out

    def start(self, ins, outs, sems):
        for cp in self._copies(ins, outs, sems):
            cp.start()

    def finish(self, ins, outs, sems):
        for cp in self._copies(ins, outs, sems):
            cp.wait()


class _ScatterJob:
    def __init__(self, pieces):
        self.ins = list(pieces)
        self.out_shapes = [jax.ShapeDtypeStruct((3,) + p.shape[1:], p.dtype) for p in pieces]
        self.aliases = {}
        self.sems = [pltpu.SemaphoreType.DMA((3 * len(pieces),))] * 2

    def _copies(self, ins, outs, sems):
        x, y, c = _mesh_pos()
        out = []
        for a in range(len(ins)):
            for r in (1, 2, 3):
                px, py = _peer_chip(x, y, r)
                out.append(_remote(ins[a].at[2 * px + py], outs[a].at[r - 1], sems, 3 * a + r - 1, (px, py, c)))
        return out

    def start(self, ins, outs, sems):
        for cp in self._copies(ins, outs, sems):
            cp.start()

    def finish(self, ins, outs, sems):
        for cp in self._copies(ins, outs, sems):
            cp.wait()


class _ShareJob:
    def __init__(self, pieces):
        self.pieces = pieces
        self.ins = [p[0] for p in pieces]
        self.out_shapes = [jax.ShapeDtypeStruct(p[0].shape, p[0].dtype) for p in pieces]
        self.aliases = {a: a for a in range(len(pieces))}
        self.sems = [pltpu.SemaphoreType.DMA((len(pieces),))] * 2

    def _copies(self, outs, sems, core):
        x, y, c = _mesh_pos()
        out = []
        for a, (_, layer) in enumerate(self.pieces):
            rh = outs[a].shape[1] // 2
            rows = outs[a].at[layer, pl.ds(core * rh, rh), :]
            out.append(_remote(rows, rows, sems, a, (x, y, 1 - c)))
        return out

    def start(self, ins, outs, sems):
        for cp in self._copies(outs, sems, lax.axis_index("c")):
            cp.start()

    def finish(self, ins, outs, sems):
        c = lax.axis_index("c")
        for cp in self._copies(outs, sems, c):
            cp.wait_send()
        for cp in self._copies(outs, sems, 1 - c):
            cp.wait_recv()


def _hosted_call(main, *, name, grid, in_specs, out_specs, out_shape, scratch_shapes=(), jobs=(), semantics=None,
                 operands=(), main_aliases=None):
    n_in, n_out, n_sc = len(in_specs), len(out_specs), len(scratch_shapes)
    counts = [(len(j.ins), len(j.out_shapes), len(j.sems)) for j in jobs]
    j_in, j_out, j_sc = (sum(c[k] for c in counts) for k in range(3))
    aliases, i0, o0 = dict(main_aliases or {}), n_in, n_out
    for j, (ci, co, _) in zip(jobs, counts):
        aliases.update({i0 + a: o0 + b for a, b in j.aliases.items()})
        i0, o0 = i0 + ci, o0 + co

    def body(*refs):
        cuts = [0, n_in, n_in + j_in, n_in + j_in + n_out, n_in + j_in + n_out + j_out,
                n_in + j_in + n_out + j_out + n_sc, len(refs)]
        m_in, jb_in, m_out, jb_out, m_sc, jb_sc = (list(refs[cuts[k]:cuts[k + 1]]) for k in range(6))

        def run(phase):
            i0 = o0 = s0 = 0
            for j, (ci, co, cs) in zip(jobs, counts):
                if hasattr(j, phase):
                    getattr(j, phase)(jb_in[i0:i0 + ci], jb_out[o0:o0 + co], jb_sc[s0:s0 + cs])
                i0, o0, s0 = i0 + ci, o0 + co, s0 + cs

        step, n_steps = 0, 1
        for ax in range(len(grid)):
            step, n_steps = step * grid[ax] + pl.program_id(ax), n_steps * grid[ax]
        if jobs:
            pl.when(step == 0)(lambda: run("start"))
        main(m_in, m_out, m_sc)
        if jobs:
            pl.when(step == max(n_steps - 1 - LATE_STEPS, 0))(lambda: run("late"))
            pl.when(step == n_steps - 1)(lambda: run("finish"))

    if semantics is None or jobs:
        semantics = ("arbitrary",) * len(grid)
    outs = pl.pallas_call(
        body, name=name, grid=grid,
        in_specs=list(in_specs) + [HBM_SPEC] * j_in,
        out_specs=list(out_specs) + [HBM_SPEC] * j_out,
        out_shape=list(out_shape) + [s for j in jobs for s in j.out_shapes],
        scratch_shapes=list(scratch_shapes) + [s for j in jobs for s in j.sems],
        input_output_aliases=aliases,
        compiler_params=pltpu.CompilerParams(dimension_semantics=semantics, vmem_limit_bytes=VMEM_LIMIT_V7X),
    )(*operands, *[a for j in jobs for a in j.ins])
    main_outs, rest, job_outs = list(outs[:n_out]), list(outs[n_out:]), []
    for _, co, _ in counts:
        job_outs.append(rest[:co])
        rest = rest[co:]
    return main_outs, job_outs


def _comm_only(jobs, name):
    return _hosted_call(lambda i, o, s: None, name=name, grid=(1,), in_specs=[], out_specs=[], out_shape=[],
                        jobs=jobs)[1]


def _norm_matmul(x, g, wg, *, tm, name, jobs=()):
    n, d = x.shape
    n_chip, _, ck = wg.shape

    def main(ins, outs, _):
        x_ref, g_ref, w_ref = ins
        o_ref, ht_ref = outs
        xf = x_ref[...]
        r = lax.rsqrt(_lanemean(xf * xf) + EPS)
        h = xf * r * g_ref[...]
        ht_ref[...] = h.T.astype(BF16)
        hb = h.astype(BF16)
        for j in range(n_chip):
            o_ref[:, j * ck:(j + 1) * ck] = jnp.dot(hb, w_ref[j], preferred_element_type=F32).astype(BF16)

    return _hosted_call(
        main, name=name, grid=(n // tm,), jobs=jobs, semantics=("parallel",), operands=(x, g, wg),
        in_specs=[pl.BlockSpec((tm, d), lambda i: (i, 0)), _full((1, d)), _resident((n_chip, d, ck), lambda i: (0, 0, 0))],
        out_specs=[pl.BlockSpec((tm, n_chip * ck), lambda i: (i, 0)), pl.BlockSpec((d, tm), lambda i: (0, i))],
        out_shape=[jax.ShapeDtypeStruct((n, n_chip * ck), BF16), jax.ShapeDtypeStruct((d, n), BF16)])


def _mixer_fwd(proj, x, cw, lg, lb, ws, bmap, wout, *, seq, tm, name, jobs=()):
    n, d = x.shape
    groups = ws.shape[0]
    gd = d // groups
    prev = _halo_prev(tm)

    def main(ins, outs, scratch):
        proj_ref, pcg_ref, pxi_ref, x_ref, cw_ref, lg_ref, lb_ref, ws_ref, bmap_ref, wout_ref = ins
        o_ref, cz_ref = outs
        vn_s, mixed_s = scratch
        seq_start = (pl.program_id(0) * tm) % seq == 0

        def piece(k):
            return proj_ref[:, k * d:(k + 1) * d].astype(F32)

        z = piece(1) * piece(2)
        zprev = jnp.where(seq_start, 0.0, pcg_ref[...].astype(F32) * pxi_ref[...].astype(F32))
        cz = (cw_ref[0:1, :] * _shift_prev(zprev, z, 2) + cw_ref[1:2, :] * _shift_prev(zprev, z, 1)
              + cw_ref[2:3, :] * z)
        cz_ref[...] = cz.astype(BF16)
        ya = piece(0) * cz

        v = piece(4)
        xc = v - _lanemean(v)
        vn = xc * lax.rsqrt(_lanemean(xc * xc) + EPS) * lg_ref[...] + lb_ref[...]
        vn_s[...] = vn.astype(BF16)
        _, wsm = _tril_weights(ws_ref, groups)
        for ck in range(tm // CHUNK):
            rows = slice(ck * CHUNK, (ck + 1) * CHUNK)
            for g in range(groups):
                cols = slice(g * gd, (g + 1) * gd)
                mixed_s[rows, cols] = (jnp.dot(wsm[g].astype(BF16), vn_s[rows, cols], preferred_element_type=F32)
                                       + bmap_ref[:, cols])
        yb = piece(3) * mixed_s[...]
        merged = jax.nn.sigmoid(piece(5)) * ya + jax.nn.sigmoid(piece(6)) * yb
        o_ref[...] = x_ref[...] + jnp.dot(merged.astype(BF16), wout_ref[...], preferred_element_type=F32)

    return _hosted_call(
        main, name=name, grid=(n // tm,), jobs=jobs, semantics=("parallel",),
        operands=(proj, proj, proj, x, cw, lg, lb, ws, bmap, wout),
        in_specs=[pl.BlockSpec((tm, N_PIECES * d), lambda i: (i, 0)),
                  pl.BlockSpec((HALO, d), lambda i: (prev(i), 1)),
                  pl.BlockSpec((HALO, d), lambda i: (prev(i), 2)),
                  pl.BlockSpec((tm, d), lambda i: (i, 0)),
                  _full((3, d)), _full((1, d)), _full((1, d)), _full((groups, CHUNK, CHUNK)), _full((CHUNK, d)),
                  _resident((d, d), lambda i: (0, 0))],
        out_specs=[pl.BlockSpec((tm, d), lambda i: (i, 0)), pl.BlockSpec((tm, d), lambda i: (i, 0))],
        out_shape=[jax.ShapeDtypeStruct((n, d), F32), jax.ShapeDtypeStruct((n, d), BF16)],
        scratch_shapes=[pltpu.VMEM((tm, d), BF16), pltpu.VMEM((tm, d), F32)])


def _ffn_fwd(up, x, cw, wd, *, seq, tm, cwid, name, jobs=()):
    n, d = x.shape
    ff = wd.shape[0]
    prev = _halo_prev(tm)

    def main(ins, outs, _):
        up_ref, pup_ref, x_ref, cw_ref, wd_ref = ins
        o_ref, conv_ref = outs
        seq_start = (pl.program_id(0) * tm) % seq == 0

        def conv(lo):
            cols = slice(lo, lo + cwid)
            cur = up_ref[:, cols].astype(F32)
            pre = jnp.where(seq_start, 0.0, pup_ref[:, cols].astype(F32))
            out = (cw_ref[0:1, cols] * _shift_prev(pre, cur, 2) + cw_ref[1:2, cols] * _shift_prev(pre, cur, 1)
                   + cw_ref[2:3, cols] * cur)
            conv_ref[:, cols] = out.astype(BF16)
            return out

        acc = x_ref[...]
        for cj in range(ff // cwid):
            gate = conv(cj * cwid)
            val = conv(ff + cj * cwid)
            a = gate * jax.nn.sigmoid(gate) * val
            acc = acc + jnp.dot(a.astype(BF16), wd_ref[cj * cwid:(cj + 1) * cwid, :], preferred_element_type=F32)
        o_ref[...] = acc

    return _hosted_call(
        main, name=name, grid=(n // tm,), jobs=jobs, semantics=("parallel",), operands=(up, up, x, cw, wd),
        in_specs=[pl.BlockSpec((tm, 2 * ff), lambda i: (i, 0)),
                  pl.BlockSpec((HALO, 2 * ff), lambda i: (prev(i), 0)),
                  pl.BlockSpec((tm, d), lambda i: (i, 0)),
                  _full((3, 2 * ff)),
                  _resident((ff, d), lambda i: (0, 0))],
        out_specs=[pl.BlockSpec((tm, d), lambda i: (i, 0)), pl.BlockSpec((tm, 2 * ff), lambda i: (i, 0))],
        out_shape=[jax.ShapeDtypeStruct((n, d), F32), jax.ShapeDtypeStruct((n, 2 * ff), BF16)])


def _final_loss(x, g, target, *, tm, name):
    n, d = x.shape

    def body(x_ref, g_ref, t_ref, dx_ref, loss_ref, dg_ref):
        @pl.when(pl.program_id(0) == 0)
        def _():
            loss_ref[...] = jnp.zeros_like(loss_ref)
            dg_ref[...] = jnp.zeros_like(dg_ref)

        xf = x_ref[...]
        r = lax.rsqrt(_lanemean(xf * xf) + EPS)
        xhat = xf * r
        diff = xhat * g_ref[...] - t_ref[...]
        loss_ref[...] += (0.5 / d) * _rowsum(jnp.sum(diff * diff, axis=-1, keepdims=True))
        dy = diff * (1.0 / d)
        dg_ref[...] += _rowsum(dy * xhat)
        dyh = dy * g_ref[...]
        dx_ref[...] = r * (dyh - xhat * _lanemean(dyh * xhat))

    return pl.pallas_call(
        body, name=name, grid=(n // tm,),
        in_specs=[pl.BlockSpec((tm, d), lambda i: (i, 0)), _full((1, d)), pl.BlockSpec((tm, d), lambda i: (i, 0))],
        out_specs=[pl.BlockSpec((tm, d), lambda i: (i, 0)), _full((HALO, CHUNK)), _full((1, d))],
        out_shape=[jax.ShapeDtypeStruct((n, d), F32), jax.ShapeDtypeStruct((HALO, CHUNK), F32),
                   jax.ShapeDtypeStruct((1, d), F32)],
        compiler_params=pltpu.CompilerParams(dimension_semantics=("arbitrary",), vmem_limit_bytes=VMEM_LIMIT_V7X),
    )(x, g, target)


def _ffn_bwd(dx, up, conv, cw, wd, *, seq, tm, cwid, name):
    n, d = dx.shape
    ff = wd.shape[0]
    nxt = _halo_next(tm, n)

    def body(dx_ref, dxn_ref, up_ref, conv_ref, nconv_ref, cw_ref, wd_ref, dup_ref, dwd_ref, dcw_ref):
        i = pl.program_id(0)
        keep_next = jnp.where(((i + 1) * tm) % seq == 0, 0.0, 1.0)

        @pl.when(i == 0)
        def _():
            dwd_ref[...] = jnp.zeros_like(dwd_ref)
            dcw_ref[...] = jnp.zeros_like(dcw_ref)

        dxe = jnp.concatenate([dx_ref[...], dxn_ref[...]], axis=0).astype(BF16)
        dxb = dxe[:tm]
        for cj in range(ff // cwid):
            rows = slice(cj * cwid, (cj + 1) * cwid)
            g_cols, v_cols = slice(cj * cwid, (cj + 1) * cwid), slice(ff + cj * cwid, ff + (cj + 1) * cwid)
            dae = lax.dot_general(dxe, wd_ref[rows, :], NT_DIMS, preferred_element_type=F32)
            da, dan = dae[:tm], dae[tm:]

            def grads(gate, val, da_rows):
                sg = jax.nn.sigmoid(gate)
                sl = gate * sg
                return sl, da_rows * val * sg * (1.0 + gate * (1.0 - sg)), da_rows * sl

            gate, val = conv_ref[:, g_cols].astype(F32), conv_ref[:, v_cols].astype(F32)
            sl, d_gate, d_val = grads(gate, val, da)
            dwd_ref[rows, :] += lax.dot_general((sl * val).astype(BF16), dxb, TN_DIMS, preferred_element_type=F32)
            _, d_gate_n, d_val_n = grads(nconv_ref[:, g_cols].astype(F32), nconv_ref[:, v_cols].astype(F32),
                                         dan * keep_next)
            for cols, dcur, dnext in ((g_cols, d_gate, d_gate_n), (v_cols, d_val, d_val_n)):
                upc = up_ref[:, cols].astype(F32)
                d1, d2 = _shift_next(dcur, dnext, 1), _shift_next(dcur, dnext, 2)
                for k, dk in enumerate((d2, d1, dcur)):
                    dcw_ref[k:k + 1, cols] += _rowsum(dk * upc)
                dup = cw_ref[2:3, cols] * dcur + cw_ref[1:2, cols] * d1 + cw_ref[0:1, cols] * d2
                dup_ref[:, cols] = dup.astype(BF16)

    return pl.pallas_call(
        body, name=name, grid=(n // tm,),
        in_specs=[pl.BlockSpec((tm, d), lambda i: (i, 0)),
                  pl.BlockSpec((HALO, d), lambda i: (nxt(i), 0)),
                  pl.BlockSpec((tm, 2 * ff), lambda i: (i, 0)),
                  pl.BlockSpec((tm, 2 * ff), lambda i: (i, 0)),
                  pl.BlockSpec((HALO, 2 * ff), lambda i: (nxt(i), 0)),
                  _full((3, 2 * ff)),
                  _resident((ff, d), lambda i: (0, 0))],
        out_specs=[pl.BlockSpec((tm, 2 * ff), lambda i: (i, 0)), _full((ff, d)), _full((3, 2 * ff))],
        out_shape=[jax.ShapeDtypeStruct((n, 2 * ff), BF16), jax.ShapeDtypeStruct((ff, d), F32),
                   jax.ShapeDtypeStruct((3, 2 * ff), F32)],
        compiler_params=pltpu.CompilerParams(dimension_semantics=("arbitrary",), vmem_limit_bytes=VMEM_LIMIT_V7X),
    )(dx, dx, up, conv, conv, cw, wd)


def _matmul_bwd_x(dy, wg, x, g, dres, *, tm, name, jobs=(), tiles=None, before=None):
    n, d = x.shape
    n_chip, _, ck = wg.shape
    first_tile, n_tiles = tiles or (0, n // tm)
    extra = [] if before is None else list(before)

    def main(ins, outs, _):
        dy_ref, w_ref, x_ref, g_ref, dres_ref = ins[:5]
        dx_ref, dg_ref = outs

        @pl.when(pl.program_id(0) == 0)
        def _():
            dg_ref[...] = jnp.zeros_like(dg_ref) if before is None else ins[6][...]

        dh = lax.dot_general(dy_ref[:, 0:ck], w_ref[0], NT_DIMS, preferred_element_type=F32)
        for j in range(1, n_chip):
            dh = dh + lax.dot_general(dy_ref[:, j * ck:(j + 1) * ck], w_ref[j], NT_DIMS, preferred_element_type=F32)
        xf = x_ref[...]
        r = lax.rsqrt(_lanemean(xf * xf) + EPS)
        xhat = xf * r
        dg_ref[...] += _rowsum(dh * xhat)
        dyh = dh * g_ref[...]
        dx_ref[...] = dres_ref[...] + r * (dyh - xhat * _lanemean(dyh * xhat))

    rows = lambda width: pl.BlockSpec((tm, width), lambda i: (i + first_tile, 0))
    return _hosted_call(
        main, name=name, grid=(n_tiles,), jobs=jobs, semantics=("arbitrary",), operands=(dy, wg, x, g, dres, *extra),
        in_specs=[rows(n_chip * ck), _resident((n_chip, d, ck), lambda i: (0, 0, 0)), rows(d), _full((1, d)), rows(d)]
        + ([] if before is None else [pl.BlockSpec(memory_space=pl.ANY), _full((1, d))]),
        out_specs=[rows(d), _full((1, d))],
        out_shape=[jax.ShapeDtypeStruct((n, d), F32), jax.ShapeDtypeStruct((1, d), F32)],
        main_aliases=None if before is None else {5: 0})


def _matmul_bwd_w(ht, dy, n_chip, *, tk, name, jobs=()):
    d, n = ht.shape
    ck = dy.shape[1] // n_chip

    def main(ins, outs, _):
        ht_ref, dy_ref = ins
        o_ref, = outs

        @pl.when(pl.program_id(1) == 0)
        def _():
            o_ref[...] = jnp.zeros_like(o_ref)

        o_ref[...] += jnp.dot(ht_ref[...], dy_ref[...], preferred_element_type=F32)

    return _hosted_call(
        main, name=name, grid=(n_chip, n // tk), jobs=jobs, semantics=("parallel", "arbitrary"), operands=(ht, dy),
        in_specs=[pl.BlockSpec((d, tk), lambda j, k: (0, k)), pl.BlockSpec((tk, ck), lambda j, k: (k, j))],
        out_specs=[pl.BlockSpec((None, d, ck), lambda j, k: (j, 0, 0))],
        out_shape=[jax.ShapeDtypeStruct((n_chip, d, ck), F32)])


def _mixer_bwd(dx, proj, cz, cw, lg, lb, ws, bmap, wout, *, seq, tm, name, jobs=()):
    n, d = dx.shape
    groups = ws.shape[0]
    gd = d // groups
    nxt = _halo_next(tm, n)
    n_tiles = n // tm

    def main(ins, outs, scratch):
        proj_ref, cz_ref, nbg_ref, nga_ref, dx_ref, dxn_ref, cw_ref, lg_ref, lb_ref, ws_ref, bmap_ref, wout_ref = ins
        dproj_ref, dwout_ref, dcw_ref, dlg_ref, dlb_ref, dws_ref, dbs_ref = outs
        vn_s, mixed_s, dmix_s, dvn_s, dbmap_s = scratch
        i = pl.program_id(0)
        keep_next = jnp.where(((i + 1) * tm) % seq == 0, 0.0, 1.0)

        @pl.when(i == 0)
        def _():
            for ref in (dwout_ref, dcw_ref, dlg_ref, dlb_ref, dws_ref, dbmap_s):
                ref[...] = jnp.zeros_like(ref)

        def piece(k):
            return proj_ref[:, k * d:(k + 1) * d].astype(F32)

        def put(k, val):
            dproj_ref[:, k * d:(k + 1) * d] = val.astype(BF16)

        w = [cw_ref[k:k + 1, :] for k in range(3)]
        cg, xi = piece(1), piece(2)
        cz = cz_ref[...].astype(F32)
        bg = piece(0)
        ya = bg * cz

        v = piece(4)
        xc = v - _lanemean(v)
        rstd = lax.rsqrt(_lanemean(xc * xc) + EPS)
        vhat = xc * rstd
        vn_s[...] = (vhat * lg_ref[...] + lb_ref[...]).astype(BF16)
        tril, wsm = _tril_weights(ws_ref, groups)
        for ck in range(tm // CHUNK):
            rows = slice(ck * CHUNK, (ck + 1) * CHUNK)
            for g in range(groups):
                cols = slice(g * gd, (g + 1) * gd)
                mixed_s[rows, cols] = (jnp.dot(wsm[g].astype(BF16), vn_s[rows, cols], preferred_element_type=F32)
                                       + bmap_ref[:, cols])
        u = piece(3)
        mixed = mixed_s[...]
        yb = u * mixed
        sa, sb = jax.nn.sigmoid(piece(5)), jax.nn.sigmoid(piece(6))
        merged = sa * ya + sb * yb

        dxe = jnp.concatenate([dx_ref[...], dxn_ref[...]], axis=0).astype(BF16)
        dme = lax.dot_general(dxe, wout_ref[...], NT_DIMS, preferred_element_type=F32)
        dm, dm_n = dme[:tm], dme[tm:]
        dwout_ref[...] += lax.dot_general(merged.astype(BF16), dxe[:tm], TN_DIMS, preferred_element_type=F32)

        put(5, dm * ya * sa * (1.0 - sa))
        put(6, dm * yb * sb * (1.0 - sb))
        d_ya, d_yb = dm * sa, dm * sb
        put(0, d_ya * cz)
        d_cz = d_ya * bg
        d_cz_n = dm_n * jax.nn.sigmoid(nga_ref[...].astype(F32)) * nbg_ref[...].astype(F32) * keep_next
        d_cz1, d_cz2 = _shift_next(d_cz, d_cz_n, 1), _shift_next(d_cz, d_cz_n, 2)
        z = cg * xi
        for k, dk in enumerate((d_cz2, d_cz1, d_cz)):
            dcw_ref[k:k + 1, :] += _rowsum(dk * z)
        dz = w[2] * d_cz + w[1] * d_cz1 + w[0] * d_cz2
        put(1, dz * xi)
        put(2, dz * cg)

        put(3, d_yb * mixed)
        d_mixed = d_yb * u
        dmix_s[...] = d_mixed.astype(BF16)
        for ck in range(tm // CHUNK):
            rows = slice(ck * CHUNK, (ck + 1) * CHUNK)
            dbmap_s[...] += d_mixed[rows, :]
            for g in range(groups):
                cols = slice(g * gd, (g + 1) * gd)
                dvn_s[rows, cols] = jnp.dot(wsm[g].T.astype(BF16), dmix_s[rows, cols], preferred_element_type=F32)
                dws_ref[g] += jnp.where(
                    tril, lax.dot_general(dmix_s[rows, cols], vn_s[rows, cols], NT_DIMS, preferred_element_type=F32),
                    0.0)
        d_vn = dvn_s[...]
        dlg_ref[...] += _rowsum(d_vn * vhat)
        dlb_ref[...] += _rowsum(d_vn)
        d_vhat = d_vn * lg_ref[...]
        put(4, rstd * (d_vhat - _lanemean(d_vhat) - vhat * _lanemean(d_vhat * vhat)))

        @pl.when(i == n_tiles - 1)
        def _():
            for g in range(groups):
                dbs_ref[:, g:g + 1] = jnp.sum(dbmap_s[:, g * gd:(g + 1) * gd], axis=-1, keepdims=True)

    return _hosted_call(
        main, name=name, grid=(n_tiles,), jobs=jobs, semantics=("arbitrary",),
        operands=(proj, cz, proj, proj, dx, dx, cw, lg, lb, ws, bmap, wout),
        in_specs=[pl.BlockSpec((tm, N_PIECES * d), lambda i: (i, 0)),
                  pl.BlockSpec((tm, d), lambda i: (i, 0)),
                  pl.BlockSpec((HALO, d), lambda i: (nxt(i), 0)),
                  pl.BlockSpec((HALO, d), lambda i: (nxt(i), 5)),
                  pl.BlockSpec((tm, d), lambda i: (i, 0)),
                  pl.BlockSpec((HALO, d), lambda i: (nxt(i), 0)),
                  _full((3, d)), _full((1, d)), _full((1, d)), _full((groups, CHUNK, CHUNK)), _full((CHUNK, d)),
                  _resident((d, d), lambda i: (0, 0))],
        out_specs=[pl.BlockSpec((tm, N_PIECES * d), lambda i: (i, 0)), _full((d, d)),
                   _full((3, d)), _full((1, d)), _full((1, d)), _full((groups, CHUNK, CHUNK)), _full((CHUNK, groups))],
        out_shape=[jax.ShapeDtypeStruct((n, N_PIECES * d), BF16), jax.ShapeDtypeStruct((d, d), F32),
                   jax.ShapeDtypeStruct((3, d), F32), jax.ShapeDtypeStruct((1, d), F32),
                   jax.ShapeDtypeStruct((1, d), F32), jax.ShapeDtypeStruct((groups, CHUNK, CHUNK), F32),
                   jax.ShapeDtypeStruct((CHUNK, groups), F32)],
        scratch_shapes=[pltpu.VMEM((tm, d), BF16), pltpu.VMEM((tm, d), F32), pltpu.VMEM((tm, d), BF16),
                        pltpu.VMEM((tm, d), F32), pltpu.VMEM((CHUNK, d), F32)])


def _all_reduce_small(pack, name, jobs=()):
    rows = pack.shape[0]
    rh = rows // 2
    assert rh % HALO == 0, rows

    def main(ins, outs, scratch):
        p_ref, = ins
        o_ref, = outs
        sib_buf, chip_buf, got_buf, send_sem, recv_sem = scratch
        sems = (send_sem, recv_sem)
        x, y, c = _mesh_pos()
        chip, sibling = 2 * x + y, (x, y, 1 - c)
        mine = pl.ds(pl.multiple_of(c * rh, HALO), rh)
        theirs = pl.ds(pl.multiple_of((1 - c) * rh, HALO), rh)
        swap = _remote(p_ref.at[theirs], sib_buf, sems, 0, sibling)
        swap.start()
        swap.wait()
        chip_buf[chip] = p_ref[mine, :] + sib_buf[...]
        copies = []
        for r in (1, 2, 3):
            px, py = _peer_chip(x, y, r)
            copies.append(_remote(chip_buf.at[chip], chip_buf.at[chip], sems, r, (px, py, c)))
        for cp in copies:
            cp.start()
        for r, cp in zip((1, 2, 3), copies):
            px, py = _peer_chip(x, y, r)
            landed = chip_buf.at[2 * px + py]
            _remote(landed, landed, sems, r, (px, py, c)).wait_recv()
            cp.wait_send()
        total = ((chip_buf[0] + chip_buf[1]) + chip_buf[2]) + chip_buf[3]
        o_ref[mine, :] = total
        chip_buf[chip] = total
        share = _remote(chip_buf.at[chip], got_buf, sems, 4, sibling)
        share.start()
        share.wait()
        o_ref[theirs, :] = got_buf[...]

    outs, job_outs = _hosted_call(
        main, name=name, grid=(1,), jobs=jobs, operands=(pack,),
        in_specs=[_full(pack.shape)], out_specs=[_full(pack.shape)],
        out_shape=[jax.ShapeDtypeStruct(pack.shape, pack.dtype)],
        scratch_shapes=[pltpu.VMEM((rh, 128), F32), pltpu.VMEM((N_CHIP, rh, 128), F32), pltpu.VMEM((rh, 128), F32),
                        pltpu.SemaphoreType.DMA((5,)), pltpu.SemaphoreType.DMA((5,))])
    return outs[0], job_outs


SUM_BLOCKS = 2


def _pair_sum(grads, recvs, core, name):
    n_t, nb = len(grads), SUM_BLOCKS
    n_chip = grads[0].shape[0]

    def body(core_ref, *refs):
        for t in range(n_t):
            refs[2 * n_t + t][...] = (refs[t][...] + refs[n_t + t][...]).astype(BF16)

    def blk(g):
        rh = g.shape[1] // 2
        assert rh % (16 * nb) == 0, g.shape
        return (None, rh // nb, g.shape[2])

    mine = [pl.BlockSpec(blk(g), lambda k, i, core_ref: (k, core_ref[0] * nb + i, 0)) for g in grads]
    plain = [pl.BlockSpec(blk(g), lambda k, i, core_ref: (k, i, 0)) for g in grads]
    return pl.pallas_call(
        body, name=name,
        grid_spec=pltpu.PrefetchScalarGridSpec(num_scalar_prefetch=1, grid=(n_chip, nb), in_specs=mine + plain,
                                               out_specs=plain),
        out_shape=[jax.ShapeDtypeStruct((n_chip, g.shape[1] // 2, g.shape[2]), BF16) for g in grads],
        compiler_params=pltpu.CompilerParams(dimension_semantics=("parallel", "parallel"),
                                             vmem_limit_bytes=VMEM_LIMIT_V7X),
    )(core, *grads, *recvs)


def _final_sum(grads, recvs, arriveds, where, prevs, layer, n_layers, name):
    n_t, nb = len(grads), SUM_BLOCKS
    extra = [] if prevs is None else list(prevs)

    def body(where_ref, *refs):
        outs = refs[len(refs) - n_t:]
        for t in range(n_t):
            g_ref, r_ref = refs[t], refs[n_t + t]
            a1, a2, a3 = (refs[(2 + s) * n_t + t] for s in range(3))
            own = g_ref[...] + r_ref[...]
            outs[t][...] = ((own + a1[...].astype(F32)) + a2[...].astype(F32)) + a3[...].astype(F32)

    def blk(g):
        return (None, g.shape[1] // 2 // nb, g.shape[2])

    in_specs = ([pl.BlockSpec(blk(g), lambda i, w: (w[1], w[0] * nb + i, 0)) for g in grads]
                + [pl.BlockSpec(blk(g), lambda i, w: (w[1], i, 0)) for g in grads]
                + [pl.BlockSpec(blk(g), lambda i, w, s=s: (s, i, 0)) for s in range(3) for g in grads]
                + [pl.BlockSpec(memory_space=pl.ANY)] * len(extra))
    return pl.pallas_call(
        body, name=name,
        grid_spec=pltpu.PrefetchScalarGridSpec(
            num_scalar_prefetch=1, grid=(nb,), in_specs=in_specs,
            out_specs=[pl.BlockSpec(blk(g), lambda i, w: (layer, w[0] * nb + i, 0)) for g in grads]),
        out_shape=[jax.ShapeDtypeStruct((n_layers,) + g.shape[1:], F32) for g in grads],
        input_output_aliases={1 + 5 * n_t + t: t for t in range(n_t)} if extra else {},
        compiler_params=pltpu.CompilerParams(dimension_semantics=("parallel",), vmem_limit_bytes=VMEM_LIMIT_V7X),
    )(where, *grads, *recvs, *arriveds, *arriveds, *arriveds, *extra)


def _adamw_math(w, g, m, v):
    m = ADAM_B1 * m + (1.0 - ADAM_B1) * g
    v = ADAM_B2 * v + (1.0 - ADAM_B2) * (g * g)
    m_hat = m / (1.0 - ADAM_B1 ** ADAM_STEP)
    v_hat = v / (1.0 - ADAM_B2 ** ADAM_STEP)
    delta = -ADAM_LR * (m_hat / (jnp.sqrt(v_hat) + ADAM_EPS) + ADAM_WD * w)
    return delta, m, v


ADAMW_BLOCKS = 8


def _adamw_big(ws, gs, ms, vs, name):
    n_t, nb = len(ws), ADAMW_BLOCKS
    n_l = ws[0].shape[0]

    def body(*refs):
        for t in range(n_t):
            res = _adamw_math(*(refs[q * n_t + t][...] for q in range(4)))
            for q in range(3):
                refs[(4 + q) * n_t + t][...] = res[q]

    def spec(w):
        assert w.shape[1] % (HALO * nb) == 0, w.shape
        return pl.BlockSpec((None, w.shape[1] // nb, w.shape[2]), lambda l, i: (l, i, 0))

    specs = [spec(w) for w in ws]
    outs = pl.pallas_call(
        body, name=name, grid=(n_l, nb), in_specs=specs * 4, out_specs=specs * 3,
        out_shape=[jax.ShapeDtypeStruct(w.shape, F32) for w in ws] * 3,
        compiler_params=pltpu.CompilerParams(dimension_semantics=("parallel", "parallel"),
                                             vmem_limit_bytes=VMEM_LIMIT_V7X),
    )(*ws, *gs, *ms, *vs)
    return outs[:n_t], outs[n_t:2 * n_t], outs[2 * n_t:]


def _adamw_small(ws, gs, ms, vs, name):
    n_p = len(ws)

    def body(*refs):
        ins, outs = refs[:4 * n_p], refs[4 * n_p:]
        for p in range(n_p):
            res = _adamw_math(ins[p][...], ins[n_p + p][...], ins[2 * n_p + p][...], ins[3 * n_p + p][...])
            for q in range(3):
                outs[q * n_p + p][...] = res[q]

    outs = pl.pallas_call(
        body, name=name, in_specs=[VMEM_SPEC] * (4 * n_p), out_specs=[VMEM_SPEC] * (3 * n_p),
        out_shape=[jax.ShapeDtypeStruct(w.shape, F32) for w in ws] * 3,
        compiler_params=pltpu.CompilerParams(vmem_limit_bytes=VMEM_LIMIT_V7X),
    )(*ws, *gs, *ms, *vs)
    return outs[:n_p], outs[n_p:2 * n_p], outs[2 * n_p:]


def kernel(x, mix_norm_g, w_in, conv_a_w, ln_v_g, ln_v_b, w_s, b_s, w_out, ffn_norm_g, w_up, conv_ffn_w, w_down, final_norm_g, loss_target, m_mix_norm_g, m_w_in, m_conv_a_w, m_ln_v_g, m_ln_v_b, m_w_s, m_b_s, m_w_out, m_ffn_norm_g, m_w_up, m_conv_ffn_w, m_w_down, m_final_norm_g, v_mix_norm_g, v_w_in, v_conv_a_w, v_ln_v_g, v_ln_v_b, v_w_s, v_b_s, v_w_out, v_ffn_norm_g, v_w_up, v_conv_ffn_w, v_w_down, v_final_norm_g):
    bsz, seq, d = x.shape
    n = bsz * seq
    n_l, groups = w_s.shape[0], w_s.shape[1]
    assert n_l == 2, "the exchange schedule below is written for two layers"
    gd = d // groups
    ff = w_down.shape[1] * N_CHIP
    mx, my, mc = _mesh_pos()
    chip = 2 * mx + my
    core = jnp.reshape(mc, (1,)).astype(jnp.int32)
    where = jnp.stack([mc, chip]).astype(jnp.int32)

    tm_mm = _row_tile(seq, 512)
    tm_ew = _row_tile(seq, 256)
    tk_w = _row_tile(seq, 2048)
    ff_chunk = ff // 2 if (ff // 2) % 128 == 0 else ff
    IN, OUT, UP, DOWN = range(4)

    shards = [w.astype(BF16) for w in (w_in, w_out, w_up, w_down)]
    piece = lambda a, l: (shards[a], l)
    (w_in_0,), = _comm_only([_GatherJob([piece(IN, 0)])], "gather_first")

    taps = jnp.concatenate([conv_a_w.reshape(n_l, -1), conv_ffn_w.reshape(n_l, -1)], axis=1)
    tap_rows = -(-taps.size // 128 // 8) * 8
    tap_pack = jnp.zeros((tap_rows * 128,), F32).at[:taps.size].set(taps.reshape(-1)).reshape(tap_rows, 128)

    bmaps = jnp.repeat(jnp.swapaxes(b_s, 1, 2), gd, axis=2)

    xs = x.reshape(n, d)
    tgt = loss_target.reshape(n, d)

    (proj0, h1_0), ((w_out_0, w_up_0), (tap_all,)) = _norm_matmul(
        xs, mix_norm_g[0][None], w_in_0, tm=tm_mm, name="fwd_in_proj_0",
        jobs=[_GatherJob([piece(OUT, 0), piece(UP, 0)]), _SmallGatherJob([tap_pack])])
    tap_all = tap_all.reshape(N_CHIP, -1)[:, :taps.size].reshape(N_CHIP, n_l, -1)
    ca = tap_all[:, :, :3 * d // N_CHIP].reshape(N_CHIP, n_l, 3, d // N_CHIP)
    cf = tap_all[:, :, 3 * d // N_CHIP:].reshape(N_CHIP, n_l, 3, 2 * ff // N_CHIP)
    conv_a_full = jnp.transpose(ca, (1, 2, 0, 3)).reshape(n_l, 3, d)
    conv_f_full = jnp.transpose(cf, (1, 2, 0, 3)).reshape(n_l, 3, 2 * ff)

    def mixer_args(l, w_out_l):
        return (conv_a_full[l], ln_v_g[l][None], ln_v_b[l][None], w_s[l], bmaps[l], w_out_l.reshape(d, d))

    (x1_0, cz0), ((w_down_0, w_out_1),) = _mixer_fwd(
        proj0, xs, *mixer_args(0, w_out_0), seq=seq, tm=tm_ew, name="fwd_mixer_0",
        jobs=[_GatherJob([piece(DOWN, 0), piece(OUT, 1)])])
    (up0, h2_0), ((w_in_1,),) = _norm_matmul(
        x1_0, ffn_norm_g[0][None], w_up_0, tm=tm_mm, name="fwd_up_proj_0", jobs=[_GatherJob([piece(IN, 1)])])
    (x2_0, conv0), _ = _ffn_fwd(up0, x1_0, conv_f_full[0], w_down_0.reshape(ff, d), seq=seq, tm=tm_ew, cwid=ff_chunk,
                                name="fwd_ffn_0")
    (proj1, h1_1), ((w_up_1, w_down_1),) = _norm_matmul(
        x2_0, mix_norm_g[1][None], w_in_1, tm=tm_mm, name="fwd_in_proj_1",
        jobs=[_GatherJob([piece(UP, 1), piece(DOWN, 1)])])
    (x1_1, cz1), _ = _mixer_fwd(proj1, x2_0, *mixer_args(1, w_out_1), seq=seq, tm=tm_ew, name="fwd_mixer_1")
    (up1, h2_1), _ = _norm_matmul(x1_1, ffn_norm_g[1][None], w_up_1, tm=tm_mm, name="fwd_up_proj_1")
    (x2_1, conv1), _ = _ffn_fwd(up1, x1_1, conv_f_full[1], w_down_1.reshape(ff, d), seq=seq, tm=tm_ew, cwid=ff_chunk,
                                name="fwd_ffn_1")
    dx, loss_tile, d_final_g = _final_loss(x2_1, final_norm_g[None], tgt, tm=tm_mm, name="final_loss")

    def chipwise(a):
        return a.reshape(N_CHIP, a.shape[0] // N_CHIP, a.shape[1])

    def pair_sums(grads, recvs, tag):
        return _pair_sum(grads, recvs, core, f"grad_pair_sum_{tag}")

    d_up, dwd, d_cf1 = _ffn_bwd(dx, up1, conv1, conv_f_full[1], w_down_1.reshape(ff, d), seq=seq, tm=tm_ew,
                                cwid=ff_chunk, name="bwd_ffn_1")
    (dx1, d_g2_1), _ = _matmul_bwd_x(d_up, w_up_1, x1_1, ffn_norm_g[1][None], dx, tm=tm_mm, name="bwd_up_x_1")
    (dwu,), _ = _matmul_bwd_w(h2_1, d_up, N_CHIP, tk=tk_w, name="bwd_up_w_1")
    (d_proj, dwo, d_ca1, d_lg1, d_lb1, d_ws1, d_bs1), _ = _mixer_bwd(
        dx1, proj1, cz1, *mixer_args(1, w_out_1), seq=seq, tm=tm_ew, name="bwd_mixer_1")
    (dx, d_g1_1), _ = _matmul_bwd_x(d_proj, w_in_1, x2_0, mix_norm_g[1][None], dx1, tm=tm_mm, name="bwd_in_x_1")
    (dwi,), _ = _matmul_bwd_w(h1_1, d_proj, N_CHIP, tk=tk_w, name="bwd_in_w_1")
    grads1 = [dwi, chipwise(dwo), dwu, chipwise(dwd)]
    small1 = [d_g1_1, d_ca1, d_lg1, d_lb1, d_ws1, d_bs1.T, d_g2_1, d_cf1]

    d_up, dwd, d_cf0 = _ffn_bwd(dx, up0, conv0, conv_f_full[0], w_down_0.reshape(ff, d), seq=seq, tm=tm_ew,
                                cwid=ff_chunk, name="bwd_ffn_0")
    (dx1, d_g2_0), (recv1,) = _matmul_bwd_x(d_up, w_up_0, x1_0, ffn_norm_g[0][None], dx, tm=tm_mm, name="bwd_up_x_0",
                                            jobs=[_SwapJob(grads1)])
    parts1 = pair_sums(grads1, recv1, 1)
    (dwu,), ((arr1_in,),) = _matmul_bwd_w(h2_0, d_up, N_CHIP, tk=tk_w, name="bwd_up_w_0",
                                          jobs=[_ScatterJob([parts1[IN]])])
    ud0 = [dwu, chipwise(dwd)]
    (d_proj, dwo, d_ca0, d_lg0, d_lb0, d_ws0, d_bs0), (arr1_rest, recv0_ud) = _mixer_bwd(
        dx1, proj0, cz0, *mixer_args(0, w_out_0), seq=seq, tm=tm_ew, name="bwd_mixer_0",
        jobs=[_ScatterJob(parts1[OUT:]), _SwapJob(ud0)])
    arrived1 = [arr1_in] + list(arr1_rest)
    gsum = _final_sum(grads1, recv1, arrived1, where, None, 1, n_l, "grad_final_sum_1")
    parts0_ud = pair_sums(ud0, recv0_ud, "0ud")
    out0 = chipwise(dwo)
    (dwi,), (gsum, arr0_ud, (recv0_out,)) = _matmul_bwd_w(
        h1_0, d_proj, N_CHIP, tk=tk_w, name="bwd_in_w_0",
        jobs=[_ShareJob([(g, 1) for g in gsum]), _ScatterJob(parts0_ud), _SwapJob([out0])])
    part0_out, = pair_sums([out0], [recv0_out], "0o")
    n_mm = n // tm_mm
    n_head = max(n_mm // 4, 1)
    head, ((arr0_out,), (recv0_in,)) = _matmul_bwd_x(
        d_proj, w_in_0, xs, mix_norm_g[0][None], dx1, tm=tm_mm, name="bwd_in_x_0a", tiles=(0, n_head),
        jobs=[_ScatterJob([part0_out]), _SwapJob([dwi])])
    part0_in, = pair_sums([dwi], [recv0_in], "0i")
    (dx, d_g1_0), ((arr0_in,),) = _matmul_bwd_x(
        d_proj, w_in_0, xs, mix_norm_g[0][None], dx1, tm=tm_mm, name="bwd_in_x_0b", tiles=(n_head, n_mm - n_head),
        before=head, jobs=[_ScatterJob([part0_in])])
    grad_x = dx.reshape(bsz, seq, d)
    grads0 = [dwi, out0] + ud0
    recv0 = [recv0_in, recv0_out] + list(recv0_ud)
    arrived0 = [arr0_in, arr0_out] + list(arr0_ud)
    small0 = [d_g1_0, d_ca0, d_lg0, d_lb0, d_ws0, d_bs0.T, d_g2_0, d_cf0]

    flat = [a.reshape(-1) for a in small0 + small1] + [d_final_g.reshape(-1), loss_tile[0, 0:1]]
    sizes = [a.size for a in flat]
    total = sum(sizes)
    rows = -(-total // 128 // (2 * HALO)) * 2 * HALO
    pack = jnp.concatenate(flat + [jnp.zeros((rows * 128 - total,), F32)]).reshape(rows, 128)
    red, _ = _all_reduce_small(pack, "small_all_reduce")
    red = red.reshape(-1)
    gsum = _final_sum(grads0, recv0, arrived0, where, gsum, 0, n_l, "grad_final_sum_0")
    (g_in, g_out, g_up, g_down), = _comm_only([_ShareJob([(g, 0) for g in gsum])], "grad_share_0")

    pieces, off = [], 0
    for s in sizes:
        pieces.append(red[off:off + s])
        off += s
    per_layer = [pieces[l * 8:(l + 1) * 8] for l in range(n_l)]

    def stacked(idx, shape):
        return jnp.stack([per_layer[l][idx].reshape(shape) for l in range(n_l)])

    def my_cols(a, width):
        return lax.dynamic_slice_in_dim(a, chip * width, width, axis=-1)

    g_mix_norm = stacked(0, (d,))
    g_conv_a = my_cols(stacked(1, (3, d)), d // N_CHIP)
    g_ln_g = stacked(2, (d,))
    g_ln_b = stacked(3, (d,))
    g_ws = stacked(4, (groups, CHUNK, CHUNK))
    g_bs = stacked(5, (groups, CHUNK))
    g_ffn_norm = stacked(6, (d,))
    g_conv_f = my_cols(stacked(7, (3, 2 * ff)), 2 * ff // N_CHIP)
    g_final = pieces[-2].reshape(1, d)
    loss = pieces[-1].reshape(())

    big_d, big_nm, big_nv = _adamw_big([w_in, w_out, w_up, w_down], [g_in, g_out, g_up, g_down],
                                       [m_w_in, m_w_out, m_w_up, m_w_down], [v_w_in, v_w_out, v_w_up, v_w_down],
                                       "adamw_big")

    sm_w = [mix_norm_g, conv_a_w, ln_v_g, ln_v_b, w_s, b_s, ffn_norm_g, conv_ffn_w, final_norm_g[None]]
    sm_g = [g_mix_norm, g_conv_a, g_ln_g, g_ln_b, g_ws, g_bs, g_ffn_norm, g_conv_f, g_final]
    sm_m = [m_mix_norm_g, m_conv_a_w, m_ln_v_g, m_ln_v_b, m_w_s, m_b_s, m_ffn_norm_g, m_conv_ffn_w, m_final_norm_g[None]]
    sm_v = [v_mix_norm_g, v_conv_a_w, v_ln_v_g, v_ln_v_b, v_w_s, v_b_s, v_ffn_norm_g, v_conv_ffn_w, v_final_norm_g[None]]
    sm_d, sm_nm, sm_nv = _adamw_small(sm_w, sm_g, sm_m, sm_v, "adamw_small")

    def ordered(sm, bigs):
        return [sm[0], bigs[0], sm[1], sm[2], sm[3], sm[4], sm[5], bigs[1], sm[6], bigs[2], sm[7], bigs[3],
                sm[8].reshape(d)]

    out_g = ordered(sm_g, [g_in, g_out, g_up, g_down])
    out_d = ordered(sm_d, big_d)
    out_m = ordered(sm_nm, big_nm)
    out_v = ordered(sm_nv, big_nv)
    return (loss, grad_x, *out_g, *out_d, *out_m, *out_v)
```

```python
import jax
import jax.numpy as jnp
from jax import lax
from jax.experimental import pallas as pl
from jax.experimental.pallas import tpu as pltpu

F32 = jnp.float32
BF16 = jnp.bfloat16
EPS = 1e-6
CHUNK = 128
N_CHIP = 4
HALO = 8
N_PIECES = 7
LATE_STEPS = 2
VMEM_LIMIT_V7X = 56 * 1024 * 1024
MESH_T = pl.DeviceIdType.MESH
HBM_SPEC = pl.BlockSpec(memory_space=pltpu.HBM)
VMEM_SPEC = pl.BlockSpec(memory_space=pltpu.VMEM)

ADAM_LR, ADAM_B1, ADAM_B2, ADAM_EPS, ADAM_WD, ADAM_STEP = 0.001, 0.9, 0.999, 1e-08, 0.01, 10

NT_DIMS = (((1,), (1,)), ((), ()))
TN_DIMS = (((0,), (0,)), ((), ()))


def _resident(block_shape, index_map):
    return pl.BlockSpec(block_shape, index_map, pipeline_mode=pl.Buffered(1))


def _full(shape):
    return pl.BlockSpec(shape, lambda *_: (0,) * len(shape))


def _row_tile(seq, want):
    t = min(seq, want)
    assert seq % t == 0 and t % CHUNK == 0, (seq, want)
    return t


def _shift_prev(prev8, cur, k):
    ext = jnp.concatenate([prev8, cur], axis=0)
    return pltpu.roll(ext, k, 0)[HALO:]


def _shift_next(cur, next8, k):
    ext = jnp.concatenate([cur, next8], axis=0)
    n = ext.shape[0]
    return pltpu.roll(ext, n - k, 0)[:n - HALO]


def _rowsum(a):
    return jnp.sum(a, axis=0, keepdims=True)


def _lanemean(a):
    return jnp.mean(a, axis=-1, keepdims=True)


def _tril_weights(ws_ref, groups):
    r = lax.broadcasted_iota(jnp.int32, (CHUNK, CHUNK), 0)
    c = lax.broadcasted_iota(jnp.int32, (CHUNK, CHUNK), 1)
    tril = r >= c
    return tril, [jnp.where(tril, ws_ref[g], 0.0) for g in range(groups)]


def _halo_prev(tm):
    return lambda i: jnp.maximum(i * (tm // HALO) - 1, 0)


def _halo_next(tm, n):
    last = n // HALO - 1
    return lambda i: jnp.minimum((i + 1) * (tm // HALO), last)


def _mesh_pos():
    return lax.axis_index("x"), lax.axis_index("y"), lax.axis_index("c")


def _peer_chip(x, y, r):
    return (1 - x if r >> 1 else x), (1 - y if r & 1 else y)


def _remote(src, dst, sems, k, to):
    return pltpu.make_async_remote_copy(src_ref=src, dst_ref=dst, send_sem=sems[0].at[k], recv_sem=sems[1].at[k],
                                        device_id=to, device_id_type=MESH_T)


class _GatherJob:
    def __init__(self, pieces):
        self.pieces = pieces
        self.ins = [p[0] for p in pieces]
        self.out_shapes = [jax.ShapeDtypeStruct((N_CHIP,) + p[0].shape[1:], p[0].dtype) for p in pieces]
        self.aliases = {}
        n = len(pieces)
        self.sems = [pltpu.SemaphoreType.DMA((3 * n,))] * 4 + [pltpu.SemaphoreType.DMA((n,))]

    def _half(self, outs, a, of_chip, core):
        rh = outs[a].shape[1] // 2
        return outs[a].at[of_chip, pl.ds(core * rh, rh), :]

    def _own(self, ins, outs, sems, chip):
        return [pltpu.make_async_copy(ins[a].at[layer], outs[a].at[chip], sems[4].at[a])
                for a, (_, layer) in enumerate(self.pieces)]

    def _sends(self, ins, outs, sems, x, y, c):
        chip, out = 2 * x + y, []
        for a, (_, layer) in enumerate(self.pieces):
            rh = outs[a].shape[1] // 2
            for r in (1, 2, 3):
                px, py = _peer_chip(x, y, r)
                out.append(_remote(ins[a].at[layer, pl.ds(c * rh, rh), :], self._half(outs, a, chip, c), sems[0:2],
                                   3 * a + r - 1, (px, py, c)))
        return out

    def _passes(self, outs, sems, x, y, c, core):
        out = []
        for a in range(len(self.pieces)):
            for r in (1, 2, 3):
                px, py = _peer_chip(x, y, r)
                landed = self._half(outs, a, 2 * px + py, core)
                out.append(_remote(landed, landed, sems[2:4], 3 * a + r - 1, (x, y, 1 - c)))
        return out

    def start(self, ins, outs, sems):
        x, y, c = _mesh_pos()
        for cp in self._own(ins, outs, sems, 2 * x + y) + self._sends(ins, outs, sems, x, y, c):
            cp.start()

    def late(self, ins, outs, sems):
        x, y, c = _mesh_pos()
        passes = self._passes(outs, sems, x, y, c, c)
        k = 0
        for a in range(len(self.pieces)):
            for r in (1, 2, 3):
                px, py = _peer_chip(x, y, r)
                landed = self._half(outs, a, 2 * px + py, c)
                _remote(landed, landed, sems[0:2], k, (px, py, c)).wait_recv()
                passes[k].start()
                k += 1

    def finish(self, ins, outs, sems):
        x, y, c = _mesh_pos()
        for cp in self._passes(outs, sems, x, y, c, 1 - c):
            cp.wait_recv()
        for cp in self._sends(ins, outs, sems, x, y, c) + self._passes(outs, sems, x, y, c, c):
            cp.wait_send()
        for cp in self._own(ins, outs, sems, 2 * x + y):
            cp.wait()


class _SmallGatherJob:
    def __init__(self, pieces):
        self.ins = list(pieces)
        self.out_shapes = [jax.ShapeDtypeStruct((N_CHIP,) + p.shape, p.dtype) for p in pieces]
        self.aliases = {}
        n = len(pieces)
        self.sems = [pltpu.SemaphoreType.DMA((3 * n,))] * 2 + [pltpu.SemaphoreType.DMA((n,))]

    def _outgoing(self, ins, outs, sems):
        x, y, c = _mesh_pos()
        mine = [outs[a].at[2 * x + y] for a in range(len(ins))]
        own = [pltpu.make_async_copy(ins[a], mine[a], sems[2].at[a]) for a in range(len(ins))]
        sends = [_remote(ins[a], mine[a], sems[0:2], 3 * a + r - 1, (*_peer_chip(x, y, r), c))
                 for a in range(len(ins)) for r in (1, 2, 3)]
        return own, sends

    def start(self, ins, outs, sems):
        own, sends = self._outgoing(ins, outs, sems)
        for cp in own + sends:
            cp.start()

    def finish(self, ins, outs, sems):
        x, y, c = _mesh_pos()
        for a in range(len(ins)):
            for r in (1, 2, 3):
                px, py = _peer_chip(x, y, r)
                landed = outs[a].at[2 * px + py]
                _remote(landed, landed, sems[0:2], 3 * a + r - 1, (px, py, c)).wait_recv()
        own, sends = self._outgoing(ins, outs, sems)
        for cp in sends:
            cp.wait_send()
        for cp in own:
            cp.wait()


class _SwapJob:
    def __init__(self, pieces):
        self.ins = list(pieces)
        self.out_shapes = [jax.ShapeDtypeStruct((g.shape[0], g.shape[1] // 2, g.shape[2]), g.dtype) for g in pieces]
        self.aliases = {}
        self.sems = [pltpu.SemaphoreType.DMA((len(pieces),))] * 2

    def _copies(self, ins, outs, sems):
        x, y, c = _mesh_pos()
        out = []
        for a in range(len(ins)):
            rh = ins[a].shape[1] // 2
            out.append(_remote(ins[a].at[:, pl.ds((1 - c) * rh, rh), :], outs[a], sems, a, (x, y, 1 - c)))
        return out

    def start(self, ins, outs, sems):
        for cp in self._copies(ins, outs, sems):
            cp.start()

    def finish(self, ins, outs, sems):
        for cp in self._copies(ins, outs, sems):
            cp.wait()


class _ScatterJob:
    def __init__(self, pieces):
        self.ins = list(pieces)
        self.out_shapes = [jax.ShapeDtypeStruct((3,) + p.shape[1:], p.dtype) for p in pieces]
        self.aliases = {}
        self.sems = [pltpu.SemaphoreType.DMA((3 * len(pieces),))] * 2

    def _copies(self, ins, outs, sems):
        x, y, c = _mesh_pos()
        out = []
        for a in range(len(ins)):
            for r in (1, 2, 3):
                px, py = _peer_chip(x, y, r)
                out.append(_remote(ins[a].at[2 * px + py], outs[a].at[r - 1], sems, 3 * a + r - 1, (px, py, c)))
        return out

    def start(self, ins, outs, sems):
        for cp in self._copies(ins, outs, sems):
            cp.start()

    def finish(self, ins, outs, sems):
        for cp in self._copies(ins, outs, sems):
            cp.wait()


class _ShareJob:
    def __init__(self, pieces):
        self.pieces = pieces
        self.ins = [p[0] for p in pieces]
        self.out_shapes = [jax.ShapeDtypeStruct(p[0].shape, p[0].dtype) for p in pieces]
        self.aliases = {a: a for a in range(len(pieces))}
        self.sems = [pltpu.SemaphoreType.DMA((len(pieces),))] * 2

    def _copies(self, outs, sems, core):
        x, y, c = _mesh_pos()
        out = []
        for a, (_, layer) in enumerate(self.pieces):
            rh = outs[a].shape[1] // 2
            rows = outs[a].at[layer, pl.ds(core * rh, rh), :]
            out.append(_remote(rows, rows, sems, a, (x, y, 1 - c)))
        return out

    def start(self, ins, outs, sems):
        for cp in self._copies(outs, sems, lax.axis_index("c")):
            cp.start()

    def finish(self, ins, outs, sems):
        c = lax.axis_index("c")
        for cp in self._copies(outs, sems, c):
            cp.wait_send()
        for cp in self._copies(outs, sems, 1 - c):
            cp.wait_recv()


def _hosted_call(main, *, name, grid, in_specs, out_specs, out_shape, scratch_shapes=(), jobs=(), semantics=None,
                 operands=(), main_aliases=None):
    n_in, n_out, n_sc = len(in_specs), len(out_specs), len(scratch_shapes)
    counts = [(len(j.ins), len(j.out_shapes), len(j.sems)) for j in jobs]
    j_in, j_out, j_sc = (sum(c[k] for c in counts) for k in range(3))
    aliases, i0, o0 = dict(main_aliases or {}), n_in, n_out
    for j, (ci, co, _) in zip(jobs, counts):
        aliases.update({i0 + a: o0 + b for a, b in j.aliases.items()})
        i0, o0 = i0 + ci, o0 + co

    def body(*refs):
        cuts = [0, n_in, n_in + j_in, n_in + j_in + n_out, n_in + j_in + n_out + j_out,
                n_in + j_in + n_out + j_out + n_sc, len(refs)]
        m_in, jb_in, m_out, jb_out, m_sc, jb_sc = (list(refs[cuts[k]:cuts[k + 1]]) for k in range(6))

        def run(phase):
            i0 = o0 = s0 = 0
            for j, (ci, co, cs) in zip(jobs, counts):
                if hasattr(j, phase):
                    getattr(j, phase)(jb_in[i0:i0 + ci], jb_out[o0:o0 + co], jb_sc[s0:s0 + cs])
                i0, o0, s0 = i0 + ci, o0 + co, s0 + cs

        step, n_steps = 0, 1
        for ax in range(len(grid)):
            step, n_steps = step * grid[ax] + pl.program_id(ax), n_steps * grid[ax]
        if jobs:
            pl.when(step == 0)(lambda: run("start"))
        main(m_in, m_out, m_sc)
        if jobs:
            pl.when(step == max(n_steps - 1 - LATE_STEPS, 0))(lambda: run("late"))
            pl.when(step == n_steps - 1)(lambda: run("finish"))

    if semantics is None or jobs:
        semantics = ("arbitrary",) * len(grid)
    outs = pl.pallas_call(
        body, name=name, grid=grid,
        in_specs=list(in_specs) + [HBM_SPEC] * j_in,
        out_specs=list(out_specs) + [HBM_SPEC] * j_out,
        out_shape=list(out_shape) + [s for j in jobs for s in j.out_shapes],
        scratch_shapes=list(scratch_shapes) + [s for j in jobs for s in j.sems],
        input_output_aliases=aliases,
        compiler_params=pltpu.CompilerParams(dimension_semantics=semantics, vmem_limit_bytes=VMEM_LIMIT_V7X),
    )(*operands, *[a for j in jobs for a in j.ins])
    main_outs, rest, job_outs = list(outs[:n_out]), list(outs[n_out:]), []
    for _, co, _ in counts:
        job_outs.append(rest[:co])
        rest = rest[co:]
    return main_outs, job_outs


def _comm_only(jobs, name):
    return _hosted_call(lambda i, o, s: None, name=name, grid=(1,), in_specs=[], out_specs=[], out_shape=[],
                        jobs=jobs)[1]


def _norm_matmul(x, g, wg, *, tm, name, jobs=()):
    n, d = x.shape
    n_chip, _, ck = wg.shape

    def main(ins, outs, _):
        x_ref, g_ref, w_ref = ins
        o_ref, ht_ref = outs
        xf = x_ref[...]
        r = lax.rsqrt(_lanemean(xf * xf) + EPS)
        h = xf * r * g_ref[...]
        ht_ref[...] = h.T.astype(BF16)
        hb = h.astype(BF16)
        for j in range(n_chip):
            o_ref[:, j * ck:(j + 1) * ck] = jnp.dot(hb, w_ref[j], preferred_element_type=F32).astype(BF16)

    return _hosted_call(
        main, name=name, grid=(n // tm,), jobs=jobs, semantics=("parallel",), operands=(x, g, wg),
        in_specs=[pl.BlockSpec((tm, d), lambda i: (i, 0)), _full((1, d)), _resident((n_chip, d, ck), lambda i: (0, 0, 0))],
        out_specs=[pl.BlockSpec((tm, n_chip * ck), lambda i: (i, 0)), pl.BlockSpec((d, tm), lambda i: (0, i))],
        out_shape=[jax.ShapeDtypeStruct((n, n_chip * ck), BF16), jax.ShapeDtypeStruct((d, n), BF16)])


def _mixer_fwd(proj, x, cw, lg, lb, ws, bmap, wout, *, seq, tm, name, jobs=()):
    n, d = x.shape
    groups = ws.shape[0]
    gd = d // groups
    prev = _halo_prev(tm)

    def main(ins, outs, scratch):
        proj_ref, pcg_ref, pxi_ref, x_ref, cw_ref, lg_ref, lb_ref, ws_ref, bmap_ref, wout_ref = ins
        o_ref, cz_ref = outs
        vn_s, mixed_s = scratch
        seq_start = (pl.program_id(0) * tm) % seq == 0

        def piece(k):
            return proj_ref[:, k * d:(k + 1) * d].astype(F32)

        z = piece(1) * piece(2)
        zprev = jnp.where(seq_start, 0.0, pcg_ref[...].astype(F32) * pxi_ref[...].astype(F32))
        cz = (cw_ref[0:1, :] * _shift_prev(zprev, z, 2) + cw_ref[1:2, :] * _shift_prev(zprev, z, 1)
              + cw_ref[2:3, :] * z)
        cz_ref[...] = cz.astype(BF16)
        ya = piece(0) * cz

        v = piece(4)
        xc = v - _lanemean(v)
        vn = xc * lax.rsqrt(_lanemean(xc * xc) + EPS) * lg_ref[...] + lb_ref[...]
        vn_s[...] = vn.astype(BF16)
        _, wsm = _tril_weights(ws_ref, groups)
        for ck in range(tm // CHUNK):
            rows = slice(ck * CHUNK, (ck + 1) * CHUNK)
            for g in range(groups):
                cols = slice(g * gd, (g + 1) * gd)
                mixed_s[rows, cols] = (jnp.dot(wsm[g].astype(BF16), vn_s[rows, cols], preferred_element_type=F32)
                                       + bmap_ref[:, cols])
        yb = piece(3) * mixed_s[...]
        merged = jax.nn.sigmoid(piece(5)) * ya + jax.nn.sigmoid(piece(6)) * yb
        o_ref[...] = x_ref[...] + jnp.dot(merged.astype(BF16), wout_ref[...], preferred_element_type=F32)

    return _hosted_call(
        main, name=name, grid=(n // tm,), jobs=jobs, semantics=("parallel",),
        operands=(proj, proj, proj, x, cw, lg, lb, ws, bmap, wout),
        in_specs=[pl.BlockSpec((tm, N_PIECES * d), lambda i: (i, 0)),
                  pl.BlockSpec((HALO, d), lambda i: (prev(i), 1)),
                  pl.BlockSpec((HALO, d), lambda i: (prev(i), 2)),
                  pl.BlockSpec((tm, d), lambda i: (i, 0)),
                  _full((3, d)), _full((1, d)), _full((1, d)), _full((groups, CHUNK, CHUNK)), _full((CHUNK, d)),
                  _resident((d, d), lambda i: (0, 0))],
        out_specs=[pl.BlockSpec((tm, d), lambda i: (i, 0)), pl.BlockSpec((tm, d), lambda i: (i, 0))],
        out_shape=[jax.ShapeDtypeStruct((n, d), F32), jax.ShapeDtypeStruct((n, d), BF16)],
        scratch_shapes=[pltpu.VMEM((tm, d), BF16), pltpu.VMEM((tm, d), F32)])


def _ffn_fwd(up, x, cw, wd, *, seq, tm, cwid, name, jobs=()):
    n, d = x.shape
    ff = wd.shape[0]
    prev = _halo_prev(tm)

    def main(ins, outs, _):
        up_ref, pup_ref, x_ref, cw_ref, wd_ref = ins
        o_ref, conv_ref = outs
        seq_start = (pl.program_id(0) * tm) % seq == 0

        def conv(lo):
            cols = slice(lo, lo + cwid)
            cur = up_ref[:, cols].astype(F32)
            pre = jnp.where(seq_start, 0.0, pup_ref[:, cols].astype(F32))
            out = (cw_ref[0:1, cols] * _shift_prev(pre, cur, 2) + cw_ref[1:2, cols] * _shift_prev(pre, cur, 1)
                   + cw_ref[2:3, cols] * cur)
            conv_ref[:, cols] = out.astype(BF16)
            return out

        acc = x_ref[...]
        for cj in range(ff // cwid):
            gate = conv(cj * cwid)
            val = conv(ff + cj * cwid)
            a = gate * jax.nn.sigmoid(gate) * val
            acc = acc + jnp.dot(a.astype(BF16), wd_ref[cj * cwid:(cj + 1) * cwid, :], preferred_element_type=F32)
        o_ref[...] = acc

    return _hosted_call(
        main, name=name, grid=(n // tm,), jobs=jobs, semantics=("parallel",), operands=(up, up, x, cw, wd),
        in_specs=[pl.BlockSpec((tm, 2 * ff), lambda i: (i, 0)),
                  pl.BlockSpec((HALO, 2 * ff), lambda i: (prev(i), 0)),
                  pl.BlockSpec((tm, d), lambda i: (i, 0)),
                  _full((3, 2 * ff)),
                  _resident((ff, d), lambda i: (0, 0))],
        out_specs=[pl.BlockSpec((tm, d), lambda i: (i, 0)), pl.BlockSpec((tm, 2 * ff), lambda i: (i, 0))],
        out_shape=[jax.ShapeDtypeStruct((n, d), F32), jax.ShapeDtypeStruct((n, 2 * ff), BF16)])


def _final_loss(x, g, target, *, tm, name):
    n, d = x.shape

    def body(x_ref, g_ref, t_ref, dx_ref, loss_ref, dg_ref):
        @pl.when(pl.program_id(0) == 0)
        def _():
            loss_ref[...] = jnp.zeros_like(loss_ref)
            dg_ref[...] = jnp.zeros_like(dg_ref)

        xf = x_ref[...]
        r = lax.rsqrt(_lanemean(xf * xf) + EPS)
        xhat = xf * r
        diff = xhat * g_ref[...] - t_ref[...]
        loss_ref[...] += (0.5 / d) * _rowsum(jnp.sum(diff * diff, axis=-1, keepdims=True))
        dy = diff * (1.0 / d)
        dg_ref[...] += _rowsum(dy * xhat)
        dyh = dy * g_ref[...]
        dx_ref[...] = r * (dyh - xhat * _lanemean(dyh * xhat))

    return pl.pallas_call(
        body, name=name, grid=(n // tm,),
        in_specs=[pl.BlockSpec((tm, d), lambda i: (i, 0)), _full((1, d)), pl.BlockSpec((tm, d), lambda i: (i, 0))],
        out_specs=[pl.BlockSpec((tm, d), lambda i: (i, 0)), _full((HALO, CHUNK)), _full((1, d))],
        out_shape=[jax.ShapeDtypeStruct((n, d), F32), jax.ShapeDtypeStruct((HALO, CHUNK), F32),
                   jax.ShapeDtypeStruct((1, d), F32)],
        compiler_params=pltpu.CompilerParams(dimension_semantics=("arbitrary",), vmem_limit_bytes=VMEM_LIMIT_V7X),
    )(x, g, target)


def _ffn_bwd(dx, up, conv, cw, wd, *, seq, tm, cwid, name):
    n, d = dx.shape
    ff = wd.shape[0]
    nxt = _halo_next(tm, n)

    def body(dx_ref, dxn_ref, up_ref, conv_ref, nconv_ref, cw_ref, wd_ref, dup_ref, dwd_ref, dcw_ref):
        i = pl.program_id(0)
        keep_next = jnp.where(((i + 1) * tm) % seq == 0, 0.0, 1.0)

        @pl.when(i == 0)
        def _():
            dwd_ref[...] = jnp.zeros_like(dwd_ref)
            dcw_ref[...] = jnp.zeros_like(dcw_ref)

        dxe = jnp.concatenate([dx_ref[...], dxn_ref[...]], axis=0).astype(BF16)
        dxb = dxe[:tm]
        for cj in range(ff // cwid):
            rows = slice(cj * cwid, (cj + 1) * cwid)
            g_cols, v_cols = slice(cj * cwid, (cj + 1) * cwid), slice(ff + cj * cwid, ff + (cj + 1) * cwid)
            dae = lax.dot_general(dxe, wd_ref[rows, :], NT_DIMS, preferred_element_type=F32)
            da, dan = dae[:tm], dae[tm:]

            def grads(gate, val, da_rows):
                sg = jax.nn.sigmoid(gate)
                sl = gate * sg
                return sl, da_rows * val * sg * (1.0 + gate * (1.0 - sg)), da_rows * sl

            gate, val = conv_ref[:, g_cols].astype(F32), conv_ref[:, v_cols].astype(F32)
            sl, d_gate, d_val = grads(gate, val, da)
            dwd_ref[rows, :] += lax.dot_general((sl * val).astype(BF16), dxb, TN_DIMS, preferred_element_type=F32)
            _, d_gate_n, d_val_n = grads(nconv_ref[:, g_cols].astype(F32), nconv_ref[:, v_cols].astype(F32),
                                         dan * keep_next)
            for cols, dcur, dnext in ((g_cols, d_gate, d_gate_n), (v_cols, d_val, d_val_n)):
                upc = up_ref[:, cols].astype(F32)
                d1, d2 = _shift_next(dcur, dnext, 1), _shift_next(dcur, dnext, 2)
                for k, dk in enumerate((d2, d1, dcur)):
                    dcw_ref[k:k + 1, cols] += _rowsum(dk * upc)
                dup = cw_ref[2:3, cols] * dcur + cw_ref[1:2, cols] * d1 + cw_ref[0:1, cols] * d2
                dup_ref[:, cols] = dup.astype(BF16)

    return pl.pallas_call(
        body, name=name, grid=(n // tm,),
        in_specs=[pl.BlockSpec((tm, d), lambda i: (i, 0)),
                  pl.BlockSpec((HALO, d), lambda i: (nxt(i), 0)),
                  pl.BlockSpec((tm, 2 * ff), lambda i: (i, 0)),
                  pl.BlockSpec((tm, 2 * ff), lambda i: (i, 0)),
                  pl.BlockSpec((HALO, 2 * ff), lambda i: (nxt(i), 0)),
                  _full((3, 2 * ff)),
                  _resident((ff, d), lambda i: (0, 0))],
        out_specs=[pl.BlockSpec((tm, 2 * ff), lambda i: (i, 0)), _full((ff, d)), _full((3, 2 * ff))],
        out_shape=[jax.ShapeDtypeStruct((n, 2 * ff), BF16), jax.ShapeDtypeStruct((ff, d), F32),
                   jax.ShapeDtypeStruct((3, 2 * ff), F32)],
        compiler_params=pltpu.CompilerParams(dimension_semantics=("arbitrary",), vmem_limit_bytes=VMEM_LIMIT_V7X),
    )(dx, dx, up, conv, conv, cw, wd)


def _matmul_bwd_x(dy, wg, x, g, dres, *, tm, name, jobs=(), tiles=None, before=None):
    n, d = x.shape
    n_chip, _, ck = wg.shape
    first_tile, n_tiles = tiles or (0, n // tm)
    extra = [] if before is None else list(before)

    def main(ins, outs, _):
        dy_ref, w_ref, x_ref, g_ref, dres_ref = ins[:5]
        dx_ref, dg_ref = outs

        @pl.when(pl.program_id(0) == 0)
        def _():
            dg_ref[...] = jnp.zeros_like(dg_ref) if before is None else ins[6][...]

        dh = lax.dot_general(dy_ref[:, 0:ck], w_ref[0], NT_DIMS, preferred_element_type=F32)
        for j in range(1, n_chip):
            dh = dh + lax.dot_general(dy_ref[:, j * ck:(j + 1) * ck], w_ref[j], NT_DIMS, preferred_element_type=F32)
        xf = x_ref[...]
        r = lax.rsqrt(_lanemean(xf * xf) + EPS)
        xhat = xf * r
        dg_ref[...] += _rowsum(dh * xhat)
        dyh = dh * g_ref[...]
        dx_ref[...] = dres_ref[...] + r * (dyh - xhat * _lanemean(dyh * xhat))

    rows = lambda width: pl.BlockSpec((tm, width), lambda i: (i + first_tile, 0))
    return _hosted_call(
        main, name=name, grid=(n_tiles,), jobs=jobs, semantics=("arbitrary",), operands=(dy, wg, x, g, dres, *extra),
        in_specs=[rows(n_chip * ck), _resident((n_chip, d, ck), lambda i: (0, 0, 0)), rows(d), _full((1, d)), rows(d)]
        + ([] if before is None else [pl.BlockSpec(memory_space=pl.ANY), _full((1, d))]),
        out_specs=[rows(d), _full((1, d))],
        out_shape=[jax.ShapeDtypeStruct((n, d), F32), jax.ShapeDtypeStruct((1, d), F32)],
        main_aliases=None if before is None else {5: 0})


def _matmul_bwd_w(ht, dy, n_chip, *, tk, name, jobs=()):
    d, n = ht.shape
    ck = dy.shape[1] // n_chip

    def main(ins, outs, _):
        ht_ref, dy_ref = ins
        o_ref, = outs

        @pl.when(pl.program_id(1) == 0)
        def _():
            o_ref[...] = jnp.zeros_like(o_ref)

        o_ref[...] += jnp.dot(ht_ref[...], dy_ref[...], preferred_element_type=F32)

    return _hosted_call(
        main, name=name, grid=(n_chip, n // tk), jobs=jobs, semantics=("parallel", "arbitrary"), operands=(ht, dy),
        in_specs=[pl.BlockSpec((d, tk), lambda j, k: (0, k)), pl.BlockSpec((tk, ck), lambda j, k: (k, j))],
        out_specs=[pl.BlockSpec((None, d, ck), lambda j, k: (j, 0, 0))],
        out_shape=[jax.ShapeDtypeStruct((n_chip, d, ck), F32)])


def _mixer_bwd(dx, proj, cz, cw, lg, lb, ws, bmap, wout, *, seq, tm, name, jobs=()):
    n, d = dx.shape
    groups = ws.shape[0]
    gd = d // groups
    nxt = _halo_next(tm, n)
    n_tiles = n // tm

    def main(ins, outs, scratch):
        proj_ref, cz_ref, nbg_ref, nga_ref, dx_ref, dxn_ref, cw_ref, lg_ref, lb_ref, ws_ref, bmap_ref, wout_ref = ins
        dproj_ref, dwout_ref, dcw_ref, dlg_ref, dlb_ref, dws_ref, dbs_ref = outs
        vn_s, mixed_s, dmix_s, dvn_s, dbmap_s = scratch
        i = pl.program_id(0)
        keep_next = jnp.where(((i + 1) * tm) % seq == 0, 0.0, 1.0)

        @pl.when(i == 0)
        def _():
            for ref in (dwout_ref, dcw_ref, dlg_ref, dlb_ref, dws_ref, dbmap_s):
                ref[...] = jnp.zeros_like(ref)

        def piece(k):
            return proj_ref[:, k * d:(k + 1) * d].astype(F32)

        def put(k, val):
            dproj_ref[:, k * d:(k + 1) * d] = val.astype(BF16)

        w = [cw_ref[k:k + 1, :] for k in range(3)]
        cg, xi = piece(1), piece(2)
        cz = cz_ref[...].astype(F32)
        bg = piece(0)
        ya = bg * cz

        v = piece(4)
        xc = v - _lanemean(v)
        rstd = lax.rsqrt(_lanemean(xc * xc) + EPS)
        vhat = xc * rstd
        vn_s[...] = (vhat * lg_ref[...] + lb_ref[...]).astype(BF16)
        tril, wsm = _tril_weights(ws_ref, groups)
        for ck in range(tm // CHUNK):
            rows = slice(ck * CHUNK, (ck + 1) * CHUNK)
            for g in range(groups):
                cols = slice(g * gd, (g + 1) * gd)
                mixed_s[rows, cols] = (jnp.dot(wsm[g].astype(BF16), vn_s[rows, cols], preferred_element_type=F32)
                                       + bmap_ref[:, cols])
        u = piece(3)
        mixed = mixed_s[...]
        yb = u * mixed
        sa, sb = jax.nn.sigmoid(piece(5)), jax.nn.sigmoid(piece(6))
        merged = sa * ya + sb * yb

        dxe = jnp.concatenate([dx_ref[...], dxn_ref[...]], axis=0).astype(BF16)
        dme = lax.dot_general(dxe, wout_ref[...], NT_DIMS, preferred_element_type=F32)
        dm, dm_n = dme[:tm], dme[tm:]
        dwout_ref[...] += lax.dot_general(merged.astype(BF16), dxe[:tm], TN_DIMS, preferred_element_type=F32)

        put(5, dm * ya * sa * (1.0 - sa))
        put(6, dm * yb * sb * (1.0 - sb))
        d_ya, d_yb = dm * sa, dm * sb
        put(0, d_ya * cz)
        d_cz = d_ya * bg
        d_cz_n = dm_n * jax.nn.sigmoid(nga_ref[...].astype(F32)) * nbg_ref[...].astype(F32) * keep_next
        d_cz1, d_cz2 = _shift_next(d_cz, d_cz_n, 1), _shift_next(d_cz, d_cz_n, 2)
        z = cg * xi
        for k, dk in enumerate((d_cz2, d_cz1, d_cz)):
            dcw_ref[k:k + 1, :] += _rowsum(dk * z)
        dz = w[2] * d_cz + w[1] * d_cz1 + w[0] * d_cz2
        put(1, dz * xi)
        put(2, dz * cg)

        put(3, d_yb * mixed)
        d_mixed = d_yb * u
        dmix_s[...] = d_mixed.astype(BF16)
        for ck in range(tm // CHUNK):
            rows = slice(ck * CHUNK, (ck + 1) * CHUNK)
            dbmap_s[...] += d_mixed[rows, :]
            for g in range(groups):
                cols = slice(g * gd, (g + 1) * gd)
                dvn_s[rows, cols] = jnp.dot(wsm[g].T.astype(BF16), dmix_s[rows, cols], preferred_element_type=F32)
                dws_ref[g] += jnp.where(
                    tril, lax.dot_general(dmix_s[rows, cols], vn_s[rows, cols], NT_DIMS, preferred_element_type=F32),
                    0.0)
        d_vn = dvn_s[...]
        dlg_ref[...] += _rowsum(d_vn * vhat)
        dlb_ref[...] += _rowsum(d_vn)
        d_vhat = d_vn * lg_ref[...]
        put(4, rstd * (d_vhat - _lanemean(d_vhat) - vhat * _lanemean(d_vhat * vhat)))

        @pl.when(i == n_tiles - 1)
        def _():
            for g in range(groups):
                dbs_ref[:, g:g + 1] = jnp.sum(dbmap_s[:, g * gd:(g + 1) * gd], axis=-1, keepdims=True)

    return _hosted_call(
        main, name=name, grid=(n_tiles,), jobs=jobs, semantics=("arbitrary",),
        operands=(proj, cz, proj, proj, dx, dx, cw, lg, lb, ws, bmap, wout),
        in_specs=[pl.BlockSpec((tm, N_PIECES * d), lambda i: (i, 0)),
                  pl.BlockSpec((tm, d), lambda i: (i, 0)),
                  pl.BlockSpec((HALO, d), lambda i: (nxt(i), 0)),
                  pl.BlockSpec((HALO, d), lambda i: (nxt(i), 5)),
                  pl.BlockSpec((tm, d), lambda i: (i, 0)),
                  pl.BlockSpec((HALO, d), lambda i: (nxt(i), 0)),
                  _full((3, d)), _full((1, d)), _full((1, d)), _full((groups, CHUNK, CHUNK)), _full((CHUNK, d)),
                  _resident((d, d), lambda i: (0, 0))],
        out_specs=[pl.BlockSpec((tm, N_PIECES * d), lambda i: (i, 0)), _full((d, d)),
                   _full((3, d)), _full((1, d)), _full((1, d)), _full((groups, CHUNK, CHUNK)), _full((CHUNK, groups))],
        out_shape=[jax.ShapeDtypeStruct((n, N_PIECES * d), BF16), jax.ShapeDtypeStruct((d, d), F32),
                   jax.ShapeDtypeStruct((3, d), F32), jax.ShapeDtypeStruct((1, d), F32),
                   jax.ShapeDtypeStruct((1, d), F32), jax.ShapeDtypeStruct((groups, CHUNK, CHUNK), F32),
                   jax.ShapeDtypeStruct((CHUNK, groups), F32)],
        scratch_shapes=[pltpu.VMEM((tm, d), BF16), pltpu.VMEM((tm, d), F32), pltpu.VMEM((tm, d), BF16),
                        pltpu.VMEM((tm, d), F32), pltpu.VMEM((CHUNK, d), F32)])


def _all_reduce_small(pack, name, jobs=()):
    rows = pack.shape[0]
    rh = rows // 2
    assert rh % HALO == 0, rows

    def main(ins, outs, scratch):
        p_ref, = ins
        o_ref, = outs
        sib_buf, chip_buf, got_buf, send_sem, recv_sem = scratch
        sems = (send_sem, recv_sem)
        x, y, c = _mesh_pos()
        chip, sibling = 2 * x + y, (x, y, 1 - c)
        mine = pl.ds(pl.multiple_of(c * rh, HALO), rh)
        theirs = pl.ds(pl.multiple_of((1 - c) * rh, HALO), rh)
        swap = _remote(p_ref.at[theirs], sib_buf, sems, 0, sibling)
        swap.start()
        swap.wait()
        chip_buf[chip] = p_ref[mine, :] + sib_buf[...]
        copies = []
        for r in (1, 2, 3):
            px, py = _peer_chip(x, y, r)
            copies.append(_remote(chip_buf.at[chip], chip_buf.at[chip], sems, r, (px, py, c)))
        for cp in copies:
            cp.start()
        for r, cp in zip((1, 2, 3), copies):
            px, py = _peer_chip(x, y, r)
            landed = chip_buf.at[2 * px + py]
            _remote(landed, landed, sems, r, (px, py, c)).wait_recv()
            cp.wait_send()
        total = ((chip_buf[0] + chip_buf[1]) + chip_buf[2]) + chip_buf[3]
        o_ref[mine, :] = total
        chip_buf[chip] = total
        share = _remote(chip_buf.at[chip], got_buf, sems, 4, sibling)
        share.start()
        share.wait()
        o_ref[theirs, :] = got_buf[...]

    outs, job_outs = _hosted_call(
        main, name=name, grid=(1,), jobs=jobs, operands=(pack,),
        in_specs=[_full(pack.shape)], out_specs=[_full(pack.shape)],
        out_shape=[jax.ShapeDtypeStruct(pack.shape, pack.dtype)],
        scratch_shapes=[pltpu.VMEM((rh, 128), F32), pltpu.VMEM((N_CHIP, rh, 128), F32), pltpu.VMEM((rh, 128), F32),
                        pltpu.SemaphoreType.DMA((5,)), pltpu.SemaphoreType.DMA((5,))])
    return outs[0], job_outs


SUM_BLOCKS = 2


def _pair_sum(grads, recvs, core, name):
    n_t, nb = len(grads), SUM_BLOCKS
    n_chip = grads[0].shape[0]

    def body(core_ref, *refs):
        for t in range(n_t):
            refs[2 * n_t + t][...] = (refs[t][...] + refs[n_t + t][...]).astype(BF16)

    def blk(g):
        rh = g.shape[1] // 2
        assert rh % (16 * nb) == 0, g.shape
        return (None, rh // nb, g.shape[2])

    mine = [pl.BlockSpec(blk(g), lambda k, i, core_ref: (k, core_ref[0] * nb + i, 0)) for g in grads]
    plain = [pl.BlockSpec(blk(g), lambda k, i, core_ref: (k, i, 0)) for g in grads]
    return pl.pallas_call(
        body, name=name,
        grid_spec=pltpu.PrefetchScalarGridSpec(num_scalar_prefetch=1, grid=(n_chip, nb), in_specs=mine + plain,
                                               out_specs=plain),
        out_shape=[jax.ShapeDtypeStruct((n_chip, g.shape[1] // 2, g.shape[2]), BF16) for g in grads],
        compiler_params=pltpu.CompilerParams(dimension_semantics=("parallel", "parallel"),
                                             vmem_limit_bytes=VMEM_LIMIT_V7X),
    )(core, *grads, *recvs)


def _final_sum(grads, recvs, arriveds, where, prevs, layer, n_layers, name):
    n_t, nb = len(grads), SUM_BLOCKS
    extra = [] if prevs is None else list(prevs)

    def body(where_ref, *refs):
        outs = refs[len(refs) - n_t:]
        for t in range(n_t):
            g_ref, r_ref = refs[t], refs[n_t + t]
            a1, a2, a3 = (refs[(2 + s) * n_t + t] for s in range(3))
            own = g_ref[...] + r_ref[...]
            outs[t][...] = ((own + a1[...].astype(F32)) + a2[...].astype(F32)) + a3[...].astype(F32)

    def blk(g):
        return (None, g.shape[1] // 2 // nb, g.shape[2])

    in_specs = ([pl.BlockSpec(blk(g), lambda i, w: (w[1], w[0] * nb + i, 0)) for g in grads]
                + [pl.BlockSpec(blk(g), lambda i, w: (w[1], i, 0)) for g in grads]
                + [pl.BlockSpec(blk(g), lambda i, w, s=s: (s, i, 0)) for s in range(3) for g in grads]
                + [pl.BlockSpec(memory_space=pl.ANY)] * len(extra))
    return pl.pallas_call(
        body, name=name,
        grid_spec=pltpu.PrefetchScalarGridSpec(
            num_scalar_prefetch=1, grid=(nb,), in_specs=in_specs,
            out_specs=[pl.BlockSpec(blk(g), lambda i, w: (layer, w[0] * nb + i, 0)) for g in grads]),
        out_shape=[jax.ShapeDtypeStruct((n_layers,) + g.shape[1:], F32) for g in grads],
        input_output_aliases={1 + 5 * n_t + t: t for t in range(n_t)} if extra else {},
        compiler_params=pltpu.CompilerParams(dimension_semantics=("parallel",), vmem_limit_bytes=VMEM_LIMIT_V7X),
    )(where, *grads, *recvs, *arriveds, *arriveds, *arriveds, *extra)


def _adamw_math(w, g, m, v):
    m = ADAM_B1 * m + (1.0 - ADAM_B1) * g
    v = ADAM_B2 * v + (1.0 - ADAM_B2) * (g * g)
    m_hat = m / (1.0 - ADAM_B1 ** ADAM_STEP)
    v_hat = v / (1.0 - ADAM_B2 ** ADAM_STEP)
    delta = -ADAM_LR * (m_hat / (jnp.sqrt(v_hat) + ADAM_EPS) + ADAM_WD * w)
    return delta, m, v


ADAMW_BLOCKS = 8


def _adamw_big(ws, gs, ms, vs, name):
    n_t, nb = len(ws), ADAMW_BLOCKS
    n_l = ws[0].shape[0]

    def body(*refs):
        for t in range(n_t):
            res = _adamw_math(*(refs[q * n_t + t][...] for q in range(4)))
            for q in range(3):
                refs[(4 + q) * n_t + t][...] = res[q]

    def spec(w):
        assert w.shape[1] % (HALO * nb) == 0, w.shape
        return pl.BlockSpec((None, w.shape[1] // nb, w.shape[2]), lambda l, i: (l, i, 0))

    specs = [spec(w) for w in ws]
    outs = pl.pallas_call(
        body, name=name, grid=(n_l, nb), in_specs=specs * 4, out_specs=specs * 3,
        out_shape=[jax.ShapeDtypeStruct(w.shape, F32) for w in ws] * 3,
        compiler_params=pltpu.CompilerParams(dimension_semantics=("parallel", "parallel"),
                                             vmem_limit_bytes=VMEM_LIMIT_V7X),
    )(*ws, *gs, *ms, *vs)
    return outs[:n_t], outs[n_t:2 * n_t], outs[2 * n_t:]


def _adamw_small(ws, gs, ms, vs, name):
    n_p = len(ws)

    def body(*refs):
        ins, outs = refs[:4 * n_p], refs[4 * n_p:]
        for p in range(n_p):
            res = _adamw_math(ins[p][...], ins[n_p + p][...], ins[2 * n_p + p][...], ins[3 * n_p + p][...])
            for q in range(3):
                outs[q * n_p + p][...] = res[q]

    outs = pl.pallas_call(
        body, name=name, in_specs=[VMEM_SPEC] * (4 * n_p), out_specs=[VMEM_SPEC] * (3 * n_p),
        out_shape=[jax.ShapeDtypeStruct(w.shape, F32) for w in ws] * 3,
        compiler_params=pltpu.CompilerParams(vmem_limit_bytes=VMEM_LIMIT_V7X),
    )(*ws, *gs, *ms, *vs)
    return outs[:n_p], outs[n_p:2 * n_p], outs[2 * n_p:]


def kernel(x, mix_norm_g, w_in, conv_a_w, ln_v_g, ln_v_b, w_s, b_s, w_out, ffn_norm_g, w_up, conv_ffn_w, w_down, final_norm_g, loss_target, m_mix_norm_g, m_w_in, m_conv_a_w, m_ln_v_g, m_ln_v_b, m_w_s, m_b_s, m_w_out, m_ffn_norm_g, m_w_up, m_conv_ffn_w, m_w_down, m_final_norm_g, v_mix_norm_g, v_w_in, v_conv_a_w, v_ln_v_g, v_ln_v_b, v_w_s, v_b_s, v_w_out, v_ffn_norm_g, v_w_up, v_conv_ffn_w, v_w_down, v_final_norm_g):
    bsz, seq, d = x.shape
    n = bsz * seq
    n_l, groups = w_s.shape[0], w_s.shape[1]
    assert n_l == 2, "the exchange schedule below is written for two layers"
    gd = d // groups
    ff = w_down.shape[1] * N_CHIP
    mx, my, mc = _mesh_pos()
    chip = 2 * mx + my
    core = jnp.reshape(mc, (1,)).astype(jnp.int32)
    where = jnp.stack([mc, chip]).astype(jnp.int32)

    tm_mm = _row_tile(seq, 512)
    tm_ew = _row_tile(seq, 256)
    tk_w = _row_tile(seq, 2048)
    ff_chunk = ff // 2 if (ff // 2) % 128 == 0 else ff
    IN, OUT, UP, DOWN = range(4)

    shards = [w.astype(BF16) for w in (w_in, w_out, w_up, w_down)]
    piece = lambda a, l: (shards[a], l)
    (w_in_0,), = _comm_only([_GatherJob([piece(IN, 0)])], "gather_first")

    taps = jnp.concatenate([conv_a_w.reshape(n_l, -1), conv_ffn_w.reshape(n_l, -1)], axis=1)
    tap_rows = -(-taps.size // 128 // 8) * 8
    tap_pack = jnp.zeros((tap_rows * 128,), F32).at[:taps.size].set(taps.reshape(-1)).reshape(tap_rows, 128)

    bmaps = jnp.repeat(jnp.swapaxes(b_s, 1, 2), gd, axis=2)

    xs = x.reshape(n, d)
    tgt = loss_target.reshape(n, d)

    (proj0, h1_0), ((w_out_0, w_up_0), (tap_all,)) = _norm_matmul(
        xs, mix_norm_g[0][None], w_in_0, tm=tm_mm, name="fwd_in_proj_0",
        jobs=[_GatherJob([piece(OUT, 0), piece(UP, 0)]), _SmallGatherJob([tap_pack])])
    tap_all = tap_all.reshape(N_CHIP, -1)[:, :taps.size].reshape(N_CHIP, n_l, -1)
    ca = tap_all[:, :, :3 * d // N_CHIP].reshape(N_CHIP, n_l, 3, d // N_CHIP)
    cf = tap_all[:, :, 3 * d // N_CHIP:].reshape(N_CHIP, n_l, 3, 2 * ff // N_CHIP)
    conv_a_full = jnp.transpose(ca, (1, 2, 0, 3)).reshape(n_l, 3, d)
    conv_f_full = jnp.transpose(cf, (1, 2, 0, 3)).reshape(n_l, 3, 2 * ff)

    def mixer_args(l, w_out_l):
        return (conv_a_full[l], ln_v_g[l][None], ln_v_b[l][None], w_s[l], bmaps[l], w_out_l.reshape(d, d))

    (x1_0, cz0), ((w_down_0, w_out_1),) = _mixer_fwd(
        proj0, xs, *mixer_args(0, w_out_0), seq=seq, tm=tm_ew, name="fwd_mixer_0",
        jobs=[_GatherJob([piece(DOWN, 0), piece(OUT, 1)])])
    (up0, h2_0), ((w_in_1,),) = _norm_matmul(
        x1_0, ffn_norm_g[0][None], w_up_0, tm=tm_mm, name="fwd_up_proj_0", jobs=[_GatherJob([piece(IN, 1)])])
    (x2_0, conv0), _ = _ffn_fwd(up0, x1_0, conv_f_full[0], w_down_0.reshape(ff, d), seq=seq, tm=tm_ew, cwid=ff_chunk,
                                name="fwd_ffn_0")
    (proj1, h1_1), ((w_up_1, w_down_1),) = _norm_matmul(
        x2_0, mix_norm_g[1][None], w_in_1, tm=tm_mm, name="fwd_in_proj_1",
        jobs=[_GatherJob([piece(UP, 1), piece(DOWN, 1)])])
    (x1_1, cz1), _ = _mixer_fwd(proj1, x2_0, *mixer_args(1, w_out_1), seq=seq, tm=tm_ew, name="fwd_mixer_1")
    (up1, h2_1), _ = _norm_matmul(x1_1, ffn_norm_g[1][None], w_up_1, tm=tm_mm, name="fwd_up_proj_1")
    (x2_1, conv1), _ = _ffn_fwd(up1, x1_1, conv_f_full[1], w_down_1.reshape(ff, d), seq=seq, tm=tm_ew, cwid=ff_chunk,
                                name="fwd_ffn_1")
    dx, loss_tile, d_final_g = _final_loss(x2_1, final_norm_g[None], tgt, tm=tm_mm, name="final_loss")

    def chipwise(a):
        return a.reshape(N_CHIP, a.shape[0] // N_CHIP, a.shape[1])

    def pair_sums(grads, recvs, tag):
        return _pair_sum(grads, recvs, core, f"grad_pair_sum_{tag}")

    d_up, dwd, d_cf1 = _ffn_bwd(dx, up1, conv1, conv_f_full[1], w_down_1.reshape(ff, d), seq=seq, tm=tm_ew,
                                cwid=ff_chunk, name="bwd_ffn_1")
    (dx1, d_g2_1), _ = _matmul_bwd_x(d_up, w_up_1, x1_1, ffn_norm_g[1][None], dx, tm=tm_mm, name="bwd_up_x_1")
    (dwu,), _ = _matmul_bwd_w(h2_1, d_up, N_CHIP, tk=tk_w, name="bwd_up_w_1")
    (d_proj, dwo, d_ca1, d_lg1, d_lb1, d_ws1, d_bs1), _ = _mixer_bwd(
        dx1, proj1, cz1, *mixer_args(1, w_out_1), seq=seq, tm=tm_ew, name="bwd_mixer_1")
    (dx, d_g1_1), _ = _matmul_bwd_x(d_proj, w_in_1, x2_0, mix_norm_g[1][None], dx1, tm=tm_mm, name="bwd_in_x_1")
    (dwi,), _ = _matmul_bwd_w(h1_1, d_proj, N_CHIP, tk=tk_w, name="bwd_in_w_1")
    grads1 = [dwi, chipwise(dwo), dwu, chipwise(dwd)]
    small1 = [d_g1_1, d_ca1, d_lg1, d_lb1, d_ws1, d_bs1.T, d_g2_1, d_cf1]

    d_up, dwd, d_cf0 = _ffn_bwd(dx, up0, conv0, conv_f_full[0], w_down_0.reshape(ff, d), seq=seq, tm=tm_ew,
                                cwid=ff_chunk, name="bwd_ffn_0")
    (dx1, d_g2_0), (recv1,) = _matmul_bwd_x(d_up, w_up_0, x1_0, ffn_norm_g[0][None], dx, tm=tm_mm, name="bwd_up_x_0",
                                            jobs=[_SwapJob(grads1)])
    parts1 = pair_sums(grads1, recv1, 1)
    (dwu,), ((arr1_in,),) = _matmul_bwd_w(h2_0, d_up, N_CHIP, tk=tk_w, name="bwd_up_w_0",
                                          jobs=[_ScatterJob([parts1[IN]])])
    ud0 = [dwu, chipwise(dwd)]
    (d_proj, dwo, d_ca0, d_lg0, d_lb0, d_ws0, d_bs0), (arr1_rest, recv0_ud) = _mixer_bwd(
        dx1, proj0, cz0, *mixer_args(0, w_out_0), seq=seq, tm=tm_ew, name="bwd_mixer_0",
        jobs=[_ScatterJob(parts1[OUT:]), _SwapJob(ud0)])
    arrived1 = [arr1_in] + list(arr1_rest)
    gsum = _final_sum(grads1, recv1, arrived1, where, None, 1, n_l, "grad_final_sum_1")
    parts0_ud = pair_sums(ud0, recv0_ud, "0ud")
    out0 = chipwise(dwo)
    (dwi,), (gsum, arr0_ud, (recv0_out,)) = _matmul_bwd_w(
        h1_0, d_proj, N_CHIP, tk=tk_w, name="bwd_in_w_0",
        jobs=[_ShareJob([(g, 1) for g in gsum]), _ScatterJob(parts0_ud), _SwapJob([out0])])
    part0_out, = pair_sums([out0], [recv0_out], "0o")
    n_mm = n // tm_mm
    n_head = max(n_mm // 4, 1)
    head, ((arr0_out,), (recv0_in,)) = _matmul_bwd_x(
        d_proj, w_in_0, xs, mix_norm_g[0][None], dx1, tm=tm_mm, name="bwd_in_x_0a", tiles=(0, n_head),
        jobs=[_ScatterJob([part0_out]), _SwapJob([dwi])])
    part0_in, = pair_sums([dwi], [recv0_in], "0i")
    (dx, d_g1_0), ((arr0_in,),) = _matmul_bwd_x(
        d_proj, w_in_0, xs, mix_norm_g[0][None], dx1, tm=tm_mm, name="bwd_in_x_0b", tiles=(n_head, n_mm - n_head),
        before=head, jobs=[_ScatterJob([part0_in])])
    grad_x = dx.reshape(bsz, seq, d)
    grads0 = [dwi, out0] + ud0
    recv0 = [recv0_in, recv0_out] + list(recv0_ud)
    arrived0 = [arr0_in, arr0_out] + list(arr0_ud)
    small0 = [d_g1_0, d_ca0, d_lg0, d_lb0, d_ws0, d_bs0.T, d_g2_0, d_cf0]

    flat = [a.reshape(-1) for a in small0 + small1] + [d_final_g.reshape(-1), loss_tile[0]]
    sizes = [a.size for a in flat]
    total = sum(sizes)
    rows = -(-total // 128 // (2 * HALO)) * 2 * HALO
    pack = jnp.concatenate(flat + [jnp.zeros((rows * 128 - total,), F32)]).reshape(rows, 128)
    red, _ = _all_reduce_small(pack, "small_all_reduce")
    red = red.reshape(-1)
    gsum = _final_sum(grads0, recv0, arrived0, where, gsum, 0, n_l, "grad_final_sum_0")
    (g_in, g_out, g_up, g_down), = _comm_only([_ShareJob([(g, 0) for g in gsum])], "grad_share_0")

    pieces, off = [], 0
    for s in sizes:
        pieces.append(red[off:off + s])
        off += s
    per_layer = [pieces[l * 8:(l + 1) * 8] for l in range(n_l)]

    def stacked(idx, shape):
        return jnp.stack([per_layer[l][idx].reshape(shape) for l in range(n_l)])

    def my_cols(a, width):
        return lax.dynamic_slice_in_dim(a, chip * width, width, axis=-1)

    g_mix_norm = stacked(0, (d,))
    g_conv_a = my_cols(stacked(1, (3, d)), d // N_CHIP)
    g_ln_g = stacked(2, (d,))
    g_ln_b = stacked(3, (d,))
    g_ws = stacked(4, (groups, CHUNK, CHUNK))
    g_bs = stacked(5, (groups, CHUNK))
    g_ffn_norm = stacked(6, (d,))
    g_conv_f = my_cols(stacked(7, (3, 2 * ff)), 2 * ff // N_CHIP)
    g_final = pieces[-2].reshape(1, d)
    loss = pieces[-1][0]

    big_d, big_nm, big_nv = _adamw_big([w_in, w_out, w_up, w_down], [g_in, g_out, g_up, g_down],
                                       [m_w_in, m_w_out, m_w_up, m_w_down], [v_w_in, v_w_out, v_w_up, v_w_down],
                                       "adamw_big")

    sm_w = [mix_norm_g, conv_a_w, ln_v_g, ln_v_b, w_s, b_s, ffn_norm_g, conv_ffn_w, final_norm_g[None]]
    sm_g = [g_mix_norm, g_conv_a, g_ln_g, g_ln_b, g_ws, g_bs, g_ffn_norm, g_conv_f, g_final]
    sm_m = [m_mix_norm_g, m_conv_a_w, m_ln_v_g, m_ln_v_b, m_w_s, m_b_s, m_ffn_norm_g, m_conv_ffn_w, m_final_norm_g[None]]
    sm_v = [v_mix_norm_g, v_conv_a_w, v_ln_v_g, v_ln_v_b, v_w_s, v_b_s, v_ffn_norm_g, v_conv_ffn_w, v_final_norm_g[None]]
    sm_d, sm_nm, sm_nv = _adamw_small(sm_w, sm_g, sm_m, sm_v, "adamw_small")

    def ordered(sm, bigs):
        return [sm[0], bigs[0], sm[1], sm[2], sm[3], sm[4], sm[5], bigs[1], sm[6], bigs[2], sm[7], bigs[3],
                sm[8].reshape(d)]

    out_g = ordered(sm_g, [g_in, g_out, g_up, g_down])
    out_d = ordered(sm_d, big_d)
    out_m = ordered(sm_nm, big_nm)
    out_v = ordered(sm_nv, big_nv)
    return (loss, grad_x, *out_g, *out_d, *out_m, *out_v)
```

```python
import jax
import jax.numpy as jnp
from jax import lax
from jax.experimental import pallas as pl
from jax.experimental.pallas import tpu as pltpu

F32 = jnp.float32
BF16 = jnp.bfloat16
EPS = 1e-6
CHUNK = 128
N_CHIP = 4
HALO = 8
N_PIECES = 7
LATE_STEPS = 2
VMEM_LIMIT_V7X = 56 * 1024 * 1024
MESH_T = pl.DeviceIdType.MESH
HBM_SPEC = pl.BlockSpec(memory_space=pltpu.HBM)
VMEM_SPEC = pl.BlockSpec(memory_space=pltpu.VMEM)

ADAM_LR, ADAM_B1, ADAM_B2, ADAM_EPS, ADAM_WD, ADAM_STEP = 0.001, 0.9, 0.999, 1e-08, 0.01, 10

NT_DIMS = (((1,), (1,)), ((), ()))
TN_DIMS = (((0,), (0,)), ((), ()))


def _resident(block_shape, index_map):
    return pl.BlockSpec(block_shape, index_map, pipeline_mode=pl.Buffered(1))


def _full(shape):
    return pl.BlockSpec(shape, lambda *_: (0,) * len(shape))


def _row_tile(seq, want):
    t = min(seq, want)
    assert seq % t == 0 and t % CHUNK == 0, (seq, want)
    return t


def _shift_prev(prev8, cur, k):
    ext = jnp.concatenate([prev8, cur], axis=0)
    return pltpu.roll(ext, k, 0)[HALO:]


def _shift_next(cur, next8, k):
    ext = jnp.concatenate([cur, next8], axis=0)
    n = ext.shape[0]
    return pltpu.roll(ext, n - k, 0)[:n - HALO]


def _rowsum(a):
    return jnp.sum(a, axis=0, keepdims=True)


def _lanemean(a):
    return jnp.mean(a, axis=-1, keepdims=True)


def _tril_weights(ws_ref, groups):
    r = lax.broadcasted_iota(jnp.int32, (CHUNK, CHUNK), 0)
    c = lax.broadcasted_iota(jnp.int32, (CHUNK, CHUNK), 1)
    tril = r >= c
    return tril, [jnp.where(tril, ws_ref[g], 0.0) for g in range(groups)]


def _halo_prev(tm):
    return lambda i: jnp.maximum(i * (tm // HALO) - 1, 0)


def _halo_next(tm, n):
    last = n // HALO - 1
    return lambda i: jnp.minimum((i + 1) * (tm // HALO), last)


def _mesh_pos():
    return lax.axis_index("x"), lax.axis_index("y"), lax.axis_index("c")


def _peer_chip(x, y, r):
    return (1 - x if r >> 1 else x), (1 - y if r & 1 else y)


def _remote(src, dst, sems, k, to):
    return pltpu.make_async_remote_copy(src_ref=src, dst_ref=dst, send_sem=sems[0].at[k], recv_sem=sems[1].at[k],
                                        device_id=to, device_id_type=MESH_T)


class _GatherJob:
    def __init__(self, pieces):
        self.pieces = pieces
        self.ins = [p[0] for p in pieces]
        self.out_shapes = [jax.ShapeDtypeStruct((N_CHIP,) + p[0].shape[1:], p[0].dtype) for p in pieces]
        self.aliases = {}
        n = len(pieces)
        self.sems = [pltpu.SemaphoreType.DMA((3 * n,))] * 4 + [pltpu.SemaphoreType.DMA((n,))]

    def _half(self, outs, a, of_chip, core):
        rh = outs[a].shape[1] // 2
        return outs[a].at[of_chip, pl.ds(core * rh, rh), :]

    def _own(self, ins, outs, sems, chip):
        return [pltpu.make_async_copy(ins[a].at[layer], outs[a].at[chip], sems[4].at[a])
                for a, (_, layer) in enumerate(self.pieces)]

    def _sends(self, ins, outs, sems, x, y, c):
        chip, out = 2 * x + y, []
        for a, (_, layer) in enumerate(self.pieces):
            rh = outs[a].shape[1] // 2
            for r in (1, 2, 3):
                px, py = _peer_chip(x, y, r)
                out.append(_remote(ins[a].at[layer, pl.ds(c * rh, rh), :], self._half(outs, a, chip, c), sems[0:2],
                                   3 * a + r - 1, (px, py, c)))
        return out

    def _passes(self, outs, sems, x, y, c, core):
        out = []
        for a in range(len(self.pieces)):
            for r in (1, 2, 3):
                px, py = _peer_chip(x, y, r)
                landed = self._half(outs, a, 2 * px + py, core)
                out.append(_remote(landed, landed, sems[2:4], 3 * a + r - 1, (x, y, 1 - c)))
        return out

    def start(self, ins, outs, sems):
        x, y, c = _mesh_pos()
        for cp in self._own(ins, outs, sems, 2 * x + y) + self._sends(ins, outs, sems, x, y, c):
            cp.start()

    def late(self, ins, outs, sems):
        x, y, c = _mesh_pos()
        passes = self._passes(outs, sems, x, y, c, c)
        k = 0
        for a in range(len(self.pieces)):
            for r in (1, 2, 3):
                px, py = _peer_chip(x, y, r)
                landed = self._half(outs, a, 2 * px + py, c)
                _remote(landed, landed, sems[0:2], k, (px, py, c)).wait_recv()
                passes[k].start()
                k += 1

    def finish(self, ins, outs, sems):
        x, y, c = _mesh_pos()
        for cp in self._passes(outs, sems, x, y, c, 1 - c):
            cp.wait_recv()
        for cp in self._sends(ins, outs, sems, x, y, c) + self._passes(outs, sems, x, y, c, c):
            cp.wait_send()
        for cp in self._own(ins, outs, sems, 2 * x + y):
            cp.wait()


class _SmallGatherJob:
    def __init__(self, pieces):
        self.ins = list(pieces)
        self.out_shapes = [jax.ShapeDtypeStruct((N_CHIP,) + p.shape, p.dtype) for p in pieces]
        self.aliases = {}
        n = len(pieces)
        self.sems = [pltpu.SemaphoreType.DMA((3 * n,))] * 2 + [pltpu.SemaphoreType.DMA((n,))]

    def _outgoing(self, ins, outs, sems):
        x, y, c = _mesh_pos()
        mine = [outs[a].at[2 * x + y] for a in range(len(ins))]
        own = [pltpu.make_async_copy(ins[a], mine[a], sems[2].at[a]) for a in range(len(ins))]
        sends = [_remote(ins[a], mine[a], sems[0:2], 3 * a + r - 1, (*_peer_chip(x, y, r), c))
                 for a in range(len(ins)) for r in (1, 2, 3)]
        return own, sends

    def start(self, ins, outs, sems):
        own, sends = self._outgoing(ins, outs, sems)
        for cp in own + sends:
            cp.start()

    def finish(self, ins, outs, sems):
        x, y, c = _mesh_pos()
        for a in range(len(ins)):
            for r in (1, 2, 3):
                px, py = _peer_chip(x, y, r)
                landed = outs[a].at[2 * px + py]
                _remote(landed, landed, sems[0:2], 3 * a + r - 1, (px, py, c)).wait_recv()
        own, sends = self._outgoing(ins, outs, sems)
        for cp in sends:
            cp.wait_send()
        for cp in own:
            cp.wait()


class _SwapJob:
    def __init__(self, pieces):
        self.ins = list(pieces)
        self.out_shapes = [jax.ShapeDtypeStruct((g.shape[0], g.shape[1] // 2, g.shape[2]), g.dtype) for g in pieces]
        self.aliases = {}
        self.sems = [pltpu.SemaphoreType.DMA((len(pieces),))] * 2

    def _copies(self, ins, outs, sems):
        x, y, c = _mesh_pos()
        out = []
        for a in range(len(ins)):
            rh = ins[a].shape[1] // 2
            out.append(_remote(ins[a].at[:, pl.ds((1 - c) * rh, rh), :], outs[a], sems, a, (x, y, 1 - c)))
        return out

    def start(self, ins, outs, sems):
        for cp in self._copies(ins, outs, sems):
            cp.start()

    def finish(self, ins, outs, sems):
        for cp in self._copies(ins, outs, sems):
            cp.wait()


class _ScatterJob:
    def __init__(self, pieces):
        self.ins = list(pieces)
        self.out_shapes = [jax.ShapeDtypeStruct((3,) + p.shape[1:], p.dtype) for p in pieces]
        self.aliases = {}
        self.sems = [pltpu.SemaphoreType.DMA((3 * len(pieces),))] * 2

    def _copies(self, ins, outs, sems):
        x, y, c = _mesh_pos()
        out = []
        for a in range(len(ins)):
            for r in (1, 2, 3):
                px, py = _peer_chip(x, y, r)
                out.append(_remote(ins[a].at[2 * px + py], outs[a].at[r - 1], sems, 3 * a + r - 1, (px, py, c)))
        return out

    def start(self, ins, outs, sems):
        for cp in self._copies(ins, outs, sems):
            cp.start()

    def finish(self, ins, outs, sems):
        for cp in self._copies(ins, outs, sems):
            cp.wait()


class _ShareJob:
    def __init__(self, pieces):
        self.pieces = pieces
        self.ins = [p[0] for p in pieces]
        self.out_shapes = [jax.ShapeDtypeStruct(p[0].shape, p[0].dtype) for p in pieces]
        self.aliases = {a: a for a in range(len(pieces))}
        self.sems = [pltpu.SemaphoreType.DMA((len(pieces),))] * 2

    def _copies(self, outs, sems, core):
        x, y, c = _mesh_pos()
        out = []
        for a, (_, layer) in enumerate(self.pieces):
            rh = outs[a].shape[1] // 2
            rows = outs[a].at[layer, pl.ds(core * rh, rh), :]
            out.append(_remote(rows, rows, sems, a, (x, y, 1 - c)))
        return out

    def start(self, ins, outs, sems):
        for cp in self._copies(outs, sems, lax.axis_index("c")):
            cp.start()

    def finish(self, ins, outs, sems):
        c = lax.axis_index("c")
        for cp in self._copies(outs, sems, c):
            cp.wait_send()
        for cp in self._copies(outs, sems, 1 - c):
            cp.wait_recv()


def _hosted_call(main, *, name, grid, in_specs, out_specs, out_shape, scratch_shapes=(), jobs=(), semantics=None,
                 operands=(), main_aliases=None):
    n_in, n_out, n_sc = len(in_specs), len(out_specs), len(scratch_shapes)
    counts = [(len(j.ins), len(j.out_shapes), len(j.sems)) for j in jobs]
    j_in, j_out, j_sc = (sum(c[k] for c in counts) for k in range(3))
    aliases, i0, o0 = dict(main_aliases or {}), n_in, n_out
    for j, (ci, co, _) in zip(jobs, counts):
        aliases.update({i0 + a: o0 + b for a, b in j.aliases.items()})
        i0, o0 = i0 + ci, o0 + co

    def body(*refs):
        cuts = [0, n_in, n_in + j_in, n_in + j_in + n_out, n_in + j_in + n_out + j_out,
                n_in + j_in + n_out + j_out + n_sc, len(refs)]
        m_in, jb_in, m_out, jb_out, m_sc, jb_sc = (list(refs[cuts[k]:cuts[k + 1]]) for k in range(6))

        def run(phase):
            i0 = o0 = s0 = 0
            for j, (ci, co, cs) in zip(jobs, counts):
                if hasattr(j, phase):
                    getattr(j, phase)(jb_in[i0:i0 + ci], jb_out[o0:o0 + co], jb_sc[s0:s0 + cs])
                i0, o0, s0 = i0 + ci, o0 + co, s0 + cs

        step, n_steps = 0, 1
        for ax in range(len(grid)):
            step, n_steps = step * grid[ax] + pl.program_id(ax), n_steps * grid[ax]
        if jobs:
            pl.when(step == 0)(lambda: run("start"))
        main(m_in, m_out, m_sc)
        if jobs:
            pl.when(step == max(n_steps - 1 - LATE_STEPS, 0))(lambda: run("late"))
            pl.when(step == n_steps - 1)(lambda: run("finish"))

    if semantics is None or jobs:
        semantics = ("arbitrary",) * len(grid)
    outs = pl.pallas_call(
        body, name=name, grid=grid,
        in_specs=list(in_specs) + [HBM_SPEC] * j_in,
        out_specs=list(out_specs) + [HBM_SPEC] * j_out,
        out_shape=list(out_shape) + [s for j in jobs for s in j.out_shapes],
        scratch_shapes=list(scratch_shapes) + [s for j in jobs for s in j.sems],
        input_output_aliases=aliases,
        compiler_params=pltpu.CompilerParams(dimension_semantics=semantics, vmem_limit_bytes=VMEM_LIMIT_V7X),
    )(*operands, *[a for j in jobs for a in j.ins])
    main_outs, rest, job_outs = list(outs[:n_out]), list(outs[n_out:]), []
    for _, co, _ in counts:
        job_outs.append(rest[:co])
        rest = rest[co:]
    return main_outs, job_outs


def _comm_only(jobs, name):
    return _hosted_call(lambda i, o, s: None, name=name, grid=(1,), in_specs=[], out_specs=[], out_shape=[],
                        jobs=jobs)[1]


def _norm_matmul(x, g, wg, *, tm, name, jobs=()):
    n, d = x.shape
    n_chip, _, ck = wg.shape

    def main(ins, outs, _):
        x_ref, g_ref, w_ref = ins
        o_ref, ht_ref = outs
        xf = x_ref[...]
        r = lax.rsqrt(_lanemean(xf * xf) + EPS)
        h = xf * r * g_ref[...]
        ht_ref[...] = h.T.astype(BF16)
        hb = h.astype(BF16)
        for j in range(n_chip):
            o_ref[:, j * ck:(j + 1) * ck] = jnp.dot(hb, w_ref[j], preferred_element_type=F32).astype(BF16)

    return _hosted_call(
        main, name=name, grid=(n // tm,), jobs=jobs, semantics=("parallel",), operands=(x, g, wg),
        in_specs=[pl.BlockSpec((tm, d), lambda i: (i, 0)), _full((1, d)), _resident((n_chip, d, ck), lambda i: (0, 0, 0))],
        out_specs=[pl.BlockSpec((tm, n_chip * ck), lambda i: (i, 0)), pl.BlockSpec((d, tm), lambda i: (0, i))],
        out_shape=[jax.ShapeDtypeStruct((n, n_chip * ck), BF16), jax.ShapeDtypeStruct((d, n), BF16)])


def _mixer_fwd(proj, x, cw, lg, lb, ws, bmap, wout, *, seq, tm, name, jobs=()):
    n, d = x.shape
    groups = ws.shape[0]
    gd = d // groups
    prev = _halo_prev(tm)

    def main(ins, outs, scratch):
        proj_ref, pcg_ref, pxi_ref, x_ref, cw_ref, lg_ref, lb_ref, ws_ref, bmap_ref, wout_ref = ins
        o_ref, cz_ref = outs
        vn_s, mixed_s = scratch
        seq_start = (pl.program_id(0) * tm) % seq == 0

        def piece(k):
            return proj_ref[:, k * d:(k + 1) * d].astype(F32)

        z = piece(1) * piece(2)
        zprev = jnp.where(seq_start, 0.0, pcg_ref[...].astype(F32) * pxi_ref[...].astype(F32))
        cz = (cw_ref[0:1, :] * _shift_prev(zprev, z, 2) + cw_ref[1:2, :] * _shift_prev(zprev, z, 1)
              + cw_ref[2:3, :] * z)
        cz_ref[...] = cz.astype(BF16)
        ya = piece(0) * cz

        v = piece(4)
        xc = v - _lanemean(v)
        vn = xc * lax.rsqrt(_lanemean(xc * xc) + EPS) * lg_ref[...] + lb_ref[...]
        vn_s[...] = vn.astype(BF16)
        _, wsm = _tril_weights(ws_ref, groups)
        for ck in range(tm // CHUNK):
            rows = slice(ck * CHUNK, (ck + 1) * CHUNK)
            for g in range(groups):
                cols = slice(g * gd, (g + 1) * gd)
                mixed_s[rows, cols] = (jnp.dot(wsm[g].astype(BF16), vn_s[rows, cols], preferred_element_type=F32)
                                       + bmap_ref[:, cols])
        yb = piece(3) * mixed_s[...]
        merged = jax.nn.sigmoid(piece(5)) * ya + jax.nn.sigmoid(piece(6)) * yb
        o_ref[...] = x_ref[...] + jnp.dot(merged.astype(BF16), wout_ref[...], preferred_element_type=F32)

    return _hosted_call(
        main, name=name, grid=(n // tm,), jobs=jobs, semantics=("parallel",),
        operands=(proj, proj, proj, x, cw, lg, lb, ws, bmap, wout),
        in_specs=[pl.BlockSpec((tm, N_PIECES * d), lambda i: (i, 0)),
                  pl.BlockSpec((HALO, d), lambda i: (prev(i), 1)),
                  pl.BlockSpec((HALO, d), lambda i: (prev(i), 2)),
                  pl.BlockSpec((tm, d), lambda i: (i, 0)),
                  _full((3, d)), _full((1, d)), _full((1, d)), _full((groups, CHUNK, CHUNK)), _full((CHUNK, d)),
                  _resident((d, d), lambda i: (0, 0))],
        out_specs=[pl.BlockSpec((tm, d), lambda i: (i, 0)), pl.BlockSpec((tm, d), lambda i: (i, 0))],
        out_shape=[jax.ShapeDtypeStruct((n, d), F32), jax.ShapeDtypeStruct((n, d), BF16)],
        scratch_shapes=[pltpu.VMEM((tm, d), BF16), pltpu.VMEM((tm, d), F32)])


def _ffn_fwd(up, x, cw, wd, *, seq, tm, cwid, name, jobs=()):
    n, d = x.shape
    ff = wd.shape[0]
    prev = _halo_prev(tm)

    def main(ins, outs, _):
        up_ref, pup_ref, x_ref, cw_ref, wd_ref = ins
        o_ref, conv_ref = outs
        seq_start = (pl.program_id(0) * tm) % seq == 0

        def conv(lo):
            cols = slice(lo, lo + cwid)
            cur = up_ref[:, cols].astype(F32)
            pre = jnp.where(seq_start, 0.0, pup_ref[:, cols].astype(F32))
            out = (cw_ref[0:1, cols] * _shift_prev(pre, cur, 2) + cw_ref[1:2, cols] * _shift_prev(pre, cur, 1)
                   + cw_ref[2:3, cols] * cur)
            conv_ref[:, cols] = out.astype(BF16)
            return out

        acc = x_ref[...]
        for cj in range(ff // cwid):
            gate = conv(cj * cwid)
            val = conv(ff + cj * cwid)
            a = gate * jax.nn.sigmoid(gate) * val
            acc = acc + jnp.dot(a.astype(BF16), wd_ref[cj * cwid:(cj + 1) * cwid, :], preferred_element_type=F32)
        o_ref[...] = acc

    return _hosted_call(
        main, name=name, grid=(n // tm,), jobs=jobs, semantics=("parallel",), operands=(up, up, x, cw, wd),
        in_specs=[pl.BlockSpec((tm, 2 * ff), lambda i: (i, 0)),
                  pl.BlockSpec((HALO, 2 * ff), lambda i: (prev(i), 0)),
                  pl.BlockSpec((tm, d), lambda i: (i, 0)),
                  _full((3, 2 * ff)),
                  _resident((ff, d), lambda i: (0, 0))],
        out_specs=[pl.BlockSpec((tm, d), lambda i: (i, 0)), pl.BlockSpec((tm, 2 * ff), lambda i: (i, 0))],
        out_shape=[jax.ShapeDtypeStruct((n, d), F32), jax.ShapeDtypeStruct((n, 2 * ff), BF16)])


def _final_loss(x, g, target, *, tm, name):
    n, d = x.shape

    def body(x_ref, g_ref, t_ref, dx_ref, loss_ref, dg_ref):
        @pl.when(pl.program_id(0) == 0)
        def _():
            loss_ref[...] = jnp.zeros_like(loss_ref)
            dg_ref[...] = jnp.zeros_like(dg_ref)

        xf = x_ref[...]
        r = lax.rsqrt(_lanemean(xf * xf) + EPS)
        xhat = xf * r
        diff = xhat * g_ref[...] - t_ref[...]
        loss_ref[...] += (0.5 / d) * _rowsum(jnp.sum(diff * diff, axis=-1, keepdims=True))
        dy = diff * (1.0 / d)
        dg_ref[...] += _rowsum(dy * xhat)
        dyh = dy * g_ref[...]
        dx_ref[...] = r * (dyh - xhat * _lanemean(dyh * xhat))

    return pl.pallas_call(
        body, name=name, grid=(n // tm,),
        in_specs=[pl.BlockSpec((tm, d), lambda i: (i, 0)), _full((1, d)), pl.BlockSpec((tm, d), lambda i: (i, 0))],
        out_specs=[pl.BlockSpec((tm, d), lambda i: (i, 0)), _full((HALO, CHUNK)), _full((1, d))],
        out_shape=[jax.ShapeDtypeStruct((n, d), F32), jax.ShapeDtypeStruct((HALO, CHUNK), F32),
                   jax.ShapeDtypeStruct((1, d), F32)],
        compiler_params=pltpu.CompilerParams(dimension_semantics=("arbitrary",), vmem_limit_bytes=VMEM_LIMIT_V7X),
    )(x, g, target)


def _ffn_bwd(dx, up, conv, cw, wd, *, seq, tm, cwid, name):
    n, d = dx.shape
    ff = wd.shape[0]
    nxt = _halo_next(tm, n)

    def body(dx_ref, dxn_ref, up_ref, conv_ref, nconv_ref, cw_ref, wd_ref, dup_ref, dwd_ref, dcw_ref):
        i = pl.program_id(0)
        keep_next = jnp.where(((i + 1) * tm) % seq == 0, 0.0, 1.0)

        @pl.when(i == 0)
        def _():
            dwd_ref[...] = jnp.zeros_like(dwd_ref)
            dcw_ref[...] = jnp.zeros_like(dcw_ref)

        dxe = jnp.concatenate([dx_ref[...], dxn_ref[...]], axis=0).astype(BF16)
        dxb = dxe[:tm]
        for cj in range(ff // cwid):
            rows = slice(cj * cwid, (cj + 1) * cwid)
            g_cols, v_cols = slice(cj * cwid, (cj + 1) * cwid), slice(ff + cj * cwid, ff + (cj + 1) * cwid)
            dae = lax.dot_general(dxe, wd_ref[rows, :], NT_DIMS, preferred_element_type=F32)
            da, dan = dae[:tm], dae[tm:]

            def grads(gate, val, da_rows):
                sg = jax.nn.sigmoid(gate)
                sl = gate * sg
                return sl, da_rows * val * sg * (1.0 + gate * (1.0 - sg)), da_rows * sl

            gate, val = conv_ref[:, g_cols].astype(F32), conv_ref[:, v_cols].astype(F32)
            sl, d_gate, d_val = grads(gate, val, da)
            dwd_ref[rows, :] += lax.dot_general((sl * val).astype(BF16), dxb, TN_DIMS, preferred_element_type=F32)
            _, d_gate_n, d_val_n = grads(nconv_ref[:, g_cols].astype(F32), nconv_ref[:, v_cols].astype(F32),
                                         dan * keep_next)
            for cols, dcur, dnext in ((g_cols, d_gate, d_gate_n), (v_cols, d_val, d_val_n)):
                upc = up_ref[:, cols].astype(F32)
                d1, d2 = _shift_next(dcur, dnext, 1), _shift_next(dcur, dnext, 2)
                for k, dk in enumerate((d2, d1, dcur)):
                    dcw_ref[k:k + 1, cols] += _rowsum(dk * upc)
                dup = cw_ref[2:3, cols] * dcur + cw_ref[1:2, cols] * d1 + cw_ref[0:1, cols] * d2
                dup_ref[:, cols] = dup.astype(BF16)

    return pl.pallas_call(
        body, name=name, grid=(n // tm,),
        in_specs=[pl.BlockSpec((tm, d), lambda i: (i, 0)),
                  pl.BlockSpec((HALO, d), lambda i: (nxt(i), 0)),
                  pl.BlockSpec((tm, 2 * ff), lambda i: (i, 0)),
                  pl.BlockSpec((tm, 2 * ff), lambda i: (i, 0)),
                  pl.BlockSpec((HALO, 2 * ff), lambda i: (nxt(i), 0)),
                  _full((3, 2 * ff)),
                  _resident((ff, d), lambda i: (0, 0))],
        out_specs=[pl.BlockSpec((tm, 2 * ff), lambda i: (i, 0)), _full((ff, d)), _full((3, 2 * ff))],
        out_shape=[jax.ShapeDtypeStruct((n, 2 * ff), BF16), jax.ShapeDtypeStruct((ff, d), F32),
                   jax.ShapeDtypeStruct((3, 2 * ff), F32)],
        compiler_params=pltpu.CompilerParams(dimension_semantics=("arbitrary",), vmem_limit_bytes=VMEM_LIMIT_V7X),
    )(dx, dx, up, conv, conv, cw, wd)


def _matmul_bwd_x(dy, wg, x, g, dres, *, tm, name, jobs=(), tiles=None, before=None):
    n, d = x.shape
    n_chip, _, ck = wg.shape
    first_tile, n_tiles = tiles or (0, n // tm)
    extra = [] if before is None else list(before)

    def main(ins, outs, _):
        dy_ref, w_ref, x_ref, g_ref, dres_ref = ins[:5]
        dx_ref, dg_ref = outs

        @pl.when(pl.program_id(0) == 0)
        def _():
            dg_ref[...] = jnp.zeros_like(dg_ref) if before is None else ins[6][...]

        dh = lax.dot_general(dy_ref[:, 0:ck], w_ref[0], NT_DIMS, preferred_element_type=F32)
        for j in range(1, n_chip):
            dh = dh + lax.dot_general(dy_ref[:, j * ck:(j + 1) * ck], w_ref[j], NT_DIMS, preferred_element_type=F32)
        xf = x_ref[...]
        r = lax.rsqrt(_lanemean(xf * xf) + EPS)
        xhat = xf * r
        dg_ref[...] += _rowsum(dh * xhat)
        dyh = dh * g_ref[...]
        dx_ref[...] = dres_ref[...] + r * (dyh - xhat * _lanemean(dyh * xhat))

    rows = lambda width: pl.BlockSpec((tm, width), lambda i: (i + first_tile, 0))
    return _hosted_call(
        main, name=name, grid=(n_tiles,), jobs=jobs, semantics=("arbitrary",), operands=(dy, wg, x, g, dres, *extra),
        in_specs=[rows(n_chip * ck), _resident((n_chip, d, ck), lambda i: (0, 0, 0)), rows(d), _full((1, d)), rows(d)]
        + ([] if before is None else [pl.BlockSpec(memory_space=pl.ANY), _full((1, d))]),
        out_specs=[rows(d), _full((1, d))],
        out_shape=[jax.ShapeDtypeStruct((n, d), F32), jax.ShapeDtypeStruct((1, d), F32)],
        main_aliases=None if before is None else {5: 0})


def _matmul_bwd_w(ht, dy, n_chip, *, tk, name, jobs=()):
    d, n = ht.shape
    ck = dy.shape[1] // n_chip

    def main(ins, outs, _):
        ht_ref, dy_ref = ins
        o_ref, = outs

        @pl.when(pl.program_id(1) == 0)
        def _():
            o_ref[...] = jnp.zeros_like(o_ref)

        o_ref[...] += jnp.dot(ht_ref[...], dy_ref[...], preferred_element_type=F32)

    return _hosted_call(
        main, name=name, grid=(n_chip, n // tk), jobs=jobs, semantics=("parallel", "arbitrary"), operands=(ht, dy),
        in_specs=[pl.BlockSpec((d, tk), lambda j, k: (0, k)), pl.BlockSpec((tk, ck), lambda j, k: (k, j))],
        out_specs=[pl.BlockSpec((None, d, ck), lambda j, k: (j, 0, 0))],
        out_shape=[jax.ShapeDtypeStruct((n_chip, d, ck), F32)])


def _mixer_bwd(dx, proj, cz, cw, lg, lb, ws, bmap, wout, *, seq, tm, name, jobs=()):
    n, d = dx.shape
    groups = ws.shape[0]
    gd = d // groups
    nxt = _halo_next(tm, n)
    n_tiles = n // tm

    def main(ins, outs, scratch):
        proj_ref, cz_ref, nbg_ref, nga_ref, dx_ref, dxn_ref, cw_ref, lg_ref, lb_ref, ws_ref, bmap_ref, wout_ref = ins
        dproj_ref, dwout_ref, dcw_ref, dlg_ref, dlb_ref, dws_ref, dbs_ref = outs
        vn_s, mixed_s, dmix_s, dvn_s, dbmap_s = scratch
        i = pl.program_id(0)
        keep_next = jnp.where(((i + 1) * tm) % seq == 0, 0.0, 1.0)

        @pl.when(i == 0)
        def _():
            for ref in (dwout_ref, dcw_ref, dlg_ref, dlb_ref, dws_ref, dbmap_s):
                ref[...] = jnp.zeros_like(ref)

        def piece(k):
            return proj_ref[:, k * d:(k + 1) * d].astype(F32)

        def put(k, val):
            dproj_ref[:, k * d:(k + 1) * d] = val.astype(BF16)

        w = [cw_ref[k:k + 1, :] for k in range(3)]
        cg, xi = piece(1), piece(2)
        cz = cz_ref[...].astype(F32)
        bg = piece(0)
        ya = bg * cz

        v = piece(4)
        xc = v - _lanemean(v)
        rstd = lax.rsqrt(_lanemean(xc * xc) + EPS)
        vhat = xc * rstd
        vn_s[...] = (vhat * lg_ref[...] + lb_ref[...]).astype(BF16)
        tril, wsm = _tril_weights(ws_ref, groups)
        for ck in range(tm // CHUNK):
            rows = slice(ck * CHUNK, (ck + 1) * CHUNK)
            for g in range(groups):
                cols = slice(g * gd, (g + 1) * gd)
                mixed_s[rows, cols] = (jnp.dot(wsm[g].astype(BF16), vn_s[rows, cols], preferred_element_type=F32)
                                       + bmap_ref[:, cols])
        u = piece(3)
        mixed = mixed_s[...]
        yb = u * mixed
        sa, sb = jax.nn.sigmoid(piece(5)), jax.nn.sigmoid(piece(6))
        merged = sa * ya + sb * yb

        dxe = jnp.concatenate([dx_ref[...], dxn_ref[...]], axis=0).astype(BF16)
        dme = lax.dot_general(dxe, wout_ref[...], NT_DIMS, preferred_element_type=F32)
        dm, dm_n = dme[:tm], dme[tm:]
        dwout_ref[...] += lax.dot_general(merged.astype(BF16), dxe[:tm], TN_DIMS, preferred_element_type=F32)

        put(5, dm * ya * sa * (1.0 - sa))
        put(6, dm * yb * sb * (1.0 - sb))
        d_ya, d_yb = dm * sa, dm * sb
        put(0, d_ya * cz)
        d_cz = d_ya * bg
        d_cz_n = dm_n * jax.nn.sigmoid(nga_ref[...].astype(F32)) * nbg_ref[...].astype(F32) * keep_next
        d_cz1, d_cz2 = _shift_next(d_cz, d_cz_n, 1), _shift_next(d_cz, d_cz_n, 2)
        z = cg * xi
        for k, dk in enumerate((d_cz2, d_cz1, d_cz)):
            dcw_ref[k:k + 1, :] += _rowsum(dk * z)
        dz = w[2] * d_cz + w[1] * d_cz1 + w[0] * d_cz2
        put(1, dz * xi)
        put(2, dz * cg)

        put(3, d_yb * mixed)
        d_mixed = d_yb * u
        dmix_s[...] = d_mixed.astype(BF16)
        for ck in range(tm // CHUNK):
            rows = slice(ck * CHUNK, (ck + 1) * CHUNK)
            dbmap_s[...] += d_mixed[rows, :]
            for g in range(groups):
                cols = slice(g * gd, (g + 1) * gd)
                dvn_s[rows, cols] = jnp.dot(wsm[g].T.astype(BF16), dmix_s[rows, cols], preferred_element_type=F32)
                dws_ref[g] += jnp.where(
                    tril, lax.dot_general(dmix_s[rows, cols], vn_s[rows, cols], NT_DIMS, preferred_element_type=F32),
                    0.0)
        d_vn = dvn_s[...]
        dlg_ref[...] += _rowsum(d_vn * vhat)
        dlb_ref[...] += _rowsum(d_vn)
        d_vhat = d_vn * lg_ref[...]
        put(4, rstd * (d_vhat - _lanemean(d_vhat) - vhat * _lanemean(d_vhat * vhat)))

        @pl.when(i == n_tiles - 1)
        def _():
            for g in range(groups):
                dbs_ref[:, g:g + 1] = jnp.sum(dbmap_s[:, g * gd:(g + 1) * gd], axis=-1, keepdims=True)

    return _hosted_call(
        main, name=name, grid=(n_tiles,), jobs=jobs, semantics=("arbitrary",),
        operands=(proj, cz, proj, proj, dx, dx, cw, lg, lb, ws, bmap, wout),
        in_specs=[pl.BlockSpec((tm, N_PIECES * d), lambda i: (i, 0)),
                  pl.BlockSpec((tm, d), lambda i: (i, 0)),
                  pl.BlockSpec((HALO, d), lambda i: (nxt(i), 0)),
                  pl.BlockSpec((HALO, d), lambda i: (nxt(i), 5)),
                  pl.BlockSpec((tm, d), lambda i: (i, 0)),
                  pl.BlockSpec((HALO, d), lambda i: (nxt(i), 0)),
                  _full((3, d)), _full((1, d)), _full((1, d)), _full((groups, CHUNK, CHUNK)), _full((CHUNK, d)),
                  _resident((d, d), lambda i: (0, 0))],
        out_specs=[pl.BlockSpec((tm, N_PIECES * d), lambda i: (i, 0)), _full((d, d)),
                   _full((3, d)), _full((1, d)), _full((1, d)), _full((groups, CHUNK, CHUNK)), _full((CHUNK, groups))],
        out_shape=[jax.ShapeDtypeStruct((n, N_PIECES * d), BF16), jax.ShapeDtypeStruct((d, d), F32),
                   jax.ShapeDtypeStruct((3, d), F32), jax.ShapeDtypeStruct((1, d), F32),
                   jax.ShapeDtypeStruct((1, d), F32), jax.ShapeDtypeStruct((groups, CHUNK, CHUNK), F32),
                   jax.ShapeDtypeStruct((CHUNK, groups), F32)],
        scratch_shapes=[pltpu.VMEM((tm, d), BF16), pltpu.VMEM((tm, d), F32), pltpu.VMEM((tm, d), BF16),
                        pltpu.VMEM((tm, d), F32), pltpu.VMEM((CHUNK, d), F32)])


def _all_reduce_small(pack, name, jobs=()):
    rows = pack.shape[0]
    rh = rows // 2
    assert rh % HALO == 0, rows

    def main(ins, outs, scratch):
        p_ref, = ins
        o_ref, = outs
        sib_buf, chip_buf, got_buf, send_sem, recv_sem = scratch
        sems = (send_sem, recv_sem)
        x, y, c = _mesh_pos()
        chip, sibling = 2 * x + y, (x, y, 1 - c)
        mine = pl.ds(pl.multiple_of(c * rh, HALO), rh)
        theirs = pl.ds(pl.multiple_of((1 - c) * rh, HALO), rh)
        swap = _remote(p_ref.at[theirs], sib_buf, sems, 0, sibling)
        swap.start()
        swap.wait()
        chip_buf[chip] = p_ref[mine, :] + sib_buf[...]
        copies = []
        for r in (1, 2, 3):
            px, py = _peer_chip(x, y, r)
            copies.append(_remote(chip_buf.at[chip], chip_buf.at[chip], sems, r, (px, py, c)))
        for cp in copies:
            cp.start()
        for r, cp in zip((1, 2, 3), copies):
            px, py = _peer_chip(x, y, r)
            landed = chip_buf.at[2 * px + py]
            _remote(landed, landed, sems, r, (px, py, c)).wait_recv()
            cp.wait_send()
        total = ((chip_buf[0] + chip_buf[1]) + chip_buf[2]) + chip_buf[3]
        o_ref[mine, :] = total
        chip_buf[chip] = total
        share = _remote(chip_buf.at[chip], got_buf, sems, 4, sibling)
        share.start()
        share.wait()
        o_ref[theirs, :] = got_buf[...]

    outs, job_outs = _hosted_call(
        main, name=name, grid=(1,), jobs=jobs, operands=(pack,),
        in_specs=[_full(pack.shape)], out_specs=[_full(pack.shape)],
        out_shape=[jax.ShapeDtypeStruct(pack.shape, pack.dtype)],
        scratch_shapes=[pltpu.VMEM((rh, 128), F32), pltpu.VMEM((N_CHIP, rh, 128), F32), pltpu.VMEM((rh, 128), F32),
                        pltpu.SemaphoreType.DMA((5,)), pltpu.SemaphoreType.DMA((5,))])
    return outs[0], job_outs


SUM_BLOCKS = 2


def _pair_sum(grads, recvs, core, name):
    n_t, nb = len(grads), SUM_BLOCKS
    n_chip = grads[0].shape[0]

    def body(core_ref, *refs):
        for t in range(n_t):
            refs[2 * n_t + t][...] = (refs[t][...] + refs[n_t + t][...]).astype(BF16)

    def blk(g):
        rh = g.shape[1] // 2
        assert rh % (16 * nb) == 0, g.shape
        return (None, rh // nb, g.shape[2])

    mine = [pl.BlockSpec(blk(g), lambda k, i, core_ref: (k, core_ref[0] * nb + i, 0)) for g in grads]
    plain = [pl.BlockSpec(blk(g), lambda k, i, core_ref: (k, i, 0)) for g in grads]
    return pl.pallas_call(
        body, name=name,
        grid_spec=pltpu.PrefetchScalarGridSpec(num_scalar_prefetch=1, grid=(n_chip, nb), in_specs=mine + plain,
                                               out_specs=plain),
        out_shape=[jax.ShapeDtypeStruct((n_chip, g.shape[1] // 2, g.shape[2]), BF16) for g in grads],
        compiler_params=pltpu.CompilerParams(dimension_semantics=("parallel", "parallel"),
                                             vmem_limit_bytes=VMEM_LIMIT_V7X),
    )(core, *grads, *recvs)


def _final_sum(grads, recvs, arriveds, where, prevs, layer, n_layers, name):
    n_t, nb = len(grads), SUM_BLOCKS
    extra = [] if prevs is None else list(prevs)

    def body(where_ref, *refs):
        outs = refs[len(refs) - n_t:]
        for t in range(n_t):
            g_ref, r_ref = refs[t], refs[n_t + t]
            a1, a2, a3 = (refs[(2 + s) * n_t + t] for s in range(3))
            own = g_ref[...] + r_ref[...]
            outs[t][...] = ((own + a1[...].astype(F32)) + a2[...].astype(F32)) + a3[...].astype(F32)

    def blk(g):
        return (None, g.shape[1] // 2 // nb, g.shape[2])

    in_specs = ([pl.BlockSpec(blk(g), lambda i, w: (w[1], w[0] * nb + i, 0)) for g in grads]
                + [pl.BlockSpec(blk(g), lambda i, w: (w[1], i, 0)) for g in grads]
                + [pl.BlockSpec(blk(g), lambda i, w, s=s: (s, i, 0)) for s in range(3) for g in grads]
                + [pl.BlockSpec(memory_space=pl.ANY)] * len(extra))
    return pl.pallas_call(
        body, name=name,
        grid_spec=pltpu.PrefetchScalarGridSpec(
            num_scalar_prefetch=1, grid=(nb,), in_specs=in_specs,
            out_specs=[pl.BlockSpec(blk(g), lambda i, w: (layer, w[0] * nb + i, 0)) for g in grads]),
        out_shape=[jax.ShapeDtypeStruct((n_layers,) + g.shape[1:], F32) for g in grads],
        input_output_aliases={1 + 5 * n_t + t: t for t in range(n_t)} if extra else {},
        compiler_params=pltpu.CompilerParams(dimension_semantics=("parallel",), vmem_limit_bytes=VMEM_LIMIT_V7X),
    )(where, *grads, *recvs, *arriveds, *arriveds, *arriveds, *extra)


def _adamw_math(w, g, m, v):
    m = ADAM_B1 * m + (1.0 - ADAM_B1) * g
    v = ADAM_B2 * v + (1.0 - ADAM_B2) * (g * g)
    m_hat = m / (1.0 - ADAM_B1 ** ADAM_STEP)
    v_hat = v / (1.0 - ADAM_B2 ** ADAM_STEP)
    delta = -ADAM_LR * (m_hat / (jnp.sqrt(v_hat) + ADAM_EPS) + ADAM_WD * w)
    return delta, m, v


ADAMW_BLOCKS = 8


def _adamw_big(ws, gs, ms, vs, name):
    n_t, nb = len(ws), ADAMW_BLOCKS
    n_l = ws[0].shape[0]

    def body(*refs):
        for t in range(n_t):
            res = _adamw_math(*(refs[q * n_t + t][...] for q in range(4)))
            for q in range(3):
                refs[(4 + q) * n_t + t][...] = res[q]

    def spec(w):
        assert w.shape[1] % (HALO * nb) == 0, w.shape
        return pl.BlockSpec((None, w.shape[1] // nb, w.shape[2]), lambda l, i: (l, i, 0))

    specs = [spec(w) for w in ws]
    outs = pl.pallas_call(
        body, name=name, grid=(n_l, nb), in_specs=specs * 4, out_specs=specs * 3,
        out_shape=[jax.ShapeDtypeStruct(w.shape, F32) for w in ws] * 3,
        compiler_params=pltpu.CompilerParams(dimension_semantics=("parallel", "parallel"),
                                             vmem_limit_bytes=VMEM_LIMIT_V7X),
    )(*ws, *gs, *ms, *vs)
    return outs[:n_t], outs[n_t:2 * n_t], outs[2 * n_t:]


def _adamw_small(ws, gs, ms, vs, name):
    n_p = len(ws)

    def body(*refs):
        ins, outs = refs[:4 * n_p], refs[4 * n_p:]
        for p in range(n_p):
            res = _adamw_math(ins[p][...], ins[n_p + p][...], ins[2 * n_p + p][...], ins[3 * n_p + p][...])
            for q in range(3):
                outs[q * n_p + p][...] = res[q]

    outs = pl.pallas_call(
        body, name=name, in_specs=[VMEM_SPEC] * (4 * n_p), out_specs=[VMEM_SPEC] * (3 * n_p),
        out_shape=[jax.ShapeDtypeStruct(w.shape, F32) for w in ws] * 3,
        compiler_params=pltpu.CompilerParams(vmem_limit_bytes=VMEM_LIMIT_V7X),
    )(*ws, *gs, *ms, *vs)
    return outs[:n_p], outs[n_p:2 * n_p], outs[2 * n_p:]


def kernel(x, mix_norm_g, w_in, conv_a_w, ln_v_g, ln_v_b, w_s, b_s, w_out, ffn_norm_g, w_up, conv_ffn_w, w_down, final_norm_g, loss_target, m_mix_norm_g, m_w_in, m_conv_a_w, m_ln_v_g, m_ln_v_b, m_w_s, m_b_s, m_w_out, m_ffn_norm_g, m_w_up, m_conv_ffn_w, m_w_down, m_final_norm_g, v_mix_norm_g, v_w_in, v_conv_a_w, v_ln_v_g, v_ln_v_b, v_w_s, v_b_s, v_w_out, v_ffn_norm_g, v_w_up, v_conv_ffn_w, v_w_down, v_final_norm_g):
    bsz, seq, d = x.shape
    n = bsz * seq
    n_l, groups = w_s.shape[0], w_s.shape[1]
    assert n_l == 2, "the exchange schedule below is written for two layers"
    gd = d // groups
    ff = w_down.shape[1] * N_CHIP
    mx, my, mc = _mesh_pos()
    chip = 2 * mx + my
    core = jnp.reshape(mc, (1,)).astype(jnp.int32)
    where = jnp.stack([mc, chip]).astype(jnp.int32)

    tm_mm = _row_tile(seq, 512)
    tm_ew = _row_tile(seq, 256)
    tk_w = _row_tile(seq, 2048)
    ff_chunk = ff // 2 if (ff // 2) % 128 == 0 else ff
    IN, OUT, UP, DOWN = range(4)

    shards = [w.astype(BF16) for w in (w_in, w_out, w_up, w_down)]
    piece = lambda a, l: (shards[a], l)
    (w_in_0,), = _comm_only([_GatherJob([piece(IN, 0)])], "gather_first")

    taps = jnp.concatenate([conv_a_w.reshape(n_l, -1), conv_ffn_w.reshape(n_l, -1)], axis=1)
    tap_rows = -(-taps.size // 128 // 8) * 8
    tap_pack = jnp.zeros((tap_rows * 128,), F32).at[:taps.size].set(taps.reshape(-1)).reshape(tap_rows, 128)

    bmaps = jnp.repeat(jnp.swapaxes(b_s, 1, 2), gd, axis=2)

    xs = x.reshape(n, d)
    tgt = loss_target.reshape(n, d)

    (proj0, h1_0), ((w_out_0, w_up_0), (tap_all,)) = _norm_matmul(
        xs, mix_norm_g[0][None], w_in_0, tm=tm_mm, name="fwd_in_proj_0",
        jobs=[_GatherJob([piece(OUT, 0), piece(UP, 0)]), _SmallGatherJob([tap_pack])])
    tap_all = tap_all.reshape(N_CHIP, -1)[:, :taps.size].reshape(N_CHIP, n_l, -1)
    ca = tap_all[:, :, :3 * d // N_CHIP].reshape(N_CHIP, n_l, 3, d // N_CHIP)
    cf = tap_all[:, :, 3 * d // N_CHIP:].reshape(N_CHIP, n_l, 3, 2 * ff // N_CHIP)
    conv_a_full = jnp.transpose(ca, (1, 2, 0, 3)).reshape(n_l, 3, d)
    conv_f_full = jnp.transpose(cf, (1, 2, 0, 3)).reshape(n_l, 3, 2 * ff)

    def mixer_args(l, w_out_l):
        return (conv_a_full[l], ln_v_g[l][None], ln_v_b[l][None], w_s[l], bmaps[l], w_out_l.reshape(d, d))

    (x1_0, cz0), ((w_down_0, w_out_1),) = _mixer_fwd(
        proj0, xs, *mixer_args(0, w_out_0), seq=seq, tm=tm_mm, name="fwd_mixer_0",
        jobs=[_GatherJob([piece(DOWN, 0), piece(OUT, 1)])])
    (up0, h2_0), ((w_in_1,),) = _norm_matmul(
        x1_0, ffn_norm_g[0][None], w_up_0, tm=tm_mm, name="fwd_up_proj_0", jobs=[_GatherJob([piece(IN, 1)])])
    (x2_0, conv0), _ = _ffn_fwd(up0, x1_0, conv_f_full[0], w_down_0.reshape(ff, d), seq=seq, tm=tm_mm, cwid=ff_chunk,
                                name="fwd_ffn_0")
    (proj1, h1_1), ((w_up_1, w_down_1),) = _norm_matmul(
        x2_0, mix_norm_g[1][None], w_in_1, tm=tm_mm, name="fwd_in_proj_1",
        jobs=[_GatherJob([piece(UP, 1), piece(DOWN, 1)])])
    (x1_1, cz1), _ = _mixer_fwd(proj1, x2_0, *mixer_args(1, w_out_1), seq=seq, tm=tm_mm, name="fwd_mixer_1")
    (up1, h2_1), _ = _norm_matmul(x1_1, ffn_norm_g[1][None], w_up_1, tm=tm_mm, name="fwd_up_proj_1")
    (x2_1, conv1), _ = _ffn_fwd(up1, x1_1, conv_f_full[1], w_down_1.reshape(ff, d), seq=seq, tm=tm_mm, cwid=ff_chunk,
                                name="fwd_ffn_1")
    dx, loss_tile, d_final_g = _final_loss(x2_1, final_norm_g[None], tgt, tm=tm_mm, name="final_loss")

    def chipwise(a):
        return a.reshape(N_CHIP, a.shape[0] // N_CHIP, a.shape[1])

    def pair_sums(grads, recvs, tag):
        return _pair_sum(grads, recvs, core, f"grad_pair_sum_{tag}")

    d_up, dwd, d_cf1 = _ffn_bwd(dx, up1, conv1, conv_f_full[1], w_down_1.reshape(ff, d), seq=seq, tm=tm_ew,
                                cwid=ff_chunk, name="bwd_ffn_1")
    (dx1, d_g2_1), _ = _matmul_bwd_x(d_up, w_up_1, x1_1, ffn_norm_g[1][None], dx, tm=tm_mm, name="bwd_up_x_1")
    (dwu,), _ = _matmul_bwd_w(h2_1, d_up, N_CHIP, tk=tk_w, name="bwd_up_w_1")
    (d_proj, dwo, d_ca1, d_lg1, d_lb1, d_ws1, d_bs1), _ = _mixer_bwd(
        dx1, proj1, cz1, *mixer_args(1, w_out_1), seq=seq, tm=tm_ew, name="bwd_mixer_1")
    (dx, d_g1_1), _ = _matmul_bwd_x(d_proj, w_in_1, x2_0, mix_norm_g[1][None], dx1, tm=tm_mm, name="bwd_in_x_1")
    (dwi,), _ = _matmul_bwd_w(h1_1, d_proj, N_CHIP, tk=tk_w, name="bwd_in_w_1")
    grads1 = [dwi, chipwise(dwo), dwu, chipwise(dwd)]
    small1 = [d_g1_1, d_ca1, d_lg1, d_lb1, d_ws1, d_bs1.T, d_g2_1, d_cf1]

    d_up, dwd, d_cf0 = _ffn_bwd(dx, up0, conv0, conv_f_full[0], w_down_0.reshape(ff, d), seq=seq, tm=tm_ew,
                                cwid=ff_chunk, name="bwd_ffn_0")
    (dx1, d_g2_0), (recv1,) = _matmul_bwd_x(d_up, w_up_0, x1_0, ffn_norm_g[0][None], dx, tm=tm_mm, name="bwd_up_x_0",
                                            jobs=[_SwapJob(grads1)])
    parts1 = pair_sums(grads1, recv1, 1)
    (dwu,), ((arr1_in,),) = _matmul_bwd_w(h2_0, d_up, N_CHIP, tk=tk_w, name="bwd_up_w_0",
                                          jobs=[_ScatterJob([parts1[IN]])])
    ud0 = [dwu, chipwise(dwd)]
    (d_proj, dwo, d_ca0, d_lg0, d_lb0, d_ws0, d_bs0), (arr1_rest, recv0_ud) = _mixer_bwd(
        dx1, proj0, cz0, *mixer_args(0, w_out_0), seq=seq, tm=tm_ew, name="bwd_mixer_0",
        jobs=[_ScatterJob(parts1[OUT:]), _SwapJob(ud0)])
    arrived1 = [arr1_in] + list(arr1_rest)
    gsum = _final_sum(grads1, recv1, arrived1, where, None, 1, n_l, "grad_final_sum_1")
    parts0_ud = pair_sums(ud0, recv0_ud, "0ud")
    out0 = chipwise(dwo)
    (dwi,), (gsum, arr0_ud, (recv0_out,)) = _matmul_bwd_w(
        h1_0, d_proj, N_CHIP, tk=tk_w, name="bwd_in_w_0",
        jobs=[_ShareJob([(g, 1) for g in gsum]), _ScatterJob(parts0_ud), _SwapJob([out0])])
    part0_out, = pair_sums([out0], [recv0_out], "0o")
    n_mm = n // tm_mm
    n_head = max(n_mm // 4, 1)
    head, ((arr0_out,), (recv0_in,)) = _matmul_bwd_x(
        d_proj, w_in_0, xs, mix_norm_g[0][None], dx1, tm=tm_mm, name="bwd_in_x_0a", tiles=(0, n_head),
        jobs=[_ScatterJob([part0_out]), _SwapJob([dwi])])
    part0_in, = pair_sums([dwi], [recv0_in], "0i")
    (dx, d_g1_0), ((arr0_in,),) = _matmul_bwd_x(
        d_proj, w_in_0, xs, mix_norm_g[0][None], dx1, tm=tm_mm, name="bwd_in_x_0b", tiles=(n_head, n_mm - n_head),
        before=head, jobs=[_ScatterJob([part0_in])])
    grad_x = dx.reshape(bsz, seq, d)
    grads0 = [dwi, out0] + ud0
    recv0 = [recv0_in, recv0_out] + list(recv0_ud)
    arrived0 = [arr0_in, arr0_out] + list(arr0_ud)
    small0 = [d_g1_0, d_ca0, d_lg0, d_lb0, d_ws0, d_bs0.T, d_g2_0, d_cf0]

    flat = [a.reshape(-1) for a in small0 + small1] + [d_final_g.reshape(-1), loss_tile[0]]
    sizes = [a.size for a in flat]
    total = sum(sizes)
    rows = -(-total // 128 // (2 * HALO)) * 2 * HALO
    pack = jnp.concatenate(flat + [jnp.zeros((rows * 128 - total,), F32)]).reshape(rows, 128)
    red, _ = _all_reduce_small(pack, "small_all_reduce")
    red = red.reshape(-1)
    gsum = _final_sum(grads0, recv0, arrived0, where, gsum, 0, n_l, "grad_final_sum_0")
    (g_in, g_out, g_up, g_down), = _comm_only([_ShareJob([(g, 0) for g in gsum])], "grad_share_0")

    pieces, off = [], 0
    for s in sizes:
        pieces.append(red[off:off + s])
        off += s
    per_layer = [pieces[l * 8:(l + 1) * 8] for l in range(n_l)]

    def stacked(idx, shape):
        return jnp.stack([per_layer[l][idx].reshape(shape) for l in range(n_l)])

    def my_cols(a, width):
        return lax.dynamic_slice_in_dim(a, chip * width, width, axis=-1)

    g_mix_norm = stacked(0, (d,))
    g_conv_a = my_cols(stacked(1, (3, d)), d // N_CHIP)
    g_ln_g = stacked(2, (d,))
    g_ln_b = stacked(3, (d,))
    g_ws = stacked(4, (groups, CHUNK, CHUNK))
    g_bs = stacked(5, (groups, CHUNK))
    g_ffn_norm = stacked(6, (d,))
    g_conv_f = my_cols(stacked(7, (3, 2 * ff)), 2 * ff // N_CHIP)
    g_final = pieces[-2].reshape(1, d)
    loss = pieces[-1][0]

    big_d, big_nm, big_nv = _adamw_big([w_in, w_out, w_up, w_down], [g_in, g_out, g_up, g_down],
                                       [m_w_in, m_w_out, m_w_up, m_w_down], [v_w_in, v_w_out, v_w_up, v_w_down],
                                       "adamw_big")

    sm_w = [mix_norm_g, conv_a_w, ln_v_g, ln_v_b, w_s, b_s, ffn_norm_g, conv_ffn_w, final_norm_g[None]]
    sm_g = [g_mix_norm, g_conv_a, g_ln_g, g_ln_b, g_ws, g_bs, g_ffn_norm, g_conv_f, g_final]
    sm_m = [m_mix_norm_g, m_conv_a_w, m_ln_v_g, m_ln_v_b, m_w_s, m_b_s, m_ffn_norm_g, m_conv_ffn_w, m_final_norm_g[None]]
    sm_v = [v_mix_norm_g, v_conv_a_w, v_ln_v_g, v_ln_v_b, v_w_s, v_b_s, v_ffn_norm_g, v_conv_ffn_w, v_final_norm_g[None]]
    sm_d, sm_nm, sm_nv = _adamw_small(sm_w, sm_g, sm_m, sm_v, "adamw_small")

    def ordered(sm, bigs):
        return [sm[0], bigs[0], sm[1], sm[2], sm[3], sm[4], sm[5], bigs[1], sm[6], bigs[2], sm[7], bigs[3],
                sm[8].reshape(d)]

    out_g = ordered(sm_g, [g_in, g_out, g_up, g_down])
    out_d = ordered(sm_d, big_d)
    out_m = ordered(sm_nm, big_nm)
    out_v = ordered(sm_nv, big_nv)
    return (loss, grad_x, *out_g, *out_d, *out_m, *out_v)
```

```python
import jax
import jax.numpy as jnp
from jax import lax
from jax.experimental import pallas as pl
from jax.experimental.pallas import tpu as pltpu

F32 = jnp.float32
BF16 = jnp.bfloat16
EPS = 1e-6
CHUNK = 128
N_CHIP = 4
HALO = 8
N_PIECES = 7
LATE_STEPS = 2
VMEM_LIMIT_V7X = 56 * 1024 * 1024
MESH_T = pl.DeviceIdType.MESH
HBM_SPEC = pl.BlockSpec(memory_space=pltpu.HBM)
VMEM_SPEC = pl.BlockSpec(memory_space=pltpu.VMEM)

ADAM_LR, ADAM_B1, ADAM_B2, ADAM_EPS, ADAM_WD, ADAM_STEP = 0.001, 0.9, 0.999, 1e-08, 0.01, 10

NT_DIMS = (((1,), (1,)), ((), ()))
TN_DIMS = (((0,), (0,)), ((), ()))


def _resident(block_shape, index_map):
    return pl.BlockSpec(block_shape, index_map, pipeline_mode=pl.Buffered(1))


def _full(shape):
    return pl.BlockSpec(shape, lambda *_: (0,) * len(shape))


def _row_tile(seq, want):
    t = min(seq, want)
    assert seq % t == 0 and t % CHUNK == 0, (seq, want)
    return t


def _shift_prev(prev8, cur, k):
    ext = jnp.concatenate([prev8, cur], axis=0)
    return pltpu.roll(ext, k, 0)[HALO:]


def _shift_next(cur, next8, k):
    ext = jnp.concatenate([cur, next8], axis=0)
    n = ext.shape[0]
    return pltpu.roll(ext, n - k, 0)[:n - HALO]


def _rowsum(a):
    return jnp.sum(a, axis=0, keepdims=True)


def _lanemean(a):
    return jnp.mean(a, axis=-1, keepdims=True)


def _tril_weights(ws_ref, groups):
    r = lax.broadcasted_iota(jnp.int32, (CHUNK, CHUNK), 0)
    c = lax.broadcasted_iota(jnp.int32, (CHUNK, CHUNK), 1)
    tril = r >= c
    return tril, [jnp.where(tril, ws_ref[g], 0.0) for g in range(groups)]


def _halo_prev(tm):
    return lambda i: jnp.maximum(i * (tm // HALO) - 1, 0)


def _halo_next(tm, n):
    last = n // HALO - 1
    return lambda i: jnp.minimum((i + 1) * (tm // HALO), last)


def _mesh_pos():
    return lax.axis_index("x"), lax.axis_index("y"), lax.axis_index("c")


def _peer_chip(x, y, r):
    return (1 - x if r >> 1 else x), (1 - y if r & 1 else y)


def _remote(src, dst, sems, k, to):
    return pltpu.make_async_remote_copy(src_ref=src, dst_ref=dst, send_sem=sems[0].at[k], recv_sem=sems[1].at[k],
                                        device_id=to, device_id_type=MESH_T)


class _GatherJob:
    def __init__(self, pieces):
        self.pieces = pieces
        self.ins = [p[0] for p in pieces]
        self.out_shapes = [jax.ShapeDtypeStruct((N_CHIP,) + p[0].shape[1:], p[0].dtype) for p in pieces]
        self.aliases = {}
        n = len(pieces)
        self.sems = [pltpu.SemaphoreType.DMA((3 * n,))] * 4 + [pltpu.SemaphoreType.DMA((n,))]

    def _half(self, outs, a, of_chip, core):
        rh = outs[a].shape[1] // 2
        return outs[a].at[of_chip, pl.ds(core * rh, rh), :]

    def _own(self, ins, outs, sems, chip):
        return [pltpu.make_async_copy(ins[a].at[layer], outs[a].at[chip], sems[4].at[a])
                for a, (_, layer) in enumerate(self.pieces)]

    def _sends(self, ins, outs, sems, x, y, c):
        chip, out = 2 * x + y, []
        for a, (_, layer) in enumerate(self.pieces):
            rh = outs[a].shape[1] // 2
            for r in (1, 2, 3):
                px, py = _peer_chip(x, y, r)
                out.append(_remote(ins[a].at[layer, pl.ds(c * rh, rh), :], self._half(outs, a, chip, c), sems[0:2],
                                   3 * a + r - 1, (px, py, c)))
        return out

    def _passes(self, outs, sems, x, y, c, core):
        out = []
        for a in range(len(self.pieces)):
            for r in (1, 2, 3):
                px, py = _peer_chip(x, y, r)
                landed = self._half(outs, a, 2 * px + py, core)
                out.append(_remote(landed, landed, sems[2:4], 3 * a + r - 1, (x, y, 1 - c)))
        return out

    def start(self, ins, outs, sems):
        x, y, c = _mesh_pos()
        for cp in self._own(ins, outs, sems, 2 * x + y) + self._sends(ins, outs, sems, x, y, c):
            cp.start()

    def late(self, ins, outs, sems):
        x, y, c = _mesh_pos()
        passes = self._passes(outs, sems, x, y, c, c)
        k = 0
        for a in range(len(self.pieces)):
            for r in (1, 2, 3):
                px, py = _peer_chip(x, y, r)
                landed = self._half(outs, a, 2 * px + py, c)
                _remote(landed, landed, sems[0:2], k, (px, py, c)).wait_recv()
                passes[k].start()
                k += 1

    def finish(self, ins, outs, sems):
        x, y, c = _mesh_pos()
        for cp in self._passes(outs, sems, x, y, c, 1 - c):
            cp.wait_recv()
        for cp in self._sends(ins, outs, sems, x, y, c) + self._passes(outs, sems, x, y, c, c):
            cp.wait_send()
        for cp in self._own(ins, outs, sems, 2 * x + y):
            cp.wait()


class _SmallGatherJob:
    def __init__(self, pieces):
        self.ins = list(pieces)
        self.out_shapes = [jax.ShapeDtypeStruct((N_CHIP,) + p.shape, p.dtype) for p in pieces]
        self.aliases = {}
        n = len(pieces)
        self.sems = [pltpu.SemaphoreType.DMA((3 * n,))] * 2 + [pltpu.SemaphoreType.DMA((n,))]

    def _outgoing(self, ins, outs, sems):
        x, y, c = _mesh_pos()
        mine = [outs[a].at[2 * x + y] for a in range(len(ins))]
        own = [pltpu.make_async_copy(ins[a], mine[a], sems[2].at[a]) for a in range(len(ins))]
        sends = [_remote(ins[a], mine[a], sems[0:2], 3 * a + r - 1, (*_peer_chip(x, y, r), c))
                 for a in range(len(ins)) for r in (1, 2, 3)]
        return own, sends

    def start(self, ins, outs, sems):
        own, sends = self._outgoing(ins, outs, sems)
        for cp in own + sends:
            cp.start()

    def finish(self, ins, outs, sems):
        x, y, c = _mesh_pos()
        for a in range(len(ins)):
            for r in (1, 2, 3):
                px, py = _peer_chip(x, y, r)
                landed = outs[a].at[2 * px + py]
                _remote(landed, landed, sems[0:2], 3 * a + r - 1, (px, py, c)).wait_recv()
        own, sends = self._outgoing(ins, outs, sems)
        for cp in sends:
            cp.wait_send()
        for cp in own:
            cp.wait()


class _SwapJob:
    def __init__(self, pieces):
        self.ins = list(pieces)
        self.out_shapes = [jax.ShapeDtypeStruct((g.shape[0], g.shape[1] // 2, g.shape[2]), g.dtype) for g in pieces]
        self.aliases = {}
        self.sems = [pltpu.SemaphoreType.DMA((len(pieces),))] * 2

    def _copies(self, ins, outs, sems):
        x, y, c = _mesh_pos()
        out = []
        for a in range(len(ins)):
            rh = ins[a].shape[1] // 2
            out.append(_remote(ins[a].at[:, pl.ds((1 - c) * rh, rh), :], outs[a], sems, a, (x, y, 1 - c)))
        return out

    def start(self, ins, outs, sems):
        for cp in self._copies(ins, outs, sems):
            cp.start()

    def finish(self, ins, outs, sems):
        for cp in self._copies(ins, outs, sems):
            cp.wait()


class _ScatterJob:
    def __init__(self, pieces):
        self.ins = list(pieces)
        self.out_shapes = [jax.ShapeDtypeStruct((3,) + p.shape[1:], p.dtype) for p in pieces]
        self.aliases = {}
        self.sems = [pltpu.SemaphoreType.DMA((3 * len(pieces),))] * 2

    def _copies(self, ins, outs, sems):
        x, y, c = _mesh_pos()
        out = []
        for a in range(len(ins)):
            for r in (1, 2, 3):
                px, py = _peer_chip(x, y, r)
                out.append(_remote(ins[a].at[2 * px + py], outs[a].at[r - 1], sems, 3 * a + r - 1, (px, py, c)))
        return out

    def start(self, ins, outs, sems):
        for cp in self._copies(ins, outs, sems):
            cp.start()

    def finish(self, ins, outs, sems):
        for cp in self._copies(ins, outs, sems):
            cp.wait()


class _ShareJob:
    def __init__(self, pieces):
        self.pieces = pieces
        self.ins = [p[0] for p in pieces]
        self.out_shapes = [jax.ShapeDtypeStruct(p[0].shape, p[0].dtype) for p in pieces]
        self.aliases = {a: a for a in range(len(pieces))}
        self.sems = [pltpu.SemaphoreType.DMA((len(pieces),))] * 2

    def _copies(self, outs, sems, core):
        x, y, c = _mesh_pos()
        out = []
        for a, (_, layer) in enumerate(self.pieces):
            rh = outs[a].shape[1] // 2
            rows = outs[a].at[layer, pl.ds(core * rh, rh), :]
            out.append(_remote(rows, rows, sems, a, (x, y, 1 - c)))
        return out

    def start(self, ins, outs, sems):
        for cp in self._copies(outs, sems, lax.axis_index("c")):
            cp.start()

    def finish(self, ins, outs, sems):
        c = lax.axis_index("c")
        for cp in self._copies(outs, sems, c):
            cp.wait_send()
        for cp in self._copies(outs, sems, 1 - c):
            cp.wait_recv()


def _hosted_call(main, *, name, grid, in_specs, out_specs, out_shape, scratch_shapes=(), jobs=(), semantics=None,
                 operands=(), main_aliases=None):
    n_in, n_out, n_sc = len(in_specs), len(out_specs), len(scratch_shapes)
    counts = [(len(j.ins), len(j.out_shapes), len(j.sems)) for j in jobs]
    j_in, j_out, j_sc = (sum(c[k] for c in counts) for k in range(3))
    aliases, i0, o0 = dict(main_aliases or {}), n_in, n_out
    for j, (ci, co, _) in zip(jobs, counts):
        aliases.update({i0 + a: o0 + b for a, b in j.aliases.items()})
        i0, o0 = i0 + ci, o0 + co

    def body(*refs):
        cuts = [0, n_in, n_in + j_in, n_in + j_in + n_out, n_in + j_in + n_out + j_out,
                n_in + j_in + n_out + j_out + n_sc, len(refs)]
        m_in, jb_in, m_out, jb_out, m_sc, jb_sc = (list(refs[cuts[k]:cuts[k + 1]]) for k in range(6))

        def run(phase):
            i0 = o0 = s0 = 0
            for j, (ci, co, cs) in zip(jobs, counts):
                if hasattr(j, phase):
                    getattr(j, phase)(jb_in[i0:i0 + ci], jb_out[o0:o0 + co], jb_sc[s0:s0 + cs])
                i0, o0, s0 = i0 + ci, o0 + co, s0 + cs

        step, n_steps = 0, 1
        for ax in range(len(grid)):
            step, n_steps = step * grid[ax] + pl.program_id(ax), n_steps * grid[ax]
        if jobs:
            pl.when(step == 0)(lambda: run("start"))
        main(m_in, m_out, m_sc)
        if jobs:
            pl.when(step == max(n_steps - 1 - LATE_STEPS, 0))(lambda: run("late"))
            pl.when(step == n_steps - 1)(lambda: run("finish"))

    if semantics is None or jobs:
        semantics = ("arbitrary",) * len(grid)
    outs = pl.pallas_call(
        body, name=name, grid=grid,
        in_specs=list(in_specs) + [HBM_SPEC] * j_in,
        out_specs=list(out_specs) + [HBM_SPEC] * j_out,
        out_shape=list(out_shape) + [s for j in jobs for s in j.out_shapes],
        scratch_shapes=list(scratch_shapes) + [s for j in jobs for s in j.sems],
        input_output_aliases=aliases,
        compiler_params=pltpu.CompilerParams(dimension_semantics=semantics, vmem_limit_bytes=VMEM_LIMIT_V7X),
    )(*operands, *[a for j in jobs for a in j.ins])
    main_outs, rest, job_outs = list(outs[:n_out]), list(outs[n_out:]), []
    for _, co, _ in counts:
        job_outs.append(rest[:co])
        rest = rest[co:]
    return main_outs, job_outs


def _comm_only(jobs, name):
    return _hosted_call(lambda i, o, s: None, name=name, grid=(1,), in_specs=[], out_specs=[], out_shape=[],
                        jobs=jobs)[1]


def _norm_matmul(x, g, wg, *, tm, name, jobs=()):
    n, d = x.shape
    n_chip, _, ck = wg.shape

    def main(ins, outs, _):
        x_ref, g_ref, w_ref = ins
        o_ref, ht_ref = outs
        xf = x_ref[...]
        r = lax.rsqrt(_lanemean(xf * xf) + EPS)
        h = xf * r * g_ref[...]
        ht_ref[...] = h.T.astype(BF16)
        hb = h.astype(BF16)
        for j in range(n_chip):
            o_ref[:, j * ck:(j + 1) * ck] = jnp.dot(hb, w_ref[j], preferred_element_type=F32).astype(BF16)

    return _hosted_call(
        main, name=name, grid=(n // tm,), jobs=jobs, semantics=("parallel",), operands=(x, g, wg),
        in_specs=[pl.BlockSpec((tm, d), lambda i: (i, 0)), _full((1, d)), _resident((n_chip, d, ck), lambda i: (0, 0, 0))],
        out_specs=[pl.BlockSpec((tm, n_chip * ck), lambda i: (i, 0)), pl.BlockSpec((d, tm), lambda i: (0, i))],
        out_shape=[jax.ShapeDtypeStruct((n, n_chip * ck), BF16), jax.ShapeDtypeStruct((d, n), BF16)])


def _mixer_fwd(proj, x, cw, lg, lb, ws, bmap, wout, *, seq, tm, name, jobs=()):
    n, d = x.shape
    groups = ws.shape[0]
    gd = d // groups
    prev = _halo_prev(tm)

    def main(ins, outs, scratch):
        proj_ref, pcg_ref, pxi_ref, x_ref, cw_ref, lg_ref, lb_ref, ws_ref, bmap_ref, wout_ref = ins
        o_ref, cz_ref = outs
        vn_s, mixed_s = scratch
        seq_start = (pl.program_id(0) * tm) % seq == 0

        def piece(k):
            return proj_ref[:, k * d:(k + 1) * d].astype(F32)

        z = piece(1) * piece(2)
        zprev = jnp.where(seq_start, 0.0, pcg_ref[...].astype(F32) * pxi_ref[...].astype(F32))
        cz = (cw_ref[0:1, :] * _shift_prev(zprev, z, 2) + cw_ref[1:2, :] * _shift_prev(zprev, z, 1)
              + cw_ref[2:3, :] * z)
        cz_ref[...] = cz.astype(BF16)
        ya = piece(0) * cz

        v = piece(4)
        xc = v - _lanemean(v)
        vn = xc * lax.rsqrt(_lanemean(xc * xc) + EPS) * lg_ref[...] + lb_ref[...]
        vn_s[...] = vn.astype(BF16)
        _, wsm = _tril_weights(ws_ref, groups)
        for ck in range(tm // CHUNK):
            rows = slice(ck * CHUNK, (ck + 1) * CHUNK)
            for g in range(groups):
                cols = slice(g * gd, (g + 1) * gd)
                mixed_s[rows, cols] = (jnp.dot(wsm[g].astype(BF16), vn_s[rows, cols], preferred_element_type=F32)
                                       + bmap_ref[:, cols])
        yb = piece(3) * mixed_s[...]
        merged = jax.nn.sigmoid(piece(5)) * ya + jax.nn.sigmoid(piece(6)) * yb
        o_ref[...] = x_ref[...] + jnp.dot(merged.astype(BF16), wout_ref[...], preferred_element_type=F32)

    return _hosted_call(
        main, name=name, grid=(n // tm,), jobs=jobs, semantics=("parallel",),
        operands=(proj, proj, proj, x, cw, lg, lb, ws, bmap, wout),
        in_specs=[pl.BlockSpec((tm, N_PIECES * d), lambda i: (i, 0)),
                  pl.BlockSpec((HALO, d), lambda i: (prev(i), 1)),
                  pl.BlockSpec((HALO, d), lambda i: (prev(i), 2)),
                  pl.BlockSpec((tm, d), lambda i: (i, 0)),
                  _full((3, d)), _full((1, d)), _full((1, d)), _full((groups, CHUNK, CHUNK)), _full((CHUNK, d)),
                  _resident((d, d), lambda i: (0, 0))],
        out_specs=[pl.BlockSpec((tm, d), lambda i: (i, 0)), pl.BlockSpec((tm, d), lambda i: (i, 0))],
        out_shape=[jax.ShapeDtypeStruct((n, d), F32), jax.ShapeDtypeStruct((n, d), BF16)],
        scratch_shapes=[pltpu.VMEM((tm, d), BF16), pltpu.VMEM((tm, d), F32)])


def _ffn_fwd(up, x, cw, wd, *, seq, tm, cwid, name, jobs=()):
    n, d = x.shape
    ff = wd.shape[0]
    prev = _halo_prev(tm)

    def main(ins, outs, _):
        up_ref, pup_ref, x_ref, cw_ref, wd_ref = ins
        o_ref, conv_ref = outs
        seq_start = (pl.program_id(0) * tm) % seq == 0

        def conv(lo):
            cols = slice(lo, lo + cwid)
            cur = up_ref[:, cols].astype(F32)
            pre = jnp.where(seq_start, 0.0, pup_ref[:, cols].astype(F32))
            out = (cw_ref[0:1, cols] * _shift_prev(pre, cur, 2) + cw_ref[1:2, cols] * _shift_prev(pre, cur, 1)
                   + cw_ref[2:3, cols] * cur)
            conv_ref[:, cols] = out.astype(BF16)
            return out

        acc = x_ref[...]
        for cj in range(ff // cwid):
            gate = conv(cj * cwid)
            val = conv(ff + cj * cwid)
            a = gate * jax.nn.sigmoid(gate) * val
            acc = acc + jnp.dot(a.astype(BF16), wd_ref[cj * cwid:(cj + 1) * cwid, :], preferred_element_type=F32)
        o_ref[...] = acc

    return _hosted_call(
        main, name=name, grid=(n // tm,), jobs=jobs, semantics=("parallel",), operands=(up, up, x, cw, wd),
        in_specs=[pl.BlockSpec((tm, 2 * ff), lambda i: (i, 0)),
                  pl.BlockSpec((HALO, 2 * ff), lambda i: (prev(i), 0)),
                  pl.BlockSpec((tm, d), lambda i: (i, 0)),
                  _full((3, 2 * ff)),
                  _resident((ff, d), lambda i: (0, 0))],
        out_specs=[pl.BlockSpec((tm, d), lambda i: (i, 0)), pl.BlockSpec((tm, 2 * ff), lambda i: (i, 0))],
        out_shape=[jax.ShapeDtypeStruct((n, d), F32), jax.ShapeDtypeStruct((n, 2 * ff), BF16)])


def _final_loss(x, g, target, *, tm, name):
    n, d = x.shape

    def body(x_ref, g_ref, t_ref, dx_ref, loss_ref, dg_ref):
        @pl.when(pl.program_id(0) == 0)
        def _():
            loss_ref[...] = jnp.zeros_like(loss_ref)
            dg_ref[...] = jnp.zeros_like(dg_ref)

        xf = x_ref[...]
        r = lax.rsqrt(_lanemean(xf * xf) + EPS)
        xhat = xf * r
        diff = xhat * g_ref[...] - t_ref[...]
        loss_ref[...] += (0.5 / d) * _rowsum(jnp.sum(diff * diff, axis=-1, keepdims=True))
        dy = diff * (1.0 / d)
        dg_ref[...] += _rowsum(dy * xhat)
        dyh = dy * g_ref[...]
        dx_ref[...] = r * (dyh - xhat * _lanemean(dyh * xhat))

    return pl.pallas_call(
        body, name=name, grid=(n // tm,),
        in_specs=[pl.BlockSpec((tm, d), lambda i: (i, 0)), _full((1, d)), pl.BlockSpec((tm, d), lambda i: (i, 0))],
        out_specs=[pl.BlockSpec((tm, d), lambda i: (i, 0)), _full((HALO, CHUNK)), _full((1, d))],
        out_shape=[jax.ShapeDtypeStruct((n, d), F32), jax.ShapeDtypeStruct((HALO, CHUNK), F32),
                   jax.ShapeDtypeStruct((1, d), F32)],
        compiler_params=pltpu.CompilerParams(dimension_semantics=("arbitrary",), vmem_limit_bytes=VMEM_LIMIT_V7X),
    )(x, g, target)


def _ffn_bwd(dx, up, conv, cw, wd, *, seq, tm, cwid, name):
    n, d = dx.shape
    ff = wd.shape[0]
    nxt = _halo_next(tm, n)

    def body(dx_ref, dxn_ref, up_ref, conv_ref, nconv_ref, cw_ref, wd_ref, dup_ref, dwd_ref, dcw_ref):
        i = pl.program_id(0)
        keep_next = jnp.where(((i + 1) * tm) % seq == 0, 0.0, 1.0)

        @pl.when(i == 0)
        def _():
            dwd_ref[...] = jnp.zeros_like(dwd_ref)
            dcw_ref[...] = jnp.zeros_like(dcw_ref)

        dxe = jnp.concatenate([dx_ref[...], dxn_ref[...]], axis=0).astype(BF16)
        dxb = dxe[:tm]
        for cj in range(ff // cwid):
            rows = slice(cj * cwid, (cj + 1) * cwid)
            g_cols, v_cols = slice(cj * cwid, (cj + 1) * cwid), slice(ff + cj * cwid, ff + (cj + 1) * cwid)
            dae = lax.dot_general(dxe, wd_ref[rows, :], NT_DIMS, preferred_element_type=F32)
            da, dan = dae[:tm], dae[tm:]

            def grads(gate, val, da_rows):
                sg = jax.nn.sigmoid(gate)
                sl = gate * sg
                return sl, da_rows * val * sg * (1.0 + gate * (1.0 - sg)), da_rows * sl

            gate, val = conv_ref[:, g_cols].astype(F32), conv_ref[:, v_cols].astype(F32)
            sl, d_gate, d_val = grads(gate, val, da)
            dwd_ref[rows, :] += lax.dot_general((sl * val).astype(BF16), dxb, TN_DIMS, preferred_element_type=F32)
            _, d_gate_n, d_val_n = grads(nconv_ref[:, g_cols].astype(F32), nconv_ref[:, v_cols].astype(F32),
                                         dan * keep_next)
            for cols, dcur, dnext in ((g_cols, d_gate, d_gate_n), (v_cols, d_val, d_val_n)):
                upc = up_ref[:, cols].astype(F32)
                d1, d2 = _shift_next(dcur, dnext, 1), _shift_next(dcur, dnext, 2)
                for k, dk in enumerate((d2, d1, dcur)):
                    dcw_ref[k:k + 1, cols] += _rowsum(dk * upc)
                dup = cw_ref[2:3, cols] * dcur + cw_ref[1:2, cols] * d1 + cw_ref[0:1, cols] * d2
                dup_ref[:, cols] = dup.astype(BF16)

    return pl.pallas_call(
        body, name=name, grid=(n // tm,),
        in_specs=[pl.BlockSpec((tm, d), lambda i: (i, 0)),
                  pl.BlockSpec((HALO, d), lambda i: (nxt(i), 0)),
                  pl.BlockSpec((tm, 2 * ff), lambda i: (i, 0)),
                  pl.BlockSpec((tm, 2 * ff), lambda i: (i, 0)),
                  pl.BlockSpec((HALO, 2 * ff), lambda i: (nxt(i), 0)),
                  _full((3, 2 * ff)),
                  _resident((ff, d), lambda i: (0, 0))],
        out_specs=[pl.BlockSpec((tm, 2 * ff), lambda i: (i, 0)), _full((ff, d)), _full((3, 2 * ff))],
        out_shape=[jax.ShapeDtypeStruct((n, 2 * ff), BF16), jax.ShapeDtypeStruct((ff, d), F32),
                   jax.ShapeDtypeStruct((3, 2 * ff), F32)],
        compiler_params=pltpu.CompilerParams(dimension_semantics=("arbitrary",), vmem_limit_bytes=VMEM_LIMIT_V7X),
    )(dx, dx, up, conv, conv, cw, wd)


def _matmul_bwd_x(dy, wg, x, g, dres, *, tm, name, jobs=(), tiles=None, before=None):
    n, d = x.shape
    n_chip, _, ck = wg.shape
    first_tile, n_tiles = tiles or (0, n // tm)
    extra = [] if before is None else list(before)

    def main(ins, outs, _):
        dy_ref, w_ref, x_ref, g_ref, dres_ref = ins[:5]
        dx_ref, dg_ref = outs

        @pl.when(pl.program_id(0) == 0)
        def _():
            dg_ref[...] = jnp.zeros_like(dg_ref) if before is None else ins[6][...]

        dh = lax.dot_general(dy_ref[:, 0:ck], w_ref[0], NT_DIMS, preferred_element_type=F32)
        for j in range(1, n_chip):
            dh = dh + lax.dot_general(dy_ref[:, j * ck:(j + 1) * ck], w_ref[j], NT_DIMS, preferred_element_type=F32)
        xf = x_ref[...]
        r = lax.rsqrt(_lanemean(xf * xf) + EPS)
        xhat = xf * r
        dg_ref[...] += _rowsum(dh * xhat)
        dyh = dh * g_ref[...]
        dx_ref[...] = dres_ref[...] + r * (dyh - xhat * _lanemean(dyh * xhat))

    rows = lambda width: pl.BlockSpec((tm, width), lambda i: (i + first_tile, 0))
    return _hosted_call(
        main, name=name, grid=(n_tiles,), jobs=jobs, semantics=("arbitrary",), operands=(dy, wg, x, g, dres, *extra),
        in_specs=[rows(n_chip * ck), _resident((n_chip, d, ck), lambda i: (0, 0, 0)), rows(d), _full((1, d)), rows(d)]
        + ([] if before is None else [pl.BlockSpec(memory_space=pl.ANY), _full((1, d))]),
        out_specs=[rows(d), _full((1, d))],
        out_shape=[jax.ShapeDtypeStruct((n, d), F32), jax.ShapeDtypeStruct((1, d), F32)],
        main_aliases=None if before is None else {5: 0})


def _matmul_bwd_w(ht, dy, n_chip, *, tk, name, jobs=()):
    d, n = ht.shape
    ck = dy.shape[1] // n_chip

    def main(ins, outs, _):
        ht_ref, dy_ref = ins
        o_ref, = outs

        @pl.when(pl.program_id(1) == 0)
        def _():
            o_ref[...] = jnp.zeros_like(o_ref)

        o_ref[...] += jnp.dot(ht_ref[...], dy_ref[...], preferred_element_type=F32)

    return _hosted_call(
        main, name=name, grid=(n_chip, n // tk), jobs=jobs, semantics=("parallel", "arbitrary"), operands=(ht, dy),
        in_specs=[pl.BlockSpec((d, tk), lambda j, k: (0, k)), pl.BlockSpec((tk, ck), lambda j, k: (k, j))],
        out_specs=[pl.BlockSpec((None, d, ck), lambda j, k: (j, 0, 0))],
        out_shape=[jax.ShapeDtypeStruct((n_chip, d, ck), F32)])


def _mixer_bwd(dx, proj, cz, cw, lg, lb, ws, bmap, wout, *, seq, tm, name, jobs=()):
    n, d = dx.shape
    groups = ws.shape[0]
    gd = d // groups
    nxt = _halo_next(tm, n)
    n_tiles = n // tm

    def main(ins, outs, scratch):
        proj_ref, cz_ref, nbg_ref, nga_ref, dx_ref, dxn_ref, cw_ref, lg_ref, lb_ref, ws_ref, bmap_ref, wout_ref = ins
        dproj_ref, dwout_ref, dcw_ref, dlg_ref, dlb_ref, dws_ref, dbs_ref = outs
        vn_s, mixed_s, dmix_s, dvn_s, dbmap_s = scratch
        i = pl.program_id(0)
        keep_next = jnp.where(((i + 1) * tm) % seq == 0, 0.0, 1.0)

        @pl.when(i == 0)
        def _():
            for ref in (dwout_ref, dcw_ref, dlg_ref, dlb_ref, dws_ref, dbmap_s):
                ref[...] = jnp.zeros_like(ref)

        def piece(k):
            return proj_ref[:, k * d:(k + 1) * d].astype(F32)

        def put(k, val):
            dproj_ref[:, k * d:(k + 1) * d] = val.astype(BF16)

        w = [cw_ref[k:k + 1, :] for k in range(3)]
        cg, xi = piece(1), piece(2)
        cz = cz_ref[...].astype(F32)
        bg = piece(0)
        ya = bg * cz

        v = piece(4)
        xc = v - _lanemean(v)
        rstd = lax.rsqrt(_lanemean(xc * xc) + EPS)
        vhat = xc * rstd
        vn_s[...] = (vhat * lg_ref[...] + lb_ref[...]).astype(BF16)
        tril, wsm = _tril_weights(ws_ref, groups)
        for ck in range(tm // CHUNK):
            rows = slice(ck * CHUNK, (ck + 1) * CHUNK)
            for g in range(groups):
                cols = slice(g * gd, (g + 1) * gd)
                mixed_s[rows, cols] = (jnp.dot(wsm[g].astype(BF16), vn_s[rows, cols], preferred_element_type=F32)
                                       + bmap_ref[:, cols])
        u = piece(3)
        mixed = mixed_s[...]
        yb = u * mixed
        sa, sb = jax.nn.sigmoid(piece(5)), jax.nn.sigmoid(piece(6))
        merged = sa * ya + sb * yb

        dxe = jnp.concatenate([dx_ref[...], dxn_ref[...]], axis=0).astype(BF16)
        dme = lax.dot_general(dxe, wout_ref[...], NT_DIMS, preferred_element_type=F32)
        dm, dm_n = dme[:tm], dme[tm:]
        dwout_ref[...] += lax.dot_general(merged.astype(BF16), dxe[:tm], TN_DIMS, preferred_element_type=F32)

        put(5, dm * ya * sa * (1.0 - sa))
        put(6, dm * yb * sb * (1.0 - sb))
        d_ya, d_yb = dm * sa, dm * sb
        put(0, d_ya * cz)
        d_cz = d_ya * bg
        d_cz_n = dm_n * jax.nn.sigmoid(nga_ref[...].astype(F32)) * nbg_ref[...].astype(F32) * keep_next
        d_cz1, d_cz2 = _shift_next(d_cz, d_cz_n, 1), _shift_next(d_cz, d_cz_n, 2)
        z = cg * xi
        for k, dk in enumerate((d_cz2, d_cz1, d_cz)):
            dcw_ref[k:k + 1, :] += _rowsum(dk * z)
        dz = w[2] * d_cz + w[1] * d_cz1 + w[0] * d_cz2
        put(1, dz * xi)
        put(2, dz * cg)

        put(3, d_yb * mixed)
        d_mixed = d_yb * u
        dmix_s[...] = d_mixed.astype(BF16)
        for ck in range(tm // CHUNK):
            rows = slice(ck * CHUNK, (ck + 1) * CHUNK)
            dbmap_s[...] += d_mixed[rows, :]
            for g in range(groups):
                cols = slice(g * gd, (g + 1) * gd)
                dvn_s[rows, cols] = jnp.dot(wsm[g].T.astype(BF16), dmix_s[rows, cols], preferred_element_type=F32)
                dws_ref[g] += jnp.where(
                    tril, lax.dot_general(dmix_s[rows, cols], vn_s[rows, cols], NT_DIMS, preferred_element_type=F32),
                    0.0)
        d_vn = dvn_s[...]
        dlg_ref[...] += _rowsum(d_vn * vhat)
        dlb_ref[...] += _rowsum(d_vn)
        d_vhat = d_vn * lg_ref[...]
        put(4, rstd * (d_vhat - _lanemean(d_vhat) - vhat * _lanemean(d_vhat * vhat)))

        @pl.when(i == n_tiles - 1)
        def _():
            for g in range(groups):
                dbs_ref[:, g:g + 1] = jnp.sum(dbmap_s[:, g * gd:(g + 1) * gd], axis=-1, keepdims=True)

    return _hosted_call(
        main, name=name, grid=(n_tiles,), jobs=jobs, semantics=("arbitrary",),
        operands=(proj, cz, proj, proj, dx, dx, cw, lg, lb, ws, bmap, wout),
        in_specs=[pl.BlockSpec((tm, N_PIECES * d), lambda i: (i, 0)),
                  pl.BlockSpec((tm, d), lambda i: (i, 0)),
                  pl.BlockSpec((HALO, d), lambda i: (nxt(i), 0)),
                  pl.BlockSpec((HALO, d), lambda i: (nxt(i), 5)),
                  pl.BlockSpec((tm, d), lambda i: (i, 0)),
                  pl.BlockSpec((HALO, d), lambda i: (nxt(i), 0)),
                  _full((3, d)), _full((1, d)), _full((1, d)), _full((groups, CHUNK, CHUNK)), _full((CHUNK, d)),
                  _resident((d, d), lambda i: (0, 0))],
        out_specs=[pl.BlockSpec((tm, N_PIECES * d), lambda i: (i, 0)), _full((d, d)),
                   _full((3, d)), _full((1, d)), _full((1, d)), _full((groups, CHUNK, CHUNK)), _full((CHUNK, groups))],
        out_shape=[jax.ShapeDtypeStruct((n, N_PIECES * d), BF16), jax.ShapeDtypeStruct((d, d), F32),
                   jax.ShapeDtypeStruct((3, d), F32), jax.ShapeDtypeStruct((1, d), F32),
                   jax.ShapeDtypeStruct((1, d), F32), jax.ShapeDtypeStruct((groups, CHUNK, CHUNK), F32),
                   jax.ShapeDtypeStruct((CHUNK, groups), F32)],
        scratch_shapes=[pltpu.VMEM((tm, d), BF16), pltpu.VMEM((tm, d), F32), pltpu.VMEM((tm, d), BF16),
                        pltpu.VMEM((tm, d), F32), pltpu.VMEM((CHUNK, d), F32)])


def _all_reduce_small(pack, name, jobs=()):
    rows = pack.shape[0]
    rh = rows // 2
    assert rh % HALO == 0, rows

    def main(ins, outs, scratch):
        p_ref, = ins
        o_ref, = outs
        sib_buf, chip_buf, got_buf, send_sem, recv_sem = scratch
        sems = (send_sem, recv_sem)
        x, y, c = _mesh_pos()
        chip, sibling = 2 * x + y, (x, y, 1 - c)
        mine = pl.ds(pl.multiple_of(c * rh, HALO), rh)
        theirs = pl.ds(pl.multiple_of((1 - c) * rh, HALO), rh)
        swap = _remote(p_ref.at[theirs], sib_buf, sems, 0, sibling)
        swap.start()
        swap.wait()
        chip_buf[chip] = p_ref[mine, :] + sib_buf[...]
        copies = []
        for r in (1, 2, 3):
            px, py = _peer_chip(x, y, r)
            copies.append(_remote(chip_buf.at[chip], chip_buf.at[chip], sems, r, (px, py, c)))
        for cp in copies:
            cp.start()
        for r, cp in zip((1, 2, 3), copies):
            px, py = _peer_chip(x, y, r)
            landed = chip_buf.at[2 * px + py]
            _remote(landed, landed, sems, r, (px, py, c)).wait_recv()
            cp.wait_send()
        total = ((chip_buf[0] + chip_buf[1]) + chip_buf[2]) + chip_buf[3]
        o_ref[mine, :] = total
        chip_buf[chip] = total
        share = _remote(chip_buf.at[chip], got_buf, sems, 4, sibling)
        share.start()
        share.wait()
        o_ref[theirs, :] = got_buf[...]

    outs, job_outs = _hosted_call(
        main, name=name, grid=(1,), jobs=jobs, operands=(pack,),
        in_specs=[_full(pack.shape)], out_specs=[_full(pack.shape)],
        out_shape=[jax.ShapeDtypeStruct(pack.shape, pack.dtype)],
        scratch_shapes=[pltpu.VMEM((rh, 128), F32), pltpu.VMEM((N_CHIP, rh, 128), F32), pltpu.VMEM((rh, 128), F32),
                        pltpu.SemaphoreType.DMA((5,)), pltpu.SemaphoreType.DMA((5,))])
    return outs[0], job_outs


SUM_BLOCKS = 2


def _pair_sum(grads, recvs, core, name):
    n_t, nb = len(grads), SUM_BLOCKS
    n_chip = grads[0].shape[0]

    def body(core_ref, *refs):
        for t in range(n_t):
            refs[2 * n_t + t][...] = (refs[t][...] + refs[n_t + t][...]).astype(BF16)

    def blk(g):
        rh = g.shape[1] // 2
        assert rh % (16 * nb) == 0, g.shape
        return (None, rh // nb, g.shape[2])

    mine = [pl.BlockSpec(blk(g), lambda k, i, core_ref: (k, core_ref[0] * nb + i, 0)) for g in grads]
    plain = [pl.BlockSpec(blk(g), lambda k, i, core_ref: (k, i, 0)) for g in grads]
    return pl.pallas_call(
        body, name=name,
        grid_spec=pltpu.PrefetchScalarGridSpec(num_scalar_prefetch=1, grid=(n_chip, nb), in_specs=mine + plain,
                                               out_specs=plain),
        out_shape=[jax.ShapeDtypeStruct((n_chip, g.shape[1] // 2, g.shape[2]), BF16) for g in grads],
        compiler_params=pltpu.CompilerParams(dimension_semantics=("parallel", "parallel"),
                                             vmem_limit_bytes=VMEM_LIMIT_V7X),
    )(core, *grads, *recvs)


def _final_sum(grads, recvs, arriveds, where, prevs, layer, n_layers, name):
    n_t, nb = len(grads), SUM_BLOCKS
    extra = [] if prevs is None else list(prevs)

    def body(where_ref, *refs):
        outs = refs[len(refs) - n_t:]
        for t in range(n_t):
            g_ref, r_ref = refs[t], refs[n_t + t]
            a1, a2, a3 = (refs[(2 + s) * n_t + t] for s in range(3))
            own = g_ref[...] + r_ref[...]
            outs[t][...] = ((own + a1[...].astype(F32)) + a2[...].astype(F32)) + a3[...].astype(F32)

    def blk(g):
        return (None, g.shape[1] // 2 // nb, g.shape[2])

    in_specs = ([pl.BlockSpec(blk(g), lambda i, w: (w[1], w[0] * nb + i, 0)) for g in grads]
                + [pl.BlockSpec(blk(g), lambda i, w: (w[1], i, 0)) for g in grads]
                + [pl.BlockSpec(blk(g), lambda i, w, s=s: (s, i, 0)) for s in range(3) for g in grads]
                + [pl.BlockSpec(memory_space=pl.ANY)] * len(extra))
    return pl.pallas_call(
        body, name=name,
        grid_spec=pltpu.PrefetchScalarGridSpec(
            num_scalar_prefetch=1, grid=(nb,), in_specs=in_specs,
            out_specs=[pl.BlockSpec(blk(g), lambda i, w: (layer, w[0] * nb + i, 0)) for g in grads]),
        out_shape=[jax.ShapeDtypeStruct((n_layers,) + g.shape[1:], F32) for g in grads],
        input_output_aliases={1 + 5 * n_t + t: t for t in range(n_t)} if extra else {},
        compiler_params=pltpu.CompilerParams(dimension_semantics=("parallel",), vmem_limit_bytes=VMEM_LIMIT_V7X),
    )(where, *grads, *recvs, *arriveds, *arriveds, *arriveds, *extra)


def _adamw_math(w, g, m, v):
    m = ADAM_B1 * m + (1.0 - ADAM_B1) * g
    v = ADAM_B2 * v + (1.0 - ADAM_B2) * (g * g)
    m_hat = m / (1.0 - ADAM_B1 ** ADAM_STEP)
    v_hat = v / (1.0 - ADAM_B2 ** ADAM_STEP)
    delta = -ADAM_LR * (m_hat / (jnp.sqrt(v_hat) + ADAM_EPS) + ADAM_WD * w)
    return delta, m, v


ADAMW_BLOCKS = 8


def _adamw_big(ws, gs, ms, vs, name):
    n_t, nb = len(ws), ADAMW_BLOCKS
    n_l = ws[0].shape[0]

    def body(*refs):
        for t in range(n_t):
            g = refs[n_t + t][...]
            res = _adamw_math(refs[t][...], g, refs[2 * n_t + t][...], refs[3 * n_t + t][...]) + (g,)
            for q in range(4):
                refs[(4 + q) * n_t + t][...] = res[q]

    def spec(w):
        assert w.shape[1] % (HALO * nb) == 0, w.shape
        return pl.BlockSpec((None, w.shape[1] // nb, w.shape[2]), lambda l, i: (l, i, 0))

    specs = [spec(w) for w in ws]
    outs = pl.pallas_call(
        body, name=name, grid=(n_l, nb), in_specs=specs * 4, out_specs=specs * 4,
        out_shape=[jax.ShapeDtypeStruct(w.shape, F32) for w in ws] * 4,
        compiler_params=pltpu.CompilerParams(dimension_semantics=("parallel", "parallel"),
                                             vmem_limit_bytes=VMEM_LIMIT_V7X),
    )(*ws, *gs, *ms, *vs)
    return outs[:n_t], outs[n_t:2 * n_t], outs[2 * n_t:3 * n_t], outs[3 * n_t:]


def _adamw_small(ws, gs, ms, vs, name):
    n_p = len(ws)

    def body(*refs):
        ins, outs = refs[:4 * n_p], refs[4 * n_p:]
        for p in range(n_p):
            res = _adamw_math(ins[p][...], ins[n_p + p][...], ins[2 * n_p + p][...], ins[3 * n_p + p][...])
            for q in range(3):
                outs[q * n_p + p][...] = res[q]

    outs = pl.pallas_call(
        body, name=name, in_specs=[VMEM_SPEC] * (4 * n_p), out_specs=[VMEM_SPEC] * (3 * n_p),
        out_shape=[jax.ShapeDtypeStruct(w.shape, F32) for w in ws] * 3,
        compiler_params=pltpu.CompilerParams(vmem_limit_bytes=VMEM_LIMIT_V7X),
    )(*ws, *gs, *ms, *vs)
    return outs[:n_p], outs[n_p:2 * n_p], outs[2 * n_p:]


def kernel(x, mix_norm_g, w_in, conv_a_w, ln_v_g, ln_v_b, w_s, b_s, w_out, ffn_norm_g, w_up, conv_ffn_w, w_down, final_norm_g, loss_target, m_mix_norm_g, m_w_in, m_conv_a_w, m_ln_v_g, m_ln_v_b, m_w_s, m_b_s, m_w_out, m_ffn_norm_g, m_w_up, m_conv_ffn_w, m_w_down, m_final_norm_g, v_mix_norm_g, v_w_in, v_conv_a_w, v_ln_v_g, v_ln_v_b, v_w_s, v_b_s, v_w_out, v_ffn_norm_g, v_w_up, v_conv_ffn_w, v_w_down, v_final_norm_g):
    bsz, seq, d = x.shape
    n = bsz * seq
    n_l, groups = w_s.shape[0], w_s.shape[1]
    assert n_l == 2, "the exchange schedule below is written for two layers"
    gd = d // groups
    ff = w_down.shape[1] * N_CHIP
    mx, my, mc = _mesh_pos()
    chip = 2 * mx + my
    core = jnp.reshape(mc, (1,)).astype(jnp.int32)
    where = jnp.stack([mc, chip]).astype(jnp.int32)

    tm_mm = _row_tile(seq, 512)
    tm_ew = _row_tile(seq, 256)
    tk_w = _row_tile(seq, 2048)
    ff_chunk = ff // 2 if (ff // 2) % 128 == 0 else ff
    IN, OUT, UP, DOWN = range(4)

    shards = [w.astype(BF16) for w in (w_in, w_out, w_up, w_down)]
    piece = lambda a, l: (shards[a], l)
    (w_in_0,), = _comm_only([_GatherJob([piece(IN, 0)])], "gather_first")

    taps = jnp.concatenate([conv_a_w.reshape(n_l, -1), conv_ffn_w.reshape(n_l, -1)], axis=1)
    tap_rows = -(-taps.size // 128 // 8) * 8
    tap_pack = jnp.zeros((tap_rows * 128,), F32).at[:taps.size].set(taps.reshape(-1)).reshape(tap_rows, 128)

    bmaps = jnp.repeat(jnp.swapaxes(b_s, 1, 2), gd, axis=2)

    xs = x.reshape(n, d)
    tgt = loss_target.reshape(n, d)

    (proj0, h1_0), ((w_out_0, w_up_0), (tap_all,)) = _norm_matmul(
        xs, mix_norm_g[0][None], w_in_0, tm=tm_mm, name="fwd_in_proj_0",
        jobs=[_GatherJob([piece(OUT, 0), piece(UP, 0)]), _SmallGatherJob([tap_pack])])
    tap_all = tap_all.reshape(N_CHIP, -1)[:, :taps.size].reshape(N_CHIP, n_l, -1)
    ca = tap_all[:, :, :3 * d // N_CHIP].reshape(N_CHIP, n_l, 3, d // N_CHIP)
    cf = tap_all[:, :, 3 * d // N_CHIP:].reshape(N_CHIP, n_l, 3, 2 * ff // N_CHIP)
    conv_a_full = jnp.transpose(ca, (1, 2, 0, 3)).reshape(n_l, 3, d)
    conv_f_full = jnp.transpose(cf, (1, 2, 0, 3)).reshape(n_l, 3, 2 * ff)

    def mixer_args(l, w_out_l):
        return (conv_a_full[l], ln_v_g[l][None], ln_v_b[l][None], w_s[l], bmaps[l], w_out_l.reshape(d, d))

    (x1_0, cz0), ((w_down_0, w_out_1),) = _mixer_fwd(
        proj0, xs, *mixer_args(0, w_out_0), seq=seq, tm=tm_mm, name="fwd_mixer_0",
        jobs=[_GatherJob([piece(DOWN, 0), piece(OUT, 1)])])
    (up0, h2_0), ((w_in_1,),) = _norm_matmul(
        x1_0, ffn_norm_g[0][None], w_up_0, tm=tm_mm, name="fwd_up_proj_0", jobs=[_GatherJob([piece(IN, 1)])])
    (x2_0, conv0), _ = _ffn_fwd(up0, x1_0, conv_f_full[0], w_down_0.reshape(ff, d), seq=seq, tm=tm_mm, cwid=ff_chunk,
                                name="fwd_ffn_0")
    (proj1, h1_1), ((w_up_1, w_down_1),) = _norm_matmul(
        x2_0, mix_norm_g[1][None], w_in_1, tm=tm_mm, name="fwd_in_proj_1",
        jobs=[_GatherJob([piece(UP, 1), piece(DOWN, 1)])])
    (x1_1, cz1), _ = _mixer_fwd(proj1, x2_0, *mixer_args(1, w_out_1), seq=seq, tm=tm_mm, name="fwd_mixer_1")
    (up1, h2_1), _ = _norm_matmul(x1_1, ffn_norm_g[1][None], w_up_1, tm=tm_mm, name="fwd_up_proj_1")
    (x2_1, conv1), _ = _ffn_fwd(up1, x1_1, conv_f_full[1], w_down_1.reshape(ff, d), seq=seq, tm=tm_mm, cwid=ff_chunk,
                                name="fwd_ffn_1")
    dx, loss_tile, d_final_g = _final_loss(x2_1, final_norm_g[None], tgt, tm=tm_mm, name="final_loss")

    def chipwise(a):
        return a.reshape(N_CHIP, a.shape[0] // N_CHIP, a.shape[1])

    def pair_sums(grads, recvs, tag):
        return _pair_sum(grads, recvs, core, f"grad_pair_sum_{tag}")

    d_up, dwd, d_cf1 = _ffn_bwd(dx, up1, conv1, conv_f_full[1], w_down_1.reshape(ff, d), seq=seq, tm=tm_ew,
                                cwid=ff_chunk, name="bwd_ffn_1")
    (dx1, d_g2_1), _ = _matmul_bwd_x(d_up, w_up_1, x1_1, ffn_norm_g[1][None], dx, tm=tm_mm, name="bwd_up_x_1")
    (dwu,), _ = _matmul_bwd_w(h2_1, d_up, N_CHIP, tk=tk_w, name="bwd_up_w_1")
    (d_proj, dwo, d_ca1, d_lg1, d_lb1, d_ws1, d_bs1), _ = _mixer_bwd(
        dx1, proj1, cz1, *mixer_args(1, w_out_1), seq=seq, tm=tm_ew, name="bwd_mixer_1")
    (dx, d_g1_1), _ = _matmul_bwd_x(d_proj, w_in_1, x2_0, mix_norm_g[1][None], dx1, tm=tm_mm, name="bwd_in_x_1")
    (dwi,), _ = _matmul_bwd_w(h1_1, d_proj, N_CHIP, tk=tk_w, name="bwd_in_w_1")
    grads1 = [dwi, chipwise(dwo), dwu, chipwise(dwd)]
    small1 = [d_g1_1, d_ca1, d_lg1, d_lb1, d_ws1, d_bs1.T, d_g2_1, d_cf1]

    d_up, dwd, d_cf0 = _ffn_bwd(dx, up0, conv0, conv_f_full[0], w_down_0.reshape(ff, d), seq=seq, tm=tm_ew,
                                cwid=ff_chunk, name="bwd_ffn_0")
    (dx1, d_g2_0), (recv1,) = _matmul_bwd_x(d_up, w_up_0, x1_0, ffn_norm_g[0][None], dx, tm=tm_mm, name="bwd_up_x_0",
                                            jobs=[_SwapJob(grads1)])
    parts1 = pair_sums(grads1, recv1, 1)
    (dwu,), ((arr1_in,),) = _matmul_bwd_w(h2_0, d_up, N_CHIP, tk=tk_w, name="bwd_up_w_0",
                                          jobs=[_ScatterJob([parts1[IN]])])
    ud0 = [dwu, chipwise(dwd)]
    (d_proj, dwo, d_ca0, d_lg0, d_lb0, d_ws0, d_bs0), (arr1_rest, recv0_ud) = _mixer_bwd(
        dx1, proj0, cz0, *mixer_args(0, w_out_0), seq=seq, tm=tm_ew, name="bwd_mixer_0",
        jobs=[_ScatterJob(parts1[OUT:]), _SwapJob(ud0)])
    arrived1 = [arr1_in] + list(arr1_rest)
    gsum = _final_sum(grads1, recv1, arrived1, where, None, 1, n_l, "grad_final_sum_1")
    parts0_ud = pair_sums(ud0, recv0_ud, "0ud")
    out0 = chipwise(dwo)
    (dwi,), (gsum, arr0_ud, (recv0_out,)) = _matmul_bwd_w(
        h1_0, d_proj, N_CHIP, tk=tk_w, name="bwd_in_w_0",
        jobs=[_ShareJob([(g, 1) for g in gsum]), _ScatterJob(parts0_ud), _SwapJob([out0])])
    part0_out, = pair_sums([out0], [recv0_out], "0o")
    n_mm = n // tm_mm
    n_head = max(n_mm // 4, 1)
    head, ((arr0_out,), (recv0_in,)) = _matmul_bwd_x(
        d_proj, w_in_0, xs, mix_norm_g[0][None], dx1, tm=tm_mm, name="bwd_in_x_0a", tiles=(0, n_head),
        jobs=[_ScatterJob([part0_out]), _SwapJob([dwi])])
    part0_in, = pair_sums([dwi], [recv0_in], "0i")
    (dx, d_g1_0), ((arr0_in,),) = _matmul_bwd_x(
        d_proj, w_in_0, xs, mix_norm_g[0][None], dx1, tm=tm_mm, name="bwd_in_x_0b", tiles=(n_head, n_mm - n_head),
        before=head, jobs=[_ScatterJob([part0_in])])
    grad_x = dx.reshape(bsz, seq, d)
    grads0 = [dwi, out0] + ud0
    recv0 = [recv0_in, recv0_out] + list(recv0_ud)
    arrived0 = [arr0_in, arr0_out] + list(arr0_ud)
    small0 = [d_g1_0, d_ca0, d_lg0, d_lb0, d_ws0, d_bs0.T, d_g2_0, d_cf0]

    flat = [a.reshape(-1) for a in small0 + small1] + [d_final_g.reshape(-1), loss_tile[0]]
    sizes = [a.size for a in flat]
    total = sum(sizes)
    rows = -(-total // 128 // (2 * HALO)) * 2 * HALO
    pack = jnp.concatenate(flat + [jnp.zeros((rows * 128 - total,), F32)]).reshape(rows, 128)
    red, _ = _all_reduce_small(pack, "small_all_reduce")
    red = red.reshape(-1)
    gsum = _final_sum(grads0, recv0, arrived0, where, gsum, 0, n_l, "grad_final_sum_0")
    (g_in, g_out, g_up, g_down), = _comm_only([_ShareJob([(g, 0) for g in gsum])], "grad_share_0")

    pieces, off = [], 0
    for s in sizes:
        pieces.append(red[off:off + s])
        off += s
    per_layer = [pieces[l * 8:(l + 1) * 8] for l in range(n_l)]

    def stacked(idx, shape):
        return jnp.stack([per_layer[l][idx].reshape(shape) for l in range(n_l)])

    def my_cols(a, width):
        return lax.dynamic_slice_in_dim(a, chip * width, width, axis=-1)

    g_mix_norm = stacked(0, (d,))
    g_conv_a = my_cols(stacked(1, (3, d)), d // N_CHIP)
    g_ln_g = stacked(2, (d,))
    g_ln_b = stacked(3, (d,))
    g_ws = stacked(4, (groups, CHUNK, CHUNK))
    g_bs = stacked(5, (groups, CHUNK))
    g_ffn_norm = stacked(6, (d,))
    g_conv_f = my_cols(stacked(7, (3, 2 * ff)), 2 * ff // N_CHIP)
    g_final = pieces[-2].reshape(1, d)
    loss = pieces[-1][0]

    big_d, big_nm, big_nv, big_g = _adamw_big(
        [w_in, w_out, w_up, w_down], [g_in, g_out, g_up, g_down], [m_w_in, m_w_out, m_w_up, m_w_down],
        [v_w_in, v_w_out, v_w_up, v_w_down], "adamw_big")

    sm_w = [mix_norm_g, conv_a_w, ln_v_g, ln_v_b, w_s, b_s, ffn_norm_g, conv_ffn_w, final_norm_g[None]]
    sm_g = [g_mix_norm, g_conv_a, g_ln_g, g_ln_b, g_ws, g_bs, g_ffn_norm, g_conv_f, g_final]
    sm_m = [m_mix_norm_g, m_conv_a_w, m_ln_v_g, m_ln_v_b, m_w_s, m_b_s, m_ffn_norm_g, m_conv_ffn_w, m_final_norm_g[None]]
    sm_v = [v_mix_norm_g, v_conv_a_w, v_ln_v_g, v_ln_v_b, v_w_s, v_b_s, v_ffn_norm_g, v_conv_ffn_w, v_final_norm_g[None]]
    sm_d, sm_nm, sm_nv = _adamw_small(sm_w, sm_g, sm_m, sm_v, "adamw_small")

    def ordered(sm, bigs):
        return [sm[0], bigs[0], sm[1], sm[2], sm[3], sm[4], sm[5], bigs[1], sm[6], bigs[2], sm[7], bigs[3],
                sm[8].reshape(d)]

    out_g = ordered(sm_g, big_g)
    out_d = ordered(sm_d, big_d)
    out_m = ordered(sm_nm, big_nm)
    out_v = ordered(sm_nv, big_nv)
    return (loss, grad_x, *out_g, *out_d, *out_m, *out_v)
```

```python
import jax
import jax.numpy as jnp
from jax import lax
from jax.experimental import pallas as pl
from jax.experimental.pallas import tpu as pltpu

F32 = jnp.float32
BF16 = jnp.bfloat16
EPS = 1e-6
CHUNK = 128
N_CHIP = 4
HALO = 8
N_PIECES = 7
LATE_STEPS = 2
VMEM_LIMIT_V7X = 56 * 1024 * 1024
MESH_T = pl.DeviceIdType.MESH
HBM_SPEC = pl.BlockSpec(memory_space=pltpu.HBM)
VMEM_SPEC = pl.BlockSpec(memory_space=pltpu.VMEM)

ADAM_LR, ADAM_B1, ADAM_B2, ADAM_EPS, ADAM_WD, ADAM_STEP = 0.001, 0.9, 0.999, 1e-08, 0.01, 10

NT_DIMS = (((1,), (1,)), ((), ()))
TN_DIMS = (((0,), (0,)), ((), ()))


def _resident(block_shape, index_map):
    return pl.BlockSpec(block_shape, index_map, pipeline_mode=pl.Buffered(1))


def _full(shape):
    return pl.BlockSpec(shape, lambda *_: (0,) * len(shape))


def _row_tile(seq, want):
    t = min(seq, want)
    assert seq % t == 0 and t % CHUNK == 0, (seq, want)
    return t


def _shift_prev(prev8, cur, k):
    ext = jnp.concatenate([prev8, cur], axis=0)
    return pltpu.roll(ext, k, 0)[HALO:]


def _shift_next(cur, next8, k):
    ext = jnp.concatenate([cur, next8], axis=0)
    n = ext.shape[0]
    return pltpu.roll(ext, n - k, 0)[:n - HALO]


def _rowsum(a):
    return jnp.sum(a, axis=0, keepdims=True)


def _lanemean(a):
    return jnp.mean(a, axis=-1, keepdims=True)


def _tril_weights(ws_ref, groups):
    r = lax.broadcasted_iota(jnp.int32, (CHUNK, CHUNK), 0)
    c = lax.broadcasted_iota(jnp.int32, (CHUNK, CHUNK), 1)
    tril = r >= c
    return tril, [jnp.where(tril, ws_ref[g], 0.0) for g in range(groups)]


def _halo_prev(tm):
    return lambda i: jnp.maximum(i * (tm // HALO) - 1, 0)


def _halo_next(tm, n):
    last = n // HALO - 1
    return lambda i: jnp.minimum((i + 1) * (tm // HALO), last)


def _mesh_pos():
    return lax.axis_index("x"), lax.axis_index("y"), lax.axis_index("c")


def _peer_chip(x, y, r):
    return (1 - x if r >> 1 else x), (1 - y if r & 1 else y)


def _remote(src, dst, sems, k, to):
    return pltpu.make_async_remote_copy(src_ref=src, dst_ref=dst, send_sem=sems[0].at[k], recv_sem=sems[1].at[k],
                                        device_id=to, device_id_type=MESH_T)


class _GatherJob:
    def __init__(self, pieces):
        self.pieces = pieces
        self.ins = [p[0] for p in pieces]
        self.out_shapes = [jax.ShapeDtypeStruct((N_CHIP,) + p[0].shape[1:], p[0].dtype) for p in pieces]
        self.aliases = {}
        n = len(pieces)
        self.sems = [pltpu.SemaphoreType.DMA((3 * n,))] * 4 + [pltpu.SemaphoreType.DMA((n,))]

    def _half(self, outs, a, of_chip, core):
        rh = outs[a].shape[1] // 2
        return outs[a].at[of_chip, pl.ds(core * rh, rh), :]

    def _own(self, ins, outs, sems, chip):
        return [pltpu.make_async_copy(ins[a].at[layer], outs[a].at[chip], sems[4].at[a])
                for a, (_, layer) in enumerate(self.pieces)]

    def _sends(self, ins, outs, sems, x, y, c):
        chip, out = 2 * x + y, []
        for a, (_, layer) in enumerate(self.pieces):
            rh = outs[a].shape[1] // 2
            for r in (1, 2, 3):
                px, py = _peer_chip(x, y, r)
                out.append(_remote(ins[a].at[layer, pl.ds(c * rh, rh), :], self._half(outs, a, chip, c), sems[0:2],
                                   3 * a + r - 1, (px, py, c)))
        return out

    def _passes(self, outs, sems, x, y, c, core):
        out = []
        for a in range(len(self.pieces)):
            for r in (1, 2, 3):
                px, py = _peer_chip(x, y, r)
                landed = self._half(outs, a, 2 * px + py, core)
                out.append(_remote(landed, landed, sems[2:4], 3 * a + r - 1, (x, y, 1 - c)))
        return out

    def start(self, ins, outs, sems):
        x, y, c = _mesh_pos()
        for cp in self._own(ins, outs, sems, 2 * x + y) + self._sends(ins, outs, sems, x, y, c):
            cp.start()

    def late(self, ins, outs, sems):
        x, y, c = _mesh_pos()
        passes = self._passes(outs, sems, x, y, c, c)
        k = 0
        for a in range(len(self.pieces)):
            for r in (1, 2, 3):
                px, py = _peer_chip(x, y, r)
                landed = self._half(outs, a, 2 * px + py, c)
                _remote(landed, landed, sems[0:2], k, (px, py, c)).wait_recv()
                passes[k].start()
                k += 1

    def finish(self, ins, outs, sems):
        x, y, c = _mesh_pos()
        for cp in self._passes(outs, sems, x, y, c, 1 - c):
            cp.wait_recv()
        for cp in self._sends(ins, outs, sems, x, y, c) + self._passes(outs, sems, x, y, c, c):
            cp.wait_send()
        for cp in self._own(ins, outs, sems, 2 * x + y):
            cp.wait()


class _SmallGatherJob:
    def __init__(self, pieces):
        self.ins = list(pieces)
        self.out_shapes = [jax.ShapeDtypeStruct((N_CHIP,) + p.shape, p.dtype) for p in pieces]
        self.aliases = {}
        n = len(pieces)
        self.sems = [pltpu.SemaphoreType.DMA((3 * n,))] * 2 + [pltpu.SemaphoreType.DMA((n,))]

    def _outgoing(self, ins, outs, sems):
        x, y, c = _mesh_pos()
        mine = [outs[a].at[2 * x + y] for a in range(len(ins))]
        own = [pltpu.make_async_copy(ins[a], mine[a], sems[2].at[a]) for a in range(len(ins))]
        sends = [_remote(ins[a], mine[a], sems[0:2], 3 * a + r - 1, (*_peer_chip(x, y, r), c))
                 for a in range(len(ins)) for r in (1, 2, 3)]
        return own, sends

    def start(self, ins, outs, sems):
        own, sends = self._outgoing(ins, outs, sems)
        for cp in own + sends:
            cp.start()

    def finish(self, ins, outs, sems):
        x, y, c = _mesh_pos()
        for a in range(len(ins)):
            for r in (1, 2, 3):
                px, py = _peer_chip(x, y, r)
                landed = outs[a].at[2 * px + py]
                _remote(landed, landed, sems[0:2], 3 * a + r - 1, (px, py, c)).wait_recv()
        own, sends = self._outgoing(ins, outs, sems)
        for cp in sends:
            cp.wait_send()
        for cp in own:
            cp.wait()


class _SwapJob:
    def __init__(self, pieces):
        self.ins = list(pieces)
        self.out_shapes = [jax.ShapeDtypeStruct((g.shape[0], g.shape[1] // 2, g.shape[2]), g.dtype) for g in pieces]
        self.aliases = {}
        self.sems = [pltpu.SemaphoreType.DMA((len(pieces),))] * 2

    def _copies(self, ins, outs, sems):
        x, y, c = _mesh_pos()
        out = []
        for a in range(len(ins)):
            rh = ins[a].shape[1] // 2
            out.append(_remote(ins[a].at[:, pl.ds((1 - c) * rh, rh), :], outs[a], sems, a, (x, y, 1 - c)))
        return out

    def start(self, ins, outs, sems):
        for cp in self._copies(ins, outs, sems):
            cp.start()

    def finish(self, ins, outs, sems):
        for cp in self._copies(ins, outs, sems):
            cp.wait()


class _ScatterJob:
    def __init__(self, pieces):
        self.ins = list(pieces)
        self.out_shapes = [jax.ShapeDtypeStruct((3,) + p.shape[1:], p.dtype) for p in pieces]
        self.aliases = {}
        self.sems = [pltpu.SemaphoreType.DMA((3 * len(pieces),))] * 2

    def _copies(self, ins, outs, sems):
        x, y, c = _mesh_pos()
        out = []
        for a in range(len(ins)):
            for r in (1, 2, 3):
                px, py = _peer_chip(x, y, r)
                out.append(_remote(ins[a].at[2 * px + py], outs[a].at[r - 1], sems, 3 * a + r - 1, (px, py, c)))
        return out

    def start(self, ins, outs, sems):
        for cp in self._copies(ins, outs, sems):
            cp.start()

    def finish(self, ins, outs, sems):
        for cp in self._copies(ins, outs, sems):
            cp.wait()


class _ShareJob:
    def __init__(self, pieces):
        self.pieces = pieces
        self.ins = [p[0] for p in pieces]
        self.out_shapes = [jax.ShapeDtypeStruct(p[0].shape, p[0].dtype) for p in pieces]
        self.aliases = {a: a for a in range(len(pieces))}
        self.sems = [pltpu.SemaphoreType.DMA((len(pieces),))] * 2

    def _copies(self, outs, sems, core):
        x, y, c = _mesh_pos()
        out = []
        for a, (_, layer) in enumerate(self.pieces):
            rh = outs[a].shape[1] // 2
            rows = outs[a].at[layer, pl.ds(core * rh, rh), :]
            out.append(_remote(rows, rows, sems, a, (x, y, 1 - c)))
        return out

    def start(self, ins, outs, sems):
        for cp in self._copies(outs, sems, lax.axis_index("c")):
            cp.start()

    def finish(self, ins, outs, sems):
        c = lax.axis_index("c")
        for cp in self._copies(outs, sems, c):
            cp.wait_send()
        for cp in self._copies(outs, sems, 1 - c):
            cp.wait_recv()


def _hosted_call(main, *, name, grid, in_specs, out_specs, out_shape, scratch_shapes=(), jobs=(), semantics=None,
                 operands=(), main_aliases=None):
    n_in, n_out, n_sc = len(in_specs), len(out_specs), len(scratch_shapes)
    counts = [(len(j.ins), len(j.out_shapes), len(j.sems)) for j in jobs]
    j_in, j_out, j_sc = (sum(c[k] for c in counts) for k in range(3))
    aliases, i0, o0 = dict(main_aliases or {}), n_in, n_out
    for j, (ci, co, _) in zip(jobs, counts):
        aliases.update({i0 + a: o0 + b for a, b in j.aliases.items()})
        i0, o0 = i0 + ci, o0 + co

    def body(*refs):
        cuts = [0, n_in, n_in + j_in, n_in + j_in + n_out, n_in + j_in + n_out + j_out,
                n_in + j_in + n_out + j_out + n_sc, len(refs)]
        m_in, jb_in, m_out, jb_out, m_sc, jb_sc = (list(refs[cuts[k]:cuts[k + 1]]) for k in range(6))

        def run(phase):
            i0 = o0 = s0 = 0
            for j, (ci, co, cs) in zip(jobs, counts):
                if hasattr(j, phase):
                    getattr(j, phase)(jb_in[i0:i0 + ci], jb_out[o0:o0 + co], jb_sc[s0:s0 + cs])
                i0, o0, s0 = i0 + ci, o0 + co, s0 + cs

        step, n_steps = 0, 1
        for ax in range(len(grid)):
            step, n_steps = step * grid[ax] + pl.program_id(ax), n_steps * grid[ax]
        if jobs:
            pl.when(step == 0)(lambda: run("start"))
        main(m_in, m_out, m_sc)
        if jobs:
            pl.when(step == max(n_steps - 1 - LATE_STEPS, 0))(lambda: run("late"))
            pl.when(step == n_steps - 1)(lambda: run("finish"))

    if semantics is None or jobs:
        semantics = ("arbitrary",) * len(grid)
    outs = pl.pallas_call(
        body, name=name, grid=grid,
        in_specs=list(in_specs) + [HBM_SPEC] * j_in,
        out_specs=list(out_specs) + [HBM_SPEC] * j_out,
        out_shape=list(out_shape) + [s for j in jobs for s in j.out_shapes],
        scratch_shapes=list(scratch_shapes) + [s for j in jobs for s in j.sems],
        input_output_aliases=aliases,
        compiler_params=pltpu.CompilerParams(dimension_semantics=semantics, vmem_limit_bytes=VMEM_LIMIT_V7X),
    )(*operands, *[a for j in jobs for a in j.ins])
    main_outs, rest, job_outs = list(outs[:n_out]), list(outs[n_out:]), []
    for _, co, _ in counts:
        job_outs.append(rest[:co])
        rest = rest[co:]
    return main_outs, job_outs


def _comm_only(jobs, name):
    return _hosted_call(lambda i, o, s: None, name=name, grid=(1,), in_specs=[], out_specs=[], out_shape=[],
                        jobs=jobs)[1]


def _norm_matmul(x, g, wg, *, tm, name, jobs=()):
    n, d = x.shape
    n_chip, _, ck = wg.shape

    def main(ins, outs, _):
        x_ref, g_ref, w_ref = ins
        o_ref, ht_ref = outs
        xf = x_ref[...]
        r = lax.rsqrt(_lanemean(xf * xf) + EPS)
        h = xf * r * g_ref[...]
        ht_ref[...] = h.T.astype(BF16)
        hb = h.astype(BF16)
        for j in range(n_chip):
            o_ref[:, j * ck:(j + 1) * ck] = jnp.dot(hb, w_ref[j], preferred_element_type=F32).astype(BF16)

    return _hosted_call(
        main, name=name, grid=(n // tm,), jobs=jobs, semantics=("parallel",), operands=(x, g, wg),
        in_specs=[pl.BlockSpec((tm, d), lambda i: (i, 0)), _full((1, d)), _resident((n_chip, d, ck), lambda i: (0, 0, 0))],
        out_specs=[pl.BlockSpec((tm, n_chip * ck), lambda i: (i, 0)), pl.BlockSpec((d, tm), lambda i: (0, i))],
        out_shape=[jax.ShapeDtypeStruct((n, n_chip * ck), BF16), jax.ShapeDtypeStruct((d, n), BF16)])


def _mixer_fwd(proj, x, cw, lg, lb, ws, bmap, wout, *, seq, tm, name, jobs=()):
    n, d = x.shape
    groups = ws.shape[0]
    gd = d // groups
    prev = _halo_prev(tm)

    def main(ins, outs, scratch):
        proj_ref, pcg_ref, pxi_ref, x_ref, cw_ref, lg_ref, lb_ref, ws_ref, bmap_ref, wout_ref = ins
        o_ref, cz_ref = outs
        vn_s, mixed_s = scratch
        seq_start = (pl.program_id(0) * tm) % seq == 0

        def piece(k):
            return proj_ref[:, k * d:(k + 1) * d].astype(F32)

        z = piece(1) * piece(2)
        zprev = jnp.where(seq_start, 0.0, pcg_ref[...].astype(F32) * pxi_ref[...].astype(F32))
        cz = (cw_ref[0:1, :] * _shift_prev(zprev, z, 2) + cw_ref[1:2, :] * _shift_prev(zprev, z, 1)
              + cw_ref[2:3, :] * z)
        cz_ref[...] = cz.astype(BF16)
        ya = piece(0) * cz

        v = piece(4)
        xc = v - _lanemean(v)
        vn = xc * lax.rsqrt(_lanemean(xc * xc) + EPS) * lg_ref[...] + lb_ref[...]
        vn_s[...] = vn.astype(BF16)
        _, wsm = _tril_weights(ws_ref, groups)
        for ck in range(tm // CHUNK):
            rows = slice(ck * CHUNK, (ck + 1) * CHUNK)
            for g in range(groups):
                cols = slice(g * gd, (g + 1) * gd)
                mixed_s[rows, cols] = (jnp.dot(wsm[g].astype(BF16), vn_s[rows, cols], preferred_element_type=F32)
                                       + bmap_ref[:, cols])
        yb = piece(3) * mixed_s[...]
        merged = jax.nn.sigmoid(piece(5)) * ya + jax.nn.sigmoid(piece(6)) * yb
        o_ref[...] = x_ref[...] + jnp.dot(merged.astype(BF16), wout_ref[...], preferred_element_type=F32)

    return _hosted_call(
        main, name=name, grid=(n // tm,), jobs=jobs, semantics=("parallel",),
        operands=(proj, proj, proj, x, cw, lg, lb, ws, bmap, wout),
        in_specs=[pl.BlockSpec((tm, N_PIECES * d), lambda i: (i, 0)),
                  pl.BlockSpec((HALO, d), lambda i: (prev(i), 1)),
                  pl.BlockSpec((HALO, d), lambda i: (prev(i), 2)),
                  pl.BlockSpec((tm, d), lambda i: (i, 0)),
                  _full((3, d)), _full((1, d)), _full((1, d)), _full((groups, CHUNK, CHUNK)), _full((CHUNK, d)),
                  _resident((d, d), lambda i: (0, 0))],
        out_specs=[pl.BlockSpec((tm, d), lambda i: (i, 0)), pl.BlockSpec((tm, d), lambda i: (i, 0))],
        out_shape=[jax.ShapeDtypeStruct((n, d), F32), jax.ShapeDtypeStruct((n, d), BF16)],
        scratch_shapes=[pltpu.VMEM((tm, d), BF16), pltpu.VMEM((tm, d), F32)])


def _ffn_fwd(up, x, cw, wd, *, seq, tm, cwid, name, jobs=()):
    n, d = x.shape
    ff = wd.shape[0]
    prev = _halo_prev(tm)

    def main(ins, outs, _):
        up_ref, pup_ref, x_ref, cw_ref, wd_ref = ins
        o_ref, conv_ref = outs
        seq_start = (pl.program_id(0) * tm) % seq == 0

        def conv(lo):
            cols = slice(lo, lo + cwid)
            cur = up_ref[:, cols].astype(F32)
            pre = jnp.where(seq_start, 0.0, pup_ref[:, cols].astype(F32))
            out = (cw_ref[0:1, cols] * _shift_prev(pre, cur, 2) + cw_ref[1:2, cols] * _shift_prev(pre, cur, 1)
                   + cw_ref[2:3, cols] * cur)
            conv_ref[:, cols] = out.astype(BF16)
            return out

        acc = x_ref[...]
        for cj in range(ff // cwid):
            gate = conv(cj * cwid)
            val = conv(ff + cj * cwid)
            a = gate * jax.nn.sigmoid(gate) * val
            acc = acc + jnp.dot(a.astype(BF16), wd_ref[cj * cwid:(cj + 1) * cwid, :], preferred_element_type=F32)
        o_ref[...] = acc

    return _hosted_call(
        main, name=name, grid=(n // tm,), jobs=jobs, semantics=("parallel",), operands=(up, up, x, cw, wd),
        in_specs=[pl.BlockSpec((tm, 2 * ff), lambda i: (i, 0)),
                  pl.BlockSpec((HALO, 2 * ff), lambda i: (prev(i), 0)),
                  pl.BlockSpec((tm, d), lambda i: (i, 0)),
                  _full((3, 2 * ff)),
                  _resident((ff, d), lambda i: (0, 0))],
        out_specs=[pl.BlockSpec((tm, d), lambda i: (i, 0)), pl.BlockSpec((tm, 2 * ff), lambda i: (i, 0))],
        out_shape=[jax.ShapeDtypeStruct((n, d), F32), jax.ShapeDtypeStruct((n, 2 * ff), BF16)])


def _final_loss(x, g, target, *, tm, name):
    n, d = x.shape

    def body(x_ref, g_ref, t_ref, dx_ref, loss_ref, dg_ref):
        @pl.when(pl.program_id(0) == 0)
        def _():
            loss_ref[...] = jnp.zeros_like(loss_ref)
            dg_ref[...] = jnp.zeros_like(dg_ref)

        xf = x_ref[...]
        r = lax.rsqrt(_lanemean(xf * xf) + EPS)
        xhat = xf * r
        diff = xhat * g_ref[...] - t_ref[...]
        loss_ref[...] += (0.5 / d) * _rowsum(jnp.sum(diff * diff, axis=-1, keepdims=True))
        dy = diff * (1.0 / d)
        dg_ref[...] += _rowsum(dy * xhat)
        dyh = dy * g_ref[...]
        dx_ref[...] = r * (dyh - xhat * _lanemean(dyh * xhat))

    return pl.pallas_call(
        body, name=name, grid=(n // tm,),
        in_specs=[pl.BlockSpec((tm, d), lambda i: (i, 0)), _full((1, d)), pl.BlockSpec((tm, d), lambda i: (i, 0))],
        out_specs=[pl.BlockSpec((tm, d), lambda i: (i, 0)), _full((HALO, CHUNK)), _full((1, d))],
        out_shape=[jax.ShapeDtypeStruct((n, d), F32), jax.ShapeDtypeStruct((HALO, CHUNK), F32),
                   jax.ShapeDtypeStruct((1, d), F32)],
        compiler_params=pltpu.CompilerParams(dimension_semantics=("arbitrary",), vmem_limit_bytes=VMEM_LIMIT_V7X),
    )(x, g, target)


def _ffn_bwd(dx, up, conv, cw, wd, *, seq, tm, cwid, name):
    n, d = dx.shape
    ff = wd.shape[0]
    nxt = _halo_next(tm, n)

    def body(dx_ref, dxn_ref, up_ref, conv_ref, nconv_ref, cw_ref, wd_ref, dup_ref, dwd_ref, dcw_ref):
        i = pl.program_id(0)
        keep_next = jnp.where(((i + 1) * tm) % seq == 0, 0.0, 1.0)

        @pl.when(i == 0)
        def _():
            dwd_ref[...] = jnp.zeros_like(dwd_ref)
            dcw_ref[...] = jnp.zeros_like(dcw_ref)

        dxe = jnp.concatenate([dx_ref[...], dxn_ref[...]], axis=0).astype(BF16)
        dxb = dxe[:tm]
        for cj in range(ff // cwid):
            rows = slice(cj * cwid, (cj + 1) * cwid)
            g_cols, v_cols = slice(cj * cwid, (cj + 1) * cwid), slice(ff + cj * cwid, ff + (cj + 1) * cwid)
            dae = lax.dot_general(dxe, wd_ref[rows, :], NT_DIMS, preferred_element_type=F32)
            da, dan = dae[:tm], dae[tm:]

            def grads(gate, val, da_rows):
                sg = jax.nn.sigmoid(gate)
                sl = gate * sg
                return sl, da_rows * val * sg * (1.0 + gate * (1.0 - sg)), da_rows * sl

            gate, val = conv_ref[:, g_cols].astype(F32), conv_ref[:, v_cols].astype(F32)
            sl, d_gate, d_val = grads(gate, val, da)
            dwd_ref[rows, :] += lax.dot_general((sl * val).astype(BF16), dxb, TN_DIMS, preferred_element_type=F32)
            _, d_gate_n, d_val_n = grads(nconv_ref[:, g_cols].astype(F32), nconv_ref[:, v_cols].astype(F32),
                                         dan * keep_next)
            for cols, dcur, dnext in ((g_cols, d_gate, d_gate_n), (v_cols, d_val, d_val_n)):
                upc = up_ref[:, cols].astype(F32)
                d1, d2 = _shift_next(dcur, dnext, 1), _shift_next(dcur, dnext, 2)
                for k, dk in enumerate((d2, d1, dcur)):
                    dcw_ref[k:k + 1, cols] += _rowsum(dk * upc)
                dup = cw_ref[2:3, cols] * dcur + cw_ref[1:2, cols] * d1 + cw_ref[0:1, cols] * d2
                dup_ref[:, cols] = dup.astype(BF16)

    return pl.pallas_call(
        body, name=name, grid=(n // tm,),
        in_specs=[pl.BlockSpec((tm, d), lambda i: (i, 0)),
                  pl.BlockSpec((HALO, d), lambda i: (nxt(i), 0)),
                  pl.BlockSpec((tm, 2 * ff), lambda i: (i, 0)),
                  pl.BlockSpec((tm, 2 * ff), lambda i: (i, 0)),
                  pl.BlockSpec((HALO, 2 * ff), lambda i: (nxt(i), 0)),
                  _full((3, 2 * ff)),
                  _resident((ff, d), lambda i: (0, 0))],
        out_specs=[pl.BlockSpec((tm, 2 * ff), lambda i: (i, 0)), _full((ff, d)), _full((3, 2 * ff))],
        out_shape=[jax.ShapeDtypeStruct((n, 2 * ff), BF16), jax.ShapeDtypeStruct((ff, d), F32),
                   jax.ShapeDtypeStruct((3, 2 * ff), F32)],
        compiler_params=pltpu.CompilerParams(dimension_semantics=("arbitrary",), vmem_limit_bytes=VMEM_LIMIT_V7X),
    )(dx, dx, up, conv, conv, cw, wd)


def _matmul_bwd_x(dy, wg, x, g, dres, *, tm, name, jobs=(), tiles=None, before=None):
    n, d = x.shape
    n_chip, _, ck = wg.shape
    first_tile, n_tiles = tiles or (0, n // tm)
    extra = [] if before is None else list(before)

    def main(ins, outs, _):
        dy_ref, w_ref, x_ref, g_ref, dres_ref = ins[:5]
        dx_ref, dg_ref = outs

        @pl.when(pl.program_id(0) == 0)
        def _():
            dg_ref[...] = jnp.zeros_like(dg_ref) if before is None else ins[6][...]

        dh = lax.dot_general(dy_ref[:, 0:ck], w_ref[0], NT_DIMS, preferred_element_type=F32)
        for j in range(1, n_chip):
            dh = dh + lax.dot_general(dy_ref[:, j * ck:(j + 1) * ck], w_ref[j], NT_DIMS, preferred_element_type=F32)
        xf = x_ref[...]
        r = lax.rsqrt(_lanemean(xf * xf) + EPS)
        xhat = xf * r
        dg_ref[...] += _rowsum(dh * xhat)
        dyh = dh * g_ref[...]
        dx_ref[...] = dres_ref[...] + r * (dyh - xhat * _lanemean(dyh * xhat))

    rows = lambda width: pl.BlockSpec((tm, width), lambda i: (i + first_tile, 0))
    return _hosted_call(
        main, name=name, grid=(n_tiles,), jobs=jobs, semantics=("arbitrary",), operands=(dy, wg, x, g, dres, *extra),
        in_specs=[rows(n_chip * ck), _resident((n_chip, d, ck), lambda i: (0, 0, 0)), rows(d), _full((1, d)), rows(d)]
        + ([] if before is None else [pl.BlockSpec(memory_space=pl.ANY), _full((1, d))]),
        out_specs=[rows(d), _full((1, d))],
        out_shape=[jax.ShapeDtypeStruct((n, d), F32), jax.ShapeDtypeStruct((1, d), F32)],
        main_aliases=None if before is None else {5: 0})


def _matmul_bwd_w(ht, dy, n_chip, *, tk, name, jobs=()):
    d, n = ht.shape
    ck = dy.shape[1] // n_chip

    def main(ins, outs, _):
        ht_ref, dy_ref = ins
        o_ref, = outs

        @pl.when(pl.program_id(1) == 0)
        def _():
            o_ref[...] = jnp.zeros_like(o_ref)

        o_ref[...] += jnp.dot(ht_ref[...], dy_ref[...], preferred_element_type=F32)

    return _hosted_call(
        main, name=name, grid=(n_chip, n // tk), jobs=jobs, semantics=("parallel", "arbitrary"), operands=(ht, dy),
        in_specs=[pl.BlockSpec((d, tk), lambda j, k: (0, k)), pl.BlockSpec((tk, ck), lambda j, k: (k, j))],
        out_specs=[pl.BlockSpec((None, d, ck), lambda j, k: (j, 0, 0))],
        out_shape=[jax.ShapeDtypeStruct((n_chip, d, ck), F32)])


def _mixer_bwd(dx, proj, cz, cw, lg, lb, ws, bmap, wout, *, seq, tm, name, jobs=()):
    n, d = dx.shape
    groups = ws.shape[0]
    gd = d // groups
    nxt = _halo_next(tm, n)
    n_tiles = n // tm

    def main(ins, outs, scratch):
        proj_ref, cz_ref, nbg_ref, nga_ref, dx_ref, dxn_ref, cw_ref, lg_ref, lb_ref, ws_ref, bmap_ref, wout_ref = ins
        dproj_ref, dwout_ref, dcw_ref, dlg_ref, dlb_ref, dws_ref, dbs_ref = outs
        vn_s, mixed_s, dmix_s, dvn_s, dbmap_s = scratch
        i = pl.program_id(0)
        keep_next = jnp.where(((i + 1) * tm) % seq == 0, 0.0, 1.0)

        @pl.when(i == 0)
        def _():
            for ref in (dwout_ref, dcw_ref, dlg_ref, dlb_ref, dws_ref, dbmap_s):
                ref[...] = jnp.zeros_like(ref)

        def piece(k):
            return proj_ref[:, k * d:(k + 1) * d].astype(F32)

        def put(k, val):
            dproj_ref[:, k * d:(k + 1) * d] = val.astype(BF16)

        w = [cw_ref[k:k + 1, :] for k in range(3)]
        cg, xi = piece(1), piece(2)
        cz = cz_ref[...].astype(F32)
        bg = piece(0)
        ya = bg * cz

        v = piece(4)
        xc = v - _lanemean(v)
        rstd = lax.rsqrt(_lanemean(xc * xc) + EPS)
        vhat = xc * rstd
        vn_s[...] = (vhat * lg_ref[...] + lb_ref[...]).astype(BF16)
        tril, wsm = _tril_weights(ws_ref, groups)
        for ck in range(tm // CHUNK):
            rows = slice(ck * CHUNK, (ck + 1) * CHUNK)
            for g in range(groups):
                cols = slice(g * gd, (g + 1) * gd)
                mixed_s[rows, cols] = (jnp.dot(wsm[g].astype(BF16), vn_s[rows, cols], preferred_element_type=F32)
                                       + bmap_ref[:, cols])
        u = piece(3)
        mixed = mixed_s[...]
        yb = u * mixed
        sa, sb = jax.nn.sigmoid(piece(5)), jax.nn.sigmoid(piece(6))
        merged = sa * ya + sb * yb

        dxe = jnp.concatenate([dx_ref[...], dxn_ref[...]], axis=0).astype(BF16)
        dme = lax.dot_general(dxe, wout_ref[...], NT_DIMS, preferred_element_type=F32)
        dm, dm_n = dme[:tm], dme[tm:]
        dwout_ref[...] += lax.dot_general(merged.astype(BF16), dxe[:tm], TN_DIMS, preferred_element_type=F32)

        put(5, dm * ya * sa * (1.0 - sa))
        put(6, dm * yb * sb * (1.0 - sb))
        d_ya, d_yb = dm * sa, dm * sb
        put(0, d_ya * cz)
        d_cz = d_ya * bg
        d_cz_n = dm_n * jax.nn.sigmoid(nga_ref[...].astype(F32)) * nbg_ref[...].astype(F32) * keep_next
        d_cz1, d_cz2 = _shift_next(d_cz, d_cz_n, 1), _shift_next(d_cz, d_cz_n, 2)
        z = cg * xi
        for k, dk in enumerate((d_cz2, d_cz1, d_cz)):
            dcw_ref[k:k + 1, :] += _rowsum(dk * z)
        dz = w[2] * d_cz + w[1] * d_cz1 + w[0] * d_cz2
        put(1, dz * xi)
        put(2, dz * cg)

        put(3, d_yb * mixed)
        d_mixed = d_yb * u
        dmix_s[...] = d_mixed.astype(BF16)
        for ck in range(tm // CHUNK):
            rows = slice(ck * CHUNK, (ck + 1) * CHUNK)
            dbmap_s[...] += d_mixed[rows, :]
            for g in range(groups):
                cols = slice(g * gd, (g + 1) * gd)
                dvn_s[rows, cols] = jnp.dot(wsm[g].T.astype(BF16), dmix_s[rows, cols], preferred_element_type=F32)
                dws_ref[g] += jnp.where(
                    tril, lax.dot_general(dmix_s[rows, cols], vn_s[rows, cols], NT_DIMS, preferred_element_type=F32),
                    0.0)
        d_vn = dvn_s[...]
        dlg_ref[...] += _rowsum(d_vn * vhat)
        dlb_ref[...] += _rowsum(d_vn)
        d_vhat = d_vn * lg_ref[...]
        put(4, rstd * (d_vhat - _lanemean(d_vhat) - vhat * _lanemean(d_vhat * vhat)))

        @pl.when(i == n_tiles - 1)
        def _():
            for g in range(groups):
                dbs_ref[:, g:g + 1] = jnp.sum(dbmap_s[:, g * gd:(g + 1) * gd], axis=-1, keepdims=True)

    return _hosted_call(
        main, name=name, grid=(n_tiles,), jobs=jobs, semantics=("arbitrary",),
        operands=(proj, cz, proj, proj, dx, dx, cw, lg, lb, ws, bmap, wout),
        in_specs=[pl.BlockSpec((tm, N_PIECES * d), lambda i: (i, 0)),
                  pl.BlockSpec((tm, d), lambda i: (i, 0)),
                  pl.BlockSpec((HALO, d), lambda i: (nxt(i), 0)),
                  pl.BlockSpec((HALO, d), lambda i: (nxt(i), 5)),
                  pl.BlockSpec((tm, d), lambda i: (i, 0)),
                  pl.BlockSpec((HALO, d), lambda i: (nxt(i), 0)),
                  _full((3, d)), _full((1, d)), _full((1, d)), _full((groups, CHUNK, CHUNK)), _full((CHUNK, d)),
                  _resident((d, d), lambda i: (0, 0))],
        out_specs=[pl.BlockSpec((tm, N_PIECES * d), lambda i: (i, 0)), _full((d, d)),
                   _full((3, d)), _full((1, d)), _full((1, d)), _full((groups, CHUNK, CHUNK)), _full((CHUNK, groups))],
        out_shape=[jax.ShapeDtypeStruct((n, N_PIECES * d), BF16), jax.ShapeDtypeStruct((d, d), F32),
                   jax.ShapeDtypeStruct((3, d), F32), jax.ShapeDtypeStruct((1, d), F32),
                   jax.ShapeDtypeStruct((1, d), F32), jax.ShapeDtypeStruct((groups, CHUNK, CHUNK), F32),
                   jax.ShapeDtypeStruct((CHUNK, groups), F32)],
        scratch_shapes=[pltpu.VMEM((tm, d), BF16), pltpu.VMEM((tm, d), F32), pltpu.VMEM((tm, d), BF16),
                        pltpu.VMEM((tm, d), F32), pltpu.VMEM((CHUNK, d), F32)])


def _all_reduce_small(pack, name, jobs=()):
    rows = pack.shape[0]
    rh = rows // 2
    assert rh % HALO == 0, rows

    def main(ins, outs, scratch):
        p_ref, = ins
        o_ref, = outs
        sib_buf, chip_buf, got_buf, send_sem, recv_sem = scratch
        sems = (send_sem, recv_sem)
        x, y, c = _mesh_pos()
        chip, sibling = 2 * x + y, (x, y, 1 - c)
        mine = pl.ds(pl.multiple_of(c * rh, HALO), rh)
        theirs = pl.ds(pl.multiple_of((1 - c) * rh, HALO), rh)
        swap = _remote(p_ref.at[theirs], sib_buf, sems, 0, sibling)
        swap.start()
        swap.wait()
        chip_buf[chip] = p_ref[mine, :] + sib_buf[...]
        copies = []
        for r in (1, 2, 3):
            px, py = _peer_chip(x, y, r)
            copies.append(_remote(chip_buf.at[chip], chip_buf.at[chip], sems, r, (px, py, c)))
        for cp in copies:
            cp.start()
        for r, cp in zip((1, 2, 3), copies):
            px, py = _peer_chip(x, y, r)
            landed = chip_buf.at[2 * px + py]
            _remote(landed, landed, sems, r, (px, py, c)).wait_recv()
            cp.wait_send()
        total = ((chip_buf[0] + chip_buf[1]) + chip_buf[2]) + chip_buf[3]
        o_ref[mine, :] = total
        chip_buf[chip] = total
        share = _remote(chip_buf.at[chip], got_buf, sems, 4, sibling)
        share.start()
        share.wait()
        o_ref[theirs, :] = got_buf[...]

    outs, job_outs = _hosted_call(
        main, name=name, grid=(1,), jobs=jobs, operands=(pack,),
        in_specs=[_full(pack.shape)], out_specs=[_full(pack.shape)],
        out_shape=[jax.ShapeDtypeStruct(pack.shape, pack.dtype)],
        scratch_shapes=[pltpu.VMEM((rh, 128), F32), pltpu.VMEM((N_CHIP, rh, 128), F32), pltpu.VMEM((rh, 128), F32),
                        pltpu.SemaphoreType.DMA((5,)), pltpu.SemaphoreType.DMA((5,))])
    return outs[0], job_outs


SUM_BLOCKS = 2


def _pair_sum(grads, recvs, core, name):
    n_t, nb = len(grads), SUM_BLOCKS
    n_chip = grads[0].shape[0]

    def body(core_ref, *refs):
        for t in range(n_t):
            refs[2 * n_t + t][...] = (refs[t][...] + refs[n_t + t][...]).astype(BF16)

    def blk(g):
        rh = g.shape[1] // 2
        assert rh % (16 * nb) == 0, g.shape
        return (None, rh // nb, g.shape[2])

    mine = [pl.BlockSpec(blk(g), lambda k, i, core_ref: (k, core_ref[0] * nb + i, 0)) for g in grads]
    plain = [pl.BlockSpec(blk(g), lambda k, i, core_ref: (k, i, 0)) for g in grads]
    return pl.pallas_call(
        body, name=name,
        grid_spec=pltpu.PrefetchScalarGridSpec(num_scalar_prefetch=1, grid=(n_chip, nb), in_specs=mine + plain,
                                               out_specs=plain),
        out_shape=[jax.ShapeDtypeStruct((n_chip, g.shape[1] // 2, g.shape[2]), BF16) for g in grads],
        compiler_params=pltpu.CompilerParams(dimension_semantics=("parallel", "parallel"),
                                             vmem_limit_bytes=VMEM_LIMIT_V7X),
    )(core, *grads, *recvs)


def _final_sum(grads, recvs, arriveds, where, prevs, layer, n_layers, name):
    n_t, nb = len(grads), SUM_BLOCKS
    extra = [] if prevs is None else list(prevs)

    def body(where_ref, *refs):
        outs = refs[len(refs) - n_t:]
        for t in range(n_t):
            g_ref, r_ref = refs[t], refs[n_t + t]
            a1, a2, a3 = (refs[(2 + s) * n_t + t] for s in range(3))
            own = g_ref[...] + r_ref[...]
            outs[t][...] = ((own + a1[...].astype(F32)) + a2[...].astype(F32)) + a3[...].astype(F32)

    def blk(g):
        return (None, g.shape[1] // 2 // nb, g.shape[2])

    in_specs = ([pl.BlockSpec(blk(g), lambda i, w: (w[1], w[0] * nb + i, 0)) for g in grads]
                + [pl.BlockSpec(blk(g), lambda i, w: (w[1], i, 0)) for g in grads]
                + [pl.BlockSpec(blk(g), lambda i, w, s=s: (s, i, 0)) for s in range(3) for g in grads]
                + [pl.BlockSpec(memory_space=pl.ANY)] * len(extra))
    return pl.pallas_call(
        body, name=name,
        grid_spec=pltpu.PrefetchScalarGridSpec(
            num_scalar_prefetch=1, grid=(nb,), in_specs=in_specs,
            out_specs=[pl.BlockSpec(blk(g), lambda i, w: (layer, w[0] * nb + i, 0)) for g in grads]),
        out_shape=[jax.ShapeDtypeStruct((n_layers,) + g.shape[1:], F32) for g in grads],
        input_output_aliases={1 + 5 * n_t + t: t for t in range(n_t)} if extra else {},
        compiler_params=pltpu.CompilerParams(dimension_semantics=("parallel",), vmem_limit_bytes=VMEM_LIMIT_V7X),
    )(where, *grads, *recvs, *arriveds, *arriveds, *arriveds, *extra)


def _adamw_math(w, g, m, v):
    m = ADAM_B1 * m + (1.0 - ADAM_B1) * g
    v = ADAM_B2 * v + (1.0 - ADAM_B2) * (g * g)
    m_hat = m / (1.0 - ADAM_B1 ** ADAM_STEP)
    v_hat = v / (1.0 - ADAM_B2 ** ADAM_STEP)
    delta = -ADAM_LR * (m_hat / (jnp.sqrt(v_hat) + ADAM_EPS) + ADAM_WD * w)
    return delta, m, v


ADAMW_BLOCKS = 8


def _adamw_big(ws, gs, ms, vs, name):
    n_t, nb = len(ws), ADAMW_BLOCKS
    n_l = ws[0].shape[0]

    def body(*refs):
        for t in range(n_t):
            g = refs[n_t + t][...]
            res = _adamw_math(refs[t][...], g, refs[2 * n_t + t][...], refs[3 * n_t + t][...]) + (g,)
            for q in range(4):
                refs[(4 + q) * n_t + t][...] = res[q]

    def spec(w):
        assert w.shape[1] % (HALO * nb) == 0, w.shape
        return pl.BlockSpec((None, w.shape[1] // nb, w.shape[2]), lambda l, i: (l, i, 0))

    specs = [spec(w) for w in ws]
    outs = pl.pallas_call(
        body, name=name, grid=(n_l, nb), in_specs=specs * 4, out_specs=specs * 4,
        out_shape=[jax.ShapeDtypeStruct(w.shape, F32) for w in ws] * 4,
        compiler_params=pltpu.CompilerParams(dimension_semantics=("parallel", "parallel"),
                                             vmem_limit_bytes=VMEM_LIMIT_V7X),
    )(*ws, *gs, *ms, *vs)
    return outs[:n_t], outs[n_t:2 * n_t], outs[2 * n_t:3 * n_t], outs[3 * n_t:]


CAST_BLOCKS = 4


def _cast_bf16(ws, name, jobs=()):
    n_t, nb = len(ws), CAST_BLOCKS
    n_l = ws[0].shape[0]

    def main(ins, outs, _):
        for t in range(n_t):
            outs[t][...] = ins[t][...].astype(BF16)

    def spec(w):
        assert w.shape[1] % (2 * HALO * nb) == 0, w.shape
        return pl.BlockSpec((None, w.shape[1] // nb, w.shape[2]), lambda l, i: (l, i, 0))

    specs = [spec(w) for w in ws]
    return _hosted_call(main, name=name, grid=(n_l, nb), jobs=jobs, semantics=("parallel", "parallel"), operands=ws,
                        in_specs=specs, out_specs=specs, out_shape=[jax.ShapeDtypeStruct(w.shape, BF16) for w in ws])


def _adamw_small(ws, gs, ms, vs, name):
    n_p = len(ws)

    def body(*refs):
        ins, outs = refs[:4 * n_p], refs[4 * n_p:]
        for p in range(n_p):
            res = _adamw_math(ins[p][...], ins[n_p + p][...], ins[2 * n_p + p][...], ins[3 * n_p + p][...])
            for q in range(3):
                outs[q * n_p + p][...] = res[q]

    outs = pl.pallas_call(
        body, name=name, in_specs=[VMEM_SPEC] * (4 * n_p), out_specs=[VMEM_SPEC] * (3 * n_p),
        out_shape=[jax.ShapeDtypeStruct(w.shape, F32) for w in ws] * 3,
        compiler_params=pltpu.CompilerParams(vmem_limit_bytes=VMEM_LIMIT_V7X),
    )(*ws, *gs, *ms, *vs)
    return outs[:n_p], outs[n_p:2 * n_p], outs[2 * n_p:]


def kernel(x, mix_norm_g, w_in, conv_a_w, ln_v_g, ln_v_b, w_s, b_s, w_out, ffn_norm_g, w_up, conv_ffn_w, w_down, final_norm_g, loss_target, m_mix_norm_g, m_w_in, m_conv_a_w, m_ln_v_g, m_ln_v_b, m_w_s, m_b_s, m_w_out, m_ffn_norm_g, m_w_up, m_conv_ffn_w, m_w_down, m_final_norm_g, v_mix_norm_g, v_w_in, v_conv_a_w, v_ln_v_g, v_ln_v_b, v_w_s, v_b_s, v_w_out, v_ffn_norm_g, v_w_up, v_conv_ffn_w, v_w_down, v_final_norm_g):
    bsz, seq, d = x.shape
    n = bsz * seq
    n_l, groups = w_s.shape[0], w_s.shape[1]
    assert n_l == 2, "the exchange schedule below is written for two layers"
    gd = d // groups
    ff = w_down.shape[1] * N_CHIP
    mx, my, mc = _mesh_pos()
    chip = 2 * mx + my
    core = jnp.reshape(mc, (1,)).astype(jnp.int32)
    where = jnp.stack([mc, chip]).astype(jnp.int32)

    tm_mm = _row_tile(seq, 512)
    tm_ew = _row_tile(seq, 256)
    tk_w = _row_tile(seq, 2048)
    ff_chunk = ff // 2 if (ff // 2) % 128 == 0 else ff
    IN, OUT, UP, DOWN = range(4)

    first = w_in[0:1].astype(BF16)
    shards, ((w_in_0,),) = _cast_bf16([w_in, w_out, w_up, w_down], "cast_weights", jobs=[_GatherJob([(first, 0)])])
    piece = lambda a, l: (shards[a], l)

    taps = jnp.concatenate([conv_a_w.reshape(n_l, -1), conv_ffn_w.reshape(n_l, -1)], axis=1)
    tap_rows = -(-taps.size // 128 // 8) * 8
    tap_pack = jnp.zeros((tap_rows * 128,), F32).at[:taps.size].set(taps.reshape(-1)).reshape(tap_rows, 128)

    bmaps = jnp.repeat(jnp.swapaxes(b_s, 1, 2), gd, axis=2)

    xs = x.reshape(n, d)
    tgt = loss_target.reshape(n, d)

    (proj0, h1_0), ((w_out_0, w_up_0), (tap_all,)) = _norm_matmul(
        xs, mix_norm_g[0][None], w_in_0, tm=tm_mm, name="fwd_in_proj_0",
        jobs=[_GatherJob([piece(OUT, 0), piece(UP, 0)]), _SmallGatherJob([tap_pack])])
    tap_all = tap_all.reshape(N_CHIP, -1)[:, :taps.size].reshape(N_CHIP, n_l, -1)
    ca = tap_all[:, :, :3 * d // N_CHIP].reshape(N_CHIP, n_l, 3, d // N_CHIP)
    cf = tap_all[:, :, 3 * d // N_CHIP:].reshape(N_CHIP, n_l, 3, 2 * ff // N_CHIP)
    conv_a_full = jnp.transpose(ca, (1, 2, 0, 3)).reshape(n_l, 3, d)
    conv_f_full = jnp.transpose(cf, (1, 2, 0, 3)).reshape(n_l, 3, 2 * ff)

    def mixer_args(l, w_out_l):
        return (conv_a_full[l], ln_v_g[l][None], ln_v_b[l][None], w_s[l], bmaps[l], w_out_l.reshape(d, d))

    (x1_0, cz0), ((w_down_0, w_out_1),) = _mixer_fwd(
        proj0, xs, *mixer_args(0, w_out_0), seq=seq, tm=tm_mm, name="fwd_mixer_0",
        jobs=[_GatherJob([piece(DOWN, 0), piece(OUT, 1)])])
    (up0, h2_0), ((w_in_1,),) = _norm_matmul(
        x1_0, ffn_norm_g[0][None], w_up_0, tm=tm_mm, name="fwd_up_proj_0", jobs=[_GatherJob([piece(IN, 1)])])
    (x2_0, conv0), _ = _ffn_fwd(up0, x1_0, conv_f_full[0], w_down_0.reshape(ff, d), seq=seq, tm=tm_mm, cwid=ff_chunk,
                                name="fwd_ffn_0")
    (proj1, h1_1), ((w_up_1, w_down_1),) = _norm_matmul(
        x2_0, mix_norm_g[1][None], w_in_1, tm=tm_mm, name="fwd_in_proj_1",
        jobs=[_GatherJob([piece(UP, 1), piece(DOWN, 1)])])
    (x1_1, cz1), _ = _mixer_fwd(proj1, x2_0, *mixer_args(1, w_out_1), seq=seq, tm=tm_mm, name="fwd_mixer_1")
    (up1, h2_1), _ = _norm_matmul(x1_1, ffn_norm_g[1][None], w_up_1, tm=tm_mm, name="fwd_up_proj_1")
    (x2_1, conv1), _ = _ffn_fwd(up1, x1_1, conv_f_full[1], w_down_1.reshape(ff, d), seq=seq, tm=tm_mm, cwid=ff_chunk,
                                name="fwd_ffn_1")
    dx, loss_tile, d_final_g = _final_loss(x2_1, final_norm_g[None], tgt, tm=tm_mm, name="final_loss")

    def chipwise(a):
        return a.reshape(N_CHIP, a.shape[0] // N_CHIP, a.shape[1])

    def pair_sums(grads, recvs, tag):
        return _pair_sum(grads, recvs, core, f"grad_pair_sum_{tag}")

    d_up, dwd, d_cf1 = _ffn_bwd(dx, up1, conv1, conv_f_full[1], w_down_1.reshape(ff, d), seq=seq, tm=tm_ew,
                                cwid=ff_chunk, name="bwd_ffn_1")
    (dx1, d_g2_1), _ = _matmul_bwd_x(d_up, w_up_1, x1_1, ffn_norm_g[1][None], dx, tm=tm_mm, name="bwd_up_x_1")
    (dwu,), _ = _matmul_bwd_w(h2_1, d_up, N_CHIP, tk=tk_w, name="bwd_up_w_1")
    (d_proj, dwo, d_ca1, d_lg1, d_lb1, d_ws1, d_bs1), _ = _mixer_bwd(
        dx1, proj1, cz1, *mixer_args(1, w_out_1), seq=seq, tm=tm_ew, name="bwd_mixer_1")
    (dx, d_g1_1), _ = _matmul_bwd_x(d_proj, w_in_1, x2_0, mix_norm_g[1][None], dx1, tm=tm_mm, name="bwd_in_x_1")
    (dwi,), _ = _matmul_bwd_w(h1_1, d_proj, N_CHIP, tk=tk_w, name="bwd_in_w_1")
    grads1 = [dwi, chipwise(dwo), dwu, chipwise(dwd)]
    small1 = [d_g1_1, d_ca1, d_lg1, d_lb1, d_ws1, d_bs1.T, d_g2_1, d_cf1]

    d_up, dwd, d_cf0 = _ffn_bwd(dx, up0, conv0, conv_f_full[0], w_down_0.reshape(ff, d), seq=seq, tm=tm_ew,
                                cwid=ff_chunk, name="bwd_ffn_0")
    (dx1, d_g2_0), (recv1,) = _matmul_bwd_x(d_up, w_up_0, x1_0, ffn_norm_g[0][None], dx, tm=tm_mm, name="bwd_up_x_0",
                                            jobs=[_SwapJob(grads1)])
    parts1 = pair_sums(grads1, recv1, 1)
    (dwu,), ((arr1_in,),) = _matmul_bwd_w(h2_0, d_up, N_CHIP, tk=tk_w, name="bwd_up_w_0",
                                          jobs=[_ScatterJob([parts1[IN]])])
    ud0 = [dwu, chipwise(dwd)]
    (d_proj, dwo, d_ca0, d_lg0, d_lb0, d_ws0, d_bs0), (arr1_rest, recv0_ud) = _mixer_bwd(
        dx1, proj0, cz0, *mixer_args(0, w_out_0), seq=seq, tm=tm_ew, name="bwd_mixer_0",
        jobs=[_ScatterJob(parts1[OUT:]), _SwapJob(ud0)])
    arrived1 = [arr1_in] + list(arr1_rest)
    gsum = _final_sum(grads1, recv1, arrived1, where, None, 1, n_l, "grad_final_sum_1")
    parts0_ud = pair_sums(ud0, recv0_ud, "0ud")
    out0 = chipwise(dwo)
    (dwi,), (gsum, arr0_ud, (recv0_out,)) = _matmul_bwd_w(
        h1_0, d_proj, N_CHIP, tk=tk_w, name="bwd_in_w_0",
        jobs=[_ShareJob([(g, 1) for g in gsum]), _ScatterJob(parts0_ud), _SwapJob([out0])])
    part0_out, = pair_sums([out0], [recv0_out], "0o")
    n_mm = n // tm_mm
    n_head = max(n_mm // 4, 1)
    head, ((arr0_out,), (recv0_in,)) = _matmul_bwd_x(
        d_proj, w_in_0, xs, mix_norm_g[0][None], dx1, tm=tm_mm, name="bwd_in_x_0a", tiles=(0, n_head),
        jobs=[_ScatterJob([part0_out]), _SwapJob([dwi])])
    part0_in, = pair_sums([dwi], [recv0_in], "0i")
    (dx, d_g1_0), ((arr0_in,),) = _matmul_bwd_x(
        d_proj, w_in_0, xs, mix_norm_g[0][None], dx1, tm=tm_mm, name="bwd_in_x_0b", tiles=(n_head, n_mm - n_head),
        before=head, jobs=[_ScatterJob([part0_in])])
    grad_x = dx.reshape(bsz, seq, d)
    grads0 = [dwi, out0] + ud0
    recv0 = [recv0_in, recv0_out] + list(recv0_ud)
    arrived0 = [arr0_in, arr0_out] + list(arr0_ud)
    small0 = [d_g1_0, d_ca0, d_lg0, d_lb0, d_ws0, d_bs0.T, d_g2_0, d_cf0]

    flat = [a.reshape(-1) for a in small0 + small1] + [d_final_g.reshape(-1), loss_tile[0]]
    sizes = [a.size for a in flat]
    total = sum(sizes)
    rows = -(-total // 128 // (2 * HALO)) * 2 * HALO
    pack = jnp.concatenate(flat + [jnp.zeros((rows * 128 - total,), F32)]).reshape(rows, 128)
    red, _ = _all_reduce_small(pack, "small_all_reduce")
    red = red.reshape(-1)
    gsum = _final_sum(grads0, recv0, arrived0, where, gsum, 0, n_l, "grad_final_sum_0")
    (g_in, g_out, g_up, g_down), = _comm_only([_ShareJob([(g, 0) for g in gsum])], "grad_share_0")

    pieces, off = [], 0
    for s in sizes:
        pieces.append(red[off:off + s])
        off += s
    per_layer = [pieces[l * 8:(l + 1) * 8] for l in range(n_l)]

    def stacked(idx, shape):
        return jnp.stack([per_layer[l][idx].reshape(shape) for l in range(n_l)])

    def my_cols(a, width):
        return lax.dynamic_slice_in_dim(a, chip * width, width, axis=-1)

    g_mix_norm = stacked(0, (d,))
    g_conv_a = my_cols(stacked(1, (3, d)), d // N_CHIP)
    g_ln_g = stacked(2, (d,))
    g_ln_b = stacked(3, (d,))
    g_ws = stacked(4, (groups, CHUNK, CHUNK))
    g_bs = stacked(5, (groups, CHUNK))
    g_ffn_norm = stacked(6, (d,))
    g_conv_f = my_cols(stacked(7, (3, 2 * ff)), 2 * ff // N_CHIP)
    g_final = pieces[-2].reshape(1, d)
    loss = pieces[-1][0]

    big_d, big_nm, big_nv, big_g = _adamw_big(
        [w_in, w_out, w_up, w_down], [g_in, g_out, g_up, g_down], [m_w_in, m_w_out, m_w_up, m_w_down],
        [v_w_in, v_w_out, v_w_up, v_w_down], "adamw_big")

    sm_w = [mix_norm_g, conv_a_w, ln_v_g, ln_v_b, w_s, b_s, ffn_norm_g, conv_ffn_w, final_norm_g[None]]
    sm_g = [g_mix_norm, g_conv_a, g_ln_g, g_ln_b, g_ws, g_bs, g_ffn_norm, g_conv_f, g_final]
    sm_m = [m_mix_norm_g, m_conv_a_w, m_ln_v_g, m_ln_v_b, m_w_s, m_b_s, m_ffn_norm_g, m_conv_ffn_w, m_final_norm_g[None]]
    sm_v = [v_mix_norm_g, v_conv_a_w, v_ln_v_g, v_ln_v_b, v_w_s, v_b_s, v_ffn_norm_g, v_conv_ffn_w, v_final_norm_g[None]]
    sm_d, sm_nm, sm_nv = _adamw_small(sm_w, sm_g, sm_m, sm_v, "adamw_small")

    def ordered(sm, bigs):
        return [sm[0], bigs[0], sm[1], sm[2], sm[3], sm[4], sm[5], bigs[1], sm[6], bigs[2], sm[7], bigs[3],
                sm[8].reshape(d)]

    out_g = ordered(sm_g, big_g)
    out_d = ordered(sm_d, big_d)
    out_m = ordered(sm_nm, big_nm)
    out_v = ordered(sm_nv, big_nv)
    return (loss, grad_x, *out_g, *out_d, *out_m, *out_v)
```

```python
import jax
import jax.numpy as jnp
from jax import lax
from jax.experimental import pallas as pl
from jax.experimental.pallas import tpu as pltpu

F32 = jnp.float32
BF16 = jnp.bfloat16
EPS = 1e-6
CHUNK = 128
N_CHIP = 4
HALO = 8
N_PIECES = 7
LATE_STEPS = 2
VMEM_LIMIT_V7X = 56 * 1024 * 1024
MESH_T = pl.DeviceIdType.MESH
HBM_SPEC = pl.BlockSpec(memory_space=pltpu.HBM)
VMEM_SPEC = pl.BlockSpec(memory_space=pltpu.VMEM)

ADAM_LR, ADAM_B1, ADAM_B2, ADAM_EPS, ADAM_WD, ADAM_STEP = 0.001, 0.9, 0.999, 1e-08, 0.01, 10

NT_DIMS = (((1,), (1,)), ((), ()))
TN_DIMS = (((0,), (0,)), ((), ()))


def _resident(block_shape, index_map):
    return pl.BlockSpec(block_shape, index_map, pipeline_mode=pl.Buffered(1))


def _full(shape):
    return pl.BlockSpec(shape, lambda *_: (0,) * len(shape))


def _row_tile(seq, want):
    t = min(seq, want)
    assert seq % t == 0 and t % CHUNK == 0, (seq, want)
    return t


def _shift_prev(prev8, cur, k):
    ext = jnp.concatenate([prev8, cur], axis=0)
    return pltpu.roll(ext, k, 0)[HALO:]


def _shift_next(cur, next8, k):
    ext = jnp.concatenate([cur, next8], axis=0)
    n = ext.shape[0]
    return pltpu.roll(ext, n - k, 0)[:n - HALO]


def _rowsum(a):
    return jnp.sum(a, axis=0, keepdims=True)


def _lanemean(a):
    return jnp.mean(a, axis=-1, keepdims=True)


def _tril_weights(ws_ref, groups):
    r = lax.broadcasted_iota(jnp.int32, (CHUNK, CHUNK), 0)
    c = lax.broadcasted_iota(jnp.int32, (CHUNK, CHUNK), 1)
    tril = r >= c
    return tril, [jnp.where(tril, ws_ref[g], 0.0) for g in range(groups)]


def _halo_prev(tm):
    return lambda i: jnp.maximum(i * (tm // HALO) - 1, 0)


def _halo_next(tm, n):
    last = n // HALO - 1
    return lambda i: jnp.minimum((i + 1) * (tm // HALO), last)


def _mesh_pos():
    return lax.axis_index("x"), lax.axis_index("y"), lax.axis_index("c")


def _peer_chip(x, y, r):
    return (1 - x if r >> 1 else x), (1 - y if r & 1 else y)


def _remote(src, dst, sems, k, to):
    return pltpu.make_async_remote_copy(src_ref=src, dst_ref=dst, send_sem=sems[0].at[k], recv_sem=sems[1].at[k],
                                        device_id=to, device_id_type=MESH_T)


class _GatherJob:
    def __init__(self, pieces):
        self.pieces = pieces
        self.ins = [p[0] for p in pieces]
        self.out_shapes = [jax.ShapeDtypeStruct((N_CHIP,) + p[0].shape[1:], p[0].dtype) for p in pieces]
        self.aliases = {}
        n = len(pieces)
        self.sems = [pltpu.SemaphoreType.DMA((3 * n,))] * 4 + [pltpu.SemaphoreType.DMA((n,))]

    def _half(self, outs, a, of_chip, core):
        rh = outs[a].shape[1] // 2
        return outs[a].at[of_chip, pl.ds(core * rh, rh), :]

    def _own(self, ins, outs, sems, chip):
        return [pltpu.make_async_copy(ins[a].at[layer], outs[a].at[chip], sems[4].at[a])
                for a, (_, layer) in enumerate(self.pieces)]

    def _sends(self, ins, outs, sems, x, y, c):
        chip, out = 2 * x + y, []
        for a, (_, layer) in enumerate(self.pieces):
            rh = outs[a].shape[1] // 2
            for r in (1, 2, 3):
                px, py = _peer_chip(x, y, r)
                out.append(_remote(ins[a].at[layer, pl.ds(c * rh, rh), :], self._half(outs, a, chip, c), sems[0:2],
                                   3 * a + r - 1, (px, py, c)))
        return out

    def _passes(self, outs, sems, x, y, c, core):
        out = []
        for a in range(len(self.pieces)):
            for r in (1, 2, 3):
                px, py = _peer_chip(x, y, r)
                landed = self._half(outs, a, 2 * px + py, core)
                out.append(_remote(landed, landed, sems[2:4], 3 * a + r - 1, (x, y, 1 - c)))
        return out

    def start(self, ins, outs, sems):
        x, y, c = _mesh_pos()
        for cp in self._own(ins, outs, sems, 2 * x + y) + self._sends(ins, outs, sems, x, y, c):
            cp.start()

    def late(self, ins, outs, sems):
        x, y, c = _mesh_pos()
        passes = self._passes(outs, sems, x, y, c, c)
        k = 0
        for a in range(len(self.pieces)):
            for r in (1, 2, 3):
                px, py = _peer_chip(x, y, r)
                landed = self._half(outs, a, 2 * px + py, c)
                _remote(landed, landed, sems[0:2], k, (px, py, c)).wait_recv()
                passes[k].start()
                k += 1

    def finish(self, ins, outs, sems):
        x, y, c = _mesh_pos()
        for cp in self._passes(outs, sems, x, y, c, 1 - c):
            cp.wait_recv()
        for cp in self._sends(ins, outs, sems, x, y, c) + self._passes(outs, sems, x, y, c, c):
            cp.wait_send()
        for cp in self._own(ins, outs, sems, 2 * x + y):
            cp.wait()


class _SmallGatherJob:
    def __init__(self, pieces):
        self.ins = list(pieces)
        self.out_shapes = [jax.ShapeDtypeStruct((N_CHIP,) + p.shape, p.dtype) for p in pieces]
        self.aliases = {}
        n = len(pieces)
        self.sems = [pltpu.SemaphoreType.DMA((3 * n,))] * 2 + [pltpu.SemaphoreType.DMA((n,))]

    def _outgoing(self, ins, outs, sems):
        x, y, c = _mesh_pos()
        mine = [outs[a].at[2 * x + y] for a in range(len(ins))]
        own = [pltpu.make_async_copy(ins[a], mine[a], sems[2].at[a]) for a in range(len(ins))]
        sends = [_remote(ins[a], mine[a], sems[0:2], 3 * a + r - 1, (*_peer_chip(x, y, r), c))
                 for a in range(len(ins)) for r in (1, 2, 3)]
        return own, sends

    def start(self, ins, outs, sems):
        own, sends = self._outgoing(ins, outs, sems)
        for cp in own + sends:
            cp.start()

    def finish(self, ins, outs, sems):
        x, y, c = _mesh_pos()
        for a in range(len(ins)):
            for r in (1, 2, 3):
                px, py = _peer_chip(x, y, r)
                landed = outs[a].at[2 * px + py]
                _remote(landed, landed, sems[0:2], 3 * a + r - 1, (px, py, c)).wait_recv()
        own, sends = self._outgoing(ins, outs, sems)
        for cp in sends:
            cp.wait_send()
        for cp in own:
            cp.wait()


class _SwapJob:
    def __init__(self, pieces):
        self.ins = list(pieces)
        self.out_shapes = [jax.ShapeDtypeStruct((g.shape[0], g.shape[1] // 2, g.shape[2]), g.dtype) for g in pieces]
        self.aliases = {}
        self.sems = [pltpu.SemaphoreType.DMA((len(pieces),))] * 2

    def _copies(self, ins, outs, sems):
        x, y, c = _mesh_pos()
        out = []
        for a in range(len(ins)):
            rh = ins[a].shape[1] // 2
            out.append(_remote(ins[a].at[:, pl.ds((1 - c) * rh, rh), :], outs[a], sems, a, (x, y, 1 - c)))
        return out

    def start(self, ins, outs, sems):
        for cp in self._copies(ins, outs, sems):
            cp.start()

    def finish(self, ins, outs, sems):
        for cp in self._copies(ins, outs, sems):
            cp.wait()


class _ScatterJob:
    def __init__(self, pieces):
        self.ins = list(pieces)
        self.out_shapes = [jax.ShapeDtypeStruct((3,) + p.shape[1:], p.dtype) for p in pieces]
        self.aliases = {}
        self.sems = [pltpu.SemaphoreType.DMA((3 * len(pieces),))] * 2

    def _copies(self, ins, outs, sems):
        x, y, c = _mesh_pos()
        out = []
        for a in range(len(ins)):
            for r in (1, 2, 3):
                px, py = _peer_chip(x, y, r)
                out.append(_remote(ins[a].at[2 * px + py], outs[a].at[r - 1], sems, 3 * a + r - 1, (px, py, c)))
        return out

    def start(self, ins, outs, sems):
        for cp in self._copies(ins, outs, sems):
            cp.start()

    def finish(self, ins, outs, sems):
        for cp in self._copies(ins, outs, sems):
            cp.wait()


class _ShareJob:
    def __init__(self, pieces):
        self.pieces = pieces
        self.ins = [p[0] for p in pieces]
        self.out_shapes = [jax.ShapeDtypeStruct(p[0].shape, p[0].dtype) for p in pieces]
        self.aliases = {a: a for a in range(len(pieces))}
        self.sems = [pltpu.SemaphoreType.DMA((len(pieces),))] * 2

    def _copies(self, outs, sems, core):
        x, y, c = _mesh_pos()
        out = []
        for a, (_, layer) in enumerate(self.pieces):
            rh = outs[a].shape[1] // 2
            rows = outs[a].at[layer, pl.ds(core * rh, rh), :]
            out.append(_remote(rows, rows, sems, a, (x, y, 1 - c)))
        return out

    def start(self, ins, outs, sems):
        for cp in self._copies(outs, sems, lax.axis_index("c")):
            cp.start()

    def finish(self, ins, outs, sems):
        c = lax.axis_index("c")
        for cp in self._copies(outs, sems, c):
            cp.wait_send()
        for cp in self._copies(outs, sems, 1 - c):
            cp.wait_recv()


def _hosted_call(main, *, name, grid, in_specs, out_specs, out_shape, scratch_shapes=(), jobs=(), semantics=None,
                 operands=(), main_aliases=None):
    n_in, n_out, n_sc = len(in_specs), len(out_specs), len(scratch_shapes)
    counts = [(len(j.ins), len(j.out_shapes), len(j.sems)) for j in jobs]
    j_in, j_out, j_sc = (sum(c[k] for c in counts) for k in range(3))
    aliases, i0, o0 = dict(main_aliases or {}), n_in, n_out
    for j, (ci, co, _) in zip(jobs, counts):
        aliases.update({i0 + a: o0 + b for a, b in j.aliases.items()})
        i0, o0 = i0 + ci, o0 + co

    def body(*refs):
        cuts = [0, n_in, n_in + j_in, n_in + j_in + n_out, n_in + j_in + n_out + j_out,
                n_in + j_in + n_out + j_out + n_sc, len(refs)]
        m_in, jb_in, m_out, jb_out, m_sc, jb_sc = (list(refs[cuts[k]:cuts[k + 1]]) for k in range(6))

        def run(phase):
            i0 = o0 = s0 = 0
            for j, (ci, co, cs) in zip(jobs, counts):
                if hasattr(j, phase):
                    getattr(j, phase)(jb_in[i0:i0 + ci], jb_out[o0:o0 + co], jb_sc[s0:s0 + cs])
                i0, o0, s0 = i0 + ci, o0 + co, s0 + cs

        step, n_steps = 0, 1
        for ax in range(len(grid)):
            step, n_steps = step * grid[ax] + pl.program_id(ax), n_steps * grid[ax]
        if jobs:
            pl.when(step == 0)(lambda: run("start"))
        main(m_in, m_out, m_sc)
        if jobs:
            pl.when(step == max(n_steps - 1 - LATE_STEPS, 0))(lambda: run("late"))
            pl.when(step == n_steps - 1)(lambda: run("finish"))

    if semantics is None or jobs:
        semantics = ("arbitrary",) * len(grid)
    outs = pl.pallas_call(
        body, name=name, grid=grid,
        in_specs=list(in_specs) + [HBM_SPEC] * j_in,
        out_specs=list(out_specs) + [HBM_SPEC] * j_out,
        out_shape=list(out_shape) + [s for j in jobs for s in j.out_shapes],
        scratch_shapes=list(scratch_shapes) + [s for j in jobs for s in j.sems],
        input_output_aliases=aliases,
        compiler_params=pltpu.CompilerParams(dimension_semantics=semantics, vmem_limit_bytes=VMEM_LIMIT_V7X),
    )(*operands, *[a for j in jobs for a in j.ins])
    main_outs, rest, job_outs = list(outs[:n_out]), list(outs[n_out:]), []
    for _, co, _ in counts:
        job_outs.append(rest[:co])
        rest = rest[co:]
    return main_outs, job_outs


def _comm_only(jobs, name):
    return _hosted_call(lambda i, o, s: None, name=name, grid=(1,), in_specs=[], out_specs=[], out_shape=[],
                        jobs=jobs)[1]


def _norm_matmul(x, g, wg, *, tm, name, jobs=()):
    n, d = x.shape
    n_chip, _, ck = wg.shape

    def main(ins, outs, _):
        x_ref, g_ref, w_ref = ins
        o_ref, ht_ref = outs
        xf = x_ref[...]
        r = lax.rsqrt(_lanemean(xf * xf) + EPS)
        h = xf * r * g_ref[...]
        ht_ref[...] = h.T.astype(BF16)
        hb = h.astype(BF16)
        for j in range(n_chip):
            o_ref[:, j * ck:(j + 1) * ck] = jnp.dot(hb, w_ref[j], preferred_element_type=F32).astype(BF16)

    return _hosted_call(
        main, name=name, grid=(n // tm,), jobs=jobs, semantics=("parallel",), operands=(x, g, wg),
        in_specs=[pl.BlockSpec((tm, d), lambda i: (i, 0)), _full((1, d)), _resident((n_chip, d, ck), lambda i: (0, 0, 0))],
        out_specs=[pl.BlockSpec((tm, n_chip * ck), lambda i: (i, 0)), pl.BlockSpec((d, tm), lambda i: (0, i))],
        out_shape=[jax.ShapeDtypeStruct((n, n_chip * ck), BF16), jax.ShapeDtypeStruct((d, n), BF16)])


def _mixer_fwd(proj, x, cw, lg, lb, ws, bmap, wout, *, seq, tm, name, jobs=()):
    n, d = x.shape
    groups = ws.shape[0]
    gd = d // groups
    prev = _halo_prev(tm)

    def main(ins, outs, scratch):
        proj_ref, pcg_ref, pxi_ref, x_ref, cw_ref, lg_ref, lb_ref, ws_ref, bmap_ref, wout_ref = ins
        o_ref, cz_ref = outs
        vn_s, mixed_s = scratch
        seq_start = (pl.program_id(0) * tm) % seq == 0

        def piece(k):
            return proj_ref[:, k * d:(k + 1) * d].astype(F32)

        z = piece(1) * piece(2)
        zprev = jnp.where(seq_start, 0.0, pcg_ref[...].astype(F32) * pxi_ref[...].astype(F32))
        cz = (cw_ref[0:1, :] * _shift_prev(zprev, z, 2) + cw_ref[1:2, :] * _shift_prev(zprev, z, 1)
              + cw_ref[2:3, :] * z)
        cz_ref[...] = cz.astype(BF16)
        ya = piece(0) * cz

        v = piece(4)
        xc = v - _lanemean(v)
        vn = xc * lax.rsqrt(_lanemean(xc * xc) + EPS) * lg_ref[...] + lb_ref[...]
        vn_s[...] = vn.astype(BF16)
        _, wsm = _tril_weights(ws_ref, groups)
        for ck in range(tm // CHUNK):
            rows = slice(ck * CHUNK, (ck + 1) * CHUNK)
            for g in range(groups):
                cols = slice(g * gd, (g + 1) * gd)
                mixed_s[rows, cols] = (jnp.dot(wsm[g].astype(BF16), vn_s[rows, cols], preferred_element_type=F32)
                                       + bmap_ref[:, cols])
        yb = piece(3) * mixed_s[...]
        merged = jax.nn.sigmoid(piece(5)) * ya + jax.nn.sigmoid(piece(6)) * yb
        o_ref[...] = x_ref[...] + jnp.dot(merged.astype(BF16), wout_ref[...], preferred_element_type=F32)

    return _hosted_call(
        main, name=name, grid=(n // tm,), jobs=jobs, semantics=("parallel",),
        operands=(proj, proj, proj, x, cw, lg, lb, ws, bmap, wout),
        in_specs=[pl.BlockSpec((tm, N_PIECES * d), lambda i: (i, 0)),
                  pl.BlockSpec((HALO, d), lambda i: (prev(i), 1)),
                  pl.BlockSpec((HALO, d), lambda i: (prev(i), 2)),
                  pl.BlockSpec((tm, d), lambda i: (i, 0)),
                  _full((3, d)), _full((1, d)), _full((1, d)), _full((groups, CHUNK, CHUNK)), _full((CHUNK, d)),
                  _resident((d, d), lambda i: (0, 0))],
        out_specs=[pl.BlockSpec((tm, d), lambda i: (i, 0)), pl.BlockSpec((tm, d), lambda i: (i, 0))],
        out_shape=[jax.ShapeDtypeStruct((n, d), F32), jax.ShapeDtypeStruct((n, d), BF16)],
        scratch_shapes=[pltpu.VMEM((tm, d), BF16), pltpu.VMEM((tm, d), F32)])


def _ffn_fwd(up, x, cw, wd, *, seq, tm, cwid, name, jobs=()):
    n, d = x.shape
    ff = wd.shape[0]
    prev = _halo_prev(tm)

    def main(ins, outs, _):
        up_ref, pup_ref, x_ref, cw_ref, wd_ref = ins
        o_ref, conv_ref = outs
        seq_start = (pl.program_id(0) * tm) % seq == 0

        def conv(lo):
            cols = slice(lo, lo + cwid)
            cur = up_ref[:, cols].astype(F32)
            pre = jnp.where(seq_start, 0.0, pup_ref[:, cols].astype(F32))
            out = (cw_ref[0:1, cols] * _shift_prev(pre, cur, 2) + cw_ref[1:2, cols] * _shift_prev(pre, cur, 1)
                   + cw_ref[2:3, cols] * cur)
            conv_ref[:, cols] = out.astype(BF16)
            return out

        acc = x_ref[...]
        for cj in range(ff // cwid):
            gate = conv(cj * cwid)
            val = conv(ff + cj * cwid)
            a = gate * jax.nn.sigmoid(gate) * val
            acc = acc + jnp.dot(a.astype(BF16), wd_ref[cj * cwid:(cj + 1) * cwid, :], preferred_element_type=F32)
        o_ref[...] = acc

    return _hosted_call(
        main, name=name, grid=(n // tm,), jobs=jobs, semantics=("parallel",), operands=(up, up, x, cw, wd),
        in_specs=[pl.BlockSpec((tm, 2 * ff), lambda i: (i, 0)),
                  pl.BlockSpec((HALO, 2 * ff), lambda i: (prev(i), 0)),
                  pl.BlockSpec((tm, d), lambda i: (i, 0)),
                  _full((3, 2 * ff)),
                  _resident((ff, d), lambda i: (0, 0))],
        out_specs=[pl.BlockSpec((tm, d), lambda i: (i, 0)), pl.BlockSpec((tm, 2 * ff), lambda i: (i, 0))],
        out_shape=[jax.ShapeDtypeStruct((n, d), F32), jax.ShapeDtypeStruct((n, 2 * ff), BF16)])


def _final_loss(x, g, target, *, tm, name):
    n, d = x.shape

    def body(x_ref, g_ref, t_ref, dx_ref, loss_ref, dg_ref):
        @pl.when(pl.program_id(0) == 0)
        def _():
            loss_ref[...] = jnp.zeros_like(loss_ref)
            dg_ref[...] = jnp.zeros_like(dg_ref)

        xf = x_ref[...]
        r = lax.rsqrt(_lanemean(xf * xf) + EPS)
        xhat = xf * r
        diff = xhat * g_ref[...] - t_ref[...]
        loss_ref[...] += (0.5 / d) * _rowsum(jnp.sum(diff * diff, axis=-1, keepdims=True))
        dy = diff * (1.0 / d)
        dg_ref[...] += _rowsum(dy * xhat)
        dyh = dy * g_ref[...]
        dx_ref[...] = r * (dyh - xhat * _lanemean(dyh * xhat))

    return pl.pallas_call(
        body, name=name, grid=(n // tm,),
        in_specs=[pl.BlockSpec((tm, d), lambda i: (i, 0)), _full((1, d)), pl.BlockSpec((tm, d), lambda i: (i, 0))],
        out_specs=[pl.BlockSpec((tm, d), lambda i: (i, 0)), _full((HALO, CHUNK)), _full((1, d))],
        out_shape=[jax.ShapeDtypeStruct((n, d), F32), jax.ShapeDtypeStruct((HALO, CHUNK), F32),
                   jax.ShapeDtypeStruct((1, d), F32)],
        compiler_params=pltpu.CompilerParams(dimension_semantics=("arbitrary",), vmem_limit_bytes=VMEM_LIMIT_V7X),
    )(x, g, target)


def _ffn_bwd(dx, up, conv, cw, wd, *, seq, tm, cwid, name):
    n, d = dx.shape
    ff = wd.shape[0]
    nxt = _halo_next(tm, n)

    def body(dx_ref, dxn_ref, up_ref, conv_ref, nconv_ref, cw_ref, wd_ref, dup_ref, dwd_ref, dcw_ref):
        i = pl.program_id(0)
        keep_next = jnp.where(((i + 1) * tm) % seq == 0, 0.0, 1.0)

        @pl.when(i == 0)
        def _():
            dwd_ref[...] = jnp.zeros_like(dwd_ref)
            dcw_ref[...] = jnp.zeros_like(dcw_ref)

        dxe = jnp.concatenate([dx_ref[...], dxn_ref[...]], axis=0).astype(BF16)
        dxb = dxe[:tm]
        for cj in range(ff // cwid):
            rows = slice(cj * cwid, (cj + 1) * cwid)
            g_cols, v_cols = slice(cj * cwid, (cj + 1) * cwid), slice(ff + cj * cwid, ff + (cj + 1) * cwid)
            dae = lax.dot_general(dxe, wd_ref[rows, :], NT_DIMS, preferred_element_type=F32)
            da, dan = dae[:tm], dae[tm:]

            def grads(gate, val, da_rows):
                sg = jax.nn.sigmoid(gate)
                sl = gate * sg
                return sl, da_rows * val * sg * (1.0 + gate * (1.0 - sg)), da_rows * sl

            gate, val = conv_ref[:, g_cols].astype(F32), conv_ref[:, v_cols].astype(F32)
            sl, d_gate, d_val = grads(gate, val, da)
            dwd_ref[rows, :] += lax.dot_general((sl * val).astype(BF16), dxb, TN_DIMS, preferred_element_type=F32)
            _, d_gate_n, d_val_n = grads(nconv_ref[:, g_cols].astype(F32), nconv_ref[:, v_cols].astype(F32),
                                         dan * keep_next)
            for cols, dcur, dnext in ((g_cols, d_gate, d_gate_n), (v_cols, d_val, d_val_n)):
                upc = up_ref[:, cols].astype(F32)
                d1, d2 = _shift_next(dcur, dnext, 1), _shift_next(dcur, dnext, 2)
                for k, dk in enumerate((d2, d1, dcur)):
                    dcw_ref[k:k + 1, cols] += _rowsum(dk * upc)
                dup = cw_ref[2:3, cols] * dcur + cw_ref[1:2, cols] * d1 + cw_ref[0:1, cols] * d2
                dup_ref[:, cols] = dup.astype(BF16)

    return pl.pallas_call(
        body, name=name, grid=(n // tm,),
        in_specs=[pl.BlockSpec((tm, d), lambda i: (i, 0)),
                  pl.BlockSpec((HALO, d), lambda i: (nxt(i), 0)),
                  pl.BlockSpec((tm, 2 * ff), lambda i: (i, 0)),
                  pl.BlockSpec((tm, 2 * ff), lambda i: (i, 0)),
                  pl.BlockSpec((HALO, 2 * ff), lambda i: (nxt(i), 0)),
                  _full((3, 2 * ff)),
                  _resident((ff, d), lambda i: (0, 0))],
        out_specs=[pl.BlockSpec((tm, 2 * ff), lambda i: (i, 0)), _full((ff, d)), _full((3, 2 * ff))],
        out_shape=[jax.ShapeDtypeStruct((n, 2 * ff), BF16), jax.ShapeDtypeStruct((ff, d), F32),
                   jax.ShapeDtypeStruct((3, 2 * ff), F32)],
        compiler_params=pltpu.CompilerParams(dimension_semantics=("arbitrary",), vmem_limit_bytes=VMEM_LIMIT_V7X),
    )(dx, dx, up, conv, conv, cw, wd)


def _matmul_bwd_x(dy, wg, x, g, dres, *, tm, name, jobs=(), tiles=None, before=None):
    n, d = x.shape
    n_chip, _, ck = wg.shape
    first_tile, n_tiles = tiles or (0, n // tm)
    extra = [] if before is None else list(before)

    def main(ins, outs, _):
        dy_ref, w_ref, x_ref, g_ref, dres_ref = ins[:5]
        dx_ref, dg_ref = outs

        @pl.when(pl.program_id(0) == 0)
        def _():
            dg_ref[...] = jnp.zeros_like(dg_ref) if before is None else ins[6][...]

        dh = lax.dot_general(dy_ref[:, 0:ck], w_ref[0], NT_DIMS, preferred_element_type=F32)
        for j in range(1, n_chip):
            dh = dh + lax.dot_general(dy_ref[:, j * ck:(j + 1) * ck], w_ref[j], NT_DIMS, preferred_element_type=F32)
        xf = x_ref[...]
        r = lax.rsqrt(_lanemean(xf * xf) + EPS)
        xhat = xf * r
        dg_ref[...] += _rowsum(dh * xhat)
        dyh = dh * g_ref[...]
        dx_ref[...] = dres_ref[...] + r * (dyh - xhat * _lanemean(dyh * xhat))

    rows = lambda width: pl.BlockSpec((tm, width), lambda i: (i + first_tile, 0))
    return _hosted_call(
        main, name=name, grid=(n_tiles,), jobs=jobs, semantics=("arbitrary",), operands=(dy, wg, x, g, dres, *extra),
        in_specs=[rows(n_chip * ck), _resident((n_chip, d, ck), lambda i: (0, 0, 0)), rows(d), _full((1, d)), rows(d)]
        + ([] if before is None else [pl.BlockSpec(memory_space=pl.ANY), _full((1, d))]),
        out_specs=[rows(d), _full((1, d))],
        out_shape=[jax.ShapeDtypeStruct((n, d), F32), jax.ShapeDtypeStruct((1, d), F32)],
        main_aliases=None if before is None else {5: 0})


def _matmul_bwd_w(ht, dy, n_chip, *, tk, name, jobs=()):
    d, n = ht.shape
    ck = dy.shape[1] // n_chip

    def main(ins, outs, _):
        ht_ref, dy_ref = ins
        o_ref, = outs

        @pl.when(pl.program_id(1) == 0)
        def _():
            o_ref[...] = jnp.zeros_like(o_ref)

        o_ref[...] += jnp.dot(ht_ref[...], dy_ref[...], preferred_element_type=F32)

    return _hosted_call(
        main, name=name, grid=(n_chip, n // tk), jobs=jobs, semantics=("parallel", "arbitrary"), operands=(ht, dy),
        in_specs=[pl.BlockSpec((d, tk), lambda j, k: (0, k)), pl.BlockSpec((tk, ck), lambda j, k: (k, j))],
        out_specs=[pl.BlockSpec((None, d, ck), lambda j, k: (j, 0, 0))],
        out_shape=[jax.ShapeDtypeStruct((n_chip, d, ck), F32)])


def _mixer_bwd(dx, proj, cz, cw, lg, lb, ws, bmap, wout, *, seq, tm, name, jobs=()):
    n, d = dx.shape
    groups = ws.shape[0]
    gd = d // groups
    nxt = _halo_next(tm, n)
    n_tiles = n // tm

    def main(ins, outs, scratch):
        proj_ref, cz_ref, nbg_ref, nga_ref, dx_ref, dxn_ref, cw_ref, lg_ref, lb_ref, ws_ref, bmap_ref, wout_ref = ins
        dproj_ref, dwout_ref, dcw_ref, dlg_ref, dlb_ref, dws_ref, dbs_ref = outs
        vn_s, mixed_s, dmix_s, dvn_s, dbmap_s = scratch
        i = pl.program_id(0)
        keep_next = jnp.where(((i + 1) * tm) % seq == 0, 0.0, 1.0)

        @pl.when(i == 0)
        def _():
            for ref in (dwout_ref, dcw_ref, dlg_ref, dlb_ref, dws_ref, dbmap_s):
                ref[...] = jnp.zeros_like(ref)

        def piece(k):
            return proj_ref[:, k * d:(k + 1) * d].astype(F32)

        def put(k, val):
            dproj_ref[:, k * d:(k + 1) * d] = val.astype(BF16)

        w = [cw_ref[k:k + 1, :] for k in range(3)]
        cg, xi = piece(1), piece(2)
        cz = cz_ref[...].astype(F32)
        bg = piece(0)
        ya = bg * cz

        v = piece(4)
        xc = v - _lanemean(v)
        rstd = lax.rsqrt(_lanemean(xc * xc) + EPS)
        vhat = xc * rstd
        vn_s[...] = (vhat * lg_ref[...] + lb_ref[...]).astype(BF16)
        tril, wsm = _tril_weights(ws_ref, groups)
        for ck in range(tm // CHUNK):
            rows = slice(ck * CHUNK, (ck + 1) * CHUNK)
            for g in range(groups):
                cols = slice(g * gd, (g + 1) * gd)
                mixed_s[rows, cols] = (jnp.dot(wsm[g].astype(BF16), vn_s[rows, cols], preferred_element_type=F32)
                                       + bmap_ref[:, cols])
        u = piece(3)
        mixed = mixed_s[...]
        yb = u * mixed
        sa, sb = jax.nn.sigmoid(piece(5)), jax.nn.sigmoid(piece(6))
        merged = sa * ya + sb * yb

        dxe = jnp.concatenate([dx_ref[...], dxn_ref[...]], axis=0).astype(BF16)
        dme = lax.dot_general(dxe, wout_ref[...], NT_DIMS, preferred_element_type=F32)
        dm, dm_n = dme[:tm], dme[tm:]
        dwout_ref[...] += lax.dot_general(merged.astype(BF16), dxe[:tm], TN_DIMS, preferred_element_type=F32)

        put(5, dm * ya * sa * (1.0 - sa))
        put(6, dm * yb * sb * (1.0 - sb))
        d_ya, d_yb = dm * sa, dm * sb
        put(0, d_ya * cz)
        d_cz = d_ya * bg
        d_cz_n = dm_n * jax.nn.sigmoid(nga_ref[...].astype(F32)) * nbg_ref[...].astype(F32) * keep_next
        d_cz1, d_cz2 = _shift_next(d_cz, d_cz_n, 1), _shift_next(d_cz, d_cz_n, 2)
        z = cg * xi
        for k, dk in enumerate((d_cz2, d_cz1, d_cz)):
            dcw_ref[k:k + 1, :] += _rowsum(dk * z)
        dz = w[2] * d_cz + w[1] * d_cz1 + w[0] * d_cz2
        put(1, dz * xi)
        put(2, dz * cg)

        put(3, d_yb * mixed)
        d_mixed = d_yb * u
        dmix_s[...] = d_mixed.astype(BF16)
        for ck in range(tm // CHUNK):
            rows = slice(ck * CHUNK, (ck + 1) * CHUNK)
            dbmap_s[...] += d_mixed[rows, :]
            for g in range(groups):
                cols = slice(g * gd, (g + 1) * gd)
                dvn_s[rows, cols] = jnp.dot(wsm[g].T.astype(BF16), dmix_s[rows, cols], preferred_element_type=F32)
                dws_ref[g] += jnp.where(
                    tril, lax.dot_general(dmix_s[rows, cols], vn_s[rows, cols], NT_DIMS, preferred_element_type=F32),
                    0.0)
        d_vn = dvn_s[...]
        dlg_ref[...] += _rowsum(d_vn * vhat)
        dlb_ref[...] += _rowsum(d_vn)
        d_vhat = d_vn * lg_ref[...]
        put(4, rstd * (d_vhat - _lanemean(d_vhat) - vhat * _lanemean(d_vhat * vhat)))

        @pl.when(i == n_tiles - 1)
        def _():
            for g in range(groups):
                dbs_ref[:, g:g + 1] = jnp.sum(dbmap_s[:, g * gd:(g + 1) * gd], axis=-1, keepdims=True)

    return _hosted_call(
        main, name=name, grid=(n_tiles,), jobs=jobs, semantics=("arbitrary",),
        operands=(proj, cz, proj, proj, dx, dx, cw, lg, lb, ws, bmap, wout),
        in_specs=[pl.BlockSpec((tm, N_PIECES * d), lambda i: (i, 0)),
                  pl.BlockSpec((tm, d), lambda i: (i, 0)),
                  pl.BlockSpec((HALO, d), lambda i: (nxt(i), 0)),
                  pl.BlockSpec((HALO, d), lambda i: (nxt(i), 5)),
                  pl.BlockSpec((tm, d), lambda i: (i, 0)),
                  pl.BlockSpec((HALO, d), lambda i: (nxt(i), 0)),
                  _full((3, d)), _full((1, d)), _full((1, d)), _full((groups, CHUNK, CHUNK)), _full((CHUNK, d)),
                  _resident((d, d), lambda i: (0, 0))],
        out_specs=[pl.BlockSpec((tm, N_PIECES * d), lambda i: (i, 0)), _full((d, d)),
                   _full((3, d)), _full((1, d)), _full((1, d)), _full((groups, CHUNK, CHUNK)), _full((CHUNK, groups))],
        out_shape=[jax.ShapeDtypeStruct((n, N_PIECES * d), BF16), jax.ShapeDtypeStruct((d, d), F32),
                   jax.ShapeDtypeStruct((3, d), F32), jax.ShapeDtypeStruct((1, d), F32),
                   jax.ShapeDtypeStruct((1, d), F32), jax.ShapeDtypeStruct((groups, CHUNK, CHUNK), F32),
                   jax.ShapeDtypeStruct((CHUNK, groups), F32)],
        scratch_shapes=[pltpu.VMEM((tm, d), BF16), pltpu.VMEM((tm, d), F32), pltpu.VMEM((tm, d), BF16),
                        pltpu.VMEM((tm, d), F32), pltpu.VMEM((CHUNK, d), F32)])


def _all_reduce_small(pack, name, jobs=()):
    rows = pack.shape[0]
    rh = rows // 2
    assert rh % HALO == 0, rows

    def main(ins, outs, scratch):
        p_ref, = ins
        o_ref, = outs
        sib_buf, chip_buf, got_buf, send_sem, recv_sem = scratch
        sems = (send_sem, recv_sem)
        x, y, c = _mesh_pos()
        chip, sibling = 2 * x + y, (x, y, 1 - c)
        mine = pl.ds(pl.multiple_of(c * rh, HALO), rh)
        theirs = pl.ds(pl.multiple_of((1 - c) * rh, HALO), rh)
        swap = _remote(p_ref.at[theirs], sib_buf, sems, 0, sibling)
        swap.start()
        swap.wait()
        chip_buf[chip] = p_ref[mine, :] + sib_buf[...]
        copies = []
        for r in (1, 2, 3):
            px, py = _peer_chip(x, y, r)
            copies.append(_remote(chip_buf.at[chip], chip_buf.at[chip], sems, r, (px, py, c)))
        for cp in copies:
            cp.start()
        for r, cp in zip((1, 2, 3), copies):
            px, py = _peer_chip(x, y, r)
            landed = chip_buf.at[2 * px + py]
            _remote(landed, landed, sems, r, (px, py, c)).wait_recv()
            cp.wait_send()
        total = ((chip_buf[0] + chip_buf[1]) + chip_buf[2]) + chip_buf[3]
        o_ref[mine, :] = total
        chip_buf[chip] = total
        share = _remote(chip_buf.at[chip], got_buf, sems, 4, sibling)
        share.start()
        share.wait()
        o_ref[theirs, :] = got_buf[...]

    outs, job_outs = _hosted_call(
        main, name=name, grid=(1,), jobs=jobs, operands=(pack,),
        in_specs=[_full(pack.shape)], out_specs=[_full(pack.shape)],
        out_shape=[jax.ShapeDtypeStruct(pack.shape, pack.dtype)],
        scratch_shapes=[pltpu.VMEM((rh, 128), F32), pltpu.VMEM((N_CHIP, rh, 128), F32), pltpu.VMEM((rh, 128), F32),
                        pltpu.SemaphoreType.DMA((5,)), pltpu.SemaphoreType.DMA((5,))])
    return outs[0], job_outs


SUM_BLOCKS = 2


def _pair_sum(grads, recvs, core, name):
    n_t, nb = len(grads), SUM_BLOCKS
    n_chip = grads[0].shape[0]

    def body(core_ref, *refs):
        for t in range(n_t):
            refs[2 * n_t + t][...] = (refs[t][...] + refs[n_t + t][...]).astype(BF16)

    def blk(g):
        rh = g.shape[1] // 2
        assert rh % (16 * nb) == 0, g.shape
        return (None, rh // nb, g.shape[2])

    mine = [pl.BlockSpec(blk(g), lambda k, i, core_ref: (k, core_ref[0] * nb + i, 0)) for g in grads]
    plain = [pl.BlockSpec(blk(g), lambda k, i, core_ref: (k, i, 0)) for g in grads]
    return pl.pallas_call(
        body, name=name,
        grid_spec=pltpu.PrefetchScalarGridSpec(num_scalar_prefetch=1, grid=(n_chip, nb), in_specs=mine + plain,
                                               out_specs=plain),
        out_shape=[jax.ShapeDtypeStruct((n_chip, g.shape[1] // 2, g.shape[2]), BF16) for g in grads],
        compiler_params=pltpu.CompilerParams(dimension_semantics=("parallel", "parallel"),
                                             vmem_limit_bytes=VMEM_LIMIT_V7X),
    )(core, *grads, *recvs)


def _final_sum(grads, recvs, arriveds, where, prevs, layer, n_layers, name):
    n_t, nb = len(grads), SUM_BLOCKS
    extra = [] if prevs is None else list(prevs)

    def body(where_ref, *refs):
        outs = refs[len(refs) - n_t:]
        for t in range(n_t):
            g_ref, r_ref = refs[t], refs[n_t + t]
            a1, a2, a3 = (refs[(2 + s) * n_t + t] for s in range(3))
            own = g_ref[...] + r_ref[...]
            outs[t][...] = ((own + a1[...].astype(F32)) + a2[...].astype(F32)) + a3[...].astype(F32)

    def blk(g):
        return (None, g.shape[1] // 2 // nb, g.shape[2])

    in_specs = ([pl.BlockSpec(blk(g), lambda i, w: (w[1], w[0] * nb + i, 0)) for g in grads]
                + [pl.BlockSpec(blk(g), lambda i, w: (w[1], i, 0)) for g in grads]
                + [pl.BlockSpec(blk(g), lambda i, w, s=s: (s, i, 0)) for s in range(3) for g in grads]
                + [pl.BlockSpec(memory_space=pl.ANY)] * len(extra))
    return pl.pallas_call(
        body, name=name,
        grid_spec=pltpu.PrefetchScalarGridSpec(
            num_scalar_prefetch=1, grid=(nb,), in_specs=in_specs,
            out_specs=[pl.BlockSpec(blk(g), lambda i, w: (layer, w[0] * nb + i, 0)) for g in grads]),
        out_shape=[jax.ShapeDtypeStruct((n_layers,) + g.shape[1:], F32) for g in grads],
        input_output_aliases={1 + 5 * n_t + t: t for t in range(n_t)} if extra else {},
        compiler_params=pltpu.CompilerParams(dimension_semantics=("parallel",), vmem_limit_bytes=VMEM_LIMIT_V7X),
    )(where, *grads, *recvs, *arriveds, *arriveds, *arriveds, *extra)


def _adamw_math(w, g, m, v):
    m = ADAM_B1 * m + (1.0 - ADAM_B1) * g
    v = ADAM_B2 * v + (1.0 - ADAM_B2) * (g * g)
    m_hat = m / (1.0 - ADAM_B1 ** ADAM_STEP)
    v_hat = v / (1.0 - ADAM_B2 ** ADAM_STEP)
    delta = -ADAM_LR * (m_hat / (jnp.sqrt(v_hat) + ADAM_EPS) + ADAM_WD * w)
    return delta, m, v


ADAMW_BLOCKS = 8


def _adamw_big(ws, gs, ms, vs, name):
    n_t, nb = len(ws), ADAMW_BLOCKS
    n_l = ws[0].shape[0]

    def body(*refs):
        for t in range(n_t):
            g = refs[n_t + t][...]
            res = _adamw_math(refs[t][...], g, refs[2 * n_t + t][...], refs[3 * n_t + t][...]) + (g,)
            for q in range(4):
                refs[(4 + q) * n_t + t][...] = res[q]

    def spec(w):
        assert w.shape[1] % (HALO * nb) == 0, w.shape
        return pl.BlockSpec((None, w.shape[1] // nb, w.shape[2]), lambda l, i: (l, i, 0))

    specs = [spec(w) for w in ws]
    outs = pl.pallas_call(
        body, name=name, grid=(n_l, nb), in_specs=specs * 4, out_specs=specs * 4,
        out_shape=[jax.ShapeDtypeStruct(w.shape, F32) for w in ws] * 4,
        compiler_params=pltpu.CompilerParams(dimension_semantics=("parallel", "parallel"),
                                             vmem_limit_bytes=VMEM_LIMIT_V7X),
    )(*ws, *gs, *ms, *vs)
    return outs[:n_t], outs[n_t:2 * n_t], outs[2 * n_t:3 * n_t], outs[3 * n_t:]


CAST_BLOCKS = 4


def _cast_bf16(ws, name, jobs=()):
    n_t, nb = len(ws), CAST_BLOCKS
    n_l = ws[0].shape[0]

    def main(ins, outs, _):
        for t in range(n_t):
            outs[t][...] = ins[t][...].astype(BF16)

    def spec(w):
        assert w.shape[1] % (2 * HALO * nb) == 0, w.shape
        return pl.BlockSpec((None, w.shape[1] // nb, w.shape[2]), lambda l, i: (l, i, 0))

    specs = [spec(w) for w in ws]
    return _hosted_call(main, name=name, grid=(n_l, nb), jobs=jobs, semantics=("parallel", "parallel"), operands=ws,
                        in_specs=specs, out_specs=specs, out_shape=[jax.ShapeDtypeStruct(w.shape, BF16) for w in ws])


def _adamw_small(ws, gs, ms, vs, name):
    n_p = len(ws)

    def body(*refs):
        ins, outs = refs[:4 * n_p], refs[4 * n_p:]
        for p in range(n_p):
            res = _adamw_math(ins[p][...], ins[n_p + p][...], ins[2 * n_p + p][...], ins[3 * n_p + p][...])
            for q in range(3):
                outs[q * n_p + p][...] = res[q]

    outs = pl.pallas_call(
        body, name=name, in_specs=[VMEM_SPEC] * (4 * n_p), out_specs=[VMEM_SPEC] * (3 * n_p),
        out_shape=[jax.ShapeDtypeStruct(w.shape, F32) for w in ws] * 3,
        compiler_params=pltpu.CompilerParams(vmem_limit_bytes=VMEM_LIMIT_V7X),
    )(*ws, *gs, *ms, *vs)
    return outs[:n_p], outs[n_p:2 * n_p], outs[2 * n_p:]


def kernel(x, mix_norm_g, w_in, conv_a_w, ln_v_g, ln_v_b, w_s, b_s, w_out, ffn_norm_g, w_up, conv_ffn_w, w_down, final_norm_g, loss_target, m_mix_norm_g, m_w_in, m_conv_a_w, m_ln_v_g, m_ln_v_b, m_w_s, m_b_s, m_w_out, m_ffn_norm_g, m_w_up, m_conv_ffn_w, m_w_down, m_final_norm_g, v_mix_norm_g, v_w_in, v_conv_a_w, v_ln_v_g, v_ln_v_b, v_w_s, v_b_s, v_w_out, v_ffn_norm_g, v_w_up, v_conv_ffn_w, v_w_down, v_final_norm_g):
    bsz, seq, d = x.shape
    n = bsz * seq
    n_l, groups = w_s.shape[0], w_s.shape[1]
    assert n_l == 2, "the exchange schedule below is written for two layers"
    gd = d // groups
    ff = w_down.shape[1] * N_CHIP
    mx, my, mc = _mesh_pos()
    chip = 2 * mx + my
    core = jnp.reshape(mc, (1,)).astype(jnp.int32)
    where = jnp.stack([mc, chip]).astype(jnp.int32)

    tm_mm = _row_tile(seq, 512)
    tm_ew = _row_tile(seq, 256)
    tk_w = _row_tile(seq, 2048)
    ff_chunk = ff // 2 if (ff // 2) % 128 == 0 else ff
    ff_strip = 256 if ff % 256 == 0 else ff_chunk
    IN, OUT, UP, DOWN = range(4)

    first = w_in[0:1].astype(BF16)
    shards, ((w_in_0,),) = _cast_bf16([w_in, w_out, w_up, w_down], "cast_weights", jobs=[_GatherJob([(first, 0)])])
    piece = lambda a, l: (shards[a], l)

    taps = jnp.concatenate([conv_a_w.reshape(n_l, -1), conv_ffn_w.reshape(n_l, -1)], axis=1)
    tap_rows = -(-taps.size // 128 // 8) * 8
    tap_pack = jnp.zeros((tap_rows * 128,), F32).at[:taps.size].set(taps.reshape(-1)).reshape(tap_rows, 128)

    bmaps = jnp.repeat(jnp.swapaxes(b_s, 1, 2), gd, axis=2)

    xs = x.reshape(n, d)
    tgt = loss_target.reshape(n, d)

    (proj0, h1_0), ((w_out_0, w_up_0), (tap_all,)) = _norm_matmul(
        xs, mix_norm_g[0][None], w_in_0, tm=tm_mm, name="fwd_in_proj_0",
        jobs=[_GatherJob([piece(OUT, 0), piece(UP, 0)]), _SmallGatherJob([tap_pack])])
    tap_all = tap_all.reshape(N_CHIP, -1)[:, :taps.size].reshape(N_CHIP, n_l, -1)
    ca = tap_all[:, :, :3 * d // N_CHIP].reshape(N_CHIP, n_l, 3, d // N_CHIP)
    cf = tap_all[:, :, 3 * d // N_CHIP:].reshape(N_CHIP, n_l, 3, 2 * ff // N_CHIP)
    conv_a_full = jnp.transpose(ca, (1, 2, 0, 3)).reshape(n_l, 3, d)
    conv_f_full = jnp.transpose(cf, (1, 2, 0, 3)).reshape(n_l, 3, 2 * ff)

    def mixer_args(l, w_out_l):
        return (conv_a_full[l], ln_v_g[l][None], ln_v_b[l][None], w_s[l], bmaps[l], w_out_l.reshape(d, d))

    (x1_0, cz0), ((w_down_0, w_out_1),) = _mixer_fwd(
        proj0, xs, *mixer_args(0, w_out_0), seq=seq, tm=tm_mm, name="fwd_mixer_0",
        jobs=[_GatherJob([piece(DOWN, 0), piece(OUT, 1)])])
    (up0, h2_0), ((w_in_1,),) = _norm_matmul(
        x1_0, ffn_norm_g[0][None], w_up_0, tm=tm_mm, name="fwd_up_proj_0", jobs=[_GatherJob([piece(IN, 1)])])
    (x2_0, conv0), _ = _ffn_fwd(up0, x1_0, conv_f_full[0], w_down_0.reshape(ff, d), seq=seq, tm=tm_mm, cwid=ff_chunk,
                                name="fwd_ffn_0")
    (proj1, h1_1), ((w_up_1, w_down_1),) = _norm_matmul(
        x2_0, mix_norm_g[1][None], w_in_1, tm=tm_mm, name="fwd_in_proj_1",
        jobs=[_GatherJob([piece(UP, 1), piece(DOWN, 1)])])
    (x1_1, cz1), _ = _mixer_fwd(proj1, x2_0, *mixer_args(1, w_out_1), seq=seq, tm=tm_mm, name="fwd_mixer_1")
    (up1, h2_1), _ = _norm_matmul(x1_1, ffn_norm_g[1][None], w_up_1, tm=tm_mm, name="fwd_up_proj_1")
    (x2_1, conv1), _ = _ffn_fwd(up1, x1_1, conv_f_full[1], w_down_1.reshape(ff, d), seq=seq, tm=tm_mm, cwid=ff_chunk,
                                name="fwd_ffn_1")
    dx, loss_tile, d_final_g = _final_loss(x2_1, final_norm_g[None], tgt, tm=tm_mm, name="final_loss")

    def chipwise(a):
        return a.reshape(N_CHIP, a.shape[0] // N_CHIP, a.shape[1])

    def pair_sums(grads, recvs, tag):
        return _pair_sum(grads, recvs, core, f"grad_pair_sum_{tag}")

    d_up, dwd, d_cf1 = _ffn_bwd(dx, up1, conv1, conv_f_full[1], w_down_1.reshape(ff, d), seq=seq, tm=tm_ew,
                                cwid=ff_strip, name="bwd_ffn_1")
    (dx1, d_g2_1), _ = _matmul_bwd_x(d_up, w_up_1, x1_1, ffn_norm_g[1][None], dx, tm=tm_mm, name="bwd_up_x_1")
    (dwu,), _ = _matmul_bwd_w(h2_1, d_up, N_CHIP, tk=tk_w, name="bwd_up_w_1")
    (d_proj, dwo, d_ca1, d_lg1, d_lb1, d_ws1, d_bs1), _ = _mixer_bwd(
        dx1, proj1, cz1, *mixer_args(1, w_out_1), seq=seq, tm=tm_ew, name="bwd_mixer_1")
    (dx, d_g1_1), _ = _matmul_bwd_x(d_proj, w_in_1, x2_0, mix_norm_g[1][None], dx1, tm=tm_mm, name="bwd_in_x_1")
    (dwi,), _ = _matmul_bwd_w(h1_1, d_proj, N_CHIP, tk=tk_w, name="bwd_in_w_1")
    grads1 = [dwi, chipwise(dwo), dwu, chipwise(dwd)]
    small1 = [d_g1_1, d_ca1, d_lg1, d_lb1, d_ws1, d_bs1.T, d_g2_1, d_cf1]

    d_up, dwd, d_cf0 = _ffn_bwd(dx, up0, conv0, conv_f_full[0], w_down_0.reshape(ff, d), seq=seq, tm=tm_ew,
                                cwid=ff_strip, name="bwd_ffn_0")
    (dx1, d_g2_0), (recv1,) = _matmul_bwd_x(d_up, w_up_0, x1_0, ffn_norm_g[0][None], dx, tm=tm_mm, name="bwd_up_x_0",
                                            jobs=[_SwapJob(grads1)])
    parts1 = pair_sums(grads1, recv1, 1)
    (dwu,), ((arr1_in,),) = _matmul_bwd_w(h2_0, d_up, N_CHIP, tk=tk_w, name="bwd_up_w_0",
                                          jobs=[_ScatterJob([parts1[IN]])])
    ud0 = [dwu, chipwise(dwd)]
    (d_proj, dwo, d_ca0, d_lg0, d_lb0, d_ws0, d_bs0), (arr1_rest, recv0_ud) = _mixer_bwd(
        dx1, proj0, cz0, *mixer_args(0, w_out_0), seq=seq, tm=tm_ew, name="bwd_mixer_0",
        jobs=[_ScatterJob(parts1[OUT:]), _SwapJob(ud0)])
    arrived1 = [arr1_in] + list(arr1_rest)
    gsum = _final_sum(grads1, recv1, arrived1, where, None, 1, n_l, "grad_final_sum_1")
    parts0_ud = pair_sums(ud0, recv0_ud, "0ud")
    out0 = chipwise(dwo)
    (dwi,), (gsum, arr0_ud, (recv0_out,)) = _matmul_bwd_w(
        h1_0, d_proj, N_CHIP, tk=tk_w, name="bwd_in_w_0",
        jobs=[_ShareJob([(g, 1) for g in gsum]), _ScatterJob(parts0_ud), _SwapJob([out0])])
    part0_out, = pair_sums([out0], [recv0_out], "0o")
    n_mm = n // tm_mm
    n_head = max(n_mm // 4, 1)
    head, ((arr0_out,), (recv0_in,)) = _matmul_bwd_x(
        d_proj, w_in_0, xs, mix_norm_g[0][None], dx1, tm=tm_mm, name="bwd_in_x_0a", tiles=(0, n_head),
        jobs=[_ScatterJob([part0_out]), _SwapJob([dwi])])
    part0_in, = pair_sums([dwi], [recv0_in], "0i")
    (dx, d_g1_0), ((arr0_in,),) = _matmul_bwd_x(
        d_proj, w_in_0, xs, mix_norm_g[0][None], dx1, tm=tm_mm, name="bwd_in_x_0b", tiles=(n_head, n_mm - n_head),
        before=head, jobs=[_ScatterJob([part0_in])])
    grad_x = dx.reshape(bsz, seq, d)
    grads0 = [dwi, out0] + ud0
    recv0 = [recv0_in, recv0_out] + list(recv0_ud)
    arrived0 = [arr0_in, arr0_out] + list(arr0_ud)
    small0 = [d_g1_0, d_ca0, d_lg0, d_lb0, d_ws0, d_bs0.T, d_g2_0, d_cf0]

    flat = [a.reshape(-1) for a in small0 + small1] + [d_final_g.reshape(-1), loss_tile[0]]
    sizes = [a.size for a in flat]
    total = sum(sizes)
    rows = -(-total // 128 // (2 * HALO)) * 2 * HALO
    pack = jnp.concatenate(flat + [jnp.zeros((rows * 128 - total,), F32)]).reshape(rows, 128)
    red, _ = _all_reduce_small(pack, "small_all_reduce")
    red = red.reshape(-1)
    gsum = _final_sum(grads0, recv0, arrived0, where, gsum, 0, n_l, "grad_final_sum_0")
    (g_in, g_out, g_up, g_down), = _comm_only([_ShareJob([(g, 0) for g in gsum])], "grad_share_0")

    pieces, off = [], 0
    for s in sizes:
        pieces.append(red[off:off + s])
        off += s
    per_layer = [pieces[l * 8:(l + 1) * 8] for l in range(n_l)]

    def stacked(idx, shape):
        return jnp.stack([per_layer[l][idx].reshape(shape) for l in range(n_l)])

    def my_cols(a, width):
        return lax.dynamic_slice_in_dim(a, chip * width, width, axis=-1)

    g_mix_norm = stacked(0, (d,))
    g_conv_a = my_cols(stacked(1, (3, d)), d // N_CHIP)
    g_ln_g = stacked(2, (d,))
    g_ln_b = stacked(3, (d,))
    g_ws = stacked(4, (groups, CHUNK, CHUNK))
    g_bs = stacked(5, (groups, CHUNK))
    g_ffn_norm = stacked(6, (d,))
    g_conv_f = my_cols(stacked(7, (3, 2 * ff)), 2 * ff // N_CHIP)
    g_final = pieces[-2].reshape(1, d)
    loss = pieces[-1][0]

    big_d, big_nm, big_nv, big_g = _adamw_big(
        [w_in, w_out, w_up, w_down], [g_in, g_out, g_up, g_down], [m_w_in, m_w_out, m_w_up, m_w_down],
        [v_w_in, v_w_out, v_w_up, v_w_down], "adamw_big")

    sm_w = [mix_norm_g, conv_a_w, ln_v_g, ln_v_b, w_s, b_s, ffn_norm_g, conv_ffn_w, final_norm_g[None]]
    sm_g = [g_mix_norm, g_conv_a, g_ln_g, g_ln_b, g_ws, g_bs, g_ffn_norm, g_conv_f, g_final]
    sm_m = [m_mix_norm_g, m_conv_a_w, m_ln_v_g, m_ln_v_b, m_w_s, m_b_s, m_ffn_norm_g, m_conv_ffn_w, m_final_norm_g[None]]
    sm_v = [v_mix_norm_g, v_conv_a_w, v_ln_v_g, v_ln_v_b, v_w_s, v_b_s, v_ffn_norm_g, v_conv_ffn_w, v_final_norm_g[None]]
    sm_d, sm_nm, sm_nv = _adamw_small(sm_w, sm_g, sm_m, sm_v, "adamw_small")

    def ordered(sm, bigs):
        return [sm[0], bigs[0], sm[1], sm[2], sm[3], sm[4], sm[5], bigs[1], sm[6], bigs[2], sm[7], bigs[3],
                sm[8].reshape(d)]

    out_g = ordered(sm_g, big_g)
    out_d = ordered(sm_d, big_d)
    out_m = ordered(sm_nm, big_nm)
    out_v = ordered(sm_nv, big_nv)
    return (loss, grad_x, *out_g, *out_d, *out_m, *out_v)
```

```python
import jax
import jax.numpy as jnp
from jax import lax
from jax.experimental import pallas as pl
from jax.experimental.pallas import tpu as pltpu

F32 = jnp.float32
BF16 = jnp.bfloat16
EPS = 1e-6
CHUNK = 128
N_CHIP = 4
HALO = 8
N_PIECES = 7
LATE_STEPS = 2
VMEM_LIMIT_V7X = 56 * 1024 * 1024
MESH_T = pl.DeviceIdType.MESH
HBM_SPEC = pl.BlockSpec(memory_space=pltpu.HBM)
VMEM_SPEC = pl.BlockSpec(memory_space=pltpu.VMEM)

ADAM_LR, ADAM_B1, ADAM_B2, ADAM_EPS, ADAM_WD, ADAM_STEP = 0.001, 0.9, 0.999, 1e-08, 0.01, 10

NT_DIMS = (((1,), (1,)), ((), ()))
TN_DIMS = (((0,), (0,)), ((), ()))


def _resident(block_shape, index_map):
    return pl.BlockSpec(block_shape, index_map, pipeline_mode=pl.Buffered(1))


def _full(shape):
    return pl.BlockSpec(shape, lambda *_: (0,) * len(shape))


def _row_tile(seq, want):
    t = min(seq, want)
    assert seq % t == 0 and t % CHUNK == 0, (seq, want)
    return t


def _shift_prev(prev8, cur, k):
    ext = jnp.concatenate([prev8, cur], axis=0)
    return pltpu.roll(ext, k, 0)[HALO:]


def _shift_next(cur, next8, k):
    ext = jnp.concatenate([cur, next8], axis=0)
    n = ext.shape[0]
    return pltpu.roll(ext, n - k, 0)[:n - HALO]


def _rowsum(a):
    return jnp.sum(a, axis=0, keepdims=True)


def _lanemean(a):
    return jnp.mean(a, axis=-1, keepdims=True)


def _tril_weights(ws_ref, groups):
    r = lax.broadcasted_iota(jnp.int32, (CHUNK, CHUNK), 0)
    c = lax.broadcasted_iota(jnp.int32, (CHUNK, CHUNK), 1)
    tril = r >= c
    return tril, [jnp.where(tril, ws_ref[g], 0.0) for g in range(groups)]


def _halo_prev(tm):
    return lambda i: jnp.maximum(i * (tm // HALO) - 1, 0)


def _halo_next(tm, n):
    last = n // HALO - 1
    return lambda i: jnp.minimum((i + 1) * (tm // HALO), last)


def _mesh_pos():
    return lax.axis_index("x"), lax.axis_index("y"), lax.axis_index("c")


def _peer_chip(x, y, r):
    return (1 - x if r >> 1 else x), (1 - y if r & 1 else y)


def _remote(src, dst, sems, k, to):
    return pltpu.make_async_remote_copy(src_ref=src, dst_ref=dst, send_sem=sems[0].at[k], recv_sem=sems[1].at[k],
                                        device_id=to, device_id_type=MESH_T)


class _GatherJob:
    def __init__(self, pieces):
        self.pieces = pieces
        self.ins = [p[0] for p in pieces]
        self.out_shapes = [jax.ShapeDtypeStruct((N_CHIP,) + p[0].shape[1:], p[0].dtype) for p in pieces]
        self.aliases = {}
        n = len(pieces)
        self.sems = [pltpu.SemaphoreType.DMA((3 * n,))] * 4 + [pltpu.SemaphoreType.DMA((n,))]

    def _half(self, outs, a, of_chip, core):
        rh = outs[a].shape[1] // 2
        return outs[a].at[of_chip, pl.ds(core * rh, rh), :]

    def _own(self, ins, outs, sems, chip):
        return [pltpu.make_async_copy(ins[a].at[layer], outs[a].at[chip], sems[4].at[a])
                for a, (_, layer) in enumerate(self.pieces)]

    def _sends(self, ins, outs, sems, x, y, c):
        chip, out = 2 * x + y, []
        for a, (_, layer) in enumerate(self.pieces):
            rh = outs[a].shape[1] // 2
            for r in (1, 2, 3):
                px, py = _peer_chip(x, y, r)
                out.append(_remote(ins[a].at[layer, pl.ds(c * rh, rh), :], self._half(outs, a, chip, c), sems[0:2],
                                   3 * a + r - 1, (px, py, c)))
        return out

    def _passes(self, outs, sems, x, y, c, core):
        out = []
        for a in range(len(self.pieces)):
            for r in (1, 2, 3):
                px, py = _peer_chip(x, y, r)
                landed = self._half(outs, a, 2 * px + py, core)
                out.append(_remote(landed, landed, sems[2:4], 3 * a + r - 1, (x, y, 1 - c)))
        return out

    def start(self, ins, outs, sems):
        x, y, c = _mesh_pos()
        for cp in self._own(ins, outs, sems, 2 * x + y) + self._sends(ins, outs, sems, x, y, c):
            cp.start()

    def late(self, ins, outs, sems):
        x, y, c = _mesh_pos()
        passes = self._passes(outs, sems, x, y, c, c)
        k = 0
        for a in range(len(self.pieces)):
            for r in (1, 2, 3):
                px, py = _peer_chip(x, y, r)
                landed = self._half(outs, a, 2 * px + py, c)
                _remote(landed, landed, sems[0:2], k, (px, py, c)).wait_recv()
                passes[k].start()
                k += 1

    def finish(self, ins, outs, sems):
        x, y, c = _mesh_pos()
        for cp in self._passes(outs, sems, x, y, c, 1 - c):
            cp.wait_recv()
        for cp in self._sends(ins, outs, sems, x, y, c) + self._passes(outs, sems, x, y, c, c):
            cp.wait_send()
        for cp in self._own(ins, outs, sems, 2 * x + y):
            cp.wait()


class _SmallGatherJob:
    def __init__(self, pieces):
        self.ins = list(pieces)
        self.out_shapes = [jax.ShapeDtypeStruct((N_CHIP,) + p.shape, p.dtype) for p in pieces]
        self.aliases = {}
        n = len(pieces)
        self.sems = [pltpu.SemaphoreType.DMA((3 * n,))] * 2 + [pltpu.SemaphoreType.DMA((n,))]

    def _outgoing(self, ins, outs, sems):
        x, y, c = _mesh_pos()
        mine = [outs[a].at[2 * x + y] for a in range(len(ins))]
        own = [pltpu.make_async_copy(ins[a], mine[a], sems[2].at[a]) for a in range(len(ins))]
        sends = [_remote(ins[a], mine[a], sems[0:2], 3 * a + r - 1, (*_peer_chip(x, y, r), c))
                 for a in range(len(ins)) for r in (1, 2, 3)]
        return own, sends

    def start(self, ins, outs, sems):
        own, sends = self._outgoing(ins, outs, sems)
        for cp in own + sends:
            cp.start()

    def finish(self, ins, outs, sems):
        x, y, c = _mesh_pos()
        for a in range(len(ins)):
            for r in (1, 2, 3):
                px, py = _peer_chip(x, y, r)
                landed = outs[a].at[2 * px + py]
                _remote(landed, landed, sems[0:2], 3 * a + r - 1, (px, py, c)).wait_recv()
        own, sends = self._outgoing(ins, outs, sems)
        for cp in sends:
            cp.wait_send()
        for cp in own:
            cp.wait()


class _SwapJob:
    def __init__(self, pieces):
        self.ins = list(pieces)
        self.out_shapes = [jax.ShapeDtypeStruct((g.shape[0], g.shape[1] // 2, g.shape[2]), g.dtype) for g in pieces]
        self.aliases = {}
        self.sems = [pltpu.SemaphoreType.DMA((len(pieces),))] * 2

    def _copies(self, ins, outs, sems):
        x, y, c = _mesh_pos()
        out = []
        for a in range(len(ins)):
            rh = ins[a].shape[1] // 2
            out.append(_remote(ins[a].at[:, pl.ds((1 - c) * rh, rh), :], outs[a], sems, a, (x, y, 1 - c)))
        return out

    def start(self, ins, outs, sems):
        for cp in self._copies(ins, outs, sems):
            cp.start()

    def finish(self, ins, outs, sems):
        for cp in self._copies(ins, outs, sems):
            cp.wait()


class _ScatterJob:
    def __init__(self, pieces):
        self.ins = list(pieces)
        self.out_shapes = [jax.ShapeDtypeStruct((3,) + p.shape[1:], p.dtype) for p in pieces]
        self.aliases = {}
        self.sems = [pltpu.SemaphoreType.DMA((3 * len(pieces),))] * 2

    def _copies(self, ins, outs, sems):
        x, y, c = _mesh_pos()
        out = []
        for a in range(len(ins)):
            for r in (1, 2, 3):
                px, py = _peer_chip(x, y, r)
                out.append(_remote(ins[a].at[2 * px + py], outs[a].at[r - 1], sems, 3 * a + r - 1, (px, py, c)))
        return out

    def start(self, ins, outs, sems):
        for cp in self._copies(ins, outs, sems):
            cp.start()

    def finish(self, ins, outs, sems):
        for cp in self._copies(ins, outs, sems):
            cp.wait()


class _ShareJob:
    def __init__(self, pieces):
        self.pieces = pieces
        self.ins = [p[0] for p in pieces]
        self.out_shapes = [jax.ShapeDtypeStruct(p[0].shape, p[0].dtype) for p in pieces]
        self.aliases = {a: a for a in range(len(pieces))}
        self.sems = [pltpu.SemaphoreType.DMA((len(pieces),))] * 2

    def _copies(self, outs, sems, core):
        x, y, c = _mesh_pos()
        out = []
        for a, (_, layer) in enumerate(self.pieces):
            rh = outs[a].shape[1] // 2
            rows = outs[a].at[layer, pl.ds(core * rh, rh), :]
            out.append(_remote(rows, rows, sems, a, (x, y, 1 - c)))
        return out

    def start(self, ins, outs, sems):
        for cp in self._copies(outs, sems, lax.axis_index("c")):
            cp.start()

    def finish(self, ins, outs, sems):
        c = lax.axis_index("c")
        for cp in self._copies(outs, sems, c):
            cp.wait_send()
        for cp in self._copies(outs, sems, 1 - c):
            cp.wait_recv()


def _hosted_call(main, *, name, grid, in_specs, out_specs, out_shape, scratch_shapes=(), jobs=(), semantics=None,
                 operands=(), main_aliases=None):
    n_in, n_out, n_sc = len(in_specs), len(out_specs), len(scratch_shapes)
    counts = [(len(j.ins), len(j.out_shapes), len(j.sems)) for j in jobs]
    j_in, j_out, j_sc = (sum(c[k] for c in counts) for k in range(3))
    aliases, i0, o0 = dict(main_aliases or {}), n_in, n_out
    for j, (ci, co, _) in zip(jobs, counts):
        aliases.update({i0 + a: o0 + b for a, b in j.aliases.items()})
        i0, o0 = i0 + ci, o0 + co

    def body(*refs):
        cuts = [0, n_in, n_in + j_in, n_in + j_in + n_out, n_in + j_in + n_out + j_out,
                n_in + j_in + n_out + j_out + n_sc, len(refs)]
        m_in, jb_in, m_out, jb_out, m_sc, jb_sc = (list(refs[cuts[k]:cuts[k + 1]]) for k in range(6))

        def run(phase):
            i0 = o0 = s0 = 0
            for j, (ci, co, cs) in zip(jobs, counts):
                if hasattr(j, phase):
                    getattr(j, phase)(jb_in[i0:i0 + ci], jb_out[o0:o0 + co], jb_sc[s0:s0 + cs])
                i0, o0, s0 = i0 + ci, o0 + co, s0 + cs

        step, n_steps = 0, 1
        for ax in range(len(grid)):
            step, n_steps = step * grid[ax] + pl.program_id(ax), n_steps * grid[ax]
        if jobs:
            pl.when(step == 0)(lambda: run("start"))
        main(m_in, m_out, m_sc)
        if jobs:
            pl.when(step == max(n_steps - 1 - LATE_STEPS, 0))(lambda: run("late"))
            pl.when(step == n_steps - 1)(lambda: run("finish"))

    if semantics is None or jobs:
        semantics = ("arbitrary",) * len(grid)
    outs = pl.pallas_call(
        body, name=name, grid=grid,
        in_specs=list(in_specs) + [HBM_SPEC] * j_in,
        out_specs=list(out_specs) + [HBM_SPEC] * j_out,
        out_shape=list(out_shape) + [s for j in jobs for s in j.out_shapes],
        scratch_shapes=list(scratch_shapes) + [s for j in jobs for s in j.sems],
        input_output_aliases=aliases,
        compiler_params=pltpu.CompilerParams(dimension_semantics=semantics, vmem_limit_bytes=VMEM_LIMIT_V7X),
    )(*operands, *[a for j in jobs for a in j.ins])
    main_outs, rest, job_outs = list(outs[:n_out]), list(outs[n_out:]), []
    for _, co, _ in counts:
        job_outs.append(rest[:co])
        rest = rest[co:]
    return main_outs, job_outs


def _comm_only(jobs, name):
    return _hosted_call(lambda i, o, s: None, name=name, grid=(1,), in_specs=[], out_specs=[], out_shape=[],
                        jobs=jobs)[1]


def _norm_matmul(x, g, wg, *, tm, name, jobs=()):
    n, d = x.shape
    n_chip, _, ck = wg.shape

    def main(ins, outs, _):
        x_ref, g_ref, w_ref = ins
        o_ref, ht_ref = outs
        xf = x_ref[...]
        r = lax.rsqrt(_lanemean(xf * xf) + EPS)
        h = xf * r * g_ref[...]
        ht_ref[...] = h.T.astype(BF16)
        hb = h.astype(BF16)
        for j in range(n_chip):
            o_ref[:, j * ck:(j + 1) * ck] = jnp.dot(hb, w_ref[j], preferred_element_type=F32).astype(BF16)

    return _hosted_call(
        main, name=name, grid=(n // tm,), jobs=jobs, semantics=("parallel",), operands=(x, g, wg),
        in_specs=[pl.BlockSpec((tm, d), lambda i: (i, 0)), _full((1, d)), _resident((n_chip, d, ck), lambda i: (0, 0, 0))],
        out_specs=[pl.BlockSpec((tm, n_chip * ck), lambda i: (i, 0)), pl.BlockSpec((d, tm), lambda i: (0, i))],
        out_shape=[jax.ShapeDtypeStruct((n, n_chip * ck), BF16), jax.ShapeDtypeStruct((d, n), BF16)])


def _mixer_fwd(proj, x, cw, lg, lb, ws, bmap, wout, *, seq, tm, name, jobs=()):
    n, d = x.shape
    groups = ws.shape[0]
    gd = d // groups
    prev = _halo_prev(tm)

    def main(ins, outs, scratch):
        proj_ref, pcg_ref, pxi_ref, x_ref, cw_ref, lg_ref, lb_ref, ws_ref, bmap_ref, wout_ref = ins
        o_ref, cz_ref = outs
        vn_s, mixed_s = scratch
        seq_start = (pl.program_id(0) * tm) % seq == 0

        def piece(k):
            return proj_ref[:, k * d:(k + 1) * d].astype(F32)

        z = piece(1) * piece(2)
        zprev = jnp.where(seq_start, 0.0, pcg_ref[...].astype(F32) * pxi_ref[...].astype(F32))
        cz = (cw_ref[0:1, :] * _shift_prev(zprev, z, 2) + cw_ref[1:2, :] * _shift_prev(zprev, z, 1)
              + cw_ref[2:3, :] * z)
        cz_ref[...] = cz.astype(BF16)
        ya = piece(0) * cz

        v = piece(4)
        xc = v - _lanemean(v)
        vn = xc * lax.rsqrt(_lanemean(xc * xc) + EPS) * lg_ref[...] + lb_ref[...]
        vn_s[...] = vn.astype(BF16)
        _, wsm = _tril_weights(ws_ref, groups)
        for ck in range(tm // CHUNK):
            rows = slice(ck * CHUNK, (ck + 1) * CHUNK)
            for g in range(groups):
                cols = slice(g * gd, (g + 1) * gd)
                mixed_s[rows, cols] = (jnp.dot(wsm[g].astype(BF16), vn_s[rows, cols], preferred_element_type=F32)
                                       + bmap_ref[:, cols])
        yb = piece(3) * mixed_s[...]
        merged = jax.nn.sigmoid(piece(5)) * ya + jax.nn.sigmoid(piece(6)) * yb
        o_ref[...] = x_ref[...] + jnp.dot(merged.astype(BF16), wout_ref[...], preferred_element_type=F32)

    return _hosted_call(
        main, name=name, grid=(n // tm,), jobs=jobs, semantics=("parallel",),
        operands=(proj, proj, proj, x, cw, lg, lb, ws, bmap, wout),
        in_specs=[pl.BlockSpec((tm, N_PIECES * d), lambda i: (i, 0)),
                  pl.BlockSpec((HALO, d), lambda i: (prev(i), 1)),
                  pl.BlockSpec((HALO, d), lambda i: (prev(i), 2)),
                  pl.BlockSpec((tm, d), lambda i: (i, 0)),
                  _full((3, d)), _full((1, d)), _full((1, d)), _full((groups, CHUNK, CHUNK)), _full((CHUNK, d)),
                  _resident((d, d), lambda i: (0, 0))],
        out_specs=[pl.BlockSpec((tm, d), lambda i: (i, 0)), pl.BlockSpec((tm, d), lambda i: (i, 0))],
        out_shape=[jax.ShapeDtypeStruct((n, d), F32), jax.ShapeDtypeStruct((n, d), BF16)],
        scratch_shapes=[pltpu.VMEM((tm, d), BF16), pltpu.VMEM((tm, d), F32)])


def _ffn_fwd(up, x, cw, wd, *, seq, tm, cwid, name, jobs=()):
    n, d = x.shape
    ff = wd.shape[0]
    prev = _halo_prev(tm)

    def main(ins, outs, _):
        up_ref, pup_ref, x_ref, cw_ref, wd_ref = ins
        o_ref, conv_ref = outs
        seq_start = (pl.program_id(0) * tm) % seq == 0

        def conv(lo):
            cols = slice(lo, lo + cwid)
            cur = up_ref[:, cols].astype(F32)
            pre = jnp.where(seq_start, 0.0, pup_ref[:, cols].astype(F32))
            out = (cw_ref[0:1, cols] * _shift_prev(pre, cur, 2) + cw_ref[1:2, cols] * _shift_prev(pre, cur, 1)
                   + cw_ref[2:3, cols] * cur)
            conv_ref[:, cols] = out.astype(BF16)
            return out

        acc = x_ref[...]
        for cj in range(ff // cwid):
            gate = conv(cj * cwid)
            val = conv(ff + cj * cwid)
            a = gate * jax.nn.sigmoid(gate) * val
            acc = acc + jnp.dot(a.astype(BF16), wd_ref[cj * cwid:(cj + 1) * cwid, :], preferred_element_type=F32)
        o_ref[...] = acc

    return _hosted_call(
        main, name=name, grid=(n // tm,), jobs=jobs, semantics=("parallel",), operands=(up, up, x, cw, wd),
        in_specs=[pl.BlockSpec((tm, 2 * ff), lambda i: (i, 0)),
                  pl.BlockSpec((HALO, 2 * ff), lambda i: (prev(i), 0)),
                  pl.BlockSpec((tm, d), lambda i: (i, 0)),
                  _full((3, 2 * ff)),
                  _resident((ff, d), lambda i: (0, 0))],
        out_specs=[pl.BlockSpec((tm, d), lambda i: (i, 0)), pl.BlockSpec((tm, 2 * ff), lambda i: (i, 0))],
        out_shape=[jax.ShapeDtypeStruct((n, d), F32), jax.ShapeDtypeStruct((n, 2 * ff), BF16)])


def _final_loss(x, g, target, *, tm, name):
    n, d = x.shape

    def body(x_ref, g_ref, t_ref, dx_ref, loss_ref, dg_ref):
        @pl.when(pl.program_id(0) == 0)
        def _():
            loss_ref[...] = jnp.zeros_like(loss_ref)
            dg_ref[...] = jnp.zeros_like(dg_ref)

        xf = x_ref[...]
        r = lax.rsqrt(_lanemean(xf * xf) + EPS)
        xhat = xf * r
        diff = xhat * g_ref[...] - t_ref[...]
        loss_ref[...] += (0.5 / d) * _rowsum(jnp.sum(diff * diff, axis=-1, keepdims=True))
        dy = diff * (1.0 / d)
        dg_ref[...] += _rowsum(dy * xhat)
        dyh = dy * g_ref[...]
        dx_ref[...] = r * (dyh - xhat * _lanemean(dyh * xhat))

    return pl.pallas_call(
        body, name=name, grid=(n // tm,),
        in_specs=[pl.BlockSpec((tm, d), lambda i: (i, 0)), _full((1, d)), pl.BlockSpec((tm, d), lambda i: (i, 0))],
        out_specs=[pl.BlockSpec((tm, d), lambda i: (i, 0)), _full((HALO, CHUNK)), _full((1, d))],
        out_shape=[jax.ShapeDtypeStruct((n, d), F32), jax.ShapeDtypeStruct((HALO, CHUNK), F32),
                   jax.ShapeDtypeStruct((1, d), F32)],
        compiler_params=pltpu.CompilerParams(dimension_semantics=("arbitrary",), vmem_limit_bytes=VMEM_LIMIT_V7X),
    )(x, g, target)


def _ffn_bwd(dx, up, conv, cw, wd, *, seq, tm, cwid, name):
    n, d = dx.shape
    ff = wd.shape[0]
    nxt = _halo_next(tm, n)

    def body(dx_ref, dxn_ref, up_ref, conv_ref, nconv_ref, cw_ref, wd_ref, dup_ref, dwd_ref, dcw_ref):
        i = pl.program_id(0)
        keep_next = jnp.where(((i + 1) * tm) % seq == 0, 0.0, 1.0)

        @pl.when(i == 0)
        def _():
            dwd_ref[...] = jnp.zeros_like(dwd_ref)
            dcw_ref[...] = jnp.zeros_like(dcw_ref)

        dxe = jnp.concatenate([dx_ref[...], dxn_ref[...]], axis=0).astype(BF16)
        dxb = dxe[:tm]
        for cj in range(ff // cwid):
            rows = slice(cj * cwid, (cj + 1) * cwid)
            g_cols, v_cols = slice(cj * cwid, (cj + 1) * cwid), slice(ff + cj * cwid, ff + (cj + 1) * cwid)
            dae = lax.dot_general(dxe, wd_ref[rows, :], NT_DIMS, preferred_element_type=F32)
            da, dan = dae[:tm], dae[tm:]

            def grads(gate, val, da_rows):
                sg = jax.nn.sigmoid(gate)
                sl = gate * sg
                return sl, da_rows * val * sg * (1.0 + gate * (1.0 - sg)), da_rows * sl

            gate, val = conv_ref[:, g_cols].astype(F32), conv_ref[:, v_cols].astype(F32)
            sl, d_gate, d_val = grads(gate, val, da)
            dwd_ref[rows, :] += lax.dot_general((sl * val).astype(BF16), dxb, TN_DIMS, preferred_element_type=F32)
            _, d_gate_n, d_val_n = grads(nconv_ref[:, g_cols].astype(F32), nconv_ref[:, v_cols].astype(F32),
                                         dan * keep_next)
            for cols, dcur, dnext in ((g_cols, d_gate, d_gate_n), (v_cols, d_val, d_val_n)):
                upc = up_ref[:, cols].astype(F32)
                d1, d2 = _shift_next(dcur, dnext, 1), _shift_next(dcur, dnext, 2)
                for k, dk in enumerate((d2, d1, dcur)):
                    dcw_ref[k:k + 1, cols] += _rowsum(dk * upc)
                dup = cw_ref[2:3, cols] * dcur + cw_ref[1:2, cols] * d1 + cw_ref[0:1, cols] * d2
                dup_ref[:, cols] = dup.astype(BF16)

    return pl.pallas_call(
        body, name=name, grid=(n // tm,),
        in_specs=[pl.BlockSpec((tm, d), lambda i: (i, 0)),
                  pl.BlockSpec((HALO, d), lambda i: (nxt(i), 0)),
                  pl.BlockSpec((tm, 2 * ff), lambda i: (i, 0)),
                  pl.BlockSpec((tm, 2 * ff), lambda i: (i, 0)),
                  pl.BlockSpec((HALO, 2 * ff), lambda i: (nxt(i), 0)),
                  _full((3, 2 * ff)),
                  _resident((ff, d), lambda i: (0, 0))],
        out_specs=[pl.BlockSpec((tm, 2 * ff), lambda i: (i, 0)), _full((ff, d)), _full((3, 2 * ff))],
        out_shape=[jax.ShapeDtypeStruct((n, 2 * ff), BF16), jax.ShapeDtypeStruct((ff, d), F32),
                   jax.ShapeDtypeStruct((3, 2 * ff), F32)],
        compiler_params=pltpu.CompilerParams(dimension_semantics=("arbitrary",), vmem_limit_bytes=VMEM_LIMIT_V7X),
    )(dx, dx, up, conv, conv, cw, wd)


def _matmul_bwd_x(dy, wg, x, g, dres, *, tm, name, jobs=(), tiles=None, before=None):
    n, d = x.shape
    n_chip, _, ck = wg.shape
    first_tile, n_tiles = tiles or (0, n // tm)
    extra = [] if before is None else list(before)

    def main(ins, outs, _):
        dy_ref, w_ref, x_ref, g_ref, dres_ref = ins[:5]
        dx_ref, dg_ref = outs

        @pl.when(pl.program_id(0) == 0)
        def _():
            dg_ref[...] = jnp.zeros_like(dg_ref) if before is None else ins[6][...]

        dh = lax.dot_general(dy_ref[:, 0:ck], w_ref[0], NT_DIMS, preferred_element_type=F32)
        for j in range(1, n_chip):
            dh = dh + lax.dot_general(dy_ref[:, j * ck:(j + 1) * ck], w_ref[j], NT_DIMS, preferred_element_type=F32)
        xf = x_ref[...]
        r = lax.rsqrt(_lanemean(xf * xf) + EPS)
        xhat = xf * r
        dg_ref[...] += _rowsum(dh * xhat)
        dyh = dh * g_ref[...]
        dx_ref[...] = dres_ref[...] + r * (dyh - xhat * _lanemean(dyh * xhat))

    rows = lambda width: pl.BlockSpec((tm, width), lambda i: (i + first_tile, 0))
    return _hosted_call(
        main, name=name, grid=(n_tiles,), jobs=jobs, semantics=("arbitrary",), operands=(dy, wg, x, g, dres, *extra),
        in_specs=[rows(n_chip * ck), _resident((n_chip, d, ck), lambda i: (0, 0, 0)), rows(d), _full((1, d)), rows(d)]
        + ([] if before is None else [pl.BlockSpec(memory_space=pl.ANY), _full((1, d))]),
        out_specs=[rows(d), _full((1, d))],
        out_shape=[jax.ShapeDtypeStruct((n, d), F32), jax.ShapeDtypeStruct((1, d), F32)],
        main_aliases=None if before is None else {5: 0})


def _matmul_bwd_w(ht, dy, n_chip, *, tk, name, jobs=()):
    d, n = ht.shape
    ck = dy.shape[1] // n_chip

    def main(ins, outs, _):
        ht_ref, dy_ref = ins
        o_ref, = outs

        @pl.when(pl.program_id(1) == 0)
        def _():
            o_ref[...] = jnp.zeros_like(o_ref)

        o_ref[...] += jnp.dot(ht_ref[...], dy_ref[...], preferred_element_type=F32)

    return _hosted_call(
        main, name=name, grid=(n_chip, n // tk), jobs=jobs, semantics=("parallel", "arbitrary"), operands=(ht, dy),
        in_specs=[pl.BlockSpec((d, tk), lambda j, k: (0, k)), pl.BlockSpec((tk, ck), lambda j, k: (k, j))],
        out_specs=[pl.BlockSpec((None, d, ck), lambda j, k: (j, 0, 0))],
        out_shape=[jax.ShapeDtypeStruct((n_chip, d, ck), F32)])


def _mixer_bwd(dx, proj, cz, cw, lg, lb, ws, bmap, wout, *, seq, tm, name, jobs=()):
    n, d = dx.shape
    groups = ws.shape[0]
    gd = d // groups
    nxt = _halo_next(tm, n)
    n_tiles = n // tm

    def main(ins, outs, scratch):
        proj_ref, cz_ref, nbg_ref, nga_ref, dx_ref, dxn_ref, cw_ref, lg_ref, lb_ref, ws_ref, bmap_ref, wout_ref = ins
        dproj_ref, dwout_ref, dcw_ref, dlg_ref, dlb_ref, dws_ref, dbs_ref = outs
        vn_s, mixed_s, dmix_s, dvn_s, dbmap_s = scratch
        i = pl.program_id(0)
        keep_next = jnp.where(((i + 1) * tm) % seq == 0, 0.0, 1.0)

        @pl.when(i == 0)
        def _():
            for ref in (dwout_ref, dcw_ref, dlg_ref, dlb_ref, dws_ref, dbmap_s):
                ref[...] = jnp.zeros_like(ref)

        def piece(k):
            return proj_ref[:, k * d:(k + 1) * d].astype(F32)

        def put(k, val):
            dproj_ref[:, k * d:(k + 1) * d] = val.astype(BF16)

        w = [cw_ref[k:k + 1, :] for k in range(3)]
        cg, xi = piece(1), piece(2)
        cz = cz_ref[...].astype(F32)
        bg = piece(0)
        ya = bg * cz

        v = piece(4)
        xc = v - _lanemean(v)
        rstd = lax.rsqrt(_lanemean(xc * xc) + EPS)
        vhat = xc * rstd
        vn_s[...] = (vhat * lg_ref[...] + lb_ref[...]).astype(BF16)
        tril, wsm = _tril_weights(ws_ref, groups)
        for ck in range(tm // CHUNK):
            rows = slice(ck * CHUNK, (ck + 1) * CHUNK)
            for g in range(groups):
                cols = slice(g * gd, (g + 1) * gd)
                mixed_s[rows, cols] = (jnp.dot(wsm[g].astype(BF16), vn_s[rows, cols], preferred_element_type=F32)
                                       + bmap_ref[:, cols])
        u = piece(3)
        mixed = mixed_s[...]
        yb = u * mixed
        sa, sb = jax.nn.sigmoid(piece(5)), jax.nn.sigmoid(piece(6))
        merged = sa * ya + sb * yb

        dxe = jnp.concatenate([dx_ref[...], dxn_ref[...]], axis=0).astype(BF16)
        dme = lax.dot_general(dxe, wout_ref[...], NT_DIMS, preferred_element_type=F32)
        dm, dm_n = dme[:tm], dme[tm:]
        dwout_ref[...] += lax.dot_general(merged.astype(BF16), dxe[:tm], TN_DIMS, preferred_element_type=F32)

        put(5, dm * ya * sa * (1.0 - sa))
        put(6, dm * yb * sb * (1.0 - sb))
        d_ya, d_yb = dm * sa, dm * sb
        put(0, d_ya * cz)
        d_cz = d_ya * bg
        d_cz_n = dm_n * jax.nn.sigmoid(nga_ref[...].astype(F32)) * nbg_ref[...].astype(F32) * keep_next
        d_cz1, d_cz2 = _shift_next(d_cz, d_cz_n, 1), _shift_next(d_cz, d_cz_n, 2)
        z = cg * xi
        for k, dk in enumerate((d_cz2, d_cz1, d_cz)):
            dcw_ref[k:k + 1, :] += _rowsum(dk * z)
        dz = w[2] * d_cz + w[1] * d_cz1 + w[0] * d_cz2
        put(1, dz * xi)
        put(2, dz * cg)

        put(3, d_yb * mixed)
        d_mixed = d_yb * u
        dmix_s[...] = d_mixed.astype(BF16)
        for ck in range(tm // CHUNK):
            rows = slice(ck * CHUNK, (ck + 1) * CHUNK)
            dbmap_s[...] += d_mixed[rows, :]
            for g in range(groups):
                cols = slice(g * gd, (g + 1) * gd)
                dvn_s[rows, cols] = jnp.dot(wsm[g].T.astype(BF16), dmix_s[rows, cols], preferred_element_type=F32)
                dws_ref[g] += jnp.where(
                    tril, lax.dot_general(dmix_s[rows, cols], vn_s[rows, cols], NT_DIMS, preferred_element_type=F32),
                    0.0)
        d_vn = dvn_s[...]
        dlg_ref[...] += _rowsum(d_vn * vhat)
        dlb_ref[...] += _rowsum(d_vn)
        d_vhat = d_vn * lg_ref[...]
        put(4, rstd * (d_vhat - _lanemean(d_vhat) - vhat * _lanemean(d_vhat * vhat)))

        @pl.when(i == n_tiles - 1)
        def _():
            for g in range(groups):
                dbs_ref[:, g:g + 1] = jnp.sum(dbmap_s[:, g * gd:(g + 1) * gd], axis=-1, keepdims=True)

    return _hosted_call(
        main, name=name, grid=(n_tiles,), jobs=jobs, semantics=("arbitrary",),
        operands=(proj, cz, proj, proj, dx, dx, cw, lg, lb, ws, bmap, wout),
        in_specs=[pl.BlockSpec((tm, N_PIECES * d), lambda i: (i, 0)),
                  pl.BlockSpec((tm, d), lambda i: (i, 0)),
                  pl.BlockSpec((HALO, d), lambda i: (nxt(i), 0)),
                  pl.BlockSpec((HALO, d), lambda i: (nxt(i), 5)),
                  pl.BlockSpec((tm, d), lambda i: (i, 0)),
                  pl.BlockSpec((HALO, d), lambda i: (nxt(i), 0)),
                  _full((3, d)), _full((1, d)), _full((1, d)), _full((groups, CHUNK, CHUNK)), _full((CHUNK, d)),
                  _resident((d, d), lambda i: (0, 0))],
        out_specs=[pl.BlockSpec((tm, N_PIECES * d), lambda i: (i, 0)), _full((d, d)),
                   _full((3, d)), _full((1, d)), _full((1, d)), _full((groups, CHUNK, CHUNK)), _full((CHUNK, groups))],
        out_shape=[jax.ShapeDtypeStruct((n, N_PIECES * d), BF16), jax.ShapeDtypeStruct((d, d), F32),
                   jax.ShapeDtypeStruct((3, d), F32), jax.ShapeDtypeStruct((1, d), F32),
                   jax.ShapeDtypeStruct((1, d), F32), jax.ShapeDtypeStruct((groups, CHUNK, CHUNK), F32),
                   jax.ShapeDtypeStruct((CHUNK, groups), F32)],
        scratch_shapes=[pltpu.VMEM((tm, d), BF16), pltpu.VMEM((tm, d), F32), pltpu.VMEM((tm, d), BF16),
                        pltpu.VMEM((tm, d), F32), pltpu.VMEM((CHUNK, d), F32)])


def _all_reduce_small(pack, name, jobs=()):
    rows = pack.shape[0]
    rh = rows // 2
    assert rh % HALO == 0, rows

    def main(ins, outs, scratch):
        p_ref, = ins
        o_ref, = outs
        sib_buf, chip_buf, got_buf, send_sem, recv_sem = scratch
        sems = (send_sem, recv_sem)
        x, y, c = _mesh_pos()
        chip, sibling = 2 * x + y, (x, y, 1 - c)
        mine = pl.ds(pl.multiple_of(c * rh, HALO), rh)
        theirs = pl.ds(pl.multiple_of((1 - c) * rh, HALO), rh)
        swap = _remote(p_ref.at[theirs], sib_buf, sems, 0, sibling)
        swap.start()
        swap.wait()
        chip_buf[chip] = p_ref[mine, :] + sib_buf[...]
        copies = []
        for r in (1, 2, 3):
            px, py = _peer_chip(x, y, r)
            copies.append(_remote(chip_buf.at[chip], chip_buf.at[chip], sems, r, (px, py, c)))
        for cp in copies:
            cp.start()
        for r, cp in zip((1, 2, 3), copies):
            px, py = _peer_chip(x, y, r)
            landed = chip_buf.at[2 * px + py]
            _remote(landed, landed, sems, r, (px, py, c)).wait_recv()
            cp.wait_send()
        total = ((chip_buf[0] + chip_buf[1]) + chip_buf[2]) + chip_buf[3]
        o_ref[mine, :] = total
        chip_buf[chip] = total
        share = _remote(chip_buf.at[chip], got_buf, sems, 4, sibling)
        share.start()
        share.wait()
        o_ref[theirs, :] = got_buf[...]

    outs, job_outs = _hosted_call(
        main, name=name, grid=(1,), jobs=jobs, operands=(pack,),
        in_specs=[_full(pack.shape)], out_specs=[_full(pack.shape)],
        out_shape=[jax.ShapeDtypeStruct(pack.shape, pack.dtype)],
        scratch_shapes=[pltpu.VMEM((rh, 128), F32), pltpu.VMEM((N_CHIP, rh, 128), F32), pltpu.VMEM((rh, 128), F32),
                        pltpu.SemaphoreType.DMA((5,)), pltpu.SemaphoreType.DMA((5,))])
    return outs[0], job_outs


SUM_BLOCKS = 2


def _pair_sum(grads, recvs, core, name):
    n_t, nb = len(grads), SUM_BLOCKS
    n_chip = grads[0].shape[0]

    def body(core_ref, *refs):
        for t in range(n_t):
            refs[2 * n_t + t][...] = (refs[t][...] + refs[n_t + t][...]).astype(BF16)

    def blk(g):
        rh = g.shape[1] // 2
        assert rh % (16 * nb) == 0, g.shape
        return (None, rh // nb, g.shape[2])

    mine = [pl.BlockSpec(blk(g), lambda k, i, core_ref: (k, core_ref[0] * nb + i, 0)) for g in grads]
    plain = [pl.BlockSpec(blk(g), lambda k, i, core_ref: (k, i, 0)) for g in grads]
    return pl.pallas_call(
        body, name=name,
        grid_spec=pltpu.PrefetchScalarGridSpec(num_scalar_prefetch=1, grid=(n_chip, nb), in_specs=mine + plain,
                                               out_specs=plain),
        out_shape=[jax.ShapeDtypeStruct((n_chip, g.shape[1] // 2, g.shape[2]), BF16) for g in grads],
        compiler_params=pltpu.CompilerParams(dimension_semantics=("parallel", "parallel"),
                                             vmem_limit_bytes=VMEM_LIMIT_V7X),
    )(core, *grads, *recvs)


def _final_sum(grads, recvs, arriveds, where, prevs, layer, n_layers, name):
    n_t, nb = len(grads), SUM_BLOCKS
    extra = [] if prevs is None else list(prevs)

    def body(where_ref, *refs):
        outs = refs[len(refs) - n_t:]
        for t in range(n_t):
            g_ref, r_ref = refs[t], refs[n_t + t]
            a1, a2, a3 = (refs[(2 + s) * n_t + t] for s in range(3))
            own = g_ref[...] + r_ref[...]
            outs[t][...] = ((own + a1[...].astype(F32)) + a2[...].astype(F32)) + a3[...].astype(F32)

    def blk(g):
        return (None, g.shape[1] // 2 // nb, g.shape[2])

    in_specs = ([pl.BlockSpec(blk(g), lambda i, w: (w[1], w[0] * nb + i, 0)) for g in grads]
                + [pl.BlockSpec(blk(g), lambda i, w: (w[1], i, 0)) for g in grads]
                + [pl.BlockSpec(blk(g), lambda i, w, s=s: (s, i, 0)) for s in range(3) for g in grads]
                + [pl.BlockSpec(memory_space=pl.ANY)] * len(extra))
    return pl.pallas_call(
        body, name=name,
        grid_spec=pltpu.PrefetchScalarGridSpec(
            num_scalar_prefetch=1, grid=(nb,), in_specs=in_specs,
            out_specs=[pl.BlockSpec(blk(g), lambda i, w: (layer, w[0] * nb + i, 0)) for g in grads]),
        out_shape=[jax.ShapeDtypeStruct((n_layers,) + g.shape[1:], F32) for g in grads],
        input_output_aliases={1 + 5 * n_t + t: t for t in range(n_t)} if extra else {},
        compiler_params=pltpu.CompilerParams(dimension_semantics=("parallel",), vmem_limit_bytes=VMEM_LIMIT_V7X),
    )(where, *grads, *recvs, *arriveds, *arriveds, *arriveds, *extra)


def _adamw_math(w, g, m, v):
    m = ADAM_B1 * m + (1.0 - ADAM_B1) * g
    v = ADAM_B2 * v + (1.0 - ADAM_B2) * (g * g)
    m_hat = m / (1.0 - ADAM_B1 ** ADAM_STEP)
    v_hat = v / (1.0 - ADAM_B2 ** ADAM_STEP)
    delta = -ADAM_LR * (m_hat / (jnp.sqrt(v_hat) + ADAM_EPS) + ADAM_WD * w)
    return delta, m, v


ADAMW_BLOCKS = 8


def _adamw_big(ws, gs, ms, vs, name):
    n_t, nb = len(ws), ADAMW_BLOCKS
    n_l = ws[0].shape[0]

    def body(*refs):
        for t in range(n_t):
            g = refs[n_t + t][...]
            res = _adamw_math(refs[t][...], g, refs[2 * n_t + t][...], refs[3 * n_t + t][...]) + (g,)
            for q in range(4):
                refs[(4 + q) * n_t + t][...] = res[q]

    def spec(w):
        assert w.shape[1] % (HALO * nb) == 0, w.shape
        return pl.BlockSpec((None, w.shape[1] // nb, w.shape[2]), lambda l, i: (l, i, 0))

    specs = [spec(w) for w in ws]
    outs = pl.pallas_call(
        body, name=name, grid=(n_l, nb), in_specs=specs * 4, out_specs=specs * 4,
        out_shape=[jax.ShapeDtypeStruct(w.shape, F32) for w in ws] * 4,
        compiler_params=pltpu.CompilerParams(dimension_semantics=("parallel", "parallel"),
                                             vmem_limit_bytes=VMEM_LIMIT_V7X),
    )(*ws, *gs, *ms, *vs)
    return outs[:n_t], outs[n_t:2 * n_t], outs[2 * n_t:3 * n_t], outs[3 * n_t:]


CAST_BLOCKS = 4


def _cast_bf16(ws, name, jobs=()):
    n_t, nb = len(ws), CAST_BLOCKS
    n_l = ws[0].shape[0]

    def main(ins, outs, _):
        for t in range(n_t):
            outs[t][...] = ins[t][...].astype(BF16)

    def spec(w):
        assert w.shape[1] % (2 * HALO * nb) == 0, w.shape
        return pl.BlockSpec((None, w.shape[1] // nb, w.shape[2]), lambda l, i: (l, i, 0))

    specs = [spec(w) for w in ws]
    return _hosted_call(main, name=name, grid=(n_l, nb), jobs=jobs, semantics=("parallel", "parallel"), operands=ws,
                        in_specs=specs, out_specs=specs, out_shape=[jax.ShapeDtypeStruct(w.shape, BF16) for w in ws])


def _adamw_small(ws, gs, ms, vs, name):
    n_p = len(ws)

    def body(*refs):
        ins, outs = refs[:4 * n_p], refs[4 * n_p:]
        for p in range(n_p):
            res = _adamw_math(ins[p][...], ins[n_p + p][...], ins[2 * n_p + p][...], ins[3 * n_p + p][...])
            for q in range(3):
                outs[q * n_p + p][...] = res[q]

    outs = pl.pallas_call(
        body, name=name, in_specs=[VMEM_SPEC] * (4 * n_p), out_specs=[VMEM_SPEC] * (3 * n_p),
        out_shape=[jax.ShapeDtypeStruct(w.shape, F32) for w in ws] * 3,
        compiler_params=pltpu.CompilerParams(vmem_limit_bytes=VMEM_LIMIT_V7X),
    )(*ws, *gs, *ms, *vs)
    return outs[:n_p], outs[n_p:2 * n_p], outs[2 * n_p:]


def kernel(x, mix_norm_g, w_in, conv_a_w, ln_v_g, ln_v_b, w_s, b_s, w_out, ffn_norm_g, w_up, conv_ffn_w, w_down, final_norm_g, loss_target, m_mix_norm_g, m_w_in, m_conv_a_w, m_ln_v_g, m_ln_v_b, m_w_s, m_b_s, m_w_out, m_ffn_norm_g, m_w_up, m_conv_ffn_w, m_w_down, m_final_norm_g, v_mix_norm_g, v_w_in, v_conv_a_w, v_ln_v_g, v_ln_v_b, v_w_s, v_b_s, v_w_out, v_ffn_norm_g, v_w_up, v_conv_ffn_w, v_w_down, v_final_norm_g):
    bsz, seq, d = x.shape
    n = bsz * seq
    n_l, groups = w_s.shape[0], w_s.shape[1]
    assert n_l == 2, "the exchange schedule below is written for two layers"
    gd = d // groups
    ff = w_down.shape[1] * N_CHIP
    mx, my, mc = _mesh_pos()
    chip = 2 * mx + my
    core = jnp.reshape(mc, (1,)).astype(jnp.int32)
    where = jnp.stack([mc, chip]).astype(jnp.int32)

    tm_mm = _row_tile(seq, 512)
    tm_ew = _row_tile(seq, 256)
    tk_w = _row_tile(seq, 2048)
    ff_chunk = ff // 2 if (ff // 2) % 128 == 0 else ff
    ff_strip = 256 if ff % 256 == 0 else ff_chunk
    IN, OUT, UP, DOWN = range(4)

    first = w_in[0:1].astype(BF16)
    shards, ((w_in_0,),) = _cast_bf16([w_in, w_out, w_up, w_down], "cast_weights", jobs=[_GatherJob([(first, 0)])])
    piece = lambda a, l: (shards[a], l)

    taps = jnp.concatenate([conv_a_w.reshape(n_l, -1), conv_ffn_w.reshape(n_l, -1)], axis=1)
    tap_rows = -(-taps.size // 128 // 8) * 8
    tap_pack = jnp.zeros((tap_rows * 128,), F32).at[:taps.size].set(taps.reshape(-1)).reshape(tap_rows, 128)

    bmaps = jnp.repeat(jnp.swapaxes(b_s, 1, 2), gd, axis=2)

    xs = x.reshape(n, d)
    tgt = loss_target.reshape(n, d)

    (proj0, h1_0), ((w_out_0, w_up_0), (tap_all,)) = _norm_matmul(
        xs, mix_norm_g[0][None], w_in_0, tm=tm_mm, name="fwd_in_proj_0",
        jobs=[_GatherJob([piece(OUT, 0), piece(UP, 0)]), _SmallGatherJob([tap_pack])])
    tap_all = tap_all.reshape(N_CHIP, -1)[:, :taps.size].reshape(N_CHIP, n_l, -1)
    ca = tap_all[:, :, :3 * d // N_CHIP].reshape(N_CHIP, n_l, 3, d // N_CHIP)
    cf = tap_all[:, :, 3 * d // N_CHIP:].reshape(N_CHIP, n_l, 3, 2 * ff // N_CHIP)
    conv_a_full = jnp.transpose(ca, (1, 2, 0, 3)).reshape(n_l, 3, d)
    conv_f_full = jnp.transpose(cf, (1, 2, 0, 3)).reshape(n_l, 3, 2 * ff)

    def mixer_args(l, w_out_l):
        return (conv_a_full[l], ln_v_g[l][None], ln_v_b[l][None], w_s[l], bmaps[l], w_out_l.reshape(d, d))

    (x1_0, cz0), ((w_down_0, w_out_1),) = _mixer_fwd(
        proj0, xs, *mixer_args(0, w_out_0), seq=seq, tm=tm_mm, name="fwd_mixer_0",
        jobs=[_GatherJob([piece(DOWN, 0), piece(OUT, 1)])])
    (up0, h2_0), ((w_in_1,),) = _norm_matmul(
        x1_0, ffn_norm_g[0][None], w_up_0, tm=tm_mm, name="fwd_up_proj_0", jobs=[_GatherJob([piece(IN, 1)])])
    (x2_0, conv0), _ = _ffn_fwd(up0, x1_0, conv_f_full[0], w_down_0.reshape(ff, d), seq=seq, tm=tm_mm, cwid=ff_strip,
                                name="fwd_ffn_0")
    (proj1, h1_1), ((w_up_1, w_down_1),) = _norm_matmul(
        x2_0, mix_norm_g[1][None], w_in_1, tm=tm_mm, name="fwd_in_proj_1",
        jobs=[_GatherJob([piece(UP, 1), piece(DOWN, 1)])])
    (x1_1, cz1), _ = _mixer_fwd(proj1, x2_0, *mixer_args(1, w_out_1), seq=seq, tm=tm_mm, name="fwd_mixer_1")
    (up1, h2_1), _ = _norm_matmul(x1_1, ffn_norm_g[1][None], w_up_1, tm=tm_mm, name="fwd_up_proj_1")
    (x2_1, conv1), _ = _ffn_fwd(up1, x1_1, conv_f_full[1], w_down_1.reshape(ff, d), seq=seq, tm=tm_mm, cwid=ff_strip,
                                name="fwd_ffn_1")
    dx, loss_tile, d_final_g = _final_loss(x2_1, final_norm_g[None], tgt, tm=tm_mm, name="final_loss")

    def chipwise(a):
        return a.reshape(N_CHIP, a.shape[0] // N_CHIP, a.shape[1])

    def pair_sums(grads, recvs, tag):
        return _pair_sum(grads, recvs, core, f"grad_pair_sum_{tag}")

    d_up, dwd, d_cf1 = _ffn_bwd(dx, up1, conv1, conv_f_full[1], w_down_1.reshape(ff, d), seq=seq, tm=tm_ew,
                                cwid=ff_strip, name="bwd_ffn_1")
    (dx1, d_g2_1), _ = _matmul_bwd_x(d_up, w_up_1, x1_1, ffn_norm_g[1][None], dx, tm=tm_mm, name="bwd_up_x_1")
    (dwu,), _ = _matmul_bwd_w(h2_1, d_up, N_CHIP, tk=tk_w, name="bwd_up_w_1")
    (d_proj, dwo, d_ca1, d_lg1, d_lb1, d_ws1, d_bs1), _ = _mixer_bwd(
        dx1, proj1, cz1, *mixer_args(1, w_out_1), seq=seq, tm=tm_ew, name="bwd_mixer_1")
    (dx, d_g1_1), _ = _matmul_bwd_x(d_proj, w_in_1, x2_0, mix_norm_g[1][None], dx1, tm=tm_mm, name="bwd_in_x_1")
    (dwi,), _ = _matmul_bwd_w(h1_1, d_proj, N_CHIP, tk=tk_w, name="bwd_in_w_1")
    grads1 = [dwi, chipwise(dwo), dwu, chipwise(dwd)]
    small1 = [d_g1_1, d_ca1, d_lg1, d_lb1, d_ws1, d_bs1.T, d_g2_1, d_cf1]

    d_up, dwd, d_cf0 = _ffn_bwd(dx, up0, conv0, conv_f_full[0], w_down_0.reshape(ff, d), seq=seq, tm=tm_ew,
                                cwid=ff_strip, name="bwd_ffn_0")
    (dx1, d_g2_0), (recv1,) = _matmul_bwd_x(d_up, w_up_0, x1_0, ffn_norm_g[0][None], dx, tm=tm_mm, name="bwd_up_x_0",
                                            jobs=[_SwapJob(grads1)])
    parts1 = pair_sums(grads1, recv1, 1)
    (dwu,), ((arr1_in,),) = _matmul_bwd_w(h2_0, d_up, N_CHIP, tk=tk_w, name="bwd_up_w_0",
                                          jobs=[_ScatterJob([parts1[IN]])])
    ud0 = [dwu, chipwise(dwd)]
    (d_proj, dwo, d_ca0, d_lg0, d_lb0, d_ws0, d_bs0), (arr1_rest, recv0_ud) = _mixer_bwd(
        dx1, proj0, cz0, *mixer_args(0, w_out_0), seq=seq, tm=tm_ew, name="bwd_mixer_0",
        jobs=[_ScatterJob(parts1[OUT:]), _SwapJob(ud0)])
    arrived1 = [arr1_in] + list(arr1_rest)
    gsum = _final_sum(grads1, recv1, arrived1, where, None, 1, n_l, "grad_final_sum_1")
    parts0_ud = pair_sums(ud0, recv0_ud, "0ud")
    out0 = chipwise(dwo)
    (dwi,), (gsum, arr0_ud, (recv0_out,)) = _matmul_bwd_w(
        h1_0, d_proj, N_CHIP, tk=tk_w, name="bwd_in_w_0",
        jobs=[_ShareJob([(g, 1) for g in gsum]), _ScatterJob(parts0_ud), _SwapJob([out0])])
    part0_out, = pair_sums([out0], [recv0_out], "0o")
    n_mm = n // tm_mm
    n_head = max(n_mm // 4, 1)
    head, ((arr0_out,), (recv0_in,)) = _matmul_bwd_x(
        d_proj, w_in_0, xs, mix_norm_g[0][None], dx1, tm=tm_mm, name="bwd_in_x_0a", tiles=(0, n_head),
        jobs=[_ScatterJob([part0_out]), _SwapJob([dwi])])
    part0_in, = pair_sums([dwi], [recv0_in], "0i")
    (dx, d_g1_0), ((arr0_in,),) = _matmul_bwd_x(
        d_proj, w_in_0, xs, mix_norm_g[0][None], dx1, tm=tm_mm, name="bwd_in_x_0b", tiles=(n_head, n_mm - n_head),
        before=head, jobs=[_ScatterJob([part0_in])])
    grad_x = dx.reshape(bsz, seq, d)
    grads0 = [dwi, out0] + ud0
    recv0 = [recv0_in, recv0_out] + list(recv0_ud)
    arrived0 = [arr0_in, arr0_out] + list(arr0_ud)
    small0 = [d_g1_0, d_ca0, d_lg0, d_lb0, d_ws0, d_bs0.T, d_g2_0, d_cf0]

    flat = [a.reshape(-1) for a in small0 + small1] + [d_final_g.reshape(-1), loss_tile[0]]
    sizes = [a.size for a in flat]
    total = sum(sizes)
    rows = -(-total // 128 // (2 * HALO)) * 2 * HALO
    pack = jnp.concatenate(flat + [jnp.zeros((rows * 128 - total,), F32)]).reshape(rows, 128)
    red, _ = _all_reduce_small(pack, "small_all_reduce")
    red = red.reshape(-1)
    gsum = _final_sum(grads0, recv0, arrived0, where, gsum, 0, n_l, "grad_final_sum_0")
    (g_in, g_out, g_up, g_down), = _comm_only([_ShareJob([(g, 0) for g in gsum])], "grad_share_0")

    pieces, off = [], 0
    for s in sizes:
        pieces.append(red[off:off + s])
        off += s
    per_layer = [pieces[l * 8:(l + 1) * 8] for l in range(n_l)]

    def stacked(idx, shape):
        return jnp.stack([per_layer[l][idx].reshape(shape) for l in range(n_l)])

    def my_cols(a, width):
        return lax.dynamic_slice_in_dim(a, chip * width, width, axis=-1)

    g_mix_norm = stacked(0, (d,))
    g_conv_a = my_cols(stacked(1, (3, d)), d // N_CHIP)
    g_ln_g = stacked(2, (d,))
    g_ln_b = stacked(3, (d,))
    g_ws = stacked(4, (groups, CHUNK, CHUNK))
    g_bs = stacked(5, (groups, CHUNK))
    g_ffn_norm = stacked(6, (d,))
    g_conv_f = my_cols(stacked(7, (3, 2 * ff)), 2 * ff // N_CHIP)
    g_final = pieces[-2].reshape(1, d)
    loss = pieces[-1][0]

    big_d, big_nm, big_nv, big_g = _adamw_big(
        [w_in, w_out, w_up, w_down], [g_in, g_out, g_up, g_down], [m_w_in, m_w_out, m_w_up, m_w_down],
        [v_w_in, v_w_out, v_w_up, v_w_down], "adamw_big")

    sm_w = [mix_norm_g, conv_a_w, ln_v_g, ln_v_b, w_s, b_s, ffn_norm_g, conv_ffn_w, final_norm_g[None]]
    sm_g = [g_mix_norm, g_conv_a, g_ln_g, g_ln_b, g_ws, g_bs, g_ffn_norm, g_conv_f, g_final]
    sm_m = [m_mix_norm_g, m_conv_a_w, m_ln_v_g, m_ln_v_b, m_w_s, m_b_s, m_ffn_norm_g, m_conv_ffn_w, m_final_norm_g[None]]
    sm_v = [v_mix_norm_g, v_conv_a_w, v_ln_v_g, v_ln_v_b, v_w_s, v_b_s, v_ffn_norm_g, v_conv_ffn_w, v_final_norm_g[None]]
    sm_d, sm_nm, sm_nv = _adamw_small(sm_w, sm_g, sm_m, sm_v, "adamw_small")

    def ordered(sm, bigs):
        return [sm[0], bigs[0], sm[1], sm[2], sm[3], sm[4], sm[5], bigs[1], sm[6], bigs[2], sm[7], bigs[3],
                sm[8].reshape(d)]

    out_g = ordered(sm_g, big_g)
    out_d = ordered(sm_d, big_d)
    out_m = ordered(sm_nm, big_nm)
    out_v = ordered(sm_nv, big_nv)
    return (loss, grad_x, *out_g, *out_d, *out_m, *out_v)
```
